```python
import math
import jax
import jax.numpy as jnp
from jax import lax
import numpy as np

D_MODEL = 1024
BATCH = 16
SEQ = 2048
DEPTH = 2

N_MIXERS = 4
GROUP_W = D_MODEL // N_MIXERS
HEAD_V = 64
N_HEADS = GROUP_W // HEAD_V
D_MIX = N_MIXERS * GROUP_W
GDN_CONV = 4
RWKV_DECAY_RANK = 64
RWKV_A_RANK = 64
RWKV_GN_EPS = 64e-5
SC_CONV = 3
GLA_HEAD_K = HEAD_V // 2
GLA_K = N_HEADS * GLA_HEAD_K
GLA_RANK = 16
GLA_TAU = 16.0
CHUNK = 64
EPS = 1e-6

GDN_COLS = 4 * GROUP_W + 2 * N_HEADS
RWKV_COLS = 4 * GROUP_W + RWKV_DECAY_RANK + RWKV_A_RANK
SC_COLS = 4 * GROUP_W
GLA_COLS = 2 * GLA_K + 2 * GROUP_W + GLA_RANK
D_IN = GDN_COLS + RWKV_COLS + SC_COLS + GLA_COLS

kernel_name = 'hybrid_parallel_heads_gdn_rwkv7_shortconv_gla'


def _split(x, sizes):
    idx = [int(i) for i in np.cumsum(sizes)[:-1]]
    return jnp.split(x, idx, axis=-1)


def _rmsnorm(x, w):
    xf = x.astype(jnp.float32)
    y = xf * lax.rsqrt(jnp.mean(xf * xf, axis=-1, keepdims=True) + EPS)
    return (y * w).astype(x.dtype)


def _l2norm(x):
    return x * lax.rsqrt(jnp.sum(x * x, axis=-1, keepdims=True) + EPS)


def _causal_dwconv(x, w):
    kw = w.shape[0]
    t = x.shape[1]
    xp = jnp.pad(x, ((0, 0), (kw - 1, 0), (0, 0)))
    return sum(xp[:, i:i + t] * w[i] for i in range(kw))


def _token_shift(x):
    return jnp.pad(x, ((0, 0), (1, 0), (0, 0)))[:, :-1]


def _heads(x, d):
    b, t, c = x.shape
    return x.reshape(b, t, c // d, d).transpose(0, 2, 1, 3)


def _merge(x):
    b, h, t, d = x.shape
    return x.transpose(0, 2, 1, 3).reshape(b, t, h * d)


def _chunks(x):
    b, h, t = x.shape[:3]
    return x.reshape((b, h, t // CHUNK, CHUNK) + x.shape[3:])


def _gated_deltanet(p, conv_w, a_log, dt_bias, norm_w):
    dtype = p.dtype
    p = p.astype(jnp.float32)
    qkv, z, a_raw, b_raw = _split(p, [3 * GROUP_W, GROUP_W, N_HEADS, N_HEADS])
    qkv = jax.nn.silu(_causal_dwconv(qkv, conv_w))
    q, k, v = jnp.split(qkv, 3, axis=-1)
    q = _chunks(_l2norm(_heads(q, HEAD_V)) * HEAD_V ** -0.5)
    k = _chunks(_l2norm(_heads(k, HEAD_V)))
    v = _chunks(_heads(v, HEAD_V))
    g = -jnp.exp(a_log) * jax.nn.softplus(a_raw + dt_bias)
    g = jnp.cumsum(_chunks(g.transpose(0, 2, 1)), axis=-1)
    beta = _chunks(jax.nn.sigmoid(b_raw).transpose(0, 2, 1))
    incl = jnp.tril(jnp.ones((CHUNK, CHUNK), bool))
    strict = jnp.tril(jnp.ones((CHUNK, CHUNK), bool), -1)
    diff = g[..., :, None] - g[..., None, :]
    decay = jnp.where(incl, jnp.exp(jnp.where(incl, diff, 0.0)), 0.0)
    kb = k * beta[..., None]
    a_mat = jnp.where(strict, jnp.einsum('bhnid,bhnjd->bhnij', kb, k) * decay, 0.0)
    t_mat = a_mat + jnp.eye(CHUNK, dtype=a_mat.dtype)
    u = lax.linalg.triangular_solve(t_mat, v * beta[..., None], left_side=True, lower=True, unit_diagonal=True)
    w = lax.linalg.triangular_solve(t_mat, kb * jnp.exp(g)[..., None], left_side=True, lower=True, unit_diagonal=True)
    attn = jnp.einsum('bhnid,bhnjd->bhnij', q, k) * decay
    g_last = g[..., -1]
    k_end = k * jnp.exp(g_last[..., None] - g)[..., None]
    q_g = q * jnp.exp(g)[..., None]

    def step(s, inp):
        q_c, k_c, u_c, w_c, attn_c, gl_c = inp
        v_new = u_c - jnp.einsum('bhck,bhkv->bhcv', w_c, s)
        o = jnp.einsum('bhck,bhkv->bhcv', q_c, s) + jnp.einsum('bhij,bhjv->bhiv', attn_c, v_new)
        s = s * jnp.exp(gl_c)[..., None, None] + jnp.einsum('bhck,bhcv->bhkv', k_c, v_new)
        return s, o

    xs = tuple(jnp.moveaxis(t, 2, 0) for t in (q_g, k_end, u, w, attn, g_last))
    s0 = jnp.zeros(q.shape[:2] + (HEAD_V, HEAD_V), jnp.float32)
    _, o = lax.scan(step, s0, xs)
    o = jnp.moveaxis(o, 0, 2)
    o = o.reshape(o.shape[0], o.shape[1], -1, HEAD_V)
    out = _merge(_rmsnorm(o, norm_w)) * jax.nn.silu(z)
    return out.astype(dtype)


def _rwkv7(p, mu, w0, w_up, a0, a_up, k_k, k_a, r_k, ln_w, ln_b):
    dtype = p.dtype
    p = p.astype(jnp.float32)
    p = p + mu * (_token_shift(p) - p)
    r, k, v, z, w_down, a_down = _split(p, [GROUP_W] * 4 + [RWKV_DECAY_RANK, RWKV_A_RANK])
    decay = jnp.exp(-math.exp(-0.5) * jax.nn.sigmoid(w0 + jnp.tanh(w_down) @ w_up))
    a = jax.nn.sigmoid(a0 + a_down @ a_up)
    b_, t_ = p.shape[:2]

    def bthd(x):
        return x.reshape(b_, t_, N_HEADS, HEAD_V)

    kk = _l2norm(bthd(k * k_k))
    k = k * (1.0 + (a - 1.0) * k_a)

    def step(s, inp):
        r_t, w_t, k_t, kk_t, a_t, v_t = inp
        sa = jnp.einsum('bhvk,bhk->bhv', s, -kk_t)
        s = (s * w_t[:, :, None, :] + sa[..., None] * (kk_t * a_t)[:, :, None, :]
             + v_t[..., None] * k_t[:, :, None, :])
        return s, jnp.einsum('bhvk,bhk->bhv', s, r_t)

    xs = tuple(jnp.moveaxis(t, 1, 0) for t in (bthd(r), bthd(decay), bthd(k), kk, bthd(a), bthd(v)))
    s0 = jnp.zeros((b_, N_HEADS, HEAD_V, HEAD_V), jnp.float32)
    _, y = lax.scan(step, s0, xs)
    y = jnp.moveaxis(y, 0, 1)
    mean = jnp.mean(y, axis=-1, keepdims=True)
    var = jnp.mean(jnp.square(y - mean), axis=-1, keepdims=True)
    yn = ((y - mean) * lax.rsqrt(var + RWKV_GN_EPS)).reshape(b_, t_, GROUP_W) * ln_w + ln_b
    bonus = (jnp.sum(bthd(r * k * r_k), axis=-1, keepdims=True) * bthd(v)).reshape(b_, t_, GROUP_W)
    out = (yn + bonus) * jax.nn.silu(z)
    return out.astype(dtype)


def _short_conv(p, conv_w):
    bg, cg, xv, z = _split(p, [GROUP_W] * 4)
    return bg * _causal_dwconv(cg * xv, conv_w) * jax.nn.silu(z)


def _gla(p, a_up, a_bias, norm_w):
    dtype = p.dtype
    p = p.astype(jnp.float32)
    q, k, v, z, a_down = _split(p, [GLA_K, GLA_K, GROUP_W, GROUP_W, GLA_RANK])
    log_a = jax.nn.log_sigmoid(a_down @ a_up + a_bias) / GLA_TAU
    q = _chunks(_heads(q, GLA_HEAD_K)) * GLA_HEAD_K ** -0.5
    k = _chunks(_heads(k, GLA_HEAD_K))
    v = _chunks(_heads(v, HEAD_V))
    bcum = jnp.cumsum(_chunks(_heads(log_a, GLA_HEAD_K)), axis=-2)
    b_last = bcum[..., -1:, :]
    q_e = q * jnp.exp(bcum)
    k_e = k * jnp.exp(-bcum)
    k_end = k * jnp.exp(b_last - bcum)
    incl = jnp.tril(jnp.ones((CHUNK, CHUNK), bool))
    attn = jnp.where(incl, jnp.einsum('bhnik,bhnjk->bhnij', q_e, k_e), 0.0)
    intra = jnp.einsum('bhnij,bhnjv->bhniv', attn, v)

    def step(s, inp):
        qe_c, ke_c, v_c, bl_c, intra_c = inp
        o = jnp.einsum('bhck,bhkv->bhcv', qe_c, s) + intra_c
        s = s * jnp.exp(bl_c)[..., 0, :, None] + jnp.einsum('bhck,bhcv->bhkv', ke_c, v_c)
        return s, o

    xs = tuple(jnp.moveaxis(t, 2, 0) for t in (q_e, k_end, v, b_last, intra))
    s0 = jnp.zeros(q.shape[:2] + (GLA_HEAD_K, HEAD_V), jnp.float32)
    _, o = lax.scan(step, s0, xs)
    o = jnp.moveaxis(o, 0, 2)
    o = o.reshape(o.shape[0], o.shape[1], -1, HEAD_V)
    out = _merge(_rmsnorm(o, norm_w)) * jax.nn.silu(z)
    return out.astype(dtype)


def _fwd_setup_inputs(seed: int = 0) -> dict:
    key = jax.random.key(seed)
    ks = jax.random.split(key, 24)
    f32 = jnp.float32
    L = DEPTH

    def nrm(k, shape, s):
        return s * jax.random.normal(k, shape, f32)

    dt = jnp.exp(jax.random.uniform(ks[5], (L, N_HEADS), f32, math.log(1e-3), math.log(1e-1)))
    return {
        'x': nrm(ks[0], (BATCH, SEQ, D_MODEL), 1.0),
        'pre_norm_w': 1.0 + nrm(ks[1], (L, D_MODEL), 0.02),
        'w_in': nrm(ks[2], (L, D_MODEL, D_IN), D_MODEL ** -0.5),
        'gdn_conv_w': nrm(ks[3], (L, GDN_CONV, 3 * GROUP_W), GDN_CONV ** -0.5),
        'gdn_a_log': jnp.log(jax.random.uniform(ks[4], (L, N_HEADS), f32, 1.0, 16.0)),
        'gdn_dt_bias': dt + jnp.log(-jnp.expm1(-dt)),
        'gdn_norm_w': 1.0 + nrm(ks[6], (L, HEAD_V), 0.02),
        'rwkv_mu': jax.random.uniform(ks[7], (L, RWKV_COLS), f32),
        'rwkv_w0': jax.random.uniform(ks[8], (L, GROUP_W), f32, -2.0, 2.0),
        'rwkv_w_up': nrm(ks[9], (L, RWKV_DECAY_RANK, GROUP_W), 0.5 * RWKV_DECAY_RANK ** -0.5),
        'rwkv_a0': nrm(ks[10], (L, GROUP_W), 0.1),
        'rwkv_a_up': nrm(ks[11], (L, RWKV_A_RANK, GROUP_W), 0.5 * RWKV_A_RANK ** -0.5),
        'rwkv_k_k': 0.85 + nrm(ks[12], (L, GROUP_W), 0.05),
        'rwkv_k_a': 1.0 + nrm(ks[13], (L, GROUP_W), 0.05),
        'rwkv_r_k': nrm(ks[14], (L, GROUP_W), 0.1),
        'rwkv_ln_w': 1.0 + nrm(ks[15], (L, GROUP_W), 0.02),
        'rwkv_ln_b': nrm(ks[16], (L, GROUP_W), 0.01),
        'sc_conv_w': nrm(ks[17], (L, SC_CONV, GROUP_W), SC_CONV ** -0.5),
        'gla_a_up': nrm(ks[18], (L, GLA_RANK, GLA_K), GLA_RANK ** -0.5),
        'gla_a_bias': 2.0 + nrm(ks[19], (L, GLA_K), 0.5),
        'gla_norm_w': 1.0 + nrm(ks[20], (L, HEAD_V), 0.02),
        'w_out': nrm(ks[21], (L, D_MIX, D_MODEL), D_MIX ** -0.5),
        'post_norm_w': 1.0 + nrm(ks[22], (L, D_MODEL), 0.02),
    }


def _fwd_reference(x, pre_norm_w, w_in, gdn_conv_w, gdn_a_log, gdn_dt_bias, gdn_norm_w,
              rwkv_mu, rwkv_w0, rwkv_w_up, rwkv_a0, rwkv_a_up, rwkv_k_k, rwkv_k_a, rwkv_r_k,
              rwkv_ln_w, rwkv_ln_b, sc_conv_w, gla_a_up, gla_a_bias, gla_norm_w, w_out, post_norm_w):
    for l in range(DEPTH):
        h = _rmsnorm(x, pre_norm_w[l])
        proj = jnp.einsum('btd,de->bte', h, w_in[l])
        p_gdn, p_rwkv, p_sc, p_gla = _split(proj, [GDN_COLS, RWKV_COLS, SC_COLS, GLA_COLS])
        y_gdn = _gated_deltanet(p_gdn, gdn_conv_w[l], gdn_a_log[l], gdn_dt_bias[l], gdn_norm_w[l])
        y_rwkv = _rwkv7(p_rwkv, rwkv_mu[l], rwkv_w0[l], rwkv_w_up[l], rwkv_a0[l], rwkv_a_up[l],
                        rwkv_k_k[l], rwkv_k_a[l], rwkv_r_k[l], rwkv_ln_w[l], rwkv_ln_b[l])
        y_sc = _short_conv(p_sc, sc_conv_w[l])
        y_gla = _gla(p_gla, gla_a_up[l], gla_a_bias[l], gla_norm_w[l])
        y = jnp.concatenate([y_gdn, y_rwkv, y_sc, y_gla], axis=-1)
        out = jnp.einsum('bte,ed->btd', y, w_out[l])
        x = x + _rmsnorm(out, post_norm_w[l])
    return x


import jax as _jax
import jax.numpy as _jnp

TWIN_FORMAT = 'train_step'
FWD_PARAMS = ['x', 'pre_norm_w', 'w_in', 'gdn_conv_w', 'gdn_a_log', 'gdn_dt_bias', 'gdn_norm_w', 'rwkv_mu', 'rwkv_w0', 'rwkv_w_up', 'rwkv_a0', 'rwkv_a_up', 'rwkv_k_k', 'rwkv_k_a', 'rwkv_r_k', 'rwkv_ln_w', 'rwkv_ln_b', 'sc_conv_w', 'gla_a_up', 'gla_a_bias', 'gla_norm_w', 'w_out', 'post_norm_w']
TWIN_WEIGHTS = ['pre_norm_w', 'w_in', 'gdn_conv_w', 'gdn_a_log', 'gdn_dt_bias', 'gdn_norm_w', 'rwkv_mu', 'rwkv_w0', 'rwkv_w_up', 'rwkv_a0', 'rwkv_a_up', 'rwkv_k_k', 'rwkv_k_a', 'rwkv_r_k', 'rwkv_ln_w', 'rwkv_ln_b', 'sc_conv_w', 'gla_a_up', 'gla_a_bias', 'gla_norm_w', 'w_out', 'post_norm_w']
TWIN_DIFF_INPUT = 'x'
TWIN_INPUTS = ['x', 'pre_norm_w', 'w_in', 'gdn_conv_w', 'gdn_a_log', 'gdn_dt_bias', 'gdn_norm_w', 'rwkv_mu', 'rwkv_w0', 'rwkv_w_up', 'rwkv_a0', 'rwkv_a_up', 'rwkv_k_k', 'rwkv_k_a', 'rwkv_r_k', 'rwkv_ln_w', 'rwkv_ln_b', 'sc_conv_w', 'gla_a_up', 'gla_a_bias', 'gla_norm_w', 'w_out', 'post_norm_w', 'loss_target', 'm_pre_norm_w', 'm_w_in', 'm_gdn_conv_w', 'm_gdn_a_log', 'm_gdn_dt_bias', 'm_gdn_norm_w', 'm_rwkv_mu', 'm_rwkv_w0', 'm_rwkv_w_up', 'm_rwkv_a0', 'm_rwkv_a_up', 'm_rwkv_k_k', 'm_rwkv_k_a', 'm_rwkv_r_k', 'm_rwkv_ln_w', 'm_rwkv_ln_b', 'm_sc_conv_w', 'm_gla_a_up', 'm_gla_a_bias', 'm_gla_norm_w', 'm_w_out', 'm_post_norm_w', 'v_pre_norm_w', 'v_w_in', 'v_gdn_conv_w', 'v_gdn_a_log', 'v_gdn_dt_bias', 'v_gdn_norm_w', 'v_rwkv_mu', 'v_rwkv_w0', 'v_rwkv_w_up', 'v_rwkv_a0', 'v_rwkv_a_up', 'v_rwkv_k_k', 'v_rwkv_k_a', 'v_rwkv_r_k', 'v_rwkv_ln_w', 'v_rwkv_ln_b', 'v_sc_conv_w', 'v_gla_a_up', 'v_gla_a_bias', 'v_gla_norm_w', 'v_w_out', 'v_post_norm_w']
TWIN_OUTPUTS = ['loss', 'grad_x', 'grad_pre_norm_w', 'grad_w_in', 'grad_gdn_conv_w', 'grad_gdn_a_log', 'grad_gdn_dt_bias', 'grad_gdn_norm_w', 'grad_rwkv_mu', 'grad_rwkv_w0', 'grad_rwkv_w_up', 'grad_rwkv_a0', 'grad_rwkv_a_up', 'grad_rwkv_k_k', 'grad_rwkv_k_a', 'grad_rwkv_r_k', 'grad_rwkv_ln_w', 'grad_rwkv_ln_b', 'grad_sc_conv_w', 'grad_gla_a_up', 'grad_gla_a_bias', 'grad_gla_norm_w', 'grad_w_out', 'grad_post_norm_w', 'delta_pre_norm_w', 'delta_w_in', 'delta_gdn_conv_w', 'delta_gdn_a_log', 'delta_gdn_dt_bias', 'delta_gdn_norm_w', 'delta_rwkv_mu', 'delta_rwkv_w0', 'delta_rwkv_w_up', 'delta_rwkv_a0', 'delta_rwkv_a_up', 'delta_rwkv_k_k', 'delta_rwkv_k_a', 'delta_rwkv_r_k', 'delta_rwkv_ln_w', 'delta_rwkv_ln_b', 'delta_sc_conv_w', 'delta_gla_a_up', 'delta_gla_a_bias', 'delta_gla_norm_w', 'delta_w_out', 'delta_post_norm_w', 'new_m_pre_norm_w', 'new_m_w_in', 'new_m_gdn_conv_w', 'new_m_gdn_a_log', 'new_m_gdn_dt_bias', 'new_m_gdn_norm_w', 'new_m_rwkv_mu', 'new_m_rwkv_w0', 'new_m_rwkv_w_up', 'new_m_rwkv_a0', 'new_m_rwkv_a_up', 'new_m_rwkv_k_k', 'new_m_rwkv_k_a', 'new_m_rwkv_r_k', 'new_m_rwkv_ln_w', 'new_m_rwkv_ln_b', 'new_m_sc_conv_w', 'new_m_gla_a_up', 'new_m_gla_a_bias', 'new_m_gla_norm_w', 'new_m_w_out', 'new_m_post_norm_w', 'new_v_pre_norm_w', 'new_v_w_in', 'new_v_gdn_conv_w', 'new_v_gdn_a_log', 'new_v_gdn_dt_bias', 'new_v_gdn_norm_w', 'new_v_rwkv_mu', 'new_v_rwkv_w0', 'new_v_rwkv_w_up', 'new_v_rwkv_a0', 'new_v_rwkv_a_up', 'new_v_rwkv_k_k', 'new_v_rwkv_k_a', 'new_v_rwkv_r_k', 'new_v_rwkv_ln_w', 'new_v_rwkv_ln_b', 'new_v_sc_conv_w', 'new_v_gla_a_up', 'new_v_gla_a_bias', 'new_v_gla_norm_w', 'new_v_w_out', 'new_v_post_norm_w']
TWIN_LEAF_KINDS = {'loss': 'loss', 'grad_x': 'grad_x', 'grad_pre_norm_w': 'grad_w', 'grad_w_in': 'grad_w', 'grad_gdn_conv_w': 'grad_w', 'grad_gdn_a_log': 'grad_w', 'grad_gdn_dt_bias': 'grad_w', 'grad_gdn_norm_w': 'grad_w', 'grad_rwkv_mu': 'grad_w', 'grad_rwkv_w0': 'grad_w', 'grad_rwkv_w_up': 'grad_w', 'grad_rwkv_a0': 'grad_w', 'grad_rwkv_a_up': 'grad_w', 'grad_rwkv_k_k': 'grad_w', 'grad_rwkv_k_a': 'grad_w', 'grad_rwkv_r_k': 'grad_w', 'grad_rwkv_ln_w': 'grad_w', 'grad_rwkv_ln_b': 'grad_w', 'grad_sc_conv_w': 'grad_w', 'grad_gla_a_up': 'grad_w', 'grad_gla_a_bias': 'grad_w', 'grad_gla_norm_w': 'grad_w', 'grad_w_out': 'grad_w', 'grad_post_norm_w': 'grad_w', 'delta_pre_norm_w': 'delta_w', 'delta_w_in': 'delta_w', 'delta_gdn_conv_w': 'delta_w', 'delta_gdn_a_log': 'delta_w', 'delta_gdn_dt_bias': 'delta_w', 'delta_gdn_norm_w': 'delta_w', 'delta_rwkv_mu': 'delta_w', 'delta_rwkv_w0': 'delta_w', 'delta_rwkv_w_up': 'delta_w', 'delta_rwkv_a0': 'delta_w', 'delta_rwkv_a_up': 'delta_w', 'delta_rwkv_k_k': 'delta_w', 'delta_rwkv_k_a': 'delta_w', 'delta_rwkv_r_k': 'delta_w', 'delta_rwkv_ln_w': 'delta_w', 'delta_rwkv_ln_b': 'delta_w', 'delta_sc_conv_w': 'delta_w', 'delta_gla_a_up': 'delta_w', 'delta_gla_a_bias': 'delta_w', 'delta_gla_norm_w': 'delta_w', 'delta_w_out': 'delta_w', 'delta_post_norm_w': 'delta_w', 'new_m_pre_norm_w': 'new_m', 'new_m_w_in': 'new_m', 'new_m_gdn_conv_w': 'new_m', 'new_m_gdn_a_log': 'new_m', 'new_m_gdn_dt_bias': 'new_m', 'new_m_gdn_norm_w': 'new_m', 'new_m_rwkv_mu': 'new_m', 'new_m_rwkv_w0': 'new_m', 'new_m_rwkv_w_up': 'new_m', 'new_m_rwkv_a0': 'new_m', 'new_m_rwkv_a_up': 'new_m', 'new_m_rwkv_k_k': 'new_m', 'new_m_rwkv_k_a': 'new_m', 'new_m_rwkv_r_k': 'new_m', 'new_m_rwkv_ln_w': 'new_m', 'new_m_rwkv_ln_b': 'new_m', 'new_m_sc_conv_w': 'new_m', 'new_m_gla_a_up': 'new_m', 'new_m_gla_a_bias': 'new_m', 'new_m_gla_norm_w': 'new_m', 'new_m_w_out': 'new_m', 'new_m_post_norm_w': 'new_m', 'new_v_pre_norm_w': 'new_v', 'new_v_w_in': 'new_v', 'new_v_gdn_conv_w': 'new_v', 'new_v_gdn_a_log': 'new_v', 'new_v_gdn_dt_bias': 'new_v', 'new_v_gdn_norm_w': 'new_v', 'new_v_rwkv_mu': 'new_v', 'new_v_rwkv_w0': 'new_v', 'new_v_rwkv_w_up': 'new_v', 'new_v_rwkv_a0': 'new_v', 'new_v_rwkv_a_up': 'new_v', 'new_v_rwkv_k_k': 'new_v', 'new_v_rwkv_k_a': 'new_v', 'new_v_rwkv_r_k': 'new_v', 'new_v_rwkv_ln_w': 'new_v', 'new_v_rwkv_ln_b': 'new_v', 'new_v_sc_conv_w': 'new_v', 'new_v_gla_a_up': 'new_v', 'new_v_gla_a_bias': 'new_v', 'new_v_gla_norm_w': 'new_v', 'new_v_w_out': 'new_v', 'new_v_post_norm_w': 'new_v'}


def _forward(args):
    return _fwd_reference(*[args[k] for k in FWD_PARAMS])


def _output_shape():
    out = _jax.eval_shape(lambda: _forward(_fwd_setup_inputs(0)))
    return out.shape, out.dtype

N_MICROBATCH = 1
ADAM_LR = 0.001
ADAM_B1 = 0.9
ADAM_B2 = 0.999
ADAM_EPS = 1e-08
ADAM_WD = 0.01
ADAM_STEP = 10
PER_EXAMPLE_BATCH_AXIS = {'x': 0, 'loss_target': 0}
SHARED_INPUTS = []
_WEIGHT_DTYPES = {'pre_norm_w': _jnp.float32, 'w_in': _jnp.float32, 'gdn_conv_w': _jnp.float32, 'gdn_a_log': _jnp.float32, 'gdn_dt_bias': _jnp.float32, 'gdn_norm_w': _jnp.float32, 'rwkv_mu': _jnp.float32, 'rwkv_w0': _jnp.float32, 'rwkv_w_up': _jnp.float32, 'rwkv_a0': _jnp.float32, 'rwkv_a_up': _jnp.float32, 'rwkv_k_k': _jnp.float32, 'rwkv_k_a': _jnp.float32, 'rwkv_r_k': _jnp.float32, 'rwkv_ln_w': _jnp.float32, 'rwkv_ln_b': _jnp.float32, 'sc_conv_w': _jnp.float32, 'gla_a_up': _jnp.float32, 'gla_a_bias': _jnp.float32, 'gla_norm_w': _jnp.float32, 'w_out': _jnp.float32, 'post_norm_w': _jnp.float32}
MOMENT_SCALE = {'pre_norm_w': 9.181772e-01, 'w_in': 4.603906e-01, 'gdn_conv_w': 4.748202e-01, 'gdn_a_log': 2.197067e+00, 'gdn_dt_bias': 2.117973e+00, 'gdn_norm_w': 1.354662e+00, 'rwkv_mu': 6.709214e-01, 'rwkv_w0': 1.899658e-01, 'rwkv_w_up': 4.637819e-02, 'rwkv_a0': 1.901416e-01, 'rwkv_a_up': 1.686505e-01, 'rwkv_k_k': 1.039058e-01, 'rwkv_k_a': 4.234545e-01, 'rwkv_r_k': 7.743423e-01, 'rwkv_ln_w': 3.705519e-01, 'rwkv_ln_b': 8.239714e-01, 'sc_conv_w': 4.848599e-01, 'gla_a_up': 1.487171e-01, 'gla_a_bias': 7.414263e-01, 'gla_norm_w': 9.913795e-01, 'w_out': 5.161074e-01, 'post_norm_w': 3.191108e+01}


def _to_microbatches(a, axis):
    t = _jnp.moveaxis(a, axis, 0)
    t = t.reshape((N_MICROBATCH, t.shape[0] // N_MICROBATCH) + t.shape[1:])
    return _jnp.moveaxis(t, 1, axis + 1)


def setup_inputs(seed: int = 0) -> dict:
    inp = _fwd_setup_inputs(seed)
    key = _jax.random.fold_in(_jax.random.key(seed), 7919)
    shape, _ = _output_shape()
    out = dict(inp)
    out["loss_target"] = _jax.random.normal(_jax.random.fold_in(key, 0), shape, _jnp.float32)
    for i, name in enumerate(TWIN_WEIGHTS):
        w = inp[name].astype(_jnp.float32)
        if MOMENT_SCALE is None:
            s = _jnp.sqrt(_jnp.mean(_jnp.square(w)) + 1e-30)
        else:
            s = MOMENT_SCALE[name]
        km, kv = _jax.random.split(_jax.random.fold_in(key, i + 1))
        out[name] = w
        out["m_" + name] = s * _jax.random.normal(km, w.shape, _jnp.float32)
        out["v_" + name] = (s * s) * _jax.random.uniform(kv, w.shape, _jnp.float32, 0.5, 1.5)
    if N_MICROBATCH > 1:
        for name, axis in PER_EXAMPLE_BATCH_AXIS.items():
            out[name] = _to_microbatches(out[name], axis)
    return {'x': out['x'], 'pre_norm_w': out['pre_norm_w'], 'w_in': out['w_in'], 'gdn_conv_w': out['gdn_conv_w'], 'gdn_a_log': out['gdn_a_log'], 'gdn_dt_bias': out['gdn_dt_bias'], 'gdn_norm_w': out['gdn_norm_w'], 'rwkv_mu': out['rwkv_mu'], 'rwkv_w0': out['rwkv_w0'], 'rwkv_w_up': out['rwkv_w_up'], 'rwkv_a0': out['rwkv_a0'], 'rwkv_a_up': out['rwkv_a_up'], 'rwkv_k_k': out['rwkv_k_k'], 'rwkv_k_a': out['rwkv_k_a'], 'rwkv_r_k': out['rwkv_r_k'], 'rwkv_ln_w': out['rwkv_ln_w'], 'rwkv_ln_b': out['rwkv_ln_b'], 'sc_conv_w': out['sc_conv_w'], 'gla_a_up': out['gla_a_up'], 'gla_a_bias': out['gla_a_bias'], 'gla_norm_w': out['gla_norm_w'], 'w_out': out['w_out'], 'post_norm_w': out['post_norm_w'], 'loss_target': out['loss_target'], 'm_pre_norm_w': out['m_pre_norm_w'], 'm_w_in': out['m_w_in'], 'm_gdn_conv_w': out['m_gdn_conv_w'], 'm_gdn_a_log': out['m_gdn_a_log'], 'm_gdn_dt_bias': out['m_gdn_dt_bias'], 'm_gdn_norm_w': out['m_gdn_norm_w'], 'm_rwkv_mu': out['m_rwkv_mu'], 'm_rwkv_w0': out['m_rwkv_w0'], 'm_rwkv_w_up': out['m_rwkv_w_up'], 'm_rwkv_a0': out['m_rwkv_a0'], 'm_rwkv_a_up': out['m_rwkv_a_up'], 'm_rwkv_k_k': out['m_rwkv_k_k'], 'm_rwkv_k_a': out['m_rwkv_k_a'], 'm_rwkv_r_k': out['m_rwkv_r_k'], 'm_rwkv_ln_w': out['m_rwkv_ln_w'], 'm_rwkv_ln_b': out['m_rwkv_ln_b'], 'm_sc_conv_w': out['m_sc_conv_w'], 'm_gla_a_up': out['m_gla_a_up'], 'm_gla_a_bias': out['m_gla_a_bias'], 'm_gla_norm_w': out['m_gla_norm_w'], 'm_w_out': out['m_w_out'], 'm_post_norm_w': out['m_post_norm_w'], 'v_pre_norm_w': out['v_pre_norm_w'], 'v_w_in': out['v_w_in'], 'v_gdn_conv_w': out['v_gdn_conv_w'], 'v_gdn_a_log': out['v_gdn_a_log'], 'v_gdn_dt_bias': out['v_gdn_dt_bias'], 'v_gdn_norm_w': out['v_gdn_norm_w'], 'v_rwkv_mu': out['v_rwkv_mu'], 'v_rwkv_w0': out['v_rwkv_w0'], 'v_rwkv_w_up': out['v_rwkv_w_up'], 'v_rwkv_a0': out['v_rwkv_a0'], 'v_rwkv_a_up': out['v_rwkv_a_up'], 'v_rwkv_k_k': out['v_rwkv_k_k'], 'v_rwkv_k_a': out['v_rwkv_k_a'], 'v_rwkv_r_k': out['v_rwkv_r_k'], 'v_rwkv_ln_w': out['v_rwkv_ln_w'], 'v_rwkv_ln_b': out['v_rwkv_ln_b'], 'v_sc_conv_w': out['v_sc_conv_w'], 'v_gla_a_up': out['v_gla_a_up'], 'v_gla_a_bias': out['v_gla_a_bias'], 'v_gla_norm_w': out['v_gla_norm_w'], 'v_w_out': out['v_w_out'], 'v_post_norm_w': out['v_post_norm_w']}


def _loss(weights, diff, rest, loss_target):
    with _jax.named_scope("forward"):
        args = {**rest, TWIN_DIFF_INPUT: diff, **{k: w.astype(_WEIGHT_DTYPES[k]) for k, w in weights.items()}}
        y = _forward(args)
    with _jax.named_scope("loss_head"):
        err = _jnp.square(y.astype(_jnp.float32) - loss_target)
        return 0.5 * _jnp.sum(_jnp.mean(err, axis=-1)) if err.ndim else 0.5 * err


def _adamw(w, g, m, v):
    m = ADAM_B1 * m + (1.0 - ADAM_B1) * g
    v = ADAM_B2 * v + (1.0 - ADAM_B2) * _jnp.square(g)
    m_hat = m / (1.0 - ADAM_B1 ** ADAM_STEP)
    v_hat = v / (1.0 - ADAM_B2 ** ADAM_STEP)
    delta = -ADAM_LR * (m_hat / (_jnp.sqrt(v_hat) + ADAM_EPS) + ADAM_WD * w)
    return delta, m, v


def reference(x, pre_norm_w, w_in, gdn_conv_w, gdn_a_log, gdn_dt_bias, gdn_norm_w, rwkv_mu, rwkv_w0, rwkv_w_up, rwkv_a0, rwkv_a_up, rwkv_k_k, rwkv_k_a, rwkv_r_k, rwkv_ln_w, rwkv_ln_b, sc_conv_w, gla_a_up, gla_a_bias, gla_norm_w, w_out, post_norm_w, loss_target, m_pre_norm_w, m_w_in, m_gdn_conv_w, m_gdn_a_log, m_gdn_dt_bias, m_gdn_norm_w, m_rwkv_mu, m_rwkv_w0, m_rwkv_w_up, m_rwkv_a0, m_rwkv_a_up, m_rwkv_k_k, m_rwkv_k_a, m_rwkv_r_k, m_rwkv_ln_w, m_rwkv_ln_b, m_sc_conv_w, m_gla_a_up, m_gla_a_bias, m_gla_norm_w, m_w_out, m_post_norm_w, v_pre_norm_w, v_w_in, v_gdn_conv_w, v_gdn_a_log, v_gdn_dt_bias, v_gdn_norm_w, v_rwkv_mu, v_rwkv_w0, v_rwkv_w_up, v_rwkv_a0, v_rwkv_a_up, v_rwkv_k_k, v_rwkv_k_a, v_rwkv_r_k, v_rwkv_ln_w, v_rwkv_ln_b, v_sc_conv_w, v_gla_a_up, v_gla_a_bias, v_gla_norm_w, v_w_out, v_post_norm_w):
    given = dict(x=x, pre_norm_w=pre_norm_w, w_in=w_in, gdn_conv_w=gdn_conv_w, gdn_a_log=gdn_a_log, gdn_dt_bias=gdn_dt_bias, gdn_norm_w=gdn_norm_w, rwkv_mu=rwkv_mu, rwkv_w0=rwkv_w0, rwkv_w_up=rwkv_w_up, rwkv_a0=rwkv_a0, rwkv_a_up=rwkv_a_up, rwkv_k_k=rwkv_k_k, rwkv_k_a=rwkv_k_a, rwkv_r_k=rwkv_r_k, rwkv_ln_w=rwkv_ln_w, rwkv_ln_b=rwkv_ln_b, sc_conv_w=sc_conv_w, gla_a_up=gla_a_up, gla_a_bias=gla_a_bias, gla_norm_w=gla_norm_w, w_out=w_out, post_norm_w=post_norm_w, loss_target=loss_target, m_pre_norm_w=m_pre_norm_w, m_w_in=m_w_in, m_gdn_conv_w=m_gdn_conv_w, m_gdn_a_log=m_gdn_a_log, m_gdn_dt_bias=m_gdn_dt_bias, m_gdn_norm_w=m_gdn_norm_w, m_rwkv_mu=m_rwkv_mu, m_rwkv_w0=m_rwkv_w0, m_rwkv_w_up=m_rwkv_w_up, m_rwkv_a0=m_rwkv_a0, m_rwkv_a_up=m_rwkv_a_up, m_rwkv_k_k=m_rwkv_k_k, m_rwkv_k_a=m_rwkv_k_a, m_rwkv_r_k=m_rwkv_r_k, m_rwkv_ln_w=m_rwkv_ln_w, m_rwkv_ln_b=m_rwkv_ln_b, m_sc_conv_w=m_sc_conv_w, m_gla_a_up=m_gla_a_up, m_gla_a_bias=m_gla_a_bias, m_gla_norm_w=m_gla_norm_w, m_w_out=m_w_out, m_post_norm_w=m_post_norm_w, v_pre_norm_w=v_pre_norm_w, v_w_in=v_w_in, v_gdn_conv_w=v_gdn_conv_w, v_gdn_a_log=v_gdn_a_log, v_gdn_dt_bias=v_gdn_dt_bias, v_gdn_norm_w=v_gdn_norm_w, v_rwkv_mu=v_rwkv_mu, v_rwkv_w0=v_rwkv_w0, v_rwkv_w_up=v_rwkv_w_up, v_rwkv_a0=v_rwkv_a0, v_rwkv_a_up=v_rwkv_a_up, v_rwkv_k_k=v_rwkv_k_k, v_rwkv_k_a=v_rwkv_k_a, v_rwkv_r_k=v_rwkv_r_k, v_rwkv_ln_w=v_rwkv_ln_w, v_rwkv_ln_b=v_rwkv_ln_b, v_sc_conv_w=v_sc_conv_w, v_gla_a_up=v_gla_a_up, v_gla_a_bias=v_gla_a_bias, v_gla_norm_w=v_gla_norm_w, v_w_out=v_w_out, v_post_norm_w=v_post_norm_w)
    weights = {n: given[n] for n in TWIN_WEIGHTS}
    shared = {n: given[n] for n in SHARED_INPUTS}
    per_example = {n: given[n] for n in ['x']}
    grad_fn = _jax.value_and_grad(_loss, argnums=(0, 1))

    def one_microbatch(ex, loss_target):
        ex = dict(ex)
        diff = ex.pop(TWIN_DIFF_INPUT)
        return grad_fn(weights, diff, {**shared, **ex}, loss_target)

    if N_MICROBATCH == 1:
        loss, (grad_w, grad_x) = one_microbatch(per_example, given["loss_target"])
    else:
        def body(carry, xs):
            loss_sum, grad_sum = carry
            l_k, (gw_k, gx_k) = one_microbatch(xs[0], xs[1])
            with _jax.named_scope("update"):
                return (loss_sum + l_k, _jax.tree.map(_jnp.add, grad_sum, gw_k)), gx_k

        init = (_jnp.zeros((), _jnp.float32), _jax.tree.map(_jnp.zeros_like, weights))
        (loss, grad_w), grad_x = _jax.lax.scan(body, init, (per_example, given["loss_target"]))
    with _jax.named_scope("update"):
        delta_w, new_m, new_v = {}, {}, {}
        for n in TWIN_WEIGHTS:
            delta_w[n], new_m[n], new_v[n] = _adamw(weights[n], grad_w[n], given["m_" + n], given["v_" + n])
    return (loss, grad_x, *[grad_w[n] for n in TWIN_WEIGHTS], *[delta_w[n] for n in TWIN_WEIGHTS],
            *[new_m[n] for n in TWIN_WEIGHTS], *[new_v[n] for n in TWIN_WEIGHTS])
```

```python
import functools
import math

import numpy as np
import jax
import jax.numpy as jnp
from jax import lax
from jax.experimental import pallas as pl
from jax.experimental.pallas import tpu as pltpu

F32 = jnp.float32
BF16 = jnp.bfloat16
HI = lax.Precision.HIGHEST

D_MODEL = 1024
DEPTH = 2
NH = 4
DH = 64
CH = 64
EPS = 1e-6
RWKV_GN_EPS = 64e-5
GLA_HEAD_K = 32
GLA_TAU = 16.0
GDN_TAPS = 4
SC_TAPS = 3
D_IN = 3992
N_DEV = 8

G_GDN_Q, G_GDN_K, G_GDN_V, G_GDN_Z, G_GDN_AB = 0, 4, 8, 12, 16
G_RWKV = 17
G_SC = 35
G_GLA_Q, G_GLA_K, G_GLA_V, G_GLA_Z, G_GLA_AD = 51, 55, 59, 63, 67
N_GROUPS = 68
GROUPS_PER_STEP = 4

C_GDN, C_RWKV, C_SC, C_GLA = 0, 1032, 2184, 3208

ADAM_LR, ADAM_B1, ADAM_B2, ADAM_EPS, ADAM_WD, ADAM_STEP = 0.001, 0.9, 0.999, 1e-08, 0.01, 10

VMEM_LIMIT = 56 * 1024 * 1024
MESH = pl.DeviceIdType.MESH

_pcall = pl.pallas_call


def _cparams(sem=None):
    if sem is None:
        return pltpu.CompilerParams(vmem_limit_bytes=VMEM_LIMIT)
    return pltpu.CompilerParams(dimension_semantics=sem, vmem_limit_bytes=VMEM_LIMIT)


def _dot(a, b):
    return jnp.dot(a, b, precision=HI, preferred_element_type=F32)


def _dot_nt(a, b):
    return lax.dot_general(a, b, (((1,), (1,)), ((), ())), precision=HI, preferred_element_type=F32)


def _dot_tn(a, b):
    return lax.dot_general(a, b, (((0,), (0,)), ((), ())), precision=HI, preferred_element_type=F32)


def _r(x):
    return x.astype(BF16)


@jax.custom_vjp
def _bmm(a, b):
    return jnp.dot(_r(a), _r(b), preferred_element_type=F32)


def _bmm_fwd(a, b):
    return _bmm(a, b), (a, b)


def _bmm_bwd(res, g):
    a, b = res
    return (lax.dot_general(_r(g), _r(b), (((1,), (1,)), ((), ())), preferred_element_type=F32),
            lax.dot_general(_r(a), _r(g), (((0,), (0,)), ((), ())), preferred_element_type=F32))


_bmm.defvjp(_bmm_fwd, _bmm_bwd)


@jax.custom_vjp
def _bmm_nt(a, b):
    return lax.dot_general(_r(a), _r(b), (((1,), (1,)), ((), ())), preferred_element_type=F32)


def _bmm_nt_fwd(a, b):
    return _bmm_nt(a, b), (a, b)


def _bmm_nt_bwd(res, g):
    a, b = res
    return (jnp.dot(_r(g), _r(b), preferred_element_type=F32),
            lax.dot_general(_r(g), _r(a), (((0,), (0,)), ((), ())), preferred_element_type=F32))


_bmm_nt.defvjp(_bmm_nt_fwd, _bmm_nt_bwd)


@jax.custom_vjp
def _bmm_tn(a, b):
    return lax.dot_general(_r(a), _r(b), (((0,), (0,)), ((), ())), preferred_element_type=F32)


def _bmm_tn_fwd(a, b):
    return _bmm_tn(a, b), (a, b)


def _bmm_tn_bwd(res, g):
    a, b = res
    return (lax.dot_general(_r(b), _r(g), (((1,), (1,)), ((), ())), preferred_element_type=F32),
            jnp.dot(_r(a), _r(g), preferred_element_type=F32))


_bmm_tn.defvjp(_bmm_tn_fwd, _bmm_tn_bwd)


def _tri(n):
    i = lax.broadcasted_iota(jnp.int32, (n, n), 0)
    j = lax.broadcasted_iota(jnp.int32, (n, n), 1)
    return i >= j, i > j, i == j


def _inv_unit_lower(a, n):
    _, _, eye = _tri(n)
    pw = -a
    inv = eye.astype(F32) + pw
    for _ in range(int(math.log2(n)) - 1):
        pw = _dot(pw, pw)
        inv = inv + _dot(inv, pw)
    return inv


def _silu(x):
    return x * jax.nn.sigmoid(x)


def _gdn_chunk(prm, cst, ins, s):
    a_log, dt_b, nw = prm
    m_a, m_b = cst
    cq, ck, cv, z, ab = ins
    incl, strict, _ = _tri(CH)
    q = _silu(cq)
    k = _silu(ck)
    v = _silu(cv)
    q = q * lax.rsqrt(jnp.sum(q * q, -1, keepdims=True) + EPS) * (DH ** -0.5)
    k = k * lax.rsqrt(jnp.sum(k * k, -1, keepdims=True) + EPS)
    a_raw = jnp.sum(ab * m_a, -1, keepdims=True)
    b_raw = jnp.sum(ab * m_b, -1, keepdims=True)
    gstep = -jnp.exp(a_log) * jax.nn.softplus(a_raw + dt_b)
    beta = jax.nn.sigmoid(b_raw)
    gc = _dot(incl.astype(F32), gstep)
    gl = jnp.sum(gstep, 0, keepdims=True)
    dec = jnp.where(incl, jnp.exp(jnp.where(incl, gc - gc.T, 0.0)), 0.0)
    kb = k * beta
    a_mat = jnp.where(strict, _bmm_nt(kb, k) * dec, 0.0)
    tinv = _inv_unit_lower(a_mat, CH)
    eg = jnp.exp(gc)
    u = _dot(tinv, v * beta)
    w = _dot(tinv, kb * eg)
    attn = _bmm_nt(q, k) * dec
    v_new = u - _bmm(w, s)
    o = _bmm(q * eg, s) + _bmm(attn, v_new)
    s_next = s * jnp.exp(gl) + _bmm_tn(k * jnp.exp(gl - gc), v_new)
    on = o * lax.rsqrt(jnp.mean(o * o, -1, keepdims=True) + EPS) * nw
    return on * _silu(z), s_next


def _gla_chunk(prm, cst, ins, st):
    a_up, a_bias, nw = prm
    q, k, v, z, ad = ins
    incl, _, _ = _tri(CH)
    la = jax.nn.log_sigmoid(_bmm(ad, a_up) + a_bias) * (1.0 / GLA_TAU)
    bc = _dot(incl.astype(F32), la)
    bl = jnp.sum(la, 0, keepdims=True)
    qe = q * (GLA_HEAD_K ** -0.5) * jnp.exp(bc)
    ke = k * jnp.exp(-bc)
    attn = jnp.where(incl, _bmm_nt(qe, ke), 0.0)
    o = _bmm_nt(qe, st) + _bmm(attn, v)
    st_next = st * jnp.exp(bl) + _bmm_tn(v, k * jnp.exp(bl - bc))
    on = o * lax.rsqrt(jnp.mean(o * o, -1, keepdims=True) + EPS) * nw
    return on * _silu(z), st_next


def _rwkv_chunk(prm, cst, ins, s):
    r, v = ins[0], ins[2]
    incl, strict, _ = _tri(CH)
    lw, kk, k2, m = _rwkv_pre(prm, ins)
    cum = _dot(incl.astype(F32), lw)
    ltot = jnp.sum(lw, 0, keepdims=True)
    n_t = -kk * jnp.exp(cum - lw)
    einv = jnp.exp(-cum)
    m_t = m * einv
    k_t = k2 * einv
    r_t = r * jnp.exp(cum)
    a_nm = jnp.where(strict, _dot_nt(n_t, m_t), 0.0)
    a_nk = jnp.where(strict, _dot_nt(n_t, k_t), 0.0)
    cm = _dot(_inv_unit_lower(-a_nm, CH), _dot_nt(n_t, s) + _dot(a_nk, v))
    y = (_dot_nt(r_t, s) + _dot(jnp.where(incl, _dot_nt(r_t, m_t), 0.0), cm)
         + _dot(jnp.where(incl, _dot_nt(r_t, k_t), 0.0), v))
    eend = jnp.exp(ltot - cum)
    s_next = s * jnp.exp(ltot) + _dot_tn(cm, m * eend) + _dot_tn(v, k2 * eend)
    return _rwkv_post(prm, ins, y, k2), s_next


def _rwkv_pre(prm, ins):
    w0, w_up, a0, a_up, k_k, k_a = prm[:6]
    k, wd, ad = ins[1], ins[4], ins[5]
    lw = -math.exp(-0.5) * jax.nn.sigmoid(w0 + _bmm(jnp.tanh(wd), w_up))
    a = jax.nn.sigmoid(a0 + _bmm(ad, a_up))
    kk = k * k_k
    kk = kk * lax.rsqrt(jnp.sum(kk * kk, -1, keepdims=True) + EPS)
    k2 = k * (1.0 + (a - 1.0) * k_a)
    return lw, kk, k2, kk * a


def _rwkv_post(prm, ins, y, k2):
    r_k, ln_w, ln_b = prm[6:]
    r, v, z = ins[0], ins[2], ins[3]
    mean = jnp.mean(y, -1, keepdims=True)
    yc = y - mean
    var = jnp.mean(yc * yc, -1, keepdims=True)
    yn = yc * lax.rsqrt(var + RWKV_GN_EPS) * ln_w + ln_b
    bonus = jnp.sum(r * k2 * r_k, -1, keepdims=True) * v
    return (yn + bonus) * _silu(z)


@jax.custom_vjp
def _bmv(s, x):
    return jnp.sum(_r(s).astype(F32) * _r(x).astype(F32), -1, keepdims=True)


def _bmv_fwd(s, x):
    return _bmv(s, x), (s, x)


def _bmv_bwd(res, g):
    s, x = res
    return g * x, jnp.sum(_r(s).astype(F32) * _r(g).astype(F32), 0, keepdims=True)


_bmv.defvjp(_bmv_fwd, _bmv_bwd)


def _rwkv_chunk_steps(prm, cst, ins, s):
    r, v = ins[0], ins[2]
    lw, kk, k2, m = _rwkv_pre(prm, ins)
    w = jnp.exp(lw)
    v_t = v.T
    lane = lax.broadcasted_iota(jnp.int32, (1, CH), 1)
    y_t = jnp.zeros((DH, CH), F32)
    for t in range(CH):
        e_t = (lane == t).astype(F32)
        row = slice(t, t + 1)
        sa = _bmv(s, -kk[row])
        s = s * w[row] + sa * m[row] + jnp.sum(v_t * e_t, -1, keepdims=True) * k2[row]
        y_t = y_t + _bmv(s, r[row]) * e_t
    return _rwkv_post(prm, ins, y_t.T, k2), s


def _slab_spec(t, gfun):
    return pl.BlockSpec((None, t, DH), lambda b, h: (gfun(h), b, 0))


def _head_spec(rows):
    return pl.BlockSpec((None, rows, DH), lambda b, h: (h, 0, 0))


def _mixer_fwd(chunk_fn, name, ins, prm, cst, nb, t, first_fn=None):
    nc = t // CH
    n_in, n_prm, n_cst = len(ins), len(prm), len(cst)

    def body(*refs):
        in_refs = refs[:n_in]
        prm_refs = refs[n_in:n_in + n_prm]
        cst_refs = refs[n_in + n_prm:n_in + n_prm + n_cst]
        y_ref, ck_ref, s_scr = refs[n_in + n_prm + n_cst:]
        prm_v = [r[...] for r in prm_refs]
        cst_v = [r[...] for r in cst_refs]
        s_scr[...] = jnp.zeros_like(s_scr)

        def chunk(c, i, fn):
            s = s_scr[...]
            ck_ref[c] = s
            y, s_next = fn(prm_v, cst_v, [r[pl.ds(i, CH), :] for r in in_refs], s)
            y_ref[pl.ds(i, CH), :] = y.astype(BF16)
            s_scr[...] = s_next

        def step(c, carry):
            chunk(c, pl.multiple_of(c * CH, CH), chunk_fn)
            return carry

        chunk(0, 0, first_fn or chunk_fn)
        lax.fori_loop(1, nc, step, 0)

    n = nb * t
    in_specs = [_slab_spec(t, (lambda h, g0=g0, ph=ph: g0 + h * ph)) for _, g0, ph in ins]
    in_specs += [_head_spec(p.shape[1]) for p in prm] + [_head_spec(p.shape[1]) for p in cst]
    return _pcall(
        body, name=name, grid=(nb, NH),
        in_specs=in_specs,
        out_specs=[_slab_spec(t, lambda h: h),
                   pl.BlockSpec((None, nc, DH, DH), lambda b, h: (b * NH + h, 0, 0, 0))],
        out_shape=[jax.ShapeDtypeStruct((NH, n, DH), BF16),
                   jax.ShapeDtypeStruct((nb * NH, nc, DH, DH), F32)],
        scratch_shapes=[pltpu.VMEM((DH, DH), F32)],
        compiler_params=_cparams(("parallel", "parallel")),
    )(*[a for a, _, _ in ins], *prm, *cst)


def _mixer_bwd(chunk_fn, name, ins, prm, cst, ck, dy, d_dtypes, nb, t, first_fn=None):
    nc = t // CH
    n_in, n_prm, n_cst = len(ins), len(prm), len(cst)

    def body(*refs):
        in_refs = refs[:n_in]
        prm_refs = refs[n_in:n_in + n_prm]
        cst_refs = refs[n_in + n_prm:n_in + n_prm + n_cst]
        ck_ref, dy_ref = refs[n_in + n_prm + n_cst:n_in + n_prm + n_cst + 2]
        outs = refs[n_in + n_prm + n_cst + 2:]
        din_refs = outs[:n_in]
        dprm_refs = outs[n_in:n_in + n_prm]
        ds_scr = outs[n_in + n_prm]
        prm_v = [r[...] for r in prm_refs]
        cst_v = [r[...] for r in cst_refs]
        ds_scr[...] = jnp.zeros_like(ds_scr)
        for r in dprm_refs:
            r[...] = jnp.zeros_like(r)

        def chunk(c, i, fn):
            ins_c = [r[pl.ds(i, CH), :] for r in in_refs]
            _, vjp = jax.vjp(lambda p, x, s: fn(p, cst_v, x, s), prm_v, ins_c, ck_ref[c])
            d_prm, d_ins, d_s = vjp((dy_ref[pl.ds(i, CH), :], ds_scr[...]))
            for r, g in zip(din_refs, d_ins):
                r[pl.ds(i, CH), :] = g.astype(r.dtype)
            for r, g in zip(dprm_refs, d_prm):
                r[...] += g
            ds_scr[...] = d_s

        def step(j, carry):
            c = nc - 1 - j
            chunk(c, pl.multiple_of(c * CH, CH), chunk_fn)
            return carry

        lax.fori_loop(0, nc - 1, step, 0)
        chunk(0, 0, first_fn or chunk_fn)

    n = nb * t
    in_specs = [_slab_spec(t, (lambda h, g0=g0, ph=ph: g0 + h * ph)) for _, g0, ph in ins]
    in_specs += [_head_spec(p.shape[1]) for p in prm] + [_head_spec(p.shape[1]) for p in cst]
    in_specs += [pl.BlockSpec((None, nc, DH, DH), lambda b, h: (b * NH + h, 0, 0, 0)),
                 _slab_spec(t, lambda h: h)]
    out_specs = [_slab_spec(t, lambda h: h) for _ in ins]
    out_specs += [pl.BlockSpec((None, p.shape[1], DH), lambda b, h: (b * NH + h, 0, 0)) for p in prm]
    out_shape = [jax.ShapeDtypeStruct((NH, n, DH), dt) for dt in d_dtypes]
    out_shape += [jax.ShapeDtypeStruct((nb * NH, p.shape[1], DH), F32) for p in prm]
    res = _pcall(
        body, name=name, grid=(nb, NH),
        in_specs=in_specs, out_specs=out_specs, out_shape=out_shape,
        scratch_shapes=[pltpu.VMEM((DH, DH), F32)],
        compiler_params=_cparams(("parallel", "parallel")),
    )(*[a for a, _, _ in ins], *prm, *cst, ck, dy)
    return res[:n_in], res[n_in:]


def _shift_down(x, s):
    if s == 0:
        return x
    row = lax.broadcasted_iota(jnp.int32, x.shape, 0)
    return jnp.where(row < s, 0.0, pltpu.roll(x, s, 0))


def _shift_up(x, s):
    if s == 0:
        return x
    t = x.shape[0]
    row = lax.broadcasted_iota(jnp.int32, x.shape, 0)
    return jnp.where(row >= t - s, 0.0, pltpu.roll(x, t - s, 0))


def _conv_fwd(p, g0, ng, w, nb, t, name):
    taps = w.shape[1]

    def body(x_ref, w_ref, y_ref):
        x = x_ref[...]
        wv = w_ref[...]
        acc = wv[taps - 1:taps, :] * x
        for i in range(taps - 1):
            acc = acc + wv[i:i + 1, :] * _shift_down(x, taps - 1 - i)
        y_ref[...] = acc

    return _pcall(
        body, name=name, grid=(ng, nb),
        in_specs=[pl.BlockSpec((None, t, DH), lambda g, b: (g0 + g, b, 0)),
                  pl.BlockSpec((None, taps, DH), lambda g, b: (g, 0, 0))],
        out_specs=pl.BlockSpec((None, t, DH), lambda g, b: (g, b, 0)),
        out_shape=jax.ShapeDtypeStruct((ng, nb * t, DH), F32),
        compiler_params=_cparams(("parallel", "parallel")),
    )(p, w)


def _conv_bwd(p, g0, ng, w, dy, nb, t, name):
    taps = w.shape[1]

    def body(x_ref, w_ref, dy_ref, dx_ref, dw_ref):
        x = x_ref[...]
        wv = w_ref[...]
        d = dy_ref[...]
        acc = wv[taps - 1:taps, :] * d
        rows = [None] * taps
        rows[taps - 1] = jnp.sum(d * x, 0, keepdims=True)
        for i in range(taps - 1):
            s = taps - 1 - i
            acc = acc + wv[i:i + 1, :] * _shift_up(d, s)
            rows[i] = jnp.sum(d * _shift_down(x, s), 0, keepdims=True)
        dx_ref[...] = acc.astype(BF16)
        for i in range(taps):
            dw_ref[i:i + 1, :] = rows[i]

    return _pcall(
        body, name=name, grid=(ng, nb),
        in_specs=[pl.BlockSpec((None, t, DH), lambda g, b: (g0 + g, b, 0)),
                  pl.BlockSpec((None, taps, DH), lambda g, b: (g, 0, 0)),
                  pl.BlockSpec((None, t, DH), lambda g, b: (g, b, 0))],
        out_specs=[pl.BlockSpec((None, t, DH), lambda g, b: (g, b, 0)),
                   pl.BlockSpec((None, None, taps, DH), lambda g, b: (g, b, 0, 0))],
        out_shape=[jax.ShapeDtypeStruct((ng, nb * t, DH), BF16),
                   jax.ShapeDtypeStruct((ng, nb, taps, DH), F32)],
        compiler_params=_cparams(("parallel", "parallel")),
    )(p, w, dy)


def _mix_fwd(p, g0, ng, mu, nb, t, name):
    def body(x_ref, mu_ref, y_ref):
        x = x_ref[...]
        y_ref[...] = x + mu_ref[...] * (_shift_down(x, 1) - x)

    return _pcall(
        body, name=name, grid=(ng, nb),
        in_specs=[pl.BlockSpec((None, t, DH), lambda g, b: (g0 + g, b, 0)),
                  pl.BlockSpec((None, 1, DH), lambda g, b: (g, 0, 0))],
        out_specs=pl.BlockSpec((None, t, DH), lambda g, b: (g, b, 0)),
        out_shape=jax.ShapeDtypeStruct((ng, nb * t, DH), F32),
        compiler_params=_cparams(("parallel", "parallel")),
    )(p, mu)


def _mix_bwd(p, g0, ng, mu, dy, nb, t, name):
    def body(x_ref, mu_ref, dy_ref, dx_ref, dmu_ref):
        x = x_ref[...]
        muv = mu_ref[...]
        d = dy_ref[...]
        dx_ref[...] = (d * (1.0 - muv) + _shift_up(d * muv, 1)).astype(BF16)
        dmu_ref[...] = jnp.sum(d * (_shift_down(x, 1) - x), 0, keepdims=True)

    return _pcall(
        body, name=name, grid=(ng, nb),
        in_specs=[pl.BlockSpec((None, t, DH), lambda g, b: (g0 + g, b, 0)),
                  pl.BlockSpec((None, 1, DH), lambda g, b: (g, 0, 0)),
                  pl.BlockSpec((None, t, DH), lambda g, b: (g, b, 0))],
        out_specs=[pl.BlockSpec((None, t, DH), lambda g, b: (g, b, 0)),
                   pl.BlockSpec((None, None, 1, DH), lambda g, b: (g, b, 0, 0))],
        out_shape=[jax.ShapeDtypeStruct((ng, nb * t, DH), BF16),
                   jax.ShapeDtypeStruct((ng, nb, 1, DH), F32)],
        compiler_params=_cparams(("parallel", "parallel")),
    )(p, mu, dy)


def _sc_specs(t):
    return [pl.BlockSpec((None, t, DH), (lambda j, b, k=k: (G_SC + 4 * k + j, b, 0))) for k in range(4)]


def _sc_fwd(p, w, nb, t, name):
    def body(b_ref, c_ref, x_ref, z_ref, w_ref, y_ref):
        u = c_ref[...] * x_ref[...]
        wv = w_ref[...]
        conv = wv[2:3, :] * u + wv[1:2, :] * _shift_down(u, 1) + wv[0:1, :] * _shift_down(u, 2)
        y_ref[...] = (b_ref[...] * conv * _silu(z_ref[...])).astype(BF16)

    return _pcall(
        body, name=name, grid=(NH, nb),
        in_specs=_sc_specs(t) + [pl.BlockSpec((None, SC_TAPS, DH), lambda j, b: (j, 0, 0))],
        out_specs=pl.BlockSpec((None, t, DH), lambda j, b: (j, b, 0)),
        out_shape=jax.ShapeDtypeStruct((NH, nb * t, DH), BF16),
        compiler_params=_cparams(("parallel", "parallel")),
    )(p, p, p, p, w)


def _sc_bwd(p, w, dy, nb, t, name):
    def body(b_ref, c_ref, x_ref, z_ref, w_ref, dy_ref, db_ref, dc_ref, dx_ref, dz_ref, dw_ref):
        bg, cg, xg, z = b_ref[...], c_ref[...], x_ref[...], z_ref[...]
        wv = w_ref[...]
        d = dy_ref[...]
        u = cg * xg
        u1 = _shift_down(u, 1)
        u2 = _shift_down(u, 2)
        conv = wv[2:3, :] * u + wv[1:2, :] * u1 + wv[0:1, :] * u2
        sg = jax.nn.sigmoid(z)
        sz = z * sg
        db_ref[...] = (d * conv * sz).astype(BF16)
        dz_ref[...] = (d * bg * conv * (sg * (1.0 + z * (1.0 - sg)))).astype(BF16)
        dconv = d * bg * sz
        du = wv[2:3, :] * dconv + wv[1:2, :] * _shift_up(dconv, 1) + wv[0:1, :] * _shift_up(dconv, 2)
        dc_ref[...] = (du * xg).astype(BF16)
        dx_ref[...] = (du * cg).astype(BF16)
        dw_ref[2:3, :] = jnp.sum(dconv * u, 0, keepdims=True)
        dw_ref[1:2, :] = jnp.sum(dconv * u1, 0, keepdims=True)
        dw_ref[0:1, :] = jnp.sum(dconv * u2, 0, keepdims=True)

    n = nb * t
    out_specs = [pl.BlockSpec((None, t, DH), lambda j, b: (j, b, 0)) for _ in range(4)]
    res = _pcall(
        body, name=name, grid=(NH, nb),
        in_specs=_sc_specs(t) + [pl.BlockSpec((None, SC_TAPS, DH), lambda j, b: (j, 0, 0)),
                                 pl.BlockSpec((None, t, DH), lambda j, b: (j, b, 0))],
        out_specs=out_specs + [pl.BlockSpec((None, None, SC_TAPS, DH), lambda j, b: (j, b, 0, 0))],
        out_shape=[jax.ShapeDtypeStruct((NH, n, DH), BF16)] * 4
        + [jax.ShapeDtypeStruct((NH, nb, SC_TAPS, DH), F32)],
        compiler_params=_cparams(("parallel", "parallel")),
    )(p, p, p, p, w, dy)
    return res[:4], res[4]


def _row_tile(n):
    return 512 if n % 512 == 0 else n


def _norm_proj(x, pre_w, w_g, name):
    n = x.shape[0]
    tm = _row_tile(n)
    gs = GROUPS_PER_STEP

    def body(x_ref, pw_ref, w_ref, h_ref, p_ref):
        @pl.when(pl.program_id(1) == 0)
        def _():
            xv = x_ref[...]
            h = xv * lax.rsqrt(jnp.mean(xv * xv, -1, keepdims=True) + EPS) * pw_ref[...]
            h_ref[...] = h.astype(BF16)

        hb = h_ref[...]
        for k in range(gs):
            p_ref[k] = jnp.dot(hb, w_ref[k], preferred_element_type=F32)

    return _pcall(
        body, name=name, grid=(n // tm, N_GROUPS // gs),
        in_specs=[pl.BlockSpec((tm, D_MODEL), lambda i, j: (i, 0)),
                  pl.BlockSpec((1, D_MODEL), lambda i, j: (0, 0)),
                  pl.BlockSpec((gs, D_MODEL, DH), lambda i, j: (j, 0, 0))],
        out_specs=[pl.BlockSpec((tm, D_MODEL), lambda i, j: (i, 0)),
                   pl.BlockSpec((gs, tm, DH), lambda i, j: (j, i, 0))],
        out_shape=[jax.ShapeDtypeStruct((n, D_MODEL), BF16),
                   jax.ShapeDtypeStruct((N_GROUPS, n, DH), F32)],
        compiler_params=_cparams(("parallel", "arbitrary")),
    )(x, pre_w, w_g)


def _out_proj_norm(ys, wout_g, x, post_w, name):
    n = x.shape[0]
    tm = _row_tile(n)

    def body(y0, y1, y2, y3, w_ref, x_ref, pw_ref, out_ref, xn_ref):
        acc = jnp.zeros((tm, D_MODEL), F32)
        for m, yr in enumerate((y0, y1, y2, y3)):
            for h in range(NH):
                acc = acc + jnp.dot(yr[h], w_ref[m * NH + h], preferred_element_type=F32)
        out_ref[...] = acc
        xn_ref[...] = x_ref[...] + acc * lax.rsqrt(jnp.mean(acc * acc, -1, keepdims=True) + EPS) * pw_ref[...]

    yspec = pl.BlockSpec((NH, tm, DH), lambda i: (0, i, 0))
    rows = pl.BlockSpec((tm, D_MODEL), lambda i: (i, 0))
    return _pcall(
        body, name=name, grid=(n // tm,),
        in_specs=[yspec] * 4 + [pl.BlockSpec((4 * NH, DH, D_MODEL), lambda i: (0, 0, 0)), rows,
                                pl.BlockSpec((1, D_MODEL), lambda i: (0, 0))],
        out_specs=[rows, rows],
        out_shape=[jax.ShapeDtypeStruct((n, D_MODEL), F32)] * 2,
        compiler_params=_cparams(("parallel",)),
    )(*ys, wout_g, x, post_w)


def _loss_grad(x, tgt, name):
    n = x.shape[0]
    tm = _row_tile(n)

    def body(x_ref, t_ref, dx_ref, l_ref):
        @pl.when(pl.program_id(0) == 0)
        def _():
            l_ref[...] = jnp.zeros_like(l_ref)

        e = x_ref[...] - t_ref[...]
        dx_ref[...] = e * (1.0 / D_MODEL)
        l_ref[...] += jnp.sum(jnp.sum(e * e, -1, keepdims=True), 0, keepdims=True) * (0.5 / D_MODEL)

    rows = pl.BlockSpec((tm, D_MODEL), lambda i: (i, 0))
    return _pcall(
        body, name=name, grid=(n // tm,),
        in_specs=[rows, rows],
        out_specs=[rows, pl.BlockSpec((1, 128), lambda i: (0, 0))],
        out_shape=[jax.ShapeDtypeStruct((n, D_MODEL), F32), jax.ShapeDtypeStruct((1, 128), F32)],
        compiler_params=_cparams(("arbitrary",)),
    )(x, tgt)


def _rmsnorm_bwd(xv, w, d):
    r = lax.rsqrt(jnp.mean(xv * xv, -1, keepdims=True) + EPS)
    xh = xv * r
    dxh = d * w
    dx = r * (dxh - xh * jnp.mean(dxh * xh, -1, keepdims=True))
    return dx, d * xh


def _post_bwd(dxn, out, post_w, woutT_g, name):
    n = dxn.shape[0]
    tm = _row_tile(n)

    def body(d_ref, o_ref, pw_ref, w_ref, do_ref, dy_ref, dpw_ref):
        @pl.when(pl.program_id(0) == 0)
        def _():
            dpw_ref[...] = jnp.zeros_like(dpw_ref)

        dout, dw_rows = _rmsnorm_bwd(o_ref[...], pw_ref[...], d_ref[...])
        dpw_ref[...] += jnp.sum(dw_rows, 0, keepdims=True)
        db = dout.astype(BF16)
        do_ref[...] = db
        for g in range(4 * NH):
            dy_ref[g] = jnp.dot(db, w_ref[g], preferred_element_type=F32)

    rows = pl.BlockSpec((tm, D_MODEL), lambda i: (i, 0))
    vec = pl.BlockSpec((1, D_MODEL), lambda i: (0, 0))
    return _pcall(
        body, name=name, grid=(n // tm,),
        in_specs=[rows, rows, vec, pl.BlockSpec((4 * NH, D_MODEL, DH), lambda i: (0, 0, 0))],
        out_specs=[rows, pl.BlockSpec((4 * NH, tm, DH), lambda i: (0, i, 0)), vec],
        out_shape=[jax.ShapeDtypeStruct((n, D_MODEL), BF16),
                   jax.ShapeDtypeStruct((4 * NH, n, DH), F32),
                   jax.ShapeDtypeStruct((1, D_MODEL), F32)],
        compiler_params=_cparams(("arbitrary",)),
    )(dxn, out, post_w, woutT_g)


def _dwout(ys, dout, name):
    n = dout.shape[0]
    tm = _row_tile(n)

    def body(y0, y1, y2, y3, d_ref, dw_ref):
        @pl.when(pl.program_id(0) == 0)
        def _():
            dw_ref[...] = jnp.zeros_like(dw_ref)

        d = d_ref[...]
        for m, yr in enumerate((y0, y1, y2, y3)):
            for h in range(NH):
                dw_ref[m * NH + h] += lax.dot_general(yr[h], d, (((0,), (0,)), ((), ())),
                                                      preferred_element_type=F32)

    yspec = pl.BlockSpec((NH, tm, DH), lambda i: (0, i, 0))
    return _pcall(
        body, name=name, grid=(n // tm,),
        in_specs=[yspec] * 4 + [pl.BlockSpec((tm, D_MODEL), lambda i: (i, 0))],
        out_specs=pl.BlockSpec((4 * NH, DH, D_MODEL), lambda i: (0, 0, 0)),
        out_shape=jax.ShapeDtypeStruct((4 * NH, DH, D_MODEL), F32),
        compiler_params=_cparams(("arbitrary",)),
    )(*ys, dout)


def _dh_prenorm_bwd(dp, winT_g, x, pre_w, dxn, name):
    n = x.shape[0]
    tm = _row_tile(n)
    gs = GROUPS_PER_STEP
    nj = N_GROUPS // gs

    def body(dp_ref, w_ref, x_ref, pw_ref, d_ref, dx_ref, dpw_ref, acc):
        i, j = pl.program_id(0), pl.program_id(1)

        @pl.when((i == 0) & (j == 0))
        def _():
            dpw_ref[...] = jnp.zeros_like(dpw_ref)

        @pl.when(j == 0)
        def _():
            acc[...] = jnp.zeros_like(acc)

        a = acc[...]
        for k in range(gs):
            a = a + jnp.dot(dp_ref[k], w_ref[k], preferred_element_type=F32)
        acc[...] = a

        @pl.when(j == nj - 1)
        def _():
            dx, dw_rows = _rmsnorm_bwd(x_ref[...], pw_ref[...], acc[...])
            dx_ref[...] = d_ref[...] + dx
            dpw_ref[...] += jnp.sum(dw_rows, 0, keepdims=True)

    rows = pl.BlockSpec((tm, D_MODEL), lambda i, j: (i, 0))
    vec = pl.BlockSpec((1, D_MODEL), lambda i, j: (0, 0))
    return _pcall(
        body, name=name, grid=(n // tm, nj),
        in_specs=[pl.BlockSpec((gs, tm, DH), lambda i, j: (j, i, 0)),
                  pl.BlockSpec((gs, DH, D_MODEL), lambda i, j: (j, 0, 0)), rows, vec, rows],
        out_specs=[rows, vec],
        out_shape=[jax.ShapeDtypeStruct((n, D_MODEL), F32), jax.ShapeDtypeStruct((1, D_MODEL), F32)],
        scratch_shapes=[pltpu.VMEM((tm, D_MODEL), F32)],
        compiler_params=_cparams(("arbitrary", "arbitrary")),
    )(dp, winT_g, x, pre_w, dxn)


def _dwin(hb, dp, name):
    n = hb.shape[0]
    tm = _row_tile(n)
    gs = GROUPS_PER_STEP

    def body(h_ref, dp_ref, dw_ref):
        @pl.when(pl.program_id(1) == 0)
        def _():
            dw_ref[...] = jnp.zeros_like(dw_ref)

        h = h_ref[...]
        for k in range(gs):
            dw_ref[k] += lax.dot_general(h, dp_ref[k], (((0,), (0,)), ((), ())), preferred_element_type=F32)

    return _pcall(
        body, name=name, grid=(N_GROUPS // gs, n // tm),
        in_specs=[pl.BlockSpec((tm, D_MODEL), lambda j, i: (i, 0)),
                  pl.BlockSpec((gs, tm, DH), lambda j, i: (j, i, 0))],
        out_specs=pl.BlockSpec((gs, D_MODEL, DH), lambda j, i: (j, 0, 0)),
        out_shape=jax.ShapeDtypeStruct((N_GROUPS, D_MODEL, DH), F32),
        compiler_params=_cparams(("parallel", "arbitrary")),
    )(hb, dp)


def _adamw(w, g, m, v, name):
    r, c = w.shape
    tr = 256 if r % 256 == 0 else r
    c1 = 1.0 - ADAM_B1 ** ADAM_STEP
    c2 = 1.0 - ADAM_B2 ** ADAM_STEP

    def body(w_ref, g_ref, m_ref, v_ref, d_ref, nm_ref, nv_ref):
        gv = g_ref[...]
        nm = ADAM_B1 * m_ref[...] + (1.0 - ADAM_B1) * gv
        nv = ADAM_B2 * v_ref[...] + (1.0 - ADAM_B2) * (gv * gv)
        nm_ref[...] = nm
        nv_ref[...] = nv
        d_ref[...] = -ADAM_LR * ((nm / c1) / (jnp.sqrt(nv / c2) + ADAM_EPS) + ADAM_WD * w_ref[...])

    spec = pl.BlockSpec((tr, c), lambda i: (i, 0))
    return _pcall(
        body, name=name, grid=(r // tr,),
        in_specs=[spec] * 4, out_specs=[spec] * 3,
        out_shape=[jax.ShapeDtypeStruct((r, c), F32)] * 3,
        compiler_params=_cparams(("parallel",)),
    )(w, g, m, v)


def _me():
    return lax.axis_index("x"), lax.axis_index("y"), lax.axis_index("c")


def _flat(x, y, c):
    return 4 * x + 2 * y + c


def _peer(k):
    x, y, c = _me()
    return (x ^ ((k >> 2) & 1), y ^ ((k >> 1) & 1), c ^ (k & 1))


def _all_gather_hbm(blk, name):
    r = blk.shape[0]

    def body(x_ref, out_ref, send_sems, recv_sems, local_sem):
        me = _flat(*_me())
        mine = pltpu.make_async_copy(x_ref, out_ref.at[me], local_sem)
        mine.start()
        copies = []
        for k in range(1, N_DEV):
            cp = pltpu.make_async_remote_copy(
                src_ref=x_ref, dst_ref=out_ref.at[me],
                send_sem=send_sems.at[k - 1], recv_sem=recv_sems.at[k - 1],
                device_id=_peer(k), device_id_type=MESH)
            cp.start()
            copies.append(cp)
        for k in range(1, N_DEV):
            src = _flat(*_peer(k))
            pltpu.make_async_remote_copy(
                src_ref=x_ref, dst_ref=out_ref.at[src],
                send_sem=send_sems.at[k - 1], recv_sem=recv_sems.at[k - 1],
                device_id=_peer(k), device_id_type=MESH).wait_recv()
        for cp in copies:
            cp.wait_send()
        mine.wait()

    return _pcall(
        body, name=name,
        in_specs=[pl.BlockSpec(memory_space=pl.ANY)],
        out_specs=pl.BlockSpec(memory_space=pl.ANY),
        out_shape=jax.ShapeDtypeStruct((N_DEV, r, 128), F32),
        scratch_shapes=[pltpu.SemaphoreType.DMA((N_DEV - 1,)), pltpu.SemaphoreType.DMA((N_DEV - 1,)),
                        pltpu.SemaphoreType.DMA],
    )(blk)


def _exchange_hbm(send, name):
    r = send.shape[1]

    def body(s_ref, out_ref, send_sems, recv_sems, local_sem):
        me = _flat(*_me())
        mine = pltpu.make_async_copy(s_ref.at[me], out_ref.at[0], local_sem)
        mine.start()
        copies = []
        for k in range(1, N_DEV):
            cp = pltpu.make_async_remote_copy(
                src_ref=s_ref.at[_flat(*_peer(k))], dst_ref=out_ref.at[k],
                send_sem=send_sems.at[k - 1], recv_sem=recv_sems.at[k - 1],
                device_id=_peer(k), device_id_type=MESH)
            cp.start()
            copies.append(cp)
        for cp in copies:
            cp.wait_recv()
        for cp in copies:
            cp.wait_send()
        mine.wait()

    return _pcall(
        body, name=name,
        in_specs=[pl.BlockSpec(memory_space=pl.ANY)],
        out_specs=pl.BlockSpec(memory_space=pl.ANY),
        out_shape=jax.ShapeDtypeStruct((N_DEV, r, 128), F32),
        scratch_shapes=[pltpu.SemaphoreType.DMA((N_DEV - 1,)), pltpu.SemaphoreType.DMA((N_DEV - 1,)),
                        pltpu.SemaphoreType.DMA],
    )(send)


def _sum_slots(a, name):
    r = a.shape[1]
    tr = 8
    for cand in (1024, 512, 256, 128, 64, 32, 16, 8):
        if r % cand == 0:
            tr = cand
            break

    def body(a_ref, o_ref):
        acc = a_ref[0]
        for d in range(1, N_DEV):
            acc = acc + a_ref[d]
        o_ref[...] = acc

    return _pcall(
        body, name=name, grid=(r // tr,),
        in_specs=[pl.BlockSpec((N_DEV, tr, 128), lambda i: (0, i, 0))],
        out_specs=pl.BlockSpec((tr, 128), lambda i: (i, 0)),
        out_shape=jax.ShapeDtypeStruct((r, 128), F32),
        compiler_params=_cparams(("parallel",)),
    )(a)


def _all_reduce_small(blk, name):
    r = blk.shape[0]

    def body(x_ref, out_ref, gath, send_sems, recv_sems):
        me = _flat(*_me())
        gath[me] = x_ref[...]
        copies = []
        for k in range(1, N_DEV):
            cp = pltpu.make_async_remote_copy(
                src_ref=x_ref, dst_ref=gath.at[me],
                send_sem=send_sems.at[k - 1], recv_sem=recv_sems.at[k - 1],
                device_id=_peer(k), device_id_type=MESH)
            cp.start()
            copies.append(cp)
        for k in range(1, N_DEV):
            src = _flat(*_peer(k))
            pltpu.make_async_remote_copy(
                src_ref=x_ref, dst_ref=gath.at[src],
                send_sem=send_sems.at[k - 1], recv_sem=recv_sems.at[k - 1],
                device_id=_peer(k), device_id_type=MESH).wait_recv()
        for cp in copies:
            cp.wait_send()
        acc = gath[0]
        for d in range(1, N_DEV):
            acc = acc + gath[d]
        out_ref[...] = acc

    return _pcall(
        body, name=name,
        in_specs=[pl.BlockSpec(memory_space=pltpu.VMEM)],
        out_specs=pl.BlockSpec(memory_space=pltpu.VMEM),
        out_shape=jax.ShapeDtypeStruct((r, 128), F32),
        scratch_shapes=[pltpu.VMEM((N_DEV, r, 128), F32),
                        pltpu.SemaphoreType.DMA((N_DEV - 1,)), pltpu.SemaphoreType.DMA((N_DEV - 1,))],
    )(blk)


def _win_to_groups(w):
    d = w.shape[0]

    def pad(a, width):
        return jnp.pad(a, ((0, 0), (0, 0), (0, DH - width)))

    parts = [
        w[:, C_GDN:C_GDN + 1024].reshape(d, 16, DH),
        pad(w[:, C_GDN + 1024:C_GDN + 1032].reshape(d, 1, 8), 8),
        w[:, C_RWKV:C_RWKV + 1152].reshape(d, 18, DH),
        w[:, C_SC:C_SC + 1024].reshape(d, 16, DH),
        pad(w[:, C_GLA:C_GLA + 256].reshape(d, 8, GLA_HEAD_K), GLA_HEAD_K),
        w[:, C_GLA + 256:C_GLA + 768].reshape(d, 8, DH),
        pad(w[:, C_GLA + 768:C_GLA + 784].reshape(d, 1, 16), 16),
    ]
    return jnp.transpose(jnp.concatenate(parts, axis=1), (1, 0, 2))


def _groups_to_win(g):
    g = jnp.transpose(g, (1, 0, 2))
    d = g.shape[0]
    return jnp.concatenate([
        g[:, 0:16].reshape(d, 1024), g[:, 16, :8],
        g[:, 17:35].reshape(d, 1152),
        g[:, 35:51].reshape(d, 1024),
        g[:, 51:59, :GLA_HEAD_K].reshape(d, 256), g[:, 59:67].reshape(d, 512), g[:, 67, :16],
    ], axis=1)


def _heads(vec):
    return vec.reshape(NH, 1, DH)


def _rep(vec4):
    return jnp.broadcast_to(vec4.reshape(NH, 1, 1), (NH, 1, DH))


def _onehot_lane(offset):
    m = np.zeros((NH, 1, DH), np.float32)
    for h in range(NH):
        m[h, 0, offset + h] = 1.0
    return jnp.asarray(m)


def _sum_b(a, nb):
    return a.reshape(nb, NH, a.shape[1], DH).sum(0)


_PACK_SIZES = (("w_in", DEPTH * D_MODEL * 499), ("w_out", DEPTH * 128 * D_MODEL), ("gdn_conv_w", DEPTH * 4 * 96),
               ("rwkv_w_up", DEPTH * 64 * 32), ("rwkv_a_up", DEPTH * 64 * 32), ("sc_conv_w", 256))
_PACK_ROWS = sum(s for _, s in _PACK_SIZES) // 128


def _pack_shard(w_in, w_out, gdn_conv_w, rwkv_w_up, rwkv_a_up, sc_conv_w):
    sc = jnp.pad(sc_conv_w.reshape(-1), (0, 256 - DEPTH * 3 * 32))
    flat = jnp.concatenate([w_in.reshape(-1), w_out.reshape(-1), gdn_conv_w.reshape(-1),
                            rwkv_w_up.reshape(-1), rwkv_a_up.reshape(-1), sc])
    return flat.reshape(_PACK_ROWS, 128)


def _unpack_shards(p):
    p = p.reshape(p.shape[0], -1)
    out, o = {}, 0
    shapes = {"w_in": (DEPTH, D_MODEL, 499), "w_out": (DEPTH, 128, D_MODEL), "gdn_conv_w": (DEPTH, 4, 96),
              "rwkv_w_up": (DEPTH, 64, 32), "rwkv_a_up": (DEPTH, 64, 32), "sc_conv_w": (DEPTH, 3, 32)}
    for name, size in _PACK_SIZES:
        shp = shapes[name]
        used = int(np.prod(shp))
        out[name] = p[:, o:o + used].reshape((p.shape[0],) + shp)
        o += size
    return out


def _gather_last(a):
    return jnp.transpose(a, (1, 2, 0, 3)).reshape(a.shape[1], a.shape[2], -1)


def _split_last(a):
    l, r, c8 = a.shape
    return jnp.transpose(a.reshape(l, r, N_DEV, c8 // N_DEV), (2, 0, 1, 3))


_SMALL = (("pre_norm_w", (DEPTH, 1024)), ("gdn_a_log", (DEPTH, 4)), ("gdn_dt_bias", (DEPTH, 4)),
          ("gdn_norm_w", (DEPTH, 64)), ("rwkv_mu", (DEPTH, 1152)), ("rwkv_w0", (DEPTH, 256)),
          ("rwkv_a0", (DEPTH, 256)), ("rwkv_k_k", (DEPTH, 256)), ("rwkv_k_a", (DEPTH, 256)),
          ("rwkv_r_k", (DEPTH, 256)), ("rwkv_ln_w", (DEPTH, 256)), ("rwkv_ln_b", (DEPTH, 256)),
          ("gla_a_up", (DEPTH, 16, 128)), ("gla_a_bias", (DEPTH, 128)), ("gla_norm_w", (DEPTH, 64)),
          ("post_norm_w", (DEPTH, 1024)))
_SMALL_TOTAL = sum(int(np.prod(s)) for _, s in _SMALL)
_SMALL_ROWS = -(-(_SMALL_TOTAL + 1) // 1024) * 8


def _pack_small(d, extra):
    flat = jnp.concatenate([d[n].reshape(-1) for n, _ in _SMALL] + [extra.reshape(-1)])
    return jnp.pad(flat, (0, _SMALL_ROWS * 128 - flat.shape[0])).reshape(_SMALL_ROWS, 128)


def _unpack_small(p):
    flat = p.reshape(-1)
    out, o = {}, 0
    for n, s in _SMALL:
        size = int(np.prod(s))
        out[n] = flat[o:o + size].reshape(s)
        o += size
    return out, flat[o]


def _layer_params(wts, l):
    conv = wts["gdn_conv_w"][l]
    q = {}
    q["gdn_conv"] = jnp.transpose(conv.reshape(GDN_TAPS, 12, DH), (1, 0, 2))
    q["gdn_prm"] = [_rep(wts["gdn_a_log"][l]), _rep(wts["gdn_dt_bias"][l]),
                    jnp.broadcast_to(wts["gdn_norm_w"][l].reshape(1, 1, DH), (NH, 1, DH))]
    q["gdn_cst"] = [_onehot_lane(0), _onehot_lane(NH)]
    q["rwkv_mu"] = wts["rwkv_mu"][l].reshape(18, 1, DH)
    w_up = jnp.transpose(wts["rwkv_w_up"][l].reshape(64, NH, DH), (1, 0, 2))
    a_up = jnp.transpose(wts["rwkv_a_up"][l].reshape(64, NH, DH), (1, 0, 2))
    q["rwkv_prm"] = [_heads(wts["rwkv_w0"][l]), w_up, _heads(wts["rwkv_a0"][l]), a_up,
                     _heads(wts["rwkv_k_k"][l]), _heads(wts["rwkv_k_a"][l]), _heads(wts["rwkv_r_k"][l]),
                     _heads(wts["rwkv_ln_w"][l]), _heads(wts["rwkv_ln_b"][l])]
    q["sc_conv"] = jnp.transpose(wts["sc_conv_w"][l].reshape(SC_TAPS, NH, DH), (1, 0, 2))
    gla_up = jnp.transpose(wts["gla_a_up"][l].reshape(16, NH, GLA_HEAD_K), (1, 0, 2))
    gla_up = jnp.pad(gla_up, ((0, 0), (0, DH - 16), (0, DH - GLA_HEAD_K)))
    gla_b = jnp.pad(wts["gla_a_bias"][l].reshape(NH, 1, GLA_HEAD_K), ((0, 0), (0, 0), (0, DH - GLA_HEAD_K)))
    q["gla_prm"] = [gla_up, gla_b, jnp.broadcast_to(wts["gla_norm_w"][l].reshape(1, 1, DH), (NH, 1, DH))]
    wg = _win_to_groups(wts["w_in"][l])
    q["w_g"] = wg.astype(BF16)
    q["wT_g"] = jnp.transpose(wg, (0, 2, 1)).astype(BF16)
    wo = wts["w_out"][l]
    q["wout_g"] = wo.reshape(4 * NH, DH, D_MODEL).astype(BF16)
    q["woutT_g"] = jnp.transpose(wo.reshape(4 * NH, DH, D_MODEL), (0, 2, 1)).astype(BF16)
    q["pre_w"] = wts["pre_norm_w"][l].reshape(1, D_MODEL)
    q["post_w"] = wts["post_norm_w"][l].reshape(1, D_MODEL)
    return q


def _mixer_inputs(p, cq, pm):
    gdn = [(cq, 0, 1), (cq, 4, 1), (cq, 8, 1), (p, G_GDN_Z, 1), (p, G_GDN_AB, 0)]
    rwkv = [(pm, 0, 1), (pm, 4, 1), (pm, 8, 1), (pm, 12, 1), (pm, 16, 0), (pm, 17, 0)]
    gla = [(p, G_GLA_Q, 1), (p, G_GLA_K, 1), (p, G_GLA_V, 1), (p, G_GLA_Z, 1), (p, G_GLA_AD, 0)]
    return gdn, rwkv, gla


def _layer_fwd(x, q, nb, t, l):
    hb, p = _norm_proj(x, q["pre_w"], q["w_g"], f"norm_proj{l}")
    cq = _conv_fwd(p, G_GDN_Q, 12, q["gdn_conv"], nb, t, f"gdn_conv{l}")
    pm = _mix_fwd(p, G_RWKV, 18, q["rwkv_mu"], nb, t, f"rwkv_mix{l}")
    gdn_in, rwkv_in, gla_in = _mixer_inputs(p, cq, pm)
    y_gdn, ck_gdn = _mixer_fwd(_gdn_chunk, f"gdn_fwd{l}", gdn_in, q["gdn_prm"], q["gdn_cst"], nb, t)
    y_rwkv, ck_rwkv = _mixer_fwd(_rwkv_chunk, f"rwkv_fwd{l}", rwkv_in, q["rwkv_prm"], [], nb, t,
                                 first_fn=_rwkv_chunk_steps)
    y_sc = _sc_fwd(p, q["sc_conv"], nb, t, f"sc_fwd{l}")
    y_gla, ck_gla = _mixer_fwd(_gla_chunk, f"gla_fwd{l}", gla_in, q["gla_prm"], [], nb, t)
    ys = (y_gdn, y_rwkv, y_sc, y_gla)
    out, xn = _out_proj_norm(ys, q["wout_g"], x, q["post_w"], f"out_proj{l}")
    saved = dict(x=x, hb=hb, p=p, cq=cq, pm=pm, ys=ys, out=out, ck=(ck_gdn, ck_rwkv, ck_gla))
    return xn, saved


def _layer_bwd(dxn, q, sv, nb, t, l):
    p, cq, pm, ys = sv["p"], sv["cq"], sv["pm"], sv["ys"]
    dout, dy, d_post = _post_bwd(dxn, sv["out"], q["post_w"], q["woutT_g"], f"post_bwd{l}")
    d_wout = _dwout(ys, dout, f"dwout{l}").reshape(D_MODEL, D_MODEL)
    gdn_in, rwkv_in, gla_in = _mixer_inputs(p, cq, pm)
    ck_gdn, ck_rwkv, ck_gla = sv["ck"]
    dy_m = [lax.slice_in_dim(dy, 4 * m, 4 * m + 4, axis=0) for m in range(4)]

    (dcq, dck, dcv, dz, dab), (da_log, ddt, dnw) = _mixer_bwd(
        _gdn_chunk, f"gdn_bwd{l}", gdn_in, q["gdn_prm"], q["gdn_cst"], ck_gdn, dy_m[0],
        (F32, F32, F32, BF16, F32), nb, t)
    dconv_in, d_gconv = _conv_bwd(p, G_GDN_Q, 12, q["gdn_conv"], jnp.concatenate([dcq, dck, dcv], 0),
                                  nb, t, f"gdn_conv_bwd{l}")
    dp_gdn = [dconv_in, dz, dab.sum(0, keepdims=True).astype(BF16)]
    g = {}
    g["gdn_conv_w"] = jnp.transpose(d_gconv.sum(1), (1, 0, 2)).reshape(GDN_TAPS, 768)
    g["gdn_a_log"] = _sum_b(da_log, nb).sum((1, 2))
    g["gdn_dt_bias"] = _sum_b(ddt, nb).sum((1, 2))
    g["gdn_norm_w"] = dnw.sum((0, 1))

    d_r, d_rprm = _mixer_bwd(_rwkv_chunk, f"rwkv_bwd{l}", rwkv_in, q["rwkv_prm"], [], ck_rwkv, dy_m[1],
                             (F32,) * 6, nb, t, first_fn=_rwkv_chunk_steps)
    dpm = jnp.concatenate(list(d_r[:4]) + [d_r[4].sum(0, keepdims=True), d_r[5].sum(0, keepdims=True)], 0)
    dp_rwkv, d_mu = _mix_bwd(p, G_RWKV, 18, q["rwkv_mu"], dpm, nb, t, f"rwkv_mix_bwd{l}")
    g["rwkv_mu"] = d_mu.sum(1).reshape(1152)
    rp = [_sum_b(a, nb) for a in d_rprm]
    g["rwkv_w0"] = rp[0].reshape(256)
    g["rwkv_w_up"] = jnp.transpose(rp[1], (1, 0, 2)).reshape(64, 256)
    g["rwkv_a0"] = rp[2].reshape(256)
    g["rwkv_a_up"] = jnp.transpose(rp[3], (1, 0, 2)).reshape(64, 256)
    for i, nme in enumerate(("rwkv_k_k", "rwkv_k_a", "rwkv_r_k", "rwkv_ln_w", "rwkv_ln_b")):
        g[nme] = rp[4 + i].reshape(256)

    dp_sc, d_scw = _sc_bwd(p, q["sc_conv"], dy_m[2], nb, t, f"sc_bwd{l}")
    g["sc_conv_w"] = jnp.transpose(d_scw.sum(1), (1, 0, 2)).reshape(SC_TAPS, 256)

    (dq, dk, dv, dzg, dad), (d_aup, d_ab, d_gnw) = _mixer_bwd(
        _gla_chunk, f"gla_bwd{l}", gla_in, q["gla_prm"], [], ck_gla, dy_m[3],
        (BF16, BF16, BF16, BF16, F32), nb, t)
    dp_gla = [dq, dk, dv, dzg, dad.sum(0, keepdims=True).astype(BF16)]
    g["gla_a_up"] = jnp.transpose(_sum_b(d_aup, nb)[:, :16, :GLA_HEAD_K], (1, 0, 2)).reshape(16, 128)
    g["gla_a_bias"] = _sum_b(d_ab, nb)[:, 0, :GLA_HEAD_K].reshape(128)
    g["gla_norm_w"] = d_gnw.sum((0, 1))

    dp = jnp.concatenate(dp_gdn + [dp_rwkv] + list(dp_sc) + dp_gla, axis=0)
    dx, d_pre = _dh_prenorm_bwd(dp, q["wT_g"], sv["x"], q["pre_w"], dxn, f"dh_bwd{l}")
    g["w_in"] = _groups_to_win(_dwin(sv["hb"], dp, f"dwin{l}"))
    g["w_out"] = d_wout
    g["pre_norm_w"] = d_pre.reshape(D_MODEL)
    g["post_norm_w"] = d_post.reshape(D_MODEL)
    return dx, g


def _local_step(x, tgt, wts):
    nb, t, d = x.shape
    xf = x.reshape(nb * t, d)
    qs, saved = [], []
    for l in range(DEPTH):
        q = _layer_params(wts, l)
        xf, sv = _layer_fwd(xf, q, nb, t, l)
        qs.append(q)
        saved.append(sv)
    dxf, lpart = _loss_grad(xf, tgt.reshape(nb * t, d), "loss")
    grads = [None] * DEPTH
    for l in reversed(range(DEPTH)):
        dxf, grads[l] = _layer_bwd(dxf, qs[l], saved[l], nb, t, l)
    full = {k: jnp.stack([grads[l][k] for l in range(DEPTH)]) for k in grads[0]}
    return lpart[0, 0], dxf.reshape(nb, t, d), full


_SHARDED = ("w_in", "w_out", "gdn_conv_w", "rwkv_w_up", "rwkv_a_up", "sc_conv_w")
_WEIGHTS = ("pre_norm_w", "w_in", "gdn_conv_w", "gdn_a_log", "gdn_dt_bias", "gdn_norm_w", "rwkv_mu", "rwkv_w0",
            "rwkv_w_up", "rwkv_a0", "rwkv_a_up", "rwkv_k_k", "rwkv_k_a", "rwkv_r_k", "rwkv_ln_w", "rwkv_ln_b",
            "sc_conv_w", "gla_a_up", "gla_a_bias", "gla_norm_w", "w_out", "post_norm_w")


def _adamw_nd(w, g, m, v, name):
    shp = w.shape
    c = shp[-1]
    two = lambda a: a.reshape(-1, c)
    return tuple(o.reshape(shp) for o in _adamw(two(w), two(g), two(m), two(v), name))


def kernel(x, pre_norm_w, w_in, gdn_conv_w, gdn_a_log, gdn_dt_bias, gdn_norm_w, rwkv_mu, rwkv_w0, rwkv_w_up, rwkv_a0, rwkv_a_up, rwkv_k_k, rwkv_k_a, rwkv_r_k, rwkv_ln_w, rwkv_ln_b, sc_conv_w, gla_a_up, gla_a_bias, gla_norm_w, w_out, post_norm_w, loss_target, m_pre_norm_w, m_w_in, m_gdn_conv_w, m_gdn_a_log, m_gdn_dt_bias, m_gdn_norm_w, m_rwkv_mu, m_rwkv_w0, m_rwkv_w_up, m_rwkv_a0, m_rwkv_a_up, m_rwkv_k_k, m_rwkv_k_a, m_rwkv_r_k, m_rwkv_ln_w, m_rwkv_ln_b, m_sc_conv_w, m_gla_a_up, m_gla_a_bias, m_gla_norm_w, m_w_out, m_post_norm_w, v_pre_norm_w, v_w_in, v_gdn_conv_w, v_gdn_a_log, v_gdn_dt_bias, v_gdn_norm_w, v_rwkv_mu, v_rwkv_w0, v_rwkv_w_up, v_rwkv_a0, v_rwkv_a_up, v_rwkv_k_k, v_rwkv_k_a, v_rwkv_r_k, v_rwkv_ln_w, v_rwkv_ln_b, v_sc_conv_w, v_gla_a_up, v_gla_a_bias, v_gla_norm_w, v_w_out, v_post_norm_w):
    env = dict(locals())
    w = {n: env[n] for n in _WEIGHTS}
    m = {n: env["m_" + n] for n in _WEIGHTS}
    v = {n: env["v_" + n] for n in _WEIGHTS}

    gathered = _unpack_shards(_all_gather_hbm(_pack_shard(*[w[n] for n in _SHARDED]), "gather_weights"))
    full = dict(w)
    for n in ("w_in", "gdn_conv_w", "rwkv_w_up", "rwkv_a_up", "sc_conv_w"):
        full[n] = _gather_last(gathered[n])
    full["w_out"] = jnp.transpose(gathered["w_out"], (1, 0, 2, 3)).reshape(DEPTH, D_MODEL, D_MODEL)

    lpart, grad_x, g = _local_step(x, loss_target, full)

    gs = {n: _split_last(g[n]) for n in ("w_in", "gdn_conv_w", "rwkv_w_up", "rwkv_a_up", "sc_conv_w")}
    gs["w_out"] = jnp.transpose(g["w_out"].reshape(DEPTH, N_DEV, 128, D_MODEL), (1, 0, 2, 3))
    send = jax.vmap(_pack_shard)(*[gs[n] for n in _SHARDED])
    mine = _unpack_shards(_sum_slots(_exchange_hbm(send, "scatter_grads"), "sum_grads")[None])
    grads = {n: mine[n][0] for n in _SHARDED}

    small, loss = _unpack_small(_all_reduce_small(_pack_small(g, lpart), "reduce_small"))
    grads.update(small)

    delta, new_m, new_v = {}, {}, {}
    for n in ("w_in", "w_out"):
        delta[n], new_m[n], new_v[n] = _adamw_nd(w[n], grads[n], m[n], v[n], "adamw_" + n)
    rest = [n for n in _WEIGHTS if n not in ("w_in", "w_out")]

    def pack_rest(d):
        flat = jnp.concatenate([d[n].reshape(-1) for n in rest])
        rows = -(-flat.shape[0] // 1024) * 8
        return jnp.pad(flat, (0, rows * 128 - flat.shape[0]), constant_values=1.0).reshape(rows, 128)

    outs = _adamw(pack_rest(w), pack_rest(grads), pack_rest(m), pack_rest(v), "adamw_rest")
    for dst, packed in zip((delta, new_m, new_v), outs):
        flat, o = packed.reshape(-1), 0
        for n in rest:
            size = int(np.prod(w[n].shape))
            dst[n] = flat[o:o + size].reshape(w[n].shape)
            o += size

    return (loss, grad_x, *[grads[n] for n in _WEIGHTS], *[delta[n] for n in _WEIGHTS],
            *[new_m[n] for n in _WEIGHTS], *[new_v[n] for n in _WEIGHTS])
```

```python
import functools
import math

import numpy as np
import jax
import jax.numpy as jnp
from jax import lax
from jax.experimental import pallas as pl
from jax.experimental.pallas import tpu as pltpu

F32 = jnp.float32
BF16 = jnp.bfloat16
HI = lax.Precision.HIGHEST

D_MODEL = 1024
DEPTH = 2
NH = 4
DH = 64
CH = 64
EPS = 1e-6
RWKV_GN_EPS = 64e-5
GLA_HEAD_K = 32
GLA_TAU = 16.0
GDN_TAPS = 4
SC_TAPS = 3
D_IN = 3992
N_DEV = 8
SHARD_COLS = D_IN // N_DEV

G_GDN = 0
G_RWKV = 16
G_SC = 32
G_GLA = 48
G_GDN_AB, G_RWKV_WD, G_RWKV_AD, G_GLA_AD = 64, 65, 66, 67
N_GROUPS = 68
GROUPS_PER_STEP = 4
TIME_BLOCK = 512

C_GDN, C_RWKV, C_SC, C_GLA = 0, 1032, 2184, 3208

ADAM_LR, ADAM_B1, ADAM_B2, ADAM_EPS, ADAM_WD, ADAM_STEP = 0.001, 0.9, 0.999, 1e-08, 0.01, 10

VMEM_LIMIT = 56 * 1024 * 1024
MESH = pl.DeviceIdType.MESH

_pcall = pl.pallas_call


def _cparams(sem=None):
    if sem is None:
        return pltpu.CompilerParams(vmem_limit_bytes=VMEM_LIMIT)
    return pltpu.CompilerParams(dimension_semantics=sem, vmem_limit_bytes=VMEM_LIMIT)


def _group_segments():
    table = [(G_GDN + i, C_GDN + DH * i, DH) for i in range(16)]
    table.append((G_GDN_AB, C_GDN + 1024, 8))
    table += [(G_RWKV + i, C_RWKV + DH * i, DH) for i in range(16)]
    table += [(G_RWKV_WD, C_RWKV + 1024, DH), (G_RWKV_AD, C_RWKV + 1088, DH)]
    table += [(G_SC + 4 * j + k, C_SC + 256 * k + DH * j, DH) for j in range(NH) for k in range(4)]
    for h in range(NH):
        table += [(G_GLA + 4 * h, C_GLA + GLA_HEAD_K * h, GLA_HEAD_K),
                  (G_GLA + 4 * h + 1, C_GLA + 128 + GLA_HEAD_K * h, GLA_HEAD_K),
                  (G_GLA + 4 * h + 2, C_GLA + 256 + DH * h, DH),
                  (G_GLA + 4 * h + 3, C_GLA + 512 + DH * h, DH)]
    table.append((G_GLA_AD, C_GLA + 768, 16))
    segs, padded = [], []
    for g, c, n in table:
        if n < DH:
            padded.append(g)
        a = 0
        while n > 0:
            d, off = divmod(c, SHARD_COLS)
            ln = min(n, SHARD_COLS - off)
            segs.append((g, a, d, off, ln))
            c, a, n = c + ln, a + ln, n - ln
    return segs, padded


_SEGMENTS, _PADDED_GROUPS = _group_segments()


def _dot(a, b):
    return jnp.dot(a, b, precision=HI, preferred_element_type=F32)


def _dot_nt(a, b):
    return lax.dot_general(a, b, (((1,), (1,)), ((), ())), precision=HI, preferred_element_type=F32)


def _dot_tn(a, b):
    return lax.dot_general(a, b, (((0,), (0,)), ((), ())), precision=HI, preferred_element_type=F32)


def _r(x):
    return x.astype(BF16)


@jax.custom_vjp
def _bmm(a, b):
    return jnp.dot(_r(a), _r(b), preferred_element_type=F32)


def _bmm_fwd(a, b):
    return _bmm(a, b), (a, b)


def _bmm_bwd(res, g):
    a, b = res
    return (lax.dot_general(_r(g), _r(b), (((1,), (1,)), ((), ())), preferred_element_type=F32),
            lax.dot_general(_r(a), _r(g), (((0,), (0,)), ((), ())), preferred_element_type=F32))


_bmm.defvjp(_bmm_fwd, _bmm_bwd)


@jax.custom_vjp
def _bmm_nt(a, b):
    return lax.dot_general(_r(a), _r(b), (((1,), (1,)), ((), ())), preferred_element_type=F32)


def _bmm_nt_fwd(a, b):
    return _bmm_nt(a, b), (a, b)


def _bmm_nt_bwd(res, g):
    a, b = res
    return (jnp.dot(_r(g), _r(b), preferred_element_type=F32),
            lax.dot_general(_r(g), _r(a), (((0,), (0,)), ((), ())), preferred_element_type=F32))


_bmm_nt.defvjp(_bmm_nt_fwd, _bmm_nt_bwd)


@jax.custom_vjp
def _bmm_tn(a, b):
    return lax.dot_general(_r(a), _r(b), (((0,), (0,)), ((), ())), preferred_element_type=F32)


def _bmm_tn_fwd(a, b):
    return _bmm_tn(a, b), (a, b)


def _bmm_tn_bwd(res, g):
    a, b = res
    return (lax.dot_general(_r(b), _r(g), (((1,), (1,)), ((), ())), preferred_element_type=F32),
            jnp.dot(_r(a), _r(g), preferred_element_type=F32))


_bmm_tn.defvjp(_bmm_tn_fwd, _bmm_tn_bwd)


def _tri(n):
    i = lax.broadcasted_iota(jnp.int32, (n, n), 0)
    j = lax.broadcasted_iota(jnp.int32, (n, n), 1)
    return i >= j, i > j, i == j


def _inv_unit_lower(a, n):
    _, _, eye = _tri(n)
    pw = -a
    inv = eye.astype(F32) + pw
    for _ in range(int(math.log2(n)) - 1):
        pw = _dot(pw, pw)
        inv = inv + _dot(inv, pw)
    return inv


def _silu(x):
    return x * jax.nn.sigmoid(x)


def _gdn_chunk(prm, cst, ins, s):
    a_log, dt_b, nw = prm
    m_a, m_b = cst
    cq, ck, cv, z, ab = ins
    incl, strict, _ = _tri(CH)
    q = _silu(cq)
    k = _silu(ck)
    v = _silu(cv)
    q = q * lax.rsqrt(jnp.sum(q * q, -1, keepdims=True) + EPS) * (DH ** -0.5)
    k = k * lax.rsqrt(jnp.sum(k * k, -1, keepdims=True) + EPS)
    a_raw = jnp.sum(ab * m_a, -1, keepdims=True)
    b_raw = jnp.sum(ab * m_b, -1, keepdims=True)
    gstep = -jnp.exp(a_log) * jax.nn.softplus(a_raw + dt_b)
    beta = jax.nn.sigmoid(b_raw)
    gc = _dot(incl.astype(F32), gstep)
    gl = jnp.sum(gstep, 0, keepdims=True)
    dec = jnp.where(incl, jnp.exp(jnp.where(incl, gc - gc.T, 0.0)), 0.0)
    kb = k * beta
    a_mat = jnp.where(strict, _bmm_nt(kb, k) * dec, 0.0)
    tinv = _inv_unit_lower(a_mat, CH)
    eg = jnp.exp(gc)
    u = _dot(tinv, v * beta)
    w = _dot(tinv, kb * eg)
    attn = _bmm_nt(q, k) * dec
    v_new = u - _bmm(w, s)
    o = _bmm(q * eg, s) + _bmm(attn, v_new)
    s_next = s * jnp.exp(gl) + _bmm_tn(k * jnp.exp(gl - gc), v_new)
    on = o * lax.rsqrt(jnp.mean(o * o, -1, keepdims=True) + EPS) * nw
    return on * _silu(z), s_next


def _gla_chunk(prm, cst, ins, st):
    a_up, a_bias, nw = prm
    q, k, v, z, ad = ins
    incl, _, _ = _tri(CH)
    la = jax.nn.log_sigmoid(_bmm(ad, a_up) + a_bias) * (1.0 / GLA_TAU)
    bc = _dot(incl.astype(F32), la)
    bl = jnp.sum(la, 0, keepdims=True)
    qe = q * (GLA_HEAD_K ** -0.5) * jnp.exp(bc)
    ke = k * jnp.exp(-bc)
    attn = jnp.where(incl, _bmm_nt(qe, ke), 0.0)
    o = _bmm_nt(qe, st) + _bmm(attn, v)
    st_next = st * jnp.exp(bl) + _bmm_tn(v, k * jnp.exp(bl - bc))
    on = o * lax.rsqrt(jnp.mean(o * o, -1, keepdims=True) + EPS) * nw
    return on * _silu(z), st_next


def _rwkv_chunk(prm, cst, ins, s):
    r, v = ins[0], ins[2]
    incl, strict, _ = _tri(CH)
    lw, kk, k2, m = _rwkv_pre(prm, ins)
    cum = _dot(incl.astype(F32), lw)
    ltot = jnp.sum(lw, 0, keepdims=True)
    n_t = -kk * jnp.exp(cum - lw)
    einv = jnp.exp(-cum)
    m_t = m * einv
    k_t = k2 * einv
    r_t = r * jnp.exp(cum)
    a_nm = jnp.where(strict, _dot_nt(n_t, m_t), 0.0)
    a_nk = jnp.where(strict, _dot_nt(n_t, k_t), 0.0)
    cm = _dot(_inv_unit_lower(-a_nm, CH), _dot_nt(n_t, s) + _dot(a_nk, v))
    y = (_dot_nt(r_t, s) + _dot(jnp.where(incl, _dot_nt(r_t, m_t), 0.0), cm)
         + _dot(jnp.where(incl, _dot_nt(r_t, k_t), 0.0), v))
    eend = jnp.exp(ltot - cum)
    s_next = s * jnp.exp(ltot) + _dot_tn(cm, m * eend) + _dot_tn(v, k2 * eend)
    return _rwkv_post(prm, ins, y, k2), s_next


def _rwkv_pre(prm, ins):
    w0, w_up, a0, a_up, k_k, k_a = prm[:6]
    k, wd, ad = ins[1], ins[4], ins[5]
    lw = -math.exp(-0.5) * jax.nn.sigmoid(w0 + _bmm(jnp.tanh(wd), w_up))
    a = jax.nn.sigmoid(a0 + _bmm(ad, a_up))
    kk = k * k_k
    kk = kk * lax.rsqrt(jnp.sum(kk * kk, -1, keepdims=True) + EPS)
    k2 = k * (1.0 + (a - 1.0) * k_a)
    return lw, kk, k2, kk * a


def _rwkv_post(prm, ins, y, k2):
    r_k, ln_w, ln_b = prm[6:]
    r, v, z = ins[0], ins[2], ins[3]
    mean = jnp.mean(y, -1, keepdims=True)
    yc = y - mean
    var = jnp.mean(yc * yc, -1, keepdims=True)
    yn = yc * lax.rsqrt(var + RWKV_GN_EPS) * ln_w + ln_b
    bonus = jnp.sum(r * k2 * r_k, -1, keepdims=True) * v
    return (yn + bonus) * _silu(z)


@jax.custom_vjp
def _bmv(s, x):
    return jnp.sum(_r(s).astype(F32) * _r(x).astype(F32), -1, keepdims=True)


def _bmv_fwd(s, x):
    return _bmv(s, x), (s, x)


def _bmv_bwd(res, g):
    s, x = res
    return g * x, jnp.sum(_r(s).astype(F32) * _r(g).astype(F32), 0, keepdims=True)


_bmv.defvjp(_bmv_fwd, _bmv_bwd)


def _rwkv_chunk_steps(prm, cst, ins, s):
    r, v = ins[0], ins[2]
    lw, kk, k2, m = _rwkv_pre(prm, ins)
    w = jnp.exp(lw)
    v_t = v.T
    lane = lax.broadcasted_iota(jnp.int32, (1, CH), 1)
    y_t = jnp.zeros((DH, CH), F32)
    for t in range(CH):
        e_t = (lane == t).astype(F32)
        row = slice(t, t + 1)
        sa = _bmv(s, -kk[row])
        s = s * w[row] + sa * m[row] + jnp.sum(v_t * e_t, -1, keepdims=True) * k2[row]
        y_t = y_t + _bmv(s, r[row]) * e_t
    return _rwkv_post(prm, ins, y_t.T, k2), s


def _time_block(t):
    return TIME_BLOCK if t % TIME_BLOCK == 0 else t


def _mixer_fwd(chunk_fn, name, ins, prm, cst, nb, t, first_fn=None):
    tb = _time_block(t)
    nt, ncb = t // tb, tb // CH
    n_in, n_prm, n_cst = len(ins), len(prm), len(cst)
    sels = [sel for _, _, _, sel in ins]

    def body(*refs):
        in_refs = refs[:n_in]
        prm_refs = refs[n_in:n_in + n_prm]
        cst_refs = refs[n_in + n_prm:n_in + n_prm + n_cst]
        y_ref, ck_ref, s_scr = refs[n_in + n_prm + n_cst:]
        step_t = pl.program_id(1)

        @pl.when(step_t == 0)
        def _():
            s_scr[...] = jnp.zeros_like(s_scr)

        def chunk(c, i, fn):
            for h in range(NH):
                s = s_scr[h]
                ck_ref[c, h] = s
                ins_c = [r[sel(h), pl.ds(i, CH), :] for r, sel in zip(in_refs, sels)]
                y, s_next = fn([r[h] for r in prm_refs], [r[h] for r in cst_refs], ins_c, s)
                y_ref[h, pl.ds(i, CH), :] = y.astype(BF16)
                s_scr[h] = s_next

        def step(c, carry):
            chunk(c, pl.multiple_of(c * CH, CH), chunk_fn)
            return carry

        if first_fn is None:
            lax.fori_loop(0, ncb, step, 0)
        else:
            @pl.when(step_t == 0)
            def _():
                chunk(0, 0, first_fn)

            @pl.when(step_t != 0)
            def _():
                chunk(0, 0, chunk_fn)

            lax.fori_loop(1, ncb, step, 0)

    in_specs = [pl.BlockSpec((ng, tb, DH), (lambda b, j, bi=bi: (bi, b * nt + j, 0))) for _, ng, bi, _ in ins]
    in_specs += [pl.BlockSpec(p.shape, lambda b, j: (0, 0, 0)) for p in list(prm) + list(cst)]
    return _pcall(
        body, name=name, grid=(nb, nt),
        in_specs=in_specs,
        out_specs=[pl.BlockSpec((NH, tb, DH), lambda b, j: (0, b * nt + j, 0)),
                   pl.BlockSpec((None, ncb, NH, DH, DH), lambda b, j: (b, j, 0, 0, 0))],
        out_shape=[jax.ShapeDtypeStruct((NH, nb * t, DH), BF16),
                   jax.ShapeDtypeStruct((nb, t // CH, NH, DH, DH), F32)],
        scratch_shapes=[pltpu.VMEM((NH, DH, DH), F32)],
        compiler_params=_cparams(("parallel", "arbitrary")),
    )(*[a for a, _, _, _ in ins], *prm, *cst)


def _mixer_bwd(chunk_fn, name, ins, prm, cst, ck, dy, dy_block, outs, routes, nb, t, first_fn=None):
    tb = _time_block(t)
    nt, ncb = t // tb, tb // CH
    n_in, n_prm, n_cst, n_out = len(ins), len(prm), len(cst), len(outs)
    sels = [sel for _, _, _, sel in ins]

    def body(*refs):
        in_refs = refs[:n_in]
        prm_refs = refs[n_in:n_in + n_prm]
        cst_refs = refs[n_in + n_prm:n_in + n_prm + n_cst]
        ck_ref, dy_ref = refs[n_in + n_prm + n_cst:n_in + n_prm + n_cst + 2]
        rest = refs[n_in + n_prm + n_cst + 2:]
        out_refs = rest[:n_out]
        dprm_refs = rest[n_out:n_out + n_prm]
        ds_scr = rest[n_out + n_prm]
        step_t = pl.program_id(1)

        @pl.when(step_t == 0)
        def _():
            ds_scr[...] = jnp.zeros_like(ds_scr)
            for r in dprm_refs:
                r[...] = jnp.zeros_like(r)

        def chunk(c, i, fn):
            shared = {}
            for h in range(NH):
                ins_c = [r[sel(h), pl.ds(i, CH), :] for r, sel in zip(in_refs, sels)]
                cst_h = [r[h] for r in cst_refs]
                _, vjp = jax.vjp(lambda p, x, s: fn(p, cst_h, x, s), [r[h] for r in prm_refs], ins_c, ck_ref[c, h])
                d_prm, d_ins, d_s = vjp((dy_ref[h, pl.ds(i, CH), :], ds_scr[h]))
                for k, g in enumerate(d_ins):
                    oi, sel, is_shared = routes[k]
                    if is_shared:
                        shared[k] = g if h == 0 else shared[k] + g
                    else:
                        out_refs[oi][sel(h), pl.ds(i, CH), :] = g.astype(out_refs[oi].dtype)
                for r, g in zip(dprm_refs, d_prm):
                    r[h] += g
                ds_scr[h] = d_s
            for k, g in shared.items():
                oi, sel, _ = routes[k]
                out_refs[oi][sel(0), pl.ds(i, CH), :] = g.astype(out_refs[oi].dtype)

        def step(j, carry):
            c = ncb - 1 - j
            chunk(c, pl.multiple_of(c * CH, CH), chunk_fn)
            return carry

        lax.fori_loop(0, ncb - 1, step, 0)
        if first_fn is None:
            chunk(0, 0, chunk_fn)
        else:
            @pl.when(step_t == nt - 1)
            def _():
                chunk(0, 0, first_fn)

            @pl.when(step_t != nt - 1)
            def _():
                chunk(0, 0, chunk_fn)

    def rows(b, j):
        return b * nt + (nt - 1 - j)

    in_specs = [pl.BlockSpec((ng, tb, DH), (lambda b, j, bi=bi: (bi, rows(b, j), 0))) for _, ng, bi, _ in ins]
    in_specs += [pl.BlockSpec(p.shape, lambda b, j: (0, 0, 0)) for p in list(prm) + list(cst)]
    in_specs += [pl.BlockSpec((None, ncb, NH, DH, DH), lambda b, j: (b, nt - 1 - j, 0, 0, 0)),
                 pl.BlockSpec((NH, tb, DH), lambda b, j: (dy_block, rows(b, j), 0))]
    out_specs = [pl.BlockSpec((ng, tb, DH), lambda b, j: (0, rows(b, j), 0)) for ng, _ in outs]
    out_specs += [pl.BlockSpec((None,) + p.shape, lambda b, j: (b, 0, 0, 0)) for p in prm]
    out_shape = [jax.ShapeDtypeStruct((ng, nb * t, DH), dt) for ng, dt in outs]
    out_shape += [jax.ShapeDtypeStruct((nb,) + p.shape, F32) for p in prm]
    res = _pcall(
        body, name=name, grid=(nb, nt),
        in_specs=in_specs, out_specs=out_specs, out_shape=out_shape,
        scratch_shapes=[pltpu.VMEM((NH, DH, DH), F32)],
        compiler_params=_cparams(("parallel", "arbitrary")),
    )(*[a for a, _, _, _ in ins], *prm, *cst, ck, dy)
    return res[:n_out], res[n_out:]


def _shift_down(x, s):
    if s == 0:
        return x
    row = lax.broadcasted_iota(jnp.int32, x.shape, 0)
    return jnp.where(row < s, 0.0, pltpu.roll(x, s, 0))


def _shift_up(x, s):
    if s == 0:
        return x
    t = x.shape[0]
    row = lax.broadcasted_iota(jnp.int32, x.shape, 0)
    return jnp.where(row >= t - s, 0.0, pltpu.roll(x, t - s, 0))


def _conv_fwd(p, g0, ng, w, nb, t, name):
    taps = w.shape[1]

    def body(x_ref, w_ref, y_ref):
        x = x_ref[...]
        acc = w_ref[taps - 1:taps, :] * x
        for i in range(taps - 1):
            acc = acc + w_ref[i:i + 1, :] * _shift_down(x, taps - 1 - i)
        y_ref[...] = acc

    return _pcall(
        body, name=name, grid=(ng, nb),
        in_specs=[pl.BlockSpec((None, t, DH), lambda g, b: (g0 + g, b, 0)),
                  pl.BlockSpec((None, taps, DH), lambda g, b: (g, 0, 0))],
        out_specs=pl.BlockSpec((None, t, DH), lambda g, b: (g, b, 0)),
        out_shape=jax.ShapeDtypeStruct((ng, nb * t, DH), F32),
        compiler_params=_cparams(("parallel", "parallel")),
    )(p, w)


def _conv_bwd(p, g0, ng, w, dy, nb, t, name):
    taps = w.shape[1]

    def body(x_ref, w_ref, dy_ref, dx_ref, dw_ref):
        x = x_ref[...]
        d = dy_ref[...]
        acc = w_ref[taps - 1:taps, :] * d
        dw_ref[taps - 1:taps, :] = jnp.sum(d * x, 0, keepdims=True)
        for i in range(taps - 1):
            s = taps - 1 - i
            acc = acc + w_ref[i:i + 1, :] * _shift_up(d, s)
            dw_ref[i:i + 1, :] = jnp.sum(d * _shift_down(x, s), 0, keepdims=True)
        dx_ref[...] = acc.astype(BF16)

    return _pcall(
        body, name=name, grid=(ng, nb),
        in_specs=[pl.BlockSpec((None, t, DH), lambda g, b: (g0 + g, b, 0)),
                  pl.BlockSpec((None, taps, DH), lambda g, b: (g, 0, 0)),
                  pl.BlockSpec((None, t, DH), lambda g, b: (g, b, 0))],
        out_specs=[pl.BlockSpec((None, t, DH), lambda g, b: (g, b, 0)),
                   pl.BlockSpec((None, None, taps, DH), lambda g, b: (g, b, 0, 0))],
        out_shape=[jax.ShapeDtypeStruct((ng, nb * t, DH), BF16),
                   jax.ShapeDtypeStruct((ng, nb, taps, DH), F32)],
        compiler_params=_cparams(("parallel", "parallel")),
    )(p, w, dy)


def _mix_group(g):
    return jnp.where(g < 16, G_RWKV + g, G_RWKV_WD + g - 16)


def _mix_fwd(p, mu, nb, t, name):
    def body(x_ref, mu_ref, y_ref):
        x = x_ref[...]
        y_ref[...] = x + mu_ref[...] * (_shift_down(x, 1) - x)

    return _pcall(
        body, name=name, grid=(18, nb),
        in_specs=[pl.BlockSpec((None, t, DH), lambda g, b: (_mix_group(g), b, 0)),
                  pl.BlockSpec((None, 1, DH), lambda g, b: (g, 0, 0))],
        out_specs=pl.BlockSpec((None, t, DH), lambda g, b: (g, b, 0)),
        out_shape=jax.ShapeDtypeStruct((18, nb * t, DH), F32),
        compiler_params=_cparams(("parallel", "parallel")),
    )(p, mu)


def _mix_bwd(p, mu, dy, nb, t, name):
    def body(x_ref, mu_ref, dy_ref, dx_ref, dmu_ref):
        x = x_ref[...]
        muv = mu_ref[...]
        d = dy_ref[...]
        dx_ref[...] = (d * (1.0 - muv) + _shift_up(d * muv, 1)).astype(BF16)
        dmu_ref[...] = jnp.sum(d * (_shift_down(x, 1) - x), 0, keepdims=True)

    return _pcall(
        body, name=name, grid=(18, nb),
        in_specs=[pl.BlockSpec((None, t, DH), lambda g, b: (_mix_group(g), b, 0)),
                  pl.BlockSpec((None, 1, DH), lambda g, b: (g, 0, 0)),
                  pl.BlockSpec((None, t, DH), lambda g, b: (g, b, 0))],
        out_specs=[pl.BlockSpec((None, t, DH), lambda g, b: (g, b, 0)),
                   pl.BlockSpec((None, None, 1, DH), lambda g, b: (g, b, 0, 0))],
        out_shape=[jax.ShapeDtypeStruct((18, nb * t, DH), BF16),
                   jax.ShapeDtypeStruct((18, nb, 1, DH), F32)],
        compiler_params=_cparams(("parallel", "parallel")),
    )(p, mu, dy)


def _sc_fwd(p, w, nb, t, name):
    def body(p_ref, w_ref, y_ref):
        u = p_ref[1] * p_ref[2]
        conv = w_ref[2:3, :] * u + w_ref[1:2, :] * _shift_down(u, 1) + w_ref[0:1, :] * _shift_down(u, 2)
        y_ref[...] = (p_ref[0] * conv * _silu(p_ref[3])).astype(BF16)

    return _pcall(
        body, name=name, grid=(NH, nb),
        in_specs=[pl.BlockSpec((4, t, DH), lambda j, b: (G_SC // 4 + j, b, 0)),
                  pl.BlockSpec((None, SC_TAPS, DH), lambda j, b: (j, 0, 0))],
        out_specs=pl.BlockSpec((None, t, DH), lambda j, b: (j, b, 0)),
        out_shape=jax.ShapeDtypeStruct((NH, nb * t, DH), BF16),
        compiler_params=_cparams(("parallel", "parallel")),
    )(p, w)


def _sc_bwd(p, w, dy, nb, t, name):
    def body(p_ref, w_ref, dy_ref, dp_ref, dw_ref):
        bg, cg, xg, z = p_ref[0], p_ref[1], p_ref[2], p_ref[3]
        d = dy_ref[...]
        u = cg * xg
        u1 = _shift_down(u, 1)
        u2 = _shift_down(u, 2)
        conv = w_ref[2:3, :] * u + w_ref[1:2, :] * u1 + w_ref[0:1, :] * u2
        sg = jax.nn.sigmoid(z)
        sz = z * sg
        dp_ref[0] = (d * conv * sz).astype(BF16)
        dp_ref[3] = (d * bg * conv * (sg * (1.0 + z * (1.0 - sg)))).astype(BF16)
        dconv = d * bg * sz
        du = w_ref[2:3, :] * dconv + w_ref[1:2, :] * _shift_up(dconv, 1) + w_ref[0:1, :] * _shift_up(dconv, 2)
        dp_ref[1] = (du * xg).astype(BF16)
        dp_ref[2] = (du * cg).astype(BF16)
        dw_ref[2:3, :] = jnp.sum(dconv * u, 0, keepdims=True)
        dw_ref[1:2, :] = jnp.sum(dconv * u1, 0, keepdims=True)
        dw_ref[0:1, :] = jnp.sum(dconv * u2, 0, keepdims=True)

    return _pcall(
        body, name=name, grid=(NH, nb),
        in_specs=[pl.BlockSpec((4, t, DH), lambda j, b: (G_SC // 4 + j, b, 0)),
                  pl.BlockSpec((None, SC_TAPS, DH), lambda j, b: (j, 0, 0)),
                  pl.BlockSpec((None, t, DH), lambda j, b: (8 + j, b, 0))],
        out_specs=[pl.BlockSpec((4, t, DH), lambda j, b: (j, b, 0)),
                   pl.BlockSpec((None, None, SC_TAPS, DH), lambda j, b: (j, b, 0, 0))],
        out_shape=[jax.ShapeDtypeStruct((4 * NH, nb * t, DH), BF16),
                   jax.ShapeDtypeStruct((NH, nb, SC_TAPS, DH), F32)],
        compiler_params=_cparams(("parallel", "parallel")),
    )(p, w, dy)


def _row_tile(n):
    return 512 if n % 512 == 0 else n


def _regroup_in(w_all, l, name):
    tr = 256

    def body(w_ref, o_ref):
        for g in _PADDED_GROUPS:
            o_ref[g] = jnp.zeros((tr, DH), BF16)
        for g, a, d, off, ln in _SEGMENTS:
            o_ref[g, :, a:a + ln] = w_ref[d, :, off:off + ln].astype(BF16)

    return _pcall(
        body, name=name, grid=(D_MODEL // tr,),
        in_specs=[pl.BlockSpec((N_DEV, None, tr, SHARD_COLS), lambda i: (0, l, i, 0))],
        out_specs=pl.BlockSpec((N_GROUPS, tr, DH), lambda i: (0, i, 0)),
        out_shape=jax.ShapeDtypeStruct((N_GROUPS, D_MODEL, DH), BF16),
        compiler_params=_cparams(("parallel",)),
    )(w_all)


def _regroup_out(dwg, name):
    tr = 256

    def body(g_ref, o_ref):
        for g, a, d, off, ln in _SEGMENTS:
            o_ref[d, :, off:off + ln] = g_ref[g, :, a:a + ln]

    return _pcall(
        body, name=name, grid=(D_MODEL // tr,),
        in_specs=[pl.BlockSpec((N_GROUPS, tr, DH), lambda i: (0, i, 0))],
        out_specs=pl.BlockSpec((N_DEV, tr, SHARD_COLS), lambda i: (0, i, 0)),
        out_shape=jax.ShapeDtypeStruct((N_DEV, D_MODEL, SHARD_COLS), F32),
        compiler_params=_cparams(("parallel",)),
    )(dwg)


def _norm_proj(x, pre_w, w_g, name):
    n = x.shape[0]
    tm = _row_tile(n)
    gs = GROUPS_PER_STEP

    def body(x_ref, pw_ref, w_ref, h_ref, p_ref):
        @pl.when(pl.program_id(1) == 0)
        def _():
            xv = x_ref[...]
            h = xv * lax.rsqrt(jnp.mean(xv * xv, -1, keepdims=True) + EPS) * pw_ref[...]
            h_ref[...] = h.astype(BF16)

        hb = h_ref[...]
        for k in range(gs):
            p_ref[k] = jnp.dot(hb, w_ref[k], preferred_element_type=F32)

    return _pcall(
        body, name=name, grid=(n // tm, N_GROUPS // gs),
        in_specs=[pl.BlockSpec((tm, D_MODEL), lambda i, j: (i, 0)),
                  pl.BlockSpec((1, D_MODEL), lambda i, j: (0, 0)),
                  pl.BlockSpec((gs, D_MODEL, DH), lambda i, j: (j, 0, 0))],
        out_specs=[pl.BlockSpec((tm, D_MODEL), lambda i, j: (i, 0)),
                   pl.BlockSpec((gs, tm, DH), lambda i, j: (j, i, 0))],
        out_shape=[jax.ShapeDtypeStruct((n, D_MODEL), BF16),
                   jax.ShapeDtypeStruct((N_GROUPS, n, DH), F32)],
        compiler_params=_cparams(("parallel", "arbitrary")),
    )(x, pre_w, w_g)


def _out_proj_norm(ys, wout_g, x, post_w, name):
    n = x.shape[0]
    tm = _row_tile(n)

    def body(y0, y1, y2, y3, w_ref, x_ref, pw_ref, out_ref, xn_ref):
        acc = jnp.zeros((tm, D_MODEL), F32)
        for m, yr in enumerate((y0, y1, y2, y3)):
            for h in range(NH):
                acc = acc + jnp.dot(yr[h], w_ref[m * NH + h], preferred_element_type=F32)
        out_ref[...] = acc
        xn_ref[...] = x_ref[...] + acc * lax.rsqrt(jnp.mean(acc * acc, -1, keepdims=True) + EPS) * pw_ref[...]

    yspec = pl.BlockSpec((NH, tm, DH), lambda i: (0, i, 0))
    rows = pl.BlockSpec((tm, D_MODEL), lambda i: (i, 0))
    return _pcall(
        body, name=name, grid=(n // tm,),
        in_specs=[yspec] * 4 + [pl.BlockSpec((4 * NH, DH, D_MODEL), lambda i: (0, 0, 0)), rows,
                                pl.BlockSpec((1, D_MODEL), lambda i: (0, 0))],
        out_specs=[rows, rows],
        out_shape=[jax.ShapeDtypeStruct((n, D_MODEL), F32)] * 2,
        compiler_params=_cparams(("parallel",)),
    )(*ys, wout_g, x, post_w)


def _loss_grad(x, tgt, name):
    n = x.shape[0]
    tm = _row_tile(n)

    def body(x_ref, t_ref, dx_ref, l_ref):
        @pl.when(pl.program_id(0) == 0)
        def _():
            l_ref[...] = jnp.zeros_like(l_ref)

        e = x_ref[...] - t_ref[...]
        dx_ref[...] = e * (1.0 / D_MODEL)
        l_ref[...] += jnp.sum(jnp.sum(e * e, -1, keepdims=True), 0, keepdims=True) * (0.5 / D_MODEL)

    rows = pl.BlockSpec((tm, D_MODEL), lambda i: (i, 0))
    return _pcall(
        body, name=name, grid=(n // tm,),
        in_specs=[rows, rows],
        out_specs=[rows, pl.BlockSpec((1, 128), lambda i: (0, 0))],
        out_shape=[jax.ShapeDtypeStruct((n, D_MODEL), F32), jax.ShapeDtypeStruct((1, 128), F32)],
        compiler_params=_cparams(("arbitrary",)),
    )(x, tgt)


def _rmsnorm_bwd(xv, w, d):
    r = lax.rsqrt(jnp.mean(xv * xv, -1, keepdims=True) + EPS)
    xh = xv * r
    dxh = d * w
    dx = r * (dxh - xh * jnp.mean(dxh * xh, -1, keepdims=True))
    return dx, d * xh


def _post_bwd(dxn, out, post_w, wout_g, name):
    n = dxn.shape[0]
    tm = _row_tile(n)

    def body(d_ref, o_ref, pw_ref, w_ref, do_ref, dy_ref, dpw_ref):
        @pl.when(pl.program_id(0) == 0)
        def _():
            dpw_ref[...] = jnp.zeros_like(dpw_ref)

        dout, dw_rows = _rmsnorm_bwd(o_ref[...], pw_ref[...], d_ref[...])
        dpw_ref[...] += jnp.sum(dw_rows, 0, keepdims=True)
        db = dout.astype(BF16)
        do_ref[...] = db
        for g in range(4 * NH):
            dy_ref[g] = lax.dot_general(db, w_ref[g], (((1,), (1,)), ((), ())), preferred_element_type=F32)

    rows = pl.BlockSpec((tm, D_MODEL), lambda i: (i, 0))
    vec = pl.BlockSpec((1, D_MODEL), lambda i: (0, 0))
    return _pcall(
        body, name=name, grid=(n // tm,),
        in_specs=[rows, rows, vec, pl.BlockSpec((4 * NH, DH, D_MODEL), lambda i: (0, 0, 0))],
        out_specs=[rows, pl.BlockSpec((4 * NH, tm, DH), lambda i: (0, i, 0)), vec],
        out_shape=[jax.ShapeDtypeStruct((n, D_MODEL), BF16),
                   jax.ShapeDtypeStruct((4 * NH, n, DH), F32),
                   jax.ShapeDtypeStruct((1, D_MODEL), F32)],
        compiler_params=_cparams(("arbitrary",)),
    )(dxn, out, post_w, wout_g)


def _dwout(ys, dout, name):
    n = dout.shape[0]
    tm = _row_tile(n)

    def body(y0, y1, y2, y3, d_ref, dw_ref):
        @pl.when(pl.program_id(0) == 0)
        def _():
            dw_ref[...] = jnp.zeros_like(dw_ref)

        d = d_ref[...]
        for m, yr in enumerate((y0, y1, y2, y3)):
            for h in range(NH):
                dw_ref[m * NH + h] += lax.dot_general(yr[h], d, (((0,), (0,)), ((), ())),
                                                      preferred_element_type=F32)

    yspec = pl.BlockSpec((NH, tm, DH), lambda i: (0, i, 0))
    return _pcall(
        body, name=name, grid=(n // tm,),
        in_specs=[yspec] * 4 + [pl.BlockSpec((tm, D_MODEL), lambda i: (i, 0))],
        out_specs=pl.BlockSpec((4 * NH, DH, D_MODEL), lambda i: (0, 0, 0)),
        out_shape=jax.ShapeDtypeStruct((4 * NH, DH, D_MODEL), F32),
        compiler_params=_cparams(("arbitrary",)),
    )(*ys, dout)


def _source_specs(sources, rows_first):
    gs = GROUPS_PER_STEP
    spans, specs, j0 = [], [], 0
    for a in sources:
        nblk = a.shape[0] // gs
        spans.append((j0, j0 + nblk))
        shape = (gs, _row_tile(a.shape[1]), DH)

        def blk(j, j0=j0, nblk=nblk):
            return jnp.clip(j - j0, 0, nblk - 1)

        if rows_first:
            specs.append(pl.BlockSpec(shape, (lambda i, j, blk=blk: (blk(j), i, 0))))
        else:
            specs.append(pl.BlockSpec(shape, (lambda j, i, blk=blk: (blk(j), i, 0))))
        j0 += nblk
    return spans, specs


def _dh_prenorm_bwd(sources, w_g, x, pre_w, dxn, name):
    n = x.shape[0]
    tm = _row_tile(n)
    gs = GROUPS_PER_STEP
    nj = N_GROUPS // gs
    spans, src_specs = _source_specs(sources, True)
    ns = len(sources)

    def body(*refs):
        src = refs[:ns]
        w_ref, x_ref, pw_ref, d_ref, dx_ref, dpw_ref, acc = refs[ns:]
        i, j = pl.program_id(0), pl.program_id(1)

        @pl.when((i == 0) & (j == 0))
        def _():
            dpw_ref[...] = jnp.zeros_like(dpw_ref)

        @pl.when(j == 0)
        def _():
            acc[...] = jnp.zeros_like(acc)

        for s_ref, (lo, hi) in zip(src, spans):
            @pl.when((j >= lo) & (j < hi))
            def _(s_ref=s_ref):
                a = acc[...]
                for k in range(gs):
                    a = a + lax.dot_general(s_ref[k], w_ref[k], (((1,), (1,)), ((), ())),
                                            preferred_element_type=F32)
                acc[...] = a

        @pl.when(j == nj - 1)
        def _():
            dx, dw_rows = _rmsnorm_bwd(x_ref[...], pw_ref[...], acc[...])
            dx_ref[...] = d_ref[...] + dx
            dpw_ref[...] += jnp.sum(dw_rows, 0, keepdims=True)

    rows = pl.BlockSpec((tm, D_MODEL), lambda i, j: (i, 0))
    vec = pl.BlockSpec((1, D_MODEL), lambda i, j: (0, 0))
    return _pcall(
        body, name=name, grid=(n // tm, nj),
        in_specs=src_specs + [pl.BlockSpec((gs, D_MODEL, DH), lambda i, j: (j, 0, 0)), rows, vec, rows],
        out_specs=[rows, vec],
        out_shape=[jax.ShapeDtypeStruct((n, D_MODEL), F32), jax.ShapeDtypeStruct((1, D_MODEL), F32)],
        scratch_shapes=[pltpu.VMEM((tm, D_MODEL), F32)],
        compiler_params=_cparams(("arbitrary", "arbitrary")),
    )(*sources, w_g, x, pre_w, dxn)


def _dwin(hb, sources, name):
    n = hb.shape[0]
    tm = _row_tile(n)
    gs = GROUPS_PER_STEP
    spans, src_specs = _source_specs(sources, False)
    ns = len(sources)

    def body(*refs):
        h_ref = refs[0]
        src = refs[1:1 + ns]
        dw_ref = refs[1 + ns]
        j = pl.program_id(0)

        @pl.when(pl.program_id(1) == 0)
        def _():
            dw_ref[...] = jnp.zeros_like(dw_ref)

        h = h_ref[...]
        for s_ref, (lo, hi) in zip(src, spans):
            @pl.when((j >= lo) & (j < hi))
            def _(s_ref=s_ref):
                for k in range(gs):
                    dw_ref[k] += lax.dot_general(h, s_ref[k], (((0,), (0,)), ((), ())),
                                                 preferred_element_type=F32)

    return _pcall(
        body, name=name, grid=(N_GROUPS // gs, n // tm),
        in_specs=[pl.BlockSpec((tm, D_MODEL), lambda j, i: (i, 0))] + src_specs,
        out_specs=pl.BlockSpec((gs, D_MODEL, DH), lambda j, i: (j, 0, 0)),
        out_shape=jax.ShapeDtypeStruct((N_GROUPS, D_MODEL, DH), F32),
        compiler_params=_cparams(("parallel", "arbitrary")),
    )(hb, *sources)


def _adamw_math(w, g, m, v):
    c1 = 1.0 - ADAM_B1 ** ADAM_STEP
    c2 = 1.0 - ADAM_B2 ** ADAM_STEP
    nm = ADAM_B1 * m + (1.0 - ADAM_B1) * g
    nv = ADAM_B2 * v + (1.0 - ADAM_B2) * (g * g)
    return -ADAM_LR * ((nm / c1) / (jnp.sqrt(nv / c2) + ADAM_EPS) + ADAM_WD * w), nm, nv


def _adamw(w, g, m, v, name):
    r, c = w.shape
    tr = 256 if r % 256 == 0 else r

    def body(w_ref, g_ref, m_ref, v_ref, d_ref, nm_ref, nv_ref):
        d_ref[...], nm_ref[...], nv_ref[...] = _adamw_math(w_ref[...], g_ref[...], m_ref[...], v_ref[...])

    spec = pl.BlockSpec((tr, c), lambda i: (i, 0))
    return _pcall(
        body, name=name, grid=(r // tr,),
        in_specs=[spec] * 4, out_specs=[spec] * 3,
        out_shape=[jax.ShapeDtypeStruct((r, c), F32)] * 3,
        compiler_params=_cparams(("parallel",)),
    )(w, g, m, v)


def _sum_adamw(parts, w, m, v, name):
    r, c = w.shape
    tr = 128 if r % 128 == 0 else r

    def body(p_ref, w_ref, m_ref, v_ref, g_ref, d_ref, nm_ref, nv_ref):
        g = p_ref[0]
        for k in range(1, N_DEV):
            g = g + p_ref[k]
        g_ref[...] = g
        d_ref[...], nm_ref[...], nv_ref[...] = _adamw_math(w_ref[...], g, m_ref[...], v_ref[...])

    spec = pl.BlockSpec((tr, c), lambda i: (i, 0))
    return _pcall(
        body, name=name, grid=(r // tr,),
        in_specs=[pl.BlockSpec((N_DEV, tr, c), lambda i: (0, i, 0))] + [spec] * 3, out_specs=[spec] * 4,
        out_shape=[jax.ShapeDtypeStruct((r, c), F32)] * 4,
        compiler_params=_cparams(("parallel",)),
    )(parts, w, m, v)


def _me():
    return lax.axis_index("x"), lax.axis_index("y"), lax.axis_index("c")


def _flat(x, y, c):
    return 4 * x + 2 * y + c


def _peer(k):
    x, y, c = _me()
    return (x ^ ((k >> 2) & 1), y ^ ((k >> 1) & 1), c ^ (k & 1))


def _all_gather(blocks, name):
    na = len(blocks)

    def body(*refs):
        x_refs, out_refs = refs[:na], refs[na:2 * na]
        send_sems, recv_sems, local_sems = refs[2 * na:]
        me = _flat(*_me())
        local = []
        for a in range(na):
            cp = pltpu.make_async_copy(x_refs[a], out_refs[a].at[me], local_sems.at[a])
            cp.start()
            local.append(cp)
        copies = []
        for k in range(1, N_DEV):
            for a in range(na):
                cp = pltpu.make_async_remote_copy(
                    src_ref=x_refs[a], dst_ref=out_refs[a].at[me],
                    send_sem=send_sems.at[a, k - 1], recv_sem=recv_sems.at[a, k - 1],
                    device_id=_peer(k), device_id_type=MESH)
                cp.start()
                copies.append(cp)
        for k in range(1, N_DEV):
            src = _flat(*_peer(k))
            for a in range(na):
                pltpu.make_async_remote_copy(
                    src_ref=x_refs[a], dst_ref=out_refs[a].at[src],
                    send_sem=send_sems.at[a, k - 1], recv_sem=recv_sems.at[a, k - 1],
                    device_id=_peer(k), device_id_type=MESH).wait_recv()
        for cp in copies:
            cp.wait_send()
        for cp in local:
            cp.wait()

    return _pcall(
        body, name=name,
        in_specs=[pl.BlockSpec(memory_space=pl.ANY)] * na,
        out_specs=[pl.BlockSpec(memory_space=pl.ANY)] * na,
        out_shape=[jax.ShapeDtypeStruct((N_DEV,) + b.shape, b.dtype) for b in blocks],
        scratch_shapes=[pltpu.SemaphoreType.DMA((na, N_DEV - 1)), pltpu.SemaphoreType.DMA((na, N_DEV - 1)),
                        pltpu.SemaphoreType.DMA((na,))],
    )(*blocks)


def _exchange(sends, layouts, out_shapes, name):
    ns, no = len(sends), len(out_shapes)

    def body(*refs):
        s_refs, out_refs = refs[:ns], refs[ns:ns + no]
        send_sems, recv_sems, local_sems = refs[ns + no:]
        me = _flat(*_me())

        def dst(i, k):
            o, pos = layouts[i]
            return out_refs[o].at[k] if pos is None else out_refs[o].at[k, pos]

        local = []
        for i in range(ns):
            cp = pltpu.make_async_copy(s_refs[i].at[me], dst(i, 0), local_sems.at[i])
            cp.start()
            local.append(cp)
        copies = []
        for k in range(1, N_DEV):
            to = _flat(*_peer(k))
            for i in range(ns):
                cp = pltpu.make_async_remote_copy(
                    src_ref=s_refs[i].at[to], dst_ref=dst(i, k),
                    send_sem=send_sems.at[i, k - 1], recv_sem=recv_sems.at[i, k - 1],
                    device_id=_peer(k), device_id_type=MESH)
                cp.start()
                copies.append(cp)
        for cp in copies:
            cp.wait_recv()
        for cp in copies:
            cp.wait_send()
        for cp in local:
            cp.wait()

    return _pcall(
        body, name=name,
        in_specs=[pl.BlockSpec(memory_space=pl.ANY)] * ns,
        out_specs=[pl.BlockSpec(memory_space=pl.ANY)] * no,
        out_shape=[jax.ShapeDtypeStruct(s, F32) for s in out_shapes],
        scratch_shapes=[pltpu.SemaphoreType.DMA((ns, N_DEV - 1)), pltpu.SemaphoreType.DMA((ns, N_DEV - 1)),
                        pltpu.SemaphoreType.DMA((ns,))],
    )(*sends)


def _sum_slots(a, name):
    r = a.shape[1]

    def body(a_ref, o_ref):
        acc = a_ref[0]
        for d in range(1, N_DEV):
            acc = acc + a_ref[d]
        o_ref[...] = acc

    return _pcall(body, name=name, out_shape=jax.ShapeDtypeStruct((r, 128), F32), compiler_params=_cparams())(a)


def _all_reduce_small(blk, name):
    r = blk.shape[0]

    def body(x_ref, out_ref, gath, send_sems, recv_sems):
        me = _flat(*_me())
        gath[me] = x_ref[...]
        copies = []
        for k in range(1, N_DEV):
            cp = pltpu.make_async_remote_copy(
                src_ref=x_ref, dst_ref=gath.at[me],
                send_sem=send_sems.at[k - 1], recv_sem=recv_sems.at[k - 1],
                device_id=_peer(k), device_id_type=MESH)
            cp.start()
            copies.append(cp)
        for k in range(1, N_DEV):
            src = _flat(*_peer(k))
            pltpu.make_async_remote_copy(
                src_ref=x_ref, dst_ref=gath.at[src],
                send_sem=send_sems.at[k - 1], recv_sem=recv_sems.at[k - 1],
                device_id=_peer(k), device_id_type=MESH).wait_recv()
        for cp in copies:
            cp.wait_send()
        acc = gath[0]
        for d in range(1, N_DEV):
            acc = acc + gath[d]
        out_ref[...] = acc

    return _pcall(
        body, name=name,
        in_specs=[pl.BlockSpec(memory_space=pltpu.VMEM)],
        out_specs=pl.BlockSpec(memory_space=pltpu.VMEM),
        out_shape=jax.ShapeDtypeStruct((r, 128), F32),
        scratch_shapes=[pltpu.VMEM((N_DEV, r, 128), F32),
                        pltpu.SemaphoreType.DMA((N_DEV - 1,)), pltpu.SemaphoreType.DMA((N_DEV - 1,))],
    )(blk)


def _heads(vec):
    return vec.reshape(NH, 1, DH)


def _rep(vec4):
    return jnp.broadcast_to(vec4.reshape(NH, 1, 1), (NH, 1, DH))


def _onehot_lane(offset):
    m = np.zeros((NH, 1, DH), np.float32)
    for h in range(NH):
        m[h, 0, offset + h] = 1.0
    return jnp.asarray(m)


_TINY = (("gdn_conv_w", (DEPTH, 4, 96)), ("rwkv_w_up", (DEPTH, 64, 32)), ("rwkv_a_up", (DEPTH, 64, 32)),
         ("sc_conv_w", (DEPTH, 3, 32)))
_TINY_ROWS = -(-sum(int(np.prod(s)) for _, s in _TINY) // 1024) * 8


def _pack_rows(arrays, rows, fill=0.0):
    flat = jnp.concatenate([a.reshape(-1) for a in arrays])
    return jnp.pad(flat, (0, rows * 128 - flat.shape[0]), constant_values=fill).reshape(rows, 128)


def _unpack_rows(p, named_shapes):
    lead = p.shape[:-2]
    flat = p.reshape(lead + (-1,))
    out, o = {}, 0
    for n, s in named_shapes:
        size = int(np.prod(s))
        out[n] = flat[..., o:o + size].reshape(lead + tuple(s))
        o += size
    return out


def _gather_last(a):
    return jnp.transpose(a, (1, 0, 2)).reshape(a.shape[1], -1)


def _split_last(a):
    r, c8 = a.shape
    return jnp.transpose(a.reshape(r, N_DEV, c8 // N_DEV), (1, 0, 2))


_SMALL = (("pre_norm_w", (DEPTH, 1024)), ("gdn_a_log", (DEPTH, 4)), ("gdn_dt_bias", (DEPTH, 4)),
          ("gdn_norm_w", (DEPTH, 64)), ("rwkv_mu", (DEPTH, 1152)), ("rwkv_w0", (DEPTH, 256)),
          ("rwkv_a0", (DEPTH, 256)), ("rwkv_k_k", (DEPTH, 256)), ("rwkv_k_a", (DEPTH, 256)),
          ("rwkv_r_k", (DEPTH, 256)), ("rwkv_ln_w", (DEPTH, 256)), ("rwkv_ln_b", (DEPTH, 256)),
          ("gla_a_up", (DEPTH, 16, 128)), ("gla_a_bias", (DEPTH, 128)), ("gla_norm_w", (DEPTH, 64)),
          ("post_norm_w", (DEPTH, 1024)), ("loss", ()))
_SMALL_ROWS = -(-sum(int(np.prod(s)) for _, s in _SMALL) // 1024) * 8


def _layer_params(wts, tiny, w_in_all, w_out_all, l):
    conv = _gather_last(tiny["gdn_conv_w"][:, l])
    q = {}
    q["gdn_conv"] = jnp.transpose(conv.reshape(GDN_TAPS, 12, DH), (1, 0, 2))
    q["gdn_prm"] = [_rep(wts["gdn_a_log"][l]), _rep(wts["gdn_dt_bias"][l]),
                    jnp.broadcast_to(wts["gdn_norm_w"][l].reshape(1, 1, DH), (NH, 1, DH))]
    q["gdn_cst"] = [_onehot_lane(0), _onehot_lane(NH)]
    q["rwkv_mu"] = wts["rwkv_mu"][l].reshape(18, 1, DH)
    w_up = jnp.transpose(_gather_last(tiny["rwkv_w_up"][:, l]).reshape(64, NH, DH), (1, 0, 2))
    a_up = jnp.transpose(_gather_last(tiny["rwkv_a_up"][:, l]).reshape(64, NH, DH), (1, 0, 2))
    q["rwkv_prm"] = [_heads(wts["rwkv_w0"][l]), w_up, _heads(wts["rwkv_a0"][l]), a_up,
                     _heads(wts["rwkv_k_k"][l]), _heads(wts["rwkv_k_a"][l]), _heads(wts["rwkv_r_k"][l]),
                     _heads(wts["rwkv_ln_w"][l]), _heads(wts["rwkv_ln_b"][l])]
    sc = _gather_last(tiny["sc_conv_w"][:, l])
    q["sc_conv"] = jnp.transpose(sc.reshape(SC_TAPS, NH, DH), (1, 0, 2))
    gla_up = jnp.transpose(wts["gla_a_up"][l].reshape(16, NH, GLA_HEAD_K), (1, 0, 2))
    gla_up = jnp.pad(gla_up, ((0, 0), (0, DH - 16), (0, DH - GLA_HEAD_K)))
    gla_b = jnp.pad(wts["gla_a_bias"][l].reshape(NH, 1, GLA_HEAD_K), ((0, 0), (0, 0), (0, DH - GLA_HEAD_K)))
    q["gla_prm"] = [gla_up, gla_b, jnp.broadcast_to(wts["gla_norm_w"][l].reshape(1, 1, DH), (NH, 1, DH))]
    q["w_g"] = _regroup_in(w_in_all, l, f"regroup_in{l}")
    q["wout_g"] = w_out_all[:, l].reshape(4 * NH, DH, D_MODEL).astype(BF16)
    q["pre_w"] = wts["pre_norm_w"][l].reshape(1, D_MODEL)
    q["post_w"] = wts["post_norm_w"][l].reshape(1, D_MODEL)
    return q


def _mixer_inputs(p, cq, pm):
    head = lambda h: h
    one = lambda h: 0
    gdn = [(cq, 4, 0, head), (cq, 4, 1, head), (cq, 4, 2, head), (p, 4, G_GDN // 4 + 3, head), (p, 1, G_GDN_AB, one)]
    rwkv = [(pm, 4, 0, head), (pm, 4, 1, head), (pm, 4, 2, head), (pm, 4, 3, head), (pm, 1, 16, one), (pm, 1, 17, one)]
    gla = [(p, 16, G_GLA // 16, (lambda h, k=k: 4 * h + k)) for k in range(4)] + [(p, 1, G_GLA_AD, one)]
    return gdn, rwkv, gla


def _layer_fwd(x, q, nb, t, l):
    hb, p = _norm_proj(x, q["pre_w"], q["w_g"], f"norm_proj{l}")
    cq = _conv_fwd(p, G_GDN, 12, q["gdn_conv"], nb, t, f"gdn_conv{l}")
    pm = _mix_fwd(p, q["rwkv_mu"], nb, t, f"rwkv_mix{l}")
    gdn_in, rwkv_in, gla_in = _mixer_inputs(p, cq, pm)
    y_gdn, ck_gdn = _mixer_fwd(_gdn_chunk, f"gdn_fwd{l}", gdn_in, q["gdn_prm"], q["gdn_cst"], nb, t)
    y_rwkv, ck_rwkv = _mixer_fwd(_rwkv_chunk, f"rwkv_fwd{l}", rwkv_in, q["rwkv_prm"], [], nb, t,
                                 first_fn=_rwkv_chunk_steps)
    y_sc = _sc_fwd(p, q["sc_conv"], nb, t, f"sc_fwd{l}")
    y_gla, ck_gla = _mixer_fwd(_gla_chunk, f"gla_fwd{l}", gla_in, q["gla_prm"], [], nb, t)
    ys = (y_gdn, y_rwkv, y_sc, y_gla)
    out, xn = _out_proj_norm(ys, q["wout_g"], x, q["post_w"], f"out_proj{l}")
    saved = dict(x=x, hb=hb, p=p, cq=cq, pm=pm, ys=ys, out=out, ck=(ck_gdn, ck_rwkv, ck_gla))
    return xn, saved


def _layer_bwd(dxn, q, sv, nb, t, l):
    p, cq, pm, ys = sv["p"], sv["cq"], sv["pm"], sv["ys"]
    dout, dy, d_post = _post_bwd(dxn, sv["out"], q["post_w"], q["wout_g"], f"post_bwd{l}")
    d_wout = _dwout(ys, dout, f"dwout{l}").reshape(N_DEV, 128, D_MODEL)
    gdn_in, rwkv_in, gla_in = _mixer_inputs(p, cq, pm)
    ck_gdn, ck_rwkv, ck_gla = sv["ck"]
    head = lambda h: h
    g = {}

    (d_conv, dz, dab), (da_log, ddt, dnw) = _mixer_bwd(
        _gdn_chunk, f"gdn_bwd{l}", gdn_in, q["gdn_prm"], q["gdn_cst"], ck_gdn, dy, 0,
        [(12, F32), (4, BF16), (1, BF16)],
        [(0, head, False), (0, lambda h: 4 + h, False), (0, lambda h: 8 + h, False), (1, head, False),
         (2, lambda h: 0, True)], nb, t)
    dconv_in, d_gconv = _conv_bwd(p, G_GDN, 12, q["gdn_conv"], d_conv, nb, t, f"gdn_conv_bwd{l}")
    g["gdn_conv_w"] = jnp.transpose(d_gconv.sum(1), (1, 0, 2)).reshape(GDN_TAPS, 768)
    g["gdn_a_log"] = da_log.sum((0, 2, 3))
    g["gdn_dt_bias"] = ddt.sum((0, 2, 3))
    g["gdn_norm_w"] = dnw.sum((0, 1, 2))

    (d_pm,), d_rprm = _mixer_bwd(
        _rwkv_chunk, f"rwkv_bwd{l}", rwkv_in, q["rwkv_prm"], [], ck_rwkv, dy, 1,
        [(18, F32)],
        [(0, head, False), (0, lambda h: 4 + h, False), (0, lambda h: 8 + h, False), (0, lambda h: 12 + h, False),
         (0, lambda h: 16, True), (0, lambda h: 17, True)], nb, t, first_fn=_rwkv_chunk_steps)
    dp_rwkv, d_mu = _mix_bwd(p, q["rwkv_mu"], d_pm, nb, t, f"rwkv_mix_bwd{l}")
    g["rwkv_mu"] = d_mu.sum(1).reshape(1152)
    rp = [a.sum(0) for a in d_rprm]
    g["rwkv_w0"] = rp[0].reshape(256)
    g["rwkv_w_up"] = jnp.transpose(rp[1], (1, 0, 2)).reshape(64, 256)
    g["rwkv_a0"] = rp[2].reshape(256)
    g["rwkv_a_up"] = jnp.transpose(rp[3], (1, 0, 2)).reshape(64, 256)
    for i, nme in enumerate(("rwkv_k_k", "rwkv_k_a", "rwkv_r_k", "rwkv_ln_w", "rwkv_ln_b")):
        g[nme] = rp[4 + i].reshape(256)

    dp_sc, d_scw = _sc_bwd(p, q["sc_conv"], dy, nb, t, f"sc_bwd{l}")
    g["sc_conv_w"] = jnp.transpose(d_scw.sum(1), (1, 0, 2)).reshape(SC_TAPS, 256)

    (dp_gla, dad), (d_aup, d_ab, d_gnw) = _mixer_bwd(
        _gla_chunk, f"gla_bwd{l}", gla_in, q["gla_prm"], [], ck_gla, dy, 3,
        [(16, BF16), (1, BF16)],
        [(0, (lambda h, k=k: 4 * h + k), False) for k in range(4)] + [(1, lambda h: 0, True)], nb, t)
    g["gla_a_up"] = jnp.transpose(d_aup.sum(0)[:, :16, :GLA_HEAD_K], (1, 0, 2)).reshape(16, 128)
    g["gla_a_bias"] = d_ab.sum(0)[:, 0, :GLA_HEAD_K].reshape(128)
    g["gla_norm_w"] = d_gnw.sum((0, 1, 2))

    singles = jnp.concatenate([dab, dp_rwkv[16:18], dad], axis=0)
    sources = [dconv_in, dz, dp_rwkv, dp_sc, dp_gla, singles]
    dx, d_pre = _dh_prenorm_bwd(sources, q["w_g"], sv["x"], q["pre_w"], dxn, f"dh_bwd{l}")
    d_win = _regroup_out(_dwin(sv["hb"], sources, f"dwin{l}"), f"regroup_out{l}")
    g["pre_norm_w"] = d_pre.reshape(D_MODEL)
    g["post_norm_w"] = d_post.reshape(D_MODEL)
    return dx, g, d_win, d_wout


def _local_step(x, tgt, wts, tiny, w_in_all, w_out_all):
    nb, t, d = x.shape
    xf = x.reshape(nb * t, d)
    qs, saved = [], []
    for l in range(DEPTH):
        q = _layer_params(wts, tiny, w_in_all, w_out_all, l)
        xf, sv = _layer_fwd(xf, q, nb, t, l)
        qs.append(q)
        saved.append(sv)
    dxf, lpart = _loss_grad(xf, tgt.reshape(nb * t, d), "loss")
    grads, d_win, d_wout = [None] * DEPTH, [None] * DEPTH, [None] * DEPTH
    for l in reversed(range(DEPTH)):
        dxf, grads[l], d_win[l], d_wout[l] = _layer_bwd(dxf, qs[l], saved[l], nb, t, l)
    small = {k: jnp.stack([grads[l][k] for l in range(DEPTH)]) for k in grads[0]}
    return lpart[0, 0], dxf.reshape(nb, t, d), small, d_win, d_wout


_WEIGHTS = ("pre_norm_w", "w_in", "gdn_conv_w", "gdn_a_log", "gdn_dt_bias", "gdn_norm_w", "rwkv_mu", "rwkv_w0",
            "rwkv_w_up", "rwkv_a0", "rwkv_a_up", "rwkv_k_k", "rwkv_k_a", "rwkv_r_k", "rwkv_ln_w", "rwkv_ln_b",
            "sc_conv_w", "gla_a_up", "gla_a_bias", "gla_norm_w", "w_out", "post_norm_w")


def kernel(x, pre_norm_w, w_in, gdn_conv_w, gdn_a_log, gdn_dt_bias, gdn_norm_w, rwkv_mu, rwkv_w0, rwkv_w_up, rwkv_a0, rwkv_a_up, rwkv_k_k, rwkv_k_a, rwkv_r_k, rwkv_ln_w, rwkv_ln_b, sc_conv_w, gla_a_up, gla_a_bias, gla_norm_w, w_out, post_norm_w, loss_target, m_pre_norm_w, m_w_in, m_gdn_conv_w, m_gdn_a_log, m_gdn_dt_bias, m_gdn_norm_w, m_rwkv_mu, m_rwkv_w0, m_rwkv_w_up, m_rwkv_a0, m_rwkv_a_up, m_rwkv_k_k, m_rwkv_k_a, m_rwkv_r_k, m_rwkv_ln_w, m_rwkv_ln_b, m_sc_conv_w, m_gla_a_up, m_gla_a_bias, m_gla_norm_w, m_w_out, m_post_norm_w, v_pre_norm_w, v_w_in, v_gdn_conv_w, v_gdn_a_log, v_gdn_dt_bias, v_gdn_norm_w, v_rwkv_mu, v_rwkv_w0, v_rwkv_w_up, v_rwkv_a0, v_rwkv_a_up, v_rwkv_k_k, v_rwkv_k_a, v_rwkv_r_k, v_rwkv_ln_w, v_rwkv_ln_b, v_sc_conv_w, v_gla_a_up, v_gla_a_bias, v_gla_norm_w, v_w_out, v_post_norm_w):
    env = dict(locals())
    w = {n: env[n] for n in _WEIGHTS}
    m = {n: env["m_" + n] for n in _WEIGHTS}
    v = {n: env["v_" + n] for n in _WEIGHTS}
    tiny_names = [n for n, _ in _TINY]

    w_in_all, w_out_all, tiny_all = _all_gather(
        [w_in, w_out, _pack_rows([w[n] for n in tiny_names], _TINY_ROWS)], "gather_weights")
    tiny = _unpack_rows(tiny_all, _TINY)

    lpart, grad_x, small, d_win, d_wout = _local_step(x, loss_target, w, tiny, w_in_all, w_out_all)

    tiny_send = jnp.stack([_pack_rows([_split_last(small[n][l])[d] for n in tiny_names for l in range(DEPTH)],
                                      _TINY_ROWS) for d in range(N_DEV)])
    r_win, r_wout, r_tiny = _exchange(
        [d_win[0], d_win[1], d_wout[0], d_wout[1], tiny_send],
        [(0, 0), (0, 1), (1, 0), (1, 1), (2, None)],
        [(N_DEV, DEPTH, D_MODEL, SHARD_COLS), (N_DEV, DEPTH, 128, D_MODEL), (N_DEV, _TINY_ROWS, 128)],
        "scatter_grads")
    grads, delta, new_m, new_v = {}, {}, {}, {}
    for n, parts in (("w_in", r_win), ("w_out", r_wout)):
        shp = w[n].shape
        two = lambda a: a.reshape(-1, shp[-1])
        res = _sum_adamw(parts.reshape(N_DEV, -1, shp[-1]), two(w[n]), two(m[n]), two(v[n]), "adamw_" + n)
        grads[n], delta[n], new_m[n], new_v[n] = [o.reshape(shp) for o in res]
    tiny_sum = _sum_slots(r_tiny, "sum_tiny").reshape(-1)
    o = 0
    for n, s in _TINY:
        size = int(np.prod(s))
        grads[n] = tiny_sum[o:o + size].reshape(s)
        o += size

    small = dict(small)
    small["loss"] = lpart
    red = _unpack_rows(_all_reduce_small(_pack_rows([small[n] for n, _ in _SMALL], _SMALL_ROWS), "reduce_small"),
                       _SMALL)
    loss = red.pop("loss")
    grads.update(red)

    rest = [n for n in _WEIGHTS if n not in ("w_in", "w_out")]
    rest_shapes = [(n, w[n].shape) for n in rest]
    rows = -(-sum(int(np.prod(s)) for _, s in rest_shapes) // 1024) * 8
    outs = _adamw(_pack_rows([w[n] for n in rest], rows), _pack_rows([grads[n] for n in rest], rows),
                  _pack_rows([m[n] for n in rest], rows), _pack_rows([v[n] for n in rest], rows, 1.0), "adamw_rest")
    for dst, packed in zip((delta, new_m, new_v), outs):
        dst.update(_unpack_rows(packed, rest_shapes))

    return (loss, grad_x, *[grads[n] for n in _WEIGHTS], *[delta[n] for n in _WEIGHTS],
            *[new_m[n] for n in _WEIGHTS], *[new_v[n] for n in _WEIGHTS])
```

```python
import functools
import math

import numpy as np
import jax
import jax.numpy as jnp
from jax import lax
from jax.experimental import pallas as pl
from jax.experimental.pallas import tpu as pltpu

F32 = jnp.float32
BF16 = jnp.bfloat16
HI = lax.Precision.HIGHEST

D_MODEL = 1024
DEPTH = 2
NH = 4
DH = 64
CH = 64
EPS = 1e-6
RWKV_GN_EPS = 64e-5
GLA_HEAD_K = 32
GLA_TAU = 16.0
GDN_TAPS = 4
SC_TAPS = 3
D_IN = 3992
N_DEV = 8
SHARD_COLS = D_IN // N_DEV

G_GDN = 0
G_RWKV = 16
G_SC = 32
G_GLA = 48
G_GDN_AB, G_RWKV_WD, G_RWKV_AD, G_GLA_AD = 64, 65, 66, 67
N_GROUPS = 68
GROUPS_PER_STEP = 4
TIME_BLOCK = 512

C_GDN, C_RWKV, C_SC, C_GLA = 0, 1032, 2184, 3208

ADAM_LR, ADAM_B1, ADAM_B2, ADAM_EPS, ADAM_WD, ADAM_STEP = 0.001, 0.9, 0.999, 1e-08, 0.01, 10

VMEM_LIMIT = 56 * 1024 * 1024
MESH = pl.DeviceIdType.MESH

_pcall = pl.pallas_call


def _cparams(sem=None):
    if sem is None:
        return pltpu.CompilerParams(vmem_limit_bytes=VMEM_LIMIT)
    return pltpu.CompilerParams(dimension_semantics=sem, vmem_limit_bytes=VMEM_LIMIT)


def _group_segments():
    table = [(G_GDN + i, C_GDN + DH * i, DH) for i in range(16)]
    table.append((G_GDN_AB, C_GDN + 1024, 8))
    table += [(G_RWKV + i, C_RWKV + DH * i, DH) for i in range(16)]
    table += [(G_RWKV_WD, C_RWKV + 1024, DH), (G_RWKV_AD, C_RWKV + 1088, DH)]
    table += [(G_SC + 4 * j + k, C_SC + 256 * k + DH * j, DH) for j in range(NH) for k in range(4)]
    for h in range(NH):
        table += [(G_GLA + h, C_GLA + GLA_HEAD_K * h, GLA_HEAD_K),
                  (G_GLA + 4 + h, C_GLA + 128 + GLA_HEAD_K * h, GLA_HEAD_K),
                  (G_GLA + 8 + h, C_GLA + 256 + DH * h, DH),
                  (G_GLA + 12 + h, C_GLA + 512 + DH * h, DH)]
    table.append((G_GLA_AD, C_GLA + 768, 16))
    segs, padded = [], []
    for g, c, n in table:
        if n < DH:
            padded.append(g)
        a = 0
        while n > 0:
            d, off = divmod(c, SHARD_COLS)
            ln = min(n, SHARD_COLS - off)
            segs.append((g, a, d, off, ln))
            c, a, n = c + ln, a + ln, n - ln
    return segs, padded


_SEGMENTS, _PADDED_GROUPS = _group_segments()


def _dn(ta, tb):
    return (((1 if ta else 2,), (2 if tb else 1,)), ((0,), (0,)))


def _hdot(a, b, ta=False, tb=False):
    return lax.dot_general(a, b, _dn(ta, tb), precision=HI, preferred_element_type=F32)


def _r(x):
    return x.astype(BF16)


def _rdot(a, b, ta=False, tb=False):
    return lax.dot_general(_r(a), _r(b), _dn(ta, tb), preferred_element_type=F32)


@jax.custom_vjp
def _bmm(a, b):
    return _rdot(a, b)


def _bmm_fwd(a, b):
    return _rdot(a, b), (a, b)


def _bmm_bwd(res, g):
    a, b = res
    return _rdot(g, b, tb=True), _rdot(a, g, ta=True)


_bmm.defvjp(_bmm_fwd, _bmm_bwd)


@jax.custom_vjp
def _bmm_nt(a, b):
    return _rdot(a, b, tb=True)


def _bmm_nt_fwd(a, b):
    return _rdot(a, b, tb=True), (a, b)


def _bmm_nt_bwd(res, g):
    a, b = res
    return _rdot(g, b), _rdot(g, a, ta=True)


_bmm_nt.defvjp(_bmm_nt_fwd, _bmm_nt_bwd)


@jax.custom_vjp
def _bmm_tn(a, b):
    return _rdot(a, b, ta=True)


def _bmm_tn_fwd(a, b):
    return _rdot(a, b, ta=True), (a, b)


def _bmm_tn_bwd(res, g):
    a, b = res
    return _rdot(b, g, tb=True), _rdot(a, g)


_bmm_tn.defvjp(_bmm_tn_fwd, _bmm_tn_bwd)


def _tri(n):
    i = lax.broadcasted_iota(jnp.int32, (n, n), 0)
    j = lax.broadcasted_iota(jnp.int32, (n, n), 1)
    return i >= j, i > j, i == j


def _heads_of(x, like):
    return jnp.broadcast_to(x[None], (like.shape[0],) + x.shape)


def _cumsum_rows(x):
    incl, _, _ = _tri(CH)
    return _hdot(_heads_of(incl.astype(F32), x), x)


def _inv_unit_lower(a):
    n = a.shape[-1]
    _, _, eye = _tri(n)
    pw = -a
    inv = eye.astype(F32) + pw
    for _ in range(int(math.log2(n)) - 1):
        pw = _hdot(pw, pw)
        inv = inv + _hdot(inv, pw)
    return inv


def _silu(x):
    return x * jax.nn.sigmoid(x)


def _t(x):
    return jnp.swapaxes(x, -1, -2)


def _gdn_chunk(prm, cst, ins, s):
    a_log, dt_b, nw = prm
    m_a, m_b = cst
    cq, ck, cv, z, ab = ins
    incl, strict, _ = _tri(CH)
    q = _silu(cq)
    k = _silu(ck)
    v = _silu(cv)
    q = q * lax.rsqrt(jnp.sum(q * q, -1, keepdims=True) + EPS) * (DH ** -0.5)
    k = k * lax.rsqrt(jnp.sum(k * k, -1, keepdims=True) + EPS)
    a_raw = jnp.sum(ab * m_a, -1, keepdims=True)
    b_raw = jnp.sum(ab * m_b, -1, keepdims=True)
    gstep = -jnp.exp(a_log) * jax.nn.softplus(a_raw + dt_b)
    beta = jax.nn.sigmoid(b_raw)
    gc = _cumsum_rows(gstep)
    gl = jnp.sum(gstep, -2, keepdims=True)
    dec = jnp.where(incl, jnp.exp(jnp.where(incl, gc - _t(gc), 0.0)), 0.0)
    kb = k * beta
    a_mat = jnp.where(strict, _bmm_nt(kb, k) * dec, 0.0)
    tinv = _inv_unit_lower(a_mat)
    eg = jnp.exp(gc)
    u = _hdot(tinv, v * beta)
    w = _hdot(tinv, kb * eg)
    attn = _bmm_nt(q, k) * dec
    v_new = u - _bmm(w, s)
    o = _bmm(q * eg, s) + _bmm(attn, v_new)
    s_next = s * jnp.exp(gl) + _bmm_tn(k * jnp.exp(gl - gc), v_new)
    on = o * lax.rsqrt(jnp.mean(o * o, -1, keepdims=True) + EPS) * nw
    return on * _silu(z), s_next


def _gla_chunk(prm, cst, ins, st):
    a_up, a_bias, nw = prm
    q, k, v, z, ad = ins
    incl, _, _ = _tri(CH)
    la = jax.nn.log_sigmoid(_bmm(_heads_of(ad, a_up), a_up) + a_bias) * (1.0 / GLA_TAU)
    bc = _cumsum_rows(la)
    bl = jnp.sum(la, -2, keepdims=True)
    qe = q * (GLA_HEAD_K ** -0.5) * jnp.exp(bc)
    ke = k * jnp.exp(-bc)
    attn = jnp.where(incl, _bmm_nt(qe, ke), 0.0)
    o = _bmm_nt(qe, st) + _bmm(attn, v)
    st_next = st * jnp.exp(bl) + _bmm_tn(v, k * jnp.exp(bl - bc))
    on = o * lax.rsqrt(jnp.mean(o * o, -1, keepdims=True) + EPS) * nw
    return on * _silu(z), st_next


def _rwkv_chunk(prm, cst, ins, s):
    r, v = ins[0], ins[2]
    incl, strict, _ = _tri(CH)
    lw, kk, k2, m = _rwkv_pre(prm, ins)
    cum = _cumsum_rows(lw)
    ltot = jnp.sum(lw, -2, keepdims=True)
    n_t = -kk * jnp.exp(cum - lw)
    einv = jnp.exp(-cum)
    m_t = m * einv
    k_t = k2 * einv
    r_t = r * jnp.exp(cum)
    a_nm = jnp.where(strict, _hdot(n_t, m_t, tb=True), 0.0)
    a_nk = jnp.where(strict, _hdot(n_t, k_t, tb=True), 0.0)
    cm = _hdot(_inv_unit_lower(-a_nm), _hdot(n_t, s, tb=True) + _hdot(a_nk, v))
    y = (_hdot(r_t, s, tb=True) + _hdot(jnp.where(incl, _hdot(r_t, m_t, tb=True), 0.0), cm)
         + _hdot(jnp.where(incl, _hdot(r_t, k_t, tb=True), 0.0), v))
    eend = jnp.exp(ltot - cum)
    s_next = s * jnp.exp(ltot) + _hdot(cm, m * eend, ta=True) + _hdot(v, k2 * eend, ta=True)
    return _rwkv_post(prm, ins, y, k2), s_next


def _rwkv_pre(prm, ins):
    w0, w_up, a0, a_up, k_k, k_a = prm[:6]
    k, wd, ad = ins[1], ins[4], ins[5]
    lw = -math.exp(-0.5) * jax.nn.sigmoid(w0 + _bmm(_heads_of(jnp.tanh(wd), w_up), w_up))
    a = jax.nn.sigmoid(a0 + _bmm(_heads_of(ad, a_up), a_up))
    kk = k * k_k
    kk = kk * lax.rsqrt(jnp.sum(kk * kk, -1, keepdims=True) + EPS)
    k2 = k * (1.0 + (a - 1.0) * k_a)
    return lw, kk, k2, kk * a


def _rwkv_post(prm, ins, y, k2):
    r_k, ln_w, ln_b = prm[6:]
    r, v, z = ins[0], ins[2], ins[3]
    mean = jnp.mean(y, -1, keepdims=True)
    yc = y - mean
    var = jnp.mean(yc * yc, -1, keepdims=True)
    yn = yc * lax.rsqrt(var + RWKV_GN_EPS) * ln_w + ln_b
    bonus = jnp.sum(r * k2 * r_k, -1, keepdims=True) * v
    return (yn + bonus) * _silu(z)


@jax.custom_vjp
def _bmv(s, x):
    return jnp.sum(_r(s).astype(F32) * _r(x).astype(F32), -1, keepdims=True)


def _bmv_fwd(s, x):
    return _bmv(s, x), (s, x)


def _bmv_bwd(res, g):
    s, x = res
    return g * x, jnp.sum(_r(s).astype(F32) * _r(g).astype(F32), -2, keepdims=True)


_bmv.defvjp(_bmv_fwd, _bmv_bwd)


def _rwkv_chunk_steps(prm, cst, ins, s):
    r, v = ins[0], ins[2]
    lw, kk, k2, m = _rwkv_pre(prm, ins)
    w = jnp.exp(lw)
    v_t = _t(v)
    lane = lax.broadcasted_iota(jnp.int32, (1, 1, CH), 2)
    y_t = jnp.zeros((s.shape[0], DH, CH), F32)
    for t in range(CH):
        e_t = (lane == t).astype(F32)
        row = (slice(None), slice(t, t + 1))
        sa = _bmv(s, -kk[row])
        s = s * w[row] + sa * m[row] + jnp.sum(v_t * e_t, -1, keepdims=True) * k2[row]
        y_t = y_t + _bmv(s, r[row]) * e_t
    return _rwkv_post(prm, ins, _t(y_t), k2), s


def _time_block(t):
    return TIME_BLOCK if t % TIME_BLOCK == 0 else t


def _load_chunk(ref, hs, i):
    return ref[hs, pl.ds(i, CH), :] if ref.shape[0] == NH else ref[0, pl.ds(i, CH), :]


def _each_head(fn):
    def step(h, carry):
        fn(pl.ds(h, 1))
        return carry

    lax.fori_loop(0, NH, step, 0)


def _mixer_fwd(chunk_fn, name, ins, prm, cst, nb, t, first_fn=None):
    tb = _time_block(t)
    nt, ncb = t // tb, tb // CH
    n_in, n_prm, n_cst = len(ins), len(prm), len(cst)

    def body(*refs):
        in_refs = refs[:n_in]
        prm_refs = refs[n_in:n_in + n_prm]
        cst_refs = refs[n_in + n_prm:n_in + n_prm + n_cst]
        y_ref, ck_ref, s_scr = refs[n_in + n_prm + n_cst:]
        step_t = pl.program_id(1)

        @pl.when(step_t == 0)
        def _():
            s_scr[...] = jnp.zeros_like(s_scr)

        def chunk(c, i, fn, hs=slice(None)):
            s = s_scr[hs]
            ck_ref[c, hs] = s
            y, s_next = fn([r[hs] for r in prm_refs], [r[hs] for r in cst_refs],
                           [_load_chunk(r, hs, i) for r in in_refs], s)
            y_ref[hs, pl.ds(i, CH), :] = y.astype(BF16)
            s_scr[hs] = s_next

        def step(c, carry):
            chunk(c, pl.multiple_of(c * CH, CH), chunk_fn)
            return carry

        if first_fn is None:
            lax.fori_loop(0, ncb, step, 0)
        else:
            @pl.when(step_t == 0)
            def _():
                _each_head(lambda hs: chunk(0, 0, first_fn, hs))

            @pl.when(step_t != 0)
            def _():
                chunk(0, 0, chunk_fn)

            lax.fori_loop(1, ncb, step, 0)

    in_specs = [pl.BlockSpec((ng, tb, DH), (lambda b, j, bi=bi: (bi, b * nt + j, 0))) for _, ng, bi in ins]
    in_specs += [pl.BlockSpec(p.shape, lambda b, j: (0, 0, 0)) for p in list(prm) + list(cst)]
    return _pcall(
        body, name=name, grid=(nb, nt),
        in_specs=in_specs,
        out_specs=[pl.BlockSpec((NH, tb, DH), lambda b, j: (0, b * nt + j, 0)),
                   pl.BlockSpec((None, ncb, NH, DH, DH), lambda b, j: (b, j, 0, 0, 0))],
        out_shape=[jax.ShapeDtypeStruct((NH, nb * t, DH), BF16),
                   jax.ShapeDtypeStruct((nb, t // CH, NH, DH, DH), F32)],
        scratch_shapes=[pltpu.VMEM((NH, DH, DH), F32)],
        compiler_params=_cparams(("parallel", "arbitrary")),
    )(*[a for a, _, _ in ins], *prm, *cst)


def _mixer_bwd(chunk_fn, name, ins, prm, cst, ck, dy, dy_block, outs, routes, nb, t, first_fn=None):
    tb = _time_block(t)
    nt, ncb = t // tb, tb // CH
    n_in, n_prm, n_cst, n_out = len(ins), len(prm), len(cst), len(outs)

    def body(*refs):
        in_refs = refs[:n_in]
        prm_refs = refs[n_in:n_in + n_prm]
        cst_refs = refs[n_in + n_prm:n_in + n_prm + n_cst]
        ck_ref, dy_ref = refs[n_in + n_prm + n_cst:n_in + n_prm + n_cst + 2]
        rest = refs[n_in + n_prm + n_cst + 2:]
        out_refs = rest[:n_out]
        dprm_refs = rest[n_out:n_out + n_prm]
        ds_scr = rest[n_out + n_prm]
        step_t = pl.program_id(1)

        @pl.when(step_t == 0)
        def _():
            ds_scr[...] = jnp.zeros_like(ds_scr)
            for r in dprm_refs:
                r[...] = jnp.zeros_like(r)

        def chunk(c, i, fn, hs=slice(None)):
            cst_v = [r[hs] for r in cst_refs]
            _, vjp = jax.vjp(lambda p, x, s: fn(p, cst_v, x, s), [r[hs] for r in prm_refs],
                             [_load_chunk(r, hs, i) for r in in_refs], ck_ref[c, hs])
            d_prm, d_ins, d_s = vjp((dy_ref[hs, pl.ds(i, CH), :], ds_scr[hs]))
            for (oi, g0), g in zip(routes, d_ins):
                o_ref = out_refs[oi]
                if g.ndim == 3:
                    o_ref.at[g0:g0 + NH][hs, pl.ds(i, CH), :] = g.astype(o_ref.dtype)
                elif isinstance(hs, slice):
                    o_ref[g0, pl.ds(i, CH), :] = g.astype(o_ref.dtype)
                else:
                    o_ref[g0, pl.ds(i, CH), :] += g.astype(o_ref.dtype)
            for r, g in zip(dprm_refs, d_prm):
                r[hs] += g
            ds_scr[hs] = d_s

        def first_chunk():
            for (oi, g0), r in zip(routes, in_refs):
                if r.shape[0] != NH:
                    out_refs[oi][g0, pl.ds(0, CH), :] = jnp.zeros((CH, DH), out_refs[oi].dtype)
            _each_head(lambda hs: chunk(0, 0, first_fn, hs))

        def step(j, carry):
            c = ncb - 1 - j
            chunk(c, pl.multiple_of(c * CH, CH), chunk_fn)
            return carry

        lax.fori_loop(0, ncb - 1, step, 0)
        if first_fn is None:
            chunk(0, 0, chunk_fn)
        else:
            @pl.when(step_t == nt - 1)
            def _():
                first_chunk()

            @pl.when(step_t != nt - 1)
            def _():
                chunk(0, 0, chunk_fn)

    def rows(b, j):
        return b * nt + (nt - 1 - j)

    in_specs = [pl.BlockSpec((ng, tb, DH), (lambda b, j, bi=bi: (bi, rows(b, j), 0))) for _, ng, bi in ins]
    in_specs += [pl.BlockSpec(p.shape, lambda b, j: (0, 0, 0)) for p in list(prm) + list(cst)]
    in_specs += [pl.BlockSpec((None, ncb, NH, DH, DH), lambda b, j: (b, nt - 1 - j, 0, 0, 0)),
                 pl.BlockSpec((NH, tb, DH), lambda b, j: (dy_block, rows(b, j), 0))]
    out_specs = [pl.BlockSpec((ng, tb, DH), lambda b, j: (0, rows(b, j), 0)) for ng, _ in outs]
    out_specs += [pl.BlockSpec((None,) + p.shape, lambda b, j: (b, 0, 0, 0)) for p in prm]
    out_shape = [jax.ShapeDtypeStruct((ng, nb * t, DH), dt) for ng, dt in outs]
    out_shape += [jax.ShapeDtypeStruct((nb,) + p.shape, F32) for p in prm]
    res = _pcall(
        body, name=name, grid=(nb, nt),
        in_specs=in_specs, out_specs=out_specs, out_shape=out_shape,
        scratch_shapes=[pltpu.VMEM((NH, DH, DH), F32)],
        compiler_params=_cparams(("parallel", "arbitrary")),
    )(*[a for a, _, _ in ins], *prm, *cst, ck, dy)
    return res[:n_out], res[n_out:]


def _shift_down(x, s):
    if s == 0:
        return x
    row = lax.broadcasted_iota(jnp.int32, x.shape, 0)
    return jnp.where(row < s, 0.0, pltpu.roll(x, s, 0))


def _shift_up(x, s):
    if s == 0:
        return x
    t = x.shape[0]
    row = lax.broadcasted_iota(jnp.int32, x.shape, 0)
    return jnp.where(row >= t - s, 0.0, pltpu.roll(x, t - s, 0))


def _conv_fwd(p, g0, ng, w, nb, t, name):
    taps = w.shape[1]

    def body(x_ref, w_ref, y_ref):
        x = x_ref[...]
        acc = w_ref[taps - 1:taps, :] * x
        for i in range(taps - 1):
            acc = acc + w_ref[i:i + 1, :] * _shift_down(x, taps - 1 - i)
        y_ref[...] = acc

    return _pcall(
        body, name=name, grid=(ng, nb),
        in_specs=[pl.BlockSpec((None, t, DH), lambda g, b: (g0 + g, b, 0)),
                  pl.BlockSpec((None, taps, DH), lambda g, b: (g, 0, 0))],
        out_specs=pl.BlockSpec((None, t, DH), lambda g, b: (g, b, 0)),
        out_shape=jax.ShapeDtypeStruct((ng, nb * t, DH), F32),
        compiler_params=_cparams(("parallel", "parallel")),
    )(p, w)


def _conv_bwd(p, g0, ng, w, dy, nb, t, name):
    taps = w.shape[1]

    def body(x_ref, w_ref, dy_ref, dx_ref, dw_ref):
        x = x_ref[...]
        d = dy_ref[...]
        acc = w_ref[taps - 1:taps, :] * d
        dw_ref[taps - 1:taps, :] = jnp.sum(d * x, 0, keepdims=True)
        for i in range(taps - 1):
            s = taps - 1 - i
            acc = acc + w_ref[i:i + 1, :] * _shift_up(d, s)
            dw_ref[i:i + 1, :] = jnp.sum(d * _shift_down(x, s), 0, keepdims=True)
        dx_ref[...] = acc.astype(BF16)

    return _pcall(
        body, name=name, grid=(ng, nb),
        in_specs=[pl.BlockSpec((None, t, DH), lambda g, b: (g0 + g, b, 0)),
                  pl.BlockSpec((None, taps, DH), lambda g, b: (g, 0, 0)),
                  pl.BlockSpec((None, t, DH), lambda g, b: (g, b, 0))],
        out_specs=[pl.BlockSpec((None, t, DH), lambda g, b: (g, b, 0)),
                   pl.BlockSpec((None, None, taps, DH), lambda g, b: (g, b, 0, 0))],
        out_shape=[jax.ShapeDtypeStruct((ng, nb * t, DH), BF16),
                   jax.ShapeDtypeStruct((ng, nb, taps, DH), F32)],
        compiler_params=_cparams(("parallel", "parallel")),
    )(p, w, dy)


def _mix_group(g):
    return jnp.where(g < 16, G_RWKV + g, G_RWKV_WD + g - 16)


def _mix_fwd(p, mu, nb, t, name):
    def body(x_ref, mu_ref, y_ref):
        x = x_ref[...]
        y_ref[...] = x + mu_ref[...] * (_shift_down(x, 1) - x)

    return _pcall(
        body, name=name, grid=(18, nb),
        in_specs=[pl.BlockSpec((None, t, DH), lambda g, b: (_mix_group(g), b, 0)),
                  pl.BlockSpec((None, 1, DH), lambda g, b: (g, 0, 0))],
        out_specs=pl.BlockSpec((None, t, DH), lambda g, b: (g, b, 0)),
        out_shape=jax.ShapeDtypeStruct((18, nb * t, DH), F32),
        compiler_params=_cparams(("parallel", "parallel")),
    )(p, mu)


def _mix_bwd(p, mu, dy, nb, t, name):
    def body(x_ref, mu_ref, dy_ref, dx_ref, dmu_ref):
        x = x_ref[...]
        muv = mu_ref[...]
        d = dy_ref[...]
        dx_ref[...] = (d * (1.0 - muv) + _shift_up(d * muv, 1)).astype(BF16)
        dmu_ref[...] = jnp.sum(d * (_shift_down(x, 1) - x), 0, keepdims=True)

    return _pcall(
        body, name=name, grid=(18, nb),
        in_specs=[pl.BlockSpec((None, t, DH), lambda g, b: (_mix_group(g), b, 0)),
                  pl.BlockSpec((None, 1, DH), lambda g, b: (g, 0, 0)),
                  pl.BlockSpec((None, t, DH), lambda g, b: (g, b, 0))],
        out_specs=[pl.BlockSpec((None, t, DH), lambda g, b: (g, b, 0)),
                   pl.BlockSpec((None, None, 1, DH), lambda g, b: (g, b, 0, 0))],
        out_shape=[jax.ShapeDtypeStruct((18, nb * t, DH), BF16),
                   jax.ShapeDtypeStruct((18, nb, 1, DH), F32)],
        compiler_params=_cparams(("parallel", "parallel")),
    )(p, mu, dy)


def _sc_fwd(p, w, nb, t, name):
    def body(p_ref, w_ref, y_ref):
        u = p_ref[1] * p_ref[2]
        conv = w_ref[2:3, :] * u + w_ref[1:2, :] * _shift_down(u, 1) + w_ref[0:1, :] * _shift_down(u, 2)
        y_ref[...] = (p_ref[0] * conv * _silu(p_ref[3])).astype(BF16)

    return _pcall(
        body, name=name, grid=(NH, nb),
        in_specs=[pl.BlockSpec((4, t, DH), lambda j, b: (G_SC // 4 + j, b, 0)),
                  pl.BlockSpec((None, SC_TAPS, DH), lambda j, b: (j, 0, 0))],
        out_specs=pl.BlockSpec((None, t, DH), lambda j, b: (j, b, 0)),
        out_shape=jax.ShapeDtypeStruct((NH, nb * t, DH), BF16),
        compiler_params=_cparams(("parallel", "parallel")),
    )(p, w)


def _sc_bwd(p, w, dy, nb, t, name):
    def body(p_ref, w_ref, dy_ref, dp_ref, dw_ref):
        bg, cg, xg, z = p_ref[0], p_ref[1], p_ref[2], p_ref[3]
        d = dy_ref[...]
        u = cg * xg
        u1 = _shift_down(u, 1)
        u2 = _shift_down(u, 2)
        conv = w_ref[2:3, :] * u + w_ref[1:2, :] * u1 + w_ref[0:1, :] * u2
        sg = jax.nn.sigmoid(z)
        sz = z * sg
        dp_ref[0] = (d * conv * sz).astype(BF16)
        dp_ref[3] = (d * bg * conv * (sg * (1.0 + z * (1.0 - sg)))).astype(BF16)
        dconv = d * bg * sz
        du = w_ref[2:3, :] * dconv + w_ref[1:2, :] * _shift_up(dconv, 1) + w_ref[0:1, :] * _shift_up(dconv, 2)
        dp_ref[1] = (du * xg).astype(BF16)
        dp_ref[2] = (du * cg).astype(BF16)
        dw_ref[2:3, :] = jnp.sum(dconv * u, 0, keepdims=True)
        dw_ref[1:2, :] = jnp.sum(dconv * u1, 0, keepdims=True)
        dw_ref[0:1, :] = jnp.sum(dconv * u2, 0, keepdims=True)

    return _pcall(
        body, name=name, grid=(NH, nb),
        in_specs=[pl.BlockSpec((4, t, DH), lambda j, b: (G_SC // 4 + j, b, 0)),
                  pl.BlockSpec((None, SC_TAPS, DH), lambda j, b: (j, 0, 0)),
                  pl.BlockSpec((None, t, DH), lambda j, b: (8 + j, b, 0))],
        out_specs=[pl.BlockSpec((4, t, DH), lambda j, b: (j, b, 0)),
                   pl.BlockSpec((None, None, SC_TAPS, DH), lambda j, b: (j, b, 0, 0))],
        out_shape=[jax.ShapeDtypeStruct((4 * NH, nb * t, DH), BF16),
                   jax.ShapeDtypeStruct((NH, nb, SC_TAPS, DH), F32)],
        compiler_params=_cparams(("parallel", "parallel")),
    )(p, w, dy)


def _row_tile(n):
    return 512 if n % 512 == 0 else n


def _regroup_in(w_all, l, name):
    tr = 256

    def body(w_ref, o_ref):
        for g in _PADDED_GROUPS:
            o_ref[g] = jnp.zeros((tr, DH), BF16)
        for g, a, d, off, ln in _SEGMENTS:
            o_ref[g, :, a:a + ln] = w_ref[d, :, off:off + ln].astype(BF16)

    return _pcall(
        body, name=name, grid=(D_MODEL // tr,),
        in_specs=[pl.BlockSpec((N_DEV, None, tr, SHARD_COLS), lambda i: (0, l, i, 0))],
        out_specs=pl.BlockSpec((N_GROUPS, tr, DH), lambda i: (0, i, 0)),
        out_shape=jax.ShapeDtypeStruct((N_GROUPS, D_MODEL, DH), BF16),
        compiler_params=_cparams(("parallel",)),
    )(w_all)


def _regroup_out(dwg, name):
    tr = 256

    def body(g_ref, o_ref):
        for g, a, d, off, ln in _SEGMENTS:
            o_ref[d, :, off:off + ln] = g_ref[g, :, a:a + ln]

    return _pcall(
        body, name=name, grid=(D_MODEL // tr,),
        in_specs=[pl.BlockSpec((N_GROUPS, tr, DH), lambda i: (0, i, 0))],
        out_specs=pl.BlockSpec((N_DEV, tr, SHARD_COLS), lambda i: (0, i, 0)),
        out_shape=jax.ShapeDtypeStruct((N_DEV, D_MODEL, SHARD_COLS), F32),
        compiler_params=_cparams(("parallel",)),
    )(dwg)


def _norm_proj(x, pre_w, w_g, name):
    n = x.shape[0]
    tm = _row_tile(n)
    gs = GROUPS_PER_STEP

    def body(x_ref, pw_ref, w_ref, h_ref, p_ref):
        @pl.when(pl.program_id(1) == 0)
        def _():
            xv = x_ref[...]
            h = xv * lax.rsqrt(jnp.mean(xv * xv, -1, keepdims=True) + EPS) * pw_ref[...]
            h_ref[...] = h.astype(BF16)

        hb = h_ref[...]
        for k in range(gs):
            p_ref[k] = jnp.dot(hb, w_ref[k], preferred_element_type=F32)

    return _pcall(
        body, name=name, grid=(n // tm, N_GROUPS // gs),
        in_specs=[pl.BlockSpec((tm, D_MODEL), lambda i, j: (i, 0)),
                  pl.BlockSpec((1, D_MODEL), lambda i, j: (0, 0)),
                  pl.BlockSpec((gs, D_MODEL, DH), lambda i, j: (j, 0, 0))],
        out_specs=[pl.BlockSpec((tm, D_MODEL), lambda i, j: (i, 0)),
                   pl.BlockSpec((gs, tm, DH), lambda i, j: (j, i, 0))],
        out_shape=[jax.ShapeDtypeStruct((n, D_MODEL), BF16),
                   jax.ShapeDtypeStruct((N_GROUPS, n, DH), F32)],
        compiler_params=_cparams(("parallel", "arbitrary")),
    )(x, pre_w, w_g)


def _out_proj_norm(ys, wout_g, x, post_w, name):
    n = x.shape[0]
    tm = _row_tile(n)

    def body(y0, y1, y2, y3, w_ref, x_ref, pw_ref, out_ref, xn_ref):
        acc = jnp.zeros((tm, D_MODEL), F32)
        for m, yr in enumerate((y0, y1, y2, y3)):
            for h in range(NH):
                acc = acc + jnp.dot(yr[h], w_ref[m * NH + h], preferred_element_type=F32)
        out_ref[...] = acc
        xn_ref[...] = x_ref[...] + acc * lax.rsqrt(jnp.mean(acc * acc, -1, keepdims=True) + EPS) * pw_ref[...]

    yspec = pl.BlockSpec((NH, tm, DH), lambda i: (0, i, 0))
    rows = pl.BlockSpec((tm, D_MODEL), lambda i: (i, 0))
    return _pcall(
        body, name=name, grid=(n // tm,),
        in_specs=[yspec] * 4 + [pl.BlockSpec((4 * NH, DH, D_MODEL), lambda i: (0, 0, 0)), rows,
                                pl.BlockSpec((1, D_MODEL), lambda i: (0, 0))],
        out_specs=[rows, rows],
        out_shape=[jax.ShapeDtypeStruct((n, D_MODEL), F32)] * 2,
        compiler_params=_cparams(("parallel",)),
    )(*ys, wout_g, x, post_w)


def _loss_grad(x, tgt, name):
    n = x.shape[0]
    tm = _row_tile(n)

    def body(x_ref, t_ref, dx_ref, l_ref):
        @pl.when(pl.program_id(0) == 0)
        def _():
            l_ref[...] = jnp.zeros_like(l_ref)

        e = x_ref[...] - t_ref[...]
        dx_ref[...] = e * (1.0 / D_MODEL)
        l_ref[...] += jnp.sum(jnp.sum(e * e, -1, keepdims=True), 0, keepdims=True) * (0.5 / D_MODEL)

    rows = pl.BlockSpec((tm, D_MODEL), lambda i: (i, 0))
    return _pcall(
        body, name=name, grid=(n // tm,),
        in_specs=[rows, rows],
        out_specs=[rows, pl.BlockSpec((1, 128), lambda i: (0, 0))],
        out_shape=[jax.ShapeDtypeStruct((n, D_MODEL), F32), jax.ShapeDtypeStruct((1, 128), F32)],
        compiler_params=_cparams(("arbitrary",)),
    )(x, tgt)


def _rmsnorm_bwd(xv, w, d):
    r = lax.rsqrt(jnp.mean(xv * xv, -1, keepdims=True) + EPS)
    xh = xv * r
    dxh = d * w
    dx = r * (dxh - xh * jnp.mean(dxh * xh, -1, keepdims=True))
    return dx, d * xh


def _post_bwd(dxn, out, post_w, wout_g, name):
    n = dxn.shape[0]
    tm = _row_tile(n)

    def body(d_ref, o_ref, pw_ref, w_ref, do_ref, dy_ref, dpw_ref):
        @pl.when(pl.program_id(0) == 0)
        def _():
            dpw_ref[...] = jnp.zeros_like(dpw_ref)

        dout, dw_rows = _rmsnorm_bwd(o_ref[...], pw_ref[...], d_ref[...])
        dpw_ref[...] += jnp.sum(dw_rows, 0, keepdims=True)
        db = dout.astype(BF16)
        do_ref[...] = db
        for g in range(4 * NH):
            dy_ref[g] = lax.dot_general(db, w_ref[g], (((1,), (1,)), ((), ())), preferred_element_type=F32)

    rows = pl.BlockSpec((tm, D_MODEL), lambda i: (i, 0))
    vec = pl.BlockSpec((1, D_MODEL), lambda i: (0, 0))
    return _pcall(
        body, name=name, grid=(n // tm,),
        in_specs=[rows, rows, vec, pl.BlockSpec((4 * NH, DH, D_MODEL), lambda i: (0, 0, 0))],
        out_specs=[rows, pl.BlockSpec((4 * NH, tm, DH), lambda i: (0, i, 0)), vec],
        out_shape=[jax.ShapeDtypeStruct((n, D_MODEL), BF16),
                   jax.ShapeDtypeStruct((4 * NH, n, DH), F32),
                   jax.ShapeDtypeStruct((1, D_MODEL), F32)],
        compiler_params=_cparams(("arbitrary",)),
    )(dxn, out, post_w, wout_g)


def _dwout(ys, dout, name):
    n = dout.shape[0]
    tm = _row_tile(n)

    def body(y0, y1, y2, y3, d_ref, dw_ref):
        @pl.when(pl.program_id(0) == 0)
        def _():
            dw_ref[...] = jnp.zeros_like(dw_ref)

        d = d_ref[...]
        for m, yr in enumerate((y0, y1, y2, y3)):
            for h in range(NH):
                dw_ref[m * NH + h] += lax.dot_general(yr[h], d, (((0,), (0,)), ((), ())),
                                                      preferred_element_type=F32)

    yspec = pl.BlockSpec((NH, tm, DH), lambda i: (0, i, 0))
    return _pcall(
        body, name=name, grid=(n // tm,),
        in_specs=[yspec] * 4 + [pl.BlockSpec((tm, D_MODEL), lambda i: (i, 0))],
        out_specs=pl.BlockSpec((4 * NH, DH, D_MODEL), lambda i: (0, 0, 0)),
        out_shape=jax.ShapeDtypeStruct((4 * NH, DH, D_MODEL), F32),
        compiler_params=_cparams(("arbitrary",)),
    )(*ys, dout)


def _source_specs(sources, rows_first):
    gs = GROUPS_PER_STEP
    spans, specs, j0 = [], [], 0
    for a in sources:
        nblk = a.shape[0] // gs
        spans.append((j0, j0 + nblk))
        shape = (gs, _row_tile(a.shape[1]), DH)

        def blk(j, j0=j0, nblk=nblk):
            return jnp.clip(j - j0, 0, nblk - 1)

        if rows_first:
            specs.append(pl.BlockSpec(shape, (lambda i, j, blk=blk: (blk(j), i, 0))))
        else:
            specs.append(pl.BlockSpec(shape, (lambda j, i, blk=blk: (blk(j), i, 0))))
        j0 += nblk
    return spans, specs


def _dh_prenorm_bwd(sources, w_g, x, pre_w, dxn, name):
    n = x.shape[0]
    tm = _row_tile(n)
    gs = GROUPS_PER_STEP
    nj = N_GROUPS // gs
    spans, src_specs = _source_specs(sources, True)
    ns = len(sources)

    def body(*refs):
        src = refs[:ns]
        w_ref, x_ref, pw_ref, d_ref, dx_ref, dpw_ref, acc = refs[ns:]
        i, j = pl.program_id(0), pl.program_id(1)

        @pl.when((i == 0) & (j == 0))
        def _():
            dpw_ref[...] = jnp.zeros_like(dpw_ref)

        @pl.when(j == 0)
        def _():
            acc[...] = jnp.zeros_like(acc)

        for s_ref, (lo, hi) in zip(src, spans):
            @pl.when((j >= lo) & (j < hi))
            def _(s_ref=s_ref):
                a = acc[...]
                for k in range(gs):
                    a = a + lax.dot_general(s_ref[k], w_ref[k], (((1,), (1,)), ((), ())),
                                            preferred_element_type=F32)
                acc[...] = a

        @pl.when(j == nj - 1)
        def _():
            dx, dw_rows = _rmsnorm_bwd(x_ref[...], pw_ref[...], acc[...])
            dx_ref[...] = d_ref[...] + dx
            dpw_ref[...] += jnp.sum(dw_rows, 0, keepdims=True)

    rows = pl.BlockSpec((tm, D_MODEL), lambda i, j: (i, 0))
    vec = pl.BlockSpec((1, D_MODEL), lambda i, j: (0, 0))
    return _pcall(
        body, name=name, grid=(n // tm, nj),
        in_specs=src_specs + [pl.BlockSpec((gs, D_MODEL, DH), lambda i, j: (j, 0, 0)), rows, vec, rows],
        out_specs=[rows, vec],
        out_shape=[jax.ShapeDtypeStruct((n, D_MODEL), F32), jax.ShapeDtypeStruct((1, D_MODEL), F32)],
        scratch_shapes=[pltpu.VMEM((tm, D_MODEL), F32)],
        compiler_params=_cparams(("arbitrary", "arbitrary")),
    )(*sources, w_g, x, pre_w, dxn)


def _dwin(hb, sources, name):
    n = hb.shape[0]
    tm = _row_tile(n)
    gs = GROUPS_PER_STEP
    spans, src_specs = _source_specs(sources, False)
    ns = len(sources)

    def body(*refs):
        h_ref = refs[0]
        src = refs[1:1 + ns]
        dw_ref = refs[1 + ns]
        j = pl.program_id(0)

        @pl.when(pl.program_id(1) == 0)
        def _():
            dw_ref[...] = jnp.zeros_like(dw_ref)

        h = h_ref[...]
        for s_ref, (lo, hi) in zip(src, spans):
            @pl.when((j >= lo) & (j < hi))
            def _(s_ref=s_ref):
                for k in range(gs):
                    dw_ref[k] += lax.dot_general(h, s_ref[k], (((0,), (0,)), ((), ())),
                                                 preferred_element_type=F32)

    return _pcall(
        body, name=name, grid=(N_GROUPS // gs, n // tm),
        in_specs=[pl.BlockSpec((tm, D_MODEL), lambda j, i: (i, 0))] + src_specs,
        out_specs=pl.BlockSpec((gs, D_MODEL, DH), lambda j, i: (j, 0, 0)),
        out_shape=jax.ShapeDtypeStruct((N_GROUPS, D_MODEL, DH), F32),
        compiler_params=_cparams(("parallel", "arbitrary")),
    )(hb, *sources)


def _adamw_math(w, g, m, v):
    c1 = 1.0 - ADAM_B1 ** ADAM_STEP
    c2 = 1.0 - ADAM_B2 ** ADAM_STEP
    nm = ADAM_B1 * m + (1.0 - ADAM_B1) * g
    nv = ADAM_B2 * v + (1.0 - ADAM_B2) * (g * g)
    return -ADAM_LR * ((nm / c1) / (jnp.sqrt(nv / c2) + ADAM_EPS) + ADAM_WD * w), nm, nv


def _adamw(w, g, m, v, name):
    r, c = w.shape
    tr = 256 if r % 256 == 0 else r

    def body(w_ref, g_ref, m_ref, v_ref, d_ref, nm_ref, nv_ref):
        d_ref[...], nm_ref[...], nv_ref[...] = _adamw_math(w_ref[...], g_ref[...], m_ref[...], v_ref[...])

    spec = pl.BlockSpec((tr, c), lambda i: (i, 0))
    return _pcall(
        body, name=name, grid=(r // tr,),
        in_specs=[spec] * 4, out_specs=[spec] * 3,
        out_shape=[jax.ShapeDtypeStruct((r, c), F32)] * 3,
        compiler_params=_cparams(("parallel",)),
    )(w, g, m, v)


def _sum_adamw(parts, w, m, v, name):
    r, c = w.shape
    tr = 128 if r % 128 == 0 else r

    def body(p_ref, w_ref, m_ref, v_ref, g_ref, d_ref, nm_ref, nv_ref):
        g = p_ref[0]
        for k in range(1, N_DEV):
            g = g + p_ref[k]
        g_ref[...] = g
        d_ref[...], nm_ref[...], nv_ref[...] = _adamw_math(w_ref[...], g, m_ref[...], v_ref[...])

    spec = pl.BlockSpec((tr, c), lambda i: (i, 0))
    return _pcall(
        body, name=name, grid=(r // tr,),
        in_specs=[pl.BlockSpec((N_DEV, tr, c), lambda i: (0, i, 0))] + [spec] * 3, out_specs=[spec] * 4,
        out_shape=[jax.ShapeDtypeStruct((r, c), F32)] * 4,
        compiler_params=_cparams(("parallel",)),
    )(parts, w, m, v)


def _me():
    return lax.axis_index("x"), lax.axis_index("y"), lax.axis_index("c")


def _flat(x, y, c):
    return 4 * x + 2 * y + c


def _peer(k):
    x, y, c = _me()
    return (x ^ ((k >> 2) & 1), y ^ ((k >> 1) & 1), c ^ (k & 1))


def _all_gather(blocks, name):
    na = len(blocks)

    def body(*refs):
        x_refs, out_refs = refs[:na], refs[na:2 * na]
        send_sems, recv_sems, local_sems = refs[2 * na:]
        me = _flat(*_me())
        local = []
        for a in range(na):
            cp = pltpu.make_async_copy(x_refs[a], out_refs[a].at[me], local_sems.at[a])
            cp.start()
            local.append(cp)
        copies = []
        for k in range(1, N_DEV):
            for a in range(na):
                cp = pltpu.make_async_remote_copy(
                    src_ref=x_refs[a], dst_ref=out_refs[a].at[me],
                    send_sem=send_sems.at[a, k - 1], recv_sem=recv_sems.at[a, k - 1],
                    device_id=_peer(k), device_id_type=MESH)
                cp.start()
                copies.append(cp)
        for k in range(1, N_DEV):
            src = _flat(*_peer(k))
            for a in range(na):
                pltpu.make_async_remote_copy(
                    src_ref=x_refs[a], dst_ref=out_refs[a].at[src],
                    send_sem=send_sems.at[a, k - 1], recv_sem=recv_sems.at[a, k - 1],
                    device_id=_peer(k), device_id_type=MESH).wait_recv()
        for cp in copies:
            cp.wait_send()
        for cp in local:
            cp.wait()

    return _pcall(
        body, name=name,
        in_specs=[pl.BlockSpec(memory_space=pl.ANY)] * na,
        out_specs=[pl.BlockSpec(memory_space=pl.ANY)] * na,
        out_shape=[jax.ShapeDtypeStruct((N_DEV,) + b.shape, b.dtype) for b in blocks],
        scratch_shapes=[pltpu.SemaphoreType.DMA((na, N_DEV - 1)), pltpu.SemaphoreType.DMA((na, N_DEV - 1)),
                        pltpu.SemaphoreType.DMA((na,))],
    )(*blocks)


def _exchange(sends, layouts, out_shapes, name):
    ns, no = len(sends), len(out_shapes)

    def body(*refs):
        s_refs, out_refs = refs[:ns], refs[ns:ns + no]
        send_sems, recv_sems, local_sems = refs[ns + no:]
        me = _flat(*_me())

        def dst(i, k):
            o, pos = layouts[i]
            return out_refs[o].at[k] if pos is None else out_refs[o].at[k, pos]

        local = []
        for i in range(ns):
            cp = pltpu.make_async_copy(s_refs[i].at[me], dst(i, 0), local_sems.at[i])
            cp.start()
            local.append(cp)
        copies = []
        for k in range(1, N_DEV):
            to = _flat(*_peer(k))
            for i in range(ns):
                cp = pltpu.make_async_remote_copy(
                    src_ref=s_refs[i].at[to], dst_ref=dst(i, k),
                    send_sem=send_sems.at[i, k - 1], recv_sem=recv_sems.at[i, k - 1],
                    device_id=_peer(k), device_id_type=MESH)
                cp.start()
                copies.append(cp)
        for cp in copies:
            cp.wait_recv()
        for cp in copies:
            cp.wait_send()
        for cp in local:
            cp.wait()

    return _pcall(
        body, name=name,
        in_specs=[pl.BlockSpec(memory_space=pl.ANY)] * ns,
        out_specs=[pl.BlockSpec(memory_space=pl.ANY)] * no,
        out_shape=[jax.ShapeDtypeStruct(s, F32) for s in out_shapes],
        scratch_shapes=[pltpu.SemaphoreType.DMA((ns, N_DEV - 1)), pltpu.SemaphoreType.DMA((ns, N_DEV - 1)),
                        pltpu.SemaphoreType.DMA((ns,))],
    )(*sends)


def _sum_slots(a, name):
    r = a.shape[1]

    def body(a_ref, o_ref):
        acc = a_ref[0]
        for d in range(1, N_DEV):
            acc = acc + a_ref[d]
        o_ref[...] = acc

    return _pcall(body, name=name, out_shape=jax.ShapeDtypeStruct((r, 128), F32), compiler_params=_cparams())(a)


def _all_reduce_small(blk, name):
    r = blk.shape[0]

    def body(x_ref, out_ref, gath, send_sems, recv_sems):
        me = _flat(*_me())
        gath[me] = x_ref[...]
        copies = []
        for k in range(1, N_DEV):
            cp = pltpu.make_async_remote_copy(
                src_ref=x_ref, dst_ref=gath.at[me],
                send_sem=send_sems.at[k - 1], recv_sem=recv_sems.at[k - 1],
                device_id=_peer(k), device_id_type=MESH)
            cp.start()
            copies.append(cp)
        for k in range(1, N_DEV):
            src = _flat(*_peer(k))
            pltpu.make_async_remote_copy(
                src_ref=x_ref, dst_ref=gath.at[src],
                send_sem=send_sems.at[k - 1], recv_sem=recv_sems.at[k - 1],
                device_id=_peer(k), device_id_type=MESH).wait_recv()
        for cp in copies:
            cp.wait_send()
        acc = gath[0]
        for d in range(1, N_DEV):
            acc = acc + gath[d]
        out_ref[...] = acc

    return _pcall(
        body, name=name,
        in_specs=[pl.BlockSpec(memory_space=pltpu.VMEM)],
        out_specs=pl.BlockSpec(memory_space=pltpu.VMEM),
        out_shape=jax.ShapeDtypeStruct((r, 128), F32),
        scratch_shapes=[pltpu.VMEM((N_DEV, r, 128), F32),
                        pltpu.SemaphoreType.DMA((N_DEV - 1,)), pltpu.SemaphoreType.DMA((N_DEV - 1,))],
    )(blk)


def _heads(vec):
    return vec.reshape(NH, 1, DH)


def _rep(vec4):
    return jnp.broadcast_to(vec4.reshape(NH, 1, 1), (NH, 1, DH))


def _onehot_lane(offset):
    m = np.zeros((NH, 1, DH), np.float32)
    for h in range(NH):
        m[h, 0, offset + h] = 1.0
    return jnp.asarray(m)


_TINY = (("gdn_conv_w", (DEPTH, 4, 96)), ("rwkv_w_up", (DEPTH, 64, 32)), ("rwkv_a_up", (DEPTH, 64, 32)),
         ("sc_conv_w", (DEPTH, 3, 32)))
_TINY_ROWS = -(-sum(int(np.prod(s)) for _, s in _TINY) // 1024) * 8


def _pack_rows(arrays, rows, fill=0.0):
    flat = jnp.concatenate([a.reshape(-1) for a in arrays])
    return jnp.pad(flat, (0, rows * 128 - flat.shape[0]), constant_values=fill).reshape(rows, 128)


def _unpack_rows(p, named_shapes):
    lead = p.shape[:-2]
    flat = p.reshape(lead + (-1,))
    out, o = {}, 0
    for n, s in named_shapes:
        size = int(np.prod(s))
        out[n] = flat[..., o:o + size].reshape(lead + tuple(s))
        o += size
    return out


def _gather_last(a):
    return jnp.transpose(a, (1, 0, 2)).reshape(a.shape[1], -1)


def _split_last(a):
    r, c8 = a.shape
    return jnp.transpose(a.reshape(r, N_DEV, c8 // N_DEV), (1, 0, 2))


_SMALL = (("pre_norm_w", (DEPTH, 1024)), ("gdn_a_log", (DEPTH, 4)), ("gdn_dt_bias", (DEPTH, 4)),
          ("gdn_norm_w", (DEPTH, 64)), ("rwkv_mu", (DEPTH, 1152)), ("rwkv_w0", (DEPTH, 256)),
          ("rwkv_a0", (DEPTH, 256)), ("rwkv_k_k", (DEPTH, 256)), ("rwkv_k_a", (DEPTH, 256)),
          ("rwkv_r_k", (DEPTH, 256)), ("rwkv_ln_w", (DEPTH, 256)), ("rwkv_ln_b", (DEPTH, 256)),
          ("gla_a_up", (DEPTH, 16, 128)), ("gla_a_bias", (DEPTH, 128)), ("gla_norm_w", (DEPTH, 64)),
          ("post_norm_w", (DEPTH, 1024)), ("loss", ()))
_SMALL_ROWS = -(-sum(int(np.prod(s)) for _, s in _SMALL) // 1024) * 8


def _layer_params(wts, tiny, w_in_all, w_out_all, l):
    conv = _gather_last(tiny["gdn_conv_w"][:, l])
    q = {}
    q["gdn_conv"] = jnp.transpose(conv.reshape(GDN_TAPS, 12, DH), (1, 0, 2))
    q["gdn_prm"] = [_rep(wts["gdn_a_log"][l]), _rep(wts["gdn_dt_bias"][l]),
                    jnp.broadcast_to(wts["gdn_norm_w"][l].reshape(1, 1, DH), (NH, 1, DH))]
    q["gdn_cst"] = [_onehot_lane(0), _onehot_lane(NH)]
    q["rwkv_mu"] = wts["rwkv_mu"][l].reshape(18, 1, DH)
    w_up = jnp.transpose(_gather_last(tiny["rwkv_w_up"][:, l]).reshape(64, NH, DH), (1, 0, 2))
    a_up = jnp.transpose(_gather_last(tiny["rwkv_a_up"][:, l]).reshape(64, NH, DH), (1, 0, 2))
    q["rwkv_prm"] = [_heads(wts["rwkv_w0"][l]), w_up, _heads(wts["rwkv_a0"][l]), a_up,
                     _heads(wts["rwkv_k_k"][l]), _heads(wts["rwkv_k_a"][l]), _heads(wts["rwkv_r_k"][l]),
                     _heads(wts["rwkv_ln_w"][l]), _heads(wts["rwkv_ln_b"][l])]
    sc = _gather_last(tiny["sc_conv_w"][:, l])
    q["sc_conv"] = jnp.transpose(sc.reshape(SC_TAPS, NH, DH), (1, 0, 2))
    gla_up = jnp.transpose(wts["gla_a_up"][l].reshape(16, NH, GLA_HEAD_K), (1, 0, 2))
    gla_up = jnp.pad(gla_up, ((0, 0), (0, DH - 16), (0, DH - GLA_HEAD_K)))
    gla_b = jnp.pad(wts["gla_a_bias"][l].reshape(NH, 1, GLA_HEAD_K), ((0, 0), (0, 0), (0, DH - GLA_HEAD_K)))
    q["gla_prm"] = [gla_up, gla_b, jnp.broadcast_to(wts["gla_norm_w"][l].reshape(1, 1, DH), (NH, 1, DH))]
    q["w_g"] = _regroup_in(w_in_all, l, f"regroup_in{l}")
    q["wout_g"] = w_out_all[:, l].reshape(4 * NH, DH, D_MODEL).astype(BF16)
    q["pre_w"] = wts["pre_norm_w"][l].reshape(1, D_MODEL)
    q["post_w"] = wts["post_norm_w"][l].reshape(1, D_MODEL)
    return q


def _mixer_inputs(p, cq, pm):
    gdn = [(cq, 4, 0), (cq, 4, 1), (cq, 4, 2), (p, 4, G_GDN // 4 + 3), (p, 1, G_GDN_AB)]
    rwkv = [(pm, 4, 0), (pm, 4, 1), (pm, 4, 2), (pm, 4, 3), (pm, 1, 16), (pm, 1, 17)]
    gla = [(p, 4, G_GLA // 4 + k) for k in range(4)] + [(p, 1, G_GLA_AD)]
    return gdn, rwkv, gla


def _layer_fwd(x, q, nb, t, l):
    hb, p = _norm_proj(x, q["pre_w"], q["w_g"], f"norm_proj{l}")
    cq = _conv_fwd(p, G_GDN, 12, q["gdn_conv"], nb, t, f"gdn_conv{l}")
    pm = _mix_fwd(p, q["rwkv_mu"], nb, t, f"rwkv_mix{l}")
    gdn_in, rwkv_in, gla_in = _mixer_inputs(p, cq, pm)
    y_gdn, ck_gdn = _mixer_fwd(_gdn_chunk, f"gdn_fwd{l}", gdn_in, q["gdn_prm"], q["gdn_cst"], nb, t)
    y_rwkv, ck_rwkv = _mixer_fwd(_rwkv_chunk, f"rwkv_fwd{l}", rwkv_in, q["rwkv_prm"], [], nb, t,
                                 first_fn=_rwkv_chunk_steps)
    y_sc = _sc_fwd(p, q["sc_conv"], nb, t, f"sc_fwd{l}")
    y_gla, ck_gla = _mixer_fwd(_gla_chunk, f"gla_fwd{l}", gla_in, q["gla_prm"], [], nb, t)
    ys = (y_gdn, y_rwkv, y_sc, y_gla)
    out, xn = _out_proj_norm(ys, q["wout_g"], x, q["post_w"], f"out_proj{l}")
    saved = dict(x=x, hb=hb, p=p, cq=cq, pm=pm, ys=ys, out=out, ck=(ck_gdn, ck_rwkv, ck_gla))
    return xn, saved


def _layer_bwd(dxn, q, sv, nb, t, l):
    p, cq, pm, ys = sv["p"], sv["cq"], sv["pm"], sv["ys"]
    dout, dy, d_post = _post_bwd(dxn, sv["out"], q["post_w"], q["wout_g"], f"post_bwd{l}")
    d_wout = _dwout(ys, dout, f"dwout{l}").reshape(N_DEV, 128, D_MODEL)
    gdn_in, rwkv_in, gla_in = _mixer_inputs(p, cq, pm)
    ck_gdn, ck_rwkv, ck_gla = sv["ck"]
    g = {}

    (d_conv, dz, dab), (da_log, ddt, dnw) = _mixer_bwd(
        _gdn_chunk, f"gdn_bwd{l}", gdn_in, q["gdn_prm"], q["gdn_cst"], ck_gdn, dy, 0,
        [(12, F32), (4, BF16), (1, BF16)], [(0, 0), (0, 4), (0, 8), (1, 0), (2, 0)], nb, t)
    dconv_in, d_gconv = _conv_bwd(p, G_GDN, 12, q["gdn_conv"], d_conv, nb, t, f"gdn_conv_bwd{l}")
    g["gdn_conv_w"] = jnp.transpose(d_gconv.sum(1), (1, 0, 2)).reshape(GDN_TAPS, 768)
    g["gdn_a_log"] = da_log.sum((0, 2, 3))
    g["gdn_dt_bias"] = ddt.sum((0, 2, 3))
    g["gdn_norm_w"] = dnw.sum((0, 1, 2))

    (d_pm,), d_rprm = _mixer_bwd(
        _rwkv_chunk, f"rwkv_bwd{l}", rwkv_in, q["rwkv_prm"], [], ck_rwkv, dy, 1,
        [(18, F32)], [(0, 0), (0, 4), (0, 8), (0, 12), (0, 16), (0, 17)], nb, t, first_fn=_rwkv_chunk_steps)
    dp_rwkv, d_mu = _mix_bwd(p, q["rwkv_mu"], d_pm, nb, t, f"rwkv_mix_bwd{l}")
    g["rwkv_mu"] = d_mu.sum(1).reshape(1152)
    rp = [a.sum(0) for a in d_rprm]
    g["rwkv_w0"] = rp[0].reshape(256)
    g["rwkv_w_up"] = jnp.transpose(rp[1], (1, 0, 2)).reshape(64, 256)
    g["rwkv_a0"] = rp[2].reshape(256)
    g["rwkv_a_up"] = jnp.transpose(rp[3], (1, 0, 2)).reshape(64, 256)
    for i, nme in enumerate(("rwkv_k_k", "rwkv_k_a", "rwkv_r_k", "rwkv_ln_w", "rwkv_ln_b")):
        g[nme] = rp[4 + i].reshape(256)

    dp_sc, d_scw = _sc_bwd(p, q["sc_conv"], dy, nb, t, f"sc_bwd{l}")
    g["sc_conv_w"] = jnp.transpose(d_scw.sum(1), (1, 0, 2)).reshape(SC_TAPS, 256)

    (dp_gla, dad), (d_aup, d_ab, d_gnw) = _mixer_bwd(
        _gla_chunk, f"gla_bwd{l}", gla_in, q["gla_prm"], [], ck_gla, dy, 3,
        [(16, BF16), (1, BF16)], [(0, 0), (0, 4), (0, 8), (0, 12), (1, 0)], nb, t)
    g["gla_a_up"] = jnp.transpose(d_aup.sum(0)[:, :16, :GLA_HEAD_K], (1, 0, 2)).reshape(16, 128)
    g["gla_a_bias"] = d_ab.sum(0)[:, 0, :GLA_HEAD_K].reshape(128)
    g["gla_norm_w"] = d_gnw.sum((0, 1, 2))

    singles = jnp.concatenate([dab, dp_rwkv[16:18], dad], axis=0)
    sources = [dconv_in, dz, dp_rwkv, dp_sc, dp_gla, singles]
    dx, d_pre = _dh_prenorm_bwd(sources, q["w_g"], sv["x"], q["pre_w"], dxn, f"dh_bwd{l}")
    d_win = _regroup_out(_dwin(sv["hb"], sources, f"dwin{l}"), f"regroup_out{l}")
    g["pre_norm_w"] = d_pre.reshape(D_MODEL)
    g["post_norm_w"] = d_post.reshape(D_MODEL)
    return dx, g, d_win, d_wout


def _local_step(x, tgt, wts, tiny, w_in_all, w_out_all):
    nb, t, d = x.shape
    xf = x.reshape(nb * t, d)
    qs, saved = [], []
    for l in range(DEPTH):
        q = _layer_params(wts, tiny, w_in_all, w_out_all, l)
        xf, sv = _layer_fwd(xf, q, nb, t, l)
        qs.append(q)
        saved.append(sv)
    dxf, lpart = _loss_grad(xf, tgt.reshape(nb * t, d), "loss")
    grads, d_win, d_wout = [None] * DEPTH, [None] * DEPTH, [None] * DEPTH
    for l in reversed(range(DEPTH)):
        dxf, grads[l], d_win[l], d_wout[l] = _layer_bwd(dxf, qs[l], saved[l], nb, t, l)
    small = {k: jnp.stack([grads[l][k] for l in range(DEPTH)]) for k in grads[0]}
    return lpart[0, 0], dxf.reshape(nb, t, d), small, d_win, d_wout


_WEIGHTS = ("pre_norm_w", "w_in", "gdn_conv_w", "gdn_a_log", "gdn_dt_bias", "gdn_norm_w", "rwkv_mu", "rwkv_w0",
            "rwkv_w_up", "rwkv_a0", "rwkv_a_up", "rwkv_k_k", "rwkv_k_a", "rwkv_r_k", "rwkv_ln_w", "rwkv_ln_b",
            "sc_conv_w", "gla_a_up", "gla_a_bias", "gla_norm_w", "w_out", "post_norm_w")


def kernel(x, pre_norm_w, w_in, gdn_conv_w, gdn_a_log, gdn_dt_bias, gdn_norm_w, rwkv_mu, rwkv_w0, rwkv_w_up, rwkv_a0, rwkv_a_up, rwkv_k_k, rwkv_k_a, rwkv_r_k, rwkv_ln_w, rwkv_ln_b, sc_conv_w, gla_a_up, gla_a_bias, gla_norm_w, w_out, post_norm_w, loss_target, m_pre_norm_w, m_w_in, m_gdn_conv_w, m_gdn_a_log, m_gdn_dt_bias, m_gdn_norm_w, m_rwkv_mu, m_rwkv_w0, m_rwkv_w_up, m_rwkv_a0, m_rwkv_a_up, m_rwkv_k_k, m_rwkv_k_a, m_rwkv_r_k, m_rwkv_ln_w, m_rwkv_ln_b, m_sc_conv_w, m_gla_a_up, m_gla_a_bias, m_gla_norm_w, m_w_out, m_post_norm_w, v_pre_norm_w, v_w_in, v_gdn_conv_w, v_gdn_a_log, v_gdn_dt_bias, v_gdn_norm_w, v_rwkv_mu, v_rwkv_w0, v_rwkv_w_up, v_rwkv_a0, v_rwkv_a_up, v_rwkv_k_k, v_rwkv_k_a, v_rwkv_r_k, v_rwkv_ln_w, v_rwkv_ln_b, v_sc_conv_w, v_gla_a_up, v_gla_a_bias, v_gla_norm_w, v_w_out, v_post_norm_w):
    env = dict(locals())
    w = {n: env[n] for n in _WEIGHTS}
    m = {n: env["m_" + n] for n in _WEIGHTS}
    v = {n: env["v_" + n] for n in _WEIGHTS}
    tiny_names = [n for n, _ in _TINY]

    w_in_all, w_out_all, tiny_all = _all_gather(
        [w_in, w_out, _pack_rows([w[n] for n in tiny_names], _TINY_ROWS)], "gather_weights")
    tiny = _unpack_rows(tiny_all, _TINY)

    lpart, grad_x, small, d_win, d_wout = _local_step(x, loss_target, w, tiny, w_in_all, w_out_all)

    tiny_send = jnp.stack([_pack_rows([_split_last(small[n][l])[d] for n in tiny_names for l in range(DEPTH)],
                                      _TINY_ROWS) for d in range(N_DEV)])
    r_win, r_wout, r_tiny = _exchange(
        [d_win[0], d_win[1], d_wout[0], d_wout[1], tiny_send],
        [(0, 0), (0, 1), (1, 0), (1, 1), (2, None)],
        [(N_DEV, DEPTH, D_MODEL, SHARD_COLS), (N_DEV, DEPTH, 128, D_MODEL), (N_DEV, _TINY_ROWS, 128)],
        "scatter_grads")
    grads, delta, new_m, new_v = {}, {}, {}, {}
    for n, parts in (("w_in", r_win), ("w_out", r_wout)):
        shp = w[n].shape
        two = lambda a: a.reshape(-1, shp[-1])
        res = _sum_adamw(parts.reshape(N_DEV, -1, shp[-1]), two(w[n]), two(m[n]), two(v[n]), "adamw_" + n)
        grads[n], delta[n], new_m[n], new_v[n] = [o.reshape(shp) for o in res]
    tiny_sum = _sum_slots(r_tiny, "sum_tiny").reshape(-1)
    o = 0
    for n, s in _TINY:
        size = int(np.prod(s))
        grads[n] = tiny_sum[o:o + size].reshape(s)
        o += size

    small = dict(small)
    small["loss"] = lpart
    red = _unpack_rows(_all_reduce_small(_pack_rows([small[n] for n, _ in _SMALL], _SMALL_ROWS), "reduce_small"),
                       _SMALL)
    loss = red.pop("loss")
    grads.update(red)

    rest = [n for n in _WEIGHTS if n not in ("w_in", "w_out")]
    rest_shapes = [(n, w[n].shape) for n in rest]
    rows = -(-sum(int(np.prod(s)) for _, s in rest_shapes) // 1024) * 8
    outs = _adamw(_pack_rows([w[n] for n in rest], rows), _pack_rows([grads[n] for n in rest], rows),
                  _pack_rows([m[n] for n in rest], rows), _pack_rows([v[n] for n in rest], rows, 1.0), "adamw_rest")
    for dst, packed in zip((delta, new_m, new_v), outs):
        dst.update(_unpack_rows(packed, rest_shapes))

    return (loss, grad_x, *[grads[n] for n in _WEIGHTS], *[delta[n] for n in _WEIGHTS],
            *[new_m[n] for n in _WEIGHTS], *[new_v[n] for n in _WEIGHTS])
```

```python
import functools
import math

import numpy as np
import jax
import jax.numpy as jnp
from jax import lax
from jax.experimental import pallas as pl
from jax.experimental.pallas import tpu as pltpu

F32 = jnp.float32
BF16 = jnp.bfloat16

D_MODEL = 1024
DEPTH = 2
NH = 4
DH = 64
CH = 64
EPS = 1e-6
RWKV_GN_EPS = 64e-5
GLA_HEAD_K = 32
GLA_TAU = 16.0
GDN_TAPS = 4
SC_TAPS = 3
D_IN = 3992
N_DEV = 8
SHARD_COLS = D_IN // N_DEV

G_GDN = 0
G_RWKV = 16
G_SC = 32
G_GLA = 48
G_GDN_AB, G_RWKV_WD, G_RWKV_AD, G_GLA_AD = 64, 65, 66, 67
N_GROUPS = 68
GROUPS_PER_STEP = 4
TIME_BLOCK = 512

C_GDN, C_RWKV, C_SC, C_GLA = 0, 1032, 2184, 3208

ADAM_LR, ADAM_B1, ADAM_B2, ADAM_EPS, ADAM_WD, ADAM_STEP = 0.001, 0.9, 0.999, 1e-08, 0.01, 10

VMEM_LIMIT = 56 * 1024 * 1024
MESH = pl.DeviceIdType.MESH

_pcall = pl.pallas_call


def _cparams(sem=None):
    if sem is None:
        return pltpu.CompilerParams(vmem_limit_bytes=VMEM_LIMIT)
    return pltpu.CompilerParams(dimension_semantics=sem, vmem_limit_bytes=VMEM_LIMIT)


def _group_segments():
    table = [(G_GDN + i, C_GDN + DH * i, DH) for i in range(16)]
    table.append((G_GDN_AB, C_GDN + 1024, 8))
    table += [(G_RWKV + i, C_RWKV + DH * i, DH) for i in range(16)]
    table += [(G_RWKV_WD, C_RWKV + 1024, DH), (G_RWKV_AD, C_RWKV + 1088, DH)]
    table += [(G_SC + 4 * j + k, C_SC + 256 * k + DH * j, DH) for j in range(NH) for k in range(4)]
    for h in range(NH):
        table += [(G_GLA + h, C_GLA + GLA_HEAD_K * h, GLA_HEAD_K),
                  (G_GLA + 4 + h, C_GLA + 128 + GLA_HEAD_K * h, GLA_HEAD_K),
                  (G_GLA + 8 + h, C_GLA + 256 + DH * h, DH),
                  (G_GLA + 12 + h, C_GLA + 512 + DH * h, DH)]
    table.append((G_GLA_AD, C_GLA + 768, 16))
    segs, padded = [], []
    for g, c, n in table:
        if n < DH:
            padded.append(g)
        a = 0
        while n > 0:
            d, off = divmod(c, SHARD_COLS)
            ln = min(n, SHARD_COLS - off)
            segs.append((g, a, d, off, ln))
            c, a, n = c + ln, a + ln, n - ln
    return segs, padded


_SEGMENTS, _PADDED_GROUPS = _group_segments()


def _dn(ta, tb):
    return (((1 if ta else 2,), (2 if tb else 1,)), ((0,), (0,)))


def _hdot(a, b, ta=False, tb=False):
    return lax.dot_general(a, b, _dn(ta, tb), precision=lax.Precision.HIGH, preferred_element_type=F32)


def _r(x):
    return x.astype(BF16)


def _rdot(a, b, ta=False, tb=False):
    return lax.dot_general(_r(a), _r(b), _dn(ta, tb), preferred_element_type=F32)


@jax.custom_vjp
def _bmm(a, b):
    return _rdot(a, b)


def _bmm_fwd(a, b):
    return _rdot(a, b), (a, b)


def _bmm_bwd(res, g):
    a, b = res
    return _rdot(g, b, tb=True), _rdot(a, g, ta=True)


_bmm.defvjp(_bmm_fwd, _bmm_bwd)


@jax.custom_vjp
def _bmm_nt(a, b):
    return _rdot(a, b, tb=True)


def _bmm_nt_fwd(a, b):
    return _rdot(a, b, tb=True), (a, b)


def _bmm_nt_bwd(res, g):
    a, b = res
    return _rdot(g, b), _rdot(g, a, ta=True)


_bmm_nt.defvjp(_bmm_nt_fwd, _bmm_nt_bwd)


@jax.custom_vjp
def _bmm_tn(a, b):
    return _rdot(a, b, ta=True)


def _bmm_tn_fwd(a, b):
    return _rdot(a, b, ta=True), (a, b)


def _bmm_tn_bwd(res, g):
    a, b = res
    return _rdot(b, g, tb=True), _rdot(a, g)


_bmm_tn.defvjp(_bmm_tn_fwd, _bmm_tn_bwd)


def _tri(n):
    i = lax.broadcasted_iota(jnp.int32, (n, n), 0)
    j = lax.broadcasted_iota(jnp.int32, (n, n), 1)
    return i >= j, i > j, i == j


def _heads_of(x, like):
    return jnp.broadcast_to(x[None], (like.shape[0],) + x.shape)


def _cumsum_rows(x):
    incl, _, _ = _tri(CH)
    return _hdot(_heads_of(incl.astype(F32), x), x)


def _inv_unit_lower(a):
    n = a.shape[-1]
    _, _, eye = _tri(n)
    pw = -a
    inv = eye.astype(F32) + pw
    for _ in range(int(math.log2(n)) - 1):
        pw = _hdot(pw, pw)
        inv = inv + _hdot(inv, pw)
    return inv


def _silu(x):
    return x * jax.nn.sigmoid(x)


def _t(x):
    return jnp.swapaxes(x, -1, -2)


def _gdn_chunk(prm, cst, ins, s):
    a_log, dt_b, nw = prm
    m_a, m_b = cst
    cq, ck, cv, z, ab = ins
    incl, strict, _ = _tri(CH)
    q = _silu(cq)
    k = _silu(ck)
    v = _silu(cv)
    q = q * lax.rsqrt(jnp.sum(q * q, -1, keepdims=True) + EPS) * (DH ** -0.5)
    k = k * lax.rsqrt(jnp.sum(k * k, -1, keepdims=True) + EPS)
    a_raw = jnp.sum(ab * m_a, -1, keepdims=True)
    b_raw = jnp.sum(ab * m_b, -1, keepdims=True)
    gstep = -jnp.exp(a_log) * jax.nn.softplus(a_raw + dt_b)
    beta = jax.nn.sigmoid(b_raw)
    gc = _cumsum_rows(gstep)
    gl = jnp.sum(gstep, -2, keepdims=True)
    dec = jnp.where(incl, jnp.exp(jnp.where(incl, gc - _t(gc), 0.0)), 0.0)
    kb = k * beta
    a_mat = jnp.where(strict, _bmm_nt(kb, k) * dec, 0.0)
    tinv = _inv_unit_lower(a_mat)
    eg = jnp.exp(gc)
    u = _hdot(tinv, v * beta)
    w = _hdot(tinv, kb * eg)
    attn = _bmm_nt(q, k) * dec
    v_new = u - _bmm(w, s)
    o = _bmm(q * eg, s) + _bmm(attn, v_new)
    s_next = s * jnp.exp(gl) + _bmm_tn(k * jnp.exp(gl - gc), v_new)
    on = o * lax.rsqrt(jnp.mean(o * o, -1, keepdims=True) + EPS) * nw
    return on * _silu(z), s_next


def _gla_chunk(prm, cst, ins, st):
    a_up, a_bias, nw = prm
    q, k, v, z, ad = ins
    incl, _, _ = _tri(CH)
    la = jax.nn.log_sigmoid(_bmm(_heads_of(ad, a_up), a_up) + a_bias) * (1.0 / GLA_TAU)
    bc = _cumsum_rows(la)
    bl = jnp.sum(la, -2, keepdims=True)
    qe = q * (GLA_HEAD_K ** -0.5) * jnp.exp(bc)
    ke = k * jnp.exp(-bc)
    attn = jnp.where(incl, _bmm_nt(qe, ke), 0.0)
    o = _bmm_nt(qe, st) + _bmm(attn, v)
    st_next = st * jnp.exp(bl) + _bmm_tn(v, k * jnp.exp(bl - bc))
    on = o * lax.rsqrt(jnp.mean(o * o, -1, keepdims=True) + EPS) * nw
    return on * _silu(z), st_next


def _rwkv_chunk(prm, cst, ins, s):
    r, v = ins[0], ins[2]
    incl, strict, _ = _tri(CH)
    lw, kk, k2, m = _rwkv_pre(prm, ins)
    cum = _cumsum_rows(lw)
    ltot = jnp.sum(lw, -2, keepdims=True)
    n_t = -kk * jnp.exp(cum - lw)
    einv = jnp.exp(-cum)
    m_t = m * einv
    k_t = k2 * einv
    r_t = r * jnp.exp(cum)
    a_nm = jnp.where(strict, _hdot(n_t, m_t, tb=True), 0.0)
    a_nk = jnp.where(strict, _hdot(n_t, k_t, tb=True), 0.0)
    cm = _hdot(_inv_unit_lower(-a_nm), _hdot(n_t, s, tb=True) + _hdot(a_nk, v))
    y = (_hdot(r_t, s, tb=True) + _hdot(jnp.where(incl, _hdot(r_t, m_t, tb=True), 0.0), cm)
         + _hdot(jnp.where(incl, _hdot(r_t, k_t, tb=True), 0.0), v))
    eend = jnp.exp(ltot - cum)
    s_next = s * jnp.exp(ltot) + _hdot(cm, m * eend, ta=True) + _hdot(v, k2 * eend, ta=True)
    return _rwkv_post(prm, ins, y, k2), s_next


def _rwkv_pre(prm, ins):
    w0, w_up, a0, a_up, k_k, k_a = prm[:6]
    k, wd, ad = ins[1], ins[4], ins[5]
    lw = -math.exp(-0.5) * jax.nn.sigmoid(w0 + _bmm(_heads_of(jnp.tanh(wd), w_up), w_up))
    a = jax.nn.sigmoid(a0 + _bmm(_heads_of(ad, a_up), a_up))
    kk = k * k_k
    kk = kk * lax.rsqrt(jnp.sum(kk * kk, -1, keepdims=True) + EPS)
    k2 = k * (1.0 + (a - 1.0) * k_a)
    return lw, kk, k2, kk * a


def _rwkv_post(prm, ins, y, k2):
    r_k, ln_w, ln_b = prm[6:]
    r, v, z = ins[0], ins[2], ins[3]
    mean = jnp.mean(y, -1, keepdims=True)
    yc = y - mean
    var = jnp.mean(yc * yc, -1, keepdims=True)
    yn = yc * lax.rsqrt(var + RWKV_GN_EPS) * ln_w + ln_b
    bonus = jnp.sum(r * k2 * r_k, -1, keepdims=True) * v
    return (yn + bonus) * _silu(z)


@jax.custom_vjp
def _bmv(s, x):
    return jnp.sum(_r(s).astype(F32) * _r(x).astype(F32), -1, keepdims=True)


def _bmv_fwd(s, x):
    return _bmv(s, x), (s, x)


def _bmv_bwd(res, g):
    s, x = res
    return g * x, jnp.sum(_r(s).astype(F32) * _r(g).astype(F32), -2, keepdims=True)


_bmv.defvjp(_bmv_fwd, _bmv_bwd)


def _rwkv_chunk_steps(prm, cst, ins, s):
    r, v = ins[0], ins[2]
    lw, kk, k2, m = _rwkv_pre(prm, ins)
    w = jnp.exp(lw)
    v_t = _t(v)
    lane = lax.broadcasted_iota(jnp.int32, (1, 1, CH), 2)
    y_t = jnp.zeros((s.shape[0], DH, CH), F32)
    for t in range(CH):
        e_t = (lane == t).astype(F32)
        row = (slice(None), slice(t, t + 1))
        sa = _bmv(s, -kk[row])
        s = s * w[row] + sa * m[row] + jnp.sum(v_t * e_t, -1, keepdims=True) * k2[row]
        y_t = y_t + _bmv(s, r[row]) * e_t
    return _rwkv_post(prm, ins, _t(y_t), k2), s


def _time_block(t):
    return TIME_BLOCK if t % TIME_BLOCK == 0 else t


def _load_chunk(ref, hs, i):
    return ref[hs, pl.ds(i, CH), :] if ref.shape[0] == NH else ref[0, pl.ds(i, CH), :]


def _each_head(fn):
    def step(h, carry):
        fn(pl.ds(h, 1))
        return carry

    lax.fori_loop(0, NH, step, 0)


def _mixer_fwd(chunk_fn, name, ins, prm, cst, nb, t, first_fn=None):
    tb = _time_block(t)
    nt, ncb = t // tb, tb // CH
    n_in, n_prm, n_cst = len(ins), len(prm), len(cst)

    def body(*refs):
        in_refs = refs[:n_in]
        prm_refs = refs[n_in:n_in + n_prm]
        cst_refs = refs[n_in + n_prm:n_in + n_prm + n_cst]
        y_ref, ck_ref, s_scr = refs[n_in + n_prm + n_cst:]
        step_t = pl.program_id(1)

        @pl.when(step_t == 0)
        def _():
            s_scr[...] = jnp.zeros_like(s_scr)

        def chunk(c, i, fn, hs=slice(None)):
            s = s_scr[hs]
            ck_ref[c, hs] = s
            y, s_next = fn([r[hs] for r in prm_refs], [r[hs] for r in cst_refs],
                           [_load_chunk(r, hs, i) for r in in_refs], s)
            y_ref[hs, pl.ds(i, CH), :] = y.astype(BF16)
            s_scr[hs] = s_next

        def step(c, carry):
            chunk(c, pl.multiple_of(c * CH, CH), chunk_fn)
            return carry

        if first_fn is None:
            lax.fori_loop(0, ncb, step, 0)
        else:
            @pl.when(step_t == 0)
            def _():
                _each_head(lambda hs: chunk(0, 0, first_fn, hs))

            @pl.when(step_t != 0)
            def _():
                chunk(0, 0, chunk_fn)

            lax.fori_loop(1, ncb, step, 0)

    in_specs = [pl.BlockSpec((ng, tb, DH), (lambda b, j, bi=bi: (bi, b * nt + j, 0))) for _, ng, bi in ins]
    in_specs += [pl.BlockSpec(p.shape, lambda b, j: (0, 0, 0)) for p in list(prm) + list(cst)]
    return _pcall(
        body, name=name, grid=(nb, nt),
        in_specs=in_specs,
        out_specs=[pl.BlockSpec((NH, tb, DH), lambda b, j: (0, b * nt + j, 0)),
                   pl.BlockSpec((None, ncb, NH, DH, DH), lambda b, j: (b, j, 0, 0, 0))],
        out_shape=[jax.ShapeDtypeStruct((NH, nb * t, DH), BF16),
                   jax.ShapeDtypeStruct((nb, t // CH, NH, DH, DH), F32)],
        scratch_shapes=[pltpu.VMEM((NH, DH, DH), F32)],
        compiler_params=_cparams(("parallel", "arbitrary")),
    )(*[a for a, _, _ in ins], *prm, *cst)


def _mixer_bwd(chunk_fn, name, ins, prm, cst, ck, dy, dy_block, outs, routes, nb, t, first_fn=None):
    tb = _time_block(t)
    nt, ncb = t // tb, tb // CH
    n_in, n_prm, n_cst, n_out = len(ins), len(prm), len(cst), len(outs)

    def body(*refs):
        in_refs = refs[:n_in]
        prm_refs = refs[n_in:n_in + n_prm]
        cst_refs = refs[n_in + n_prm:n_in + n_prm + n_cst]
        ck_ref, dy_ref = refs[n_in + n_prm + n_cst:n_in + n_prm + n_cst + 2]
        rest = refs[n_in + n_prm + n_cst + 2:]
        out_refs = rest[:n_out]
        dprm_refs = rest[n_out:n_out + n_prm]
        ds_scr = rest[n_out + n_prm]
        step_t = pl.program_id(1)

        @pl.when(step_t == 0)
        def _():
            ds_scr[...] = jnp.zeros_like(ds_scr)
            for r in dprm_refs:
                r[...] = jnp.zeros_like(r)

        def chunk(c, i, fn, hs=slice(None)):
            cst_v = [r[hs] for r in cst_refs]
            _, vjp = jax.vjp(lambda p, x, s: fn(p, cst_v, x, s), [r[hs] for r in prm_refs],
                             [_load_chunk(r, hs, i) for r in in_refs], ck_ref[c, hs])
            d_prm, d_ins, d_s = vjp((dy_ref[hs, pl.ds(i, CH), :], ds_scr[hs]))
            for (oi, g0), g in zip(routes, d_ins):
                o_ref = out_refs[oi]
                if g.ndim == 3:
                    o_ref.at[g0:g0 + NH][hs, pl.ds(i, CH), :] = g.astype(o_ref.dtype)
                elif isinstance(hs, slice):
                    o_ref[g0, pl.ds(i, CH), :] = g.astype(o_ref.dtype)
                else:
                    o_ref[g0, pl.ds(i, CH), :] += g.astype(o_ref.dtype)
            for r, g in zip(dprm_refs, d_prm):
                r[hs] += g
            ds_scr[hs] = d_s

        def first_chunk():
            for (oi, g0), r in zip(routes, in_refs):
                if r.shape[0] != NH:
                    out_refs[oi][g0, pl.ds(0, CH), :] = jnp.zeros((CH, DH), out_refs[oi].dtype)
            _each_head(lambda hs: chunk(0, 0, first_fn, hs))

        def step(j, carry):
            c = ncb - 1 - j
            chunk(c, pl.multiple_of(c * CH, CH), chunk_fn)
            return carry

        lax.fori_loop(0, ncb - 1, step, 0)
        if first_fn is None:
            chunk(0, 0, chunk_fn)
        else:
            @pl.when(step_t == nt - 1)
            def _():
                first_chunk()

            @pl.when(step_t != nt - 1)
            def _():
                chunk(0, 0, chunk_fn)

    def rows(b, j):
        return b * nt + (nt - 1 - j)

    in_specs = [pl.BlockSpec((ng, tb, DH), (lambda b, j, bi=bi: (bi, rows(b, j), 0))) for _, ng, bi in ins]
    in_specs += [pl.BlockSpec(p.shape, lambda b, j: (0, 0, 0)) for p in list(prm) + list(cst)]
    in_specs += [pl.BlockSpec((None, ncb, NH, DH, DH), lambda b, j: (b, nt - 1 - j, 0, 0, 0)),
                 pl.BlockSpec((NH, tb, DH), lambda b, j: (dy_block, rows(b, j), 0))]
    out_specs = [pl.BlockSpec((ng, tb, DH), lambda b, j: (0, rows(b, j), 0)) for ng, _ in outs]
    out_specs += [pl.BlockSpec((None,) + p.shape, lambda b, j: (b, 0, 0, 0)) for p in prm]
    out_shape = [jax.ShapeDtypeStruct((ng, nb * t, DH), dt) for ng, dt in outs]
    out_shape += [jax.ShapeDtypeStruct((nb,) + p.shape, F32) for p in prm]
    res = _pcall(
        body, name=name, grid=(nb, nt),
        in_specs=in_specs, out_specs=out_specs, out_shape=out_shape,
        scratch_shapes=[pltpu.VMEM((NH, DH, DH), F32)],
        compiler_params=_cparams(("parallel", "arbitrary")),
    )(*[a for a, _, _ in ins], *prm, *cst, ck, dy)
    return res[:n_out], res[n_out:]


def _shift_down(x, s):
    if s == 0:
        return x
    row = lax.broadcasted_iota(jnp.int32, x.shape, 0)
    return jnp.where(row < s, 0.0, pltpu.roll(x, s, 0))


def _shift_up(x, s):
    if s == 0:
        return x
    t = x.shape[0]
    row = lax.broadcasted_iota(jnp.int32, x.shape, 0)
    return jnp.where(row >= t - s, 0.0, pltpu.roll(x, t - s, 0))


def _conv_fwd(p, g0, ng, w, nb, t, name):
    taps = w.shape[1]

    def body(x_ref, w_ref, y_ref):
        x = x_ref[...]
        acc = w_ref[taps - 1:taps, :] * x
        for i in range(taps - 1):
            acc = acc + w_ref[i:i + 1, :] * _shift_down(x, taps - 1 - i)
        y_ref[...] = acc

    return _pcall(
        body, name=name, grid=(ng, nb),
        in_specs=[pl.BlockSpec((None, t, DH), lambda g, b: (g0 + g, b, 0)),
                  pl.BlockSpec((None, taps, DH), lambda g, b: (g, 0, 0))],
        out_specs=pl.BlockSpec((None, t, DH), lambda g, b: (g, b, 0)),
        out_shape=jax.ShapeDtypeStruct((ng, nb * t, DH), F32),
        compiler_params=_cparams(("parallel", "parallel")),
    )(p, w)


def _conv_bwd(p, g0, ng, w, dy, nb, t, name):
    taps = w.shape[1]

    def body(x_ref, w_ref, dy_ref, dx_ref, dw_ref):
        x = x_ref[...]
        d = dy_ref[...]
        acc = w_ref[taps - 1:taps, :] * d
        dw_ref[taps - 1:taps, :] = jnp.sum(d * x, 0, keepdims=True)
        for i in range(taps - 1):
            s = taps - 1 - i
            acc = acc + w_ref[i:i + 1, :] * _shift_up(d, s)
            dw_ref[i:i + 1, :] = jnp.sum(d * _shift_down(x, s), 0, keepdims=True)
        dx_ref[...] = acc.astype(BF16)

    return _pcall(
        body, name=name, grid=(ng, nb),
        in_specs=[pl.BlockSpec((None, t, DH), lambda g, b: (g0 + g, b, 0)),
                  pl.BlockSpec((None, taps, DH), lambda g, b: (g, 0, 0)),
                  pl.BlockSpec((None, t, DH), lambda g, b: (g, b, 0))],
        out_specs=[pl.BlockSpec((None, t, DH), lambda g, b: (g, b, 0)),
                   pl.BlockSpec((None, None, taps, DH), lambda g, b: (g, b, 0, 0))],
        out_shape=[jax.ShapeDtypeStruct((ng, nb * t, DH), BF16),
                   jax.ShapeDtypeStruct((ng, nb, taps, DH), F32)],
        compiler_params=_cparams(("parallel", "parallel")),
    )(p, w, dy)


def _mix_group(g):
    return jnp.where(g < 16, G_RWKV + g, G_RWKV_WD + g - 16)


def _mix_fwd(p, mu, nb, t, name):
    def body(x_ref, mu_ref, y_ref):
        x = x_ref[...]
        y_ref[...] = x + mu_ref[...] * (_shift_down(x, 1) - x)

    return _pcall(
        body, name=name, grid=(18, nb),
        in_specs=[pl.BlockSpec((None, t, DH), lambda g, b: (_mix_group(g), b, 0)),
                  pl.BlockSpec((None, 1, DH), lambda g, b: (g, 0, 0))],
        out_specs=pl.BlockSpec((None, t, DH), lambda g, b: (g, b, 0)),
        out_shape=jax.ShapeDtypeStruct((18, nb * t, DH), F32),
        compiler_params=_cparams(("parallel", "parallel")),
    )(p, mu)


def _mix_bwd(p, mu, dy, nb, t, name):
    def body(x_ref, mu_ref, dy_ref, dx_ref, dmu_ref):
        x = x_ref[...]
        muv = mu_ref[...]
        d = dy_ref[...]
        dx_ref[...] = (d * (1.0 - muv) + _shift_up(d * muv, 1)).astype(BF16)
        dmu_ref[...] = jnp.sum(d * (_shift_down(x, 1) - x), 0, keepdims=True)

    return _pcall(
        body, name=name, grid=(18, nb),
        in_specs=[pl.BlockSpec((None, t, DH), lambda g, b: (_mix_group(g), b, 0)),
                  pl.BlockSpec((None, 1, DH), lambda g, b: (g, 0, 0)),
                  pl.BlockSpec((None, t, DH), lambda g, b: (g, b, 0))],
        out_specs=[pl.BlockSpec((None, t, DH), lambda g, b: (g, b, 0)),
                   pl.BlockSpec((None, None, 1, DH), lambda g, b: (g, b, 0, 0))],
        out_shape=[jax.ShapeDtypeStruct((18, nb * t, DH), BF16),
                   jax.ShapeDtypeStruct((18, nb, 1, DH), F32)],
        compiler_params=_cparams(("parallel", "parallel")),
    )(p, mu, dy)


def _sc_fwd(p, w, nb, t, name):
    def body(p_ref, w_ref, y_ref):
        u = p_ref[1] * p_ref[2]
        conv = w_ref[2:3, :] * u + w_ref[1:2, :] * _shift_down(u, 1) + w_ref[0:1, :] * _shift_down(u, 2)
        y_ref[...] = (p_ref[0] * conv * _silu(p_ref[3])).astype(BF16)

    return _pcall(
        body, name=name, grid=(NH, nb),
        in_specs=[pl.BlockSpec((4, t, DH), lambda j, b: (G_SC // 4 + j, b, 0)),
                  pl.BlockSpec((None, SC_TAPS, DH), lambda j, b: (j, 0, 0))],
        out_specs=pl.BlockSpec((None, t, DH), lambda j, b: (j, b, 0)),
        out_shape=jax.ShapeDtypeStruct((NH, nb * t, DH), BF16),
        compiler_params=_cparams(("parallel", "parallel")),
    )(p, w)


def _sc_bwd(p, w, dy, nb, t, name):
    def body(p_ref, w_ref, dy_ref, dp_ref, dw_ref):
        bg, cg, xg, z = p_ref[0], p_ref[1], p_ref[2], p_ref[3]
        d = dy_ref[...]
        u = cg * xg
        u1 = _shift_down(u, 1)
        u2 = _shift_down(u, 2)
        conv = w_ref[2:3, :] * u + w_ref[1:2, :] * u1 + w_ref[0:1, :] * u2
        sg = jax.nn.sigmoid(z)
        sz = z * sg
        dp_ref[0] = (d * conv * sz).astype(BF16)
        dp_ref[3] = (d * bg * conv * (sg * (1.0 + z * (1.0 - sg)))).astype(BF16)
        dconv = d * bg * sz
        du = w_ref[2:3, :] * dconv + w_ref[1:2, :] * _shift_up(dconv, 1) + w_ref[0:1, :] * _shift_up(dconv, 2)
        dp_ref[1] = (du * xg).astype(BF16)
        dp_ref[2] = (du * cg).astype(BF16)
        dw_ref[2:3, :] = jnp.sum(dconv * u, 0, keepdims=True)
        dw_ref[1:2, :] = jnp.sum(dconv * u1, 0, keepdims=True)
        dw_ref[0:1, :] = jnp.sum(dconv * u2, 0, keepdims=True)

    return _pcall(
        body, name=name, grid=(NH, nb),
        in_specs=[pl.BlockSpec((4, t, DH), lambda j, b: (G_SC // 4 + j, b, 0)),
                  pl.BlockSpec((None, SC_TAPS, DH), lambda j, b: (j, 0, 0)),
                  pl.BlockSpec((None, t, DH), lambda j, b: (8 + j, b, 0))],
        out_specs=[pl.BlockSpec((4, t, DH), lambda j, b: (j, b, 0)),
                   pl.BlockSpec((None, None, SC_TAPS, DH), lambda j, b: (j, b, 0, 0))],
        out_shape=[jax.ShapeDtypeStruct((4 * NH, nb * t, DH), BF16),
                   jax.ShapeDtypeStruct((NH, nb, SC_TAPS, DH), F32)],
        compiler_params=_cparams(("parallel", "parallel")),
    )(p, w, dy)


def _row_tile(n):
    return 512 if n % 512 == 0 else n


def _regroup_in(w_all, l, name):
    tr = 256
    gs = GROUPS_PER_STEP

    def body(w_ref, o_ref):
        for g in _PADDED_GROUPS:
            o_ref[g // gs, :, DH * (g % gs):DH * (g % gs + 1)] = jnp.zeros((tr, DH), BF16)
        for g, a, d, off, ln in _SEGMENTS:
            lane = DH * (g % gs) + a
            o_ref[g // gs, :, lane:lane + ln] = w_ref[d, :, off:off + ln].astype(BF16)

    return _pcall(
        body, name=name, grid=(D_MODEL // tr,),
        in_specs=[pl.BlockSpec((N_DEV, None, tr, SHARD_COLS), lambda i: (0, l, i, 0))],
        out_specs=pl.BlockSpec((N_GROUPS // gs, tr, gs * DH), lambda i: (0, i, 0)),
        out_shape=jax.ShapeDtypeStruct((N_GROUPS // gs, D_MODEL, gs * DH), BF16),
        compiler_params=_cparams(("parallel",)),
    )(w_all)


def _regroup_out(dwg, name):
    tr = 256
    gs = GROUPS_PER_STEP

    def body(g_ref, o_ref):
        for g, a, d, off, ln in _SEGMENTS:
            lane = DH * (g % gs) + a
            o_ref[d, :, off:off + ln] = g_ref[g // gs, :, lane:lane + ln]

    return _pcall(
        body, name=name, grid=(D_MODEL // tr,),
        in_specs=[pl.BlockSpec((N_GROUPS // gs, tr, gs * DH), lambda i: (0, i, 0))],
        out_specs=pl.BlockSpec((N_DEV, tr, SHARD_COLS), lambda i: (0, i, 0)),
        out_shape=jax.ShapeDtypeStruct((N_DEV, D_MODEL, SHARD_COLS), F32),
        compiler_params=_cparams(("parallel",)),
    )(dwg)


def _norm_proj(x, pre_w, w_g, name):
    n = x.shape[0]
    tm = _row_tile(n)
    gs = GROUPS_PER_STEP

    def body(x_ref, pw_ref, w_ref, h_ref, p_ref):
        @pl.when(pl.program_id(1) == 0)
        def _():
            xv = x_ref[...]
            h = xv * lax.rsqrt(jnp.mean(xv * xv, -1, keepdims=True) + EPS) * pw_ref[...]
            h_ref[...] = h.astype(BF16)

        r = jnp.dot(h_ref[...], w_ref[...], preferred_element_type=F32)
        for k in range(gs):
            p_ref[k] = r[:, DH * k:DH * (k + 1)]

    return _pcall(
        body, name=name, grid=(n // tm, N_GROUPS // gs),
        in_specs=[pl.BlockSpec((tm, D_MODEL), lambda i, j: (i, 0)),
                  pl.BlockSpec((1, D_MODEL), lambda i, j: (0, 0)),
                  pl.BlockSpec((None, D_MODEL, gs * DH), lambda i, j: (j, 0, 0))],
        out_specs=[pl.BlockSpec((tm, D_MODEL), lambda i, j: (i, 0)),
                   pl.BlockSpec((gs, tm, DH), lambda i, j: (j, i, 0))],
        out_shape=[jax.ShapeDtypeStruct((n, D_MODEL), BF16),
                   jax.ShapeDtypeStruct((N_GROUPS, n, DH), F32)],
        compiler_params=_cparams(("parallel", "arbitrary")),
    )(x, pre_w, w_g)


def _out_proj_norm(ys, wout_g, x, post_w, name):
    n = x.shape[0]
    tm = _row_tile(n)

    def body(y0, y1, y2, y3, w_ref, x_ref, pw_ref, out_ref, xn_ref):
        acc = jnp.zeros((tm, D_MODEL), F32)
        for m, yr in enumerate((y0, y1, y2, y3)):
            for h in range(NH):
                acc = acc + jnp.dot(yr[h], w_ref[m * NH + h], preferred_element_type=F32)
        out_ref[...] = acc
        xn_ref[...] = x_ref[...] + acc * lax.rsqrt(jnp.mean(acc * acc, -1, keepdims=True) + EPS) * pw_ref[...]

    yspec = pl.BlockSpec((NH, tm, DH), lambda i: (0, i, 0))
    rows = pl.BlockSpec((tm, D_MODEL), lambda i: (i, 0))
    return _pcall(
        body, name=name, grid=(n // tm,),
        in_specs=[yspec] * 4 + [pl.BlockSpec((4 * NH, DH, D_MODEL), lambda i: (0, 0, 0)), rows,
                                pl.BlockSpec((1, D_MODEL), lambda i: (0, 0))],
        out_specs=[rows, rows],
        out_shape=[jax.ShapeDtypeStruct((n, D_MODEL), F32)] * 2,
        compiler_params=_cparams(("parallel",)),
    )(*ys, wout_g, x, post_w)


def _loss_grad(x, tgt, name):
    n = x.shape[0]
    tm = _row_tile(n)

    def body(x_ref, t_ref, dx_ref, l_ref):
        @pl.when(pl.program_id(0) == 0)
        def _():
            l_ref[...] = jnp.zeros_like(l_ref)

        e = x_ref[...] - t_ref[...]
        dx_ref[...] = e * (1.0 / D_MODEL)
        l_ref[...] += jnp.sum(jnp.sum(e * e, -1, keepdims=True), 0, keepdims=True) * (0.5 / D_MODEL)

    rows = pl.BlockSpec((tm, D_MODEL), lambda i: (i, 0))
    return _pcall(
        body, name=name, grid=(n // tm,),
        in_specs=[rows, rows],
        out_specs=[rows, pl.BlockSpec((1, 128), lambda i: (0, 0))],
        out_shape=[jax.ShapeDtypeStruct((n, D_MODEL), F32), jax.ShapeDtypeStruct((1, 128), F32)],
        compiler_params=_cparams(("arbitrary",)),
    )(x, tgt)


def _rmsnorm_bwd(xv, w, d):
    r = lax.rsqrt(jnp.mean(xv * xv, -1, keepdims=True) + EPS)
    xh = xv * r
    dxh = d * w
    dx = r * (dxh - xh * jnp.mean(dxh * xh, -1, keepdims=True))
    return dx, d * xh


def _post_bwd(dxn, out, post_w, wout_g, name):
    n = dxn.shape[0]
    tm = _row_tile(n)

    def body(d_ref, o_ref, pw_ref, w_ref, do_ref, dy_ref, dpw_ref):
        @pl.when(pl.program_id(0) == 0)
        def _():
            dpw_ref[...] = jnp.zeros_like(dpw_ref)

        dout, dw_rows = _rmsnorm_bwd(o_ref[...], pw_ref[...], d_ref[...])
        dpw_ref[...] += jnp.sum(dw_rows, 0, keepdims=True)
        db = dout.astype(BF16)
        do_ref[...] = db
        for g in range(4 * NH):
            dy_ref[g] = lax.dot_general(db, w_ref[g], (((1,), (1,)), ((), ())), preferred_element_type=F32)

    rows = pl.BlockSpec((tm, D_MODEL), lambda i: (i, 0))
    vec = pl.BlockSpec((1, D_MODEL), lambda i: (0, 0))
    return _pcall(
        body, name=name, grid=(n // tm,),
        in_specs=[rows, rows, vec, pl.BlockSpec((4 * NH, DH, D_MODEL), lambda i: (0, 0, 0))],
        out_specs=[rows, pl.BlockSpec((4 * NH, tm, DH), lambda i: (0, i, 0)), vec],
        out_shape=[jax.ShapeDtypeStruct((n, D_MODEL), BF16),
                   jax.ShapeDtypeStruct((4 * NH, n, DH), F32),
                   jax.ShapeDtypeStruct((1, D_MODEL), F32)],
        compiler_params=_cparams(("arbitrary",)),
    )(dxn, out, post_w, wout_g)


def _dwout(ys, dout, name):
    n = dout.shape[0]
    tm = _row_tile(n)

    def body(y0, y1, y2, y3, d_ref, dw_ref):
        @pl.when(pl.program_id(0) == 0)
        def _():
            dw_ref[...] = jnp.zeros_like(dw_ref)

        d = d_ref[...]
        for m, yr in enumerate((y0, y1, y2, y3)):
            for h in range(NH):
                dw_ref[m * NH + h] += lax.dot_general(yr[h], d, (((0,), (0,)), ((), ())),
                                                      preferred_element_type=F32)

    yspec = pl.BlockSpec((NH, tm, DH), lambda i: (0, i, 0))
    return _pcall(
        body, name=name, grid=(n // tm,),
        in_specs=[yspec] * 4 + [pl.BlockSpec((tm, D_MODEL), lambda i: (i, 0))],
        out_specs=pl.BlockSpec((4 * NH, DH, D_MODEL), lambda i: (0, 0, 0)),
        out_shape=jax.ShapeDtypeStruct((4 * NH, DH, D_MODEL), F32),
        compiler_params=_cparams(("arbitrary",)),
    )(*ys, dout)


def _source_specs(sources, rows_first):
    gs = GROUPS_PER_STEP
    spans, specs, j0 = [], [], 0
    for a in sources:
        nblk = a.shape[0] // gs
        spans.append((j0, j0 + nblk))
        shape = (gs, _row_tile(a.shape[1]), DH)

        def blk(j, j0=j0, nblk=nblk):
            return jnp.clip(j - j0, 0, nblk - 1)

        if rows_first:
            specs.append(pl.BlockSpec(shape, (lambda i, j, blk=blk: (blk(j), i, 0))))
        else:
            specs.append(pl.BlockSpec(shape, (lambda j, i, blk=blk: (blk(j), i, 0))))
        j0 += nblk
    return spans, specs


def _dh_prenorm_bwd(sources, w_g, x, pre_w, dxn, name):
    n = x.shape[0]
    tm = _row_tile(n)
    gs = GROUPS_PER_STEP
    nj = N_GROUPS // gs
    spans, src_specs = _source_specs(sources, True)
    ns = len(sources)

    def body(*refs):
        src = refs[:ns]
        w_ref, x_ref, pw_ref, d_ref, dx_ref, dpw_ref, acc = refs[ns:]
        i, j = pl.program_id(0), pl.program_id(1)

        @pl.when((i == 0) & (j == 0))
        def _():
            dpw_ref[...] = jnp.zeros_like(dpw_ref)

        @pl.when(j == 0)
        def _():
            acc[...] = jnp.zeros_like(acc)

        for s_ref, (lo, hi) in zip(src, spans):
            @pl.when((j >= lo) & (j < hi))
            def _(s_ref=s_ref):
                four = jnp.concatenate([s_ref[k] for k in range(gs)], axis=-1)
                acc[...] += lax.dot_general(four, w_ref[...], (((1,), (1,)), ((), ())), preferred_element_type=F32)

        @pl.when(j == nj - 1)
        def _():
            dx, dw_rows = _rmsnorm_bwd(x_ref[...], pw_ref[...], acc[...])
            dx_ref[...] = d_ref[...] + dx
            dpw_ref[...] += jnp.sum(dw_rows, 0, keepdims=True)

    rows = pl.BlockSpec((tm, D_MODEL), lambda i, j: (i, 0))
    vec = pl.BlockSpec((1, D_MODEL), lambda i, j: (0, 0))
    return _pcall(
        body, name=name, grid=(n // tm, nj),
        in_specs=src_specs + [pl.BlockSpec((None, D_MODEL, gs * DH), lambda i, j: (j, 0, 0)), rows, vec, rows],
        out_specs=[rows, vec],
        out_shape=[jax.ShapeDtypeStruct((n, D_MODEL), F32), jax.ShapeDtypeStruct((1, D_MODEL), F32)],
        scratch_shapes=[pltpu.VMEM((tm, D_MODEL), F32)],
        compiler_params=_cparams(("arbitrary", "arbitrary")),
    )(*sources, w_g, x, pre_w, dxn)


def _dwin(hb, sources, name):
    n = hb.shape[0]
    tm = _row_tile(n)
    gs = GROUPS_PER_STEP
    spans, src_specs = _source_specs(sources, False)
    ns = len(sources)

    def body(*refs):
        h_ref = refs[0]
        src = refs[1:1 + ns]
        dw_ref = refs[1 + ns]
        j = pl.program_id(0)

        @pl.when(pl.program_id(1) == 0)
        def _():
            dw_ref[...] = jnp.zeros_like(dw_ref)

        h = h_ref[...]
        for s_ref, (lo, hi) in zip(src, spans):
            @pl.when((j >= lo) & (j < hi))
            def _(s_ref=s_ref):
                four = jnp.concatenate([s_ref[k] for k in range(gs)], axis=-1)
                dw_ref[...] += lax.dot_general(h, four, (((0,), (0,)), ((), ())), preferred_element_type=F32)

    return _pcall(
        body, name=name, grid=(N_GROUPS // gs, n // tm),
        in_specs=[pl.BlockSpec((tm, D_MODEL), lambda j, i: (i, 0))] + src_specs,
        out_specs=pl.BlockSpec((None, D_MODEL, gs * DH), lambda j, i: (j, 0, 0)),
        out_shape=jax.ShapeDtypeStruct((N_GROUPS // gs, D_MODEL, gs * DH), F32),
        compiler_params=_cparams(("parallel", "arbitrary")),
    )(hb, *sources)


def _adamw_math(w, g, m, v):
    c1 = 1.0 - ADAM_B1 ** ADAM_STEP
    c2 = 1.0 - ADAM_B2 ** ADAM_STEP
    nm = ADAM_B1 * m + (1.0 - ADAM_B1) * g
    nv = ADAM_B2 * v + (1.0 - ADAM_B2) * (g * g)
    return -ADAM_LR * ((nm / c1) / (jnp.sqrt(nv / c2) + ADAM_EPS) + ADAM_WD * w), nm, nv


def _adamw(w, g, m, v, name):
    r, c = w.shape
    tr = 256 if r % 256 == 0 else r

    def body(w_ref, g_ref, m_ref, v_ref, d_ref, nm_ref, nv_ref):
        d_ref[...], nm_ref[...], nv_ref[...] = _adamw_math(w_ref[...], g_ref[...], m_ref[...], v_ref[...])

    spec = pl.BlockSpec((tr, c), lambda i: (i, 0))
    return _pcall(
        body, name=name, grid=(r // tr,),
        in_specs=[spec] * 4, out_specs=[spec] * 3,
        out_shape=[jax.ShapeDtypeStruct((r, c), F32)] * 3,
        compiler_params=_cparams(("parallel",)),
    )(w, g, m, v)


def _sum_adamw(parts, w, m, v, name):
    r, c = w.shape
    tr = 128 if r % 128 == 0 else r

    def body(p_ref, w_ref, m_ref, v_ref, g_ref, d_ref, nm_ref, nv_ref):
        g = p_ref[0]
        for k in range(1, N_DEV):
            g = g + p_ref[k]
        g_ref[...] = g
        d_ref[...], nm_ref[...], nv_ref[...] = _adamw_math(w_ref[...], g, m_ref[...], v_ref[...])

    spec = pl.BlockSpec((tr, c), lambda i: (i, 0))
    return _pcall(
        body, name=name, grid=(r // tr,),
        in_specs=[pl.BlockSpec((N_DEV, tr, c), lambda i: (0, i, 0))] + [spec] * 3, out_specs=[spec] * 4,
        out_shape=[jax.ShapeDtypeStruct((r, c), F32)] * 4,
        compiler_params=_cparams(("parallel",)),
    )(parts, w, m, v)


def _me():
    return lax.axis_index("x"), lax.axis_index("y"), lax.axis_index("c")


def _flat(x, y, c):
    return 4 * x + 2 * y + c


def _peer(k):
    x, y, c = _me()
    return (x ^ ((k >> 2) & 1), y ^ ((k >> 1) & 1), c ^ (k & 1))


def _all_gather(blocks, name):
    na = len(blocks)

    def body(*refs):
        x_refs, out_refs = refs[:na], refs[na:2 * na]
        send_sems, recv_sems, local_sems = refs[2 * na:]
        me = _flat(*_me())
        local = []
        for a in range(na):
            cp = pltpu.make_async_copy(x_refs[a], out_refs[a].at[me], local_sems.at[a])
            cp.start()
            local.append(cp)
        copies = []
        for k in range(1, N_DEV):
            for a in range(na):
                cp = pltpu.make_async_remote_copy(
                    src_ref=x_refs[a], dst_ref=out_refs[a].at[me],
                    send_sem=send_sems.at[a, k - 1], recv_sem=recv_sems.at[a, k - 1],
                    device_id=_peer(k), device_id_type=MESH)
                cp.start()
                copies.append(cp)
        for k in range(1, N_DEV):
            src = _flat(*_peer(k))
            for a in range(na):
                pltpu.make_async_remote_copy(
                    src_ref=x_refs[a], dst_ref=out_refs[a].at[src],
                    send_sem=send_sems.at[a, k - 1], recv_sem=recv_sems.at[a, k - 1],
                    device_id=_peer(k), device_id_type=MESH).wait_recv()
        for cp in copies:
            cp.wait_send()
        for cp in local:
            cp.wait()

    return _pcall(
        body, name=name,
        in_specs=[pl.BlockSpec(memory_space=pl.ANY)] * na,
        out_specs=[pl.BlockSpec(memory_space=pl.ANY)] * na,
        out_shape=[jax.ShapeDtypeStruct((N_DEV,) + b.shape, b.dtype) for b in blocks],
        scratch_shapes=[pltpu.SemaphoreType.DMA((na, N_DEV - 1)), pltpu.SemaphoreType.DMA((na, N_DEV - 1)),
                        pltpu.SemaphoreType.DMA((na,))],
    )(*blocks)


def _exchange(sends, layouts, out_shapes, name):
    ns, no = len(sends), len(out_shapes)

    def body(*refs):
        s_refs, out_refs = refs[:ns], refs[ns:ns + no]
        send_sems, recv_sems, local_sems = refs[ns + no:]
        me = _flat(*_me())

        def dst(i, k):
            o, pos = layouts[i]
            return out_refs[o].at[k] if pos is None else out_refs[o].at[k, pos]

        local = []
        for i in range(ns):
            cp = pltpu.make_async_copy(s_refs[i].at[me], dst(i, 0), local_sems.at[i])
            cp.start()
            local.append(cp)
        copies = []
        for k in range(1, N_DEV):
            to = _flat(*_peer(k))
            for i in range(ns):
                cp = pltpu.make_async_remote_copy(
                    src_ref=s_refs[i].at[to], dst_ref=dst(i, k),
                    send_sem=send_sems.at[i, k - 1], recv_sem=recv_sems.at[i, k - 1],
                    device_id=_peer(k), device_id_type=MESH)
                cp.start()
                copies.append(cp)
        for cp in copies:
            cp.wait_recv()
        for cp in copies:
            cp.wait_send()
        for cp in local:
            cp.wait()

    return _pcall(
        body, name=name,
        in_specs=[pl.BlockSpec(memory_space=pl.ANY)] * ns,
        out_specs=[pl.BlockSpec(memory_space=pl.ANY)] * no,
        out_shape=[jax.ShapeDtypeStruct(s, F32) for s in out_shapes],
        scratch_shapes=[pltpu.SemaphoreType.DMA((ns, N_DEV - 1)), pltpu.SemaphoreType.DMA((ns, N_DEV - 1)),
                        pltpu.SemaphoreType.DMA((ns,))],
    )(*sends)


def _sum_slots(a, name):
    r = a.shape[1]

    def body(a_ref, o_ref):
        acc = a_ref[0]
        for d in range(1, N_DEV):
            acc = acc + a_ref[d]
        o_ref[...] = acc

    return _pcall(body, name=name, out_shape=jax.ShapeDtypeStruct((r, 128), F32), compiler_params=_cparams())(a)


def _all_reduce_small(blk, name):
    r = blk.shape[0]

    def body(x_ref, out_ref, gath, send_sems, recv_sems):
        me = _flat(*_me())
        gath[me] = x_ref[...]
        copies = []
        for k in range(1, N_DEV):
            cp = pltpu.make_async_remote_copy(
                src_ref=x_ref, dst_ref=gath.at[me],
                send_sem=send_sems.at[k - 1], recv_sem=recv_sems.at[k - 1],
                device_id=_peer(k), device_id_type=MESH)
            cp.start()
            copies.append(cp)
        for k in range(1, N_DEV):
            src = _flat(*_peer(k))
            pltpu.make_async_remote_copy(
                src_ref=x_ref, dst_ref=gath.at[src],
                send_sem=send_sems.at[k - 1], recv_sem=recv_sems.at[k - 1],
                device_id=_peer(k), device_id_type=MESH).wait_recv()
        for cp in copies:
            cp.wait_send()
        acc = gath[0]
        for d in range(1, N_DEV):
            acc = acc + gath[d]
        out_ref[...] = acc

    return _pcall(
        body, name=name,
        in_specs=[pl.BlockSpec(memory_space=pltpu.VMEM)],
        out_specs=pl.BlockSpec(memory_space=pltpu.VMEM),
        out_shape=jax.ShapeDtypeStruct((r, 128), F32),
        scratch_shapes=[pltpu.VMEM((N_DEV, r, 128), F32),
                        pltpu.SemaphoreType.DMA((N_DEV - 1,)), pltpu.SemaphoreType.DMA((N_DEV - 1,))],
    )(blk)


def _heads(vec):
    return vec.reshape(NH, 1, DH)


def _rep(vec4):
    return jnp.broadcast_to(vec4.reshape(NH, 1, 1), (NH, 1, DH))


def _onehot_lane(offset):
    m = np.zeros((NH, 1, DH), np.float32)
    for h in range(NH):
        m[h, 0, offset + h] = 1.0
    return jnp.asarray(m)


_TINY = (("gdn_conv_w", (DEPTH, 4, 96)), ("rwkv_w_up", (DEPTH, 64, 32)), ("rwkv_a_up", (DEPTH, 64, 32)),
         ("sc_conv_w", (DEPTH, 3, 32)))
_TINY_ROWS = -(-sum(int(np.prod(s)) for _, s in _TINY) // 1024) * 8


def _pack_rows(arrays, rows, fill=0.0):
    flat = jnp.concatenate([a.reshape(-1) for a in arrays])
    return jnp.pad(flat, (0, rows * 128 - flat.shape[0]), constant_values=fill).reshape(rows, 128)


def _unpack_rows(p, named_shapes):
    lead = p.shape[:-2]
    flat = p.reshape(lead + (-1,))
    out, o = {}, 0
    for n, s in named_shapes:
        size = int(np.prod(s))
        out[n] = flat[..., o:o + size].reshape(lead + tuple(s))
        o += size
    return out


def _gather_last(a):
    return jnp.transpose(a, (1, 0, 2)).reshape(a.shape[1], -1)


def _split_last(a):
    r, c8 = a.shape
    return jnp.transpose(a.reshape(r, N_DEV, c8 // N_DEV), (1, 0, 2))


_SMALL = (("pre_norm_w", (DEPTH, 1024)), ("gdn_a_log", (DEPTH, 4)), ("gdn_dt_bias", (DEPTH, 4)),
          ("gdn_norm_w", (DEPTH, 64)), ("rwkv_mu", (DEPTH, 1152)), ("rwkv_w0", (DEPTH, 256)),
          ("rwkv_a0", (DEPTH, 256)), ("rwkv_k_k", (DEPTH, 256)), ("rwkv_k_a", (DEPTH, 256)),
          ("rwkv_r_k", (DEPTH, 256)), ("rwkv_ln_w", (DEPTH, 256)), ("rwkv_ln_b", (DEPTH, 256)),
          ("gla_a_up", (DEPTH, 16, 128)), ("gla_a_bias", (DEPTH, 128)), ("gla_norm_w", (DEPTH, 64)),
          ("post_norm_w", (DEPTH, 1024)), ("loss", ()))
_SMALL_ROWS = -(-sum(int(np.prod(s)) for _, s in _SMALL) // 1024) * 8


def _layer_params(wts, tiny, w_in_all, w_out_all, l):
    conv = _gather_last(tiny["gdn_conv_w"][:, l])
    q = {}
    q["gdn_conv"] = jnp.transpose(conv.reshape(GDN_TAPS, 12, DH), (1, 0, 2))
    q["gdn_prm"] = [_rep(wts["gdn_a_log"][l]), _rep(wts["gdn_dt_bias"][l]),
                    jnp.broadcast_to(wts["gdn_norm_w"][l].reshape(1, 1, DH), (NH, 1, DH))]
    q["gdn_cst"] = [_onehot_lane(0), _onehot_lane(NH)]
    q["rwkv_mu"] = wts["rwkv_mu"][l].reshape(18, 1, DH)
    w_up = jnp.transpose(_gather_last(tiny["rwkv_w_up"][:, l]).reshape(64, NH, DH), (1, 0, 2))
    a_up = jnp.transpose(_gather_last(tiny["rwkv_a_up"][:, l]).reshape(64, NH, DH), (1, 0, 2))
    q["rwkv_prm"] = [_heads(wts["rwkv_w0"][l]), w_up, _heads(wts["rwkv_a0"][l]), a_up,
                     _heads(wts["rwkv_k_k"][l]), _heads(wts["rwkv_k_a"][l]), _heads(wts["rwkv_r_k"][l]),
                     _heads(wts["rwkv_ln_w"][l]), _heads(wts["rwkv_ln_b"][l])]
    sc = _gather_last(tiny["sc_conv_w"][:, l])
    q["sc_conv"] = jnp.transpose(sc.reshape(SC_TAPS, NH, DH), (1, 0, 2))
    gla_up = jnp.transpose(wts["gla_a_up"][l].reshape(16, NH, GLA_HEAD_K), (1, 0, 2))
    gla_up = jnp.pad(gla_up, ((0, 0), (0, DH - 16), (0, DH - GLA_HEAD_K)))
    gla_b = jnp.pad(wts["gla_a_bias"][l].reshape(NH, 1, GLA_HEAD_K), ((0, 0), (0, 0), (0, DH - GLA_HEAD_K)))
    q["gla_prm"] = [gla_up, gla_b, jnp.broadcast_to(wts["gla_norm_w"][l].reshape(1, 1, DH), (NH, 1, DH))]
    q["w_g"] = _regroup_in(w_in_all, l, f"regroup_in{l}")
    q["wout_g"] = w_out_all[:, l].reshape(4 * NH, DH, D_MODEL).astype(BF16)
    q["pre_w"] = wts["pre_norm_w"][l].reshape(1, D_MODEL)
    q["post_w"] = wts["post_norm_w"][l].reshape(1, D_MODEL)
    return q


def _mixer_inputs(p, cq, pm):
    gdn = [(cq, 4, 0), (cq, 4, 1), (cq, 4, 2), (p, 4, G_GDN // 4 + 3), (p, 1, G_GDN_AB)]
    rwkv = [(pm, 4, 0), (pm, 4, 1), (pm, 4, 2), (pm, 4, 3), (pm, 1, 16), (pm, 1, 17)]
    gla = [(p, 4, G_GLA // 4 + k) for k in range(4)] + [(p, 1, G_GLA_AD)]
    return gdn, rwkv, gla


def _layer_fwd(x, q, nb, t, l):
    hb, p = _norm_proj(x, q["pre_w"], q["w_g"], f"norm_proj{l}")
    cq = _conv_fwd(p, G_GDN, 12, q["gdn_conv"], nb, t, f"gdn_conv{l}")
    pm = _mix_fwd(p, q["rwkv_mu"], nb, t, f"rwkv_mix{l}")
    gdn_in, rwkv_in, gla_in = _mixer_inputs(p, cq, pm)
    y_gdn, ck_gdn = _mixer_fwd(_gdn_chunk, f"gdn_fwd{l}", gdn_in, q["gdn_prm"], q["gdn_cst"], nb, t)
    y_rwkv, ck_rwkv = _mixer_fwd(_rwkv_chunk, f"rwkv_fwd{l}", rwkv_in, q["rwkv_prm"], [], nb, t,
                                 first_fn=_rwkv_chunk_steps)
    y_sc = _sc_fwd(p, q["sc_conv"], nb, t, f"sc_fwd{l}")
    y_gla, ck_gla = _mixer_fwd(_gla_chunk, f"gla_fwd{l}", gla_in, q["gla_prm"], [], nb, t)
    ys = (y_gdn, y_rwkv, y_sc, y_gla)
    out, xn = _out_proj_norm(ys, q["wout_g"], x, q["post_w"], f"out_proj{l}")
    saved = dict(x=x, hb=hb, p=p, cq=cq, pm=pm, ys=ys, out=out, ck=(ck_gdn, ck_rwkv, ck_gla))
    return xn, saved


def _layer_bwd(dxn, q, sv, nb, t, l):
    p, cq, pm, ys = sv["p"], sv["cq"], sv["pm"], sv["ys"]
    dout, dy, d_post = _post_bwd(dxn, sv["out"], q["post_w"], q["wout_g"], f"post_bwd{l}")
    d_wout = _dwout(ys, dout, f"dwout{l}").reshape(N_DEV, 128, D_MODEL)
    gdn_in, rwkv_in, gla_in = _mixer_inputs(p, cq, pm)
    ck_gdn, ck_rwkv, ck_gla = sv["ck"]
    g = {}

    (d_conv, dz, dab), (da_log, ddt, dnw) = _mixer_bwd(
        _gdn_chunk, f"gdn_bwd{l}", gdn_in, q["gdn_prm"], q["gdn_cst"], ck_gdn, dy, 0,
        [(12, F32), (4, BF16), (1, BF16)], [(0, 0), (0, 4), (0, 8), (1, 0), (2, 0)], nb, t)
    dconv_in, d_gconv = _conv_bwd(p, G_GDN, 12, q["gdn_conv"], d_conv, nb, t, f"gdn_conv_bwd{l}")
    g["gdn_conv_w"] = jnp.transpose(d_gconv.sum(1), (1, 0, 2)).reshape(GDN_TAPS, 768)
    g["gdn_a_log"] = da_log.sum((0, 2, 3))
    g["gdn_dt_bias"] = ddt.sum((0, 2, 3))
    g["gdn_norm_w"] = dnw.sum((0, 1, 2))

    (d_pm,), d_rprm = _mixer_bwd(
        _rwkv_chunk, f"rwkv_bwd{l}", rwkv_in, q["rwkv_prm"], [], ck_rwkv, dy, 1,
        [(18, F32)], [(0, 0), (0, 4), (0, 8), (0, 12), (0, 16), (0, 17)], nb, t, first_fn=_rwkv_chunk_steps)
    dp_rwkv, d_mu = _mix_bwd(p, q["rwkv_mu"], d_pm, nb, t, f"rwkv_mix_bwd{l}")
    g["rwkv_mu"] = d_mu.sum(1).reshape(1152)
    rp = [a.sum(0) for a in d_rprm]
    g["rwkv_w0"] = rp[0].reshape(256)
    g["rwkv_w_up"] = jnp.transpose(rp[1], (1, 0, 2)).reshape(64, 256)
    g["rwkv_a0"] = rp[2].reshape(256)
    g["rwkv_a_up"] = jnp.transpose(rp[3], (1, 0, 2)).reshape(64, 256)
    for i, nme in enumerate(("rwkv_k_k", "rwkv_k_a", "rwkv_r_k", "rwkv_ln_w", "rwkv_ln_b")):
        g[nme] = rp[4 + i].reshape(256)

    dp_sc, d_scw = _sc_bwd(p, q["sc_conv"], dy, nb, t, f"sc_bwd{l}")
    g["sc_conv_w"] = jnp.transpose(d_scw.sum(1), (1, 0, 2)).reshape(SC_TAPS, 256)

    (dp_gla, dad), (d_aup, d_ab, d_gnw) = _mixer_bwd(
        _gla_chunk, f"gla_bwd{l}", gla_in, q["gla_prm"], [], ck_gla, dy, 3,
        [(16, BF16), (1, BF16)], [(0, 0), (0, 4), (0, 8), (0, 12), (1, 0)], nb, t)
    g["gla_a_up"] = jnp.transpose(d_aup.sum(0)[:, :16, :GLA_HEAD_K], (1, 0, 2)).reshape(16, 128)
    g["gla_a_bias"] = d_ab.sum(0)[:, 0, :GLA_HEAD_K].reshape(128)
    g["gla_norm_w"] = d_gnw.sum((0, 1, 2))

    singles = jnp.concatenate([dab, dp_rwkv[16:18], dad], axis=0)
    sources = [dconv_in, dz, dp_rwkv, dp_sc, dp_gla, singles]
    dx, d_pre = _dh_prenorm_bwd(sources, q["w_g"], sv["x"], q["pre_w"], dxn, f"dh_bwd{l}")
    d_win = _regroup_out(_dwin(sv["hb"], sources, f"dwin{l}"), f"regroup_out{l}")
    g["pre_norm_w"] = d_pre.reshape(D_MODEL)
    g["post_norm_w"] = d_post.reshape(D_MODEL)
    return dx, g, d_win, d_wout


def _local_step(x, tgt, wts, tiny, w_in_all, w_out_all):
    nb, t, d = x.shape
    xf = x.reshape(nb * t, d)
    qs, saved = [], []
    for l in range(DEPTH):
        q = _layer_params(wts, tiny, w_in_all, w_out_all, l)
        xf, sv = _layer_fwd(xf, q, nb, t, l)
        qs.append(q)
        saved.append(sv)
    dxf, lpart = _loss_grad(xf, tgt.reshape(nb * t, d), "loss")
    grads, d_win, d_wout = [None] * DEPTH, [None] * DEPTH, [None] * DEPTH
    for l in reversed(range(DEPTH)):
        dxf, grads[l], d_win[l], d_wout[l] = _layer_bwd(dxf, qs[l], saved[l], nb, t, l)
    small = {k: jnp.stack([grads[l][k] for l in range(DEPTH)]) for k in grads[0]}
    return lpart[0, 0], dxf.reshape(nb, t, d), small, d_win, d_wout


_WEIGHTS = ("pre_norm_w", "w_in", "gdn_conv_w", "gdn_a_log", "gdn_dt_bias", "gdn_norm_w", "rwkv_mu", "rwkv_w0",
            "rwkv_w_up", "rwkv_a0", "rwkv_a_up", "rwkv_k_k", "rwkv_k_a", "rwkv_r_k", "rwkv_ln_w", "rwkv_ln_b",
            "sc_conv_w", "gla_a_up", "gla_a_bias", "gla_norm_w", "w_out", "post_norm_w")


def kernel(x, pre_norm_w, w_in, gdn_conv_w, gdn_a_log, gdn_dt_bias, gdn_norm_w, rwkv_mu, rwkv_w0, rwkv_w_up, rwkv_a0, rwkv_a_up, rwkv_k_k, rwkv_k_a, rwkv_r_k, rwkv_ln_w, rwkv_ln_b, sc_conv_w, gla_a_up, gla_a_bias, gla_norm_w, w_out, post_norm_w, loss_target, m_pre_norm_w, m_w_in, m_gdn_conv_w, m_gdn_a_log, m_gdn_dt_bias, m_gdn_norm_w, m_rwkv_mu, m_rwkv_w0, m_rwkv_w_up, m_rwkv_a0, m_rwkv_a_up, m_rwkv_k_k, m_rwkv_k_a, m_rwkv_r_k, m_rwkv_ln_w, m_rwkv_ln_b, m_sc_conv_w, m_gla_a_up, m_gla_a_bias, m_gla_norm_w, m_w_out, m_post_norm_w, v_pre_norm_w, v_w_in, v_gdn_conv_w, v_gdn_a_log, v_gdn_dt_bias, v_gdn_norm_w, v_rwkv_mu, v_rwkv_w0, v_rwkv_w_up, v_rwkv_a0, v_rwkv_a_up, v_rwkv_k_k, v_rwkv_k_a, v_rwkv_r_k, v_rwkv_ln_w, v_rwkv_ln_b, v_sc_conv_w, v_gla_a_up, v_gla_a_bias, v_gla_norm_w, v_w_out, v_post_norm_w):
    env = dict(locals())
    w = {n: env[n] for n in _WEIGHTS}
    m = {n: env["m_" + n] for n in _WEIGHTS}
    v = {n: env["v_" + n] for n in _WEIGHTS}
    tiny_names = [n for n, _ in _TINY]

    w_in_all, w_out_all, tiny_all = _all_gather(
        [w_in.astype(BF16), w_out.astype(BF16), _pack_rows([w[n] for n in tiny_names], _TINY_ROWS)],
        "gather_weights")
    tiny = _unpack_rows(tiny_all, _TINY)

    lpart, grad_x, small, d_win, d_wout = _local_step(x, loss_target, w, tiny, w_in_all, w_out_all)

    tiny_send = jnp.stack([_pack_rows([_split_last(small[n][l])[d] for n in tiny_names for l in range(DEPTH)],
                                      _TINY_ROWS) for d in range(N_DEV)])
    r_win, r_wout, r_tiny = _exchange(
        [d_win[0], d_win[1], d_wout[0], d_wout[1], tiny_send],
        [(0, 0), (0, 1), (1, 0), (1, 1), (2, None)],
        [(N_DEV, DEPTH, D_MODEL, SHARD_COLS), (N_DEV, DEPTH, 128, D_MODEL), (N_DEV, _TINY_ROWS, 128)],
        "scatter_grads")
    grads, delta, new_m, new_v = {}, {}, {}, {}
    for n, parts in (("w_in", r_win), ("w_out", r_wout)):
        shp = w[n].shape
        two = lambda a: a.reshape(-1, shp[-1])
        res = _sum_adamw(parts.reshape(N_DEV, -1, shp[-1]), two(w[n]), two(m[n]), two(v[n]), "adamw_" + n)
        grads[n], delta[n], new_m[n], new_v[n] = [o.reshape(shp) for o in res]
    tiny_sum = _sum_slots(r_tiny, "sum_tiny").reshape(-1)
    o = 0
    for n, s in _TINY:
        size = int(np.prod(s))
        grads[n] = tiny_sum[o:o + size].reshape(s)
        o += size

    small = dict(small)
    small["loss"] = lpart
    red = _unpack_rows(_all_reduce_small(_pack_rows([small[n] for n, _ in _SMALL], _SMALL_ROWS), "reduce_small"),
                       _SMALL)
    loss = red.pop("loss")
    grads.update(red)

    rest = [n for n in _WEIGHTS if n not in ("w_in", "w_out")]
    rest_shapes = [(n, w[n].shape) for n in rest]
    rows = -(-sum(int(np.prod(s)) for _, s in rest_shapes) // 1024) * 8
    outs = _adamw(_pack_rows([w[n] for n in rest], rows), _pack_rows([grads[n] for n in rest], rows),
                  _pack_rows([m[n] for n in rest], rows), _pack_rows([v[n] for n in rest], rows, 1.0), "adamw_rest")
    for dst, packed in zip((delta, new_m, new_v), outs):
        dst.update(_unpack_rows(packed, rest_shapes))

    return (loss, grad_x, *[grads[n] for n in _WEIGHTS], *[delta[n] for n in _WEIGHTS],
            *[new_m[n] for n in _WEIGHTS], *[new_v[n] for n in _WEIGHTS])
```

```python
import functools
import math

import numpy as np
import jax
import jax.numpy as jnp
from jax import lax
from jax.experimental import pallas as pl
from jax.experimental.pallas import tpu as pltpu

F32 = jnp.float32
BF16 = jnp.bfloat16

D_MODEL = 1024
DEPTH = 2
NH = 4
DH = 64
CH = 64
EPS = 1e-6
RWKV_GN_EPS = 64e-5
GLA_HEAD_K = 32
GLA_TAU = 16.0
GDN_TAPS = 4
SC_TAPS = 3
D_IN = 3992
N_DEV = 8
SHARD_COLS = D_IN // N_DEV

G_GDN = 0
G_RWKV = 16
G_SC = 32
G_GLA = 48
G_GDN_AB, G_RWKV_WD, G_RWKV_AD, G_GLA_AD = 64, 65, 66, 67
N_GROUPS = 68
GROUPS_PER_STEP = 4
TIME_BLOCK = 256

C_GDN, C_RWKV, C_SC, C_GLA = 0, 1032, 2184, 3208

ADAM_LR, ADAM_B1, ADAM_B2, ADAM_EPS, ADAM_WD, ADAM_STEP = 0.001, 0.9, 0.999, 1e-08, 0.01, 10

VMEM_LIMIT = 56 * 1024 * 1024
MESH = pl.DeviceIdType.MESH

_pcall = pl.pallas_call


def _cparams(sem=None):
    if sem is None:
        return pltpu.CompilerParams(vmem_limit_bytes=VMEM_LIMIT)
    return pltpu.CompilerParams(dimension_semantics=sem, vmem_limit_bytes=VMEM_LIMIT)


def _group_segments():
    table = [(G_GDN + i, C_GDN + DH * i, DH) for i in range(16)]
    table.append((G_GDN_AB, C_GDN + 1024, 8))
    table += [(G_RWKV + i, C_RWKV + DH * i, DH) for i in range(16)]
    table += [(G_RWKV_WD, C_RWKV + 1024, DH), (G_RWKV_AD, C_RWKV + 1088, DH)]
    table += [(G_SC + 4 * j + k, C_SC + 256 * k + DH * j, DH) for j in range(NH) for k in range(4)]
    for h in range(NH):
        table += [(G_GLA + h, C_GLA + GLA_HEAD_K * h, GLA_HEAD_K),
                  (G_GLA + 4 + h, C_GLA + 128 + GLA_HEAD_K * h, GLA_HEAD_K),
                  (G_GLA + 8 + h, C_GLA + 256 + DH * h, DH),
                  (G_GLA + 12 + h, C_GLA + 512 + DH * h, DH)]
    table.append((G_GLA_AD, C_GLA + 768, 16))
    segs, padded = [], []
    for g, c, n in table:
        if n < DH:
            padded.append(g)
        a = 0
        while n > 0:
            d, off = divmod(c, SHARD_COLS)
            ln = min(n, SHARD_COLS - off)
            segs.append((g, a, d, off, ln))
            c, a, n = c + ln, a + ln, n - ln
    return segs, padded


_SEGMENTS, _PADDED_GROUPS = _group_segments()


def _dn(ta, tb):
    return (((1 if ta else 2,), (2 if tb else 1,)), ((0,), (0,)))


def _hdot(a, b, ta=False, tb=False):
    return lax.dot_general(a, b, _dn(ta, tb), precision=lax.Precision.HIGH, preferred_element_type=F32)


def _r(x):
    return x.astype(BF16)


def _rdot(a, b, ta=False, tb=False):
    return lax.dot_general(_r(a), _r(b), _dn(ta, tb), preferred_element_type=F32)


@jax.custom_vjp
def _bmm(a, b):
    return _rdot(a, b)


def _bmm_fwd(a, b):
    return _rdot(a, b), (a, b)


def _bmm_bwd(res, g):
    a, b = res
    return _rdot(g, b, tb=True), _rdot(a, g, ta=True)


_bmm.defvjp(_bmm_fwd, _bmm_bwd)


@jax.custom_vjp
def _bmm_nt(a, b):
    return _rdot(a, b, tb=True)


def _bmm_nt_fwd(a, b):
    return _rdot(a, b, tb=True), (a, b)


def _bmm_nt_bwd(res, g):
    a, b = res
    return _rdot(g, b), _rdot(g, a, ta=True)


_bmm_nt.defvjp(_bmm_nt_fwd, _bmm_nt_bwd)


@jax.custom_vjp
def _bmm_tn(a, b):
    return _rdot(a, b, ta=True)


def _bmm_tn_fwd(a, b):
    return _rdot(a, b, ta=True), (a, b)


def _bmm_tn_bwd(res, g):
    a, b = res
    return _rdot(b, g, tb=True), _rdot(a, g)


_bmm_tn.defvjp(_bmm_tn_fwd, _bmm_tn_bwd)


def _tri(n):
    i = lax.broadcasted_iota(jnp.int32, (n, n), 0)
    j = lax.broadcasted_iota(jnp.int32, (n, n), 1)
    return i >= j, i > j, i == j


def _heads_of(x, like):
    n = like.shape[0]
    if x.ndim == 2:
        return jnp.broadcast_to(x[None], (n,) + x.shape)
    seqs = x.shape[0]
    return jnp.broadcast_to(x[:, None], (seqs, n // seqs) + x.shape[1:]).reshape((n,) + x.shape[1:])


def _cumsum_rows(x):
    incl, _, _ = _tri(CH)
    return _hdot(_heads_of(incl.astype(F32), x), x)


def _inv_unit_lower(a):
    n = a.shape[-1]
    _, _, eye = _tri(n)
    pw = -a
    inv = eye.astype(F32) + pw
    for _ in range(int(math.log2(n)) - 1):
        pw = _hdot(pw, pw)
        inv = inv + _hdot(inv, pw)
    return inv


def _silu(x):
    return x * jax.nn.sigmoid(x)


def _t(x):
    return jnp.swapaxes(x, -1, -2)


def _gdn_chunk(prm, cst, ins, s):
    a_log, dt_b, nw = prm
    m_a, m_b = cst
    cq, ck, cv, z, ab = ins
    ab = _heads_of(ab, m_a)
    incl, strict, _ = _tri(CH)
    q = _silu(cq)
    k = _silu(ck)
    v = _silu(cv)
    q = q * lax.rsqrt(jnp.sum(q * q, -1, keepdims=True) + EPS) * (DH ** -0.5)
    k = k * lax.rsqrt(jnp.sum(k * k, -1, keepdims=True) + EPS)
    a_raw = jnp.sum(ab * m_a, -1, keepdims=True)
    b_raw = jnp.sum(ab * m_b, -1, keepdims=True)
    gstep = -jnp.exp(a_log) * jax.nn.softplus(a_raw + dt_b)
    beta = jax.nn.sigmoid(b_raw)
    gc = _cumsum_rows(gstep)
    gl = jnp.sum(gstep, -2, keepdims=True)
    dec = jnp.where(incl, jnp.exp(jnp.where(incl, gc - _t(gc), 0.0)), 0.0)
    kb = k * beta
    a_mat = jnp.where(strict, _bmm_nt(kb, k) * dec, 0.0)
    tinv = _inv_unit_lower(a_mat)
    eg = jnp.exp(gc)
    u = _hdot(tinv, v * beta)
    w = _hdot(tinv, kb * eg)
    attn = _bmm_nt(q, k) * dec
    v_new = u - _bmm(w, s)
    o = _bmm(q * eg, s) + _bmm(attn, v_new)
    s_next = s * jnp.exp(gl) + _bmm_tn(k * jnp.exp(gl - gc), v_new)
    on = o * lax.rsqrt(jnp.mean(o * o, -1, keepdims=True) + EPS) * nw
    return on * _silu(z), s_next


def _gla_chunk(prm, cst, ins, st):
    a_up, a_bias, nw = prm
    q, k, v, z, ad = ins
    incl, _, _ = _tri(CH)
    la = jax.nn.log_sigmoid(_bmm(_heads_of(ad, a_up), a_up) + a_bias) * (1.0 / GLA_TAU)
    bc = _cumsum_rows(la)
    bl = jnp.sum(la, -2, keepdims=True)
    qe = q * (GLA_HEAD_K ** -0.5) * jnp.exp(bc)
    ke = k * jnp.exp(-bc)
    attn = jnp.where(incl, _bmm_nt(qe, ke), 0.0)
    o = _bmm_nt(qe, st) + _bmm(attn, v)
    st_next = st * jnp.exp(bl) + _bmm_tn(v, k * jnp.exp(bl - bc))
    on = o * lax.rsqrt(jnp.mean(o * o, -1, keepdims=True) + EPS) * nw
    return on * _silu(z), st_next


def _rwkv_chunk(prm, cst, ins, s):
    r, v = ins[0], ins[2]
    incl, strict, _ = _tri(CH)
    lw, kk, k2, m = _rwkv_pre(prm, ins)
    cum = _cumsum_rows(lw)
    ltot = jnp.sum(lw, -2, keepdims=True)
    n_t = -kk * jnp.exp(cum - lw)
    einv = jnp.exp(-cum)
    m_t = m * einv
    k_t = k2 * einv
    r_t = r * jnp.exp(cum)
    a_nm = jnp.where(strict, _hdot(n_t, m_t, tb=True), 0.0)
    a_nk = jnp.where(strict, _hdot(n_t, k_t, tb=True), 0.0)
    cm = _hdot(_inv_unit_lower(-a_nm), _hdot(n_t, s, tb=True) + _hdot(a_nk, v))
    y = (_hdot(r_t, s, tb=True) + _hdot(jnp.where(incl, _hdot(r_t, m_t, tb=True), 0.0), cm)
         + _hdot(jnp.where(incl, _hdot(r_t, k_t, tb=True), 0.0), v))
    eend = jnp.exp(ltot - cum)
    s_next = s * jnp.exp(ltot) + _hdot(cm, m * eend, ta=True) + _hdot(v, k2 * eend, ta=True)
    return _rwkv_post(prm, ins, y, k2), s_next


def _rwkv_pre(prm, ins):
    w0, w_up, a0, a_up, k_k, k_a = prm[:6]
    k, wd, ad = ins[1], ins[4], ins[5]
    lw = -math.exp(-0.5) * jax.nn.sigmoid(w0 + _bmm(_heads_of(jnp.tanh(wd), w_up), w_up))
    a = jax.nn.sigmoid(a0 + _bmm(_heads_of(ad, a_up), a_up))
    kk = k * k_k
    kk = kk * lax.rsqrt(jnp.sum(kk * kk, -1, keepdims=True) + EPS)
    k2 = k * (1.0 + (a - 1.0) * k_a)
    return lw, kk, k2, kk * a


def _rwkv_post(prm, ins, y, k2):
    r_k, ln_w, ln_b = prm[6:]
    r, v, z = ins[0], ins[2], ins[3]
    mean = jnp.mean(y, -1, keepdims=True)
    yc = y - mean
    var = jnp.mean(yc * yc, -1, keepdims=True)
    yn = yc * lax.rsqrt(var + RWKV_GN_EPS) * ln_w + ln_b
    bonus = jnp.sum(r * k2 * r_k, -1, keepdims=True) * v
    return (yn + bonus) * _silu(z)


@jax.custom_vjp
def _bmv(s, x):
    return jnp.sum(_r(s).astype(F32) * _r(x).astype(F32), -1, keepdims=True)


def _bmv_fwd(s, x):
    return _bmv(s, x), (s, x)


def _bmv_bwd(res, g):
    s, x = res
    return g * x, jnp.sum(_r(s).astype(F32) * _r(g).astype(F32), -2, keepdims=True)


_bmv.defvjp(_bmv_fwd, _bmv_bwd)


def _rwkv_chunk_steps(prm, cst, ins, s):
    r, v = ins[0], ins[2]
    lw, kk, k2, m = _rwkv_pre(prm, ins)
    w = jnp.exp(lw)
    v_t = _t(v)
    lane = lax.broadcasted_iota(jnp.int32, (1, 1, CH), 2)
    y_t = jnp.zeros((s.shape[0], DH, CH), F32)
    for t in range(CH):
        e_t = (lane == t).astype(F32)
        row = (slice(None), slice(t, t + 1))
        sa = _bmv(s, -kk[row])
        s = s * w[row] + sa * m[row] + jnp.sum(v_t * e_t, -1, keepdims=True) * k2[row]
        y_t = y_t + _bmv(s, r[row]) * e_t
    return _rwkv_post(prm, ins, _t(y_t), k2), s


def _time_block(t):
    return TIME_BLOCK if t % TIME_BLOCK == 0 else t


def _load_chunk(ref, i):
    nb = ref.shape[1]
    if ref.shape[0] == NH:
        return jnp.concatenate([ref[:, b, pl.ds(i, CH), :] for b in range(nb)], axis=0)
    return ref[0, :, pl.ds(i, CH), :]


def _load_chunk_of(ref, c, i):
    if ref.shape[0] == NH:
        return ref[pl.ds(c % NH, 1), c // NH, pl.ds(i, CH), :]
    return ref[0, pl.ds(c // NH, 1), pl.ds(i, CH), :]


def _each_chain(n, fn):
    def step(c, carry):
        fn(c)
        return carry

    lax.fori_loop(0, n, step, 0)


def _mixer_fwd(chunk_fn, name, ins, prm, cst, nb, t, first_fn=None):
    tb = _time_block(t)
    nt, ncb, nch = t // tb, tb // CH, nb * NH
    n_in, n_prm, n_cst = len(ins), len(prm), len(cst)

    def body(*refs):
        in_refs = refs[:n_in]
        prm_refs = refs[n_in:n_in + n_prm]
        cst_refs = refs[n_in + n_prm:n_in + n_prm + n_cst]
        y_ref, ck_ref, s_scr = refs[n_in + n_prm + n_cst:]
        step_t = pl.program_id(0)

        @pl.when(step_t == 0)
        def _():
            s_scr[...] = jnp.zeros_like(s_scr)

        def chunk(c, i):
            s = s_scr[...]
            ck_ref[c] = s
            y, s_next = chunk_fn([jnp.tile(r[...], (nb, 1, 1)) for r in prm_refs],
                                 [jnp.tile(r[...], (nb, 1, 1)) for r in cst_refs],
                                 [_load_chunk(r, i) for r in in_refs], s)
            for b in range(nb):
                y_ref[:, b, pl.ds(i, CH), :] = y[b * NH:(b + 1) * NH].astype(BF16)
            s_scr[...] = s_next

        def first_chunk_of(c):
            one, h = pl.ds(c, 1), pl.ds(c % NH, 1)
            s = s_scr[one]
            ck_ref[0, one] = s
            y, s_next = first_fn([r[h] for r in prm_refs], [r[h] for r in cst_refs],
                                 [_load_chunk_of(r, c, 0) for r in in_refs], s)
            y_ref[h, c // NH, pl.ds(0, CH), :] = y.astype(BF16)
            s_scr[one] = s_next

        def step(c, carry):
            chunk(c, pl.multiple_of(c * CH, CH))
            return carry

        if first_fn is None:
            lax.fori_loop(0, ncb, step, 0)
        else:
            @pl.when(step_t == 0)
            def _():
                _each_chain(nch, first_chunk_of)

            @pl.when(step_t != 0)
            def _():
                chunk(0, 0)

            lax.fori_loop(1, ncb, step, 0)

    in_specs = [pl.BlockSpec((ng, nb, tb, DH), (lambda j, bi=bi: (bi, 0, j, 0))) for _, ng, bi in ins]
    in_specs += [pl.BlockSpec(p.shape, lambda j: (0, 0, 0)) for p in list(prm) + list(cst)]
    y, ck = _pcall(
        body, name=name, grid=(nt,),
        in_specs=in_specs,
        out_specs=[pl.BlockSpec((NH, nb, tb, DH), lambda j: (0, 0, j, 0)),
                   pl.BlockSpec((ncb, nch, DH, DH), lambda j: (j, 0, 0, 0))],
        out_shape=[jax.ShapeDtypeStruct((NH, nb, t, DH), BF16),
                   jax.ShapeDtypeStruct((t // CH, nch, DH, DH), F32)],
        scratch_shapes=[pltpu.VMEM((nch, DH, DH), F32)],
        compiler_params=_cparams(("arbitrary",)),
    )(*[a.reshape(a.shape[0], nb, t, DH) for a, _, _ in ins], *prm, *cst)
    return y.reshape(NH, nb * t, DH), ck


def _mixer_bwd(chunk_fn, name, ins, prm, cst, ck, dy, dy_block, outs, routes, nb, t, first_fn=None):
    tb = _time_block(t)
    nt, ncb, nch = t // tb, tb // CH, nb * NH
    n_in, n_prm, n_cst, n_out = len(ins), len(prm), len(cst), len(outs)

    def body(*refs):
        in_refs = refs[:n_in]
        prm_refs = refs[n_in:n_in + n_prm]
        cst_refs = refs[n_in + n_prm:n_in + n_prm + n_cst]
        ck_ref, dy_ref = refs[n_in + n_prm + n_cst:n_in + n_prm + n_cst + 2]
        rest = refs[n_in + n_prm + n_cst + 2:]
        out_refs = rest[:n_out]
        dprm_refs = rest[n_out:n_out + n_prm]
        ds_scr = rest[n_out + n_prm]
        step_t = pl.program_id(0)

        @pl.when(step_t == 0)
        def _():
            ds_scr[...] = jnp.zeros_like(ds_scr)
            for r in dprm_refs:
                r[...] = jnp.zeros_like(r)

        def chunk(c, i):
            cst_v = [jnp.tile(r[...], (nb, 1, 1)) for r in cst_refs]
            _, vjp = jax.vjp(lambda p, x, s: chunk_fn(p, cst_v, x, s),
                             [jnp.tile(r[...], (nb, 1, 1)) for r in prm_refs],
                             [_load_chunk(r, i) for r in in_refs], ck_ref[c])
            dy_c = jnp.concatenate([dy_ref[:, b, pl.ds(i, CH), :] for b in range(nb)], axis=0)
            d_prm, d_ins, d_s = vjp((dy_c, ds_scr[...]))
            for (oi, g0), r, g in zip(routes, in_refs, d_ins):
                o_ref = out_refs[oi]
                if r.shape[0] == NH:
                    for b in range(nb):
                        o_ref[g0:g0 + NH, b, pl.ds(i, CH), :] = g[b * NH:(b + 1) * NH].astype(o_ref.dtype)
                else:
                    o_ref[g0, :, pl.ds(i, CH), :] = g.astype(o_ref.dtype)
            for r, g in zip(dprm_refs, d_prm):
                r[...] += g
            ds_scr[...] = d_s

        def first_chunk_of(c):
            one, h, b = pl.ds(c, 1), pl.ds(c % NH, 1), c // NH
            cst_v = [r[h] for r in cst_refs]
            _, vjp = jax.vjp(lambda p, x, s: first_fn(p, cst_v, x, s), [r[h] for r in prm_refs],
                             [_load_chunk_of(r, c, 0) for r in in_refs], ck_ref[0, one])
            d_prm, d_ins, d_s = vjp((dy_ref[h, b, pl.ds(0, CH), :], ds_scr[one]))
            for (oi, g0), r, g in zip(routes, in_refs, d_ins):
                o_ref = out_refs[oi]
                if r.shape[0] == NH:
                    o_ref.at[g0:g0 + NH][h, b, pl.ds(0, CH), :] = g.astype(o_ref.dtype)
                else:
                    o_ref[g0, pl.ds(b, 1), pl.ds(0, CH), :] += g.astype(o_ref.dtype)
            for r, g in zip(dprm_refs, d_prm):
                r[one] += g
            ds_scr[one] = d_s

        def first_chunk():
            for (oi, g0), r in zip(routes, in_refs):
                if r.shape[0] != NH:
                    out_refs[oi][g0, :, pl.ds(0, CH), :] = jnp.zeros((nb, CH, DH), out_refs[oi].dtype)
            _each_chain(nch, first_chunk_of)

        def step(j, carry):
            c = ncb - 1 - j
            chunk(c, pl.multiple_of(c * CH, CH))
            return carry

        lax.fori_loop(0, ncb - 1, step, 0)
        if first_fn is None:
            chunk(0, 0)
        else:
            @pl.when(step_t == nt - 1)
            def _():
                first_chunk()

            @pl.when(step_t != nt - 1)
            def _():
                chunk(0, 0)

    def back(j):
        return nt - 1 - j

    in_specs = [pl.BlockSpec((ng, nb, tb, DH), (lambda j, bi=bi: (bi, 0, back(j), 0))) for _, ng, bi in ins]
    in_specs += [pl.BlockSpec(p.shape, lambda j: (0, 0, 0)) for p in list(prm) + list(cst)]
    in_specs += [pl.BlockSpec((ncb, nch, DH, DH), lambda j: (back(j), 0, 0, 0)),
                 pl.BlockSpec((NH, nb, tb, DH), lambda j: (dy_block, 0, back(j), 0))]
    out_specs = [pl.BlockSpec((ng, nb, tb, DH), lambda j: (0, 0, back(j), 0)) for ng, _ in outs]
    out_specs += [pl.BlockSpec((nch,) + p.shape[1:], lambda j: (0, 0, 0)) for p in prm]
    out_shape = [jax.ShapeDtypeStruct((ng, nb, t, DH), dt) for ng, dt in outs]
    out_shape += [jax.ShapeDtypeStruct((nch,) + p.shape[1:], F32) for p in prm]
    res = _pcall(
        body, name=name, grid=(nt,),
        in_specs=in_specs, out_specs=out_specs, out_shape=out_shape,
        scratch_shapes=[pltpu.VMEM((nch, DH, DH), F32)],
        compiler_params=_cparams(("arbitrary",)),
    )(*[a.reshape(a.shape[0], nb, t, DH) for a, _, _ in ins], *prm, *cst, ck, dy.reshape(dy.shape[0], nb, t, DH))
    d_outs = [o.reshape(o.shape[0], nb * t, DH) for o in res[:n_out]]
    return d_outs, [g.reshape((nb,) + p.shape) for g, p in zip(res[n_out:], prm)]


def _shift_down(x, s):
    if s == 0:
        return x
    row = lax.broadcasted_iota(jnp.int32, x.shape, 0)
    return jnp.where(row < s, 0.0, pltpu.roll(x, s, 0))


def _shift_up(x, s):
    if s == 0:
        return x
    t = x.shape[0]
    row = lax.broadcasted_iota(jnp.int32, x.shape, 0)
    return jnp.where(row >= t - s, 0.0, pltpu.roll(x, t - s, 0))


def _conv_fwd(p, g0, ng, w, nb, t, name):
    taps = w.shape[1]

    def body(x_ref, w_ref, y_ref):
        x = x_ref[...]
        acc = w_ref[taps - 1:taps, :] * x
        for i in range(taps - 1):
            acc = acc + w_ref[i:i + 1, :] * _shift_down(x, taps - 1 - i)
        y_ref[...] = acc

    return _pcall(
        body, name=name, grid=(ng, nb),
        in_specs=[pl.BlockSpec((None, t, DH), lambda g, b: (g0 + g, b, 0)),
                  pl.BlockSpec((None, taps, DH), lambda g, b: (g, 0, 0))],
        out_specs=pl.BlockSpec((None, t, DH), lambda g, b: (g, b, 0)),
        out_shape=jax.ShapeDtypeStruct((ng, nb * t, DH), F32),
        compiler_params=_cparams(("parallel", "parallel")),
    )(p, w)


def _conv_bwd(p, g0, ng, w, dy, nb, t, name):
    taps = w.shape[1]

    def body(x_ref, w_ref, dy_ref, dx_ref, dw_ref):
        x = x_ref[...]
        d = dy_ref[...]
        acc = w_ref[taps - 1:taps, :] * d
        dw_ref[taps - 1:taps, :] = jnp.sum(d * x, 0, keepdims=True)
        for i in range(taps - 1):
            s = taps - 1 - i
            acc = acc + w_ref[i:i + 1, :] * _shift_up(d, s)
            dw_ref[i:i + 1, :] = jnp.sum(d * _shift_down(x, s), 0, keepdims=True)
        dx_ref[...] = acc.astype(BF16)

    return _pcall(
        body, name=name, grid=(ng, nb),
        in_specs=[pl.BlockSpec((None, t, DH), lambda g, b: (g0 + g, b, 0)),
                  pl.BlockSpec((None, taps, DH), lambda g, b: (g, 0, 0)),
                  pl.BlockSpec((None, t, DH), lambda g, b: (g, b, 0))],
        out_specs=[pl.BlockSpec((None, t, DH), lambda g, b: (g, b, 0)),
                   pl.BlockSpec((None, None, taps, DH), lambda g, b: (g, b, 0, 0))],
        out_shape=[jax.ShapeDtypeStruct((ng, nb * t, DH), BF16),
                   jax.ShapeDtypeStruct((ng, nb, taps, DH), F32)],
        compiler_params=_cparams(("parallel", "parallel")),
    )(p, w, dy)


def _mix_group(g):
    return jnp.where(g < 16, G_RWKV + g, G_RWKV_WD + g - 16)


def _mix_fwd(p, mu, nb, t, name):
    def body(x_ref, mu_ref, y_ref):
        x = x_ref[...]
        y_ref[...] = x + mu_ref[...] * (_shift_down(x, 1) - x)

    return _pcall(
        body, name=name, grid=(18, nb),
        in_specs=[pl.BlockSpec((None, t, DH), lambda g, b: (_mix_group(g), b, 0)),
                  pl.BlockSpec((None, 1, DH), lambda g, b: (g, 0, 0))],
        out_specs=pl.BlockSpec((None, t, DH), lambda g, b: (g, b, 0)),
        out_shape=jax.ShapeDtypeStruct((18, nb * t, DH), F32),
        compiler_params=_cparams(("parallel", "parallel")),
    )(p, mu)


def _mix_bwd(p, mu, dy, nb, t, name):
    def body(x_ref, mu_ref, dy_ref, dx_ref, dmu_ref):
        x = x_ref[...]
        muv = mu_ref[...]
        d = dy_ref[...]
        dx_ref[...] = (d * (1.0 - muv) + _shift_up(d * muv, 1)).astype(BF16)
        dmu_ref[...] = jnp.sum(d * (_shift_down(x, 1) - x), 0, keepdims=True)

    return _pcall(
        body, name=name, grid=(18, nb),
        in_specs=[pl.BlockSpec((None, t, DH), lambda g, b: (_mix_group(g), b, 0)),
                  pl.BlockSpec((None, 1, DH), lambda g, b: (g, 0, 0)),
                  pl.BlockSpec((None, t, DH), lambda g, b: (g, b, 0))],
        out_specs=[pl.BlockSpec((None, t, DH), lambda g, b: (g, b, 0)),
                   pl.BlockSpec((None, None, 1, DH), lambda g, b: (g, b, 0, 0))],
        out_shape=[jax.ShapeDtypeStruct((18, nb * t, DH), BF16),
                   jax.ShapeDtypeStruct((18, nb, 1, DH), F32)],
        compiler_params=_cparams(("parallel", "parallel")),
    )(p, mu, dy)


def _sc_fwd(p, w, nb, t, name):
    def body(p_ref, w_ref, y_ref):
        u = p_ref[1] * p_ref[2]
        conv = w_ref[2:3, :] * u + w_ref[1:2, :] * _shift_down(u, 1) + w_ref[0:1, :] * _shift_down(u, 2)
        y_ref[...] = (p_ref[0] * conv * _silu(p_ref[3])).astype(BF16)

    return _pcall(
        body, name=name, grid=(NH, nb),
        in_specs=[pl.BlockSpec((4, t, DH), lambda j, b: (G_SC // 4 + j, b, 0)),
                  pl.BlockSpec((None, SC_TAPS, DH), lambda j, b: (j, 0, 0))],
        out_specs=pl.BlockSpec((None, t, DH), lambda j, b: (j, b, 0)),
        out_shape=jax.ShapeDtypeStruct((NH, nb * t, DH), BF16),
        compiler_params=_cparams(("parallel", "parallel")),
    )(p, w)


def _sc_bwd(p, w, dy, nb, t, name):
    def body(p_ref, w_ref, dy_ref, dp_ref, dw_ref):
        bg, cg, xg, z = p_ref[0], p_ref[1], p_ref[2], p_ref[3]
        d = dy_ref[...]
        u = cg * xg
        u1 = _shift_down(u, 1)
        u2 = _shift_down(u, 2)
        conv = w_ref[2:3, :] * u + w_ref[1:2, :] * u1 + w_ref[0:1, :] * u2
        sg = jax.nn.sigmoid(z)
        sz = z * sg
        dp_ref[0] = (d * conv * sz).astype(BF16)
        dp_ref[3] = (d * bg * conv * (sg * (1.0 + z * (1.0 - sg)))).astype(BF16)
        dconv = d * bg * sz
        du = w_ref[2:3, :] * dconv + w_ref[1:2, :] * _shift_up(dconv, 1) + w_ref[0:1, :] * _shift_up(dconv, 2)
        dp_ref[1] = (du * xg).astype(BF16)
        dp_ref[2] = (du * cg).astype(BF16)
        dw_ref[2:3, :] = jnp.sum(dconv * u, 0, keepdims=True)
        dw_ref[1:2, :] = jnp.sum(dconv * u1, 0, keepdims=True)
        dw_ref[0:1, :] = jnp.sum(dconv * u2, 0, keepdims=True)

    return _pcall(
        body, name=name, grid=(NH, nb),
        in_specs=[pl.BlockSpec((4, t, DH), lambda j, b: (G_SC // 4 + j, b, 0)),
                  pl.BlockSpec((None, SC_TAPS, DH), lambda j, b: (j, 0, 0)),
                  pl.BlockSpec((None, t, DH), lambda j, b: (8 + j, b, 0))],
        out_specs=[pl.BlockSpec((4, t, DH), lambda j, b: (j, b, 0)),
                   pl.BlockSpec((None, None, SC_TAPS, DH), lambda j, b: (j, b, 0, 0))],
        out_shape=[jax.ShapeDtypeStruct((4 * NH, nb * t, DH), BF16),
                   jax.ShapeDtypeStruct((NH, nb, SC_TAPS, DH), F32)],
        compiler_params=_cparams(("parallel", "parallel")),
    )(p, w, dy)


def _row_tile(n):
    return 1024 if n % 1024 == 0 else n


def _regroup_in(w_all, l, name):
    tr = 256
    gs = GROUPS_PER_STEP

    def body(w_ref, o_ref):
        for g in _PADDED_GROUPS:
            o_ref[g // gs, :, DH * (g % gs):DH * (g % gs + 1)] = jnp.zeros((tr, DH), BF16)
        for g, a, d, off, ln in _SEGMENTS:
            lane = DH * (g % gs) + a
            o_ref[g // gs, :, lane:lane + ln] = w_ref[d, :, off:off + ln].astype(BF16)

    return _pcall(
        body, name=name, grid=(D_MODEL // tr,),
        in_specs=[pl.BlockSpec((N_DEV, None, tr, SHARD_COLS), lambda i: (0, l, i, 0))],
        out_specs=pl.BlockSpec((N_GROUPS // gs, tr, gs * DH), lambda i: (0, i, 0)),
        out_shape=jax.ShapeDtypeStruct((N_GROUPS // gs, D_MODEL, gs * DH), BF16),
        compiler_params=_cparams(("parallel",)),
    )(w_all)


def _regroup_out(dwg, name):
    tr = 256
    gs = GROUPS_PER_STEP

    def body(g_ref, o_ref):
        for g, a, d, off, ln in _SEGMENTS:
            lane = DH * (g % gs) + a
            o_ref[d, :, off:off + ln] = g_ref[g // gs, :, lane:lane + ln]

    return _pcall(
        body, name=name, grid=(D_MODEL // tr,),
        in_specs=[pl.BlockSpec((N_GROUPS // gs, tr, gs * DH), lambda i: (0, i, 0))],
        out_specs=pl.BlockSpec((N_DEV, tr, SHARD_COLS), lambda i: (0, i, 0)),
        out_shape=jax.ShapeDtypeStruct((N_DEV, D_MODEL, SHARD_COLS), F32),
        compiler_params=_cparams(("parallel",)),
    )(dwg)


def _norm_proj(x, pre_w, w_g, name):
    n = x.shape[0]
    tm = _row_tile(n)
    gs = GROUPS_PER_STEP

    def body(x_ref, pw_ref, w_ref, h_ref, p_ref):
        @pl.when(pl.program_id(1) == 0)
        def _():
            xv = x_ref[...]
            h = xv * lax.rsqrt(jnp.mean(xv * xv, -1, keepdims=True) + EPS) * pw_ref[...]
            h_ref[...] = h.astype(BF16)

        r = jnp.dot(h_ref[...], w_ref[...], preferred_element_type=F32)
        for k in range(gs):
            p_ref[k] = r[:, DH * k:DH * (k + 1)]

    return _pcall(
        body, name=name, grid=(n // tm, N_GROUPS // gs),
        in_specs=[pl.BlockSpec((tm, D_MODEL), lambda i, j: (i, 0)),
                  pl.BlockSpec((1, D_MODEL), lambda i, j: (0, 0)),
                  pl.BlockSpec((None, D_MODEL, gs * DH), lambda i, j: (j, 0, 0))],
        out_specs=[pl.BlockSpec((tm, D_MODEL), lambda i, j: (i, 0)),
                   pl.BlockSpec((gs, tm, DH), lambda i, j: (j, i, 0))],
        out_shape=[jax.ShapeDtypeStruct((n, D_MODEL), BF16),
                   jax.ShapeDtypeStruct((N_GROUPS, n, DH), F32)],
        compiler_params=_cparams(("parallel", "arbitrary")),
    )(x, pre_w, w_g)


def _out_proj_norm(ys, wout_g, x, post_w, name):
    n = x.shape[0]
    tm = _row_tile(n)

    def body(y0, y1, y2, y3, w_ref, x_ref, pw_ref, out_ref, xn_ref):
        acc = jnp.zeros((tm, D_MODEL), F32)
        for m, yr in enumerate((y0, y1, y2, y3)):
            for h in range(NH):
                acc = acc + jnp.dot(yr[h], w_ref[m * NH + h], preferred_element_type=F32)
        out_ref[...] = acc
        xn_ref[...] = x_ref[...] + acc * lax.rsqrt(jnp.mean(acc * acc, -1, keepdims=True) + EPS) * pw_ref[...]

    yspec = pl.BlockSpec((NH, tm, DH), lambda i: (0, i, 0))
    rows = pl.BlockSpec((tm, D_MODEL), lambda i: (i, 0))
    return _pcall(
        body, name=name, grid=(n // tm,),
        in_specs=[yspec] * 4 + [pl.BlockSpec((4 * NH, DH, D_MODEL), lambda i: (0, 0, 0)), rows,
                                pl.BlockSpec((1, D_MODEL), lambda i: (0, 0))],
        out_specs=[rows, rows],
        out_shape=[jax.ShapeDtypeStruct((n, D_MODEL), F32)] * 2,
        compiler_params=_cparams(("parallel",)),
    )(*ys, wout_g, x, post_w)


def _loss_grad(x, tgt, name):
    n = x.shape[0]
    tm = _row_tile(n)

    def body(x_ref, t_ref, dx_ref, l_ref):
        @pl.when(pl.program_id(0) == 0)
        def _():
            l_ref[...] = jnp.zeros_like(l_ref)

        e = x_ref[...] - t_ref[...]
        dx_ref[...] = e * (1.0 / D_MODEL)
        l_ref[...] += jnp.sum(jnp.sum(e * e, -1, keepdims=True), 0, keepdims=True) * (0.5 / D_MODEL)

    rows = pl.BlockSpec((tm, D_MODEL), lambda i: (i, 0))
    return _pcall(
        body, name=name, grid=(n // tm,),
        in_specs=[rows, rows],
        out_specs=[rows, pl.BlockSpec((1, 128), lambda i: (0, 0))],
        out_shape=[jax.ShapeDtypeStruct((n, D_MODEL), F32), jax.ShapeDtypeStruct((1, 128), F32)],
        compiler_params=_cparams(("arbitrary",)),
    )(x, tgt)


def _rmsnorm_bwd(xv, w, d):
    r = lax.rsqrt(jnp.mean(xv * xv, -1, keepdims=True) + EPS)
    xh = xv * r
    dxh = d * w
    dx = r * (dxh - xh * jnp.mean(dxh * xh, -1, keepdims=True))
    return dx, d * xh


def _post_bwd(dxn, out, post_w, wout_g, name):
    n = dxn.shape[0]
    tm = _row_tile(n)

    def body(d_ref, o_ref, pw_ref, w_ref, do_ref, dy_ref, dpw_ref):
        @pl.when(pl.program_id(0) == 0)
        def _():
            dpw_ref[...] = jnp.zeros_like(dpw_ref)

        dout, dw_rows = _rmsnorm_bwd(o_ref[...], pw_ref[...], d_ref[...])
        dpw_ref[...] += jnp.sum(dw_rows, 0, keepdims=True)
        db = dout.astype(BF16)
        do_ref[...] = db
        for g in range(4 * NH):
            dy_ref[g] = lax.dot_general(db, w_ref[g], (((1,), (1,)), ((), ())), preferred_element_type=F32)

    rows = pl.BlockSpec((tm, D_MODEL), lambda i: (i, 0))
    vec = pl.BlockSpec((1, D_MODEL), lambda i: (0, 0))
    return _pcall(
        body, name=name, grid=(n // tm,),
        in_specs=[rows, rows, vec, pl.BlockSpec((4 * NH, DH, D_MODEL), lambda i: (0, 0, 0))],
        out_specs=[rows, pl.BlockSpec((4 * NH, tm, DH), lambda i: (0, i, 0)), vec],
        out_shape=[jax.ShapeDtypeStruct((n, D_MODEL), BF16),
                   jax.ShapeDtypeStruct((4 * NH, n, DH), F32),
                   jax.ShapeDtypeStruct((1, D_MODEL), F32)],
        compiler_params=_cparams(("arbitrary",)),
    )(dxn, out, post_w, wout_g)


def _dwout(ys, dout, name):
    n = dout.shape[0]
    tm = _row_tile(n)

    def body(y0, y1, y2, y3, d_ref, dw_ref):
        @pl.when(pl.program_id(0) == 0)
        def _():
            dw_ref[...] = jnp.zeros_like(dw_ref)

        d = d_ref[...]
        for m, yr in enumerate((y0, y1, y2, y3)):
            for h in range(NH):
                dw_ref[m * NH + h] += lax.dot_general(yr[h], d, (((0,), (0,)), ((), ())),
                                                      preferred_element_type=F32)

    yspec = pl.BlockSpec((NH, tm, DH), lambda i: (0, i, 0))
    return _pcall(
        body, name=name, grid=(n // tm,),
        in_specs=[yspec] * 4 + [pl.BlockSpec((tm, D_MODEL), lambda i: (i, 0))],
        out_specs=pl.BlockSpec((4 * NH, DH, D_MODEL), lambda i: (0, 0, 0)),
        out_shape=jax.ShapeDtypeStruct((4 * NH, DH, D_MODEL), F32),
        compiler_params=_cparams(("arbitrary",)),
    )(*ys, dout)


def _source_specs(sources, rows_first):
    gs = GROUPS_PER_STEP
    spans, specs, j0 = [], [], 0
    for a in sources:
        nblk = a.shape[0] // gs
        spans.append((j0, j0 + nblk))
        shape = (gs, _row_tile(a.shape[1]), DH)

        def blk(j, j0=j0, nblk=nblk):
            return jnp.clip(j - j0, 0, nblk - 1)

        if rows_first:
            specs.append(pl.BlockSpec(shape, (lambda i, j, blk=blk: (blk(j), i, 0))))
        else:
            specs.append(pl.BlockSpec(shape, (lambda j, i, blk=blk: (blk(j), i, 0))))
        j0 += nblk
    return spans, specs


def _dh_prenorm_bwd(sources, w_g, x, pre_w, dxn, name):
    n = x.shape[0]
    tm = _row_tile(n)
    gs = GROUPS_PER_STEP
    nj = N_GROUPS // gs
    spans, src_specs = _source_specs(sources, True)
    ns = len(sources)

    def body(*refs):
        src = refs[:ns]
        w_ref, x_ref, pw_ref, d_ref, dx_ref, dpw_ref, acc = refs[ns:]
        i, j = pl.program_id(0), pl.program_id(1)

        @pl.when((i == 0) & (j == 0))
        def _():
            dpw_ref[...] = jnp.zeros_like(dpw_ref)

        @pl.when(j == 0)
        def _():
            acc[...] = jnp.zeros_like(acc)

        for s_ref, (lo, hi) in zip(src, spans):
            @pl.when((j >= lo) & (j < hi))
            def _(s_ref=s_ref):
                four = jnp.concatenate([s_ref[k] for k in range(gs)], axis=-1)
                acc[...] += lax.dot_general(four, w_ref[...], (((1,), (1,)), ((), ())), preferred_element_type=F32)

        @pl.when(j == nj - 1)
        def _():
            dx, dw_rows = _rmsnorm_bwd(x_ref[...], pw_ref[...], acc[...])
            dx_ref[...] = d_ref[...] + dx
            dpw_ref[...] += jnp.sum(dw_rows, 0, keepdims=True)

    rows = pl.BlockSpec((tm, D_MODEL), lambda i, j: (i, 0))
    vec = pl.BlockSpec((1, D_MODEL), lambda i, j: (0, 0))
    return _pcall(
        body, name=name, grid=(n // tm, nj),
        in_specs=src_specs + [pl.BlockSpec((None, D_MODEL, gs * DH), lambda i, j: (j, 0, 0)), rows, vec, rows],
        out_specs=[rows, vec],
        out_shape=[jax.ShapeDtypeStruct((n, D_MODEL), F32), jax.ShapeDtypeStruct((1, D_MODEL), F32)],
        scratch_shapes=[pltpu.VMEM((tm, D_MODEL), F32)],
        compiler_params=_cparams(("arbitrary", "arbitrary")),
    )(*sources, w_g, x, pre_w, dxn)


def _dwin(hb, sources, name):
    n = hb.shape[0]
    tm = _row_tile(n)
    gs = GROUPS_PER_STEP
    spans, src_specs = _source_specs(sources, False)
    ns = len(sources)

    def body(*refs):
        h_ref = refs[0]
        src = refs[1:1 + ns]
        dw_ref = refs[1 + ns]
        j = pl.program_id(0)

        @pl.when(pl.program_id(1) == 0)
        def _():
            dw_ref[...] = jnp.zeros_like(dw_ref)

        h = h_ref[...]
        for s_ref, (lo, hi) in zip(src, spans):
            @pl.when((j >= lo) & (j < hi))
            def _(s_ref=s_ref):
                four = jnp.concatenate([s_ref[k] for k in range(gs)], axis=-1)
                dw_ref[...] += jnp.dot(h, four, preferred_element_type=F32)

    return _pcall(
        body, name=name, grid=(N_GROUPS // gs, n // tm),
        in_specs=[pl.BlockSpec((D_MODEL, tm), lambda j, i: (0, i))] + src_specs,
        out_specs=pl.BlockSpec((None, D_MODEL, gs * DH), lambda j, i: (j, 0, 0)),
        out_shape=jax.ShapeDtypeStruct((N_GROUPS // gs, D_MODEL, gs * DH), F32),
        compiler_params=_cparams(("parallel", "arbitrary")),
    )(jnp.transpose(hb), *sources)


def _adamw_math(w, g, m, v):
    c1 = 1.0 - ADAM_B1 ** ADAM_STEP
    c2 = 1.0 - ADAM_B2 ** ADAM_STEP
    nm = ADAM_B1 * m + (1.0 - ADAM_B1) * g
    nv = ADAM_B2 * v + (1.0 - ADAM_B2) * (g * g)
    return -ADAM_LR * ((nm / c1) / (jnp.sqrt(nv / c2) + ADAM_EPS) + ADAM_WD * w), nm, nv


def _adamw(w, g, m, v, name):
    r, c = w.shape
    tr = 256 if r % 256 == 0 else r

    def body(w_ref, g_ref, m_ref, v_ref, d_ref, nm_ref, nv_ref):
        d_ref[...], nm_ref[...], nv_ref[...] = _adamw_math(w_ref[...], g_ref[...], m_ref[...], v_ref[...])

    spec = pl.BlockSpec((tr, c), lambda i: (i, 0))
    return _pcall(
        body, name=name, grid=(r // tr,),
        in_specs=[spec] * 4, out_specs=[spec] * 3,
        out_shape=[jax.ShapeDtypeStruct((r, c), F32)] * 3,
        compiler_params=_cparams(("parallel",)),
    )(w, g, m, v)


def _sum_adamw(parts, w, m, v, name):
    r, c = w.shape
    tr = 128 if r % 128 == 0 else r

    def body(p_ref, w_ref, m_ref, v_ref, g_ref, d_ref, nm_ref, nv_ref):
        g = p_ref[0]
        for k in range(1, N_DEV):
            g = g + p_ref[k]
        g_ref[...] = g
        d_ref[...], nm_ref[...], nv_ref[...] = _adamw_math(w_ref[...], g, m_ref[...], v_ref[...])

    spec = pl.BlockSpec((tr, c), lambda i: (i, 0))
    return _pcall(
        body, name=name, grid=(r // tr,),
        in_specs=[pl.BlockSpec((N_DEV, tr, c), lambda i: (0, i, 0))] + [spec] * 3, out_specs=[spec] * 4,
        out_shape=[jax.ShapeDtypeStruct((r, c), F32)] * 4,
        compiler_params=_cparams(("parallel",)),
    )(parts, w, m, v)


def _me():
    return lax.axis_index("x"), lax.axis_index("y"), lax.axis_index("c")


def _flat(x, y, c):
    return 4 * x + 2 * y + c


def _peer(k):
    x, y, c = _me()
    return (x ^ ((k >> 2) & 1), y ^ ((k >> 1) & 1), c ^ (k & 1))


def _all_gather(blocks, name):
    na = len(blocks)

    def body(*refs):
        x_refs, out_refs = refs[:na], refs[na:2 * na]
        send_sems, recv_sems, local_sems = refs[2 * na:]
        me = _flat(*_me())
        local = []
        for a in range(na):
            cp = pltpu.make_async_copy(x_refs[a], out_refs[a].at[me], local_sems.at[a])
            cp.start()
            local.append(cp)
        copies = []
        for k in range(1, N_DEV):
            for a in range(na):
                cp = pltpu.make_async_remote_copy(
                    src_ref=x_refs[a], dst_ref=out_refs[a].at[me],
                    send_sem=send_sems.at[a, k - 1], recv_sem=recv_sems.at[a, k - 1],
                    device_id=_peer(k), device_id_type=MESH)
                cp.start()
                copies.append(cp)
        for k in range(1, N_DEV):
            src = _flat(*_peer(k))
            for a in range(na):
                pltpu.make_async_remote_copy(
                    src_ref=x_refs[a], dst_ref=out_refs[a].at[src],
                    send_sem=send_sems.at[a, k - 1], recv_sem=recv_sems.at[a, k - 1],
                    device_id=_peer(k), device_id_type=MESH).wait_recv()
        for cp in copies:
            cp.wait_send()
        for cp in local:
            cp.wait()

    return _pcall(
        body, name=name,
        in_specs=[pl.BlockSpec(memory_space=pl.ANY)] * na,
        out_specs=[pl.BlockSpec(memory_space=pl.ANY)] * na,
        out_shape=[jax.ShapeDtypeStruct((N_DEV,) + b.shape, b.dtype) for b in blocks],
        scratch_shapes=[pltpu.SemaphoreType.DMA((na, N_DEV - 1)), pltpu.SemaphoreType.DMA((na, N_DEV - 1)),
                        pltpu.SemaphoreType.DMA((na,))],
    )(*blocks)


def _exchange(sends, layouts, out_shapes, name):
    ns, no = len(sends), len(out_shapes)

    def body(*refs):
        s_refs, out_refs = refs[:ns], refs[ns:ns + no]
        send_sems, recv_sems, local_sems = refs[ns + no:]
        me = _flat(*_me())

        def dst(i, k):
            o, pos = layouts[i]
            return out_refs[o].at[k] if pos is None else out_refs[o].at[k, pos]

        local = []
        for i in range(ns):
            cp = pltpu.make_async_copy(s_refs[i].at[me], dst(i, 0), local_sems.at[i])
            cp.start()
            local.append(cp)
        copies = []
        for k in range(1, N_DEV):
            to = _flat(*_peer(k))
            for i in range(ns):
                cp = pltpu.make_async_remote_copy(
                    src_ref=s_refs[i].at[to], dst_ref=dst(i, k),
                    send_sem=send_sems.at[i, k - 1], recv_sem=recv_sems.at[i, k - 1],
                    device_id=_peer(k), device_id_type=MESH)
                cp.start()
                copies.append(cp)
        for cp in copies:
            cp.wait_recv()
        for cp in copies:
            cp.wait_send()
        for cp in local:
            cp.wait()

    return _pcall(
        body, name=name,
        in_specs=[pl.BlockSpec(memory_space=pl.ANY)] * ns,
        out_specs=[pl.BlockSpec(memory_space=pl.ANY)] * no,
        out_shape=[jax.ShapeDtypeStruct(s, F32) for s in out_shapes],
        scratch_shapes=[pltpu.SemaphoreType.DMA((ns, N_DEV - 1)), pltpu.SemaphoreType.DMA((ns, N_DEV - 1)),
                        pltpu.SemaphoreType.DMA((ns,))],
    )(*sends)


def _sum_slots(a, name):
    r = a.shape[1]

    def body(a_ref, o_ref):
        acc = a_ref[0]
        for d in range(1, N_DEV):
            acc = acc + a_ref[d]
        o_ref[...] = acc

    return _pcall(body, name=name, out_shape=jax.ShapeDtypeStruct((r, 128), F32), compiler_params=_cparams())(a)


def _all_reduce_small(blk, name):
    r = blk.shape[0]

    def body(x_ref, out_ref, gath, send_sems, recv_sems):
        me = _flat(*_me())
        gath[me] = x_ref[...]
        copies = []
        for k in range(1, N_DEV):
            cp = pltpu.make_async_remote_copy(
                src_ref=x_ref, dst_ref=gath.at[me],
                send_sem=send_sems.at[k - 1], recv_sem=recv_sems.at[k - 1],
                device_id=_peer(k), device_id_type=MESH)
            cp.start()
            copies.append(cp)
        for k in range(1, N_DEV):
            src = _flat(*_peer(k))
            pltpu.make_async_remote_copy(
                src_ref=x_ref, dst_ref=gath.at[src],
                send_sem=send_sems.at[k - 1], recv_sem=recv_sems.at[k - 1],
                device_id=_peer(k), device_id_type=MESH).wait_recv()
        for cp in copies:
            cp.wait_send()
        acc = gath[0]
        for d in range(1, N_DEV):
            acc = acc + gath[d]
        out_ref[...] = acc

    return _pcall(
        body, name=name,
        in_specs=[pl.BlockSpec(memory_space=pltpu.VMEM)],
        out_specs=pl.BlockSpec(memory_space=pltpu.VMEM),
        out_shape=jax.ShapeDtypeStruct((r, 128), F32),
        scratch_shapes=[pltpu.VMEM((N_DEV, r, 128), F32),
                        pltpu.SemaphoreType.DMA((N_DEV - 1,)), pltpu.SemaphoreType.DMA((N_DEV - 1,))],
    )(blk)


def _heads(vec):
    return vec.reshape(NH, 1, DH)


def _rep(vec4):
    return jnp.broadcast_to(vec4.reshape(NH, 1, 1), (NH, 1, DH))


def _onehot_lane(offset):
    m = np.zeros((NH, 1, DH), np.float32)
    for h in range(NH):
        m[h, 0, offset + h] = 1.0
    return jnp.asarray(m)


_TINY = (("gdn_conv_w", (DEPTH, 4, 96)), ("rwkv_w_up", (DEPTH, 64, 32)), ("rwkv_a_up", (DEPTH, 64, 32)),
         ("sc_conv_w", (DEPTH, 3, 32)))
_TINY_ROWS = -(-sum(int(np.prod(s)) for _, s in _TINY) // 1024) * 8


def _pack_rows(arrays, rows, fill=0.0):
    flat = jnp.concatenate([a.reshape(-1) for a in arrays])
    return jnp.pad(flat, (0, rows * 128 - flat.shape[0]), constant_values=fill).reshape(rows, 128)


def _unpack_rows(p, named_shapes):
    lead = p.shape[:-2]
    flat = p.reshape(lead + (-1,))
    out, o = {}, 0
    for n, s in named_shapes:
        size = int(np.prod(s))
        out[n] = flat[..., o:o + size].reshape(lead + tuple(s))
        o += size
    return out


def _gather_last(a):
    return jnp.transpose(a, (1, 0, 2)).reshape(a.shape[1], -1)


def _split_last(a):
    r, c8 = a.shape
    return jnp.transpose(a.reshape(r, N_DEV, c8 // N_DEV), (1, 0, 2))


_SMALL = (("pre_norm_w", (DEPTH, 1024)), ("gdn_a_log", (DEPTH, 4)), ("gdn_dt_bias", (DEPTH, 4)),
          ("gdn_norm_w", (DEPTH, 64)), ("rwkv_mu", (DEPTH, 1152)), ("rwkv_w0", (DEPTH, 256)),
          ("rwkv_a0", (DEPTH, 256)), ("rwkv_k_k", (DEPTH, 256)), ("rwkv_k_a", (DEPTH, 256)),
          ("rwkv_r_k", (DEPTH, 256)), ("rwkv_ln_w", (DEPTH, 256)), ("rwkv_ln_b", (DEPTH, 256)),
          ("gla_a_up", (DEPTH, 16, 128)), ("gla_a_bias", (DEPTH, 128)), ("gla_norm_w", (DEPTH, 64)),
          ("post_norm_w", (DEPTH, 1024)), ("loss", ()))
_SMALL_ROWS = -(-sum(int(np.prod(s)) for _, s in _SMALL) // 1024) * 8


def _layer_params(wts, tiny, w_in_all, w_out_all, l):
    conv = _gather_last(tiny["gdn_conv_w"][:, l])
    q = {}
    q["gdn_conv"] = jnp.transpose(conv.reshape(GDN_TAPS, 12, DH), (1, 0, 2))
    q["gdn_prm"] = [_rep(wts["gdn_a_log"][l]), _rep(wts["gdn_dt_bias"][l]),
                    jnp.broadcast_to(wts["gdn_norm_w"][l].reshape(1, 1, DH), (NH, 1, DH))]
    q["gdn_cst"] = [_onehot_lane(0), _onehot_lane(NH)]
    q["rwkv_mu"] = wts["rwkv_mu"][l].reshape(18, 1, DH)
    w_up = jnp.transpose(_gather_last(tiny["rwkv_w_up"][:, l]).reshape(64, NH, DH), (1, 0, 2))
    a_up = jnp.transpose(_gather_last(tiny["rwkv_a_up"][:, l]).reshape(64, NH, DH), (1, 0, 2))
    q["rwkv_prm"] = [_heads(wts["rwkv_w0"][l]), w_up, _heads(wts["rwkv_a0"][l]), a_up,
                     _heads(wts["rwkv_k_k"][l]), _heads(wts["rwkv_k_a"][l]), _heads(wts["rwkv_r_k"][l]),
                     _heads(wts["rwkv_ln_w"][l]), _heads(wts["rwkv_ln_b"][l])]
    sc = _gather_last(tiny["sc_conv_w"][:, l])
    q["sc_conv"] = jnp.transpose(sc.reshape(SC_TAPS, NH, DH), (1, 0, 2))
    gla_up = jnp.transpose(wts["gla_a_up"][l].reshape(16, NH, GLA_HEAD_K), (1, 0, 2))
    gla_up = jnp.pad(gla_up, ((0, 0), (0, DH - 16), (0, DH - GLA_HEAD_K)))
    gla_b = jnp.pad(wts["gla_a_bias"][l].reshape(NH, 1, GLA_HEAD_K), ((0, 0), (0, 0), (0, DH - GLA_HEAD_K)))
    q["gla_prm"] = [gla_up, gla_b, jnp.broadcast_to(wts["gla_norm_w"][l].reshape(1, 1, DH), (NH, 1, DH))]
    q["w_g"] = _regroup_in(w_in_all, l, f"regroup_in{l}")
    q["wout_g"] = w_out_all[:, l].reshape(4 * NH, DH, D_MODEL).astype(BF16)
    q["pre_w"] = wts["pre_norm_w"][l].reshape(1, D_MODEL)
    q["post_w"] = wts["post_norm_w"][l].reshape(1, D_MODEL)
    return q


def _mixer_inputs(p, cq, pm):
    gdn = [(cq, 4, 0), (cq, 4, 1), (cq, 4, 2), (p, 4, G_GDN // 4 + 3), (p, 1, G_GDN_AB)]
    rwkv = [(pm, 4, 0), (pm, 4, 1), (pm, 4, 2), (pm, 4, 3), (pm, 1, 16), (pm, 1, 17)]
    gla = [(p, 4, G_GLA // 4 + k) for k in range(4)] + [(p, 1, G_GLA_AD)]
    return gdn, rwkv, gla


def _layer_fwd(x, q, nb, t, l):
    hb, p = _norm_proj(x, q["pre_w"], q["w_g"], f"norm_proj{l}")
    cq = _conv_fwd(p, G_GDN, 12, q["gdn_conv"], nb, t, f"gdn_conv{l}")
    pm = _mix_fwd(p, q["rwkv_mu"], nb, t, f"rwkv_mix{l}")
    gdn_in, rwkv_in, gla_in = _mixer_inputs(p, cq, pm)
    y_gdn, ck_gdn = _mixer_fwd(_gdn_chunk, f"gdn_fwd{l}", gdn_in, q["gdn_prm"], q["gdn_cst"], nb, t)
    y_rwkv, ck_rwkv = _mixer_fwd(_rwkv_chunk, f"rwkv_fwd{l}", rwkv_in, q["rwkv_prm"], [], nb, t,
                                 first_fn=_rwkv_chunk_steps)
    y_sc = _sc_fwd(p, q["sc_conv"], nb, t, f"sc_fwd{l}")
    y_gla, ck_gla = _mixer_fwd(_gla_chunk, f"gla_fwd{l}", gla_in, q["gla_prm"], [], nb, t)
    ys = (y_gdn, y_rwkv, y_sc, y_gla)
    out, xn = _out_proj_norm(ys, q["wout_g"], x, q["post_w"], f"out_proj{l}")
    saved = dict(x=x, hb=hb, p=p, cq=cq, pm=pm, ys=ys, out=out, ck=(ck_gdn, ck_rwkv, ck_gla))
    return xn, saved


def _layer_bwd(dxn, q, sv, nb, t, l):
    p, cq, pm, ys = sv["p"], sv["cq"], sv["pm"], sv["ys"]
    dout, dy, d_post = _post_bwd(dxn, sv["out"], q["post_w"], q["wout_g"], f"post_bwd{l}")
    d_wout = _dwout(ys, dout, f"dwout{l}").reshape(N_DEV, 128, D_MODEL)
    gdn_in, rwkv_in, gla_in = _mixer_inputs(p, cq, pm)
    ck_gdn, ck_rwkv, ck_gla = sv["ck"]
    g = {}

    (d_conv, dz, dab), (da_log, ddt, dnw) = _mixer_bwd(
        _gdn_chunk, f"gdn_bwd{l}", gdn_in, q["gdn_prm"], q["gdn_cst"], ck_gdn, dy, 0,
        [(12, F32), (4, BF16), (1, BF16)], [(0, 0), (0, 4), (0, 8), (1, 0), (2, 0)], nb, t)
    dconv_in, d_gconv = _conv_bwd(p, G_GDN, 12, q["gdn_conv"], d_conv, nb, t, f"gdn_conv_bwd{l}")
    g["gdn_conv_w"] = jnp.transpose(d_gconv.sum(1), (1, 0, 2)).reshape(GDN_TAPS, 768)
    g["gdn_a_log"] = da_log.sum((0, 2, 3))
    g["gdn_dt_bias"] = ddt.sum((0, 2, 3))
    g["gdn_norm_w"] = dnw.sum((0, 1, 2))

    (d_pm,), d_rprm = _mixer_bwd(
        _rwkv_chunk, f"rwkv_bwd{l}", rwkv_in, q["rwkv_prm"], [], ck_rwkv, dy, 1,
        [(18, F32)], [(0, 0), (0, 4), (0, 8), (0, 12), (0, 16), (0, 17)], nb, t, first_fn=_rwkv_chunk_steps)
    dp_rwkv, d_mu = _mix_bwd(p, q["rwkv_mu"], d_pm, nb, t, f"rwkv_mix_bwd{l}")
    g["rwkv_mu"] = d_mu.sum(1).reshape(1152)
    rp = [a.sum(0) for a in d_rprm]
    g["rwkv_w0"] = rp[0].reshape(256)
    g["rwkv_w_up"] = jnp.transpose(rp[1], (1, 0, 2)).reshape(64, 256)
    g["rwkv_a0"] = rp[2].reshape(256)
    g["rwkv_a_up"] = jnp.transpose(rp[3], (1, 0, 2)).reshape(64, 256)
    for i, nme in enumerate(("rwkv_k_k", "rwkv_k_a", "rwkv_r_k", "rwkv_ln_w", "rwkv_ln_b")):
        g[nme] = rp[4 + i].reshape(256)

    dp_sc, d_scw = _sc_bwd(p, q["sc_conv"], dy, nb, t, f"sc_bwd{l}")
    g["sc_conv_w"] = jnp.transpose(d_scw.sum(1), (1, 0, 2)).reshape(SC_TAPS, 256)

    (dp_gla, dad), (d_aup, d_ab, d_gnw) = _mixer_bwd(
        _gla_chunk, f"gla_bwd{l}", gla_in, q["gla_prm"], [], ck_gla, dy, 3,
        [(16, BF16), (1, BF16)], [(0, 0), (0, 4), (0, 8), (0, 12), (1, 0)], nb, t)
    g["gla_a_up"] = jnp.transpose(d_aup.sum(0)[:, :16, :GLA_HEAD_K], (1, 0, 2)).reshape(16, 128)
    g["gla_a_bias"] = d_ab.sum(0)[:, 0, :GLA_HEAD_K].reshape(128)
    g["gla_norm_w"] = d_gnw.sum((0, 1, 2))

    singles = jnp.concatenate([dab, dp_rwkv[16:18], dad], axis=0)
    sources = [dconv_in, dz, dp_rwkv, dp_sc, dp_gla, singles]
    dx, d_pre = _dh_prenorm_bwd(sources, q["w_g"], sv["x"], q["pre_w"], dxn, f"dh_bwd{l}")
    d_win = _regroup_out(_dwin(sv["hb"], sources, f"dwin{l}"), f"regroup_out{l}")
    g["pre_norm_w"] = d_pre.reshape(D_MODEL)
    g["post_norm_w"] = d_post.reshape(D_MODEL)
    return dx, g, d_win, d_wout


def _local_step(x, tgt, wts, tiny, w_in_all, w_out_all):
    nb, t, d = x.shape
    xf = x.reshape(nb * t, d)
    qs, saved = [], []
    for l in range(DEPTH):
        q = _layer_params(wts, tiny, w_in_all, w_out_all, l)
        xf, sv = _layer_fwd(xf, q, nb, t, l)
        qs.append(q)
        saved.append(sv)
    dxf, lpart = _loss_grad(xf, tgt.reshape(nb * t, d), "loss")
    grads, d_win, d_wout = [None] * DEPTH, [None] * DEPTH, [None] * DEPTH
    for l in reversed(range(DEPTH)):
        dxf, grads[l], d_win[l], d_wout[l] = _layer_bwd(dxf, qs[l], saved[l], nb, t, l)
    small = {k: jnp.stack([grads[l][k] for l in range(DEPTH)]) for k in grads[0]}
    return lpart[0, 0], dxf.reshape(nb, t, d), small, d_win, d_wout


_WEIGHTS = ("pre_norm_w", "w_in", "gdn_conv_w", "gdn_a_log", "gdn_dt_bias", "gdn_norm_w", "rwkv_mu", "rwkv_w0",
            "rwkv_w_up", "rwkv_a0", "rwkv_a_up", "rwkv_k_k", "rwkv_k_a", "rwkv_r_k", "rwkv_ln_w", "rwkv_ln_b",
            "sc_conv_w", "gla_a_up", "gla_a_bias", "gla_norm_w", "w_out", "post_norm_w")


def kernel(x, pre_norm_w, w_in, gdn_conv_w, gdn_a_log, gdn_dt_bias, gdn_norm_w, rwkv_mu, rwkv_w0, rwkv_w_up, rwkv_a0, rwkv_a_up, rwkv_k_k, rwkv_k_a, rwkv_r_k, rwkv_ln_w, rwkv_ln_b, sc_conv_w, gla_a_up, gla_a_bias, gla_norm_w, w_out, post_norm_w, loss_target, m_pre_norm_w, m_w_in, m_gdn_conv_w, m_gdn_a_log, m_gdn_dt_bias, m_gdn_norm_w, m_rwkv_mu, m_rwkv_w0, m_rwkv_w_up, m_rwkv_a0, m_rwkv_a_up, m_rwkv_k_k, m_rwkv_k_a, m_rwkv_r_k, m_rwkv_ln_w, m_rwkv_ln_b, m_sc_conv_w, m_gla_a_up, m_gla_a_bias, m_gla_norm_w, m_w_out, m_post_norm_w, v_pre_norm_w, v_w_in, v_gdn_conv_w, v_gdn_a_log, v_gdn_dt_bias, v_gdn_norm_w, v_rwkv_mu, v_rwkv_w0, v_rwkv_w_up, v_rwkv_a0, v_rwkv_a_up, v_rwkv_k_k, v_rwkv_k_a, v_rwkv_r_k, v_rwkv_ln_w, v_rwkv_ln_b, v_sc_conv_w, v_gla_a_up, v_gla_a_bias, v_gla_norm_w, v_w_out, v_post_norm_w):
    env = dict(locals())
    w = {n: env[n] for n in _WEIGHTS}
    m = {n: env["m_" + n] for n in _WEIGHTS}
    v = {n: env["v_" + n] for n in _WEIGHTS}
    tiny_names = [n for n, _ in _TINY]

    w_in_all, w_out_all, tiny_all = _all_gather(
        [w_in.astype(BF16), w_out.astype(BF16), _pack_rows([w[n] for n in tiny_names], _TINY_ROWS)],
        "gather_weights")
    tiny = _unpack_rows(tiny_all, _TINY)

    lpart, grad_x, small, d_win, d_wout = _local_step(x, loss_target, w, tiny, w_in_all, w_out_all)

    tiny_send = jnp.stack([_pack_rows([_split_last(small[n][l])[d] for n in tiny_names for l in range(DEPTH)],
                                      _TINY_ROWS) for d in range(N_DEV)])
    r_win, r_wout, r_tiny = _exchange(
        [d_win[0], d_win[1], d_wout[0], d_wout[1], tiny_send],
        [(0, 0), (0, 1), (1, 0), (1, 1), (2, None)],
        [(N_DEV, DEPTH, D_MODEL, SHARD_COLS), (N_DEV, DEPTH, 128, D_MODEL), (N_DEV, _TINY_ROWS, 128)],
        "scatter_grads")
    grads, delta, new_m, new_v = {}, {}, {}, {}
    for n, parts in (("w_in", r_win), ("w_out", r_wout)):
        shp = w[n].shape
        two = lambda a: a.reshape(-1, shp[-1])
        res = _sum_adamw(parts.reshape(N_DEV, -1, shp[-1]), two(w[n]), two(m[n]), two(v[n]), "adamw_" + n)
        grads[n], delta[n], new_m[n], new_v[n] = [o.reshape(shp) for o in res]
    tiny_sum = _sum_slots(r_tiny, "sum_tiny").reshape(-1)
    o = 0
    for n, s in _TINY:
        size = int(np.prod(s))
        grads[n] = tiny_sum[o:o + size].reshape(s)
        o += size

    small = dict(small)
    small["loss"] = lpart
    red = _unpack_rows(_all_reduce_small(_pack_rows([small[n] for n, _ in _SMALL], _SMALL_ROWS), "reduce_small"),
                       _SMALL)
    loss = red.pop("loss")
    grads.update(red)

    rest = [n for n in _WEIGHTS if n not in ("w_in", "w_out")]
    rest_shapes = [(n, w[n].shape) for n in rest]
    rows = -(-sum(int(np.prod(s)) for _, s in rest_shapes) // 1024) * 8
    outs = _adamw(_pack_rows([w[n] for n in rest], rows), _pack_rows([grads[n] for n in rest], rows),
                  _pack_rows([m[n] for n in rest], rows), _pack_rows([v[n] for n in rest], rows, 1.0), "adamw_rest")
    for dst, packed in zip((delta, new_m, new_v), outs):
        dst.update(_unpack_rows(packed, rest_shapes))

    return (loss, grad_x, *[grads[n] for n in _WEIGHTS], *[delta[n] for n in _WEIGHTS],
            *[new_m[n] for n in _WEIGHTS], *[new_v[n] for n in _WEIGHTS])
```

```python
import collections
import functools
import math

import numpy as np
import jax
import jax.numpy as jnp
from jax import lax
from jax.experimental import pallas as pl
from jax.experimental.pallas import tpu as pltpu

F32 = jnp.float32
BF16 = jnp.bfloat16

D_MODEL = 1024
DEPTH = 2
NH = 4
DH = 64
CH = 64
EPS = 1e-6
RWKV_GN_EPS = 64e-5
GLA_HEAD_K = 32
GLA_TAU = 16.0
GDN_TAPS = 4
SC_TAPS = 3
D_IN = 3992
N_DEV = 8
SHARD_COLS = D_IN // N_DEV

G_GDN = 0
G_RWKV = 16
G_SC = 32
G_GLA = 48
G_GDN_AB, G_RWKV_WD, G_RWKV_AD, G_GLA_AD = 64, 65, 66, 67
N_GROUPS = 68
GROUPS_PER_STEP = 4
TIME_BLOCK = 256

C_GDN, C_RWKV, C_SC, C_GLA = 0, 1032, 2184, 3208

ADAM_LR, ADAM_B1, ADAM_B2, ADAM_EPS, ADAM_WD, ADAM_STEP = 0.001, 0.9, 0.999, 1e-08, 0.01, 10

VMEM_LIMIT = 56 * 1024 * 1024
MESH = pl.DeviceIdType.MESH

_pcall = pl.pallas_call

_Comm = collections.namedtuple("_Comm", "operands out_shapes copies")


def _cparams(sem=None):
    if sem is None:
        return pltpu.CompilerParams(vmem_limit_bytes=VMEM_LIMIT)
    return pltpu.CompilerParams(dimension_semantics=sem, vmem_limit_bytes=VMEM_LIMIT)


def _group_segments():
    table = [(G_GDN + i, C_GDN + DH * i, DH) for i in range(16)]
    table.append((G_GDN_AB, C_GDN + 1024, 8))
    table += [(G_RWKV + i, C_RWKV + DH * i, DH) for i in range(16)]
    table += [(G_RWKV_WD, C_RWKV + 1024, DH), (G_RWKV_AD, C_RWKV + 1088, DH)]
    table += [(G_SC + 4 * j + k, C_SC + 256 * k + DH * j, DH) for j in range(NH) for k in range(4)]
    for h in range(NH):
        table += [(G_GLA + h, C_GLA + GLA_HEAD_K * h, GLA_HEAD_K),
                  (G_GLA + 4 + h, C_GLA + 128 + GLA_HEAD_K * h, GLA_HEAD_K),
                  (G_GLA + 8 + h, C_GLA + 256 + DH * h, DH),
                  (G_GLA + 12 + h, C_GLA + 512 + DH * h, DH)]
    table.append((G_GLA_AD, C_GLA + 768, 16))
    segs, padded = [], []
    for g, c, n in table:
        if n < DH:
            padded.append(g)
        a = 0
        while n > 0:
            d, off = divmod(c, SHARD_COLS)
            ln = min(n, SHARD_COLS - off)
            segs.append((g, a, d, off, ln))
            c, a, n = c + ln, a + ln, n - ln
    return segs, padded


_SEGMENTS, _PADDED_GROUPS = _group_segments()


def _dn(ta, tb):
    return (((1 if ta else 2,), (2 if tb else 1,)), ((0,), (0,)))


def _hdot(a, b, ta=False, tb=False):
    return lax.dot_general(a, b, _dn(ta, tb), precision=lax.Precision.HIGH, preferred_element_type=F32)


def _r(x):
    return x.astype(BF16)


def _rdot(a, b, ta=False, tb=False):
    return lax.dot_general(_r(a), _r(b), _dn(ta, tb), preferred_element_type=F32)


@jax.custom_vjp
def _bmm(a, b):
    return _rdot(a, b)


def _bmm_fwd(a, b):
    return _rdot(a, b), (a, b)


def _bmm_bwd(res, g):
    a, b = res
    return _rdot(g, b, tb=True), _rdot(a, g, ta=True)


_bmm.defvjp(_bmm_fwd, _bmm_bwd)


@jax.custom_vjp
def _bmm_nt(a, b):
    return _rdot(a, b, tb=True)


def _bmm_nt_fwd(a, b):
    return _rdot(a, b, tb=True), (a, b)


def _bmm_nt_bwd(res, g):
    a, b = res
    return _rdot(g, b), _rdot(g, a, ta=True)


_bmm_nt.defvjp(_bmm_nt_fwd, _bmm_nt_bwd)


@jax.custom_vjp
def _bmm_tn(a, b):
    return _rdot(a, b, ta=True)


def _bmm_tn_fwd(a, b):
    return _rdot(a, b, ta=True), (a, b)


def _bmm_tn_bwd(res, g):
    a, b = res
    return _rdot(b, g, tb=True), _rdot(a, g)


_bmm_tn.defvjp(_bmm_tn_fwd, _bmm_tn_bwd)


def _tri(n):
    i = lax.broadcasted_iota(jnp.int32, (n, n), 0)
    j = lax.broadcasted_iota(jnp.int32, (n, n), 1)
    return i >= j, i > j, i == j


def _heads_of(x, like):
    n = like.shape[0]
    if x.ndim == 2:
        return jnp.broadcast_to(x[None], (n,) + x.shape)
    seqs = x.shape[0]
    return jnp.broadcast_to(x[:, None], (seqs, n // seqs) + x.shape[1:]).reshape((n,) + x.shape[1:])


def _cumsum_rows(x):
    incl, _, _ = _tri(CH)
    return _hdot(_heads_of(incl.astype(F32), x), x)


def _inv_unit_lower(a):
    n = a.shape[-1]
    _, _, eye = _tri(n)
    pw = -a
    inv = eye.astype(F32) + pw
    for _ in range(int(math.log2(n)) - 1):
        pw = _hdot(pw, pw)
        inv = inv + _hdot(inv, pw)
    return inv


def _silu(x):
    return x * jax.nn.sigmoid(x)


def _t(x):
    return jnp.swapaxes(x, -1, -2)


def _gdn_chunk(prm, cst, ins, s):
    a_log, dt_b, nw = prm
    m_a, m_b = cst
    cq, ck, cv, z, ab = ins
    ab = _heads_of(ab, m_a)
    incl, strict, _ = _tri(CH)
    q = _silu(cq)
    k = _silu(ck)
    v = _silu(cv)
    q = q * lax.rsqrt(jnp.sum(q * q, -1, keepdims=True) + EPS) * (DH ** -0.5)
    k = k * lax.rsqrt(jnp.sum(k * k, -1, keepdims=True) + EPS)
    a_raw = jnp.sum(ab * m_a, -1, keepdims=True)
    b_raw = jnp.sum(ab * m_b, -1, keepdims=True)
    gstep = -jnp.exp(a_log) * jax.nn.softplus(a_raw + dt_b)
    beta = jax.nn.sigmoid(b_raw)
    gc = _cumsum_rows(gstep)
    gl = jnp.sum(gstep, -2, keepdims=True)
    dec = jnp.where(incl, jnp.exp(jnp.where(incl, gc - _t(gc), 0.0)), 0.0)
    kb = k * beta
    a_mat = jnp.where(strict, _bmm_nt(kb, k) * dec, 0.0)
    tinv = _inv_unit_lower(a_mat)
    eg = jnp.exp(gc)
    u = _hdot(tinv, v * beta)
    w = _hdot(tinv, kb * eg)
    attn = _bmm_nt(q, k) * dec
    v_new = u - _bmm(w, s)
    o = _bmm(q * eg, s) + _bmm(attn, v_new)
    s_next = s * jnp.exp(gl) + _bmm_tn(k * jnp.exp(gl - gc), v_new)
    on = o * lax.rsqrt(jnp.mean(o * o, -1, keepdims=True) + EPS) * nw
    return on * _silu(z), s_next


def _gla_chunk(prm, cst, ins, st):
    a_up, a_bias, nw = prm
    q, k, v, z, ad = ins
    incl, _, _ = _tri(CH)
    la = jax.nn.log_sigmoid(_bmm(_heads_of(ad, a_up), a_up) + a_bias) * (1.0 / GLA_TAU)
    bc = _cumsum_rows(la)
    bl = jnp.sum(la, -2, keepdims=True)
    qe = q * (GLA_HEAD_K ** -0.5) * jnp.exp(bc)
    ke = k * jnp.exp(-bc)
    attn = jnp.where(incl, _bmm_nt(qe, ke), 0.0)
    o = _bmm_nt(qe, st) + _bmm(attn, v)
    st_next = st * jnp.exp(bl) + _bmm_tn(v, k * jnp.exp(bl - bc))
    on = o * lax.rsqrt(jnp.mean(o * o, -1, keepdims=True) + EPS) * nw
    return on * _silu(z), st_next


def _rwkv_chunk(prm, cst, ins, s):
    r, v = ins[0], ins[2]
    incl, strict, _ = _tri(CH)
    lw, kk, k2, m = _rwkv_pre(prm, ins)
    cum = _cumsum_rows(lw)
    ltot = jnp.sum(lw, -2, keepdims=True)
    n_t = -kk * jnp.exp(cum - lw)
    einv = jnp.exp(-cum)
    m_t = m * einv
    k_t = k2 * einv
    r_t = r * jnp.exp(cum)
    a_nm = jnp.where(strict, _hdot(n_t, m_t, tb=True), 0.0)
    a_nk = jnp.where(strict, _hdot(n_t, k_t, tb=True), 0.0)
    cm = _hdot(_inv_unit_lower(-a_nm), _hdot(n_t, s, tb=True) + _hdot(a_nk, v))
    y = (_hdot(r_t, s, tb=True) + _hdot(jnp.where(incl, _hdot(r_t, m_t, tb=True), 0.0), cm)
         + _hdot(jnp.where(incl, _hdot(r_t, k_t, tb=True), 0.0), v))
    eend = jnp.exp(ltot - cum)
    s_next = s * jnp.exp(ltot) + _hdot(cm, m * eend, ta=True) + _hdot(v, k2 * eend, ta=True)
    return _rwkv_post(prm, ins, y, k2), s_next


def _rwkv_pre(prm, ins):
    w0, w_up, a0, a_up, k_k, k_a = prm[:6]
    k, wd, ad = ins[1], ins[4], ins[5]
    lw = -math.exp(-0.5) * jax.nn.sigmoid(w0 + _bmm(_heads_of(jnp.tanh(wd), w_up), w_up))
    a = jax.nn.sigmoid(a0 + _bmm(_heads_of(ad, a_up), a_up))
    kk = k * k_k
    kk = kk * lax.rsqrt(jnp.sum(kk * kk, -1, keepdims=True) + EPS)
    k2 = k * (1.0 + (a - 1.0) * k_a)
    return lw, kk, k2, kk * a


def _rwkv_post(prm, ins, y, k2):
    r_k, ln_w, ln_b = prm[6:]
    r, v, z = ins[0], ins[2], ins[3]
    mean = jnp.mean(y, -1, keepdims=True)
    yc = y - mean
    var = jnp.mean(yc * yc, -1, keepdims=True)
    yn = yc * lax.rsqrt(var + RWKV_GN_EPS) * ln_w + ln_b
    bonus = jnp.sum(r * k2 * r_k, -1, keepdims=True) * v
    return (yn + bonus) * _silu(z)


@jax.custom_vjp
def _bmv(s, x):
    return jnp.sum(_r(s).astype(F32) * _r(x).astype(F32), -1, keepdims=True)


def _bmv_fwd(s, x):
    return _bmv(s, x), (s, x)


def _bmv_bwd(res, g):
    s, x = res
    return g * x, jnp.sum(_r(s).astype(F32) * _r(g).astype(F32), -2, keepdims=True)


_bmv.defvjp(_bmv_fwd, _bmv_bwd)


def _rwkv_chunk_steps(prm, cst, ins, s):
    r, v = ins[0], ins[2]
    lw, kk, k2, m = _rwkv_pre(prm, ins)
    w = jnp.exp(lw)
    v_t = _t(v)
    lane = lax.broadcasted_iota(jnp.int32, (1, 1, CH), 2)
    y_t = jnp.zeros((s.shape[0], DH, CH), F32)
    for t in range(CH):
        e_t = (lane == t).astype(F32)
        row = (slice(None), slice(t, t + 1))
        sa = _bmv(s, -kk[row])
        s = s * w[row] + sa * m[row] + jnp.sum(v_t * e_t, -1, keepdims=True) * k2[row]
        y_t = y_t + _bmv(s, r[row]) * e_t
    return _rwkv_post(prm, ins, _t(y_t), k2), s


def _time_block(t):
    return TIME_BLOCK if t % TIME_BLOCK == 0 else t


def _load_chunk(ref, i):
    nb = ref.shape[1]
    if ref.shape[0] == NH:
        return jnp.concatenate([ref[:, b, pl.ds(i, CH), :] for b in range(nb)], axis=0)
    return ref[0, :, pl.ds(i, CH), :]


def _load_chunk_of(ref, c, i):
    if ref.shape[0] == NH:
        return ref[pl.ds(c % NH, 1), c // NH, pl.ds(i, CH), :]
    return ref[0, pl.ds(c // NH, 1), pl.ds(i, CH), :]


def _each_chain(n, fn):
    def step(c, carry):
        fn(c)
        return carry

    lax.fori_loop(0, n, step, 0)


def _mixer_fwd(chunk_fn, name, ins, prm, cst, nb, t, first_fn=None, side=None):
    tb = _time_block(t)
    nt, ncb, nch = t // tb, tb // CH, nb * NH
    n_in, n_prm, n_cst = len(ins), len(prm), len(cst)
    n_main, n_side = n_in + n_prm + n_cst, len(side.operands) if side else 0

    def body(*refs):
        in_refs = refs[:n_in]
        prm_refs = refs[n_in:n_in + n_prm]
        cst_refs = refs[n_in + n_prm:n_main]
        side_in = refs[n_main:n_main + n_side]
        y_ref, ck_ref = refs[n_main + n_side:n_main + n_side + 2]
        side_out = refs[n_main + n_side + 2:n_main + 2 * n_side + 2]
        s_scr = refs[n_main + 2 * n_side + 2]
        sems = refs[n_main + 2 * n_side + 3:]
        step_t = pl.program_id(0)

        if side is not None:
            @pl.when(step_t == 0)
            def _():
                _comm_start(side.copies(side_in, side_out, sems))

        @pl.when(step_t == 0)
        def _():
            s_scr[...] = jnp.zeros_like(s_scr)

        def chunk(c, i):
            s = s_scr[...]
            ck_ref[c] = s
            y, s_next = chunk_fn([jnp.tile(r[...], (nb, 1, 1)) for r in prm_refs],
                                 [jnp.tile(r[...], (nb, 1, 1)) for r in cst_refs],
                                 [_load_chunk(r, i) for r in in_refs], s)
            for b in range(nb):
                y_ref[:, b, pl.ds(i, CH), :] = y[b * NH:(b + 1) * NH].astype(BF16)
            s_scr[...] = s_next

        def first_chunk_of(c):
            one, h = pl.ds(c, 1), pl.ds(c % NH, 1)
            s = s_scr[one]
            ck_ref[0, one] = s
            y, s_next = first_fn([r[h] for r in prm_refs], [r[h] for r in cst_refs],
                                 [_load_chunk_of(r, c, 0) for r in in_refs], s)
            y_ref[h, c // NH, pl.ds(0, CH), :] = y.astype(BF16)
            s_scr[one] = s_next

        def step(c, carry):
            chunk(c, pl.multiple_of(c * CH, CH))
            return carry

        if first_fn is None:
            lax.fori_loop(0, ncb, step, 0)
        else:
            @pl.when(step_t == 0)
            def _():
                _each_chain(nch, first_chunk_of)

            @pl.when(step_t != 0)
            def _():
                chunk(0, 0)

            lax.fori_loop(1, ncb, step, 0)

        if side is not None:
            @pl.when(step_t == nt - 1)
            def _():
                _comm_wait(side.copies(side_in, side_out, sems))

    hbm = pl.BlockSpec(memory_space=pl.ANY)
    in_specs = [pl.BlockSpec((ng, nb, tb, DH), (lambda j, bi=bi: (bi, 0, j, 0))) for _, ng, bi in ins]
    in_specs += [pl.BlockSpec(p.shape, lambda j: (0, 0, 0)) for p in list(prm) + list(cst)]
    y, ck, *side_res = _pcall(
        body, name=name, grid=(nt,),
        in_specs=in_specs + [hbm] * n_side,
        out_specs=[pl.BlockSpec((NH, nb, tb, DH), lambda j: (0, 0, j, 0)),
                   pl.BlockSpec((ncb, nch, DH, DH), lambda j: (j, 0, 0, 0))] + [hbm] * n_side,
        out_shape=[jax.ShapeDtypeStruct((NH, nb, t, DH), BF16),
                   jax.ShapeDtypeStruct((t // CH, nch, DH, DH), F32)] + (side.out_shapes if side else []),
        scratch_shapes=[pltpu.VMEM((nch, DH, DH), F32)] + (_comm_scratch(side) if side else []),
        compiler_params=_cparams(("arbitrary",)),
    )(*[a.reshape(a.shape[0], nb, t, DH) for a, _, _ in ins], *prm, *cst, *(side.operands if side else []))
    return y.reshape(NH, nb * t, DH), ck, side_res


def _mixer_bwd(chunk_fn, name, ins, prm, cst, ck, dy, dy_block, outs, routes, nb, t, first_fn=None, side=None):
    tb = _time_block(t)
    nt, ncb, nch = t // tb, tb // CH, nb * NH
    n_in, n_prm, n_cst, n_out = len(ins), len(prm), len(cst), len(outs)
    n_main, n_side = n_in + n_prm + n_cst + 2, len(side.operands) if side else 0

    def body(*refs):
        in_refs = refs[:n_in]
        prm_refs = refs[n_in:n_in + n_prm]
        cst_refs = refs[n_in + n_prm:n_in + n_prm + n_cst]
        ck_ref, dy_ref = refs[n_main - 2:n_main]
        side_in = refs[n_main:n_main + n_side]
        rest = refs[n_main + n_side:]
        out_refs = rest[:n_out]
        dprm_refs = rest[n_out:n_out + n_prm]
        side_out = rest[n_out + n_prm:n_out + n_prm + n_side]
        ds_scr = rest[n_out + n_prm + n_side]
        sems = rest[n_out + n_prm + n_side + 1:]
        step_t = pl.program_id(0)

        if side is not None:
            @pl.when(step_t == 0)
            def _():
                _comm_start(side.copies(side_in, side_out, sems))

        @pl.when(step_t == 0)
        def _():
            ds_scr[...] = jnp.zeros_like(ds_scr)
            for r in dprm_refs:
                r[...] = jnp.zeros_like(r)

        def chunk(c, i):
            cst_v = [jnp.tile(r[...], (nb, 1, 1)) for r in cst_refs]
            _, vjp = jax.vjp(lambda p, x, s: chunk_fn(p, cst_v, x, s),
                             [jnp.tile(r[...], (nb, 1, 1)) for r in prm_refs],
                             [_load_chunk(r, i) for r in in_refs], ck_ref[c])
            dy_c = jnp.concatenate([dy_ref[:, b, pl.ds(i, CH), :] for b in range(nb)], axis=0)
            d_prm, d_ins, d_s = vjp((dy_c, ds_scr[...]))
            for (oi, g0), r, g in zip(routes, in_refs, d_ins):
                o_ref = out_refs[oi]
                if r.shape[0] == NH:
                    for b in range(nb):
                        o_ref[g0:g0 + NH, b, pl.ds(i, CH), :] = g[b * NH:(b + 1) * NH].astype(o_ref.dtype)
                else:
                    o_ref[g0, :, pl.ds(i, CH), :] = g.astype(o_ref.dtype)
            for r, g in zip(dprm_refs, d_prm):
                r[...] += g
            ds_scr[...] = d_s

        def first_chunk_of(c):
            one, h, b = pl.ds(c, 1), pl.ds(c % NH, 1), c // NH
            cst_v = [r[h] for r in cst_refs]
            _, vjp = jax.vjp(lambda p, x, s: first_fn(p, cst_v, x, s), [r[h] for r in prm_refs],
                             [_load_chunk_of(r, c, 0) for r in in_refs], ck_ref[0, one])
            d_prm, d_ins, d_s = vjp((dy_ref[h, b, pl.ds(0, CH), :], ds_scr[one]))
            for (oi, g0), r, g in zip(routes, in_refs, d_ins):
                o_ref = out_refs[oi]
                if r.shape[0] == NH:
                    o_ref.at[g0:g0 + NH][h, b, pl.ds(0, CH), :] = g.astype(o_ref.dtype)
                else:
                    o_ref[g0, pl.ds(b, 1), pl.ds(0, CH), :] += g.astype(o_ref.dtype)
            for r, g in zip(dprm_refs, d_prm):
                r[one] += g
            ds_scr[one] = d_s

        def first_chunk():
            for (oi, g0), r in zip(routes, in_refs):
                if r.shape[0] != NH:
                    out_refs[oi][g0, :, pl.ds(0, CH), :] = jnp.zeros((nb, CH, DH), out_refs[oi].dtype)
            _each_chain(nch, first_chunk_of)

        def step(j, carry):
            c = ncb - 1 - j
            chunk(c, pl.multiple_of(c * CH, CH))
            return carry

        lax.fori_loop(0, ncb - 1, step, 0)
        if first_fn is None:
            chunk(0, 0)
        else:
            @pl.when(step_t == nt - 1)
            def _():
                first_chunk()

            @pl.when(step_t != nt - 1)
            def _():
                chunk(0, 0)

        if side is not None:
            @pl.when(step_t == nt - 1)
            def _():
                _comm_wait(side.copies(side_in, side_out, sems))

    def back(j):
        return nt - 1 - j

    hbm = pl.BlockSpec(memory_space=pl.ANY)
    in_specs = [pl.BlockSpec((ng, nb, tb, DH), (lambda j, bi=bi: (bi, 0, back(j), 0))) for _, ng, bi in ins]
    in_specs += [pl.BlockSpec(p.shape, lambda j: (0, 0, 0)) for p in list(prm) + list(cst)]
    in_specs += [pl.BlockSpec((ncb, nch, DH, DH), lambda j: (back(j), 0, 0, 0)),
                 pl.BlockSpec((NH, nb, tb, DH), lambda j: (dy_block, 0, back(j), 0))]
    out_specs = [pl.BlockSpec((ng, nb, tb, DH), lambda j: (0, 0, back(j), 0)) for ng, _ in outs]
    out_specs += [pl.BlockSpec((nch,) + p.shape[1:], lambda j: (0, 0, 0)) for p in prm]
    out_shape = [jax.ShapeDtypeStruct((ng, nb, t, DH), dt) for ng, dt in outs]
    out_shape += [jax.ShapeDtypeStruct((nch,) + p.shape[1:], F32) for p in prm]
    res = _pcall(
        body, name=name, grid=(nt,),
        in_specs=in_specs + [hbm] * n_side, out_specs=out_specs + [hbm] * n_side,
        out_shape=out_shape + (side.out_shapes if side else []),
        scratch_shapes=[pltpu.VMEM((nch, DH, DH), F32)] + (_comm_scratch(side) if side else []),
        compiler_params=_cparams(("arbitrary",)),
    )(*[a.reshape(a.shape[0], nb, t, DH) for a, _, _ in ins], *prm, *cst, ck, dy.reshape(dy.shape[0], nb, t, DH),
      *(side.operands if side else []))
    d_outs = [o.reshape(o.shape[0], nb * t, DH) for o in res[:n_out]]
    d_prm = [g.reshape((nb,) + p.shape) for g, p in zip(res[n_out:n_out + n_prm], prm)]
    return d_outs, d_prm, res[n_out + n_prm:]


def _shift_down(x, s):
    if s == 0:
        return x
    row = lax.broadcasted_iota(jnp.int32, x.shape, 0)
    return jnp.where(row < s, 0.0, pltpu.roll(x, s, 0))


def _shift_up(x, s):
    if s == 0:
        return x
    t = x.shape[0]
    row = lax.broadcasted_iota(jnp.int32, x.shape, 0)
    return jnp.where(row >= t - s, 0.0, pltpu.roll(x, t - s, 0))


def _conv_fwd(p, g0, ng, w, nb, t, name):
    taps = w.shape[1]

    def body(x_ref, w_ref, y_ref):
        x = x_ref[...]
        acc = w_ref[taps - 1:taps, :] * x
        for i in range(taps - 1):
            acc = acc + w_ref[i:i + 1, :] * _shift_down(x, taps - 1 - i)
        y_ref[...] = acc

    return _pcall(
        body, name=name, grid=(ng, nb),
        in_specs=[pl.BlockSpec((None, t, DH), lambda g, b: (g0 + g, b, 0)),
                  pl.BlockSpec((None, taps, DH), lambda g, b: (g, 0, 0))],
        out_specs=pl.BlockSpec((None, t, DH), lambda g, b: (g, b, 0)),
        out_shape=jax.ShapeDtypeStruct((ng, nb * t, DH), F32),
        compiler_params=_cparams(("parallel", "parallel")),
    )(p, w)


def _conv_bwd(p, g0, ng, w, dy, nb, t, name):
    taps = w.shape[1]

    def body(x_ref, w_ref, dy_ref, dx_ref, dw_ref):
        x = x_ref[...]
        d = dy_ref[...]
        acc = w_ref[taps - 1:taps, :] * d
        dw_ref[taps - 1:taps, :] = jnp.sum(d * x, 0, keepdims=True)
        for i in range(taps - 1):
            s = taps - 1 - i
            acc = acc + w_ref[i:i + 1, :] * _shift_up(d, s)
            dw_ref[i:i + 1, :] = jnp.sum(d * _shift_down(x, s), 0, keepdims=True)
        dx_ref[...] = acc.astype(BF16)

    return _pcall(
        body, name=name, grid=(ng, nb),
        in_specs=[pl.BlockSpec((None, t, DH), lambda g, b: (g0 + g, b, 0)),
                  pl.BlockSpec((None, taps, DH), lambda g, b: (g, 0, 0)),
                  pl.BlockSpec((None, t, DH), lambda g, b: (g, b, 0))],
        out_specs=[pl.BlockSpec((None, t, DH), lambda g, b: (g, b, 0)),
                   pl.BlockSpec((None, None, taps, DH), lambda g, b: (g, b, 0, 0))],
        out_shape=[jax.ShapeDtypeStruct((ng, nb * t, DH), BF16),
                   jax.ShapeDtypeStruct((ng, nb, taps, DH), F32)],
        compiler_params=_cparams(("parallel", "parallel")),
    )(p, w, dy)


def _mix_group(g):
    return jnp.where(g < 16, G_RWKV + g, G_RWKV_WD + g - 16)


def _mix_fwd(p, mu, nb, t, name):
    def body(x_ref, mu_ref, y_ref):
        x = x_ref[...]
        y_ref[...] = x + mu_ref[...] * (_shift_down(x, 1) - x)

    return _pcall(
        body, name=name, grid=(18, nb),
        in_specs=[pl.BlockSpec((None, t, DH), lambda g, b: (_mix_group(g), b, 0)),
                  pl.BlockSpec((None, 1, DH), lambda g, b: (g, 0, 0))],
        out_specs=pl.BlockSpec((None, t, DH), lambda g, b: (g, b, 0)),
        out_shape=jax.ShapeDtypeStruct((18, nb * t, DH), F32),
        compiler_params=_cparams(("parallel", "parallel")),
    )(p, mu)


def _mix_bwd(p, mu, dy, nb, t, name):
    def body(x_ref, mu_ref, dy_ref, dx_ref, dmu_ref):
        x = x_ref[...]
        muv = mu_ref[...]
        d = dy_ref[...]
        dx_ref[...] = (d * (1.0 - muv) + _shift_up(d * muv, 1)).astype(BF16)
        dmu_ref[...] = jnp.sum(d * (_shift_down(x, 1) - x), 0, keepdims=True)

    return _pcall(
        body, name=name, grid=(18, nb),
        in_specs=[pl.BlockSpec((None, t, DH), lambda g, b: (_mix_group(g), b, 0)),
                  pl.BlockSpec((None, 1, DH), lambda g, b: (g, 0, 0)),
                  pl.BlockSpec((None, t, DH), lambda g, b: (g, b, 0))],
        out_specs=[pl.BlockSpec((None, t, DH), lambda g, b: (g, b, 0)),
                   pl.BlockSpec((None, None, 1, DH), lambda g, b: (g, b, 0, 0))],
        out_shape=[jax.ShapeDtypeStruct((18, nb * t, DH), BF16),
                   jax.ShapeDtypeStruct((18, nb, 1, DH), F32)],
        compiler_params=_cparams(("parallel", "parallel")),
    )(p, mu, dy)


def _sc_fwd(p, w, nb, t, name):
    def body(p_ref, w_ref, y_ref):
        u = p_ref[1] * p_ref[2]
        conv = w_ref[2:3, :] * u + w_ref[1:2, :] * _shift_down(u, 1) + w_ref[0:1, :] * _shift_down(u, 2)
        y_ref[...] = (p_ref[0] * conv * _silu(p_ref[3])).astype(BF16)

    return _pcall(
        body, name=name, grid=(NH, nb),
        in_specs=[pl.BlockSpec((4, t, DH), lambda j, b: (G_SC // 4 + j, b, 0)),
                  pl.BlockSpec((None, SC_TAPS, DH), lambda j, b: (j, 0, 0))],
        out_specs=pl.BlockSpec((None, t, DH), lambda j, b: (j, b, 0)),
        out_shape=jax.ShapeDtypeStruct((NH, nb * t, DH), BF16),
        compiler_params=_cparams(("parallel", "parallel")),
    )(p, w)


def _sc_bwd(p, w, dy, nb, t, name):
    def body(p_ref, w_ref, dy_ref, dp_ref, dw_ref):
        bg, cg, xg, z = p_ref[0], p_ref[1], p_ref[2], p_ref[3]
        d = dy_ref[...]
        u = cg * xg
        u1 = _shift_down(u, 1)
        u2 = _shift_down(u, 2)
        conv = w_ref[2:3, :] * u + w_ref[1:2, :] * u1 + w_ref[0:1, :] * u2
        sg = jax.nn.sigmoid(z)
        sz = z * sg
        dp_ref[0] = (d * conv * sz).astype(BF16)
        dp_ref[3] = (d * bg * conv * (sg * (1.0 + z * (1.0 - sg)))).astype(BF16)
        dconv = d * bg * sz
        du = w_ref[2:3, :] * dconv + w_ref[1:2, :] * _shift_up(dconv, 1) + w_ref[0:1, :] * _shift_up(dconv, 2)
        dp_ref[1] = (du * xg).astype(BF16)
        dp_ref[2] = (du * cg).astype(BF16)
        dw_ref[2:3, :] = jnp.sum(dconv * u, 0, keepdims=True)
        dw_ref[1:2, :] = jnp.sum(dconv * u1, 0, keepdims=True)
        dw_ref[0:1, :] = jnp.sum(dconv * u2, 0, keepdims=True)

    return _pcall(
        body, name=name, grid=(NH, nb),
        in_specs=[pl.BlockSpec((4, t, DH), lambda j, b: (G_SC // 4 + j, b, 0)),
                  pl.BlockSpec((None, SC_TAPS, DH), lambda j, b: (j, 0, 0)),
                  pl.BlockSpec((None, t, DH), lambda j, b: (8 + j, b, 0))],
        out_specs=[pl.BlockSpec((4, t, DH), lambda j, b: (j, b, 0)),
                   pl.BlockSpec((None, None, SC_TAPS, DH), lambda j, b: (j, b, 0, 0))],
        out_shape=[jax.ShapeDtypeStruct((4 * NH, nb * t, DH), BF16),
                   jax.ShapeDtypeStruct((NH, nb, SC_TAPS, DH), F32)],
        compiler_params=_cparams(("parallel", "parallel")),
    )(p, w, dy)


def _row_tile(n):
    return 1024 if n % 1024 == 0 else n


def _regroup_in(w_all, name):
    tr = 256
    gs = GROUPS_PER_STEP

    def body(w_ref, o_ref):
        for g in _PADDED_GROUPS:
            o_ref[g // gs, :, DH * (g % gs):DH * (g % gs + 1)] = jnp.zeros((tr, DH), BF16)
        for g, a, d, off, ln in _SEGMENTS:
            lane = DH * (g % gs) + a
            o_ref[g // gs, :, lane:lane + ln] = w_ref[d, :, off:off + ln].astype(BF16)

    return _pcall(
        body, name=name, grid=(D_MODEL // tr,),
        in_specs=[pl.BlockSpec((N_DEV, tr, SHARD_COLS), lambda i: (0, i, 0))],
        out_specs=pl.BlockSpec((N_GROUPS // gs, tr, gs * DH), lambda i: (0, i, 0)),
        out_shape=jax.ShapeDtypeStruct((N_GROUPS // gs, D_MODEL, gs * DH), BF16),
        compiler_params=_cparams(("parallel",)),
    )(w_all)


def _regroup_out(dwg, name):
    tr = 256
    gs = GROUPS_PER_STEP

    def body(g_ref, o_ref):
        for g, a, d, off, ln in _SEGMENTS:
            lane = DH * (g % gs) + a
            o_ref[d, :, off:off + ln] = g_ref[g // gs, :, lane:lane + ln]

    return _pcall(
        body, name=name, grid=(D_MODEL // tr,),
        in_specs=[pl.BlockSpec((N_GROUPS // gs, tr, gs * DH), lambda i: (0, i, 0))],
        out_specs=pl.BlockSpec((N_DEV, tr, SHARD_COLS), lambda i: (0, i, 0)),
        out_shape=jax.ShapeDtypeStruct((N_DEV, D_MODEL, SHARD_COLS), F32),
        compiler_params=_cparams(("parallel",)),
    )(dwg)


def _norm_proj(x, pre_w, w_g, name):
    n = x.shape[0]
    tm = _row_tile(n)
    gs = GROUPS_PER_STEP

    def body(x_ref, pw_ref, w_ref, h_ref, p_ref):
        @pl.when(pl.program_id(1) == 0)
        def _():
            xv = x_ref[...]
            h = xv * lax.rsqrt(jnp.mean(xv * xv, -1, keepdims=True) + EPS) * pw_ref[...]
            h_ref[...] = h.astype(BF16)

        r = jnp.dot(h_ref[...], w_ref[...], preferred_element_type=F32)
        for k in range(gs):
            p_ref[k] = r[:, DH * k:DH * (k + 1)]

    return _pcall(
        body, name=name, grid=(n // tm, N_GROUPS // gs),
        in_specs=[pl.BlockSpec((tm, D_MODEL), lambda i, j: (i, 0)),
                  pl.BlockSpec((1, D_MODEL), lambda i, j: (0, 0)),
                  pl.BlockSpec((None, D_MODEL, gs * DH), lambda i, j: (j, 0, 0))],
        out_specs=[pl.BlockSpec((tm, D_MODEL), lambda i, j: (i, 0)),
                   pl.BlockSpec((gs, tm, DH), lambda i, j: (j, i, 0))],
        out_shape=[jax.ShapeDtypeStruct((n, D_MODEL), BF16),
                   jax.ShapeDtypeStruct((N_GROUPS, n, DH), F32)],
        compiler_params=_cparams(("parallel", "arbitrary")),
    )(x, pre_w, w_g)


def _out_proj_norm(ys, wout_g, x, post_w, name):
    n = x.shape[0]
    tm = _row_tile(n)

    def body(y0, y1, y2, y3, w_ref, x_ref, pw_ref, out_ref, xn_ref):
        acc = jnp.zeros((tm, D_MODEL), F32)
        for m, yr in enumerate((y0, y1, y2, y3)):
            for h in range(NH):
                acc = acc + jnp.dot(yr[h], w_ref[m * NH + h], preferred_element_type=F32)
        out_ref[...] = acc
        xn_ref[...] = x_ref[...] + acc * lax.rsqrt(jnp.mean(acc * acc, -1, keepdims=True) + EPS) * pw_ref[...]

    yspec = pl.BlockSpec((NH, tm, DH), lambda i: (0, i, 0))
    rows = pl.BlockSpec((tm, D_MODEL), lambda i: (i, 0))
    return _pcall(
        body, name=name, grid=(n // tm,),
        in_specs=[yspec] * 4 + [pl.BlockSpec((4 * NH, DH, D_MODEL), lambda i: (0, 0, 0)), rows,
                                pl.BlockSpec((1, D_MODEL), lambda i: (0, 0))],
        out_specs=[rows, rows],
        out_shape=[jax.ShapeDtypeStruct((n, D_MODEL), F32)] * 2,
        compiler_params=_cparams(("parallel",)),
    )(*ys, wout_g, x, post_w)


def _loss_grad(x, tgt, name):
    n = x.shape[0]
    tm = _row_tile(n)

    def body(x_ref, t_ref, dx_ref, l_ref):
        @pl.when(pl.program_id(0) == 0)
        def _():
            l_ref[...] = jnp.zeros_like(l_ref)

        e = x_ref[...] - t_ref[...]
        dx_ref[...] = e * (1.0 / D_MODEL)
        l_ref[...] += jnp.sum(jnp.sum(e * e, -1, keepdims=True), 0, keepdims=True) * (0.5 / D_MODEL)

    rows = pl.BlockSpec((tm, D_MODEL), lambda i: (i, 0))
    return _pcall(
        body, name=name, grid=(n // tm,),
        in_specs=[rows, rows],
        out_specs=[rows, pl.BlockSpec((1, 128), lambda i: (0, 0))],
        out_shape=[jax.ShapeDtypeStruct((n, D_MODEL), F32), jax.ShapeDtypeStruct((1, 128), F32)],
        compiler_params=_cparams(("arbitrary",)),
    )(x, tgt)


def _rmsnorm_bwd(xv, w, d):
    r = lax.rsqrt(jnp.mean(xv * xv, -1, keepdims=True) + EPS)
    xh = xv * r
    dxh = d * w
    dx = r * (dxh - xh * jnp.mean(dxh * xh, -1, keepdims=True))
    return dx, d * xh


def _post_bwd(dxn, out, post_w, wout_g, name):
    n = dxn.shape[0]
    tm = _row_tile(n)

    def body(d_ref, o_ref, pw_ref, w_ref, do_ref, dy_ref, dpw_ref):
        @pl.when(pl.program_id(0) == 0)
        def _():
            dpw_ref[...] = jnp.zeros_like(dpw_ref)

        dout, dw_rows = _rmsnorm_bwd(o_ref[...], pw_ref[...], d_ref[...])
        dpw_ref[...] += jnp.sum(dw_rows, 0, keepdims=True)
        db = dout.astype(BF16)
        do_ref[...] = db
        for g in range(4 * NH):
            dy_ref[g] = lax.dot_general(db, w_ref[g], (((1,), (1,)), ((), ())), preferred_element_type=F32)

    rows = pl.BlockSpec((tm, D_MODEL), lambda i: (i, 0))
    vec = pl.BlockSpec((1, D_MODEL), lambda i: (0, 0))
    return _pcall(
        body, name=name, grid=(n // tm,),
        in_specs=[rows, rows, vec, pl.BlockSpec((4 * NH, DH, D_MODEL), lambda i: (0, 0, 0))],
        out_specs=[rows, pl.BlockSpec((4 * NH, tm, DH), lambda i: (0, i, 0)), vec],
        out_shape=[jax.ShapeDtypeStruct((n, D_MODEL), BF16),
                   jax.ShapeDtypeStruct((4 * NH, n, DH), F32),
                   jax.ShapeDtypeStruct((1, D_MODEL), F32)],
        compiler_params=_cparams(("arbitrary",)),
    )(dxn, out, post_w, wout_g)


def _dwout(ys, dout, name):
    n = dout.shape[0]
    tm = _row_tile(n)

    def body(y0, y1, y2, y3, d_ref, dw_ref):
        @pl.when(pl.program_id(0) == 0)
        def _():
            dw_ref[...] = jnp.zeros_like(dw_ref)

        d = d_ref[...]
        for m, yr in enumerate((y0, y1, y2, y3)):
            for h in range(NH):
                dw_ref[m * NH + h] += lax.dot_general(yr[h], d, (((0,), (0,)), ((), ())),
                                                      preferred_element_type=F32)

    yspec = pl.BlockSpec((NH, tm, DH), lambda i: (0, i, 0))
    return _pcall(
        body, name=name, grid=(n // tm,),
        in_specs=[yspec] * 4 + [pl.BlockSpec((tm, D_MODEL), lambda i: (i, 0))],
        out_specs=pl.BlockSpec((4 * NH, DH, D_MODEL), lambda i: (0, 0, 0)),
        out_shape=jax.ShapeDtypeStruct((4 * NH, DH, D_MODEL), F32),
        compiler_params=_cparams(("arbitrary",)),
    )(*ys, dout)


def _source_specs(sources, rows_first):
    gs = GROUPS_PER_STEP
    spans, specs, j0 = [], [], 0
    for a in sources:
        nblk = a.shape[0] // gs
        spans.append((j0, j0 + nblk))
        shape = (gs, _row_tile(a.shape[1]), DH)

        def blk(j, j0=j0, nblk=nblk):
            return jnp.clip(j - j0, 0, nblk - 1)

        if rows_first:
            specs.append(pl.BlockSpec(shape, (lambda i, j, blk=blk: (blk(j), i, 0))))
        else:
            specs.append(pl.BlockSpec(shape, (lambda j, i, blk=blk: (blk(j), i, 0))))
        j0 += nblk
    return spans, specs


def _dh_prenorm_bwd(sources, w_g, x, pre_w, dxn, name):
    n = x.shape[0]
    tm = _row_tile(n)
    gs = GROUPS_PER_STEP
    nj = N_GROUPS // gs
    spans, src_specs = _source_specs(sources, True)
    ns = len(sources)

    def body(*refs):
        src = refs[:ns]
        w_ref, x_ref, pw_ref, d_ref, dx_ref, dpw_ref, acc = refs[ns:]
        i, j = pl.program_id(0), pl.program_id(1)

        @pl.when((i == 0) & (j == 0))
        def _():
            dpw_ref[...] = jnp.zeros_like(dpw_ref)

        @pl.when(j == 0)
        def _():
            acc[...] = jnp.zeros_like(acc)

        for s_ref, (lo, hi) in zip(src, spans):
            @pl.when((j >= lo) & (j < hi))
            def _(s_ref=s_ref):
                four = jnp.concatenate([s_ref[k] for k in range(gs)], axis=-1)
                acc[...] += lax.dot_general(four, w_ref[...], (((1,), (1,)), ((), ())), preferred_element_type=F32)

        @pl.when(j == nj - 1)
        def _():
            dx, dw_rows = _rmsnorm_bwd(x_ref[...], pw_ref[...], acc[...])
            dx_ref[...] = d_ref[...] + dx
            dpw_ref[...] += jnp.sum(dw_rows, 0, keepdims=True)

    rows = pl.BlockSpec((tm, D_MODEL), lambda i, j: (i, 0))
    vec = pl.BlockSpec((1, D_MODEL), lambda i, j: (0, 0))
    return _pcall(
        body, name=name, grid=(n // tm, nj),
        in_specs=src_specs + [pl.BlockSpec((None, D_MODEL, gs * DH), lambda i, j: (j, 0, 0)), rows, vec, rows],
        out_specs=[rows, vec],
        out_shape=[jax.ShapeDtypeStruct((n, D_MODEL), F32), jax.ShapeDtypeStruct((1, D_MODEL), F32)],
        scratch_shapes=[pltpu.VMEM((tm, D_MODEL), F32)],
        compiler_params=_cparams(("arbitrary", "arbitrary")),
    )(*sources, w_g, x, pre_w, dxn)


def _dwin(hb, sources, name):
    n = hb.shape[0]
    tm = _row_tile(n)
    gs = GROUPS_PER_STEP
    spans, src_specs = _source_specs(sources, False)
    ns = len(sources)

    def body(*refs):
        h_ref = refs[0]
        src = refs[1:1 + ns]
        dw_ref = refs[1 + ns]
        j = pl.program_id(0)

        @pl.when(pl.program_id(1) == 0)
        def _():
            dw_ref[...] = jnp.zeros_like(dw_ref)

        h = h_ref[...]
        for s_ref, (lo, hi) in zip(src, spans):
            @pl.when((j >= lo) & (j < hi))
            def _(s_ref=s_ref):
                four = jnp.concatenate([s_ref[k] for k in range(gs)], axis=-1)
                dw_ref[...] += jnp.dot(h, four, preferred_element_type=F32)

    return _pcall(
        body, name=name, grid=(N_GROUPS // gs, n // tm),
        in_specs=[pl.BlockSpec((D_MODEL, tm), lambda j, i: (0, i))] + src_specs,
        out_specs=pl.BlockSpec((None, D_MODEL, gs * DH), lambda j, i: (j, 0, 0)),
        out_shape=jax.ShapeDtypeStruct((N_GROUPS // gs, D_MODEL, gs * DH), F32),
        compiler_params=_cparams(("parallel", "arbitrary")),
    )(jnp.transpose(hb), *sources)


def _adamw_math(w, g, m, v):
    c1 = 1.0 - ADAM_B1 ** ADAM_STEP
    c2 = 1.0 - ADAM_B2 ** ADAM_STEP
    nm = ADAM_B1 * m + (1.0 - ADAM_B1) * g
    nv = ADAM_B2 * v + (1.0 - ADAM_B2) * (g * g)
    return -ADAM_LR * ((nm / c1) / (jnp.sqrt(nv / c2) + ADAM_EPS) + ADAM_WD * w), nm, nv


def _adamw(w, g, m, v, name):
    r, c = w.shape
    tr = 256 if r % 256 == 0 else r

    def body(w_ref, g_ref, m_ref, v_ref, d_ref, nm_ref, nv_ref):
        d_ref[...], nm_ref[...], nv_ref[...] = _adamw_math(w_ref[...], g_ref[...], m_ref[...], v_ref[...])

    spec = pl.BlockSpec((tr, c), lambda i: (i, 0))
    return _pcall(
        body, name=name, grid=(r // tr,),
        in_specs=[spec] * 4, out_specs=[spec] * 3,
        out_shape=[jax.ShapeDtypeStruct((r, c), F32)] * 3,
        compiler_params=_cparams(("parallel",)),
    )(w, g, m, v)


def _sum_adamw(parts, w, m, v, name):
    r, c = w.shape
    tr = 128 if r % 128 == 0 else r

    def body(p_ref, w_ref, m_ref, v_ref, g_ref, d_ref, nm_ref, nv_ref):
        g = p_ref[0]
        for k in range(1, N_DEV):
            g = g + p_ref[k]
        g_ref[...] = g
        d_ref[...], nm_ref[...], nv_ref[...] = _adamw_math(w_ref[...], g, m_ref[...], v_ref[...])

    spec = pl.BlockSpec((tr, c), lambda i: (i, 0))
    return _pcall(
        body, name=name, grid=(r // tr,),
        in_specs=[pl.BlockSpec((N_DEV, tr, c), lambda i: (0, i, 0))] + [spec] * 3, out_specs=[spec] * 4,
        out_shape=[jax.ShapeDtypeStruct((r, c), F32)] * 4,
        compiler_params=_cparams(("parallel",)),
    )(parts, w, m, v)


def _me():
    return lax.axis_index("x"), lax.axis_index("y"), lax.axis_index("c")


def _flat(x, y, c):
    return 4 * x + 2 * y + c


def _peer(k):
    x, y, c = _me()
    return (x ^ ((k >> 2) & 1), y ^ ((k >> 1) & 1), c ^ (k & 1))


def _gather_plan(blocks):
    def copies(x_refs, out_refs, sems):
        send_sems, recv_sems, local_sems = sems
        me = _flat(*_me())
        local = [pltpu.make_async_copy(x, o.at[me], local_sems.at[a]) for a, (x, o) in enumerate(zip(x_refs, out_refs))]
        outgoing, incoming = [], []
        for k in range(1, N_DEV):
            src = _flat(*_peer(k))
            for a, (x, o) in enumerate(zip(x_refs, out_refs)):
                for slot, group in ((me, outgoing), (src, incoming)):
                    group.append(pltpu.make_async_remote_copy(
                        src_ref=x, dst_ref=o.at[slot], send_sem=send_sems.at[a, k - 1], recv_sem=recv_sems.at[a, k - 1],
                        device_id=_peer(k), device_id_type=MESH))
        return local, outgoing, incoming

    return _Comm(list(blocks), [jax.ShapeDtypeStruct((N_DEV,) + b.shape, b.dtype) for b in blocks], copies)


def _exchange_plan(sends):
    def copies(s_refs, out_refs, sems):
        send_sems, recv_sems, local_sems = sems
        me = _flat(*_me())
        local = [pltpu.make_async_copy(s.at[me], o.at[0], local_sems.at[i]) for i, (s, o) in enumerate(zip(s_refs, out_refs))]
        outgoing = []
        for k in range(1, N_DEV):
            to = _flat(*_peer(k))
            for i, (s, o) in enumerate(zip(s_refs, out_refs)):
                outgoing.append(pltpu.make_async_remote_copy(
                    src_ref=s.at[to], dst_ref=o.at[k], send_sem=send_sems.at[i, k - 1], recv_sem=recv_sems.at[i, k - 1],
                    device_id=_peer(k), device_id_type=MESH))
        return local, outgoing, outgoing

    return _Comm(list(sends), [jax.ShapeDtypeStruct(s.shape, s.dtype) for s in sends], copies)


def _comm_scratch(plan):
    n = len(plan.operands)
    return [pltpu.SemaphoreType.DMA((n, N_DEV - 1)), pltpu.SemaphoreType.DMA((n, N_DEV - 1)),
            pltpu.SemaphoreType.DMA((n,))]


def _comm_start(copies):
    local, outgoing, _ = copies
    for cp in local + outgoing:
        cp.start()


def _comm_wait(copies):
    local, outgoing, incoming = copies
    for cp in incoming:
        cp.wait_recv()
    for cp in outgoing:
        cp.wait_send()
    for cp in local:
        cp.wait()


def _run_comm(plan, name):
    n = len(plan.operands)

    def body(*refs):
        copies = plan.copies(refs[:n], refs[n:2 * n], refs[2 * n:])
        _comm_start(copies)
        _comm_wait(copies)

    return _pcall(
        body, name=name,
        in_specs=[pl.BlockSpec(memory_space=pl.ANY)] * n,
        out_specs=[pl.BlockSpec(memory_space=pl.ANY)] * n,
        out_shape=plan.out_shapes,
        scratch_shapes=_comm_scratch(plan),
    )(*plan.operands)


def _sum_slots(a, name):
    r = a.shape[1]

    def body(a_ref, o_ref):
        acc = a_ref[0]
        for d in range(1, N_DEV):
            acc = acc + a_ref[d]
        o_ref[...] = acc

    return _pcall(body, name=name, out_shape=jax.ShapeDtypeStruct((r, 128), F32), compiler_params=_cparams())(a)


def _all_reduce_small(blk, name):
    r = blk.shape[0]

    def body(x_ref, out_ref, gath, send_sems, recv_sems):
        me = _flat(*_me())
        gath[me] = x_ref[...]
        copies = []
        for k in range(1, N_DEV):
            cp = pltpu.make_async_remote_copy(
                src_ref=x_ref, dst_ref=gath.at[me],
                send_sem=send_sems.at[k - 1], recv_sem=recv_sems.at[k - 1],
                device_id=_peer(k), device_id_type=MESH)
            cp.start()
            copies.append(cp)
        for k in range(1, N_DEV):
            src = _flat(*_peer(k))
            pltpu.make_async_remote_copy(
                src_ref=x_ref, dst_ref=gath.at[src],
                send_sem=send_sems.at[k - 1], recv_sem=recv_sems.at[k - 1],
                device_id=_peer(k), device_id_type=MESH).wait_recv()
        for cp in copies:
            cp.wait_send()
        acc = gath[0]
        for d in range(1, N_DEV):
            acc = acc + gath[d]
        out_ref[...] = acc

    return _pcall(
        body, name=name,
        in_specs=[pl.BlockSpec(memory_space=pltpu.VMEM)],
        out_specs=pl.BlockSpec(memory_space=pltpu.VMEM),
        out_shape=jax.ShapeDtypeStruct((r, 128), F32),
        scratch_shapes=[pltpu.VMEM((N_DEV, r, 128), F32),
                        pltpu.SemaphoreType.DMA((N_DEV - 1,)), pltpu.SemaphoreType.DMA((N_DEV - 1,))],
    )(blk)


def _heads(vec):
    return vec.reshape(NH, 1, DH)


def _rep(vec4):
    return jnp.broadcast_to(vec4.reshape(NH, 1, 1), (NH, 1, DH))


def _onehot_lane(offset):
    m = np.zeros((NH, 1, DH), np.float32)
    for h in range(NH):
        m[h, 0, offset + h] = 1.0
    return jnp.asarray(m)


_TINY = (("gdn_conv_w", (DEPTH, 4, 96)), ("rwkv_w_up", (DEPTH, 64, 32)), ("rwkv_a_up", (DEPTH, 64, 32)),
         ("sc_conv_w", (DEPTH, 3, 32)))
_TINY_ROWS = -(-sum(int(np.prod(s)) for _, s in _TINY) // 1024) * 8


def _pack_rows(arrays, rows, fill=0.0):
    flat = jnp.concatenate([a.reshape(-1) for a in arrays])
    return jnp.pad(flat, (0, rows * 128 - flat.shape[0]), constant_values=fill).reshape(rows, 128)


def _unpack_rows(p, named_shapes):
    lead = p.shape[:-2]
    flat = p.reshape(lead + (-1,))
    out, o = {}, 0
    for n, s in named_shapes:
        size = int(np.prod(s))
        out[n] = flat[..., o:o + size].reshape(lead + tuple(s))
        o += size
    return out


def _gather_last(a):
    return jnp.transpose(a, (1, 0, 2)).reshape(a.shape[1], -1)


def _split_last(a):
    r, c8 = a.shape
    return jnp.transpose(a.reshape(r, N_DEV, c8 // N_DEV), (1, 0, 2))


_SMALL = (("pre_norm_w", (DEPTH, 1024)), ("gdn_a_log", (DEPTH, 4)), ("gdn_dt_bias", (DEPTH, 4)),
          ("gdn_norm_w", (DEPTH, 64)), ("rwkv_mu", (DEPTH, 1152)), ("rwkv_w0", (DEPTH, 256)),
          ("rwkv_a0", (DEPTH, 256)), ("rwkv_k_k", (DEPTH, 256)), ("rwkv_k_a", (DEPTH, 256)),
          ("rwkv_r_k", (DEPTH, 256)), ("rwkv_ln_w", (DEPTH, 256)), ("rwkv_ln_b", (DEPTH, 256)),
          ("gla_a_up", (DEPTH, 16, 128)), ("gla_a_bias", (DEPTH, 128)), ("gla_norm_w", (DEPTH, 64)),
          ("post_norm_w", (DEPTH, 1024)), ("loss", ()))
_SMALL_ROWS = -(-sum(int(np.prod(s)) for _, s in _SMALL) // 1024) * 8


def _big_weights(w_in_all, w_out_all, l):
    return dict(w_g=_regroup_in(w_in_all, f"regroup_in{l}"),
                wout_g=w_out_all.reshape(4 * NH, DH, D_MODEL).astype(BF16))


def _layer_params(wts, tiny, l):
    conv = _gather_last(tiny["gdn_conv_w"][:, l])
    q = {}
    q["gdn_conv"] = jnp.transpose(conv.reshape(GDN_TAPS, 12, DH), (1, 0, 2))
    q["gdn_prm"] = [_rep(wts["gdn_a_log"][l]), _rep(wts["gdn_dt_bias"][l]),
                    jnp.broadcast_to(wts["gdn_norm_w"][l].reshape(1, 1, DH), (NH, 1, DH))]
    q["gdn_cst"] = [_onehot_lane(0), _onehot_lane(NH)]
    q["rwkv_mu"] = wts["rwkv_mu"][l].reshape(18, 1, DH)
    w_up = jnp.transpose(_gather_last(tiny["rwkv_w_up"][:, l]).reshape(64, NH, DH), (1, 0, 2))
    a_up = jnp.transpose(_gather_last(tiny["rwkv_a_up"][:, l]).reshape(64, NH, DH), (1, 0, 2))
    q["rwkv_prm"] = [_heads(wts["rwkv_w0"][l]), w_up, _heads(wts["rwkv_a0"][l]), a_up,
                     _heads(wts["rwkv_k_k"][l]), _heads(wts["rwkv_k_a"][l]), _heads(wts["rwkv_r_k"][l]),
                     _heads(wts["rwkv_ln_w"][l]), _heads(wts["rwkv_ln_b"][l])]
    sc = _gather_last(tiny["sc_conv_w"][:, l])
    q["sc_conv"] = jnp.transpose(sc.reshape(SC_TAPS, NH, DH), (1, 0, 2))
    gla_up = jnp.transpose(wts["gla_a_up"][l].reshape(16, NH, GLA_HEAD_K), (1, 0, 2))
    gla_up = jnp.pad(gla_up, ((0, 0), (0, DH - 16), (0, DH - GLA_HEAD_K)))
    gla_b = jnp.pad(wts["gla_a_bias"][l].reshape(NH, 1, GLA_HEAD_K), ((0, 0), (0, 0), (0, DH - GLA_HEAD_K)))
    q["gla_prm"] = [gla_up, gla_b, jnp.broadcast_to(wts["gla_norm_w"][l].reshape(1, 1, DH), (NH, 1, DH))]
    q["pre_w"] = wts["pre_norm_w"][l].reshape(1, D_MODEL)
    q["post_w"] = wts["post_norm_w"][l].reshape(1, D_MODEL)
    return q


def _mixer_inputs(p, cq, pm):
    gdn = [(cq, 4, 0), (cq, 4, 1), (cq, 4, 2), (p, 4, G_GDN // 4 + 3), (p, 1, G_GDN_AB)]
    rwkv = [(pm, 4, 0), (pm, 4, 1), (pm, 4, 2), (pm, 4, 3), (pm, 1, 16), (pm, 1, 17)]
    gla = [(p, 4, G_GLA // 4 + k) for k in range(4)] + [(p, 1, G_GLA_AD)]
    return gdn, rwkv, gla


def _layer_fwd(x, q, nb, t, l, side=None):
    hb, p = _norm_proj(x, q["pre_w"], q["w_g"], f"norm_proj{l}")
    cq = _conv_fwd(p, G_GDN, 12, q["gdn_conv"], nb, t, f"gdn_conv{l}")
    pm = _mix_fwd(p, q["rwkv_mu"], nb, t, f"rwkv_mix{l}")
    gdn_in, rwkv_in, gla_in = _mixer_inputs(p, cq, pm)
    y_gdn, ck_gdn, _ = _mixer_fwd(_gdn_chunk, f"gdn_fwd{l}", gdn_in, q["gdn_prm"], q["gdn_cst"], nb, t)
    y_rwkv, ck_rwkv, side_res = _mixer_fwd(_rwkv_chunk, f"rwkv_fwd{l}", rwkv_in, q["rwkv_prm"], [], nb, t,
                                           first_fn=_rwkv_chunk_steps, side=side)
    y_sc = _sc_fwd(p, q["sc_conv"], nb, t, f"sc_fwd{l}")
    y_gla, ck_gla, _ = _mixer_fwd(_gla_chunk, f"gla_fwd{l}", gla_in, q["gla_prm"], [], nb, t)
    ys = (y_gdn, y_rwkv, y_sc, y_gla)
    out, xn = _out_proj_norm(ys, q["wout_g"], x, q["post_w"], f"out_proj{l}")
    saved = dict(x=x, hb=hb, p=p, cq=cq, pm=pm, ys=ys, out=out, ck=(ck_gdn, ck_rwkv, ck_gla))
    return xn, saved, side_res


def _layer_bwd(dxn, q, sv, nb, t, l, side=None):
    p, cq, pm, ys = sv["p"], sv["cq"], sv["pm"], sv["ys"]
    dout, dy, d_post = _post_bwd(dxn, sv["out"], q["post_w"], q["wout_g"], f"post_bwd{l}")
    d_wout = _dwout(ys, dout, f"dwout{l}").reshape(N_DEV, 128, D_MODEL)
    gdn_in, rwkv_in, gla_in = _mixer_inputs(p, cq, pm)
    ck_gdn, ck_rwkv, ck_gla = sv["ck"]
    g = {}

    (d_conv, dz, dab), (da_log, ddt, dnw), _ = _mixer_bwd(
        _gdn_chunk, f"gdn_bwd{l}", gdn_in, q["gdn_prm"], q["gdn_cst"], ck_gdn, dy, 0,
        [(12, F32), (4, BF16), (1, BF16)], [(0, 0), (0, 4), (0, 8), (1, 0), (2, 0)], nb, t)
    dconv_in, d_gconv = _conv_bwd(p, G_GDN, 12, q["gdn_conv"], d_conv, nb, t, f"gdn_conv_bwd{l}")
    g["gdn_conv_w"] = jnp.transpose(d_gconv.sum(1), (1, 0, 2)).reshape(GDN_TAPS, 768)
    g["gdn_a_log"] = da_log.sum((0, 2, 3))
    g["gdn_dt_bias"] = ddt.sum((0, 2, 3))
    g["gdn_norm_w"] = dnw.sum((0, 1, 2))

    (d_pm,), d_rprm, side_res = _mixer_bwd(
        _rwkv_chunk, f"rwkv_bwd{l}", rwkv_in, q["rwkv_prm"], [], ck_rwkv, dy, 1,
        [(18, F32)], [(0, 0), (0, 4), (0, 8), (0, 12), (0, 16), (0, 17)], nb, t, first_fn=_rwkv_chunk_steps,
        side=side)
    dp_rwkv, d_mu = _mix_bwd(p, q["rwkv_mu"], d_pm, nb, t, f"rwkv_mix_bwd{l}")
    g["rwkv_mu"] = d_mu.sum(1).reshape(1152)
    rp = [a.sum(0) for a in d_rprm]
    g["rwkv_w0"] = rp[0].reshape(256)
    g["rwkv_w_up"] = jnp.transpose(rp[1], (1, 0, 2)).reshape(64, 256)
    g["rwkv_a0"] = rp[2].reshape(256)
    g["rwkv_a_up"] = jnp.transpose(rp[3], (1, 0, 2)).reshape(64, 256)
    for i, nme in enumerate(("rwkv_k_k", "rwkv_k_a", "rwkv_r_k", "rwkv_ln_w", "rwkv_ln_b")):
        g[nme] = rp[4 + i].reshape(256)

    dp_sc, d_scw = _sc_bwd(p, q["sc_conv"], dy, nb, t, f"sc_bwd{l}")
    g["sc_conv_w"] = jnp.transpose(d_scw.sum(1), (1, 0, 2)).reshape(SC_TAPS, 256)

    (dp_gla, dad), (d_aup, d_ab, d_gnw), _ = _mixer_bwd(
        _gla_chunk, f"gla_bwd{l}", gla_in, q["gla_prm"], [], ck_gla, dy, 3,
        [(16, BF16), (1, BF16)], [(0, 0), (0, 4), (0, 8), (0, 12), (1, 0)], nb, t)
    g["gla_a_up"] = jnp.transpose(d_aup.sum(0)[:, :16, :GLA_HEAD_K], (1, 0, 2)).reshape(16, 128)
    g["gla_a_bias"] = d_ab.sum(0)[:, 0, :GLA_HEAD_K].reshape(128)
    g["gla_norm_w"] = d_gnw.sum((0, 1, 2))

    singles = jnp.concatenate([dab, dp_rwkv[16:18], dad], axis=0)
    sources = [dconv_in, dz, dp_rwkv, dp_sc, dp_gla, singles]
    dx, d_pre = _dh_prenorm_bwd(sources, q["w_g"], sv["x"], q["pre_w"], dxn, f"dh_bwd{l}")
    d_win = _regroup_out(_dwin(sv["hb"], sources, f"dwin{l}"), f"regroup_out{l}")
    g["pre_norm_w"] = d_pre.reshape(D_MODEL)
    g["post_norm_w"] = d_post.reshape(D_MODEL)
    return dx, g, d_win, d_wout, side_res


def _local_step(x, tgt, wts, tiny, w_in_all, w_out_all, later_shards=None):
    nb, t, d = x.shape
    xf = x.reshape(nb * t, d)
    overlap = later_shards is not None
    qs, saved = [], []
    big = _big_weights(w_in_all[0], w_out_all[0], 0)
    for l in range(DEPTH):
        q = dict(_layer_params(wts, tiny, l), **big)
        nxt = l + 1 < DEPTH
        side = _gather_plan(later_shards[l]) if overlap and nxt else None
        xf, sv, got = _layer_fwd(xf, q, nb, t, l, side)
        if nxt:
            big = _big_weights(*(got if overlap else (w_in_all[l + 1], w_out_all[l + 1])), l + 1)
        qs.append(q)
        saved.append(sv)
    dxf, lpart = _loss_grad(xf, tgt.reshape(nb * t, d), "loss")
    grads, d_win, d_wout = [None] * DEPTH, [None] * DEPTH, [None] * DEPTH
    for l in reversed(range(DEPTH)):
        side = _exchange_plan([d_win[l + 1], d_wout[l + 1]]) if overlap and l + 1 < DEPTH else None
        dxf, grads[l], d_win[l], d_wout[l], got = _layer_bwd(dxf, qs[l], saved[l], nb, t, l, side)
        if side is not None:
            d_win[l + 1], d_wout[l + 1] = got
    small = {k: jnp.stack([grads[l][k] for l in range(DEPTH)]) for k in grads[0]}
    return lpart[0, 0], dxf.reshape(nb, t, d), small, d_win, d_wout


_WEIGHTS = ("pre_norm_w", "w_in", "gdn_conv_w", "gdn_a_log", "gdn_dt_bias", "gdn_norm_w", "rwkv_mu", "rwkv_w0",
            "rwkv_w_up", "rwkv_a0", "rwkv_a_up", "rwkv_k_k", "rwkv_k_a", "rwkv_r_k", "rwkv_ln_w", "rwkv_ln_b",
            "sc_conv_w", "gla_a_up", "gla_a_bias", "gla_norm_w", "w_out", "post_norm_w")


def kernel(x, pre_norm_w, w_in, gdn_conv_w, gdn_a_log, gdn_dt_bias, gdn_norm_w, rwkv_mu, rwkv_w0, rwkv_w_up, rwkv_a0, rwkv_a_up, rwkv_k_k, rwkv_k_a, rwkv_r_k, rwkv_ln_w, rwkv_ln_b, sc_conv_w, gla_a_up, gla_a_bias, gla_norm_w, w_out, post_norm_w, loss_target, m_pre_norm_w, m_w_in, m_gdn_conv_w, m_gdn_a_log, m_gdn_dt_bias, m_gdn_norm_w, m_rwkv_mu, m_rwkv_w0, m_rwkv_w_up, m_rwkv_a0, m_rwkv_a_up, m_rwkv_k_k, m_rwkv_k_a, m_rwkv_r_k, m_rwkv_ln_w, m_rwkv_ln_b, m_sc_conv_w, m_gla_a_up, m_gla_a_bias, m_gla_norm_w, m_w_out, m_post_norm_w, v_pre_norm_w, v_w_in, v_gdn_conv_w, v_gdn_a_log, v_gdn_dt_bias, v_gdn_norm_w, v_rwkv_mu, v_rwkv_w0, v_rwkv_w_up, v_rwkv_a0, v_rwkv_a_up, v_rwkv_k_k, v_rwkv_k_a, v_rwkv_r_k, v_rwkv_ln_w, v_rwkv_ln_b, v_sc_conv_w, v_gla_a_up, v_gla_a_bias, v_gla_norm_w, v_w_out, v_post_norm_w):
    env = dict(locals())
    w = {n: env[n] for n in _WEIGHTS}
    m = {n: env["m_" + n] for n in _WEIGHTS}
    v = {n: env["v_" + n] for n in _WEIGHTS}
    tiny_names = [n for n, _ in _TINY]

    w_in_b, w_out_b = w_in.astype(BF16), w_out.astype(BF16)
    w_in_0, w_out_0, tiny_all = _run_comm(
        _gather_plan([w_in_b[0], w_out_b[0], _pack_rows([w[n] for n in tiny_names], _TINY_ROWS)]), "gather_weights")
    tiny = _unpack_rows(tiny_all, _TINY)

    lpart, grad_x, small, r_win, r_wout = _local_step(
        x, loss_target, w, tiny, [w_in_0], [w_out_0], later_shards=[(w_in_b[l], w_out_b[l]) for l in range(1, DEPTH)])

    tiny_send = jnp.stack([_pack_rows([_split_last(small[n][l])[d] for n in tiny_names for l in range(DEPTH)],
                                      _TINY_ROWS) for d in range(N_DEV)])
    r_win[0], r_wout[0], r_tiny = _run_comm(_exchange_plan([r_win[0], r_wout[0], tiny_send]), "scatter_grads")
    grads, delta, new_m, new_v = {}, {}, {}, {}
    for n, parts in (("w_in", r_win), ("w_out", r_wout)):
        res = [_sum_adamw(parts[l], w[n][l], m[n][l], v[n][l], f"adamw_{n}{l}") for l in range(DEPTH)]
        grads[n], delta[n], new_m[n], new_v[n] = [jnp.stack(o) for o in zip(*res)]
    tiny_sum = _sum_slots(r_tiny, "sum_tiny").reshape(-1)
    o = 0
    for n, s in _TINY:
        size = int(np.prod(s))
        grads[n] = tiny_sum[o:o + size].reshape(s)
        o += size

    small = dict(small)
    small["loss"] = lpart
    red = _unpack_rows(_all_reduce_small(_pack_rows([small[n] for n, _ in _SMALL], _SMALL_ROWS), "reduce_small"),
                       _SMALL)
    loss = red.pop("loss")
    grads.update(red)

    rest = [n for n in _WEIGHTS if n not in ("w_in", "w_out")]
    rest_shapes = [(n, w[n].shape) for n in rest]
    rows = -(-sum(int(np.prod(s)) for _, s in rest_shapes) // 1024) * 8
    outs = _adamw(_pack_rows([w[n] for n in rest], rows), _pack_rows([grads[n] for n in rest], rows),
                  _pack_rows([m[n] for n in rest], rows), _pack_rows([v[n] for n in rest], rows, 1.0), "adamw_rest")
    for dst, packed in zip((delta, new_m, new_v), outs):
        dst.update(_unpack_rows(packed, rest_shapes))

    return (loss, grad_x, *[grads[n] for n in _WEIGHTS], *[delta[n] for n in _WEIGHTS],
            *[new_m[n] for n in _WEIGHTS], *[new_v[n] for n in _WEIGHTS])
```

```python
import collections
import functools
import math

import numpy as np
import jax
import jax.numpy as jnp
from jax import lax
from jax.experimental import pallas as pl
from jax.experimental.pallas import tpu as pltpu

F32 = jnp.float32
BF16 = jnp.bfloat16

D_MODEL = 1024
DEPTH = 2
NH = 4
DH = 64
CH = 64
EPS = 1e-6
RWKV_GN_EPS = 64e-5
GLA_HEAD_K = 32
GLA_TAU = 16.0
GDN_TAPS = 4
SC_TAPS = 3
D_IN = 3992
N_DEV = 8
SHARD_COLS = D_IN // N_DEV

G_GDN = 0
G_RWKV = 16
G_SC = 32
G_GLA = 48
G_GDN_AB, G_RWKV_WD, G_RWKV_AD, G_GLA_AD = 64, 65, 66, 67
N_GROUPS = 68
GROUPS_PER_STEP = 4
TIME_BLOCK = 256

C_GDN, C_RWKV, C_SC, C_GLA = 0, 1032, 2184, 3208

ADAM_LR, ADAM_B1, ADAM_B2, ADAM_EPS, ADAM_WD, ADAM_STEP = 0.001, 0.9, 0.999, 1e-08, 0.01, 10

VMEM_LIMIT = 56 * 1024 * 1024
MESH = pl.DeviceIdType.MESH

_pcall = pl.pallas_call

_Comm = collections.namedtuple("_Comm", "operands out_shapes copies")


def _cparams(sem=None):
    if sem is None:
        return pltpu.CompilerParams(vmem_limit_bytes=VMEM_LIMIT)
    return pltpu.CompilerParams(dimension_semantics=sem, vmem_limit_bytes=VMEM_LIMIT)


def _group_segments():
    table = [(G_GDN + i, C_GDN + DH * i, DH) for i in range(16)]
    table.append((G_GDN_AB, C_GDN + 1024, 8))
    table += [(G_RWKV + i, C_RWKV + DH * i, DH) for i in range(16)]
    table += [(G_RWKV_WD, C_RWKV + 1024, DH), (G_RWKV_AD, C_RWKV + 1088, DH)]
    table += [(G_SC + 4 * j + k, C_SC + 256 * k + DH * j, DH) for j in range(NH) for k in range(4)]
    for h in range(NH):
        table += [(G_GLA + h, C_GLA + GLA_HEAD_K * h, GLA_HEAD_K),
                  (G_GLA + 4 + h, C_GLA + 128 + GLA_HEAD_K * h, GLA_HEAD_K),
                  (G_GLA + 8 + h, C_GLA + 256 + DH * h, DH),
                  (G_GLA + 12 + h, C_GLA + 512 + DH * h, DH)]
    table.append((G_GLA_AD, C_GLA + 768, 16))
    segs, padded = [], []
    for g, c, n in table:
        if n < DH:
            padded.append(g)
        a = 0
        while n > 0:
            d, off = divmod(c, SHARD_COLS)
            ln = min(n, SHARD_COLS - off)
            segs.append((g, a, d, off, ln))
            c, a, n = c + ln, a + ln, n - ln
    return segs, padded


_SEGMENTS, _PADDED_GROUPS = _group_segments()


def _dn(ta, tb):
    return (((1 if ta else 2,), (2 if tb else 1,)), ((0,), (0,)))


def _hdot(a, b, ta=False, tb=False):
    return lax.dot_general(a, b, _dn(ta, tb), precision=lax.Precision.HIGH, preferred_element_type=F32)


def _r(x):
    return x.astype(BF16)


def _rdot(a, b, ta=False, tb=False):
    return lax.dot_general(_r(a), _r(b), _dn(ta, tb), preferred_element_type=F32)


@jax.custom_vjp
def _bmm(a, b):
    return _rdot(a, b)


def _bmm_fwd(a, b):
    return _rdot(a, b), (a, b)


def _bmm_bwd(res, g):
    a, b = res
    return _rdot(g, b, tb=True), _rdot(a, g, ta=True)


_bmm.defvjp(_bmm_fwd, _bmm_bwd)


@jax.custom_vjp
def _bmm_nt(a, b):
    return _rdot(a, b, tb=True)


def _bmm_nt_fwd(a, b):
    return _rdot(a, b, tb=True), (a, b)


def _bmm_nt_bwd(res, g):
    a, b = res
    return _rdot(g, b), _rdot(g, a, ta=True)


_bmm_nt.defvjp(_bmm_nt_fwd, _bmm_nt_bwd)


@jax.custom_vjp
def _bmm_tn(a, b):
    return _rdot(a, b, ta=True)


def _bmm_tn_fwd(a, b):
    return _rdot(a, b, ta=True), (a, b)


def _bmm_tn_bwd(res, g):
    a, b = res
    return _rdot(b, g, tb=True), _rdot(a, g)


_bmm_tn.defvjp(_bmm_tn_fwd, _bmm_tn_bwd)


def _tri(n):
    i = lax.broadcasted_iota(jnp.int32, (n, n), 0)
    j = lax.broadcasted_iota(jnp.int32, (n, n), 1)
    return i >= j, i > j, i == j


def _heads_of(x, like):
    n = like.shape[0]
    if x.ndim == 2:
        return jnp.broadcast_to(x[None], (n,) + x.shape)
    seqs = x.shape[0]
    return jnp.broadcast_to(x[:, None], (seqs, n // seqs) + x.shape[1:]).reshape((n,) + x.shape[1:])


def _cumsum_rows(x):
    incl, _, _ = _tri(CH)
    return _hdot(_heads_of(incl.astype(F32), x), x)


@jax.custom_vjp
def _inv_unit_lower(a):
    n = a.shape[-1]
    _, _, eye = _tri(n)
    pw = -a
    inv = eye.astype(F32) + pw
    for _ in range(int(math.log2(n)) - 1):
        pw = _hdot(pw, pw)
        inv = inv + _hdot(inv, pw)
    return inv


def _inv_unit_lower_fwd(a):
    inv = _inv_unit_lower(a)
    return inv, inv


def _inv_unit_lower_bwd(inv, g):
    return (-_hdot(_hdot(inv, g, ta=True), inv, tb=True),)


_inv_unit_lower.defvjp(_inv_unit_lower_fwd, _inv_unit_lower_bwd)


def _silu(x):
    return x * jax.nn.sigmoid(x)


def _t(x):
    return jnp.swapaxes(x, -1, -2)


def _gdn_chunk(prm, cst, ins, s):
    a_log, dt_b, nw = prm
    m_a, m_b = cst
    cq, ck, cv, z, ab = ins
    ab = _heads_of(ab, m_a)
    incl, strict, _ = _tri(CH)
    q = _silu(cq)
    k = _silu(ck)
    v = _silu(cv)
    q = q * lax.rsqrt(jnp.sum(q * q, -1, keepdims=True) + EPS) * (DH ** -0.5)
    k = k * lax.rsqrt(jnp.sum(k * k, -1, keepdims=True) + EPS)
    a_raw = jnp.sum(ab * m_a, -1, keepdims=True)
    b_raw = jnp.sum(ab * m_b, -1, keepdims=True)
    gstep = -jnp.exp(a_log) * jax.nn.softplus(a_raw + dt_b)
    beta = jax.nn.sigmoid(b_raw)
    gc = _cumsum_rows(gstep)
    gl = jnp.sum(gstep, -2, keepdims=True)
    dec = jnp.where(incl, jnp.exp(jnp.where(incl, gc - _t(gc), 0.0)), 0.0)
    kb = k * beta
    a_mat = jnp.where(strict, _bmm_nt(kb, k) * dec, 0.0)
    tinv = _inv_unit_lower(a_mat)
    eg = jnp.exp(gc)
    u = _hdot(tinv, v * beta)
    w = _hdot(tinv, kb * eg)
    attn = _bmm_nt(q, k) * dec
    v_new = u - _bmm(w, s)
    o = _bmm(q * eg, s) + _bmm(attn, v_new)
    s_next = s * jnp.exp(gl) + _bmm_tn(k * jnp.exp(gl - gc), v_new)
    on = o * lax.rsqrt(jnp.mean(o * o, -1, keepdims=True) + EPS) * nw
    return on * _silu(z), s_next


def _gla_chunk(prm, cst, ins, st):
    a_up, a_bias, nw = prm
    q, k, v, z, ad = ins
    incl, _, _ = _tri(CH)
    la = jax.nn.log_sigmoid(_bmm(_heads_of(ad, a_up), a_up) + a_bias) * (1.0 / GLA_TAU)
    bc = _cumsum_rows(la)
    bl = jnp.sum(la, -2, keepdims=True)
    qe = q * (GLA_HEAD_K ** -0.5) * jnp.exp(bc)
    ke = k * jnp.exp(-bc)
    attn = jnp.where(incl, _bmm_nt(qe, ke), 0.0)
    o = _bmm_nt(qe, st) + _bmm(attn, v)
    st_next = st * jnp.exp(bl) + _bmm_tn(v, k * jnp.exp(bl - bc))
    on = o * lax.rsqrt(jnp.mean(o * o, -1, keepdims=True) + EPS) * nw
    return on * _silu(z), st_next


def _rwkv_chunk(prm, cst, ins, s):
    r, v = ins[0], ins[2]
    incl, strict, _ = _tri(CH)
    lw, kk, k2, m = _rwkv_pre(prm, ins)
    cum = _cumsum_rows(lw)
    ltot = jnp.sum(lw, -2, keepdims=True)
    n_t = -kk * jnp.exp(cum - lw)
    einv = jnp.exp(-cum)
    m_t = m * einv
    k_t = k2 * einv
    r_t = r * jnp.exp(cum)
    a_nm = jnp.where(strict, _hdot(n_t, m_t, tb=True), 0.0)
    a_nk = jnp.where(strict, _hdot(n_t, k_t, tb=True), 0.0)
    cm = _hdot(_inv_unit_lower(-a_nm), _hdot(n_t, s, tb=True) + _hdot(a_nk, v))
    y = (_hdot(r_t, s, tb=True) + _hdot(jnp.where(incl, _hdot(r_t, m_t, tb=True), 0.0), cm)
         + _hdot(jnp.where(incl, _hdot(r_t, k_t, tb=True), 0.0), v))
    eend = jnp.exp(ltot - cum)
    s_next = s * jnp.exp(ltot) + _hdot(cm, m * eend, ta=True) + _hdot(v, k2 * eend, ta=True)
    return _rwkv_post(prm, ins, y, k2), s_next


def _rwkv_pre(prm, ins):
    w0, w_up, a0, a_up, k_k, k_a = prm[:6]
    k, wd, ad = ins[1], ins[4], ins[5]
    lw = -math.exp(-0.5) * jax.nn.sigmoid(w0 + _bmm(_heads_of(jnp.tanh(wd), w_up), w_up))
    a = jax.nn.sigmoid(a0 + _bmm(_heads_of(ad, a_up), a_up))
    kk = k * k_k
    kk = kk * lax.rsqrt(jnp.sum(kk * kk, -1, keepdims=True) + EPS)
    k2 = k * (1.0 + (a - 1.0) * k_a)
    return lw, kk, k2, kk * a


def _rwkv_post(prm, ins, y, k2):
    r_k, ln_w, ln_b = prm[6:]
    r, v, z = ins[0], ins[2], ins[3]
    mean = jnp.mean(y, -1, keepdims=True)
    yc = y - mean
    var = jnp.mean(yc * yc, -1, keepdims=True)
    yn = yc * lax.rsqrt(var + RWKV_GN_EPS) * ln_w + ln_b
    bonus = jnp.sum(r * k2 * r_k, -1, keepdims=True) * v
    return (yn + bonus) * _silu(z)


@jax.custom_vjp
def _bmv(s, x):
    return jnp.sum(_r(s).astype(F32) * _r(x).astype(F32), -1, keepdims=True)


def _bmv_fwd(s, x):
    return _bmv(s, x), (s, x)


def _bmv_bwd(res, g):
    s, x = res
    return g * x, jnp.sum(_r(s).astype(F32) * _r(g).astype(F32), -2, keepdims=True)


_bmv.defvjp(_bmv_fwd, _bmv_bwd)


def _rwkv_chunk_steps(prm, cst, ins, s):
    r, v = ins[0], ins[2]
    lw, kk, k2, m = _rwkv_pre(prm, ins)
    w = jnp.exp(lw)
    v_t = _t(v)
    lane = lax.broadcasted_iota(jnp.int32, (1, 1, CH), 2)
    y_t = jnp.zeros((s.shape[0], DH, CH), F32)
    for t in range(CH):
        e_t = (lane == t).astype(F32)
        row = (slice(None), slice(t, t + 1))
        sa = _bmv(s, -kk[row])
        s = s * w[row] + sa * m[row] + jnp.sum(v_t * e_t, -1, keepdims=True) * k2[row]
        y_t = y_t + _bmv(s, r[row]) * e_t
    return _rwkv_post(prm, ins, _t(y_t), k2), s


def _time_block(t):
    return TIME_BLOCK if t % TIME_BLOCK == 0 else t


def _load_chunk(ref, i):
    nb = ref.shape[1]
    if ref.shape[0] == NH:
        return jnp.concatenate([ref[:, b, pl.ds(i, CH), :] for b in range(nb)], axis=0)
    return ref[0, :, pl.ds(i, CH), :]


def _load_chunk_of(ref, c, i):
    if ref.shape[0] == NH:
        return ref[pl.ds(c % NH, 1), c // NH, pl.ds(i, CH), :]
    return ref[0, pl.ds(c // NH, 1), pl.ds(i, CH), :]


def _each_chain(n, fn):
    def step(c, carry):
        fn(c)
        return carry

    lax.fori_loop(0, n, step, 0)


def _mixer_fwd(chunk_fn, name, ins, prm, cst, nb, t, first_fn=None, side=None):
    tb = _time_block(t)
    nt, ncb, nch = t // tb, tb // CH, nb * NH
    n_in, n_prm, n_cst = len(ins), len(prm), len(cst)
    n_main, n_side = n_in + n_prm + n_cst, len(side.operands) if side else 0

    def body(*refs):
        in_refs = refs[:n_in]
        prm_refs = refs[n_in:n_in + n_prm]
        cst_refs = refs[n_in + n_prm:n_main]
        side_in = refs[n_main:n_main + n_side]
        y_ref, ck_ref = refs[n_main + n_side:n_main + n_side + 2]
        side_out = refs[n_main + n_side + 2:n_main + 2 * n_side + 2]
        s_scr = refs[n_main + 2 * n_side + 2]
        sems = refs[n_main + 2 * n_side + 3:]
        step_t = pl.program_id(0)

        if side is not None:
            @pl.when(step_t == 0)
            def _():
                _comm_start(side.copies(side_in, side_out, sems))

        @pl.when(step_t == 0)
        def _():
            s_scr[...] = jnp.zeros_like(s_scr)

        def chunk(c, i):
            s = s_scr[...]
            ck_ref[c] = s
            y, s_next = chunk_fn([jnp.tile(r[...], (nb, 1, 1)) for r in prm_refs],
                                 [jnp.tile(r[...], (nb, 1, 1)) for r in cst_refs],
                                 [_load_chunk(r, i) for r in in_refs], s)
            for b in range(nb):
                y_ref[:, b, pl.ds(i, CH), :] = y[b * NH:(b + 1) * NH].astype(BF16)
            s_scr[...] = s_next

        def first_chunk_of(c):
            one, h = pl.ds(c, 1), pl.ds(c % NH, 1)
            s = s_scr[one]
            ck_ref[0, one] = s
            y, s_next = first_fn([r[h] for r in prm_refs], [r[h] for r in cst_refs],
                                 [_load_chunk_of(r, c, 0) for r in in_refs], s)
            y_ref[h, c // NH, pl.ds(0, CH), :] = y.astype(BF16)
            s_scr[one] = s_next

        def step(c, carry):
            chunk(c, pl.multiple_of(c * CH, CH))
            return carry

        if first_fn is None:
            lax.fori_loop(0, ncb, step, 0)
        else:
            @pl.when(step_t == 0)
            def _():
                _each_chain(nch, first_chunk_of)

            @pl.when(step_t != 0)
            def _():
                chunk(0, 0)

            lax.fori_loop(1, ncb, step, 0)

        if side is not None:
            @pl.when(step_t == nt - 1)
            def _():
                _comm_wait(side.copies(side_in, side_out, sems))

    hbm = pl.BlockSpec(memory_space=pl.ANY)
    in_specs = [pl.BlockSpec((ng, nb, tb, DH), (lambda j, bi=bi: (bi, 0, j, 0))) for _, ng, bi in ins]
    in_specs += [pl.BlockSpec(p.shape, lambda j: (0, 0, 0)) for p in list(prm) + list(cst)]
    y, ck, *side_res = _pcall(
        body, name=name, grid=(nt,),
        in_specs=in_specs + [hbm] * n_side,
        out_specs=[pl.BlockSpec((NH, nb, tb, DH), lambda j: (0, 0, j, 0)),
                   pl.BlockSpec((ncb, nch, DH, DH), lambda j: (j, 0, 0, 0))] + [hbm] * n_side,
        out_shape=[jax.ShapeDtypeStruct((NH, nb, t, DH), BF16),
                   jax.ShapeDtypeStruct((t // CH, nch, DH, DH), F32)] + (side.out_shapes if side else []),
        scratch_shapes=[pltpu.VMEM((nch, DH, DH), F32)] + (_comm_scratch(side) if side else []),
        compiler_params=_cparams(("arbitrary",)),
    )(*[a.reshape(a.shape[0], nb, t, DH) for a, _, _ in ins], *prm, *cst, *(side.operands if side else []))
    return y.reshape(NH, nb * t, DH), ck, side_res


def _mixer_bwd(chunk_fn, name, ins, prm, cst, ck, dy, dy_block, outs, routes, nb, t, first_fn=None, side=None):
    tb = _time_block(t)
    nt, ncb, nch = t // tb, tb // CH, nb * NH
    n_in, n_prm, n_cst, n_out = len(ins), len(prm), len(cst), len(outs)
    n_main, n_side = n_in + n_prm + n_cst + 2, len(side.operands) if side else 0

    def body(*refs):
        in_refs = refs[:n_in]
        prm_refs = refs[n_in:n_in + n_prm]
        cst_refs = refs[n_in + n_prm:n_in + n_prm + n_cst]
        ck_ref, dy_ref = refs[n_main - 2:n_main]
        side_in = refs[n_main:n_main + n_side]
        rest = refs[n_main + n_side:]
        out_refs = rest[:n_out]
        dprm_refs = rest[n_out:n_out + n_prm]
        side_out = rest[n_out + n_prm:n_out + n_prm + n_side]
        ds_scr = rest[n_out + n_prm + n_side]
        sems = rest[n_out + n_prm + n_side + 1:]
        step_t = pl.program_id(0)

        if side is not None:
            @pl.when(step_t == 0)
            def _():
                _comm_start(side.copies(side_in, side_out, sems))

        @pl.when(step_t == 0)
        def _():
            ds_scr[...] = jnp.zeros_like(ds_scr)
            for r in dprm_refs:
                r[...] = jnp.zeros_like(r)

        def chunk(c, i):
            cst_v = [jnp.tile(r[...], (nb, 1, 1)) for r in cst_refs]
            _, vjp = jax.vjp(lambda p, x, s: chunk_fn(p, cst_v, x, s),
                             [jnp.tile(r[...], (nb, 1, 1)) for r in prm_refs],
                             [_load_chunk(r, i) for r in in_refs], ck_ref[c])
            dy_c = jnp.concatenate([dy_ref[:, b, pl.ds(i, CH), :] for b in range(nb)], axis=0)
            d_prm, d_ins, d_s = vjp((dy_c, ds_scr[...]))
            for (oi, g0), r, g in zip(routes, in_refs, d_ins):
                o_ref = out_refs[oi]
                if r.shape[0] == NH:
                    for b in range(nb):
                        o_ref[g0:g0 + NH, b, pl.ds(i, CH), :] = g[b * NH:(b + 1) * NH].astype(o_ref.dtype)
                else:
                    o_ref[g0, :, pl.ds(i, CH), :] = g.astype(o_ref.dtype)
            for r, g in zip(dprm_refs, d_prm):
                r[...] += g
            ds_scr[...] = d_s

        def first_chunk_of(c):
            one, h, b = pl.ds(c, 1), pl.ds(c % NH, 1), c // NH
            cst_v = [r[h] for r in cst_refs]
            _, vjp = jax.vjp(lambda p, x, s: first_fn(p, cst_v, x, s), [r[h] for r in prm_refs],
                             [_load_chunk_of(r, c, 0) for r in in_refs], ck_ref[0, one])
            d_prm, d_ins, d_s = vjp((dy_ref[h, b, pl.ds(0, CH), :], ds_scr[one]))
            for (oi, g0), r, g in zip(routes, in_refs, d_ins):
                o_ref = out_refs[oi]
                if r.shape[0] == NH:
                    o_ref.at[g0:g0 + NH][h, b, pl.ds(0, CH), :] = g.astype(o_ref.dtype)
                else:
                    o_ref[g0, pl.ds(b, 1), pl.ds(0, CH), :] += g.astype(o_ref.dtype)
            for r, g in zip(dprm_refs, d_prm):
                r[one] += g
            ds_scr[one] = d_s

        def first_chunk():
            for (oi, g0), r in zip(routes, in_refs):
                if r.shape[0] != NH:
                    out_refs[oi][g0, :, pl.ds(0, CH), :] = jnp.zeros((nb, CH, DH), out_refs[oi].dtype)
            _each_chain(nch, first_chunk_of)

        def step(j, carry):
            c = ncb - 1 - j
            chunk(c, pl.multiple_of(c * CH, CH))
            return carry

        lax.fori_loop(0, ncb - 1, step, 0)
        if first_fn is None:
            chunk(0, 0)
        else:
            @pl.when(step_t == nt - 1)
            def _():
                first_chunk()

            @pl.when(step_t != nt - 1)
            def _():
                chunk(0, 0)

        if side is not None:
            @pl.when(step_t == nt - 1)
            def _():
                _comm_wait(side.copies(side_in, side_out, sems))

    def back(j):
        return nt - 1 - j

    hbm = pl.BlockSpec(memory_space=pl.ANY)
    in_specs = [pl.BlockSpec((ng, nb, tb, DH), (lambda j, bi=bi: (bi, 0, back(j), 0))) for _, ng, bi in ins]
    in_specs += [pl.BlockSpec(p.shape, lambda j: (0, 0, 0)) for p in list(prm) + list(cst)]
    in_specs += [pl.BlockSpec((ncb, nch, DH, DH), lambda j: (back(j), 0, 0, 0)),
                 pl.BlockSpec((NH, nb, tb, DH), lambda j: (dy_block, 0, back(j), 0))]
    out_specs = [pl.BlockSpec((ng, nb, tb, DH), lambda j: (0, 0, back(j), 0)) for ng, _ in outs]
    out_specs += [pl.BlockSpec((nch,) + p.shape[1:], lambda j: (0, 0, 0)) for p in prm]
    out_shape = [jax.ShapeDtypeStruct((ng, nb, t, DH), dt) for ng, dt in outs]
    out_shape += [jax.ShapeDtypeStruct((nch,) + p.shape[1:], F32) for p in prm]
    res = _pcall(
        body, name=name, grid=(nt,),
        in_specs=in_specs + [hbm] * n_side, out_specs=out_specs + [hbm] * n_side,
        out_shape=out_shape + (side.out_shapes if side else []),
        scratch_shapes=[pltpu.VMEM((nch, DH, DH), F32)] + (_comm_scratch(side) if side else []),
        compiler_params=_cparams(("arbitrary",)),
    )(*[a.reshape(a.shape[0], nb, t, DH) for a, _, _ in ins], *prm, *cst, ck, dy.reshape(dy.shape[0], nb, t, DH),
      *(side.operands if side else []))
    d_outs = [o.reshape(o.shape[0], nb * t, DH) for o in res[:n_out]]
    d_prm = [g.reshape((nb,) + p.shape) for g, p in zip(res[n_out:n_out + n_prm], prm)]
    return d_outs, d_prm, res[n_out + n_prm:]


def _shift_down(x, s):
    if s == 0:
        return x
    row = lax.broadcasted_iota(jnp.int32, x.shape, 0)
    return jnp.where(row < s, 0.0, pltpu.roll(x, s, 0))


def _shift_up(x, s):
    if s == 0:
        return x
    t = x.shape[0]
    row = lax.broadcasted_iota(jnp.int32, x.shape, 0)
    return jnp.where(row >= t - s, 0.0, pltpu.roll(x, t - s, 0))


def _conv_fwd(p, g0, ng, w, nb, t, name):
    taps = w.shape[1]

    def body(x_ref, w_ref, y_ref):
        x = x_ref[...]
        acc = w_ref[taps - 1:taps, :] * x
        for i in range(taps - 1):
            acc = acc + w_ref[i:i + 1, :] * _shift_down(x, taps - 1 - i)
        y_ref[...] = acc

    return _pcall(
        body, name=name, grid=(ng, nb),
        in_specs=[pl.BlockSpec((None, t, DH), lambda g, b: (g0 + g, b, 0)),
                  pl.BlockSpec((None, taps, DH), lambda g, b: (g, 0, 0))],
        out_specs=pl.BlockSpec((None, t, DH), lambda g, b: (g, b, 0)),
        out_shape=jax.ShapeDtypeStruct((ng, nb * t, DH), F32),
        compiler_params=_cparams(("parallel", "parallel")),
    )(p, w)


def _conv_bwd(p, g0, ng, w, dy, nb, t, name):
    taps = w.shape[1]

    def body(x_ref, w_ref, dy_ref, dx_ref, dw_ref):
        x = x_ref[...]
        d = dy_ref[...]
        acc = w_ref[taps - 1:taps, :] * d
        dw_ref[taps - 1:taps, :] = jnp.sum(d * x, 0, keepdims=True)
        for i in range(taps - 1):
            s = taps - 1 - i
            acc = acc + w_ref[i:i + 1, :] * _shift_up(d, s)
            dw_ref[i:i + 1, :] = jnp.sum(d * _shift_down(x, s), 0, keepdims=True)
        dx_ref[...] = acc.astype(BF16)

    return _pcall(
        body, name=name, grid=(ng, nb),
        in_specs=[pl.BlockSpec((None, t, DH), lambda g, b: (g0 + g, b, 0)),
                  pl.BlockSpec((None, taps, DH), lambda g, b: (g, 0, 0)),
                  pl.BlockSpec((None, t, DH), lambda g, b: (g, b, 0))],
        out_specs=[pl.BlockSpec((None, t, DH), lambda g, b: (g, b, 0)),
                   pl.BlockSpec((None, None, taps, DH), lambda g, b: (g, b, 0, 0))],
        out_shape=[jax.ShapeDtypeStruct((ng, nb * t, DH), BF16),
                   jax.ShapeDtypeStruct((ng, nb, taps, DH), F32)],
        compiler_params=_cparams(("parallel", "parallel")),
    )(p, w, dy)


def _mix_group(g):
    return jnp.where(g < 16, G_RWKV + g, G_RWKV_WD + g - 16)


def _mix_fwd(p, mu, nb, t, name):
    def body(x_ref, mu_ref, y_ref):
        x = x_ref[...]
        y_ref[...] = x + mu_ref[...] * (_shift_down(x, 1) - x)

    return _pcall(
        body, name=name, grid=(18, nb),
        in_specs=[pl.BlockSpec((None, t, DH), lambda g, b: (_mix_group(g), b, 0)),
                  pl.BlockSpec((None, 1, DH), lambda g, b: (g, 0, 0))],
        out_specs=pl.BlockSpec((None, t, DH), lambda g, b: (g, b, 0)),
        out_shape=jax.ShapeDtypeStruct((18, nb * t, DH), F32),
        compiler_params=_cparams(("parallel", "parallel")),
    )(p, mu)


def _mix_bwd(p, mu, dy, nb, t, name):
    def body(x_ref, mu_ref, dy_ref, dx_ref, dmu_ref):
        x = x_ref[...]
        muv = mu_ref[...]
        d = dy_ref[...]
        dx_ref[...] = (d * (1.0 - muv) + _shift_up(d * muv, 1)).astype(BF16)
        dmu_ref[...] = jnp.sum(d * (_shift_down(x, 1) - x), 0, keepdims=True)

    return _pcall(
        body, name=name, grid=(18, nb),
        in_specs=[pl.BlockSpec((None, t, DH), lambda g, b: (_mix_group(g), b, 0)),
                  pl.BlockSpec((None, 1, DH), lambda g, b: (g, 0, 0)),
                  pl.BlockSpec((None, t, DH), lambda g, b: (g, b, 0))],
        out_specs=[pl.BlockSpec((None, t, DH), lambda g, b: (g, b, 0)),
                   pl.BlockSpec((None, None, 1, DH), lambda g, b: (g, b, 0, 0))],
        out_shape=[jax.ShapeDtypeStruct((18, nb * t, DH), BF16),
                   jax.ShapeDtypeStruct((18, nb, 1, DH), F32)],
        compiler_params=_cparams(("parallel", "parallel")),
    )(p, mu, dy)


def _sc_fwd(p, w, nb, t, name):
    def body(p_ref, w_ref, y_ref):
        u = p_ref[1] * p_ref[2]
        conv = w_ref[2:3, :] * u + w_ref[1:2, :] * _shift_down(u, 1) + w_ref[0:1, :] * _shift_down(u, 2)
        y_ref[...] = (p_ref[0] * conv * _silu(p_ref[3])).astype(BF16)

    return _pcall(
        body, name=name, grid=(NH, nb),
        in_specs=[pl.BlockSpec((4, t, DH), lambda j, b: (G_SC // 4 + j, b, 0)),
                  pl.BlockSpec((None, SC_TAPS, DH), lambda j, b: (j, 0, 0))],
        out_specs=pl.BlockSpec((None, t, DH), lambda j, b: (j, b, 0)),
        out_shape=jax.ShapeDtypeStruct((NH, nb * t, DH), BF16),
        compiler_params=_cparams(("parallel", "parallel")),
    )(p, w)


def _sc_bwd(p, w, dy, nb, t, name):
    def body(p_ref, w_ref, dy_ref, dp_ref, dw_ref):
        bg, cg, xg, z = p_ref[0], p_ref[1], p_ref[2], p_ref[3]
        d = dy_ref[...]
        u = cg * xg
        u1 = _shift_down(u, 1)
        u2 = _shift_down(u, 2)
        conv = w_ref[2:3, :] * u + w_ref[1:2, :] * u1 + w_ref[0:1, :] * u2
        sg = jax.nn.sigmoid(z)
        sz = z * sg
        dp_ref[0] = (d * conv * sz).astype(BF16)
        dp_ref[3] = (d * bg * conv * (sg * (1.0 + z * (1.0 - sg)))).astype(BF16)
        dconv = d * bg * sz
        du = w_ref[2:3, :] * dconv + w_ref[1:2, :] * _shift_up(dconv, 1) + w_ref[0:1, :] * _shift_up(dconv, 2)
        dp_ref[1] = (du * xg).astype(BF16)
        dp_ref[2] = (du * cg).astype(BF16)
        dw_ref[2:3, :] = jnp.sum(dconv * u, 0, keepdims=True)
        dw_ref[1:2, :] = jnp.sum(dconv * u1, 0, keepdims=True)
        dw_ref[0:1, :] = jnp.sum(dconv * u2, 0, keepdims=True)

    return _pcall(
        body, name=name, grid=(NH, nb),
        in_specs=[pl.BlockSpec((4, t, DH), lambda j, b: (G_SC // 4 + j, b, 0)),
                  pl.BlockSpec((None, SC_TAPS, DH), lambda j, b: (j, 0, 0)),
                  pl.BlockSpec((None, t, DH), lambda j, b: (8 + j, b, 0))],
        out_specs=[pl.BlockSpec((4, t, DH), lambda j, b: (j, b, 0)),
                   pl.BlockSpec((None, None, SC_TAPS, DH), lambda j, b: (j, b, 0, 0))],
        out_shape=[jax.ShapeDtypeStruct((4 * NH, nb * t, DH), BF16),
                   jax.ShapeDtypeStruct((NH, nb, SC_TAPS, DH), F32)],
        compiler_params=_cparams(("parallel", "parallel")),
    )(p, w, dy)


def _row_tile(n):
    return 1024 if n % 1024 == 0 else n


def _regroup_in(w_all, name):
    tr = 256
    gs = GROUPS_PER_STEP

    def body(w_ref, o_ref):
        for g in _PADDED_GROUPS:
            o_ref[g // gs, :, DH * (g % gs):DH * (g % gs + 1)] = jnp.zeros((tr, DH), BF16)
        for g, a, d, off, ln in _SEGMENTS:
            lane = DH * (g % gs) + a
            o_ref[g // gs, :, lane:lane + ln] = w_ref[d, :, off:off + ln].astype(BF16)

    return _pcall(
        body, name=name, grid=(D_MODEL // tr,),
        in_specs=[pl.BlockSpec((N_DEV, tr, SHARD_COLS), lambda i: (0, i, 0))],
        out_specs=pl.BlockSpec((N_GROUPS // gs, tr, gs * DH), lambda i: (0, i, 0)),
        out_shape=jax.ShapeDtypeStruct((N_GROUPS // gs, D_MODEL, gs * DH), BF16),
        compiler_params=_cparams(("parallel",)),
    )(w_all)


def _regroup_out(dwg, name):
    tr = 256
    gs = GROUPS_PER_STEP

    def body(g_ref, o_ref):
        for g, a, d, off, ln in _SEGMENTS:
            lane = DH * (g % gs) + a
            o_ref[d, :, off:off + ln] = g_ref[g // gs, :, lane:lane + ln]

    return _pcall(
        body, name=name, grid=(D_MODEL // tr,),
        in_specs=[pl.BlockSpec((N_GROUPS // gs, tr, gs * DH), lambda i: (0, i, 0))],
        out_specs=pl.BlockSpec((N_DEV, tr, SHARD_COLS), lambda i: (0, i, 0)),
        out_shape=jax.ShapeDtypeStruct((N_DEV, D_MODEL, SHARD_COLS), F32),
        compiler_params=_cparams(("parallel",)),
    )(dwg)


def _norm_proj(x, pre_w, w_g, name):
    n = x.shape[0]
    tm = _row_tile(n)
    gs = GROUPS_PER_STEP

    def body(x_ref, pw_ref, w_ref, h_ref, p_ref):
        @pl.when(pl.program_id(1) == 0)
        def _():
            xv = x_ref[...]
            h = xv * lax.rsqrt(jnp.mean(xv * xv, -1, keepdims=True) + EPS) * pw_ref[...]
            h_ref[...] = h.astype(BF16)

        r = jnp.dot(h_ref[...], w_ref[...], preferred_element_type=F32)
        for k in range(gs):
            p_ref[k] = r[:, DH * k:DH * (k + 1)]

    return _pcall(
        body, name=name, grid=(n // tm, N_GROUPS // gs),
        in_specs=[pl.BlockSpec((tm, D_MODEL), lambda i, j: (i, 0)),
                  pl.BlockSpec((1, D_MODEL), lambda i, j: (0, 0)),
                  pl.BlockSpec((None, D_MODEL, gs * DH), lambda i, j: (j, 0, 0))],
        out_specs=[pl.BlockSpec((tm, D_MODEL), lambda i, j: (i, 0)),
                   pl.BlockSpec((gs, tm, DH), lambda i, j: (j, i, 0))],
        out_shape=[jax.ShapeDtypeStruct((n, D_MODEL), BF16),
                   jax.ShapeDtypeStruct((N_GROUPS, n, DH), F32)],
        compiler_params=_cparams(("parallel", "arbitrary")),
    )(x, pre_w, w_g)


def _out_proj_norm(ys, wout_g, x, post_w, name):
    n = x.shape[0]
    tm = _row_tile(n)

    def body(y0, y1, y2, y3, w_ref, x_ref, pw_ref, out_ref, xn_ref):
        acc = jnp.zeros((tm, D_MODEL), F32)
        for m, yr in enumerate((y0, y1, y2, y3)):
            for h in range(NH):
                acc = acc + jnp.dot(yr[h], w_ref[m * NH + h], preferred_element_type=F32)
        out_ref[...] = acc
        xn_ref[...] = x_ref[...] + acc * lax.rsqrt(jnp.mean(acc * acc, -1, keepdims=True) + EPS) * pw_ref[...]

    yspec = pl.BlockSpec((NH, tm, DH), lambda i: (0, i, 0))
    rows = pl.BlockSpec((tm, D_MODEL), lambda i: (i, 0))
    return _pcall(
        body, name=name, grid=(n // tm,),
        in_specs=[yspec] * 4 + [pl.BlockSpec((4 * NH, DH, D_MODEL), lambda i: (0, 0, 0)), rows,
                                pl.BlockSpec((1, D_MODEL), lambda i: (0, 0))],
        out_specs=[rows, rows],
        out_shape=[jax.ShapeDtypeStruct((n, D_MODEL), F32)] * 2,
        compiler_params=_cparams(("parallel",)),
    )(*ys, wout_g, x, post_w)


def _loss_grad(x, tgt, name):
    n = x.shape[0]
    tm = _row_tile(n)

    def body(x_ref, t_ref, dx_ref, l_ref):
        @pl.when(pl.program_id(0) == 0)
        def _():
            l_ref[...] = jnp.zeros_like(l_ref)

        e = x_ref[...] - t_ref[...]
        dx_ref[...] = e * (1.0 / D_MODEL)
        l_ref[...] += jnp.sum(jnp.sum(e * e, -1, keepdims=True), 0, keepdims=True) * (0.5 / D_MODEL)

    rows = pl.BlockSpec((tm, D_MODEL), lambda i: (i, 0))
    return _pcall(
        body, name=name, grid=(n // tm,),
        in_specs=[rows, rows],
        out_specs=[rows, pl.BlockSpec((1, 128), lambda i: (0, 0))],
        out_shape=[jax.ShapeDtypeStruct((n, D_MODEL), F32), jax.ShapeDtypeStruct((1, 128), F32)],
        compiler_params=_cparams(("arbitrary",)),
    )(x, tgt)


def _rmsnorm_bwd(xv, w, d):
    r = lax.rsqrt(jnp.mean(xv * xv, -1, keepdims=True) + EPS)
    xh = xv * r
    dxh = d * w
    dx = r * (dxh - xh * jnp.mean(dxh * xh, -1, keepdims=True))
    return dx, d * xh


def _post_bwd(dxn, out, post_w, wout_g, name):
    n = dxn.shape[0]
    tm = _row_tile(n)

    def body(d_ref, o_ref, pw_ref, w_ref, do_ref, dy_ref, dpw_ref):
        @pl.when(pl.program_id(0) == 0)
        def _():
            dpw_ref[...] = jnp.zeros_like(dpw_ref)

        dout, dw_rows = _rmsnorm_bwd(o_ref[...], pw_ref[...], d_ref[...])
        dpw_ref[...] += jnp.sum(dw_rows, 0, keepdims=True)
        db = dout.astype(BF16)
        do_ref[...] = db
        for g in range(4 * NH):
            dy_ref[g] = lax.dot_general(db, w_ref[g], (((1,), (1,)), ((), ())), preferred_element_type=F32)

    rows = pl.BlockSpec((tm, D_MODEL), lambda i: (i, 0))
    vec = pl.BlockSpec((1, D_MODEL), lambda i: (0, 0))
    return _pcall(
        body, name=name, grid=(n // tm,),
        in_specs=[rows, rows, vec, pl.BlockSpec((4 * NH, DH, D_MODEL), lambda i: (0, 0, 0))],
        out_specs=[rows, pl.BlockSpec((4 * NH, tm, DH), lambda i: (0, i, 0)), vec],
        out_shape=[jax.ShapeDtypeStruct((n, D_MODEL), BF16),
                   jax.ShapeDtypeStruct((4 * NH, n, DH), F32),
                   jax.ShapeDtypeStruct((1, D_MODEL), F32)],
        compiler_params=_cparams(("arbitrary",)),
    )(dxn, out, post_w, wout_g)


def _dwout(ys, dout, name):
    n = dout.shape[0]
    tm = _row_tile(n)

    def body(y0, y1, y2, y3, d_ref, dw_ref):
        @pl.when(pl.program_id(0) == 0)
        def _():
            dw_ref[...] = jnp.zeros_like(dw_ref)

        d = d_ref[...]
        for m, yr in enumerate((y0, y1, y2, y3)):
            for h in range(NH):
                dw_ref[m * NH + h] += lax.dot_general(yr[h], d, (((0,), (0,)), ((), ())),
                                                      preferred_element_type=F32)

    yspec = pl.BlockSpec((NH, tm, DH), lambda i: (0, i, 0))
    return _pcall(
        body, name=name, grid=(n // tm,),
        in_specs=[yspec] * 4 + [pl.BlockSpec((tm, D_MODEL), lambda i: (i, 0))],
        out_specs=pl.BlockSpec((4 * NH, DH, D_MODEL), lambda i: (0, 0, 0)),
        out_shape=jax.ShapeDtypeStruct((4 * NH, DH, D_MODEL), F32),
        compiler_params=_cparams(("arbitrary",)),
    )(*ys, dout)


def _source_specs(sources, rows_first):
    gs = GROUPS_PER_STEP
    spans, specs, j0 = [], [], 0
    for a in sources:
        nblk = a.shape[0] // gs
        spans.append((j0, j0 + nblk))
        shape = (gs, _row_tile(a.shape[1]), DH)

        def blk(j, j0=j0, nblk=nblk):
            return jnp.clip(j - j0, 0, nblk - 1)

        if rows_first:
            specs.append(pl.BlockSpec(shape, (lambda i, j, blk=blk: (blk(j), i, 0))))
        else:
            specs.append(pl.BlockSpec(shape, (lambda j, i, blk=blk: (blk(j), i, 0))))
        j0 += nblk
    return spans, specs


def _dh_prenorm_bwd(sources, w_g, x, pre_w, dxn, name):
    n = x.shape[0]
    tm = _row_tile(n)
    gs = GROUPS_PER_STEP
    nj = N_GROUPS // gs
    spans, src_specs = _source_specs(sources, True)
    ns = len(sources)

    def body(*refs):
        src = refs[:ns]
        w_ref, x_ref, pw_ref, d_ref, dx_ref, dpw_ref, acc = refs[ns:]
        i, j = pl.program_id(0), pl.program_id(1)

        @pl.when((i == 0) & (j == 0))
        def _():
            dpw_ref[...] = jnp.zeros_like(dpw_ref)

        @pl.when(j == 0)
        def _():
            acc[...] = jnp.zeros_like(acc)

        for s_ref, (lo, hi) in zip(src, spans):
            @pl.when((j >= lo) & (j < hi))
            def _(s_ref=s_ref):
                four = jnp.concatenate([s_ref[k] for k in range(gs)], axis=-1)
                acc[...] += lax.dot_general(four, w_ref[...], (((1,), (1,)), ((), ())), preferred_element_type=F32)

        @pl.when(j == nj - 1)
        def _():
            dx, dw_rows = _rmsnorm_bwd(x_ref[...], pw_ref[...], acc[...])
            dx_ref[...] = d_ref[...] + dx
            dpw_ref[...] += jnp.sum(dw_rows, 0, keepdims=True)

    rows = pl.BlockSpec((tm, D_MODEL), lambda i, j: (i, 0))
    vec = pl.BlockSpec((1, D_MODEL), lambda i, j: (0, 0))
    return _pcall(
        body, name=name, grid=(n // tm, nj),
        in_specs=src_specs + [pl.BlockSpec((None, D_MODEL, gs * DH), lambda i, j: (j, 0, 0)), rows, vec, rows],
        out_specs=[rows, vec],
        out_shape=[jax.ShapeDtypeStruct((n, D_MODEL), F32), jax.ShapeDtypeStruct((1, D_MODEL), F32)],
        scratch_shapes=[pltpu.VMEM((tm, D_MODEL), F32)],
        compiler_params=_cparams(("arbitrary", "arbitrary")),
    )(*sources, w_g, x, pre_w, dxn)


def _dwin(hb, sources, name):
    n = hb.shape[0]
    tm = _row_tile(n)
    gs = GROUPS_PER_STEP
    spans, src_specs = _source_specs(sources, False)
    ns = len(sources)

    def body(*refs):
        h_ref = refs[0]
        src = refs[1:1 + ns]
        dw_ref = refs[1 + ns]
        j = pl.program_id(0)

        @pl.when(pl.program_id(1) == 0)
        def _():
            dw_ref[...] = jnp.zeros_like(dw_ref)

        h = h_ref[...]
        for s_ref, (lo, hi) in zip(src, spans):
            @pl.when((j >= lo) & (j < hi))
            def _(s_ref=s_ref):
                four = jnp.concatenate([s_ref[k] for k in range(gs)], axis=-1)
                dw_ref[...] += jnp.dot(h, four, preferred_element_type=F32)

    return _pcall(
        body, name=name, grid=(N_GROUPS // gs, n // tm),
        in_specs=[pl.BlockSpec((D_MODEL, tm), lambda j, i: (0, i))] + src_specs,
        out_specs=pl.BlockSpec((None, D_MODEL, gs * DH), lambda j, i: (j, 0, 0)),
        out_shape=jax.ShapeDtypeStruct((N_GROUPS // gs, D_MODEL, gs * DH), F32),
        compiler_params=_cparams(("parallel", "arbitrary")),
    )(jnp.transpose(hb), *sources)


def _adamw_math(w, g, m, v):
    c1 = 1.0 - ADAM_B1 ** ADAM_STEP
    c2 = 1.0 - ADAM_B2 ** ADAM_STEP
    nm = ADAM_B1 * m + (1.0 - ADAM_B1) * g
    nv = ADAM_B2 * v + (1.0 - ADAM_B2) * (g * g)
    return -ADAM_LR * ((nm / c1) / (jnp.sqrt(nv / c2) + ADAM_EPS) + ADAM_WD * w), nm, nv


def _adamw(w, g, m, v, name):
    r, c = w.shape
    tr = 256 if r % 256 == 0 else r

    def body(w_ref, g_ref, m_ref, v_ref, d_ref, nm_ref, nv_ref):
        d_ref[...], nm_ref[...], nv_ref[...] = _adamw_math(w_ref[...], g_ref[...], m_ref[...], v_ref[...])

    spec = pl.BlockSpec((tr, c), lambda i: (i, 0))
    return _pcall(
        body, name=name, grid=(r // tr,),
        in_specs=[spec] * 4, out_specs=[spec] * 3,
        out_shape=[jax.ShapeDtypeStruct((r, c), F32)] * 3,
        compiler_params=_cparams(("parallel",)),
    )(w, g, m, v)


def _sum_adamw(parts, w, m, v, name):
    r, c = w.shape
    tr = 128 if r % 128 == 0 else r

    def body(p_ref, w_ref, m_ref, v_ref, g_ref, d_ref, nm_ref, nv_ref):
        g = p_ref[0]
        for k in range(1, N_DEV):
            g = g + p_ref[k]
        g_ref[...] = g
        d_ref[...], nm_ref[...], nv_ref[...] = _adamw_math(w_ref[...], g, m_ref[...], v_ref[...])

    spec = pl.BlockSpec((tr, c), lambda i: (i, 0))
    return _pcall(
        body, name=name, grid=(r // tr,),
        in_specs=[pl.BlockSpec((N_DEV, tr, c), lambda i: (0, i, 0))] + [spec] * 3, out_specs=[spec] * 4,
        out_shape=[jax.ShapeDtypeStruct((r, c), F32)] * 4,
        compiler_params=_cparams(("parallel",)),
    )(parts, w, m, v)


def _me():
    return lax.axis_index("x"), lax.axis_index("y"), lax.axis_index("c")


def _flat(x, y, c):
    return 4 * x + 2 * y + c


def _peer(k):
    x, y, c = _me()
    return (x ^ ((k >> 2) & 1), y ^ ((k >> 1) & 1), c ^ (k & 1))


def _gather_plan(blocks):
    def copies(x_refs, out_refs, sems):
        send_sems, recv_sems, local_sems = sems
        me = _flat(*_me())
        local = [pltpu.make_async_copy(x, o.at[me], local_sems.at[a]) for a, (x, o) in enumerate(zip(x_refs, out_refs))]
        outgoing, incoming = [], []
        for k in range(1, N_DEV):
            src = _flat(*_peer(k))
            for a, (x, o) in enumerate(zip(x_refs, out_refs)):
                for slot, group in ((me, outgoing), (src, incoming)):
                    group.append(pltpu.make_async_remote_copy(
                        src_ref=x, dst_ref=o.at[slot], send_sem=send_sems.at[a, k - 1], recv_sem=recv_sems.at[a, k - 1],
                        device_id=_peer(k), device_id_type=MESH))
        return local, outgoing, incoming

    return _Comm(list(blocks), [jax.ShapeDtypeStruct((N_DEV,) + b.shape, b.dtype) for b in blocks], copies)


def _exchange_plan(sends):
    def copies(s_refs, out_refs, sems):
        send_sems, recv_sems, local_sems = sems
        me = _flat(*_me())
        local = [pltpu.make_async_copy(s.at[me], o.at[0], local_sems.at[i]) for i, (s, o) in enumerate(zip(s_refs, out_refs))]
        outgoing = []
        for k in range(1, N_DEV):
            to = _flat(*_peer(k))
            for i, (s, o) in enumerate(zip(s_refs, out_refs)):
                outgoing.append(pltpu.make_async_remote_copy(
                    src_ref=s.at[to], dst_ref=o.at[k], send_sem=send_sems.at[i, k - 1], recv_sem=recv_sems.at[i, k - 1],
                    device_id=_peer(k), device_id_type=MESH))
        return local, outgoing, outgoing

    return _Comm(list(sends), [jax.ShapeDtypeStruct(s.shape, s.dtype) for s in sends], copies)


def _comm_scratch(plan):
    n = len(plan.operands)
    return [pltpu.SemaphoreType.DMA((n, N_DEV - 1)), pltpu.SemaphoreType.DMA((n, N_DEV - 1)),
            pltpu.SemaphoreType.DMA((n,))]


def _comm_start(copies):
    local, outgoing, _ = copies
    for cp in local + outgoing:
        cp.start()


def _comm_wait(copies):
    local, outgoing, incoming = copies
    for cp in incoming:
        cp.wait_recv()
    for cp in outgoing:
        cp.wait_send()
    for cp in local:
        cp.wait()


def _run_comm(plan, name):
    n = len(plan.operands)

    def body(*refs):
        copies = plan.copies(refs[:n], refs[n:2 * n], refs[2 * n:])
        _comm_start(copies)
        _comm_wait(copies)

    return _pcall(
        body, name=name,
        in_specs=[pl.BlockSpec(memory_space=pl.ANY)] * n,
        out_specs=[pl.BlockSpec(memory_space=pl.ANY)] * n,
        out_shape=plan.out_shapes,
        scratch_shapes=_comm_scratch(plan),
    )(*plan.operands)


def _sum_slots(a, name):
    r = a.shape[1]

    def body(a_ref, o_ref):
        acc = a_ref[0]
        for d in range(1, N_DEV):
            acc = acc + a_ref[d]
        o_ref[...] = acc

    return _pcall(body, name=name, out_shape=jax.ShapeDtypeStruct((r, 128), F32), compiler_params=_cparams())(a)


def _all_reduce_small(blk, name):
    r = blk.shape[0]

    def body(x_ref, out_ref, gath, send_sems, recv_sems):
        me = _flat(*_me())
        gath[me] = x_ref[...]
        copies = []
        for k in range(1, N_DEV):
            cp = pltpu.make_async_remote_copy(
                src_ref=x_ref, dst_ref=gath.at[me],
                send_sem=send_sems.at[k - 1], recv_sem=recv_sems.at[k - 1],
                device_id=_peer(k), device_id_type=MESH)
            cp.start()
            copies.append(cp)
        for k in range(1, N_DEV):
            src = _flat(*_peer(k))
            pltpu.make_async_remote_copy(
                src_ref=x_ref, dst_ref=gath.at[src],
                send_sem=send_sems.at[k - 1], recv_sem=recv_sems.at[k - 1],
                device_id=_peer(k), device_id_type=MESH).wait_recv()
        for cp in copies:
            cp.wait_send()
        acc = gath[0]
        for d in range(1, N_DEV):
            acc = acc + gath[d]
        out_ref[...] = acc

    return _pcall(
        body, name=name,
        in_specs=[pl.BlockSpec(memory_space=pltpu.VMEM)],
        out_specs=pl.BlockSpec(memory_space=pltpu.VMEM),
        out_shape=jax.ShapeDtypeStruct((r, 128), F32),
        scratch_shapes=[pltpu.VMEM((N_DEV, r, 128), F32),
                        pltpu.SemaphoreType.DMA((N_DEV - 1,)), pltpu.SemaphoreType.DMA((N_DEV - 1,))],
    )(blk)


def _heads(vec):
    return vec.reshape(NH, 1, DH)


def _rep(vec4):
    return jnp.broadcast_to(vec4.reshape(NH, 1, 1), (NH, 1, DH))


def _onehot_lane(offset):
    m = np.zeros((NH, 1, DH), np.float32)
    for h in range(NH):
        m[h, 0, offset + h] = 1.0
    return jnp.asarray(m)


_TINY = (("gdn_conv_w", (DEPTH, 4, 96)), ("rwkv_w_up", (DEPTH, 64, 32)), ("rwkv_a_up", (DEPTH, 64, 32)),
         ("sc_conv_w", (DEPTH, 3, 32)))
_TINY_ROWS = -(-sum(int(np.prod(s)) for _, s in _TINY) // 1024) * 8


def _pack_rows(arrays, rows, fill=0.0):
    flat = jnp.concatenate([a.reshape(-1) for a in arrays])
    return jnp.pad(flat, (0, rows * 128 - flat.shape[0]), constant_values=fill).reshape(rows, 128)


def _unpack_rows(p, named_shapes):
    lead = p.shape[:-2]
    flat = p.reshape(lead + (-1,))
    out, o = {}, 0
    for n, s in named_shapes:
        size = int(np.prod(s))
        out[n] = flat[..., o:o + size].reshape(lead + tuple(s))
        o += size
    return out


def _gather_last(a):
    return jnp.transpose(a, (1, 0, 2)).reshape(a.shape[1], -1)


def _split_last(a):
    r, c8 = a.shape
    return jnp.transpose(a.reshape(r, N_DEV, c8 // N_DEV), (1, 0, 2))


_SMALL = (("pre_norm_w", (DEPTH, 1024)), ("gdn_a_log", (DEPTH, 4)), ("gdn_dt_bias", (DEPTH, 4)),
          ("gdn_norm_w", (DEPTH, 64)), ("rwkv_mu", (DEPTH, 1152)), ("rwkv_w0", (DEPTH, 256)),
          ("rwkv_a0", (DEPTH, 256)), ("rwkv_k_k", (DEPTH, 256)), ("rwkv_k_a", (DEPTH, 256)),
          ("rwkv_r_k", (DEPTH, 256)), ("rwkv_ln_w", (DEPTH, 256)), ("rwkv_ln_b", (DEPTH, 256)),
          ("gla_a_up", (DEPTH, 16, 128)), ("gla_a_bias", (DEPTH, 128)), ("gla_norm_w", (DEPTH, 64)),
          ("post_norm_w", (DEPTH, 1024)), ("loss", ()))
_SMALL_ROWS = -(-sum(int(np.prod(s)) for _, s in _SMALL) // 1024) * 8


def _big_weights(w_in_all, w_out_all, l):
    return dict(w_g=_regroup_in(w_in_all, f"regroup_in{l}"),
                wout_g=w_out_all.reshape(4 * NH, DH, D_MODEL).astype(BF16))


def _layer_params(wts, tiny, l):
    conv = _gather_last(tiny["gdn_conv_w"][:, l])
    q = {}
    q["gdn_conv"] = jnp.transpose(conv.reshape(GDN_TAPS, 12, DH), (1, 0, 2))
    q["gdn_prm"] = [_rep(wts["gdn_a_log"][l]), _rep(wts["gdn_dt_bias"][l]),
                    jnp.broadcast_to(wts["gdn_norm_w"][l].reshape(1, 1, DH), (NH, 1, DH))]
    q["gdn_cst"] = [_onehot_lane(0), _onehot_lane(NH)]
    q["rwkv_mu"] = wts["rwkv_mu"][l].reshape(18, 1, DH)
    w_up = jnp.transpose(_gather_last(tiny["rwkv_w_up"][:, l]).reshape(64, NH, DH), (1, 0, 2))
    a_up = jnp.transpose(_gather_last(tiny["rwkv_a_up"][:, l]).reshape(64, NH, DH), (1, 0, 2))
    q["rwkv_prm"] = [_heads(wts["rwkv_w0"][l]), w_up, _heads(wts["rwkv_a0"][l]), a_up,
                     _heads(wts["rwkv_k_k"][l]), _heads(wts["rwkv_k_a"][l]), _heads(wts["rwkv_r_k"][l]),
                     _heads(wts["rwkv_ln_w"][l]), _heads(wts["rwkv_ln_b"][l])]
    sc = _gather_last(tiny["sc_conv_w"][:, l])
    q["sc_conv"] = jnp.transpose(sc.reshape(SC_TAPS, NH, DH), (1, 0, 2))
    gla_up = jnp.transpose(wts["gla_a_up"][l].reshape(16, NH, GLA_HEAD_K), (1, 0, 2))
    gla_up = jnp.pad(gla_up, ((0, 0), (0, DH - 16), (0, DH - GLA_HEAD_K)))
    gla_b = jnp.pad(wts["gla_a_bias"][l].reshape(NH, 1, GLA_HEAD_K), ((0, 0), (0, 0), (0, DH - GLA_HEAD_K)))
    q["gla_prm"] = [gla_up, gla_b, jnp.broadcast_to(wts["gla_norm_w"][l].reshape(1, 1, DH), (NH, 1, DH))]
    q["pre_w"] = wts["pre_norm_w"][l].reshape(1, D_MODEL)
    q["post_w"] = wts["post_norm_w"][l].reshape(1, D_MODEL)
    return q


def _mixer_inputs(p, cq, pm):
    gdn = [(cq, 4, 0), (cq, 4, 1), (cq, 4, 2), (p, 4, G_GDN // 4 + 3), (p, 1, G_GDN_AB)]
    rwkv = [(pm, 4, 0), (pm, 4, 1), (pm, 4, 2), (pm, 4, 3), (pm, 1, 16), (pm, 1, 17)]
    gla = [(p, 4, G_GLA // 4 + k) for k in range(4)] + [(p, 1, G_GLA_AD)]
    return gdn, rwkv, gla


def _layer_fwd(x, q, nb, t, l, side=None):
    hb, p = _norm_proj(x, q["pre_w"], q["w_g"], f"norm_proj{l}")
    cq = _conv_fwd(p, G_GDN, 12, q["gdn_conv"], nb, t, f"gdn_conv{l}")
    pm = _mix_fwd(p, q["rwkv_mu"], nb, t, f"rwkv_mix{l}")
    gdn_in, rwkv_in, gla_in = _mixer_inputs(p, cq, pm)
    y_gdn, ck_gdn, _ = _mixer_fwd(_gdn_chunk, f"gdn_fwd{l}", gdn_in, q["gdn_prm"], q["gdn_cst"], nb, t)
    y_rwkv, ck_rwkv, side_res = _mixer_fwd(_rwkv_chunk, f"rwkv_fwd{l}", rwkv_in, q["rwkv_prm"], [], nb, t,
                                           first_fn=_rwkv_chunk_steps, side=side)
    y_sc = _sc_fwd(p, q["sc_conv"], nb, t, f"sc_fwd{l}")
    y_gla, ck_gla, _ = _mixer_fwd(_gla_chunk, f"gla_fwd{l}", gla_in, q["gla_prm"], [], nb, t)
    ys = (y_gdn, y_rwkv, y_sc, y_gla)
    out, xn = _out_proj_norm(ys, q["wout_g"], x, q["post_w"], f"out_proj{l}")
    saved = dict(x=x, hb=hb, p=p, cq=cq, pm=pm, ys=ys, out=out, ck=(ck_gdn, ck_rwkv, ck_gla))
    return xn, saved, side_res


def _layer_bwd(dxn, q, sv, nb, t, l, side=None):
    p, cq, pm, ys = sv["p"], sv["cq"], sv["pm"], sv["ys"]
    dout, dy, d_post = _post_bwd(dxn, sv["out"], q["post_w"], q["wout_g"], f"post_bwd{l}")
    d_wout = _dwout(ys, dout, f"dwout{l}").reshape(N_DEV, 128, D_MODEL)
    gdn_in, rwkv_in, gla_in = _mixer_inputs(p, cq, pm)
    ck_gdn, ck_rwkv, ck_gla = sv["ck"]
    g = {}

    (d_conv, dz, dab), (da_log, ddt, dnw), _ = _mixer_bwd(
        _gdn_chunk, f"gdn_bwd{l}", gdn_in, q["gdn_prm"], q["gdn_cst"], ck_gdn, dy, 0,
        [(12, F32), (4, BF16), (1, BF16)], [(0, 0), (0, 4), (0, 8), (1, 0), (2, 0)], nb, t)
    dconv_in, d_gconv = _conv_bwd(p, G_GDN, 12, q["gdn_conv"], d_conv, nb, t, f"gdn_conv_bwd{l}")
    g["gdn_conv_w"] = jnp.transpose(d_gconv.sum(1), (1, 0, 2)).reshape(GDN_TAPS, 768)
    g["gdn_a_log"] = da_log.sum((0, 2, 3))
    g["gdn_dt_bias"] = ddt.sum((0, 2, 3))
    g["gdn_norm_w"] = dnw.sum((0, 1, 2))

    (d_pm,), d_rprm, side_res = _mixer_bwd(
        _rwkv_chunk, f"rwkv_bwd{l}", rwkv_in, q["rwkv_prm"], [], ck_rwkv, dy, 1,
        [(18, F32)], [(0, 0), (0, 4), (0, 8), (0, 12), (0, 16), (0, 17)], nb, t, first_fn=_rwkv_chunk_steps,
        side=side)
    dp_rwkv, d_mu = _mix_bwd(p, q["rwkv_mu"], d_pm, nb, t, f"rwkv_mix_bwd{l}")
    g["rwkv_mu"] = d_mu.sum(1).reshape(1152)
    rp = [a.sum(0) for a in d_rprm]
    g["rwkv_w0"] = rp[0].reshape(256)
    g["rwkv_w_up"] = jnp.transpose(rp[1], (1, 0, 2)).reshape(64, 256)
    g["rwkv_a0"] = rp[2].reshape(256)
    g["rwkv_a_up"] = jnp.transpose(rp[3], (1, 0, 2)).reshape(64, 256)
    for i, nme in enumerate(("rwkv_k_k", "rwkv_k_a", "rwkv_r_k", "rwkv_ln_w", "rwkv_ln_b")):
        g[nme] = rp[4 + i].reshape(256)

    dp_sc, d_scw = _sc_bwd(p, q["sc_conv"], dy, nb, t, f"sc_bwd{l}")
    g["sc_conv_w"] = jnp.transpose(d_scw.sum(1), (1, 0, 2)).reshape(SC_TAPS, 256)

    (dp_gla, dad), (d_aup, d_ab, d_gnw), _ = _mixer_bwd(
        _gla_chunk, f"gla_bwd{l}", gla_in, q["gla_prm"], [], ck_gla, dy, 3,
        [(16, BF16), (1, BF16)], [(0, 0), (0, 4), (0, 8), (0, 12), (1, 0)], nb, t)
    g["gla_a_up"] = jnp.transpose(d_aup.sum(0)[:, :16, :GLA_HEAD_K], (1, 0, 2)).reshape(16, 128)
    g["gla_a_bias"] = d_ab.sum(0)[:, 0, :GLA_HEAD_K].reshape(128)
    g["gla_norm_w"] = d_gnw.sum((0, 1, 2))

    singles = jnp.concatenate([dab, dp_rwkv[16:18], dad], axis=0)
    sources = [dconv_in, dz, dp_rwkv, dp_sc, dp_gla, singles]
    dx, d_pre = _dh_prenorm_bwd(sources, q["w_g"], sv["x"], q["pre_w"], dxn, f"dh_bwd{l}")
    d_win = _regroup_out(_dwin(sv["hb"], sources, f"dwin{l}"), f"regroup_out{l}")
    g["pre_norm_w"] = d_pre.reshape(D_MODEL)
    g["post_norm_w"] = d_post.reshape(D_MODEL)
    return dx, g, d_win, d_wout, side_res


def _local_step(x, tgt, wts, tiny, w_in_all, w_out_all, later_shards=None):
    nb, t, d = x.shape
    xf = x.reshape(nb * t, d)
    overlap = later_shards is not None
    qs, saved = [], []
    big = _big_weights(w_in_all[0], w_out_all[0], 0)
    for l in range(DEPTH):
        q = dict(_layer_params(wts, tiny, l), **big)
        nxt = l + 1 < DEPTH
        side = _gather_plan(later_shards[l]) if overlap and nxt else None
        xf, sv, got = _layer_fwd(xf, q, nb, t, l, side)
        if nxt:
            big = _big_weights(*(got if overlap else (w_in_all[l + 1], w_out_all[l + 1])), l + 1)
        qs.append(q)
        saved.append(sv)
    dxf, lpart = _loss_grad(xf, tgt.reshape(nb * t, d), "loss")
    grads, d_win, d_wout = [None] * DEPTH, [None] * DEPTH, [None] * DEPTH
    for l in reversed(range(DEPTH)):
        side = _exchange_plan([d_win[l + 1], d_wout[l + 1]]) if overlap and l + 1 < DEPTH else None
        dxf, grads[l], d_win[l], d_wout[l], got = _layer_bwd(dxf, qs[l], saved[l], nb, t, l, side)
        if side is not None:
            d_win[l + 1], d_wout[l + 1] = got
    small = {k: jnp.stack([grads[l][k] for l in range(DEPTH)]) for k in grads[0]}
    return lpart[0, 0], dxf.reshape(nb, t, d), small, d_win, d_wout


_WEIGHTS = ("pre_norm_w", "w_in", "gdn_conv_w", "gdn_a_log", "gdn_dt_bias", "gdn_norm_w", "rwkv_mu", "rwkv_w0",
            "rwkv_w_up", "rwkv_a0", "rwkv_a_up", "rwkv_k_k", "rwkv_k_a", "rwkv_r_k", "rwkv_ln_w", "rwkv_ln_b",
            "sc_conv_w", "gla_a_up", "gla_a_bias", "gla_norm_w", "w_out", "post_norm_w")


def kernel(x, pre_norm_w, w_in, gdn_conv_w, gdn_a_log, gdn_dt_bias, gdn_norm_w, rwkv_mu, rwkv_w0, rwkv_w_up, rwkv_a0, rwkv_a_up, rwkv_k_k, rwkv_k_a, rwkv_r_k, rwkv_ln_w, rwkv_ln_b, sc_conv_w, gla_a_up, gla_a_bias, gla_norm_w, w_out, post_norm_w, loss_target, m_pre_norm_w, m_w_in, m_gdn_conv_w, m_gdn_a_log, m_gdn_dt_bias, m_gdn_norm_w, m_rwkv_mu, m_rwkv_w0, m_rwkv_w_up, m_rwkv_a0, m_rwkv_a_up, m_rwkv_k_k, m_rwkv_k_a, m_rwkv_r_k, m_rwkv_ln_w, m_rwkv_ln_b, m_sc_conv_w, m_gla_a_up, m_gla_a_bias, m_gla_norm_w, m_w_out, m_post_norm_w, v_pre_norm_w, v_w_in, v_gdn_conv_w, v_gdn_a_log, v_gdn_dt_bias, v_gdn_norm_w, v_rwkv_mu, v_rwkv_w0, v_rwkv_w_up, v_rwkv_a0, v_rwkv_a_up, v_rwkv_k_k, v_rwkv_k_a, v_rwkv_r_k, v_rwkv_ln_w, v_rwkv_ln_b, v_sc_conv_w, v_gla_a_up, v_gla_a_bias, v_gla_norm_w, v_w_out, v_post_norm_w):
    env = dict(locals())
    w = {n: env[n] for n in _WEIGHTS}
    m = {n: env["m_" + n] for n in _WEIGHTS}
    v = {n: env["v_" + n] for n in _WEIGHTS}
    tiny_names = [n for n, _ in _TINY]

    w_in_b, w_out_b = w_in.astype(BF16), w_out.astype(BF16)
    w_in_0, w_out_0, tiny_all = _run_comm(
        _gather_plan([w_in_b[0], w_out_b[0], _pack_rows([w[n] for n in tiny_names], _TINY_ROWS)]), "gather_weights")
    tiny = _unpack_rows(tiny_all, _TINY)

    lpart, grad_x, small, r_win, r_wout = _local_step(
        x, loss_target, w, tiny, [w_in_0], [w_out_0], later_shards=[(w_in_b[l], w_out_b[l]) for l in range(1, DEPTH)])

    tiny_send = jnp.stack([_pack_rows([_split_last(small[n][l])[d] for n in tiny_names for l in range(DEPTH)],
                                      _TINY_ROWS) for d in range(N_DEV)])
    r_win[0], r_wout[0], r_tiny = _run_comm(_exchange_plan([r_win[0], r_wout[0], tiny_send]), "scatter_grads")
    grads, delta, new_m, new_v = {}, {}, {}, {}
    for n, parts in (("w_in", r_win), ("w_out", r_wout)):
        res = [_sum_adamw(parts[l], w[n][l], m[n][l], v[n][l], f"adamw_{n}{l}") for l in range(DEPTH)]
        grads[n], delta[n], new_m[n], new_v[n] = [jnp.stack(o) for o in zip(*res)]
    tiny_sum = _sum_slots(r_tiny, "sum_tiny").reshape(-1)
    o = 0
    for n, s in _TINY:
        size = int(np.prod(s))
        grads[n] = tiny_sum[o:o + size].reshape(s)
        o += size

    small = dict(small)
    small["loss"] = lpart
    red = _unpack_rows(_all_reduce_small(_pack_rows([small[n] for n, _ in _SMALL], _SMALL_ROWS), "reduce_small"),
                       _SMALL)
    loss = red.pop("loss")
    grads.update(red)

    rest = [n for n in _WEIGHTS if n not in ("w_in", "w_out")]
    rest_shapes = [(n, w[n].shape) for n in rest]
    rows = -(-sum(int(np.prod(s)) for _, s in rest_shapes) // 1024) * 8
    outs = _adamw(_pack_rows([w[n] for n in rest], rows), _pack_rows([grads[n] for n in rest], rows),
                  _pack_rows([m[n] for n in rest], rows), _pack_rows([v[n] for n in rest], rows, 1.0), "adamw_rest")
    for dst, packed in zip((delta, new_m, new_v), outs):
        dst.update(_unpack_rows(packed, rest_shapes))

    return (loss, grad_x, *[grads[n] for n in _WEIGHTS], *[delta[n] for n in _WEIGHTS],
            *[new_m[n] for n in _WEIGHTS], *[new_v[n] for n in _WEIGHTS])
```

```python
import collections
import functools
import math

import numpy as np
import jax
import jax.numpy as jnp
from jax import lax
from jax.experimental import pallas as pl
from jax.experimental.pallas import tpu as pltpu

F32 = jnp.float32
BF16 = jnp.bfloat16

D_MODEL = 1024
DEPTH = 2
NH = 4
DH = 64
CH = 64
EPS = 1e-6
RWKV_GN_EPS = 64e-5
GLA_HEAD_K = 32
GLA_TAU = 16.0
GDN_TAPS = 4
SC_TAPS = 3
D_IN = 3992
N_DEV = 8
SHARD_COLS = D_IN // N_DEV

G_GDN = 0
G_RWKV = 16
G_SC = 32
G_GLA = 48
G_GDN_AB, G_RWKV_WD, G_RWKV_AD, G_GLA_AD = 64, 65, 66, 67
N_GROUPS = 68
GROUPS_PER_STEP = 4
TIME_BLOCK = 256
RWKV_EXACT_STEPS = 16

C_GDN, C_RWKV, C_SC, C_GLA = 0, 1032, 2184, 3208

ADAM_LR, ADAM_B1, ADAM_B2, ADAM_EPS, ADAM_WD, ADAM_STEP = 0.001, 0.9, 0.999, 1e-08, 0.01, 10

VMEM_LIMIT = 56 * 1024 * 1024
MESH = pl.DeviceIdType.MESH

_pcall = pl.pallas_call

_Comm = collections.namedtuple("_Comm", "operands out_shapes copies")


def _cparams(sem=None):
    if sem is None:
        return pltpu.CompilerParams(vmem_limit_bytes=VMEM_LIMIT)
    return pltpu.CompilerParams(dimension_semantics=sem, vmem_limit_bytes=VMEM_LIMIT)


def _group_segments():
    table = [(G_GDN + i, C_GDN + DH * i, DH) for i in range(16)]
    table.append((G_GDN_AB, C_GDN + 1024, 8))
    table += [(G_RWKV + i, C_RWKV + DH * i, DH) for i in range(16)]
    table += [(G_RWKV_WD, C_RWKV + 1024, DH), (G_RWKV_AD, C_RWKV + 1088, DH)]
    table += [(G_SC + 4 * j + k, C_SC + 256 * k + DH * j, DH) for j in range(NH) for k in range(4)]
    for h in range(NH):
        table += [(G_GLA + h, C_GLA + GLA_HEAD_K * h, GLA_HEAD_K),
                  (G_GLA + 4 + h, C_GLA + 128 + GLA_HEAD_K * h, GLA_HEAD_K),
                  (G_GLA + 8 + h, C_GLA + 256 + DH * h, DH),
                  (G_GLA + 12 + h, C_GLA + 512 + DH * h, DH)]
    table.append((G_GLA_AD, C_GLA + 768, 16))
    segs, padded = [], []
    for g, c, n in table:
        if n < DH:
            padded.append(g)
        a = 0
        while n > 0:
            d, off = divmod(c, SHARD_COLS)
            ln = min(n, SHARD_COLS - off)
            segs.append((g, a, d, off, ln))
            c, a, n = c + ln, a + ln, n - ln
    return segs, padded


_SEGMENTS, _PADDED_GROUPS = _group_segments()


def _dn(ta, tb):
    return (((1 if ta else 2,), (2 if tb else 1,)), ((0,), (0,)))


def _hdot(a, b, ta=False, tb=False):
    return lax.dot_general(a, b, _dn(ta, tb), precision=lax.Precision.HIGH, preferred_element_type=F32)


def _r(x):
    return x.astype(BF16)


def _rdot(a, b, ta=False, tb=False):
    return lax.dot_general(_r(a), _r(b), _dn(ta, tb), preferred_element_type=F32)


@jax.custom_vjp
def _bmm(a, b):
    return _rdot(a, b)


def _bmm_fwd(a, b):
    return _rdot(a, b), (a, b)


def _bmm_bwd(res, g):
    a, b = res
    return _rdot(g, b, tb=True), _rdot(a, g, ta=True)


_bmm.defvjp(_bmm_fwd, _bmm_bwd)


@jax.custom_vjp
def _bmm_nt(a, b):
    return _rdot(a, b, tb=True)


def _bmm_nt_fwd(a, b):
    return _rdot(a, b, tb=True), (a, b)


def _bmm_nt_bwd(res, g):
    a, b = res
    return _rdot(g, b), _rdot(g, a, ta=True)


_bmm_nt.defvjp(_bmm_nt_fwd, _bmm_nt_bwd)


@jax.custom_vjp
def _bmm_tn(a, b):
    return _rdot(a, b, ta=True)


def _bmm_tn_fwd(a, b):
    return _rdot(a, b, ta=True), (a, b)


def _bmm_tn_bwd(res, g):
    a, b = res
    return _rdot(b, g, tb=True), _rdot(a, g)


_bmm_tn.defvjp(_bmm_tn_fwd, _bmm_tn_bwd)


def _tri(n):
    i = lax.broadcasted_iota(jnp.int32, (n, n), 0)
    j = lax.broadcasted_iota(jnp.int32, (n, n), 1)
    return i >= j, i > j, i == j


def _heads_of(x, like):
    n = like.shape[0]
    if x.ndim == 2:
        return jnp.broadcast_to(x[None], (n,) + x.shape)
    seqs = x.shape[0]
    return jnp.broadcast_to(x[:, None], (seqs, n // seqs) + x.shape[1:]).reshape((n,) + x.shape[1:])


def _cumsum_rows(x):
    incl, _, _ = _tri(x.shape[-2])
    return _hdot(_heads_of(incl.astype(F32), x), x)


@jax.custom_vjp
def _inv_unit_lower(a):
    n = a.shape[-1]
    _, _, eye = _tri(n)
    pw = -a
    inv = eye.astype(F32) + pw
    for _ in range(math.ceil(math.log2(n)) - 1):
        pw = _hdot(pw, pw)
        inv = inv + _hdot(inv, pw)
    return inv


def _inv_unit_lower_fwd(a):
    inv = _inv_unit_lower(a)
    return inv, inv


def _inv_unit_lower_bwd(inv, g):
    return (-_hdot(_hdot(inv, g, ta=True), inv, tb=True),)


_inv_unit_lower.defvjp(_inv_unit_lower_fwd, _inv_unit_lower_bwd)


def _silu(x):
    return x * jax.nn.sigmoid(x)


def _t(x):
    return jnp.swapaxes(x, -1, -2)


def _gdn_chunk(prm, cst, ins, s):
    a_log, dt_b, nw = prm
    m_a, m_b = cst
    cq, ck, cv, z, ab = ins
    ab = _heads_of(ab, m_a)
    incl, strict, _ = _tri(CH)
    q = _silu(cq)
    k = _silu(ck)
    v = _silu(cv)
    q = q * lax.rsqrt(jnp.sum(q * q, -1, keepdims=True) + EPS) * (DH ** -0.5)
    k = k * lax.rsqrt(jnp.sum(k * k, -1, keepdims=True) + EPS)
    a_raw = jnp.sum(ab * m_a, -1, keepdims=True)
    b_raw = jnp.sum(ab * m_b, -1, keepdims=True)
    gstep = -jnp.exp(a_log) * jax.nn.softplus(a_raw + dt_b)
    beta = jax.nn.sigmoid(b_raw)
    gc = _cumsum_rows(gstep)
    gl = jnp.sum(gstep, -2, keepdims=True)
    dec = jnp.where(incl, jnp.exp(jnp.where(incl, gc - _t(gc), 0.0)), 0.0)
    kb = k * beta
    a_mat = jnp.where(strict, _bmm_nt(kb, k) * dec, 0.0)
    tinv = _inv_unit_lower(a_mat)
    eg = jnp.exp(gc)
    u = _hdot(tinv, v * beta)
    w = _hdot(tinv, kb * eg)
    attn = _bmm_nt(q, k) * dec
    v_new = u - _bmm(w, s)
    o = _bmm(q * eg, s) + _bmm(attn, v_new)
    s_next = s * jnp.exp(gl) + _bmm_tn(k * jnp.exp(gl - gc), v_new)
    on = o * lax.rsqrt(jnp.mean(o * o, -1, keepdims=True) + EPS) * nw
    return on * _silu(z), s_next


def _gla_chunk(prm, cst, ins, st):
    a_up, a_bias, nw = prm
    q, k, v, z, ad = ins
    incl, _, _ = _tri(CH)
    la = jax.nn.log_sigmoid(_bmm(_heads_of(ad, a_up), a_up) + a_bias) * (1.0 / GLA_TAU)
    bc = _cumsum_rows(la)
    bl = jnp.sum(la, -2, keepdims=True)
    qe = q * (GLA_HEAD_K ** -0.5) * jnp.exp(bc)
    ke = k * jnp.exp(-bc)
    attn = jnp.where(incl, _bmm_nt(qe, ke), 0.0)
    o = _bmm_nt(qe, st) + _bmm(attn, v)
    st_next = st * jnp.exp(bl) + _bmm_tn(v, k * jnp.exp(bl - bc))
    on = o * lax.rsqrt(jnp.mean(o * o, -1, keepdims=True) + EPS) * nw
    return on * _silu(z), st_next


def _rwkv_chunk(prm, cst, ins, s):
    r, v = ins[0], ins[2]
    incl, strict, _ = _tri(r.shape[-2])
    lw, kk, k2, m = _rwkv_pre(prm, ins)
    cum = _cumsum_rows(lw)
    ltot = jnp.sum(lw, -2, keepdims=True)
    n_t = -kk * jnp.exp(cum - lw)
    einv = jnp.exp(-cum)
    m_t = m * einv
    k_t = k2 * einv
    r_t = r * jnp.exp(cum)
    a_nm = jnp.where(strict, _hdot(n_t, m_t, tb=True), 0.0)
    a_nk = jnp.where(strict, _hdot(n_t, k_t, tb=True), 0.0)
    cm = _hdot(_inv_unit_lower(-a_nm), _hdot(n_t, s, tb=True) + _hdot(a_nk, v))
    y = (_hdot(r_t, s, tb=True) + _hdot(jnp.where(incl, _hdot(r_t, m_t, tb=True), 0.0), cm)
         + _hdot(jnp.where(incl, _hdot(r_t, k_t, tb=True), 0.0), v))
    eend = jnp.exp(ltot - cum)
    s_next = s * jnp.exp(ltot) + _hdot(cm, m * eend, ta=True) + _hdot(v, k2 * eend, ta=True)
    return _rwkv_post(prm, ins, y, k2), s_next


def _rwkv_pre(prm, ins):
    w0, w_up, a0, a_up, k_k, k_a = prm[:6]
    k, wd, ad = ins[1], ins[4], ins[5]
    lw = -math.exp(-0.5) * jax.nn.sigmoid(w0 + _bmm(_heads_of(jnp.tanh(wd), w_up), w_up))
    a = jax.nn.sigmoid(a0 + _bmm(_heads_of(ad, a_up), a_up))
    kk = k * k_k
    kk = kk * lax.rsqrt(jnp.sum(kk * kk, -1, keepdims=True) + EPS)
    k2 = k * (1.0 + (a - 1.0) * k_a)
    return lw, kk, k2, kk * a


def _rwkv_post(prm, ins, y, k2):
    r_k, ln_w, ln_b = prm[6:]
    r, v, z = ins[0], ins[2], ins[3]
    mean = jnp.mean(y, -1, keepdims=True)
    yc = y - mean
    var = jnp.mean(yc * yc, -1, keepdims=True)
    yn = yc * lax.rsqrt(var + RWKV_GN_EPS) * ln_w + ln_b
    bonus = jnp.sum(r * k2 * r_k, -1, keepdims=True) * v
    return (yn + bonus) * _silu(z)


@jax.custom_vjp
def _bmv(s, x):
    return jnp.sum(_r(s).astype(F32) * _r(x).astype(F32), -1, keepdims=True)


def _bmv_fwd(s, x):
    return _bmv(s, x), (s, x)


def _bmv_bwd(res, g):
    s, x = res
    return g * x, jnp.sum(_r(s).astype(F32) * _r(g).astype(F32), -2, keepdims=True)


_bmv.defvjp(_bmv_fwd, _bmv_bwd)


def _rwkv_steps(prm, cst, ins, s, steps):
    r, v = ins[0], ins[2]
    lw, kk, k2, m = _rwkv_pre(prm, ins)
    w = jnp.exp(lw)
    v_t = _t(v)
    lane = lax.broadcasted_iota(jnp.int32, (1, 1, CH), 2)
    y_t = jnp.zeros((s.shape[0], DH, CH), F32)
    for t in range(steps):
        e_t = (lane == t).astype(F32)
        row = (slice(None), slice(t, t + 1))
        sa = _bmv(s, -kk[row])
        s = s * w[row] + sa * m[row] + jnp.sum(v_t * e_t, -1, keepdims=True) * k2[row]
        y_t = y_t + _bmv(s, r[row]) * e_t
    return _rwkv_post(prm, ins, _t(y_t), k2)[:, :steps], s


def _rwkv_first_chunk(prm, cst, ins, s):
    k = RWKV_EXACT_STEPS
    y_head, s = _rwkv_steps(prm, cst, ins, s, k)
    y_tail, s = _rwkv_chunk(prm, cst, [x[..., k:, :] for x in ins], s)
    return jnp.concatenate([y_head, y_tail], axis=-2), s


def _time_block(t):
    return TIME_BLOCK if t % TIME_BLOCK == 0 else t


def _load_chunk(ref, i):
    nb = ref.shape[1]
    if ref.shape[0] == NH:
        return jnp.concatenate([ref[:, b, pl.ds(i, CH), :] for b in range(nb)], axis=0)
    return ref[0, :, pl.ds(i, CH), :]


def _load_chunk_of(ref, c, i):
    if ref.shape[0] == NH:
        return ref[pl.ds(c % NH, 1), c // NH, pl.ds(i, CH), :]
    return ref[0, pl.ds(c // NH, 1), pl.ds(i, CH), :]


def _each_chain(n, fn):
    def step(c, carry):
        fn(c)
        return carry

    lax.fori_loop(0, n, step, 0)


def _mixer_fwd(chunk_fn, name, ins, prm, cst, nb, t, first_fn=None, side=None):
    tb = _time_block(t)
    nt, ncb, nch = t // tb, tb // CH, nb * NH
    n_in, n_prm, n_cst = len(ins), len(prm), len(cst)
    n_main, n_side = n_in + n_prm + n_cst, len(side.operands) if side else 0

    def body(*refs):
        in_refs = refs[:n_in]
        prm_refs = refs[n_in:n_in + n_prm]
        cst_refs = refs[n_in + n_prm:n_main]
        side_in = refs[n_main:n_main + n_side]
        y_ref, ck_ref = refs[n_main + n_side:n_main + n_side + 2]
        side_out = refs[n_main + n_side + 2:n_main + 2 * n_side + 2]
        s_scr = refs[n_main + 2 * n_side + 2]
        sems = refs[n_main + 2 * n_side + 3:]
        step_t = pl.program_id(0)

        if side is not None:
            @pl.when(step_t == 0)
            def _():
                _comm_start(side.copies(side_in, side_out, sems))

        @pl.when(step_t == 0)
        def _():
            s_scr[...] = jnp.zeros_like(s_scr)

        def chunk(c, i):
            s = s_scr[...]
            ck_ref[c] = s
            y, s_next = chunk_fn([jnp.tile(r[...], (nb, 1, 1)) for r in prm_refs],
                                 [jnp.tile(r[...], (nb, 1, 1)) for r in cst_refs],
                                 [_load_chunk(r, i) for r in in_refs], s)
            for b in range(nb):
                y_ref[:, b, pl.ds(i, CH), :] = y[b * NH:(b + 1) * NH].astype(BF16)
            s_scr[...] = s_next

        def first_chunk_of(c):
            one, h = pl.ds(c, 1), pl.ds(c % NH, 1)
            s = s_scr[one]
            ck_ref[0, one] = s
            y, s_next = first_fn([r[h] for r in prm_refs], [r[h] for r in cst_refs],
                                 [_load_chunk_of(r, c, 0) for r in in_refs], s)
            y_ref[h, c // NH, pl.ds(0, CH), :] = y.astype(BF16)
            s_scr[one] = s_next

        def step(c, carry):
            chunk(c, pl.multiple_of(c * CH, CH))
            return carry

        if first_fn is None:
            lax.fori_loop(0, ncb, step, 0)
        else:
            @pl.when(step_t == 0)
            def _():
                _each_chain(nch, first_chunk_of)

            @pl.when(step_t != 0)
            def _():
                chunk(0, 0)

            lax.fori_loop(1, ncb, step, 0)

        if side is not None:
            @pl.when(step_t == nt - 1)
            def _():
                _comm_wait(side.copies(side_in, side_out, sems))

    hbm = pl.BlockSpec(memory_space=pl.ANY)
    in_specs = [pl.BlockSpec((ng, nb, tb, DH), (lambda j, bi=bi: (bi, 0, j, 0))) for _, ng, bi in ins]
    in_specs += [pl.BlockSpec(p.shape, lambda j: (0, 0, 0)) for p in list(prm) + list(cst)]
    y, ck, *side_res = _pcall(
        body, name=name, grid=(nt,),
        in_specs=in_specs + [hbm] * n_side,
        out_specs=[pl.BlockSpec((NH, nb, tb, DH), lambda j: (0, 0, j, 0)),
                   pl.BlockSpec((ncb, nch, DH, DH), lambda j: (j, 0, 0, 0))] + [hbm] * n_side,
        out_shape=[jax.ShapeDtypeStruct((NH, nb, t, DH), BF16),
                   jax.ShapeDtypeStruct((t // CH, nch, DH, DH), F32)] + (side.out_shapes if side else []),
        scratch_shapes=[pltpu.VMEM((nch, DH, DH), F32)] + (_comm_scratch(side) if side else []),
        compiler_params=_cparams(("arbitrary",)),
    )(*[a.reshape(a.shape[0], nb, t, DH) for a, _, _ in ins], *prm, *cst, *(side.operands if side else []))
    return y.reshape(NH, nb * t, DH), ck, side_res


def _mixer_bwd(chunk_fn, name, ins, prm, cst, ck, dy, dy_block, outs, routes, nb, t, first_fn=None, side=None):
    tb = _time_block(t)
    nt, ncb, nch = t // tb, tb // CH, nb * NH
    n_in, n_prm, n_cst, n_out = len(ins), len(prm), len(cst), len(outs)
    n_main, n_side = n_in + n_prm + n_cst + 2, len(side.operands) if side else 0

    def body(*refs):
        in_refs = refs[:n_in]
        prm_refs = refs[n_in:n_in + n_prm]
        cst_refs = refs[n_in + n_prm:n_in + n_prm + n_cst]
        ck_ref, dy_ref = refs[n_main - 2:n_main]
        side_in = refs[n_main:n_main + n_side]
        rest = refs[n_main + n_side:]
        out_refs = rest[:n_out]
        dprm_refs = rest[n_out:n_out + n_prm]
        side_out = rest[n_out + n_prm:n_out + n_prm + n_side]
        ds_scr = rest[n_out + n_prm + n_side]
        sems = rest[n_out + n_prm + n_side + 1:]
        step_t = pl.program_id(0)

        if side is not None:
            @pl.when(step_t == 0)
            def _():
                _comm_start(side.copies(side_in, side_out, sems))

        @pl.when(step_t == 0)
        def _():
            ds_scr[...] = jnp.zeros_like(ds_scr)
            for r in dprm_refs:
                r[...] = jnp.zeros_like(r)

        def chunk(c, i):
            cst_v = [jnp.tile(r[...], (nb, 1, 1)) for r in cst_refs]
            _, vjp = jax.vjp(lambda p, x, s: chunk_fn(p, cst_v, x, s),
                             [jnp.tile(r[...], (nb, 1, 1)) for r in prm_refs],
                             [_load_chunk(r, i) for r in in_refs], ck_ref[c])
            dy_c = jnp.concatenate([dy_ref[:, b, pl.ds(i, CH), :] for b in range(nb)], axis=0)
            d_prm, d_ins, d_s = vjp((dy_c, ds_scr[...]))
            for (oi, g0), r, g in zip(routes, in_refs, d_ins):
                o_ref = out_refs[oi]
                if r.shape[0] == NH:
                    for b in range(nb):
                        o_ref[g0:g0 + NH, b, pl.ds(i, CH), :] = g[b * NH:(b + 1) * NH].astype(o_ref.dtype)
                else:
                    o_ref[g0, :, pl.ds(i, CH), :] = g.astype(o_ref.dtype)
            for r, g in zip(dprm_refs, d_prm):
                r[...] += g
            ds_scr[...] = d_s

        def first_chunk_of(c):
            one, h, b = pl.ds(c, 1), pl.ds(c % NH, 1), c // NH
            cst_v = [r[h] for r in cst_refs]
            _, vjp = jax.vjp(lambda p, x, s: first_fn(p, cst_v, x, s), [r[h] for r in prm_refs],
                             [_load_chunk_of(r, c, 0) for r in in_refs], ck_ref[0, one])
            d_prm, d_ins, d_s = vjp((dy_ref[h, b, pl.ds(0, CH), :], ds_scr[one]))
            for (oi, g0), r, g in zip(routes, in_refs, d_ins):
                o_ref = out_refs[oi]
                if r.shape[0] == NH:
                    o_ref.at[g0:g0 + NH][h, b, pl.ds(0, CH), :] = g.astype(o_ref.dtype)
                else:
                    o_ref[g0, pl.ds(b, 1), pl.ds(0, CH), :] += g.astype(o_ref.dtype)
            for r, g in zip(dprm_refs, d_prm):
                r[one] += g
            ds_scr[one] = d_s

        def first_chunk():
            for (oi, g0), r in zip(routes, in_refs):
                if r.shape[0] != NH:
                    out_refs[oi][g0, :, pl.ds(0, CH), :] = jnp.zeros((nb, CH, DH), out_refs[oi].dtype)
            _each_chain(nch, first_chunk_of)

        def step(j, carry):
            c = ncb - 1 - j
            chunk(c, pl.multiple_of(c * CH, CH))
            return carry

        lax.fori_loop(0, ncb - 1, step, 0)
        if first_fn is None:
            chunk(0, 0)
        else:
            @pl.when(step_t == nt - 1)
            def _():
                first_chunk()

            @pl.when(step_t != nt - 1)
            def _():
                chunk(0, 0)

        if side is not None:
            @pl.when(step_t == nt - 1)
            def _():
                _comm_wait(side.copies(side_in, side_out, sems))

    def back(j):
        return nt - 1 - j

    hbm = pl.BlockSpec(memory_space=pl.ANY)
    in_specs = [pl.BlockSpec((ng, nb, tb, DH), (lambda j, bi=bi: (bi, 0, back(j), 0))) for _, ng, bi in ins]
    in_specs += [pl.BlockSpec(p.shape, lambda j: (0, 0, 0)) for p in list(prm) + list(cst)]
    in_specs += [pl.BlockSpec((ncb, nch, DH, DH), lambda j: (back(j), 0, 0, 0)),
                 pl.BlockSpec((NH, nb, tb, DH), lambda j: (dy_block, 0, back(j), 0))]
    out_specs = [pl.BlockSpec((ng, nb, tb, DH), lambda j: (0, 0, back(j), 0)) for ng, _ in outs]
    out_specs += [pl.BlockSpec((nch,) + p.shape[1:], lambda j: (0, 0, 0)) for p in prm]
    out_shape = [jax.ShapeDtypeStruct((ng, nb, t, DH), dt) for ng, dt in outs]
    out_shape += [jax.ShapeDtypeStruct((nch,) + p.shape[1:], F32) for p in prm]
    res = _pcall(
        body, name=name, grid=(nt,),
        in_specs=in_specs + [hbm] * n_side, out_specs=out_specs + [hbm] * n_side,
        out_shape=out_shape + (side.out_shapes if side else []),
        scratch_shapes=[pltpu.VMEM((nch, DH, DH), F32)] + (_comm_scratch(side) if side else []),
        compiler_params=_cparams(("arbitrary",)),
    )(*[a.reshape(a.shape[0], nb, t, DH) for a, _, _ in ins], *prm, *cst, ck, dy.reshape(dy.shape[0], nb, t, DH),
      *(side.operands if side else []))
    d_outs = [o.reshape(o.shape[0], nb * t, DH) for o in res[:n_out]]
    d_prm = [g.reshape((nb,) + p.shape) for g, p in zip(res[n_out:n_out + n_prm], prm)]
    return d_outs, d_prm, res[n_out + n_prm:]


def _shift_down(x, s):
    if s == 0:
        return x
    row = lax.broadcasted_iota(jnp.int32, x.shape, 0)
    return jnp.where(row < s, 0.0, pltpu.roll(x, s, 0))


def _shift_up(x, s):
    if s == 0:
        return x
    t = x.shape[0]
    row = lax.broadcasted_iota(jnp.int32, x.shape, 0)
    return jnp.where(row >= t - s, 0.0, pltpu.roll(x, t - s, 0))


def _conv_fwd(p, g0, ng, w, nb, t, name):
    taps = w.shape[1]

    def body(x_ref, w_ref, y_ref):
        x = x_ref[...]
        acc = w_ref[taps - 1:taps, :] * x
        for i in range(taps - 1):
            acc = acc + w_ref[i:i + 1, :] * _shift_down(x, taps - 1 - i)
        y_ref[...] = acc

    return _pcall(
        body, name=name, grid=(ng, nb),
        in_specs=[pl.BlockSpec((None, t, DH), lambda g, b: (g0 + g, b, 0)),
                  pl.BlockSpec((None, taps, DH), lambda g, b: (g, 0, 0))],
        out_specs=pl.BlockSpec((None, t, DH), lambda g, b: (g, b, 0)),
        out_shape=jax.ShapeDtypeStruct((ng, nb * t, DH), F32),
        compiler_params=_cparams(("parallel", "parallel")),
    )(p, w)


def _conv_bwd(p, g0, ng, w, dy, nb, t, name):
    taps = w.shape[1]

    def body(x_ref, w_ref, dy_ref, dx_ref, dw_ref):
        x = x_ref[...]
        d = dy_ref[...]
        acc = w_ref[taps - 1:taps, :] * d
        dw_ref[taps - 1:taps, :] = jnp.sum(d * x, 0, keepdims=True)
        for i in range(taps - 1):
            s = taps - 1 - i
            acc = acc + w_ref[i:i + 1, :] * _shift_up(d, s)
            dw_ref[i:i + 1, :] = jnp.sum(d * _shift_down(x, s), 0, keepdims=True)
        dx_ref[...] = acc.astype(BF16)

    return _pcall(
        body, name=name, grid=(ng, nb),
        in_specs=[pl.BlockSpec((None, t, DH), lambda g, b: (g0 + g, b, 0)),
                  pl.BlockSpec((None, taps, DH), lambda g, b: (g, 0, 0)),
                  pl.BlockSpec((None, t, DH), lambda g, b: (g, b, 0))],
        out_specs=[pl.BlockSpec((None, t, DH), lambda g, b: (g, b, 0)),
                   pl.BlockSpec((None, None, taps, DH), lambda g, b: (g, b, 0, 0))],
        out_shape=[jax.ShapeDtypeStruct((ng, nb * t, DH), BF16),
                   jax.ShapeDtypeStruct((ng, nb, taps, DH), F32)],
        compiler_params=_cparams(("parallel", "parallel")),
    )(p, w, dy)


def _mix_group(g):
    return jnp.where(g < 16, G_RWKV + g, G_RWKV_WD + g - 16)


def _mix_fwd(p, mu, nb, t, name):
    def body(x_ref, mu_ref, y_ref):
        x = x_ref[...]
        y_ref[...] = x + mu_ref[...] * (_shift_down(x, 1) - x)

    return _pcall(
        body, name=name, grid=(18, nb),
        in_specs=[pl.BlockSpec((None, t, DH), lambda g, b: (_mix_group(g), b, 0)),
                  pl.BlockSpec((None, 1, DH), lambda g, b: (g, 0, 0))],
        out_specs=pl.BlockSpec((None, t, DH), lambda g, b: (g, b, 0)),
        out_shape=jax.ShapeDtypeStruct((18, nb * t, DH), F32),
        compiler_params=_cparams(("parallel", "parallel")),
    )(p, mu)


def _mix_bwd(p, mu, dy, nb, t, name):
    def body(x_ref, mu_ref, dy_ref, dx_ref, dmu_ref):
        x = x_ref[...]
        muv = mu_ref[...]
        d = dy_ref[...]
        dx_ref[...] = (d * (1.0 - muv) + _shift_up(d * muv, 1)).astype(BF16)
        dmu_ref[...] = jnp.sum(d * (_shift_down(x, 1) - x), 0, keepdims=True)

    return _pcall(
        body, name=name, grid=(18, nb),
        in_specs=[pl.BlockSpec((None, t, DH), lambda g, b: (_mix_group(g), b, 0)),
                  pl.BlockSpec((None, 1, DH), lambda g, b: (g, 0, 0)),
                  pl.BlockSpec((None, t, DH), lambda g, b: (g, b, 0))],
        out_specs=[pl.BlockSpec((None, t, DH), lambda g, b: (g, b, 0)),
                   pl.BlockSpec((None, None, 1, DH), lambda g, b: (g, b, 0, 0))],
        out_shape=[jax.ShapeDtypeStruct((18, nb * t, DH), BF16),
                   jax.ShapeDtypeStruct((18, nb, 1, DH), F32)],
        compiler_params=_cparams(("parallel", "parallel")),
    )(p, mu, dy)


def _sc_fwd(p, w, nb, t, name):
    def body(p_ref, w_ref, y_ref):
        u = p_ref[1] * p_ref[2]
        conv = w_ref[2:3, :] * u + w_ref[1:2, :] * _shift_down(u, 1) + w_ref[0:1, :] * _shift_down(u, 2)
        y_ref[...] = (p_ref[0] * conv * _silu(p_ref[3])).astype(BF16)

    return _pcall(
        body, name=name, grid=(NH, nb),
        in_specs=[pl.BlockSpec((4, t, DH), lambda j, b: (G_SC // 4 + j, b, 0)),
                  pl.BlockSpec((None, SC_TAPS, DH), lambda j, b: (j, 0, 0))],
        out_specs=pl.BlockSpec((None, t, DH), lambda j, b: (j, b, 0)),
        out_shape=jax.ShapeDtypeStruct((NH, nb * t, DH), BF16),
        compiler_params=_cparams(("parallel", "parallel")),
    )(p, w)


def _sc_bwd(p, w, dy, nb, t, name):
    def body(p_ref, w_ref, dy_ref, dp_ref, dw_ref):
        bg, cg, xg, z = p_ref[0], p_ref[1], p_ref[2], p_ref[3]
        d = dy_ref[...]
        u = cg * xg
        u1 = _shift_down(u, 1)
        u2 = _shift_down(u, 2)
        conv = w_ref[2:3, :] * u + w_ref[1:2, :] * u1 + w_ref[0:1, :] * u2
        sg = jax.nn.sigmoid(z)
        sz = z * sg
        dp_ref[0] = (d * conv * sz).astype(BF16)
        dp_ref[3] = (d * bg * conv * (sg * (1.0 + z * (1.0 - sg)))).astype(BF16)
        dconv = d * bg * sz
        du = w_ref[2:3, :] * dconv + w_ref[1:2, :] * _shift_up(dconv, 1) + w_ref[0:1, :] * _shift_up(dconv, 2)
        dp_ref[1] = (du * xg).astype(BF16)
        dp_ref[2] = (du * cg).astype(BF16)
        dw_ref[2:3, :] = jnp.sum(dconv * u, 0, keepdims=True)
        dw_ref[1:2, :] = jnp.sum(dconv * u1, 0, keepdims=True)
        dw_ref[0:1, :] = jnp.sum(dconv * u2, 0, keepdims=True)

    return _pcall(
        body, name=name, grid=(NH, nb),
        in_specs=[pl.BlockSpec((4, t, DH), lambda j, b: (G_SC // 4 + j, b, 0)),
                  pl.BlockSpec((None, SC_TAPS, DH), lambda j, b: (j, 0, 0)),
                  pl.BlockSpec((None, t, DH), lambda j, b: (8 + j, b, 0))],
        out_specs=[pl.BlockSpec((4, t, DH), lambda j, b: (j, b, 0)),
                   pl.BlockSpec((None, None, SC_TAPS, DH), lambda j, b: (j, b, 0, 0))],
        out_shape=[jax.ShapeDtypeStruct((4 * NH, nb * t, DH), BF16),
                   jax.ShapeDtypeStruct((NH, nb, SC_TAPS, DH), F32)],
        compiler_params=_cparams(("parallel", "parallel")),
    )(p, w, dy)


def _row_tile(n):
    return 1024 if n % 1024 == 0 else n


def _regroup_in(w_all, name):
    tr = 256
    gs = GROUPS_PER_STEP

    def body(w_ref, o_ref):
        for g in _PADDED_GROUPS:
            o_ref[g // gs, :, DH * (g % gs):DH * (g % gs + 1)] = jnp.zeros((tr, DH), BF16)
        for g, a, d, off, ln in _SEGMENTS:
            lane = DH * (g % gs) + a
            o_ref[g // gs, :, lane:lane + ln] = w_ref[d, :, off:off + ln].astype(BF16)

    return _pcall(
        body, name=name, grid=(D_MODEL // tr,),
        in_specs=[pl.BlockSpec((N_DEV, tr, SHARD_COLS), lambda i: (0, i, 0))],
        out_specs=pl.BlockSpec((N_GROUPS // gs, tr, gs * DH), lambda i: (0, i, 0)),
        out_shape=jax.ShapeDtypeStruct((N_GROUPS // gs, D_MODEL, gs * DH), BF16),
        compiler_params=_cparams(("parallel",)),
    )(w_all)


def _regroup_out(dwg, name):
    tr = 256
    gs = GROUPS_PER_STEP

    def body(g_ref, o_ref):
        for g, a, d, off, ln in _SEGMENTS:
            lane = DH * (g % gs) + a
            o_ref[d, :, off:off + ln] = g_ref[g // gs, :, lane:lane + ln]

    return _pcall(
        body, name=name, grid=(D_MODEL // tr,),
        in_specs=[pl.BlockSpec((N_GROUPS // gs, tr, gs * DH), lambda i: (0, i, 0))],
        out_specs=pl.BlockSpec((N_DEV, tr, SHARD_COLS), lambda i: (0, i, 0)),
        out_shape=jax.ShapeDtypeStruct((N_DEV, D_MODEL, SHARD_COLS), F32),
        compiler_params=_cparams(("parallel",)),
    )(dwg)


def _norm_proj(x, pre_w, w_g, name):
    n = x.shape[0]
    tm = _row_tile(n)
    gs = GROUPS_PER_STEP

    def body(x_ref, pw_ref, w_ref, h_ref, p_ref):
        @pl.when(pl.program_id(1) == 0)
        def _():
            xv = x_ref[...]
            h = xv * lax.rsqrt(jnp.mean(xv * xv, -1, keepdims=True) + EPS) * pw_ref[...]
            h_ref[...] = h.astype(BF16)

        r = jnp.dot(h_ref[...], w_ref[...], preferred_element_type=F32)
        for k in range(gs):
            p_ref[k] = r[:, DH * k:DH * (k + 1)]

    return _pcall(
        body, name=name, grid=(n // tm, N_GROUPS // gs),
        in_specs=[pl.BlockSpec((tm, D_MODEL), lambda i, j: (i, 0)),
                  pl.BlockSpec((1, D_MODEL), lambda i, j: (0, 0)),
                  pl.BlockSpec((None, D_MODEL, gs * DH), lambda i, j: (j, 0, 0))],
        out_specs=[pl.BlockSpec((tm, D_MODEL), lambda i, j: (i, 0)),
                   pl.BlockSpec((gs, tm, DH), lambda i, j: (j, i, 0))],
        out_shape=[jax.ShapeDtypeStruct((n, D_MODEL), BF16),
                   jax.ShapeDtypeStruct((N_GROUPS, n, DH), F32)],
        compiler_params=_cparams(("parallel", "arbitrary")),
    )(x, pre_w, w_g)


def _out_proj_norm(ys, wout_g, x, post_w, name):
    n = x.shape[0]
    tm = _row_tile(n)

    def body(y0, y1, y2, y3, w_ref, x_ref, pw_ref, out_ref, xn_ref):
        acc = jnp.zeros((tm, D_MODEL), F32)
        for m, yr in enumerate((y0, y1, y2, y3)):
            for h in range(NH):
                acc = acc + jnp.dot(yr[h], w_ref[m * NH + h], preferred_element_type=F32)
        out_ref[...] = acc
        xn_ref[...] = x_ref[...] + acc * lax.rsqrt(jnp.mean(acc * acc, -1, keepdims=True) + EPS) * pw_ref[...]

    yspec = pl.BlockSpec((NH, tm, DH), lambda i: (0, i, 0))
    rows = pl.BlockSpec((tm, D_MODEL), lambda i: (i, 0))
    return _pcall(
        body, name=name, grid=(n // tm,),
        in_specs=[yspec] * 4 + [pl.BlockSpec((4 * NH, DH, D_MODEL), lambda i: (0, 0, 0)), rows,
                                pl.BlockSpec((1, D_MODEL), lambda i: (0, 0))],
        out_specs=[rows, rows],
        out_shape=[jax.ShapeDtypeStruct((n, D_MODEL), F32)] * 2,
        compiler_params=_cparams(("parallel",)),
    )(*ys, wout_g, x, post_w)


def _loss_grad(x, tgt, name):
    n = x.shape[0]
    tm = _row_tile(n)

    def body(x_ref, t_ref, dx_ref, l_ref):
        @pl.when(pl.program_id(0) == 0)
        def _():
            l_ref[...] = jnp.zeros_like(l_ref)

        e = x_ref[...] - t_ref[...]
        dx_ref[...] = e * (1.0 / D_MODEL)
        l_ref[...] += jnp.sum(jnp.sum(e * e, -1, keepdims=True), 0, keepdims=True) * (0.5 / D_MODEL)

    rows = pl.BlockSpec((tm, D_MODEL), lambda i: (i, 0))
    return _pcall(
        body, name=name, grid=(n // tm,),
        in_specs=[rows, rows],
        out_specs=[rows, pl.BlockSpec((1, 128), lambda i: (0, 0))],
        out_shape=[jax.ShapeDtypeStruct((n, D_MODEL), F32), jax.ShapeDtypeStruct((1, 128), F32)],
        compiler_params=_cparams(("arbitrary",)),
    )(x, tgt)


def _rmsnorm_bwd(xv, w, d):
    r = lax.rsqrt(jnp.mean(xv * xv, -1, keepdims=True) + EPS)
    xh = xv * r
    dxh = d * w
    dx = r * (dxh - xh * jnp.mean(dxh * xh, -1, keepdims=True))
    return dx, d * xh


def _post_bwd(dxn, out, post_w, wout_g, name):
    n = dxn.shape[0]
    tm = _row_tile(n)

    def body(d_ref, o_ref, pw_ref, w_ref, do_ref, dy_ref, dpw_ref):
        @pl.when(pl.program_id(0) == 0)
        def _():
            dpw_ref[...] = jnp.zeros_like(dpw_ref)

        dout, dw_rows = _rmsnorm_bwd(o_ref[...], pw_ref[...], d_ref[...])
        dpw_ref[...] += jnp.sum(dw_rows, 0, keepdims=True)
        db = dout.astype(BF16)
        do_ref[...] = db
        for g in range(4 * NH):
            dy_ref[g] = lax.dot_general(db, w_ref[g], (((1,), (1,)), ((), ())), preferred_element_type=F32)

    rows = pl.BlockSpec((tm, D_MODEL), lambda i: (i, 0))
    vec = pl.BlockSpec((1, D_MODEL), lambda i: (0, 0))
    return _pcall(
        body, name=name, grid=(n // tm,),
        in_specs=[rows, rows, vec, pl.BlockSpec((4 * NH, DH, D_MODEL), lambda i: (0, 0, 0))],
        out_specs=[rows, pl.BlockSpec((4 * NH, tm, DH), lambda i: (0, i, 0)), vec],
        out_shape=[jax.ShapeDtypeStruct((n, D_MODEL), BF16),
                   jax.ShapeDtypeStruct((4 * NH, n, DH), F32),
                   jax.ShapeDtypeStruct((1, D_MODEL), F32)],
        compiler_params=_cparams(("arbitrary",)),
    )(dxn, out, post_w, wout_g)


def _dwout(ys, dout, name):
    n = dout.shape[0]
    tm = _row_tile(n)

    def body(y0, y1, y2, y3, d_ref, dw_ref):
        @pl.when(pl.program_id(0) == 0)
        def _():
            dw_ref[...] = jnp.zeros_like(dw_ref)

        d = d_ref[...]
        for m, yr in enumerate((y0, y1, y2, y3)):
            for h in range(NH):
                dw_ref[m * NH + h] += lax.dot_general(yr[h], d, (((0,), (0,)), ((), ())),
                                                      preferred_element_type=F32)

    yspec = pl.BlockSpec((NH, tm, DH), lambda i: (0, i, 0))
    return _pcall(
        body, name=name, grid=(n // tm,),
        in_specs=[yspec] * 4 + [pl.BlockSpec((tm, D_MODEL), lambda i: (i, 0))],
        out_specs=pl.BlockSpec((4 * NH, DH, D_MODEL), lambda i: (0, 0, 0)),
        out_shape=jax.ShapeDtypeStruct((4 * NH, DH, D_MODEL), F32),
        compiler_params=_cparams(("arbitrary",)),
    )(*ys, dout)


def _source_specs(sources, rows_first):
    gs = GROUPS_PER_STEP
    spans, specs, j0 = [], [], 0
    for a in sources:
        nblk = a.shape[0] // gs
        spans.append((j0, j0 + nblk))
        shape = (gs, _row_tile(a.shape[1]), DH)

        def blk(j, j0=j0, nblk=nblk):
            return jnp.clip(j - j0, 0, nblk - 1)

        if rows_first:
            specs.append(pl.BlockSpec(shape, (lambda i, j, blk=blk: (blk(j), i, 0))))
        else:
            specs.append(pl.BlockSpec(shape, (lambda j, i, blk=blk: (blk(j), i, 0))))
        j0 += nblk
    return spans, specs


def _dh_prenorm_bwd(sources, w_g, x, pre_w, dxn, name):
    n = x.shape[0]
    tm = _row_tile(n)
    gs = GROUPS_PER_STEP
    nj = N_GROUPS // gs
    spans, src_specs = _source_specs(sources, True)
    ns = len(sources)

    def body(*refs):
        src = refs[:ns]
        w_ref, x_ref, pw_ref, d_ref, dx_ref, dpw_ref, acc = refs[ns:]
        i, j = pl.program_id(0), pl.program_id(1)

        @pl.when((i == 0) & (j == 0))
        def _():
            dpw_ref[...] = jnp.zeros_like(dpw_ref)

        @pl.when(j == 0)
        def _():
            acc[...] = jnp.zeros_like(acc)

        for s_ref, (lo, hi) in zip(src, spans):
            @pl.when((j >= lo) & (j < hi))
            def _(s_ref=s_ref):
                four = jnp.concatenate([s_ref[k] for k in range(gs)], axis=-1)
                acc[...] += lax.dot_general(four, w_ref[...], (((1,), (1,)), ((), ())), preferred_element_type=F32)

        @pl.when(j == nj - 1)
        def _():
            dx, dw_rows = _rmsnorm_bwd(x_ref[...], pw_ref[...], acc[...])
            dx_ref[...] = d_ref[...] + dx
            dpw_ref[...] += jnp.sum(dw_rows, 0, keepdims=True)

    rows = pl.BlockSpec((tm, D_MODEL), lambda i, j: (i, 0))
    vec = pl.BlockSpec((1, D_MODEL), lambda i, j: (0, 0))
    return _pcall(
        body, name=name, grid=(n // tm, nj),
        in_specs=src_specs + [pl.BlockSpec((None, D_MODEL, gs * DH), lambda i, j: (j, 0, 0)), rows, vec, rows],
        out_specs=[rows, vec],
        out_shape=[jax.ShapeDtypeStruct((n, D_MODEL), F32), jax.ShapeDtypeStruct((1, D_MODEL), F32)],
        scratch_shapes=[pltpu.VMEM((tm, D_MODEL), F32)],
        compiler_params=_cparams(("arbitrary", "arbitrary")),
    )(*sources, w_g, x, pre_w, dxn)


def _dwin(hb, sources, name):
    n = hb.shape[0]
    tm = _row_tile(n)
    gs = GROUPS_PER_STEP
    spans, src_specs = _source_specs(sources, False)
    ns = len(sources)

    def body(*refs):
        h_ref = refs[0]
        src = refs[1:1 + ns]
        dw_ref = refs[1 + ns]
        j = pl.program_id(0)

        @pl.when(pl.program_id(1) == 0)
        def _():
            dw_ref[...] = jnp.zeros_like(dw_ref)

        h = h_ref[...]
        for s_ref, (lo, hi) in zip(src, spans):
            @pl.when((j >= lo) & (j < hi))
            def _(s_ref=s_ref):
                four = jnp.concatenate([s_ref[k] for k in range(gs)], axis=-1)
                dw_ref[...] += jnp.dot(h, four, preferred_element_type=F32)

    return _pcall(
        body, name=name, grid=(N_GROUPS // gs, n // tm),
        in_specs=[pl.BlockSpec((D_MODEL, tm), lambda j, i: (0, i))] + src_specs,
        out_specs=pl.BlockSpec((None, D_MODEL, gs * DH), lambda j, i: (j, 0, 0)),
        out_shape=jax.ShapeDtypeStruct((N_GROUPS // gs, D_MODEL, gs * DH), F32),
        compiler_params=_cparams(("parallel", "arbitrary")),
    )(jnp.transpose(hb), *sources)


def _adamw_math(w, g, m, v):
    c1 = 1.0 - ADAM_B1 ** ADAM_STEP
    c2 = 1.0 - ADAM_B2 ** ADAM_STEP
    nm = ADAM_B1 * m + (1.0 - ADAM_B1) * g
    nv = ADAM_B2 * v + (1.0 - ADAM_B2) * (g * g)
    return -ADAM_LR * ((nm / c1) / (jnp.sqrt(nv / c2) + ADAM_EPS) + ADAM_WD * w), nm, nv


def _adamw(w, g, m, v, name):
    r, c = w.shape
    tr = 256 if r % 256 == 0 else r

    def body(w_ref, g_ref, m_ref, v_ref, d_ref, nm_ref, nv_ref):
        d_ref[...], nm_ref[...], nv_ref[...] = _adamw_math(w_ref[...], g_ref[...], m_ref[...], v_ref[...])

    spec = pl.BlockSpec((tr, c), lambda i: (i, 0))
    return _pcall(
        body, name=name, grid=(r // tr,),
        in_specs=[spec] * 4, out_specs=[spec] * 3,
        out_shape=[jax.ShapeDtypeStruct((r, c), F32)] * 3,
        compiler_params=_cparams(("parallel",)),
    )(w, g, m, v)


def _sum_adamw(parts, w, m, v, name):
    r, c = w.shape
    tr = 128 if r % 128 == 0 else r

    def body(p_ref, w_ref, m_ref, v_ref, g_ref, d_ref, nm_ref, nv_ref):
        g = p_ref[0]
        for k in range(1, N_DEV):
            g = g + p_ref[k]
        g_ref[...] = g
        d_ref[...], nm_ref[...], nv_ref[...] = _adamw_math(w_ref[...], g, m_ref[...], v_ref[...])

    spec = pl.BlockSpec((tr, c), lambda i: (i, 0))
    return _pcall(
        body, name=name, grid=(r // tr,),
        in_specs=[pl.BlockSpec((N_DEV, tr, c), lambda i: (0, i, 0))] + [spec] * 3, out_specs=[spec] * 4,
        out_shape=[jax.ShapeDtypeStruct((r, c), F32)] * 4,
        compiler_params=_cparams(("parallel",)),
    )(parts, w, m, v)


def _me():
    return lax.axis_index("x"), lax.axis_index("y"), lax.axis_index("c")


def _flat(x, y, c):
    return 4 * x + 2 * y + c


def _peer(k):
    x, y, c = _me()
    return (x ^ ((k >> 2) & 1), y ^ ((k >> 1) & 1), c ^ (k & 1))


def _gather_plan(blocks):
    def copies(x_refs, out_refs, sems):
        send_sems, recv_sems, local_sems = sems
        me = _flat(*_me())
        local = [pltpu.make_async_copy(x, o.at[me], local_sems.at[a]) for a, (x, o) in enumerate(zip(x_refs, out_refs))]
        outgoing, incoming = [], []
        for k in range(1, N_DEV):
            src = _flat(*_peer(k))
            for a, (x, o) in enumerate(zip(x_refs, out_refs)):
                for slot, group in ((me, outgoing), (src, incoming)):
                    group.append(pltpu.make_async_remote_copy(
                        src_ref=x, dst_ref=o.at[slot], send_sem=send_sems.at[a, k - 1], recv_sem=recv_sems.at[a, k - 1],
                        device_id=_peer(k), device_id_type=MESH))
        return local, outgoing, incoming

    return _Comm(list(blocks), [jax.ShapeDtypeStruct((N_DEV,) + b.shape, b.dtype) for b in blocks], copies)


def _exchange_plan(sends):
    def copies(s_refs, out_refs, sems):
        send_sems, recv_sems, local_sems = sems
        me = _flat(*_me())
        local = [pltpu.make_async_copy(s.at[me], o.at[0], local_sems.at[i]) for i, (s, o) in enumerate(zip(s_refs, out_refs))]
        outgoing = []
        for k in range(1, N_DEV):
            to = _flat(*_peer(k))
            for i, (s, o) in enumerate(zip(s_refs, out_refs)):
                outgoing.append(pltpu.make_async_remote_copy(
                    src_ref=s.at[to], dst_ref=o.at[k], send_sem=send_sems.at[i, k - 1], recv_sem=recv_sems.at[i, k - 1],
                    device_id=_peer(k), device_id_type=MESH))
        return local, outgoing, outgoing

    return _Comm(list(sends), [jax.ShapeDtypeStruct(s.shape, s.dtype) for s in sends], copies)


def _comm_scratch(plan):
    n = len(plan.operands)
    return [pltpu.SemaphoreType.DMA((n, N_DEV - 1)), pltpu.SemaphoreType.DMA((n, N_DEV - 1)),
            pltpu.SemaphoreType.DMA((n,))]


def _comm_start(copies):
    local, outgoing, _ = copies
    for cp in local + outgoing:
        cp.start()


def _comm_wait(copies):
    local, outgoing, incoming = copies
    for cp in incoming:
        cp.wait_recv()
    for cp in outgoing:
        cp.wait_send()
    for cp in local:
        cp.wait()


def _run_comm(plan, name):
    n = len(plan.operands)

    def body(*refs):
        copies = plan.copies(refs[:n], refs[n:2 * n], refs[2 * n:])
        _comm_start(copies)
        _comm_wait(copies)

    return _pcall(
        body, name=name,
        in_specs=[pl.BlockSpec(memory_space=pl.ANY)] * n,
        out_specs=[pl.BlockSpec(memory_space=pl.ANY)] * n,
        out_shape=plan.out_shapes,
        scratch_shapes=_comm_scratch(plan),
    )(*plan.operands)


def _sum_slots(a, name):
    r = a.shape[1]

    def body(a_ref, o_ref):
        acc = a_ref[0]
        for d in range(1, N_DEV):
            acc = acc + a_ref[d]
        o_ref[...] = acc

    return _pcall(body, name=name, out_shape=jax.ShapeDtypeStruct((r, 128), F32), compiler_params=_cparams())(a)


def _all_reduce_small(blk, name):
    r = blk.shape[0]

    def body(x_ref, out_ref, gath, send_sems, recv_sems):
        me = _flat(*_me())
        gath[me] = x_ref[...]
        copies = []
        for k in range(1, N_DEV):
            cp = pltpu.make_async_remote_copy(
                src_ref=x_ref, dst_ref=gath.at[me],
                send_sem=send_sems.at[k - 1], recv_sem=recv_sems.at[k - 1],
                device_id=_peer(k), device_id_type=MESH)
            cp.start()
            copies.append(cp)
        for k in range(1, N_DEV):
            src = _flat(*_peer(k))
            pltpu.make_async_remote_copy(
                src_ref=x_ref, dst_ref=gath.at[src],
                send_sem=send_sems.at[k - 1], recv_sem=recv_sems.at[k - 1],
                device_id=_peer(k), device_id_type=MESH).wait_recv()
        for cp in copies:
            cp.wait_send()
        acc = gath[0]
        for d in range(1, N_DEV):
            acc = acc + gath[d]
        out_ref[...] = acc

    return _pcall(
        body, name=name,
        in_specs=[pl.BlockSpec(memory_space=pltpu.VMEM)],
        out_specs=pl.BlockSpec(memory_space=pltpu.VMEM),
        out_shape=jax.ShapeDtypeStruct((r, 128), F32),
        scratch_shapes=[pltpu.VMEM((N_DEV, r, 128), F32),
                        pltpu.SemaphoreType.DMA((N_DEV - 1,)), pltpu.SemaphoreType.DMA((N_DEV - 1,))],
    )(blk)


def _heads(vec):
    return vec.reshape(NH, 1, DH)


def _rep(vec4):
    return jnp.broadcast_to(vec4.reshape(NH, 1, 1), (NH, 1, DH))


def _onehot_lane(offset):
    m = np.zeros((NH, 1, DH), np.float32)
    for h in range(NH):
        m[h, 0, offset + h] = 1.0
    return jnp.asarray(m)


_TINY = (("gdn_conv_w", (DEPTH, 4, 96)), ("rwkv_w_up", (DEPTH, 64, 32)), ("rwkv_a_up", (DEPTH, 64, 32)),
         ("sc_conv_w", (DEPTH, 3, 32)))
_TINY_ROWS = -(-sum(int(np.prod(s)) for _, s in _TINY) // 1024) * 8


def _pack_rows(arrays, rows, fill=0.0):
    flat = jnp.concatenate([a.reshape(-1) for a in arrays])
    return jnp.pad(flat, (0, rows * 128 - flat.shape[0]), constant_values=fill).reshape(rows, 128)


def _unpack_rows(p, named_shapes):
    lead = p.shape[:-2]
    flat = p.reshape(lead + (-1,))
    out, o = {}, 0
    for n, s in named_shapes:
        size = int(np.prod(s))
        out[n] = flat[..., o:o + size].reshape(lead + tuple(s))
        o += size
    return out


def _gather_last(a):
    return jnp.transpose(a, (1, 0, 2)).reshape(a.shape[1], -1)


def _split_last(a):
    r, c8 = a.shape
    return jnp.transpose(a.reshape(r, N_DEV, c8 // N_DEV), (1, 0, 2))


_SMALL = (("pre_norm_w", (DEPTH, 1024)), ("gdn_a_log", (DEPTH, 4)), ("gdn_dt_bias", (DEPTH, 4)),
          ("gdn_norm_w", (DEPTH, 64)), ("rwkv_mu", (DEPTH, 1152)), ("rwkv_w0", (DEPTH, 256)),
          ("rwkv_a0", (DEPTH, 256)), ("rwkv_k_k", (DEPTH, 256)), ("rwkv_k_a", (DEPTH, 256)),
          ("rwkv_r_k", (DEPTH, 256)), ("rwkv_ln_w", (DEPTH, 256)), ("rwkv_ln_b", (DEPTH, 256)),
          ("gla_a_up", (DEPTH, 16, 128)), ("gla_a_bias", (DEPTH, 128)), ("gla_norm_w", (DEPTH, 64)),
          ("post_norm_w", (DEPTH, 1024)), ("loss", ()))
_SMALL_ROWS = -(-sum(int(np.prod(s)) for _, s in _SMALL) // 1024) * 8


def _big_weights(w_in_all, w_out_all, l):
    return dict(w_g=_regroup_in(w_in_all, f"regroup_in{l}"),
                wout_g=w_out_all.reshape(4 * NH, DH, D_MODEL).astype(BF16))


def _layer_params(wts, tiny, l):
    conv = _gather_last(tiny["gdn_conv_w"][:, l])
    q = {}
    q["gdn_conv"] = jnp.transpose(conv.reshape(GDN_TAPS, 12, DH), (1, 0, 2))
    q["gdn_prm"] = [_rep(wts["gdn_a_log"][l]), _rep(wts["gdn_dt_bias"][l]),
                    jnp.broadcast_to(wts["gdn_norm_w"][l].reshape(1, 1, DH), (NH, 1, DH))]
    q["gdn_cst"] = [_onehot_lane(0), _onehot_lane(NH)]
    q["rwkv_mu"] = wts["rwkv_mu"][l].reshape(18, 1, DH)
    w_up = jnp.transpose(_gather_last(tiny["rwkv_w_up"][:, l]).reshape(64, NH, DH), (1, 0, 2))
    a_up = jnp.transpose(_gather_last(tiny["rwkv_a_up"][:, l]).reshape(64, NH, DH), (1, 0, 2))
    q["rwkv_prm"] = [_heads(wts["rwkv_w0"][l]), w_up, _heads(wts["rwkv_a0"][l]), a_up,
                     _heads(wts["rwkv_k_k"][l]), _heads(wts["rwkv_k_a"][l]), _heads(wts["rwkv_r_k"][l]),
                     _heads(wts["rwkv_ln_w"][l]), _heads(wts["rwkv_ln_b"][l])]
    sc = _gather_last(tiny["sc_conv_w"][:, l])
    q["sc_conv"] = jnp.transpose(sc.reshape(SC_TAPS, NH, DH), (1, 0, 2))
    gla_up = jnp.transpose(wts["gla_a_up"][l].reshape(16, NH, GLA_HEAD_K), (1, 0, 2))
    gla_up = jnp.pad(gla_up, ((0, 0), (0, DH - 16), (0, DH - GLA_HEAD_K)))
    gla_b = jnp.pad(wts["gla_a_bias"][l].reshape(NH, 1, GLA_HEAD_K), ((0, 0), (0, 0), (0, DH - GLA_HEAD_K)))
    q["gla_prm"] = [gla_up, gla_b, jnp.broadcast_to(wts["gla_norm_w"][l].reshape(1, 1, DH), (NH, 1, DH))]
    q["pre_w"] = wts["pre_norm_w"][l].reshape(1, D_MODEL)
    q["post_w"] = wts["post_norm_w"][l].reshape(1, D_MODEL)
    return q


def _mixer_inputs(p, cq, pm):
    gdn = [(cq, 4, 0), (cq, 4, 1), (cq, 4, 2), (p, 4, G_GDN // 4 + 3), (p, 1, G_GDN_AB)]
    rwkv = [(pm, 4, 0), (pm, 4, 1), (pm, 4, 2), (pm, 4, 3), (pm, 1, 16), (pm, 1, 17)]
    gla = [(p, 4, G_GLA // 4 + k) for k in range(4)] + [(p, 1, G_GLA_AD)]
    return gdn, rwkv, gla


def _layer_fwd(x, q, nb, t, l, side=None):
    hb, p = _norm_proj(x, q["pre_w"], q["w_g"], f"norm_proj{l}")
    cq = _conv_fwd(p, G_GDN, 12, q["gdn_conv"], nb, t, f"gdn_conv{l}")
    pm = _mix_fwd(p, q["rwkv_mu"], nb, t, f"rwkv_mix{l}")
    gdn_in, rwkv_in, gla_in = _mixer_inputs(p, cq, pm)
    y_gdn, ck_gdn, _ = _mixer_fwd(_gdn_chunk, f"gdn_fwd{l}", gdn_in, q["gdn_prm"], q["gdn_cst"], nb, t)
    y_rwkv, ck_rwkv, side_res = _mixer_fwd(_rwkv_chunk, f"rwkv_fwd{l}", rwkv_in, q["rwkv_prm"], [], nb, t,
                                           first_fn=_rwkv_first_chunk, side=side)
    y_sc = _sc_fwd(p, q["sc_conv"], nb, t, f"sc_fwd{l}")
    y_gla, ck_gla, _ = _mixer_fwd(_gla_chunk, f"gla_fwd{l}", gla_in, q["gla_prm"], [], nb, t)
    ys = (y_gdn, y_rwkv, y_sc, y_gla)
    out, xn = _out_proj_norm(ys, q["wout_g"], x, q["post_w"], f"out_proj{l}")
    saved = dict(x=x, hb=hb, p=p, cq=cq, pm=pm, ys=ys, out=out, ck=(ck_gdn, ck_rwkv, ck_gla))
    return xn, saved, side_res


def _layer_bwd(dxn, q, sv, nb, t, l, side=None):
    p, cq, pm, ys = sv["p"], sv["cq"], sv["pm"], sv["ys"]
    dout, dy, d_post = _post_bwd(dxn, sv["out"], q["post_w"], q["wout_g"], f"post_bwd{l}")
    d_wout = _dwout(ys, dout, f"dwout{l}").reshape(N_DEV, 128, D_MODEL)
    gdn_in, rwkv_in, gla_in = _mixer_inputs(p, cq, pm)
    ck_gdn, ck_rwkv, ck_gla = sv["ck"]
    g = {}

    (d_conv, dz, dab), (da_log, ddt, dnw), _ = _mixer_bwd(
        _gdn_chunk, f"gdn_bwd{l}", gdn_in, q["gdn_prm"], q["gdn_cst"], ck_gdn, dy, 0,
        [(12, F32), (4, BF16), (1, BF16)], [(0, 0), (0, 4), (0, 8), (1, 0), (2, 0)], nb, t)
    dconv_in, d_gconv = _conv_bwd(p, G_GDN, 12, q["gdn_conv"], d_conv, nb, t, f"gdn_conv_bwd{l}")
    g["gdn_conv_w"] = jnp.transpose(d_gconv.sum(1), (1, 0, 2)).reshape(GDN_TAPS, 768)
    g["gdn_a_log"] = da_log.sum((0, 2, 3))
    g["gdn_dt_bias"] = ddt.sum((0, 2, 3))
    g["gdn_norm_w"] = dnw.sum((0, 1, 2))

    (d_pm,), d_rprm, side_res = _mixer_bwd(
        _rwkv_chunk, f"rwkv_bwd{l}", rwkv_in, q["rwkv_prm"], [], ck_rwkv, dy, 1,
        [(18, F32)], [(0, 0), (0, 4), (0, 8), (0, 12), (0, 16), (0, 17)], nb, t, first_fn=_rwkv_first_chunk,
        side=side)
    dp_rwkv, d_mu = _mix_bwd(p, q["rwkv_mu"], d_pm, nb, t, f"rwkv_mix_bwd{l}")
    g["rwkv_mu"] = d_mu.sum(1).reshape(1152)
    rp = [a.sum(0) for a in d_rprm]
    g["rwkv_w0"] = rp[0].reshape(256)
    g["rwkv_w_up"] = jnp.transpose(rp[1], (1, 0, 2)).reshape(64, 256)
    g["rwkv_a0"] = rp[2].reshape(256)
    g["rwkv_a_up"] = jnp.transpose(rp[3], (1, 0, 2)).reshape(64, 256)
    for i, nme in enumerate(("rwkv_k_k", "rwkv_k_a", "rwkv_r_k", "rwkv_ln_w", "rwkv_ln_b")):
        g[nme] = rp[4 + i].reshape(256)

    dp_sc, d_scw = _sc_bwd(p, q["sc_conv"], dy, nb, t, f"sc_bwd{l}")
    g["sc_conv_w"] = jnp.transpose(d_scw.sum(1), (1, 0, 2)).reshape(SC_TAPS, 256)

    (dp_gla, dad), (d_aup, d_ab, d_gnw), _ = _mixer_bwd(
        _gla_chunk, f"gla_bwd{l}", gla_in, q["gla_prm"], [], ck_gla, dy, 3,
        [(16, BF16), (1, BF16)], [(0, 0), (0, 4), (0, 8), (0, 12), (1, 0)], nb, t)
    g["gla_a_up"] = jnp.transpose(d_aup.sum(0)[:, :16, :GLA_HEAD_K], (1, 0, 2)).reshape(16, 128)
    g["gla_a_bias"] = d_ab.sum(0)[:, 0, :GLA_HEAD_K].reshape(128)
    g["gla_norm_w"] = d_gnw.sum((0, 1, 2))

    singles = jnp.concatenate([dab, dp_rwkv[16:18], dad], axis=0)
    sources = [dconv_in, dz, dp_rwkv, dp_sc, dp_gla, singles]
    dx, d_pre = _dh_prenorm_bwd(sources, q["w_g"], sv["x"], q["pre_w"], dxn, f"dh_bwd{l}")
    d_win = _regroup_out(_dwin(sv["hb"], sources, f"dwin{l}"), f"regroup_out{l}")
    g["pre_norm_w"] = d_pre.reshape(D_MODEL)
    g["post_norm_w"] = d_post.reshape(D_MODEL)
    return dx, g, d_win, d_wout, side_res


def _local_step(x, tgt, wts, tiny, w_in_all, w_out_all, later_shards=None):
    nb, t, d = x.shape
    xf = x.reshape(nb * t, d)
    overlap = later_shards is not None
    qs, saved = [], []
    big = _big_weights(w_in_all[0], w_out_all[0], 0)
    for l in range(DEPTH):
        q = dict(_layer_params(wts, tiny, l), **big)
        nxt = l + 1 < DEPTH
        side = _gather_plan(later_shards[l]) if overlap and nxt else None
        xf, sv, got = _layer_fwd(xf, q, nb, t, l, side)
        if nxt:
            big = _big_weights(*(got if overlap else (w_in_all[l + 1], w_out_all[l + 1])), l + 1)
        qs.append(q)
        saved.append(sv)
    dxf, lpart = _loss_grad(xf, tgt.reshape(nb * t, d), "loss")
    grads, d_win, d_wout = [None] * DEPTH, [None] * DEPTH, [None] * DEPTH
    for l in reversed(range(DEPTH)):
        side = _exchange_plan([d_win[l + 1], d_wout[l + 1]]) if overlap and l + 1 < DEPTH else None
        dxf, grads[l], d_win[l], d_wout[l], got = _layer_bwd(dxf, qs[l], saved[l], nb, t, l, side)
        if side is not None:
            d_win[l + 1], d_wout[l + 1] = got
    small = {k: jnp.stack([grads[l][k] for l in range(DEPTH)]) for k in grads[0]}
    return lpart[0, 0], dxf.reshape(nb, t, d), small, d_win, d_wout


_WEIGHTS = ("pre_norm_w", "w_in", "gdn_conv_w", "gdn_a_log", "gdn_dt_bias", "gdn_norm_w", "rwkv_mu", "rwkv_w0",
            "rwkv_w_up", "rwkv_a0", "rwkv_a_up", "rwkv_k_k", "rwkv_k_a", "rwkv_r_k", "rwkv_ln_w", "rwkv_ln_b",
            "sc_conv_w", "gla_a_up", "gla_a_bias", "gla_norm_w", "w_out", "post_norm_w")


def kernel(x, pre_norm_w, w_in, gdn_conv_w, gdn_a_log, gdn_dt_bias, gdn_norm_w, rwkv_mu, rwkv_w0, rwkv_w_up, rwkv_a0, rwkv_a_up, rwkv_k_k, rwkv_k_a, rwkv_r_k, rwkv_ln_w, rwkv_ln_b, sc_conv_w, gla_a_up, gla_a_bias, gla_norm_w, w_out, post_norm_w, loss_target, m_pre_norm_w, m_w_in, m_gdn_conv_w, m_gdn_a_log, m_gdn_dt_bias, m_gdn_norm_w, m_rwkv_mu, m_rwkv_w0, m_rwkv_w_up, m_rwkv_a0, m_rwkv_a_up, m_rwkv_k_k, m_rwkv_k_a, m_rwkv_r_k, m_rwkv_ln_w, m_rwkv_ln_b, m_sc_conv_w, m_gla_a_up, m_gla_a_bias, m_gla_norm_w, m_w_out, m_post_norm_w, v_pre_norm_w, v_w_in, v_gdn_conv_w, v_gdn_a_log, v_gdn_dt_bias, v_gdn_norm_w, v_rwkv_mu, v_rwkv_w0, v_rwkv_w_up, v_rwkv_a0, v_rwkv_a_up, v_rwkv_k_k, v_rwkv_k_a, v_rwkv_r_k, v_rwkv_ln_w, v_rwkv_ln_b, v_sc_conv_w, v_gla_a_up, v_gla_a_bias, v_gla_norm_w, v_w_out, v_post_norm_w):
    env = dict(locals())
    w = {n: env[n] for n in _WEIGHTS}
    m = {n: env["m_" + n] for n in _WEIGHTS}
    v = {n: env["v_" + n] for n in _WEIGHTS}
    tiny_names = [n for n, _ in _TINY]

    w_in_b, w_out_b = w_in.astype(BF16), w_out.astype(BF16)
    w_in_0, w_out_0, tiny_all = _run_comm(
        _gather_plan([w_in_b[0], w_out_b[0], _pack_rows([w[n] for n in tiny_names], _TINY_ROWS)]), "gather_weights")
    tiny = _unpack_rows(tiny_all, _TINY)

    lpart, grad_x, small, r_win, r_wout = _local_step(
        x, loss_target, w, tiny, [w_in_0], [w_out_0], later_shards=[(w_in_b[l], w_out_b[l]) for l in range(1, DEPTH)])

    tiny_send = jnp.stack([_pack_rows([_split_last(small[n][l])[d] for n in tiny_names for l in range(DEPTH)],
                                      _TINY_ROWS) for d in range(N_DEV)])
    r_win[0], r_wout[0], r_tiny = _run_comm(_exchange_plan([r_win[0], r_wout[0], tiny_send]), "scatter_grads")
    grads, delta, new_m, new_v = {}, {}, {}, {}
    for n, parts in (("w_in", r_win), ("w_out", r_wout)):
        res = [_sum_adamw(parts[l], w[n][l], m[n][l], v[n][l], f"adamw_{n}{l}") for l in range(DEPTH)]
        grads[n], delta[n], new_m[n], new_v[n] = [jnp.stack(o) for o in zip(*res)]
    tiny_sum = _sum_slots(r_tiny, "sum_tiny").reshape(-1)
    o = 0
    for n, s in _TINY:
        size = int(np.prod(s))
        grads[n] = tiny_sum[o:o + size].reshape(s)
        o += size

    small = dict(small)
    small["loss"] = lpart
    red = _unpack_rows(_all_reduce_small(_pack_rows([small[n] for n, _ in _SMALL], _SMALL_ROWS), "reduce_small"),
                       _SMALL)
    loss = red.pop("loss")
    grads.update(red)

    rest = [n for n in _WEIGHTS if n not in ("w_in", "w_out")]
    rest_shapes = [(n, w[n].shape) for n in rest]
    rows = -(-sum(int(np.prod(s)) for _, s in rest_shapes) // 1024) * 8
    outs = _adamw(_pack_rows([w[n] for n in rest], rows), _pack_rows([grads[n] for n in rest], rows),
                  _pack_rows([m[n] for n in rest], rows), _pack_rows([v[n] for n in rest], rows, 1.0), "adamw_rest")
    for dst, packed in zip((delta, new_m, new_v), outs):
        dst.update(_unpack_rows(packed, rest_shapes))

    return (loss, grad_x, *[grads[n] for n in _WEIGHTS], *[delta[n] for n in _WEIGHTS],
            *[new_m[n] for n in _WEIGHTS], *[new_v[n] for n in _WEIGHTS])
```

```python
import collections
import functools
import math

import numpy as np
import jax
import jax.numpy as jnp
from jax import lax
from jax.experimental import pallas as pl
from jax.experimental.pallas import tpu as pltpu

F32 = jnp.float32
BF16 = jnp.bfloat16

D_MODEL = 1024
DEPTH = 2
NH = 4
DH = 64
CH = 64
EPS = 1e-6
RWKV_GN_EPS = 64e-5
GLA_HEAD_K = 32
GLA_TAU = 16.0
GDN_TAPS = 4
SC_TAPS = 3
D_IN = 3992
N_DEV = 8
SHARD_COLS = D_IN // N_DEV

G_GDN = 0
G_RWKV = 16
G_SC = 32
G_GLA = 48
G_GDN_AB, G_RWKV_WD, G_RWKV_AD, G_GLA_AD = 64, 65, 66, 67
N_GROUPS = 68
GROUPS_PER_STEP = 4
TIME_BLOCK = 256
RWKV_EXACT_STEPS = 16

C_GDN, C_RWKV, C_SC, C_GLA = 0, 1032, 2184, 3208

ADAM_LR, ADAM_B1, ADAM_B2, ADAM_EPS, ADAM_WD, ADAM_STEP = 0.001, 0.9, 0.999, 1e-08, 0.01, 10

VMEM_LIMIT = 56 * 1024 * 1024
MESH = pl.DeviceIdType.MESH

_pcall = pl.pallas_call

_Comm = collections.namedtuple("_Comm", "operands out_shapes copies")


def _cparams(sem=None):
    if sem is None:
        return pltpu.CompilerParams(vmem_limit_bytes=VMEM_LIMIT)
    return pltpu.CompilerParams(dimension_semantics=sem, vmem_limit_bytes=VMEM_LIMIT)


def _group_segments():
    table = [(G_GDN + i, C_GDN + DH * i, DH) for i in range(16)]
    table.append((G_GDN_AB, C_GDN + 1024, 8))
    table += [(G_RWKV + i, C_RWKV + DH * i, DH) for i in range(16)]
    table += [(G_RWKV_WD, C_RWKV + 1024, DH), (G_RWKV_AD, C_RWKV + 1088, DH)]
    table += [(G_SC + 4 * j + k, C_SC + 256 * k + DH * j, DH) for j in range(NH) for k in range(4)]
    for h in range(NH):
        table += [(G_GLA + h, C_GLA + GLA_HEAD_K * h, GLA_HEAD_K),
                  (G_GLA + 4 + h, C_GLA + 128 + GLA_HEAD_K * h, GLA_HEAD_K),
                  (G_GLA + 8 + h, C_GLA + 256 + DH * h, DH),
                  (G_GLA + 12 + h, C_GLA + 512 + DH * h, DH)]
    table.append((G_GLA_AD, C_GLA + 768, 16))
    segs, padded = [], []
    for g, c, n in table:
        if n < DH:
            padded.append(g)
        a = 0
        while n > 0:
            d, off = divmod(c, SHARD_COLS)
            ln = min(n, SHARD_COLS - off)
            segs.append((g, a, d, off, ln))
            c, a, n = c + ln, a + ln, n - ln
    return segs, padded


_SEGMENTS, _PADDED_GROUPS = _group_segments()


def _dn(ta, tb):
    return (((1 if ta else 2,), (2 if tb else 1,)), ((0,), (0,)))


def _hdot(a, b, ta=False, tb=False):
    return lax.dot_general(a, b, _dn(ta, tb), precision=lax.Precision.HIGH, preferred_element_type=F32)


def _r(x):
    return x.astype(BF16)


def _rdot(a, b, ta=False, tb=False):
    return lax.dot_general(_r(a), _r(b), _dn(ta, tb), preferred_element_type=F32)


@jax.custom_vjp
def _bmm(a, b):
    return _rdot(a, b)


def _bmm_fwd(a, b):
    return _rdot(a, b), (a, b)


def _bmm_bwd(res, g):
    a, b = res
    return _rdot(g, b, tb=True), _rdot(a, g, ta=True)


_bmm.defvjp(_bmm_fwd, _bmm_bwd)


@jax.custom_vjp
def _bmm_nt(a, b):
    return _rdot(a, b, tb=True)


def _bmm_nt_fwd(a, b):
    return _rdot(a, b, tb=True), (a, b)


def _bmm_nt_bwd(res, g):
    a, b = res
    return _rdot(g, b), _rdot(g, a, ta=True)


_bmm_nt.defvjp(_bmm_nt_fwd, _bmm_nt_bwd)


@jax.custom_vjp
def _bmm_tn(a, b):
    return _rdot(a, b, ta=True)


def _bmm_tn_fwd(a, b):
    return _rdot(a, b, ta=True), (a, b)


def _bmm_tn_bwd(res, g):
    a, b = res
    return _rdot(b, g, tb=True), _rdot(a, g)


_bmm_tn.defvjp(_bmm_tn_fwd, _bmm_tn_bwd)


def _tri(n):
    i = lax.broadcasted_iota(jnp.int32, (n, n), 0)
    j = lax.broadcasted_iota(jnp.int32, (n, n), 1)
    return i >= j, i > j, i == j


def _heads_of(x, like):
    n = like.shape[0]
    if x.ndim == 2:
        return jnp.broadcast_to(x[None], (n,) + x.shape)
    seqs = x.shape[0]
    return jnp.broadcast_to(x[:, None], (seqs, n // seqs) + x.shape[1:]).reshape((n,) + x.shape[1:])


def _cumsum_rows(x):
    incl, _, _ = _tri(x.shape[-2])
    return _hdot(_heads_of(incl.astype(F32), x), x)


@jax.custom_vjp
def _inv_unit_lower(a):
    n = a.shape[-1]
    _, _, eye = _tri(n)
    pw = -a
    inv = eye.astype(F32) + pw
    for _ in range(math.ceil(math.log2(n)) - 1):
        pw = _hdot(pw, pw)
        inv = inv + _hdot(inv, pw)
    return inv


def _inv_unit_lower_fwd(a):
    inv = _inv_unit_lower(a)
    return inv, inv


def _inv_unit_lower_bwd(inv, g):
    return (-_hdot(_hdot(inv, g, ta=True), inv, tb=True),)


_inv_unit_lower.defvjp(_inv_unit_lower_fwd, _inv_unit_lower_bwd)


def _silu(x):
    return x * jax.nn.sigmoid(x)


def _t(x):
    return jnp.swapaxes(x, -1, -2)


def _gdn_chunk(prm, cst, ins, s):
    a_log, dt_b, nw = prm
    m_a, m_b = cst
    cq, ck, cv, z, ab = ins
    ab = _heads_of(ab, m_a)
    incl, strict, _ = _tri(CH)
    q = _silu(cq)
    k = _silu(ck)
    v = _silu(cv)
    q = q * lax.rsqrt(jnp.sum(q * q, -1, keepdims=True) + EPS) * (DH ** -0.5)
    k = k * lax.rsqrt(jnp.sum(k * k, -1, keepdims=True) + EPS)
    a_raw = jnp.sum(ab * m_a, -1, keepdims=True)
    b_raw = jnp.sum(ab * m_b, -1, keepdims=True)
    gstep = -jnp.exp(a_log) * jax.nn.softplus(a_raw + dt_b)
    beta = jax.nn.sigmoid(b_raw)
    gc = _cumsum_rows(gstep)
    gl = jnp.sum(gstep, -2, keepdims=True)
    dec = jnp.where(incl, jnp.exp(jnp.where(incl, gc - _t(gc), 0.0)), 0.0)
    kb = k * beta
    a_mat = jnp.where(strict, _bmm_nt(kb, k) * dec, 0.0)
    tinv = _inv_unit_lower(a_mat)
    eg = jnp.exp(gc)
    u = _hdot(tinv, v * beta)
    w = _hdot(tinv, kb * eg)
    attn = _bmm_nt(q, k) * dec
    v_new = u - _bmm(w, s)
    o = _bmm(q * eg, s) + _bmm(attn, v_new)
    s_next = s * jnp.exp(gl) + _bmm_tn(k * jnp.exp(gl - gc), v_new)
    on = o * lax.rsqrt(jnp.mean(o * o, -1, keepdims=True) + EPS) * nw
    return on * _silu(z), s_next


def _gla_chunk(prm, cst, ins, st):
    a_up, a_bias, nw = prm
    q, k, v, z, ad = ins
    incl, _, _ = _tri(CH)
    la = jax.nn.log_sigmoid(_bmm(_heads_of(ad, a_up), a_up) + a_bias) * (1.0 / GLA_TAU)
    bc = _cumsum_rows(la)
    bl = jnp.sum(la, -2, keepdims=True)
    qe = q * (GLA_HEAD_K ** -0.5) * jnp.exp(bc)
    ke = k * jnp.exp(-bc)
    attn = jnp.where(incl, _bmm_nt(qe, ke), 0.0)
    o = _bmm_nt(qe, st) + _bmm(attn, v)
    st_next = st * jnp.exp(bl) + _bmm_tn(v, k * jnp.exp(bl - bc))
    on = o * lax.rsqrt(jnp.mean(o * o, -1, keepdims=True) + EPS) * nw
    return on * _silu(z), st_next


def _rwkv_chunk(prm, cst, ins, s):
    r, v = ins[0], ins[2]
    incl, strict, _ = _tri(r.shape[-2])
    lw, kk, k2, m = _rwkv_pre(prm, ins)
    cum = _cumsum_rows(lw)
    ltot = jnp.sum(lw, -2, keepdims=True)
    n_t = -kk * jnp.exp(cum - lw)
    einv = jnp.exp(-cum)
    m_t = m * einv
    k_t = k2 * einv
    r_t = r * jnp.exp(cum)
    a_nm = jnp.where(strict, _hdot(n_t, m_t, tb=True), 0.0)
    a_nk = jnp.where(strict, _hdot(n_t, k_t, tb=True), 0.0)
    cm = _hdot(_inv_unit_lower(-a_nm), _hdot(n_t, s, tb=True) + _hdot(a_nk, v))
    y = (_hdot(r_t, s, tb=True) + _hdot(jnp.where(incl, _hdot(r_t, m_t, tb=True), 0.0), cm)
         + _hdot(jnp.where(incl, _hdot(r_t, k_t, tb=True), 0.0), v))
    eend = jnp.exp(ltot - cum)
    s_next = s * jnp.exp(ltot) + _hdot(cm, m * eend, ta=True) + _hdot(v, k2 * eend, ta=True)
    return _rwkv_post(prm, ins, y, k2), s_next


def _rwkv_pre(prm, ins):
    w0, w_up, a0, a_up, k_k, k_a = prm[:6]
    k, wd, ad = ins[1], ins[4], ins[5]
    lw = -math.exp(-0.5) * jax.nn.sigmoid(w0 + _bmm(_heads_of(jnp.tanh(wd), w_up), w_up))
    a = jax.nn.sigmoid(a0 + _bmm(_heads_of(ad, a_up), a_up))
    kk = k * k_k
    kk = kk * lax.rsqrt(jnp.sum(kk * kk, -1, keepdims=True) + EPS)
    k2 = k * (1.0 + (a - 1.0) * k_a)
    return lw, kk, k2, kk * a


def _rwkv_post(prm, ins, y, k2):
    r_k, ln_w, ln_b = prm[6:]
    r, v, z = ins[0], ins[2], ins[3]
    mean = jnp.mean(y, -1, keepdims=True)
    yc = y - mean
    var = jnp.mean(yc * yc, -1, keepdims=True)
    yn = yc * lax.rsqrt(var + RWKV_GN_EPS) * ln_w + ln_b
    bonus = jnp.sum(r * k2 * r_k, -1, keepdims=True) * v
    return (yn + bonus) * _silu(z)


@jax.custom_vjp
def _bmv(s, x):
    return jnp.sum(_r(s).astype(F32) * _r(x).astype(F32), -1, keepdims=True)


def _bmv_fwd(s, x):
    return _bmv(s, x), (s, x)


def _bmv_bwd(res, g):
    s, x = res
    return g * x, jnp.sum(_r(s).astype(F32) * _r(g).astype(F32), -2, keepdims=True)


_bmv.defvjp(_bmv_fwd, _bmv_bwd)


def _rwkv_steps(prm, cst, ins, s, steps):
    r, v = ins[0], ins[2]
    lw, kk, k2, m = _rwkv_pre(prm, ins)
    w = jnp.exp(lw)
    v_t = _t(v)
    lane = lax.broadcasted_iota(jnp.int32, (1, 1, CH), 2)
    y_t = jnp.zeros((s.shape[0], DH, CH), F32)
    for t in range(steps):
        e_t = (lane == t).astype(F32)
        row = (slice(None), slice(t, t + 1))
        sa = _bmv(s, -kk[row])
        s = s * w[row] + sa * m[row] + jnp.sum(v_t * e_t, -1, keepdims=True) * k2[row]
        y_t = y_t + _bmv(s, r[row]) * e_t
    return _rwkv_post(prm, ins, _t(y_t), k2)[:, :steps], s


def _rwkv_first_chunk(prm, cst, ins, s):
    k = RWKV_EXACT_STEPS
    y_head, s = _rwkv_steps(prm, cst, ins, s, k)
    y_tail, s = _rwkv_chunk(prm, cst, [x[..., k:, :] for x in ins], s)
    return jnp.concatenate([y_head, y_tail], axis=-2), s


def _time_block(t):
    return TIME_BLOCK if t % TIME_BLOCK == 0 else t


def _load_chunk(ref, i):
    nb = ref.shape[1]
    if ref.shape[0] == NH:
        return jnp.concatenate([ref[:, b, pl.ds(i, CH), :] for b in range(nb)], axis=0)
    return ref[0, :, pl.ds(i, CH), :]


def _load_chunk_of(ref, c, i):
    if ref.shape[0] == NH:
        return ref[pl.ds(c % NH, 1), c // NH, pl.ds(i, CH), :]
    return ref[0, pl.ds(c // NH, 1), pl.ds(i, CH), :]


def _each_chain(n, fn):
    def step(c, carry):
        fn(c)
        return carry

    lax.fori_loop(0, n, step, 0)


def _mixer_fwd(chunk_fn, name, ins, prm, cst, nb, t, first_fn=None, side=None):
    tb = _time_block(t)
    nt, ncb, nch = t // tb, tb // CH, nb * NH
    n_in, n_prm, n_cst = len(ins), len(prm), len(cst)
    n_main, n_side = n_in + n_prm + n_cst, len(side.operands) if side else 0

    def body(*refs):
        in_refs = refs[:n_in]
        prm_refs = refs[n_in:n_in + n_prm]
        cst_refs = refs[n_in + n_prm:n_main]
        side_in = refs[n_main:n_main + n_side]
        y_ref, ck_ref = refs[n_main + n_side:n_main + n_side + 2]
        side_out = refs[n_main + n_side + 2:n_main + 2 * n_side + 2]
        s_scr = refs[n_main + 2 * n_side + 2]
        sems = refs[n_main + 2 * n_side + 3:]
        step_t = pl.program_id(0)

        if side is not None:
            @pl.when(step_t == 0)
            def _():
                _comm_start(side.copies(side_in, side_out, sems))

        @pl.when(step_t == 0)
        def _():
            s_scr[...] = jnp.zeros_like(s_scr)

        def chunk(c, i):
            s = s_scr[...]
            ck_ref[c] = s
            y, s_next = chunk_fn([jnp.tile(r[...], (nb, 1, 1)) for r in prm_refs],
                                 [jnp.tile(r[...], (nb, 1, 1)) for r in cst_refs],
                                 [_load_chunk(r, i) for r in in_refs], s)
            for b in range(nb):
                y_ref[:, b, pl.ds(i, CH), :] = y[b * NH:(b + 1) * NH].astype(BF16)
            s_scr[...] = s_next

        def first_chunk_of(c):
            one, h = pl.ds(c, 1), pl.ds(c % NH, 1)
            s = s_scr[one]
            ck_ref[0, one] = s
            y, s_next = first_fn([r[h] for r in prm_refs], [r[h] for r in cst_refs],
                                 [_load_chunk_of(r, c, 0) for r in in_refs], s)
            y_ref[h, c // NH, pl.ds(0, CH), :] = y.astype(BF16)
            s_scr[one] = s_next

        def step(c, carry):
            chunk(c, pl.multiple_of(c * CH, CH))
            return carry

        if first_fn is None:
            lax.fori_loop(0, ncb, step, 0)
        else:
            @pl.when(step_t == 0)
            def _():
                _each_chain(nch, first_chunk_of)

            @pl.when(step_t != 0)
            def _():
                chunk(0, 0)

            lax.fori_loop(1, ncb, step, 0)

        if side is not None:
            @pl.when(step_t == nt - 1)
            def _():
                _comm_wait(side.copies(side_in, side_out, sems))

    hbm = pl.BlockSpec(memory_space=pl.ANY)
    in_specs = [pl.BlockSpec((ng, nb, tb, DH), (lambda j, bi=bi: (bi, 0, j, 0))) for _, ng, bi in ins]
    in_specs += [pl.BlockSpec(p.shape, lambda j: (0, 0, 0)) for p in list(prm) + list(cst)]
    y, ck, *side_res = _pcall(
        body, name=name, grid=(nt,),
        in_specs=in_specs + [hbm] * n_side,
        out_specs=[pl.BlockSpec((NH, nb, tb, DH), lambda j: (0, 0, j, 0)),
                   pl.BlockSpec((ncb, nch, DH, DH), lambda j: (j, 0, 0, 0))] + [hbm] * n_side,
        out_shape=[jax.ShapeDtypeStruct((NH, nb, t, DH), BF16),
                   jax.ShapeDtypeStruct((t // CH, nch, DH, DH), F32)] + (side.out_shapes if side else []),
        scratch_shapes=[pltpu.VMEM((nch, DH, DH), F32)] + (_comm_scratch(side) if side else []),
        compiler_params=_cparams(("arbitrary",)),
    )(*[a.reshape(a.shape[0], nb, t, DH) for a, _, _ in ins], *prm, *cst, *(side.operands if side else []))
    return y.reshape(NH, nb * t, DH), ck, side_res


def _mixer_bwd(chunk_fn, name, ins, prm, cst, ck, dy, dy_block, outs, routes, nb, t, first_fn=None, side=None):
    tb = _time_block(t)
    nt, ncb, nch = t // tb, tb // CH, nb * NH
    n_in, n_prm, n_cst, n_out = len(ins), len(prm), len(cst), len(outs)
    n_main, n_side = n_in + n_prm + n_cst + 2, len(side.operands) if side else 0

    def body(*refs):
        in_refs = refs[:n_in]
        prm_refs = refs[n_in:n_in + n_prm]
        cst_refs = refs[n_in + n_prm:n_in + n_prm + n_cst]
        ck_ref, dy_ref = refs[n_main - 2:n_main]
        side_in = refs[n_main:n_main + n_side]
        rest = refs[n_main + n_side:]
        out_refs = rest[:n_out]
        dprm_refs = rest[n_out:n_out + n_prm]
        side_out = rest[n_out + n_prm:n_out + n_prm + n_side]
        ds_scr = rest[n_out + n_prm + n_side]
        sems = rest[n_out + n_prm + n_side + 1:]
        step_t = pl.program_id(0)

        if side is not None:
            @pl.when(step_t == 0)
            def _():
                _comm_start(side.copies(side_in, side_out, sems))

        @pl.when(step_t == 0)
        def _():
            ds_scr[...] = jnp.zeros_like(ds_scr)
            for r in dprm_refs:
                r[...] = jnp.zeros_like(r)

        def chunk(c, i):
            cst_v = [jnp.tile(r[...], (nb, 1, 1)) for r in cst_refs]
            _, vjp = jax.vjp(lambda p, x, s: chunk_fn(p, cst_v, x, s),
                             [jnp.tile(r[...], (nb, 1, 1)) for r in prm_refs],
                             [_load_chunk(r, i) for r in in_refs], ck_ref[c])
            dy_c = jnp.concatenate([dy_ref[:, b, pl.ds(i, CH), :] for b in range(nb)], axis=0)
            d_prm, d_ins, d_s = vjp((dy_c, ds_scr[...]))
            for (oi, g0), r, g in zip(routes, in_refs, d_ins):
                o_ref = out_refs[oi]
                if r.shape[0] == NH:
                    for b in range(nb):
                        o_ref[g0:g0 + NH, b, pl.ds(i, CH), :] = g[b * NH:(b + 1) * NH].astype(o_ref.dtype)
                else:
                    o_ref[g0, :, pl.ds(i, CH), :] = g.astype(o_ref.dtype)
            for r, g in zip(dprm_refs, d_prm):
                r[...] += g
            ds_scr[...] = d_s

        def first_chunk_of(c):
            one, h, b = pl.ds(c, 1), pl.ds(c % NH, 1), c // NH
            cst_v = [r[h] for r in cst_refs]
            _, vjp = jax.vjp(lambda p, x, s: first_fn(p, cst_v, x, s), [r[h] for r in prm_refs],
                             [_load_chunk_of(r, c, 0) for r in in_refs], ck_ref[0, one])
            d_prm, d_ins, d_s = vjp((dy_ref[h, b, pl.ds(0, CH), :], ds_scr[one]))
            for (oi, g0), r, g in zip(routes, in_refs, d_ins):
                o_ref = out_refs[oi]
                if r.shape[0] == NH:
                    o_ref.at[g0:g0 + NH][h, b, pl.ds(0, CH), :] = g.astype(o_ref.dtype)
                else:
                    o_ref[g0, pl.ds(b, 1), pl.ds(0, CH), :] += g.astype(o_ref.dtype)
            for r, g in zip(dprm_refs, d_prm):
                r[one] += g
            ds_scr[one] = d_s

        def first_chunk():
            for (oi, g0), r in zip(routes, in_refs):
                if r.shape[0] != NH:
                    out_refs[oi][g0, :, pl.ds(0, CH), :] = jnp.zeros((nb, CH, DH), out_refs[oi].dtype)
            _each_chain(nch, first_chunk_of)

        def step(j, carry):
            c = ncb - 1 - j
            chunk(c, pl.multiple_of(c * CH, CH))
            return carry

        lax.fori_loop(0, ncb - 1, step, 0)
        if first_fn is None:
            chunk(0, 0)
        else:
            @pl.when(step_t == nt - 1)
            def _():
                first_chunk()

            @pl.when(step_t != nt - 1)
            def _():
                chunk(0, 0)

        if side is not None:
            @pl.when(step_t == nt - 1)
            def _():
                _comm_wait(side.copies(side_in, side_out, sems))

    def back(j):
        return nt - 1 - j

    hbm = pl.BlockSpec(memory_space=pl.ANY)
    in_specs = [pl.BlockSpec((ng, nb, tb, DH), (lambda j, bi=bi: (bi, 0, back(j), 0))) for _, ng, bi in ins]
    in_specs += [pl.BlockSpec(p.shape, lambda j: (0, 0, 0)) for p in list(prm) + list(cst)]
    in_specs += [pl.BlockSpec((ncb, nch, DH, DH), lambda j: (back(j), 0, 0, 0)),
                 pl.BlockSpec((NH, nb, tb, DH), lambda j: (dy_block, 0, back(j), 0))]
    out_specs = [pl.BlockSpec((ng, nb, tb, DH), lambda j: (0, 0, back(j), 0)) for ng, _ in outs]
    out_specs += [pl.BlockSpec((nch,) + p.shape[1:], lambda j: (0, 0, 0)) for p in prm]
    out_shape = [jax.ShapeDtypeStruct((ng, nb, t, DH), dt) for ng, dt in outs]
    out_shape += [jax.ShapeDtypeStruct((nch,) + p.shape[1:], F32) for p in prm]
    res = _pcall(
        body, name=name, grid=(nt,),
        in_specs=in_specs + [hbm] * n_side, out_specs=out_specs + [hbm] * n_side,
        out_shape=out_shape + (side.out_shapes if side else []),
        scratch_shapes=[pltpu.VMEM((nch, DH, DH), F32)] + (_comm_scratch(side) if side else []),
        compiler_params=_cparams(("arbitrary",)),
    )(*[a.reshape(a.shape[0], nb, t, DH) for a, _, _ in ins], *prm, *cst, ck, dy.reshape(dy.shape[0], nb, t, DH),
      *(side.operands if side else []))
    d_outs = [o.reshape(o.shape[0], nb * t, DH) for o in res[:n_out]]
    d_prm = [g.reshape((nb,) + p.shape) for g, p in zip(res[n_out:n_out + n_prm], prm)]
    return d_outs, d_prm, res[n_out + n_prm:]


def _shift_down(x, s):
    if s == 0:
        return x
    row = lax.broadcasted_iota(jnp.int32, x.shape, 0)
    return jnp.where(row < s, 0.0, pltpu.roll(x, s, 0))


def _shift_up(x, s):
    if s == 0:
        return x
    t = x.shape[0]
    row = lax.broadcasted_iota(jnp.int32, x.shape, 0)
    return jnp.where(row >= t - s, 0.0, pltpu.roll(x, t - s, 0))


def _conv_fwd(p, g0, ng, w, nb, t, name):
    taps = w.shape[1]

    def body(x_ref, w_ref, y_ref):
        x = x_ref[...]
        acc = w_ref[taps - 1:taps, :] * x
        for i in range(taps - 1):
            acc = acc + w_ref[i:i + 1, :] * _shift_down(x, taps - 1 - i)
        y_ref[...] = acc

    return _pcall(
        body, name=name, grid=(ng, nb),
        in_specs=[pl.BlockSpec((None, t, DH), lambda g, b: (g0 + g, b, 0)),
                  pl.BlockSpec((None, taps, DH), lambda g, b: (g, 0, 0))],
        out_specs=pl.BlockSpec((None, t, DH), lambda g, b: (g, b, 0)),
        out_shape=jax.ShapeDtypeStruct((ng, nb * t, DH), F32),
        compiler_params=_cparams(("parallel", "parallel")),
    )(p, w)


def _conv_bwd(p, g0, ng, w, dy, nb, t, name):
    taps = w.shape[1]

    def body(x_ref, w_ref, dy_ref, dx_ref, dw_ref):
        x = x_ref[...]
        d = dy_ref[...]
        acc = w_ref[taps - 1:taps, :] * d
        dw_ref[taps - 1:taps, :] = jnp.sum(d * x, 0, keepdims=True)
        for i in range(taps - 1):
            s = taps - 1 - i
            acc = acc + w_ref[i:i + 1, :] * _shift_up(d, s)
            dw_ref[i:i + 1, :] = jnp.sum(d * _shift_down(x, s), 0, keepdims=True)
        dx_ref[...] = acc.astype(BF16)

    return _pcall(
        body, name=name, grid=(ng, nb),
        in_specs=[pl.BlockSpec((None, t, DH), lambda g, b: (g0 + g, b, 0)),
                  pl.BlockSpec((None, taps, DH), lambda g, b: (g, 0, 0)),
                  pl.BlockSpec((None, t, DH), lambda g, b: (g, b, 0))],
        out_specs=[pl.BlockSpec((None, t, DH), lambda g, b: (g, b, 0)),
                   pl.BlockSpec((None, None, taps, DH), lambda g, b: (g, b, 0, 0))],
        out_shape=[jax.ShapeDtypeStruct((ng, nb * t, DH), BF16),
                   jax.ShapeDtypeStruct((ng, nb, taps, DH), F32)],
        compiler_params=_cparams(("parallel", "parallel")),
    )(p, w, dy)


def _mix_group(g):
    return jnp.where(g < 16, G_RWKV + g, G_RWKV_WD + g - 16)


def _mix_fwd(p, mu, nb, t, name):
    def body(x_ref, mu_ref, y_ref):
        x = x_ref[...]
        y_ref[...] = x + mu_ref[...] * (_shift_down(x, 1) - x)

    return _pcall(
        body, name=name, grid=(18, nb),
        in_specs=[pl.BlockSpec((None, t, DH), lambda g, b: (_mix_group(g), b, 0)),
                  pl.BlockSpec((None, 1, DH), lambda g, b: (g, 0, 0))],
        out_specs=pl.BlockSpec((None, t, DH), lambda g, b: (g, b, 0)),
        out_shape=jax.ShapeDtypeStruct((18, nb * t, DH), F32),
        compiler_params=_cparams(("parallel", "parallel")),
    )(p, mu)


def _mix_bwd(p, mu, dy, nb, t, name):
    def body(x_ref, mu_ref, dy_ref, dx_ref, dmu_ref):
        x = x_ref[...]
        muv = mu_ref[...]
        d = dy_ref[...]
        dx_ref[...] = (d * (1.0 - muv) + _shift_up(d * muv, 1)).astype(BF16)
        dmu_ref[...] = jnp.sum(d * (_shift_down(x, 1) - x), 0, keepdims=True)

    return _pcall(
        body, name=name, grid=(18, nb),
        in_specs=[pl.BlockSpec((None, t, DH), lambda g, b: (_mix_group(g), b, 0)),
                  pl.BlockSpec((None, 1, DH), lambda g, b: (g, 0, 0)),
                  pl.BlockSpec((None, t, DH), lambda g, b: (g, b, 0))],
        out_specs=[pl.BlockSpec((None, t, DH), lambda g, b: (g, b, 0)),
                   pl.BlockSpec((None, None, 1, DH), lambda g, b: (g, b, 0, 0))],
        out_shape=[jax.ShapeDtypeStruct((18, nb * t, DH), BF16),
                   jax.ShapeDtypeStruct((18, nb, 1, DH), F32)],
        compiler_params=_cparams(("parallel", "parallel")),
    )(p, mu, dy)


def _sc_fwd(p, w, nb, t, name):
    def body(p_ref, w_ref, y_ref):
        u = p_ref[1] * p_ref[2]
        conv = w_ref[2:3, :] * u + w_ref[1:2, :] * _shift_down(u, 1) + w_ref[0:1, :] * _shift_down(u, 2)
        y_ref[...] = (p_ref[0] * conv * _silu(p_ref[3])).astype(BF16)

    return _pcall(
        body, name=name, grid=(NH, nb),
        in_specs=[pl.BlockSpec((4, t, DH), lambda j, b: (G_SC // 4 + j, b, 0)),
                  pl.BlockSpec((None, SC_TAPS, DH), lambda j, b: (j, 0, 0))],
        out_specs=pl.BlockSpec((None, t, DH), lambda j, b: (j, b, 0)),
        out_shape=jax.ShapeDtypeStruct((NH, nb * t, DH), BF16),
        compiler_params=_cparams(("parallel", "parallel")),
    )(p, w)


def _sc_bwd(p, w, dy, nb, t, name):
    def body(p_ref, w_ref, dy_ref, dp_ref, dw_ref):
        bg, cg, xg, z = p_ref[0], p_ref[1], p_ref[2], p_ref[3]
        d = dy_ref[...]
        u = cg * xg
        u1 = _shift_down(u, 1)
        u2 = _shift_down(u, 2)
        conv = w_ref[2:3, :] * u + w_ref[1:2, :] * u1 + w_ref[0:1, :] * u2
        sg = jax.nn.sigmoid(z)
        sz = z * sg
        dp_ref[0] = (d * conv * sz).astype(BF16)
        dp_ref[3] = (d * bg * conv * (sg * (1.0 + z * (1.0 - sg)))).astype(BF16)
        dconv = d * bg * sz
        du = w_ref[2:3, :] * dconv + w_ref[1:2, :] * _shift_up(dconv, 1) + w_ref[0:1, :] * _shift_up(dconv, 2)
        dp_ref[1] = (du * xg).astype(BF16)
        dp_ref[2] = (du * cg).astype(BF16)
        dw_ref[2:3, :] = jnp.sum(dconv * u, 0, keepdims=True)
        dw_ref[1:2, :] = jnp.sum(dconv * u1, 0, keepdims=True)
        dw_ref[0:1, :] = jnp.sum(dconv * u2, 0, keepdims=True)

    return _pcall(
        body, name=name, grid=(NH, nb),
        in_specs=[pl.BlockSpec((4, t, DH), lambda j, b: (G_SC // 4 + j, b, 0)),
                  pl.BlockSpec((None, SC_TAPS, DH), lambda j, b: (j, 0, 0)),
                  pl.BlockSpec((None, t, DH), lambda j, b: (8 + j, b, 0))],
        out_specs=[pl.BlockSpec((4, t, DH), lambda j, b: (j, b, 0)),
                   pl.BlockSpec((None, None, SC_TAPS, DH), lambda j, b: (j, b, 0, 0))],
        out_shape=[jax.ShapeDtypeStruct((4 * NH, nb * t, DH), BF16),
                   jax.ShapeDtypeStruct((NH, nb, SC_TAPS, DH), F32)],
        compiler_params=_cparams(("parallel", "parallel")),
    )(p, w, dy)


def _row_tile(n):
    return 1024 if n % 1024 == 0 else n


def _regroup_in(w_all, name):
    tr = 256
    gs = GROUPS_PER_STEP

    def body(w_ref, o_ref):
        for g in _PADDED_GROUPS:
            o_ref[g // gs, :, DH * (g % gs):DH * (g % gs + 1)] = jnp.zeros((tr, DH), BF16)
        for g, a, d, off, ln in _SEGMENTS:
            lane = DH * (g % gs) + a
            o_ref[g // gs, :, lane:lane + ln] = w_ref[d, :, off:off + ln].astype(BF16)

    return _pcall(
        body, name=name, grid=(D_MODEL // tr,),
        in_specs=[pl.BlockSpec((N_DEV, tr, SHARD_COLS), lambda i: (0, i, 0))],
        out_specs=pl.BlockSpec((N_GROUPS // gs, tr, gs * DH), lambda i: (0, i, 0)),
        out_shape=jax.ShapeDtypeStruct((N_GROUPS // gs, D_MODEL, gs * DH), BF16),
        compiler_params=_cparams(("parallel",)),
    )(w_all)


def _regroup_out(dwg, name):
    tr = 256
    gs = GROUPS_PER_STEP

    def body(g_ref, o_ref):
        for g, a, d, off, ln in _SEGMENTS:
            lane = DH * (g % gs) + a
            o_ref[d, :, off:off + ln] = g_ref[g // gs, :, lane:lane + ln]

    return _pcall(
        body, name=name, grid=(D_MODEL // tr,),
        in_specs=[pl.BlockSpec((N_GROUPS // gs, tr, gs * DH), lambda i: (0, i, 0))],
        out_specs=pl.BlockSpec((N_DEV, tr, SHARD_COLS), lambda i: (0, i, 0)),
        out_shape=jax.ShapeDtypeStruct((N_DEV, D_MODEL, SHARD_COLS), F32),
        compiler_params=_cparams(("parallel",)),
    )(dwg)


def _norm_proj(x, pre_w, w_g, name):
    n = x.shape[0]
    tm = _row_tile(n)
    gs = GROUPS_PER_STEP

    def body(x_ref, pw_ref, w_ref, h_ref, p_ref):
        @pl.when(pl.program_id(1) == 0)
        def _():
            xv = x_ref[...]
            h = xv * lax.rsqrt(jnp.mean(xv * xv, -1, keepdims=True) + EPS) * pw_ref[...]
            h_ref[...] = h.astype(BF16)

        r = jnp.dot(h_ref[...], w_ref[...], preferred_element_type=F32)
        for k in range(gs):
            p_ref[k] = r[:, DH * k:DH * (k + 1)]

    return _pcall(
        body, name=name, grid=(n // tm, N_GROUPS // gs),
        in_specs=[pl.BlockSpec((tm, D_MODEL), lambda i, j: (i, 0)),
                  pl.BlockSpec((1, D_MODEL), lambda i, j: (0, 0)),
                  pl.BlockSpec((None, D_MODEL, gs * DH), lambda i, j: (j, 0, 0))],
        out_specs=[pl.BlockSpec((tm, D_MODEL), lambda i, j: (i, 0)),
                   pl.BlockSpec((gs, tm, DH), lambda i, j: (j, i, 0))],
        out_shape=[jax.ShapeDtypeStruct((n, D_MODEL), BF16),
                   jax.ShapeDtypeStruct((N_GROUPS, n, DH), F32)],
        compiler_params=_cparams(("parallel", "arbitrary")),
    )(x, pre_w, w_g)


def _out_proj_norm(ys, wout_g, x, post_w, name):
    n = x.shape[0]
    tm = _row_tile(n)

    def body(y0, y1, y2, y3, w_ref, x_ref, pw_ref, out_ref, xn_ref):
        acc = jnp.zeros((tm, D_MODEL), F32)
        for m, yr in enumerate((y0, y1, y2, y3)):
            for h in range(NH):
                acc = acc + jnp.dot(yr[h], w_ref[m * NH + h], preferred_element_type=F32)
        out_ref[...] = acc
        xn_ref[...] = x_ref[...] + acc * lax.rsqrt(jnp.mean(acc * acc, -1, keepdims=True) + EPS) * pw_ref[...]

    yspec = pl.BlockSpec((NH, tm, DH), lambda i: (0, i, 0))
    rows = pl.BlockSpec((tm, D_MODEL), lambda i: (i, 0))
    return _pcall(
        body, name=name, grid=(n // tm,),
        in_specs=[yspec] * 4 + [pl.BlockSpec((4 * NH, DH, D_MODEL), lambda i: (0, 0, 0)), rows,
                                pl.BlockSpec((1, D_MODEL), lambda i: (0, 0))],
        out_specs=[rows, rows],
        out_shape=[jax.ShapeDtypeStruct((n, D_MODEL), F32)] * 2,
        compiler_params=_cparams(("parallel",)),
    )(*ys, wout_g, x, post_w)


def _loss_grad(x, tgt, name):
    n = x.shape[0]
    tm = _row_tile(n)

    def body(x_ref, t_ref, dx_ref, l_ref):
        @pl.when(pl.program_id(0) == 0)
        def _():
            l_ref[...] = jnp.zeros_like(l_ref)

        e = x_ref[...] - t_ref[...]
        dx_ref[...] = e * (1.0 / D_MODEL)
        l_ref[...] += jnp.sum(jnp.sum(e * e, -1, keepdims=True), 0, keepdims=True) * (0.5 / D_MODEL)

    rows = pl.BlockSpec((tm, D_MODEL), lambda i: (i, 0))
    return _pcall(
        body, name=name, grid=(n // tm,),
        in_specs=[rows, rows],
        out_specs=[rows, pl.BlockSpec((1, 128), lambda i: (0, 0))],
        out_shape=[jax.ShapeDtypeStruct((n, D_MODEL), F32), jax.ShapeDtypeStruct((1, 128), F32)],
        compiler_params=_cparams(("arbitrary",)),
    )(x, tgt)


def _rmsnorm_bwd(xv, w, d):
    r = lax.rsqrt(jnp.mean(xv * xv, -1, keepdims=True) + EPS)
    xh = xv * r
    dxh = d * w
    dx = r * (dxh - xh * jnp.mean(dxh * xh, -1, keepdims=True))
    return dx, d * xh


def _post_bwd(dxn, out, post_w, wout_g, name):
    n = dxn.shape[0]
    tm = _row_tile(n)

    def body(d_ref, o_ref, pw_ref, w_ref, do_ref, dy_ref, dpw_ref):
        @pl.when(pl.program_id(0) == 0)
        def _():
            dpw_ref[...] = jnp.zeros_like(dpw_ref)

        dout, dw_rows = _rmsnorm_bwd(o_ref[...], pw_ref[...], d_ref[...])
        dpw_ref[...] += jnp.sum(dw_rows, 0, keepdims=True)
        db = dout.astype(BF16)
        do_ref[...] = db
        for g in range(4 * NH):
            dy_ref[g] = lax.dot_general(db, w_ref[g], (((1,), (1,)), ((), ())), preferred_element_type=F32)

    rows = pl.BlockSpec((tm, D_MODEL), lambda i: (i, 0))
    vec = pl.BlockSpec((1, D_MODEL), lambda i: (0, 0))
    return _pcall(
        body, name=name, grid=(n // tm,),
        in_specs=[rows, rows, vec, pl.BlockSpec((4 * NH, DH, D_MODEL), lambda i: (0, 0, 0))],
        out_specs=[rows, pl.BlockSpec((4 * NH, tm, DH), lambda i: (0, i, 0)), vec],
        out_shape=[jax.ShapeDtypeStruct((n, D_MODEL), BF16),
                   jax.ShapeDtypeStruct((4 * NH, n, DH), F32),
                   jax.ShapeDtypeStruct((1, D_MODEL), F32)],
        compiler_params=_cparams(("arbitrary",)),
    )(dxn, out, post_w, wout_g)


def _dwout(ys, dout, name):
    n = dout.shape[0]
    tm = _row_tile(n)

    def body(y0, y1, y2, y3, d_ref, dw_ref):
        @pl.when(pl.program_id(0) == 0)
        def _():
            dw_ref[...] = jnp.zeros_like(dw_ref)

        d = d_ref[...]
        for m, yr in enumerate((y0, y1, y2, y3)):
            for h in range(NH):
                dw_ref[m * NH + h] += lax.dot_general(yr[h], d, (((0,), (0,)), ((), ())),
                                                      preferred_element_type=F32)

    yspec = pl.BlockSpec((NH, tm, DH), lambda i: (0, i, 0))
    return _pcall(
        body, name=name, grid=(n // tm,),
        in_specs=[yspec] * 4 + [pl.BlockSpec((tm, D_MODEL), lambda i: (i, 0))],
        out_specs=pl.BlockSpec((4 * NH, DH, D_MODEL), lambda i: (0, 0, 0)),
        out_shape=jax.ShapeDtypeStruct((4 * NH, DH, D_MODEL), F32),
        compiler_params=_cparams(("arbitrary",)),
    )(*ys, dout)


def _source_specs(sources, rows_first):
    gs = GROUPS_PER_STEP
    spans, specs, j0 = [], [], 0
    for a in sources:
        nblk = a.shape[0] // gs
        spans.append((j0, j0 + nblk))
        shape = (gs, _row_tile(a.shape[1]), DH)

        def blk(j, j0=j0, nblk=nblk):
            return jnp.clip(j - j0, 0, nblk - 1)

        if rows_first:
            specs.append(pl.BlockSpec(shape, (lambda i, j, blk=blk: (blk(j), i, 0))))
        else:
            specs.append(pl.BlockSpec(shape, (lambda j, i, blk=blk: (blk(j), i, 0))))
        j0 += nblk
    return spans, specs


def _dh_prenorm_bwd(sources, w_g, x, pre_w, dxn, name, side=None):
    n = x.shape[0]
    tm = _row_tile(n)
    gs = GROUPS_PER_STEP
    nj = N_GROUPS // gs
    ni = n // tm
    spans, src_specs = _source_specs(sources, True)
    ns = len(sources)
    n_side = len(side.operands) if side else 0

    def body(*refs):
        src = refs[:ns]
        w_ref, x_ref, pw_ref, d_ref = refs[ns:ns + 4]
        side_in = refs[ns + 4:ns + 4 + n_side]
        dx_ref, dpw_ref = refs[ns + 4 + n_side:ns + 6 + n_side]
        side_out = refs[ns + 6 + n_side:ns + 6 + 2 * n_side]
        acc = refs[ns + 6 + 2 * n_side]
        sems = refs[ns + 7 + 2 * n_side:]
        i, j = pl.program_id(0), pl.program_id(1)

        if side is not None:
            @pl.when((i == 0) & (j == 0))
            def _():
                _comm_start(side.copies(side_in, side_out, sems))

        @pl.when((i == 0) & (j == 0))
        def _():
            dpw_ref[...] = jnp.zeros_like(dpw_ref)

        @pl.when(j == 0)
        def _():
            acc[...] = jnp.zeros_like(acc)

        for s_ref, (lo, hi) in zip(src, spans):
            @pl.when((j >= lo) & (j < hi))
            def _(s_ref=s_ref):
                four = jnp.concatenate([s_ref[k] for k in range(gs)], axis=-1)
                acc[...] += lax.dot_general(four, w_ref[...], (((1,), (1,)), ((), ())), preferred_element_type=F32)

        @pl.when(j == nj - 1)
        def _():
            dx, dw_rows = _rmsnorm_bwd(x_ref[...], pw_ref[...], acc[...])
            dx_ref[...] = d_ref[...] + dx
            dpw_ref[...] += jnp.sum(dw_rows, 0, keepdims=True)

        if side is not None:
            @pl.when((i == ni - 1) & (j == nj - 1))
            def _():
                _comm_wait(side.copies(side_in, side_out, sems))

    hbm = pl.BlockSpec(memory_space=pl.ANY)
    rows = pl.BlockSpec((tm, D_MODEL), lambda i, j: (i, 0))
    vec = pl.BlockSpec((1, D_MODEL), lambda i, j: (0, 0))
    dx, dpw, *side_res = _pcall(
        body, name=name, grid=(ni, nj),
        in_specs=src_specs + [pl.BlockSpec((None, D_MODEL, gs * DH), lambda i, j: (j, 0, 0)), rows, vec, rows]
        + [hbm] * n_side,
        out_specs=[rows, vec] + [hbm] * n_side,
        out_shape=[jax.ShapeDtypeStruct((n, D_MODEL), F32), jax.ShapeDtypeStruct((1, D_MODEL), F32)]
        + (side.out_shapes if side else []),
        scratch_shapes=[pltpu.VMEM((tm, D_MODEL), F32)] + (_comm_scratch(side) if side else []),
        compiler_params=_cparams(("arbitrary", "arbitrary")),
    )(*sources, w_g, x, pre_w, dxn, *(side.operands if side else []))
    return dx, dpw, side_res


def _dwin(hb, sources, name):
    n = hb.shape[0]
    tm = _row_tile(n)
    gs = GROUPS_PER_STEP
    ni, nj = n // tm, N_GROUPS // gs
    spans, src_specs = _source_specs(sources, True)
    ns = len(sources)

    def body(*refs):
        h_ref = refs[0]
        src = refs[1:1 + ns]
        out_ref, acc, sem = refs[1 + ns:]
        i, j = pl.program_id(0), pl.program_id(1)

        @pl.when((i == 0) & (j == 0))
        def _():
            acc[...] = jnp.zeros_like(acc)

        h = h_ref[...]
        for s_ref, (lo, hi) in zip(src, spans):
            @pl.when((j >= lo) & (j < hi))
            def _(s_ref=s_ref):
                four = jnp.concatenate([s_ref[k] for k in range(gs)], axis=-1)
                acc[j] += jnp.dot(h, four, preferred_element_type=F32)

        @pl.when((i == ni - 1) & (j == nj - 1))
        def _():
            done = pltpu.make_async_copy(acc, out_ref, sem)
            done.start()
            done.wait()

    return _pcall(
        body, name=name, grid=(ni, nj),
        in_specs=[pl.BlockSpec((D_MODEL, tm), lambda i, j: (0, i))] + src_specs,
        out_specs=pl.BlockSpec(memory_space=pl.ANY),
        out_shape=jax.ShapeDtypeStruct((nj, D_MODEL, gs * DH), F32),
        scratch_shapes=[pltpu.VMEM((nj, D_MODEL, gs * DH), F32), pltpu.SemaphoreType.DMA],
        compiler_params=_cparams(("arbitrary", "arbitrary")),
    )(jnp.transpose(hb), *sources)


def _adamw_math(w, g, m, v):
    c1 = 1.0 - ADAM_B1 ** ADAM_STEP
    c2 = 1.0 - ADAM_B2 ** ADAM_STEP
    nm = ADAM_B1 * m + (1.0 - ADAM_B1) * g
    nv = ADAM_B2 * v + (1.0 - ADAM_B2) * (g * g)
    return -ADAM_LR * ((nm / c1) / (jnp.sqrt(nv / c2) + ADAM_EPS) + ADAM_WD * w), nm, nv


def _adamw(w, g, m, v, name):
    r, c = w.shape
    tr = 256 if r % 256 == 0 else r

    def body(w_ref, g_ref, m_ref, v_ref, d_ref, nm_ref, nv_ref):
        d_ref[...], nm_ref[...], nv_ref[...] = _adamw_math(w_ref[...], g_ref[...], m_ref[...], v_ref[...])

    spec = pl.BlockSpec((tr, c), lambda i: (i, 0))
    return _pcall(
        body, name=name, grid=(r // tr,),
        in_specs=[spec] * 4, out_specs=[spec] * 3,
        out_shape=[jax.ShapeDtypeStruct((r, c), F32)] * 3,
        compiler_params=_cparams(("parallel",)),
    )(w, g, m, v)


def _sum_adamw(parts, w, m, v, name):
    r, c = w.shape
    tr = 128 if r % 128 == 0 else r

    def body(p_ref, w_ref, m_ref, v_ref, g_ref, d_ref, nm_ref, nv_ref):
        g = p_ref[0]
        for k in range(1, N_DEV):
            g = g + p_ref[k]
        g_ref[...] = g
        d_ref[...], nm_ref[...], nv_ref[...] = _adamw_math(w_ref[...], g, m_ref[...], v_ref[...])

    spec = pl.BlockSpec((tr, c), lambda i: (i, 0))
    return _pcall(
        body, name=name, grid=(r // tr,),
        in_specs=[pl.BlockSpec((N_DEV, tr, c), lambda i: (0, i, 0))] + [spec] * 3, out_specs=[spec] * 4,
        out_shape=[jax.ShapeDtypeStruct((r, c), F32)] * 4,
        compiler_params=_cparams(("parallel",)),
    )(parts, w, m, v)


def _me():
    return lax.axis_index("x"), lax.axis_index("y"), lax.axis_index("c")


def _flat(x, y, c):
    return 4 * x + 2 * y + c


def _peer(k):
    x, y, c = _me()
    return (x ^ ((k >> 2) & 1), y ^ ((k >> 1) & 1), c ^ (k & 1))


def _gather_plan(blocks):
    def copies(x_refs, out_refs, sems):
        send_sems, recv_sems, local_sems = sems
        me = _flat(*_me())
        local = [pltpu.make_async_copy(x, o.at[me], local_sems.at[a]) for a, (x, o) in enumerate(zip(x_refs, out_refs))]
        outgoing, incoming = [], []
        for k in range(1, N_DEV):
            src = _flat(*_peer(k))
            for a, (x, o) in enumerate(zip(x_refs, out_refs)):
                for slot, group in ((me, outgoing), (src, incoming)):
                    group.append(pltpu.make_async_remote_copy(
                        src_ref=x, dst_ref=o.at[slot], send_sem=send_sems.at[a, k - 1], recv_sem=recv_sems.at[a, k - 1],
                        device_id=_peer(k), device_id_type=MESH))
        return local, outgoing, incoming

    return _Comm(list(blocks), [jax.ShapeDtypeStruct((N_DEV,) + b.shape, b.dtype) for b in blocks], copies)


def _exchange_plan(sends):
    def copies(s_refs, out_refs, sems):
        send_sems, recv_sems, local_sems = sems
        me = _flat(*_me())
        local = [pltpu.make_async_copy(s.at[me], o.at[0], local_sems.at[i]) for i, (s, o) in enumerate(zip(s_refs, out_refs))]
        outgoing = []
        for k in range(1, N_DEV):
            to = _flat(*_peer(k))
            for i, (s, o) in enumerate(zip(s_refs, out_refs)):
                outgoing.append(pltpu.make_async_remote_copy(
                    src_ref=s.at[to], dst_ref=o.at[k], send_sem=send_sems.at[i, k - 1], recv_sem=recv_sems.at[i, k - 1],
                    device_id=_peer(k), device_id_type=MESH))
        return local, outgoing, outgoing

    return _Comm(list(sends), [jax.ShapeDtypeStruct(s.shape, s.dtype) for s in sends], copies)


def _comm_scratch(plan):
    n = len(plan.operands)
    return [pltpu.SemaphoreType.DMA((n, N_DEV - 1)), pltpu.SemaphoreType.DMA((n, N_DEV - 1)),
            pltpu.SemaphoreType.DMA((n,))]


def _comm_start(copies):
    local, outgoing, _ = copies
    for cp in local + outgoing:
        cp.start()


def _comm_wait(copies):
    local, outgoing, incoming = copies
    for cp in incoming:
        cp.wait_recv()
    for cp in outgoing:
        cp.wait_send()
    for cp in local:
        cp.wait()


def _run_comm(plan, name):
    n = len(plan.operands)

    def body(*refs):
        copies = plan.copies(refs[:n], refs[n:2 * n], refs[2 * n:])
        _comm_start(copies)
        _comm_wait(copies)

    return _pcall(
        body, name=name,
        in_specs=[pl.BlockSpec(memory_space=pl.ANY)] * n,
        out_specs=[pl.BlockSpec(memory_space=pl.ANY)] * n,
        out_shape=plan.out_shapes,
        scratch_shapes=_comm_scratch(plan),
    )(*plan.operands)


def _sum_slots(a, name):
    r = a.shape[1]

    def body(a_ref, o_ref):
        acc = a_ref[0]
        for d in range(1, N_DEV):
            acc = acc + a_ref[d]
        o_ref[...] = acc

    return _pcall(body, name=name, out_shape=jax.ShapeDtypeStruct((r, 128), F32), compiler_params=_cparams())(a)


def _all_reduce_small(blk, name):
    r = blk.shape[0]

    def body(x_ref, out_ref, gath, send_sems, recv_sems):
        me = _flat(*_me())
        gath[me] = x_ref[...]
        copies = []
        for k in range(1, N_DEV):
            cp = pltpu.make_async_remote_copy(
                src_ref=x_ref, dst_ref=gath.at[me],
                send_sem=send_sems.at[k - 1], recv_sem=recv_sems.at[k - 1],
                device_id=_peer(k), device_id_type=MESH)
            cp.start()
            copies.append(cp)
        for k in range(1, N_DEV):
            src = _flat(*_peer(k))
            pltpu.make_async_remote_copy(
                src_ref=x_ref, dst_ref=gath.at[src],
                send_sem=send_sems.at[k - 1], recv_sem=recv_sems.at[k - 1],
                device_id=_peer(k), device_id_type=MESH).wait_recv()
        for cp in copies:
            cp.wait_send()
        acc = gath[0]
        for d in range(1, N_DEV):
            acc = acc + gath[d]
        out_ref[...] = acc

    return _pcall(
        body, name=name,
        in_specs=[pl.BlockSpec(memory_space=pltpu.VMEM)],
        out_specs=pl.BlockSpec(memory_space=pltpu.VMEM),
        out_shape=jax.ShapeDtypeStruct((r, 128), F32),
        scratch_shapes=[pltpu.VMEM((N_DEV, r, 128), F32),
                        pltpu.SemaphoreType.DMA((N_DEV - 1,)), pltpu.SemaphoreType.DMA((N_DEV - 1,))],
    )(blk)


def _heads(vec):
    return vec.reshape(NH, 1, DH)


def _rep(vec4):
    return jnp.broadcast_to(vec4.reshape(NH, 1, 1), (NH, 1, DH))


def _onehot_lane(offset):
    m = np.zeros((NH, 1, DH), np.float32)
    for h in range(NH):
        m[h, 0, offset + h] = 1.0
    return jnp.asarray(m)


_TINY = (("gdn_conv_w", (DEPTH, 4, 96)), ("rwkv_w_up", (DEPTH, 64, 32)), ("rwkv_a_up", (DEPTH, 64, 32)),
         ("sc_conv_w", (DEPTH, 3, 32)))
_TINY_ROWS = -(-sum(int(np.prod(s)) for _, s in _TINY) // 1024) * 8


def _pack_rows(arrays, rows, fill=0.0):
    flat = jnp.concatenate([a.reshape(-1) for a in arrays])
    return jnp.pad(flat, (0, rows * 128 - flat.shape[0]), constant_values=fill).reshape(rows, 128)


def _unpack_rows(p, named_shapes):
    lead = p.shape[:-2]
    flat = p.reshape(lead + (-1,))
    out, o = {}, 0
    for n, s in named_shapes:
        size = int(np.prod(s))
        out[n] = flat[..., o:o + size].reshape(lead + tuple(s))
        o += size
    return out


def _gather_last(a):
    return jnp.transpose(a, (1, 0, 2)).reshape(a.shape[1], -1)


def _split_last(a):
    r, c8 = a.shape
    return jnp.transpose(a.reshape(r, N_DEV, c8 // N_DEV), (1, 0, 2))


_SMALL = (("pre_norm_w", (DEPTH, 1024)), ("gdn_a_log", (DEPTH, 4)), ("gdn_dt_bias", (DEPTH, 4)),
          ("gdn_norm_w", (DEPTH, 64)), ("rwkv_mu", (DEPTH, 1152)), ("rwkv_w0", (DEPTH, 256)),
          ("rwkv_a0", (DEPTH, 256)), ("rwkv_k_k", (DEPTH, 256)), ("rwkv_k_a", (DEPTH, 256)),
          ("rwkv_r_k", (DEPTH, 256)), ("rwkv_ln_w", (DEPTH, 256)), ("rwkv_ln_b", (DEPTH, 256)),
          ("gla_a_up", (DEPTH, 16, 128)), ("gla_a_bias", (DEPTH, 128)), ("gla_norm_w", (DEPTH, 64)),
          ("post_norm_w", (DEPTH, 1024)), ("loss", ()))
_SMALL_ROWS = -(-sum(int(np.prod(s)) for _, s in _SMALL) // 1024) * 8


def _big_weights(w_in_all, w_out_all, l):
    return dict(w_g=_regroup_in(w_in_all, f"regroup_in{l}"),
                wout_g=w_out_all.reshape(4 * NH, DH, D_MODEL).astype(BF16))


def _layer_params(wts, tiny, l):
    conv = _gather_last(tiny["gdn_conv_w"][:, l])
    q = {}
    q["gdn_conv"] = jnp.transpose(conv.reshape(GDN_TAPS, 12, DH), (1, 0, 2))
    q["gdn_prm"] = [_rep(wts["gdn_a_log"][l]), _rep(wts["gdn_dt_bias"][l]),
                    jnp.broadcast_to(wts["gdn_norm_w"][l].reshape(1, 1, DH), (NH, 1, DH))]
    q["gdn_cst"] = [_onehot_lane(0), _onehot_lane(NH)]
    q["rwkv_mu"] = wts["rwkv_mu"][l].reshape(18, 1, DH)
    w_up = jnp.transpose(_gather_last(tiny["rwkv_w_up"][:, l]).reshape(64, NH, DH), (1, 0, 2))
    a_up = jnp.transpose(_gather_last(tiny["rwkv_a_up"][:, l]).reshape(64, NH, DH), (1, 0, 2))
    q["rwkv_prm"] = [_heads(wts["rwkv_w0"][l]), w_up, _heads(wts["rwkv_a0"][l]), a_up,
                     _heads(wts["rwkv_k_k"][l]), _heads(wts["rwkv_k_a"][l]), _heads(wts["rwkv_r_k"][l]),
                     _heads(wts["rwkv_ln_w"][l]), _heads(wts["rwkv_ln_b"][l])]
    sc = _gather_last(tiny["sc_conv_w"][:, l])
    q["sc_conv"] = jnp.transpose(sc.reshape(SC_TAPS, NH, DH), (1, 0, 2))
    gla_up = jnp.transpose(wts["gla_a_up"][l].reshape(16, NH, GLA_HEAD_K), (1, 0, 2))
    gla_up = jnp.pad(gla_up, ((0, 0), (0, DH - 16), (0, DH - GLA_HEAD_K)))
    gla_b = jnp.pad(wts["gla_a_bias"][l].reshape(NH, 1, GLA_HEAD_K), ((0, 0), (0, 0), (0, DH - GLA_HEAD_K)))
    q["gla_prm"] = [gla_up, gla_b, jnp.broadcast_to(wts["gla_norm_w"][l].reshape(1, 1, DH), (NH, 1, DH))]
    q["pre_w"] = wts["pre_norm_w"][l].reshape(1, D_MODEL)
    q["post_w"] = wts["post_norm_w"][l].reshape(1, D_MODEL)
    return q


def _mixer_inputs(p, cq, pm):
    gdn = [(cq, 4, 0), (cq, 4, 1), (cq, 4, 2), (p, 4, G_GDN // 4 + 3), (p, 1, G_GDN_AB)]
    rwkv = [(pm, 4, 0), (pm, 4, 1), (pm, 4, 2), (pm, 4, 3), (pm, 1, 16), (pm, 1, 17)]
    gla = [(p, 4, G_GLA // 4 + k) for k in range(4)] + [(p, 1, G_GLA_AD)]
    return gdn, rwkv, gla


def _layer_fwd(x, q, nb, t, l, side=None):
    hb, p = _norm_proj(x, q["pre_w"], q["w_g"], f"norm_proj{l}")
    cq = _conv_fwd(p, G_GDN, 12, q["gdn_conv"], nb, t, f"gdn_conv{l}")
    pm = _mix_fwd(p, q["rwkv_mu"], nb, t, f"rwkv_mix{l}")
    gdn_in, rwkv_in, gla_in = _mixer_inputs(p, cq, pm)
    y_gdn, ck_gdn, _ = _mixer_fwd(_gdn_chunk, f"gdn_fwd{l}", gdn_in, q["gdn_prm"], q["gdn_cst"], nb, t)
    y_rwkv, ck_rwkv, side_res = _mixer_fwd(_rwkv_chunk, f"rwkv_fwd{l}", rwkv_in, q["rwkv_prm"], [], nb, t,
                                           first_fn=_rwkv_first_chunk, side=side)
    y_sc = _sc_fwd(p, q["sc_conv"], nb, t, f"sc_fwd{l}")
    y_gla, ck_gla, _ = _mixer_fwd(_gla_chunk, f"gla_fwd{l}", gla_in, q["gla_prm"], [], nb, t)
    ys = (y_gdn, y_rwkv, y_sc, y_gla)
    out, xn = _out_proj_norm(ys, q["wout_g"], x, q["post_w"], f"out_proj{l}")
    saved = dict(x=x, hb=hb, p=p, cq=cq, pm=pm, ys=ys, out=out, ck=(ck_gdn, ck_rwkv, ck_gla))
    return xn, saved, side_res


def _layer_bwd(dxn, q, sv, nb, t, l, side=None, exchange_own=False):
    p, cq, pm, ys = sv["p"], sv["cq"], sv["pm"], sv["ys"]
    dout, dy, d_post = _post_bwd(dxn, sv["out"], q["post_w"], q["wout_g"], f"post_bwd{l}")
    d_wout = _dwout(ys, dout, f"dwout{l}").reshape(N_DEV, 128, D_MODEL)
    gdn_in, rwkv_in, gla_in = _mixer_inputs(p, cq, pm)
    ck_gdn, ck_rwkv, ck_gla = sv["ck"]
    g = {}

    (d_conv, dz, dab), (da_log, ddt, dnw), _ = _mixer_bwd(
        _gdn_chunk, f"gdn_bwd{l}", gdn_in, q["gdn_prm"], q["gdn_cst"], ck_gdn, dy, 0,
        [(12, F32), (4, BF16), (1, BF16)], [(0, 0), (0, 4), (0, 8), (1, 0), (2, 0)], nb, t)
    dconv_in, d_gconv = _conv_bwd(p, G_GDN, 12, q["gdn_conv"], d_conv, nb, t, f"gdn_conv_bwd{l}")
    g["gdn_conv_w"] = jnp.transpose(d_gconv.sum(1), (1, 0, 2)).reshape(GDN_TAPS, 768)
    g["gdn_a_log"] = da_log.sum((0, 2, 3))
    g["gdn_dt_bias"] = ddt.sum((0, 2, 3))
    g["gdn_norm_w"] = dnw.sum((0, 1, 2))

    (d_pm,), d_rprm, side_res = _mixer_bwd(
        _rwkv_chunk, f"rwkv_bwd{l}", rwkv_in, q["rwkv_prm"], [], ck_rwkv, dy, 1,
        [(18, F32)], [(0, 0), (0, 4), (0, 8), (0, 12), (0, 16), (0, 17)], nb, t, first_fn=_rwkv_first_chunk,
        side=side)
    dp_rwkv, d_mu = _mix_bwd(p, q["rwkv_mu"], d_pm, nb, t, f"rwkv_mix_bwd{l}")
    g["rwkv_mu"] = d_mu.sum(1).reshape(1152)
    rp = [a.sum(0) for a in d_rprm]
    g["rwkv_w0"] = rp[0].reshape(256)
    g["rwkv_w_up"] = jnp.transpose(rp[1], (1, 0, 2)).reshape(64, 256)
    g["rwkv_a0"] = rp[2].reshape(256)
    g["rwkv_a_up"] = jnp.transpose(rp[3], (1, 0, 2)).reshape(64, 256)
    for i, nme in enumerate(("rwkv_k_k", "rwkv_k_a", "rwkv_r_k", "rwkv_ln_w", "rwkv_ln_b")):
        g[nme] = rp[4 + i].reshape(256)

    dp_sc, d_scw = _sc_bwd(p, q["sc_conv"], dy, nb, t, f"sc_bwd{l}")
    g["sc_conv_w"] = jnp.transpose(d_scw.sum(1), (1, 0, 2)).reshape(SC_TAPS, 256)

    (dp_gla, dad), (d_aup, d_ab, d_gnw), _ = _mixer_bwd(
        _gla_chunk, f"gla_bwd{l}", gla_in, q["gla_prm"], [], ck_gla, dy, 3,
        [(16, BF16), (1, BF16)], [(0, 0), (0, 4), (0, 8), (0, 12), (1, 0)], nb, t)
    g["gla_a_up"] = jnp.transpose(d_aup.sum(0)[:, :16, :GLA_HEAD_K], (1, 0, 2)).reshape(16, 128)
    g["gla_a_bias"] = d_ab.sum(0)[:, 0, :GLA_HEAD_K].reshape(128)
    g["gla_norm_w"] = d_gnw.sum((0, 1, 2))

    singles = jnp.concatenate([dab, dp_rwkv[16:18], dad], axis=0)
    sources = [dconv_in, dz, dp_rwkv, dp_sc, dp_gla, singles]
    d_win = _regroup_out(_dwin(sv["hb"], sources, f"dwin{l}"), f"regroup_out{l}")
    own = _exchange_plan([d_win, d_wout]) if exchange_own else None
    dx, d_pre, got = _dh_prenorm_bwd(sources, q["w_g"], sv["x"], q["pre_w"], dxn, f"dh_bwd{l}", own)
    if exchange_own:
        d_win, d_wout = got
    g["pre_norm_w"] = d_pre.reshape(D_MODEL)
    g["post_norm_w"] = d_post.reshape(D_MODEL)
    return dx, g, d_win, d_wout, side_res


def _local_step(x, tgt, wts, tiny, w_in_all, w_out_all, later_shards=None):
    nb, t, d = x.shape
    xf = x.reshape(nb * t, d)
    overlap = later_shards is not None
    qs, saved = [], []
    big = _big_weights(w_in_all[0], w_out_all[0], 0)
    for l in range(DEPTH):
        q = dict(_layer_params(wts, tiny, l), **big)
        nxt = l + 1 < DEPTH
        side = _gather_plan(later_shards[l]) if overlap and nxt else None
        xf, sv, got = _layer_fwd(xf, q, nb, t, l, side)
        if nxt:
            big = _big_weights(*(got if overlap else (w_in_all[l + 1], w_out_all[l + 1])), l + 1)
        qs.append(q)
        saved.append(sv)
    dxf, lpart = _loss_grad(xf, tgt.reshape(nb * t, d), "loss")
    grads, d_win, d_wout = [None] * DEPTH, [None] * DEPTH, [None] * DEPTH
    for l in reversed(range(DEPTH)):
        side = _exchange_plan([d_win[l + 1], d_wout[l + 1]]) if overlap and l + 1 < DEPTH else None
        dxf, grads[l], d_win[l], d_wout[l], got = _layer_bwd(dxf, qs[l], saved[l], nb, t, l, side,
                                                             exchange_own=overlap and l == 0)
        if side is not None:
            d_win[l + 1], d_wout[l + 1] = got
    small = {k: jnp.stack([grads[l][k] for l in range(DEPTH)]) for k in grads[0]}
    return lpart[0, 0], dxf.reshape(nb, t, d), small, d_win, d_wout


_WEIGHTS = ("pre_norm_w", "w_in", "gdn_conv_w", "gdn_a_log", "gdn_dt_bias", "gdn_norm_w", "rwkv_mu", "rwkv_w0",
            "rwkv_w_up", "rwkv_a0", "rwkv_a_up", "rwkv_k_k", "rwkv_k_a", "rwkv_r_k", "rwkv_ln_w", "rwkv_ln_b",
            "sc_conv_w", "gla_a_up", "gla_a_bias", "gla_norm_w", "w_out", "post_norm_w")


def kernel(x, pre_norm_w, w_in, gdn_conv_w, gdn_a_log, gdn_dt_bias, gdn_norm_w, rwkv_mu, rwkv_w0, rwkv_w_up, rwkv_a0, rwkv_a_up, rwkv_k_k, rwkv_k_a, rwkv_r_k, rwkv_ln_w, rwkv_ln_b, sc_conv_w, gla_a_up, gla_a_bias, gla_norm_w, w_out, post_norm_w, loss_target, m_pre_norm_w, m_w_in, m_gdn_conv_w, m_gdn_a_log, m_gdn_dt_bias, m_gdn_norm_w, m_rwkv_mu, m_rwkv_w0, m_rwkv_w_up, m_rwkv_a0, m_rwkv_a_up, m_rwkv_k_k, m_rwkv_k_a, m_rwkv_r_k, m_rwkv_ln_w, m_rwkv_ln_b, m_sc_conv_w, m_gla_a_up, m_gla_a_bias, m_gla_norm_w, m_w_out, m_post_norm_w, v_pre_norm_w, v_w_in, v_gdn_conv_w, v_gdn_a_log, v_gdn_dt_bias, v_gdn_norm_w, v_rwkv_mu, v_rwkv_w0, v_rwkv_w_up, v_rwkv_a0, v_rwkv_a_up, v_rwkv_k_k, v_rwkv_k_a, v_rwkv_r_k, v_rwkv_ln_w, v_rwkv_ln_b, v_sc_conv_w, v_gla_a_up, v_gla_a_bias, v_gla_norm_w, v_w_out, v_post_norm_w):
    env = dict(locals())
    w = {n: env[n] for n in _WEIGHTS}
    m = {n: env["m_" + n] for n in _WEIGHTS}
    v = {n: env["v_" + n] for n in _WEIGHTS}
    tiny_names = [n for n, _ in _TINY]

    w_in_b, w_out_b = w_in.astype(BF16), w_out.astype(BF16)
    w_in_0, w_out_0, tiny_all = _run_comm(
        _gather_plan([w_in_b[0], w_out_b[0], _pack_rows([w[n] for n in tiny_names], _TINY_ROWS)]), "gather_weights")
    tiny = _unpack_rows(tiny_all, _TINY)

    lpart, grad_x, small, r_win, r_wout = _local_step(
        x, loss_target, w, tiny, [w_in_0], [w_out_0], later_shards=[(w_in_b[l], w_out_b[l]) for l in range(1, DEPTH)])

    tiny_send = jnp.stack([_pack_rows([_split_last(small[n][l])[d] for n in tiny_names for l in range(DEPTH)],
                                      _TINY_ROWS) for d in range(N_DEV)])
    (r_tiny,) = _run_comm(_exchange_plan([tiny_send]), "scatter_grads")
    grads, delta, new_m, new_v = {}, {}, {}, {}
    for n, parts in (("w_in", r_win), ("w_out", r_wout)):
        res = [_sum_adamw(parts[l], w[n][l], m[n][l], v[n][l], f"adamw_{n}{l}") for l in range(DEPTH)]
        grads[n], delta[n], new_m[n], new_v[n] = [jnp.stack(o) for o in zip(*res)]
    tiny_sum = _sum_slots(r_tiny, "sum_tiny").reshape(-1)
    o = 0
    for n, s in _TINY:
        size = int(np.prod(s))
        grads[n] = tiny_sum[o:o + size].reshape(s)
        o += size

    small = dict(small)
    small["loss"] = lpart
    red = _unpack_rows(_all_reduce_small(_pack_rows([small[n] for n, _ in _SMALL], _SMALL_ROWS), "reduce_small"),
                       _SMALL)
    loss = red.pop("loss")
    grads.update(red)

    rest = [n for n in _WEIGHTS if n not in ("w_in", "w_out")]
    rest_shapes = [(n, w[n].shape) for n in rest]
    rows = -(-sum(int(np.prod(s)) for _, s in rest_shapes) // 1024) * 8
    outs = _adamw(_pack_rows([w[n] for n in rest], rows), _pack_rows([grads[n] for n in rest], rows),
                  _pack_rows([m[n] for n in rest], rows), _pack_rows([v[n] for n in rest], rows, 1.0), "adamw_rest")
    for dst, packed in zip((delta, new_m, new_v), outs):
        dst.update(_unpack_rows(packed, rest_shapes))

    return (loss, grad_x, *[grads[n] for n in _WEIGHTS], *[delta[n] for n in _WEIGHTS],
            *[new_m[n] for n in _WEIGHTS], *[new_v[n] for n in _WEIGHTS])
```

```python
import collections
import functools
import math

import numpy as np
import jax
import jax.numpy as jnp
from jax import lax
from jax.experimental import pallas as pl
from jax.experimental.pallas import tpu as pltpu

F32 = jnp.float32
BF16 = jnp.bfloat16

D_MODEL = 1024
DEPTH = 2
NH = 4
DH = 64
CH = 64
EPS = 1e-6
RWKV_GN_EPS = 64e-5
GLA_HEAD_K = 32
GLA_TAU = 16.0
GDN_TAPS = 4
SC_TAPS = 3
D_IN = 3992
N_DEV = 8
SHARD_COLS = D_IN // N_DEV

G_GDN = 0
G_RWKV = 16
G_SC = 32
G_GLA = 48
G_GDN_AB, G_RWKV_WD, G_RWKV_AD, G_GLA_AD = 64, 65, 66, 67
N_GROUPS = 68
GROUPS_PER_STEP = 4
TIME_BLOCK = 256
RWKV_EXACT_STEPS = 16

C_GDN, C_RWKV, C_SC, C_GLA = 0, 1032, 2184, 3208

ADAM_LR, ADAM_B1, ADAM_B2, ADAM_EPS, ADAM_WD, ADAM_STEP = 0.001, 0.9, 0.999, 1e-08, 0.01, 10

VMEM_LIMIT = 56 * 1024 * 1024
MESH = pl.DeviceIdType.MESH

_pcall = pl.pallas_call

_Comm = collections.namedtuple("_Comm", "operands out_shapes copies")


def _cparams(sem=None):
    if sem is None:
        return pltpu.CompilerParams(vmem_limit_bytes=VMEM_LIMIT)
    return pltpu.CompilerParams(dimension_semantics=sem, vmem_limit_bytes=VMEM_LIMIT)


def _group_segments():
    table = [(G_GDN + i, C_GDN + DH * i, DH) for i in range(16)]
    table.append((G_GDN_AB, C_GDN + 1024, 8))
    table += [(G_RWKV + i, C_RWKV + DH * i, DH) for i in range(16)]
    table += [(G_RWKV_WD, C_RWKV + 1024, DH), (G_RWKV_AD, C_RWKV + 1088, DH)]
    table += [(G_SC + 4 * j + k, C_SC + 256 * k + DH * j, DH) for j in range(NH) for k in range(4)]
    for h in range(NH):
        table += [(G_GLA + h, C_GLA + GLA_HEAD_K * h, GLA_HEAD_K),
                  (G_GLA + 4 + h, C_GLA + 128 + GLA_HEAD_K * h, GLA_HEAD_K),
                  (G_GLA + 8 + h, C_GLA + 256 + DH * h, DH),
                  (G_GLA + 12 + h, C_GLA + 512 + DH * h, DH)]
    table.append((G_GLA_AD, C_GLA + 768, 16))
    segs, padded = [], []
    for g, c, n in table:
        if n < DH:
            padded.append(g)
        a = 0
        while n > 0:
            d, off = divmod(c, SHARD_COLS)
            ln = min(n, SHARD_COLS - off)
            segs.append((g, a, d, off, ln))
            c, a, n = c + ln, a + ln, n - ln
    return segs, padded


_SEGMENTS, _PADDED_GROUPS = _group_segments()


def _dn(ta, tb):
    return (((1 if ta else 2,), (2 if tb else 1,)), ((0,), (0,)))


def _hdot(a, b, ta=False, tb=False):
    return lax.dot_general(a, b, _dn(ta, tb), precision=lax.Precision.HIGH, preferred_element_type=F32)


def _r(x):
    return x.astype(BF16)


def _rdot(a, b, ta=False, tb=False):
    return lax.dot_general(_r(a), _r(b), _dn(ta, tb), preferred_element_type=F32)


@jax.custom_vjp
def _bmm(a, b):
    return _rdot(a, b)


def _bmm_fwd(a, b):
    return _rdot(a, b), (a, b)


def _bmm_bwd(res, g):
    a, b = res
    return _rdot(g, b, tb=True), _rdot(a, g, ta=True)


_bmm.defvjp(_bmm_fwd, _bmm_bwd)


@jax.custom_vjp
def _bmm_nt(a, b):
    return _rdot(a, b, tb=True)


def _bmm_nt_fwd(a, b):
    return _rdot(a, b, tb=True), (a, b)


def _bmm_nt_bwd(res, g):
    a, b = res
    return _rdot(g, b), _rdot(g, a, ta=True)


_bmm_nt.defvjp(_bmm_nt_fwd, _bmm_nt_bwd)


@jax.custom_vjp
def _bmm_tn(a, b):
    return _rdot(a, b, ta=True)


def _bmm_tn_fwd(a, b):
    return _rdot(a, b, ta=True), (a, b)


def _bmm_tn_bwd(res, g):
    a, b = res
    return _rdot(b, g, tb=True), _rdot(a, g)


_bmm_tn.defvjp(_bmm_tn_fwd, _bmm_tn_bwd)


def _tri(n):
    i = lax.broadcasted_iota(jnp.int32, (n, n), 0)
    j = lax.broadcasted_iota(jnp.int32, (n, n), 1)
    return i >= j, i > j, i == j


def _heads_of(x, like):
    n = like.shape[0]
    if x.ndim == 2:
        return jnp.broadcast_to(x[None], (n,) + x.shape)
    seqs = x.shape[0]
    return jnp.broadcast_to(x[:, None], (seqs, n // seqs) + x.shape[1:]).reshape((n,) + x.shape[1:])


def _cumsum_rows(x):
    incl, _, _ = _tri(x.shape[-2])
    return _hdot(_heads_of(incl.astype(F32), x), x)


@jax.custom_vjp
def _inv_unit_lower(a):
    n = a.shape[-1]
    _, _, eye = _tri(n)
    pw = -a
    inv = eye.astype(F32) + pw
    for _ in range(math.ceil(math.log2(n)) - 1):
        pw = _hdot(pw, pw)
        inv = inv + _hdot(inv, pw)
    return inv


def _inv_unit_lower_fwd(a):
    inv = _inv_unit_lower(a)
    return inv, inv


def _inv_unit_lower_bwd(inv, g):
    return (-_hdot(_hdot(inv, g, ta=True), inv, tb=True),)


_inv_unit_lower.defvjp(_inv_unit_lower_fwd, _inv_unit_lower_bwd)


@jax.custom_vjp
def _inv_reuse(a, inv):
    return inv


def _inv_reuse_fwd(a, inv):
    return inv, inv


def _inv_reuse_bwd(inv, g):
    return _inv_unit_lower_bwd(inv, g)[0], jnp.zeros_like(inv)


_inv_reuse.defvjp(_inv_reuse_fwd, _inv_reuse_bwd)


def _silu(x):
    return x * jax.nn.sigmoid(x)


def _t(x):
    return jnp.swapaxes(x, -1, -2)


def _gdn_chunk(prm, cst, ins, s, tinv=None):
    a_log, dt_b, nw = prm
    m_a, m_b = cst
    cq, ck, cv, z, ab = ins
    ab = _heads_of(ab, m_a)
    incl, strict, _ = _tri(CH)
    q = _silu(cq)
    k = _silu(ck)
    v = _silu(cv)
    q = q * lax.rsqrt(jnp.sum(q * q, -1, keepdims=True) + EPS) * (DH ** -0.5)
    k = k * lax.rsqrt(jnp.sum(k * k, -1, keepdims=True) + EPS)
    a_raw = jnp.sum(ab * m_a, -1, keepdims=True)
    b_raw = jnp.sum(ab * m_b, -1, keepdims=True)
    gstep = -jnp.exp(a_log) * jax.nn.softplus(a_raw + dt_b)
    beta = jax.nn.sigmoid(b_raw)
    gc = _cumsum_rows(gstep)
    gl = jnp.sum(gstep, -2, keepdims=True)
    dec = jnp.where(incl, jnp.exp(jnp.where(incl, gc - _t(gc), 0.0)), 0.0)
    kb = k * beta
    a_mat = jnp.where(strict, _bmm_nt(kb, k) * dec, 0.0)
    tinv = _inv_unit_lower(a_mat) if tinv is None else _inv_reuse(a_mat, tinv)
    eg = jnp.exp(gc)
    u = _hdot(tinv, v * beta)
    w = _hdot(tinv, kb * eg)
    attn = _bmm_nt(q, k) * dec
    v_new = u - _bmm(w, s)
    o = _bmm(q * eg, s) + _bmm(attn, v_new)
    s_next = s * jnp.exp(gl) + _bmm_tn(k * jnp.exp(gl - gc), v_new)
    on = o * lax.rsqrt(jnp.mean(o * o, -1, keepdims=True) + EPS) * nw
    return on * _silu(z), s_next, tinv


def _gla_chunk(prm, cst, ins, st):
    a_up, a_bias, nw = prm
    q, k, v, z, ad = ins
    incl, _, _ = _tri(CH)
    la = jax.nn.log_sigmoid(_bmm(_heads_of(ad, a_up), a_up) + a_bias) * (1.0 / GLA_TAU)
    bc = _cumsum_rows(la)
    bl = jnp.sum(la, -2, keepdims=True)
    qe = q * (GLA_HEAD_K ** -0.5) * jnp.exp(bc)
    ke = k * jnp.exp(-bc)
    attn = jnp.where(incl, _bmm_nt(qe, ke), 0.0)
    o = _bmm_nt(qe, st) + _bmm(attn, v)
    st_next = st * jnp.exp(bl) + _bmm_tn(v, k * jnp.exp(bl - bc))
    on = o * lax.rsqrt(jnp.mean(o * o, -1, keepdims=True) + EPS) * nw
    return on * _silu(z), st_next


def _rwkv_chunk(prm, cst, ins, s, inv=None):
    r, v = ins[0], ins[2]
    incl, strict, _ = _tri(r.shape[-2])
    lw, kk, k2, m = _rwkv_pre(prm, ins)
    cum = _cumsum_rows(lw)
    ltot = jnp.sum(lw, -2, keepdims=True)
    n_t = -kk * jnp.exp(cum - lw)
    einv = jnp.exp(-cum)
    m_t = m * einv
    k_t = k2 * einv
    r_t = r * jnp.exp(cum)
    a_nm = jnp.where(strict, _hdot(n_t, m_t, tb=True), 0.0)
    a_nk = jnp.where(strict, _hdot(n_t, k_t, tb=True), 0.0)
    inv = _inv_unit_lower(-a_nm) if inv is None else _inv_reuse(-a_nm, inv)
    cm = _hdot(inv, _hdot(n_t, s, tb=True) + _bmm(a_nk, v))
    y = (_bmm_nt(r_t, s) + _bmm(jnp.where(incl, _hdot(r_t, m_t, tb=True), 0.0), cm)
         + _bmm(jnp.where(incl, _hdot(r_t, k_t, tb=True), 0.0), v))
    eend = jnp.exp(ltot - cum)
    s_next = s * jnp.exp(ltot) + _bmm_tn(cm, m * eend) + _bmm_tn(v, k2 * eend)
    return _rwkv_post(prm, ins, y, k2), s_next, inv


def _rwkv_pre(prm, ins):
    w0, w_up, a0, a_up, k_k, k_a = prm[:6]
    k, wd, ad = ins[1], ins[4], ins[5]
    lw = -math.exp(-0.5) * jax.nn.sigmoid(w0 + _bmm(_heads_of(jnp.tanh(wd), w_up), w_up))
    a = jax.nn.sigmoid(a0 + _bmm(_heads_of(ad, a_up), a_up))
    kk = k * k_k
    kk = kk * lax.rsqrt(jnp.sum(kk * kk, -1, keepdims=True) + EPS)
    k2 = k * (1.0 + (a - 1.0) * k_a)
    return lw, kk, k2, kk * a


def _rwkv_post(prm, ins, y, k2):
    r_k, ln_w, ln_b = prm[6:]
    r, v, z = ins[0], ins[2], ins[3]
    mean = jnp.mean(y, -1, keepdims=True)
    yc = y - mean
    var = jnp.mean(yc * yc, -1, keepdims=True)
    yn = yc * lax.rsqrt(var + RWKV_GN_EPS) * ln_w + ln_b
    bonus = jnp.sum(r * k2 * r_k, -1, keepdims=True) * v
    return (yn + bonus) * _silu(z)


@jax.custom_vjp
def _bmv(s, x):
    return jnp.sum(_r(s).astype(F32) * _r(x).astype(F32), -1, keepdims=True)


def _bmv_fwd(s, x):
    return _bmv(s, x), (s, x)


def _bmv_bwd(res, g):
    s, x = res
    return g * x, jnp.sum(_r(s).astype(F32) * _r(g).astype(F32), -2, keepdims=True)


_bmv.defvjp(_bmv_fwd, _bmv_bwd)


def _rwkv_steps(prm, cst, ins, s, steps):
    r, v = ins[0], ins[2]
    lw, kk, k2, m = _rwkv_pre(prm, ins)
    w = jnp.exp(lw)
    v_t = _t(v)
    lane = lax.broadcasted_iota(jnp.int32, (1, 1, CH), 2)
    y_t = jnp.zeros((s.shape[0], DH, CH), F32)
    for t in range(steps):
        e_t = (lane == t).astype(F32)
        row = (slice(None), slice(t, t + 1))
        sa = _bmv(s, -kk[row])
        s = s * w[row] + sa * m[row] + jnp.sum(v_t * e_t, -1, keepdims=True) * k2[row]
        y_t = y_t + _bmv(s, r[row]) * e_t
    return _rwkv_post(prm, ins, _t(y_t), k2)[:, :steps], s


def _rwkv_first_chunk(prm, cst, ins, s):
    k = RWKV_EXACT_STEPS
    y_head, s = _rwkv_steps(prm, cst, ins, s, k)
    y_tail, s, _ = _rwkv_chunk(prm, cst, [x[..., k:, :] for x in ins], s)
    return jnp.concatenate([y_head, y_tail], axis=-2), s


def _time_block(t):
    return TIME_BLOCK if t % TIME_BLOCK == 0 else t


def _load_chunk(ref, i):
    nb = ref.shape[1]
    if ref.shape[0] == NH:
        return jnp.concatenate([ref[:, b, pl.ds(i, CH), :] for b in range(nb)], axis=0)
    return ref[0, :, pl.ds(i, CH), :]


def _load_chunk_of(ref, c, i):
    if ref.shape[0] == NH:
        return ref[pl.ds(c % NH, 1), c // NH, pl.ds(i, CH), :]
    return ref[0, pl.ds(c // NH, 1), pl.ds(i, CH), :]


def _each_chain(n, fn):
    def step(c, carry):
        fn(c)
        return carry

    lax.fori_loop(0, n, step, 0)


def _mixer_fwd(chunk_fn, name, ins, prm, cst, nb, t, first_fn=None, side=None, n_kept=0):
    tb = _time_block(t)
    nt, ncb, nch = t // tb, tb // CH, nb * NH
    n_in, n_prm, n_cst = len(ins), len(prm), len(cst)
    n_main, n_side = n_in + n_prm + n_cst, len(side.operands) if side else 0

    def body(*refs):
        in_refs = refs[:n_in]
        prm_refs = refs[n_in:n_in + n_prm]
        cst_refs = refs[n_in + n_prm:n_main]
        side_in = refs[n_main:n_main + n_side]
        y_ref, ck_ref = refs[n_main + n_side:n_main + n_side + 2]
        side_out = refs[n_main + n_side + 2:n_main + 2 * n_side + 2]
        s_scr = refs[n_main + 2 * n_side + 2]
        sems = refs[n_main + 2 * n_side + 3:]
        step_t = pl.program_id(0)

        if side is not None:
            @pl.when(step_t == 0)
            def _():
                _comm_start(side.copies(side_in, side_out, sems))

        @pl.when(step_t == 0)
        def _():
            s_scr[...] = jnp.zeros_like(s_scr)

        def chunk(c, i):
            s = s_scr[...]
            y, s_next, *kept = chunk_fn([jnp.tile(r[...], (nb, 1, 1)) for r in prm_refs],
                                        [jnp.tile(r[...], (nb, 1, 1)) for r in cst_refs],
                                        [_load_chunk(r, i) for r in in_refs], s)
            for e, a in enumerate([s] + kept):
                ck_ref[c, e] = a
            for b in range(nb):
                y_ref[:, b, pl.ds(i, CH), :] = y[b * NH:(b + 1) * NH].astype(BF16)
            s_scr[...] = s_next

        def first_chunk_of(c):
            one, h = pl.ds(c, 1), pl.ds(c % NH, 1)
            s = s_scr[one]
            ck_ref[0, 0, one] = s
            for e in range(n_kept):
                ck_ref[0, 1 + e, one] = jnp.zeros((1, DH, DH), F32)
            y, s_next = first_fn([r[h] for r in prm_refs], [r[h] for r in cst_refs],
                                 [_load_chunk_of(r, c, 0) for r in in_refs], s)
            y_ref[h, c // NH, pl.ds(0, CH), :] = y.astype(BF16)
            s_scr[one] = s_next

        def step(c, carry):
            chunk(c, pl.multiple_of(c * CH, CH))
            return carry

        if first_fn is None:
            lax.fori_loop(0, ncb, step, 0)
        else:
            @pl.when(step_t == 0)
            def _():
                _each_chain(nch, first_chunk_of)

            @pl.when(step_t != 0)
            def _():
                chunk(0, 0)

            lax.fori_loop(1, ncb, step, 0)

        if side is not None:
            @pl.when(step_t == nt - 1)
            def _():
                _comm_wait(side.copies(side_in, side_out, sems))

    hbm = pl.BlockSpec(memory_space=pl.ANY)
    in_specs = [pl.BlockSpec((ng, nb, tb, DH), (lambda j, bi=bi: (bi, 0, j, 0))) for _, ng, bi in ins]
    in_specs += [pl.BlockSpec(p.shape, lambda j: (0, 0, 0)) for p in list(prm) + list(cst)]
    y, ck, *side_res = _pcall(
        body, name=name, grid=(nt,),
        in_specs=in_specs + [hbm] * n_side,
        out_specs=[pl.BlockSpec((NH, nb, tb, DH), lambda j: (0, 0, j, 0)),
                   pl.BlockSpec((ncb, 1 + n_kept, nch, DH, DH), lambda j: (j, 0, 0, 0, 0))] + [hbm] * n_side,
        out_shape=[jax.ShapeDtypeStruct((NH, nb, t, DH), BF16),
                   jax.ShapeDtypeStruct((t // CH, 1 + n_kept, nch, DH, DH), F32)]
        + (side.out_shapes if side else []),
        scratch_shapes=[pltpu.VMEM((nch, DH, DH), F32)] + (_comm_scratch(side) if side else []),
        compiler_params=_cparams(("arbitrary",)),
    )(*[a.reshape(a.shape[0], nb, t, DH) for a, _, _ in ins], *prm, *cst, *(side.operands if side else []))
    return y.reshape(NH, nb * t, DH), ck, side_res


def _mixer_bwd(chunk_fn, name, ins, prm, cst, ck, dy, dy_block, outs, routes, nb, t, first_fn=None, side=None):
    tb = _time_block(t)
    nt, ncb, nch = t // tb, tb // CH, nb * NH
    n_in, n_prm, n_cst, n_out = len(ins), len(prm), len(cst), len(outs)
    n_main, n_side = n_in + n_prm + n_cst + 2, len(side.operands) if side else 0
    n_kept = ck.shape[1] - 1

    def body(*refs):
        in_refs = refs[:n_in]
        prm_refs = refs[n_in:n_in + n_prm]
        cst_refs = refs[n_in + n_prm:n_in + n_prm + n_cst]
        ck_ref, dy_ref = refs[n_main - 2:n_main]
        side_in = refs[n_main:n_main + n_side]
        rest = refs[n_main + n_side:]
        out_refs = rest[:n_out]
        dprm_refs = rest[n_out:n_out + n_prm]
        side_out = rest[n_out + n_prm:n_out + n_prm + n_side]
        ds_scr = rest[n_out + n_prm + n_side]
        sems = rest[n_out + n_prm + n_side + 1:]
        step_t = pl.program_id(0)

        if side is not None:
            @pl.when(step_t == 0)
            def _():
                _comm_start(side.copies(side_in, side_out, sems))

        @pl.when(step_t == 0)
        def _():
            ds_scr[...] = jnp.zeros_like(ds_scr)
            for r in dprm_refs:
                r[...] = jnp.zeros_like(r)

        def chunk(c, i):
            cst_v = [jnp.tile(r[...], (nb, 1, 1)) for r in cst_refs]
            kept = [ck_ref[c, 1 + e] for e in range(n_kept)]
            _, vjp = jax.vjp(lambda p, x, s: chunk_fn(p, cst_v, x, s, *kept)[:2],
                             [jnp.tile(r[...], (nb, 1, 1)) for r in prm_refs],
                             [_load_chunk(r, i) for r in in_refs], ck_ref[c, 0])
            dy_c = jnp.concatenate([dy_ref[:, b, pl.ds(i, CH), :] for b in range(nb)], axis=0)
            d_prm, d_ins, d_s = vjp((dy_c, ds_scr[...]))
            for (oi, g0), r, g in zip(routes, in_refs, d_ins):
                o_ref = out_refs[oi]
                if r.shape[0] == NH:
                    for b in range(nb):
                        o_ref[g0:g0 + NH, b, pl.ds(i, CH), :] = g[b * NH:(b + 1) * NH].astype(o_ref.dtype)
                else:
                    o_ref[g0, :, pl.ds(i, CH), :] = g.astype(o_ref.dtype)
            for r, g in zip(dprm_refs, d_prm):
                r[...] += g
            ds_scr[...] = d_s

        def first_chunk_of(c):
            one, h, b = pl.ds(c, 1), pl.ds(c % NH, 1), c // NH
            cst_v = [r[h] for r in cst_refs]
            _, vjp = jax.vjp(lambda p, x, s: first_fn(p, cst_v, x, s), [r[h] for r in prm_refs],
                             [_load_chunk_of(r, c, 0) for r in in_refs], ck_ref[0, 0, one])
            d_prm, d_ins, d_s = vjp((dy_ref[h, b, pl.ds(0, CH), :], ds_scr[one]))
            for (oi, g0), r, g in zip(routes, in_refs, d_ins):
                o_ref = out_refs[oi]
                if r.shape[0] == NH:
                    o_ref.at[g0:g0 + NH][h, b, pl.ds(0, CH), :] = g.astype(o_ref.dtype)
                else:
                    o_ref[g0, pl.ds(b, 1), pl.ds(0, CH), :] += g.astype(o_ref.dtype)
            for r, g in zip(dprm_refs, d_prm):
                r[one] += g
            ds_scr[one] = d_s

        def first_chunk():
            for (oi, g0), r in zip(routes, in_refs):
                if r.shape[0] != NH:
                    out_refs[oi][g0, :, pl.ds(0, CH), :] = jnp.zeros((nb, CH, DH), out_refs[oi].dtype)
            _each_chain(nch, first_chunk_of)

        def step(j, carry):
            c = ncb - 1 - j
            chunk(c, pl.multiple_of(c * CH, CH))
            return carry

        lax.fori_loop(0, ncb - 1, step, 0)
        if first_fn is None:
            chunk(0, 0)
        else:
            @pl.when(step_t == nt - 1)
            def _():
                first_chunk()

            @pl.when(step_t != nt - 1)
            def _():
                chunk(0, 0)

        if side is not None:
            @pl.when(step_t == nt - 1)
            def _():
                _comm_wait(side.copies(side_in, side_out, sems))

    def back(j):
        return nt - 1 - j

    hbm = pl.BlockSpec(memory_space=pl.ANY)
    in_specs = [pl.BlockSpec((ng, nb, tb, DH), (lambda j, bi=bi: (bi, 0, back(j), 0))) for _, ng, bi in ins]
    in_specs += [pl.BlockSpec(p.shape, lambda j: (0, 0, 0)) for p in list(prm) + list(cst)]
    in_specs += [pl.BlockSpec((ncb, 1 + n_kept, nch, DH, DH), lambda j: (back(j), 0, 0, 0, 0)),
                 pl.BlockSpec((NH, nb, tb, DH), lambda j: (dy_block, 0, back(j), 0))]
    out_specs = [pl.BlockSpec((ng, nb, tb, DH), lambda j: (0, 0, back(j), 0)) for ng, _ in outs]
    out_specs += [pl.BlockSpec((nch,) + p.shape[1:], lambda j: (0, 0, 0)) for p in prm]
    out_shape = [jax.ShapeDtypeStruct((ng, nb, t, DH), dt) for ng, dt in outs]
    out_shape += [jax.ShapeDtypeStruct((nch,) + p.shape[1:], F32) for p in prm]
    res = _pcall(
        body, name=name, grid=(nt,),
        in_specs=in_specs + [hbm] * n_side, out_specs=out_specs + [hbm] * n_side,
        out_shape=out_shape + (side.out_shapes if side else []),
        scratch_shapes=[pltpu.VMEM((nch, DH, DH), F32)] + (_comm_scratch(side) if side else []),
        compiler_params=_cparams(("arbitrary",)),
    )(*[a.reshape(a.shape[0], nb, t, DH) for a, _, _ in ins], *prm, *cst, ck, dy.reshape(dy.shape[0], nb, t, DH),
      *(side.operands if side else []))
    d_outs = [o.reshape(o.shape[0], nb * t, DH) for o in res[:n_out]]
    d_prm = [g.reshape((nb,) + p.shape) for g, p in zip(res[n_out:n_out + n_prm], prm)]
    return d_outs, d_prm, res[n_out + n_prm:]


def _shift_down(x, s):
    if s == 0:
        return x
    row = lax.broadcasted_iota(jnp.int32, x.shape, 0)
    return jnp.where(row < s, 0.0, pltpu.roll(x, s, 0))


def _shift_up(x, s):
    if s == 0:
        return x
    t = x.shape[0]
    row = lax.broadcasted_iota(jnp.int32, x.shape, 0)
    return jnp.where(row >= t - s, 0.0, pltpu.roll(x, t - s, 0))


def _conv_fwd(p, g0, ng, w, nb, t, name):
    taps = w.shape[1]

    def body(x_ref, w_ref, y_ref):
        x = x_ref[...]
        acc = w_ref[taps - 1:taps, :] * x
        for i in range(taps - 1):
            acc = acc + w_ref[i:i + 1, :] * _shift_down(x, taps - 1 - i)
        y_ref[...] = acc

    return _pcall(
        body, name=name, grid=(ng, nb),
        in_specs=[pl.BlockSpec((None, t, DH), lambda g, b: (g0 + g, b, 0)),
                  pl.BlockSpec((None, taps, DH), lambda g, b: (g, 0, 0))],
        out_specs=pl.BlockSpec((None, t, DH), lambda g, b: (g, b, 0)),
        out_shape=jax.ShapeDtypeStruct((ng, nb * t, DH), F32),
        compiler_params=_cparams(("parallel", "parallel")),
    )(p, w)


def _conv_bwd(p, g0, ng, w, dy, nb, t, name):
    taps = w.shape[1]

    def body(x_ref, w_ref, dy_ref, dx_ref, dw_ref):
        x = x_ref[...]
        d = dy_ref[...]
        acc = w_ref[taps - 1:taps, :] * d
        dw_ref[taps - 1:taps, :] = jnp.sum(d * x, 0, keepdims=True)
        for i in range(taps - 1):
            s = taps - 1 - i
            acc = acc + w_ref[i:i + 1, :] * _shift_up(d, s)
            dw_ref[i:i + 1, :] = jnp.sum(d * _shift_down(x, s), 0, keepdims=True)
        dx_ref[...] = acc.astype(BF16)

    return _pcall(
        body, name=name, grid=(ng, nb),
        in_specs=[pl.BlockSpec((None, t, DH), lambda g, b: (g0 + g, b, 0)),
                  pl.BlockSpec((None, taps, DH), lambda g, b: (g, 0, 0)),
                  pl.BlockSpec((None, t, DH), lambda g, b: (g, b, 0))],
        out_specs=[pl.BlockSpec((None, t, DH), lambda g, b: (g, b, 0)),
                   pl.BlockSpec((None, None, taps, DH), lambda g, b: (g, b, 0, 0))],
        out_shape=[jax.ShapeDtypeStruct((ng, nb * t, DH), BF16),
                   jax.ShapeDtypeStruct((ng, nb, taps, DH), F32)],
        compiler_params=_cparams(("parallel", "parallel")),
    )(p, w, dy)


def _mix_group(g):
    return jnp.where(g < 16, G_RWKV + g, G_RWKV_WD + g - 16)


def _mix_fwd(p, mu, nb, t, name):
    def body(x_ref, mu_ref, y_ref):
        x = x_ref[...]
        y_ref[...] = x + mu_ref[...] * (_shift_down(x, 1) - x)

    return _pcall(
        body, name=name, grid=(18, nb),
        in_specs=[pl.BlockSpec((None, t, DH), lambda g, b: (_mix_group(g), b, 0)),
                  pl.BlockSpec((None, 1, DH), lambda g, b: (g, 0, 0))],
        out_specs=pl.BlockSpec((None, t, DH), lambda g, b: (g, b, 0)),
        out_shape=jax.ShapeDtypeStruct((18, nb * t, DH), F32),
        compiler_params=_cparams(("parallel", "parallel")),
    )(p, mu)


def _mix_bwd(p, mu, dy, nb, t, name):
    def body(x_ref, mu_ref, dy_ref, dx_ref, dmu_ref):
        x = x_ref[...]
        muv = mu_ref[...]
        d = dy_ref[...]
        dx_ref[...] = (d * (1.0 - muv) + _shift_up(d * muv, 1)).astype(BF16)
        dmu_ref[...] = jnp.sum(d * (_shift_down(x, 1) - x), 0, keepdims=True)

    return _pcall(
        body, name=name, grid=(18, nb),
        in_specs=[pl.BlockSpec((None, t, DH), lambda g, b: (_mix_group(g), b, 0)),
                  pl.BlockSpec((None, 1, DH), lambda g, b: (g, 0, 0)),
                  pl.BlockSpec((None, t, DH), lambda g, b: (g, b, 0))],
        out_specs=[pl.BlockSpec((None, t, DH), lambda g, b: (g, b, 0)),
                   pl.BlockSpec((None, None, 1, DH), lambda g, b: (g, b, 0, 0))],
        out_shape=[jax.ShapeDtypeStruct((18, nb * t, DH), BF16),
                   jax.ShapeDtypeStruct((18, nb, 1, DH), F32)],
        compiler_params=_cparams(("parallel", "parallel")),
    )(p, mu, dy)


def _sc_fwd(p, w, nb, t, name):
    def body(p_ref, w_ref, y_ref):
        u = p_ref[1] * p_ref[2]
        conv = w_ref[2:3, :] * u + w_ref[1:2, :] * _shift_down(u, 1) + w_ref[0:1, :] * _shift_down(u, 2)
        y_ref[...] = (p_ref[0] * conv * _silu(p_ref[3])).astype(BF16)

    return _pcall(
        body, name=name, grid=(NH, nb),
        in_specs=[pl.BlockSpec((4, t, DH), lambda j, b: (G_SC // 4 + j, b, 0)),
                  pl.BlockSpec((None, SC_TAPS, DH), lambda j, b: (j, 0, 0))],
        out_specs=pl.BlockSpec((None, t, DH), lambda j, b: (j, b, 0)),
        out_shape=jax.ShapeDtypeStruct((NH, nb * t, DH), BF16),
        compiler_params=_cparams(("parallel", "parallel")),
    )(p, w)


def _sc_bwd(p, w, dy, nb, t, name):
    def body(p_ref, w_ref, dy_ref, dp_ref, dw_ref):
        bg, cg, xg, z = p_ref[0], p_ref[1], p_ref[2], p_ref[3]
        d = dy_ref[...]
        u = cg * xg
        u1 = _shift_down(u, 1)
        u2 = _shift_down(u, 2)
        conv = w_ref[2:3, :] * u + w_ref[1:2, :] * u1 + w_ref[0:1, :] * u2
        sg = jax.nn.sigmoid(z)
        sz = z * sg
        dp_ref[0] = (d * conv * sz).astype(BF16)
        dp_ref[3] = (d * bg * conv * (sg * (1.0 + z * (1.0 - sg)))).astype(BF16)
        dconv = d * bg * sz
        du = w_ref[2:3, :] * dconv + w_ref[1:2, :] * _shift_up(dconv, 1) + w_ref[0:1, :] * _shift_up(dconv, 2)
        dp_ref[1] = (du * xg).astype(BF16)
        dp_ref[2] = (du * cg).astype(BF16)
        dw_ref[2:3, :] = jnp.sum(dconv * u, 0, keepdims=True)
        dw_ref[1:2, :] = jnp.sum(dconv * u1, 0, keepdims=True)
        dw_ref[0:1, :] = jnp.sum(dconv * u2, 0, keepdims=True)

    return _pcall(
        body, name=name, grid=(NH, nb),
        in_specs=[pl.BlockSpec((4, t, DH), lambda j, b: (G_SC // 4 + j, b, 0)),
                  pl.BlockSpec((None, SC_TAPS, DH), lambda j, b: (j, 0, 0)),
                  pl.BlockSpec((None, t, DH), lambda j, b: (8 + j, b, 0))],
        out_specs=[pl.BlockSpec((4, t, DH), lambda j, b: (j, b, 0)),
                   pl.BlockSpec((None, None, SC_TAPS, DH), lambda j, b: (j, b, 0, 0))],
        out_shape=[jax.ShapeDtypeStruct((4 * NH, nb * t, DH), BF16),
                   jax.ShapeDtypeStruct((NH, nb, SC_TAPS, DH), F32)],
        compiler_params=_cparams(("parallel", "parallel")),
    )(p, w, dy)


def _row_tile(n):
    return 1024 if n % 1024 == 0 else n


def _regroup_in(w_all, name):
    tr = 256
    gs = GROUPS_PER_STEP

    def body(w_ref, o_ref):
        for g in _PADDED_GROUPS:
            o_ref[g // gs, :, DH * (g % gs):DH * (g % gs + 1)] = jnp.zeros((tr, DH), BF16)
        for g, a, d, off, ln in _SEGMENTS:
            lane = DH * (g % gs) + a
            o_ref[g // gs, :, lane:lane + ln] = w_ref[d, :, off:off + ln].astype(BF16)

    return _pcall(
        body, name=name, grid=(D_MODEL // tr,),
        in_specs=[pl.BlockSpec((N_DEV, tr, SHARD_COLS), lambda i: (0, i, 0))],
        out_specs=pl.BlockSpec((N_GROUPS // gs, tr, gs * DH), lambda i: (0, i, 0)),
        out_shape=jax.ShapeDtypeStruct((N_GROUPS // gs, D_MODEL, gs * DH), BF16),
        compiler_params=_cparams(("parallel",)),
    )(w_all)


def _regroup_out(dwg, name):
    tr = 256
    gs = GROUPS_PER_STEP

    def body(g_ref, o_ref):
        for g, a, d, off, ln in _SEGMENTS:
            lane = DH * (g % gs) + a
            o_ref[d, :, off:off + ln] = g_ref[g // gs, :, lane:lane + ln]

    return _pcall(
        body, name=name, grid=(D_MODEL // tr,),
        in_specs=[pl.BlockSpec((N_GROUPS // gs, tr, gs * DH), lambda i: (0, i, 0))],
        out_specs=pl.BlockSpec((N_DEV, tr, SHARD_COLS), lambda i: (0, i, 0)),
        out_shape=jax.ShapeDtypeStruct((N_DEV, D_MODEL, SHARD_COLS), F32),
        compiler_params=_cparams(("parallel",)),
    )(dwg)


def _norm_proj(x, pre_w, w_g, name):
    n = x.shape[0]
    tm = _row_tile(n)
    gs = GROUPS_PER_STEP

    def body(x_ref, pw_ref, w_ref, h_ref, p_ref):
        @pl.when(pl.program_id(1) == 0)
        def _():
            xv = x_ref[...]
            h = xv * lax.rsqrt(jnp.mean(xv * xv, -1, keepdims=True) + EPS) * pw_ref[...]
            h_ref[...] = h.astype(BF16)

        r = jnp.dot(h_ref[...], w_ref[...], preferred_element_type=F32)
        for k in range(gs):
            p_ref[k] = r[:, DH * k:DH * (k + 1)]

    return _pcall(
        body, name=name, grid=(n // tm, N_GROUPS // gs),
        in_specs=[pl.BlockSpec((tm, D_MODEL), lambda i, j: (i, 0)),
                  pl.BlockSpec((1, D_MODEL), lambda i, j: (0, 0)),
                  pl.BlockSpec((None, D_MODEL, gs * DH), lambda i, j: (j, 0, 0))],
        out_specs=[pl.BlockSpec((tm, D_MODEL), lambda i, j: (i, 0)),
                   pl.BlockSpec((gs, tm, DH), lambda i, j: (j, i, 0))],
        out_shape=[jax.ShapeDtypeStruct((n, D_MODEL), BF16),
                   jax.ShapeDtypeStruct((N_GROUPS, n, DH), F32)],
        compiler_params=_cparams(("parallel", "arbitrary")),
    )(x, pre_w, w_g)


def _out_proj_norm(ys, wout_g, x, post_w, name):
    n = x.shape[0]
    tm = _row_tile(n)

    def body(y0, y1, y2, y3, w_ref, x_ref, pw_ref, out_ref, xn_ref):
        acc = jnp.zeros((tm, D_MODEL), F32)
        for m, yr in enumerate((y0, y1, y2, y3)):
            for h in range(NH):
                acc = acc + jnp.dot(yr[h], w_ref[m * NH + h], preferred_element_type=F32)
        out_ref[...] = acc
        xn_ref[...] = x_ref[...] + acc * lax.rsqrt(jnp.mean(acc * acc, -1, keepdims=True) + EPS) * pw_ref[...]

    yspec = pl.BlockSpec((NH, tm, DH), lambda i: (0, i, 0))
    rows = pl.BlockSpec((tm, D_MODEL), lambda i: (i, 0))
    return _pcall(
        body, name=name, grid=(n // tm,),
        in_specs=[yspec] * 4 + [pl.BlockSpec((4 * NH, DH, D_MODEL), lambda i: (0, 0, 0)), rows,
                                pl.BlockSpec((1, D_MODEL), lambda i: (0, 0))],
        out_specs=[rows, rows],
        out_shape=[jax.ShapeDtypeStruct((n, D_MODEL), F32)] * 2,
        compiler_params=_cparams(("parallel",)),
    )(*ys, wout_g, x, post_w)


def _loss_grad(x, tgt, name):
    n = x.shape[0]
    tm = _row_tile(n)

    def body(x_ref, t_ref, dx_ref, l_ref):
        @pl.when(pl.program_id(0) == 0)
        def _():
            l_ref[...] = jnp.zeros_like(l_ref)

        e = x_ref[...] - t_ref[...]
        dx_ref[...] = e * (1.0 / D_MODEL)
        l_ref[...] += jnp.sum(jnp.sum(e * e, -1, keepdims=True), 0, keepdims=True) * (0.5 / D_MODEL)

    rows = pl.BlockSpec((tm, D_MODEL), lambda i: (i, 0))
    return _pcall(
        body, name=name, grid=(n // tm,),
        in_specs=[rows, rows],
        out_specs=[rows, pl.BlockSpec((1, 128), lambda i: (0, 0))],
        out_shape=[jax.ShapeDtypeStruct((n, D_MODEL), F32), jax.ShapeDtypeStruct((1, 128), F32)],
        compiler_params=_cparams(("arbitrary",)),
    )(x, tgt)


def _rmsnorm_bwd(xv, w, d):
    r = lax.rsqrt(jnp.mean(xv * xv, -1, keepdims=True) + EPS)
    xh = xv * r
    dxh = d * w
    dx = r * (dxh - xh * jnp.mean(dxh * xh, -1, keepdims=True))
    return dx, d * xh


def _post_bwd(dxn, out, post_w, wout_g, name):
    n = dxn.shape[0]
    tm = _row_tile(n)

    def body(d_ref, o_ref, pw_ref, w_ref, do_ref, dy_ref, dpw_ref):
        @pl.when(pl.program_id(0) == 0)
        def _():
            dpw_ref[...] = jnp.zeros_like(dpw_ref)

        dout, dw_rows = _rmsnorm_bwd(o_ref[...], pw_ref[...], d_ref[...])
        dpw_ref[...] += jnp.sum(dw_rows, 0, keepdims=True)
        db = dout.astype(BF16)
        do_ref[...] = db
        for g in range(4 * NH):
            dy_ref[g] = lax.dot_general(db, w_ref[g], (((1,), (1,)), ((), ())), preferred_element_type=F32)

    rows = pl.BlockSpec((tm, D_MODEL), lambda i: (i, 0))
    vec = pl.BlockSpec((1, D_MODEL), lambda i: (0, 0))
    return _pcall(
        body, name=name, grid=(n // tm,),
        in_specs=[rows, rows, vec, pl.BlockSpec((4 * NH, DH, D_MODEL), lambda i: (0, 0, 0))],
        out_specs=[rows, pl.BlockSpec((4 * NH, tm, DH), lambda i: (0, i, 0)), vec],
        out_shape=[jax.ShapeDtypeStruct((n, D_MODEL), BF16),
                   jax.ShapeDtypeStruct((4 * NH, n, DH), F32),
                   jax.ShapeDtypeStruct((1, D_MODEL), F32)],
        compiler_params=_cparams(("arbitrary",)),
    )(dxn, out, post_w, wout_g)


def _dwout(ys, dout, name):
    n = dout.shape[0]
    tm = _row_tile(n)

    def body(y0, y1, y2, y3, d_ref, dw_ref):
        @pl.when(pl.program_id(0) == 0)
        def _():
            dw_ref[...] = jnp.zeros_like(dw_ref)

        d = d_ref[...]
        for m, yr in enumerate((y0, y1, y2, y3)):
            for h in range(NH):
                dw_ref[m * NH + h] += lax.dot_general(yr[h], d, (((0,), (0,)), ((), ())),
                                                      preferred_element_type=F32)

    yspec = pl.BlockSpec((NH, tm, DH), lambda i: (0, i, 0))
    return _pcall(
        body, name=name, grid=(n // tm,),
        in_specs=[yspec] * 4 + [pl.BlockSpec((tm, D_MODEL), lambda i: (i, 0))],
        out_specs=pl.BlockSpec((4 * NH, DH, D_MODEL), lambda i: (0, 0, 0)),
        out_shape=jax.ShapeDtypeStruct((4 * NH, DH, D_MODEL), F32),
        compiler_params=_cparams(("arbitrary",)),
    )(*ys, dout)


def _source_specs(sources, rows_first):
    gs = GROUPS_PER_STEP
    spans, specs, j0 = [], [], 0
    for a in sources:
        nblk = a.shape[0] // gs
        spans.append((j0, j0 + nblk))
        shape = (gs, _row_tile(a.shape[1]), DH)

        def blk(j, j0=j0, nblk=nblk):
            return jnp.clip(j - j0, 0, nblk - 1)

        if rows_first:
            specs.append(pl.BlockSpec(shape, (lambda i, j, blk=blk: (blk(j), i, 0))))
        else:
            specs.append(pl.BlockSpec(shape, (lambda j, i, blk=blk: (blk(j), i, 0))))
        j0 += nblk
    return spans, specs


def _dh_prenorm_bwd(sources, w_g, x, pre_w, dxn, name, side=None):
    n = x.shape[0]
    tm = _row_tile(n)
    gs = GROUPS_PER_STEP
    nj = N_GROUPS // gs
    ni = n // tm
    spans, src_specs = _source_specs(sources, True)
    ns = len(sources)
    n_side = len(side.operands) if side else 0

    def body(*refs):
        src = refs[:ns]
        w_ref, x_ref, pw_ref, d_ref = refs[ns:ns + 4]
        side_in = refs[ns + 4:ns + 4 + n_side]
        dx_ref, dpw_ref = refs[ns + 4 + n_side:ns + 6 + n_side]
        side_out = refs[ns + 6 + n_side:ns + 6 + 2 * n_side]
        acc = refs[ns + 6 + 2 * n_side]
        sems = refs[ns + 7 + 2 * n_side:]
        i, j = pl.program_id(0), pl.program_id(1)

        if side is not None:
            @pl.when((i == 0) & (j == 0))
            def _():
                _comm_start(side.copies(side_in, side_out, sems))

        @pl.when((i == 0) & (j == 0))
        def _():
            dpw_ref[...] = jnp.zeros_like(dpw_ref)

        @pl.when(j == 0)
        def _():
            acc[...] = jnp.zeros_like(acc)

        for s_ref, (lo, hi) in zip(src, spans):
            @pl.when((j >= lo) & (j < hi))
            def _(s_ref=s_ref):
                four = jnp.concatenate([s_ref[k] for k in range(gs)], axis=-1)
                acc[...] += lax.dot_general(four, w_ref[...], (((1,), (1,)), ((), ())), preferred_element_type=F32)

        @pl.when(j == nj - 1)
        def _():
            dx, dw_rows = _rmsnorm_bwd(x_ref[...], pw_ref[...], acc[...])
            dx_ref[...] = d_ref[...] + dx
            dpw_ref[...] += jnp.sum(dw_rows, 0, keepdims=True)

        if side is not None:
            @pl.when((i == ni - 1) & (j == nj - 1))
            def _():
                _comm_wait(side.copies(side_in, side_out, sems))

    hbm = pl.BlockSpec(memory_space=pl.ANY)
    rows = pl.BlockSpec((tm, D_MODEL), lambda i, j: (i, 0))
    vec = pl.BlockSpec((1, D_MODEL), lambda i, j: (0, 0))
    dx, dpw, *side_res = _pcall(
        body, name=name, grid=(ni, nj),
        in_specs=src_specs + [pl.BlockSpec((None, D_MODEL, gs * DH), lambda i, j: (j, 0, 0)), rows, vec, rows]
        + [hbm] * n_side,
        out_specs=[rows, vec] + [hbm] * n_side,
        out_shape=[jax.ShapeDtypeStruct((n, D_MODEL), F32), jax.ShapeDtypeStruct((1, D_MODEL), F32)]
        + (side.out_shapes if side else []),
        scratch_shapes=[pltpu.VMEM((tm, D_MODEL), F32)] + (_comm_scratch(side) if side else []),
        compiler_params=_cparams(("arbitrary", "arbitrary")),
    )(*sources, w_g, x, pre_w, dxn, *(side.operands if side else []))
    return dx, dpw, side_res


def _dwin(hb, sources, name):
    n = hb.shape[0]
    tm = _row_tile(n)
    gs = GROUPS_PER_STEP
    ni, nj = n // tm, N_GROUPS // gs
    spans, src_specs = _source_specs(sources, True)
    ns = len(sources)

    def body(*refs):
        h_ref = refs[0]
        src = refs[1:1 + ns]
        out_ref, acc, sem = refs[1 + ns:]
        i, j = pl.program_id(0), pl.program_id(1)

        @pl.when((i == 0) & (j == 0))
        def _():
            acc[...] = jnp.zeros_like(acc)

        h = h_ref[...]
        for s_ref, (lo, hi) in zip(src, spans):
            @pl.when((j >= lo) & (j < hi))
            def _(s_ref=s_ref):
                four = jnp.concatenate([s_ref[k] for k in range(gs)], axis=-1)
                acc[j] += jnp.dot(h, four, preferred_element_type=F32)

        @pl.when((i == ni - 1) & (j == nj - 1))
        def _():
            done = pltpu.make_async_copy(acc, out_ref, sem)
            done.start()
            done.wait()

    return _pcall(
        body, name=name, grid=(ni, nj),
        in_specs=[pl.BlockSpec((D_MODEL, tm), lambda i, j: (0, i))] + src_specs,
        out_specs=pl.BlockSpec(memory_space=pl.ANY),
        out_shape=jax.ShapeDtypeStruct((nj, D_MODEL, gs * DH), F32),
        scratch_shapes=[pltpu.VMEM((nj, D_MODEL, gs * DH), F32), pltpu.SemaphoreType.DMA],
        compiler_params=_cparams(("arbitrary", "arbitrary")),
    )(jnp.transpose(hb), *sources)


def _adamw_math(w, g, m, v):
    c1 = 1.0 - ADAM_B1 ** ADAM_STEP
    c2 = 1.0 - ADAM_B2 ** ADAM_STEP
    nm = ADAM_B1 * m + (1.0 - ADAM_B1) * g
    nv = ADAM_B2 * v + (1.0 - ADAM_B2) * (g * g)
    return -ADAM_LR * ((nm / c1) / (jnp.sqrt(nv / c2) + ADAM_EPS) + ADAM_WD * w), nm, nv


def _adamw(w, g, m, v, name):
    r, c = w.shape
    tr = 256 if r % 256 == 0 else r

    def body(w_ref, g_ref, m_ref, v_ref, d_ref, nm_ref, nv_ref):
        d_ref[...], nm_ref[...], nv_ref[...] = _adamw_math(w_ref[...], g_ref[...], m_ref[...], v_ref[...])

    spec = pl.BlockSpec((tr, c), lambda i: (i, 0))
    return _pcall(
        body, name=name, grid=(r // tr,),
        in_specs=[spec] * 4, out_specs=[spec] * 3,
        out_shape=[jax.ShapeDtypeStruct((r, c), F32)] * 3,
        compiler_params=_cparams(("parallel",)),
    )(w, g, m, v)


def _sum_adamw(parts, w, m, v, name):
    r, c = w.shape
    tr = 128 if r % 128 == 0 else r

    def body(p_ref, w_ref, m_ref, v_ref, g_ref, d_ref, nm_ref, nv_ref):
        g = p_ref[0]
        for k in range(1, N_DEV):
            g = g + p_ref[k]
        g_ref[...] = g
        d_ref[...], nm_ref[...], nv_ref[...] = _adamw_math(w_ref[...], g, m_ref[...], v_ref[...])

    spec = pl.BlockSpec((tr, c), lambda i: (i, 0))
    return _pcall(
        body, name=name, grid=(r // tr,),
        in_specs=[pl.BlockSpec((N_DEV, tr, c), lambda i: (0, i, 0))] + [spec] * 3, out_specs=[spec] * 4,
        out_shape=[jax.ShapeDtypeStruct((r, c), F32)] * 4,
        compiler_params=_cparams(("parallel",)),
    )(parts, w, m, v)


def _me():
    return lax.axis_index("x"), lax.axis_index("y"), lax.axis_index("c")


def _flat(x, y, c):
    return 4 * x + 2 * y + c


def _peer(k):
    x, y, c = _me()
    return (x ^ ((k >> 2) & 1), y ^ ((k >> 1) & 1), c ^ (k & 1))


def _gather_plan(blocks):
    def copies(x_refs, out_refs, sems):
        send_sems, recv_sems, local_sems = sems
        me = _flat(*_me())
        local = [pltpu.make_async_copy(x, o.at[me], local_sems.at[a]) for a, (x, o) in enumerate(zip(x_refs, out_refs))]
        outgoing, incoming = [], []
        for k in range(1, N_DEV):
            src = _flat(*_peer(k))
            for a, (x, o) in enumerate(zip(x_refs, out_refs)):
                for slot, group in ((me, outgoing), (src, incoming)):
                    group.append(pltpu.make_async_remote_copy(
                        src_ref=x, dst_ref=o.at[slot], send_sem=send_sems.at[a, k - 1], recv_sem=recv_sems.at[a, k - 1],
                        device_id=_peer(k), device_id_type=MESH))
        return local, outgoing, incoming

    return _Comm(list(blocks), [jax.ShapeDtypeStruct((N_DEV,) + b.shape, b.dtype) for b in blocks], copies)


def _exchange_plan(sends):
    def copies(s_refs, out_refs, sems):
        send_sems, recv_sems, local_sems = sems
        me = _flat(*_me())
        local = [pltpu.make_async_copy(s.at[me], o.at[0], local_sems.at[i]) for i, (s, o) in enumerate(zip(s_refs, out_refs))]
        outgoing = []
        for k in range(1, N_DEV):
            to = _flat(*_peer(k))
            for i, (s, o) in enumerate(zip(s_refs, out_refs)):
                outgoing.append(pltpu.make_async_remote_copy(
                    src_ref=s.at[to], dst_ref=o.at[k], send_sem=send_sems.at[i, k - 1], recv_sem=recv_sems.at[i, k - 1],
                    device_id=_peer(k), device_id_type=MESH))
        return local, outgoing, outgoing

    return _Comm(list(sends), [jax.ShapeDtypeStruct(s.shape, s.dtype) for s in sends], copies)


def _comm_scratch(plan):
    n = len(plan.operands)
    return [pltpu.SemaphoreType.DMA((n, N_DEV - 1)), pltpu.SemaphoreType.DMA((n, N_DEV - 1)),
            pltpu.SemaphoreType.DMA((n,))]


def _comm_start(copies):
    local, outgoing, _ = copies
    for cp in local + outgoing:
        cp.start()


def _comm_wait(copies):
    local, outgoing, incoming = copies
    for cp in incoming:
        cp.wait_recv()
    for cp in outgoing:
        cp.wait_send()
    for cp in local:
        cp.wait()


def _run_comm(plan, name):
    n = len(plan.operands)

    def body(*refs):
        copies = plan.copies(refs[:n], refs[n:2 * n], refs[2 * n:])
        _comm_start(copies)
        _comm_wait(copies)

    return _pcall(
        body, name=name,
        in_specs=[pl.BlockSpec(memory_space=pl.ANY)] * n,
        out_specs=[pl.BlockSpec(memory_space=pl.ANY)] * n,
        out_shape=plan.out_shapes,
        scratch_shapes=_comm_scratch(plan),
    )(*plan.operands)


def _sum_slots(a, name):
    r = a.shape[1]

    def body(a_ref, o_ref):
        acc = a_ref[0]
        for d in range(1, N_DEV):
            acc = acc + a_ref[d]
        o_ref[...] = acc

    return _pcall(body, name=name, out_shape=jax.ShapeDtypeStruct((r, 128), F32), compiler_params=_cparams())(a)


def _all_reduce_small(blk, name):
    r = blk.shape[0]

    def body(x_ref, out_ref, gath, send_sems, recv_sems):
        me = _flat(*_me())
        gath[me] = x_ref[...]
        copies = []
        for k in range(1, N_DEV):
            cp = pltpu.make_async_remote_copy(
                src_ref=x_ref, dst_ref=gath.at[me],
                send_sem=send_sems.at[k - 1], recv_sem=recv_sems.at[k - 1],
                device_id=_peer(k), device_id_type=MESH)
            cp.start()
            copies.append(cp)
        for k in range(1, N_DEV):
            src = _flat(*_peer(k))
            pltpu.make_async_remote_copy(
                src_ref=x_ref, dst_ref=gath.at[src],
                send_sem=send_sems.at[k - 1], recv_sem=recv_sems.at[k - 1],
                device_id=_peer(k), device_id_type=MESH).wait_recv()
        for cp in copies:
            cp.wait_send()
        acc = gath[0]
        for d in range(1, N_DEV):
            acc = acc + gath[d]
        out_ref[...] = acc

    return _pcall(
        body, name=name,
        in_specs=[pl.BlockSpec(memory_space=pltpu.VMEM)],
        out_specs=pl.BlockSpec(memory_space=pltpu.VMEM),
        out_shape=jax.ShapeDtypeStruct((r, 128), F32),
        scratch_shapes=[pltpu.VMEM((N_DEV, r, 128), F32),
                        pltpu.SemaphoreType.DMA((N_DEV - 1,)), pltpu.SemaphoreType.DMA((N_DEV - 1,))],
    )(blk)


def _heads(vec):
    return vec.reshape(NH, 1, DH)


def _rep(vec4):
    return jnp.broadcast_to(vec4.reshape(NH, 1, 1), (NH, 1, DH))


def _onehot_lane(offset):
    m = np.zeros((NH, 1, DH), np.float32)
    for h in range(NH):
        m[h, 0, offset + h] = 1.0
    return jnp.asarray(m)


_TINY = (("gdn_conv_w", (DEPTH, 4, 96)), ("rwkv_w_up", (DEPTH, 64, 32)), ("rwkv_a_up", (DEPTH, 64, 32)),
         ("sc_conv_w", (DEPTH, 3, 32)))
_TINY_ROWS = -(-sum(int(np.prod(s)) for _, s in _TINY) // 1024) * 8


def _pack_rows(arrays, rows, fill=0.0):
    flat = jnp.concatenate([a.reshape(-1) for a in arrays])
    return jnp.pad(flat, (0, rows * 128 - flat.shape[0]), constant_values=fill).reshape(rows, 128)


def _unpack_rows(p, named_shapes):
    lead = p.shape[:-2]
    flat = p.reshape(lead + (-1,))
    out, o = {}, 0
    for n, s in named_shapes:
        size = int(np.prod(s))
        out[n] = flat[..., o:o + size].reshape(lead + tuple(s))
        o += size
    return out


def _gather_last(a):
    return jnp.transpose(a, (1, 0, 2)).reshape(a.shape[1], -1)


def _split_last(a):
    r, c8 = a.shape
    return jnp.transpose(a.reshape(r, N_DEV, c8 // N_DEV), (1, 0, 2))


_SMALL = (("pre_norm_w", (DEPTH, 1024)), ("gdn_a_log", (DEPTH, 4)), ("gdn_dt_bias", (DEPTH, 4)),
          ("gdn_norm_w", (DEPTH, 64)), ("rwkv_mu", (DEPTH, 1152)), ("rwkv_w0", (DEPTH, 256)),
          ("rwkv_a0", (DEPTH, 256)), ("rwkv_k_k", (DEPTH, 256)), ("rwkv_k_a", (DEPTH, 256)),
          ("rwkv_r_k", (DEPTH, 256)), ("rwkv_ln_w", (DEPTH, 256)), ("rwkv_ln_b", (DEPTH, 256)),
          ("gla_a_up", (DEPTH, 16, 128)), ("gla_a_bias", (DEPTH, 128)), ("gla_norm_w", (DEPTH, 64)),
          ("post_norm_w", (DEPTH, 1024)), ("loss", ()))
_SMALL_ROWS = -(-sum(int(np.prod(s)) for _, s in _SMALL) // 1024) * 8


def _big_weights(w_in_all, w_out_all, l):
    return dict(w_g=_regroup_in(w_in_all, f"regroup_in{l}"),
                wout_g=w_out_all.reshape(4 * NH, DH, D_MODEL).astype(BF16))


def _layer_params(wts, tiny, l):
    conv = _gather_last(tiny["gdn_conv_w"][:, l])
    q = {}
    q["gdn_conv"] = jnp.transpose(conv.reshape(GDN_TAPS, 12, DH), (1, 0, 2))
    q["gdn_prm"] = [_rep(wts["gdn_a_log"][l]), _rep(wts["gdn_dt_bias"][l]),
                    jnp.broadcast_to(wts["gdn_norm_w"][l].reshape(1, 1, DH), (NH, 1, DH))]
    q["gdn_cst"] = [_onehot_lane(0), _onehot_lane(NH)]
    q["rwkv_mu"] = wts["rwkv_mu"][l].reshape(18, 1, DH)
    w_up = jnp.transpose(_gather_last(tiny["rwkv_w_up"][:, l]).reshape(64, NH, DH), (1, 0, 2))
    a_up = jnp.transpose(_gather_last(tiny["rwkv_a_up"][:, l]).reshape(64, NH, DH), (1, 0, 2))
    q["rwkv_prm"] = [_heads(wts["rwkv_w0"][l]), w_up, _heads(wts["rwkv_a0"][l]), a_up,
                     _heads(wts["rwkv_k_k"][l]), _heads(wts["rwkv_k_a"][l]), _heads(wts["rwkv_r_k"][l]),
                     _heads(wts["rwkv_ln_w"][l]), _heads(wts["rwkv_ln_b"][l])]
    sc = _gather_last(tiny["sc_conv_w"][:, l])
    q["sc_conv"] = jnp.transpose(sc.reshape(SC_TAPS, NH, DH), (1, 0, 2))
    gla_up = jnp.transpose(wts["gla_a_up"][l].reshape(16, NH, GLA_HEAD_K), (1, 0, 2))
    gla_up = jnp.pad(gla_up, ((0, 0), (0, DH - 16), (0, DH - GLA_HEAD_K)))
    gla_b = jnp.pad(wts["gla_a_bias"][l].reshape(NH, 1, GLA_HEAD_K), ((0, 0), (0, 0), (0, DH - GLA_HEAD_K)))
    q["gla_prm"] = [gla_up, gla_b, jnp.broadcast_to(wts["gla_norm_w"][l].reshape(1, 1, DH), (NH, 1, DH))]
    q["pre_w"] = wts["pre_norm_w"][l].reshape(1, D_MODEL)
    q["post_w"] = wts["post_norm_w"][l].reshape(1, D_MODEL)
    return q


def _mixer_inputs(p, cq, pm):
    gdn = [(cq, 4, 0), (cq, 4, 1), (cq, 4, 2), (p, 4, G_GDN // 4 + 3), (p, 1, G_GDN_AB)]
    rwkv = [(pm, 4, 0), (pm, 4, 1), (pm, 4, 2), (pm, 4, 3), (pm, 1, 16), (pm, 1, 17)]
    gla = [(p, 4, G_GLA // 4 + k) for k in range(4)] + [(p, 1, G_GLA_AD)]
    return gdn, rwkv, gla


def _layer_fwd(x, q, nb, t, l, side=None):
    hb, p = _norm_proj(x, q["pre_w"], q["w_g"], f"norm_proj{l}")
    cq = _conv_fwd(p, G_GDN, 12, q["gdn_conv"], nb, t, f"gdn_conv{l}")
    pm = _mix_fwd(p, q["rwkv_mu"], nb, t, f"rwkv_mix{l}")
    gdn_in, rwkv_in, gla_in = _mixer_inputs(p, cq, pm)
    y_gdn, ck_gdn, _ = _mixer_fwd(_gdn_chunk, f"gdn_fwd{l}", gdn_in, q["gdn_prm"], q["gdn_cst"], nb, t, n_kept=1)
    y_rwkv, ck_rwkv, side_res = _mixer_fwd(_rwkv_chunk, f"rwkv_fwd{l}", rwkv_in, q["rwkv_prm"], [], nb, t,
                                           first_fn=_rwkv_first_chunk, side=side, n_kept=1)
    y_sc = _sc_fwd(p, q["sc_conv"], nb, t, f"sc_fwd{l}")
    y_gla, ck_gla, _ = _mixer_fwd(_gla_chunk, f"gla_fwd{l}", gla_in, q["gla_prm"], [], nb, t)
    ys = (y_gdn, y_rwkv, y_sc, y_gla)
    out, xn = _out_proj_norm(ys, q["wout_g"], x, q["post_w"], f"out_proj{l}")
    saved = dict(x=x, hb=hb, p=p, cq=cq, pm=pm, ys=ys, out=out, ck=(ck_gdn, ck_rwkv, ck_gla))
    return xn, saved, side_res


def _layer_bwd(dxn, q, sv, nb, t, l, side=None, exchange_own=False):
    p, cq, pm, ys = sv["p"], sv["cq"], sv["pm"], sv["ys"]
    dout, dy, d_post = _post_bwd(dxn, sv["out"], q["post_w"], q["wout_g"], f"post_bwd{l}")
    d_wout = _dwout(ys, dout, f"dwout{l}").reshape(N_DEV, 128, D_MODEL)
    gdn_in, rwkv_in, gla_in = _mixer_inputs(p, cq, pm)
    ck_gdn, ck_rwkv, ck_gla = sv["ck"]
    g = {}

    (d_conv, dz, dab), (da_log, ddt, dnw), _ = _mixer_bwd(
        _gdn_chunk, f"gdn_bwd{l}", gdn_in, q["gdn_prm"], q["gdn_cst"], ck_gdn, dy, 0,
        [(12, F32), (4, BF16), (1, BF16)], [(0, 0), (0, 4), (0, 8), (1, 0), (2, 0)], nb, t)
    dconv_in, d_gconv = _conv_bwd(p, G_GDN, 12, q["gdn_conv"], d_conv, nb, t, f"gdn_conv_bwd{l}")
    g["gdn_conv_w"] = jnp.transpose(d_gconv.sum(1), (1, 0, 2)).reshape(GDN_TAPS, 768)
    g["gdn_a_log"] = da_log.sum((0, 2, 3))
    g["gdn_dt_bias"] = ddt.sum((0, 2, 3))
    g["gdn_norm_w"] = dnw.sum((0, 1, 2))

    (d_pm,), d_rprm, side_res = _mixer_bwd(
        _rwkv_chunk, f"rwkv_bwd{l}", rwkv_in, q["rwkv_prm"], [], ck_rwkv, dy, 1,
        [(18, F32)], [(0, 0), (0, 4), (0, 8), (0, 12), (0, 16), (0, 17)], nb, t, first_fn=_rwkv_first_chunk,
        side=side)
    dp_rwkv, d_mu = _mix_bwd(p, q["rwkv_mu"], d_pm, nb, t, f"rwkv_mix_bwd{l}")
    g["rwkv_mu"] = d_mu.sum(1).reshape(1152)
    rp = [a.sum(0) for a in d_rprm]
    g["rwkv_w0"] = rp[0].reshape(256)
    g["rwkv_w_up"] = jnp.transpose(rp[1], (1, 0, 2)).reshape(64, 256)
    g["rwkv_a0"] = rp[2].reshape(256)
    g["rwkv_a_up"] = jnp.transpose(rp[3], (1, 0, 2)).reshape(64, 256)
    for i, nme in enumerate(("rwkv_k_k", "rwkv_k_a", "rwkv_r_k", "rwkv_ln_w", "rwkv_ln_b")):
        g[nme] = rp[4 + i].reshape(256)

    dp_sc, d_scw = _sc_bwd(p, q["sc_conv"], dy, nb, t, f"sc_bwd{l}")
    g["sc_conv_w"] = jnp.transpose(d_scw.sum(1), (1, 0, 2)).reshape(SC_TAPS, 256)

    (dp_gla, dad), (d_aup, d_ab, d_gnw), _ = _mixer_bwd(
        _gla_chunk, f"gla_bwd{l}", gla_in, q["gla_prm"], [], ck_gla, dy, 3,
        [(16, BF16), (1, BF16)], [(0, 0), (0, 4), (0, 8), (0, 12), (1, 0)], nb, t)
    g["gla_a_up"] = jnp.transpose(d_aup.sum(0)[:, :16, :GLA_HEAD_K], (1, 0, 2)).reshape(16, 128)
    g["gla_a_bias"] = d_ab.sum(0)[:, 0, :GLA_HEAD_K].reshape(128)
    g["gla_norm_w"] = d_gnw.sum((0, 1, 2))

    singles = jnp.concatenate([dab, dp_rwkv[16:18], dad], axis=0)
    sources = [dconv_in, dz, dp_rwkv, dp_sc, dp_gla, singles]
    d_win = _regroup_out(_dwin(sv["hb"], sources, f"dwin{l}"), f"regroup_out{l}")
    own = _exchange_plan([d_win, d_wout]) if exchange_own else None
    dx, d_pre, got = _dh_prenorm_bwd(sources, q["w_g"], sv["x"], q["pre_w"], dxn, f"dh_bwd{l}", own)
    if exchange_own:
        d_win, d_wout = got
    g["pre_norm_w"] = d_pre.reshape(D_MODEL)
    g["post_norm_w"] = d_post.reshape(D_MODEL)
    return dx, g, d_win, d_wout, side_res


def _local_step(x, tgt, wts, tiny, w_in_all, w_out_all, later_shards=None):
    nb, t, d = x.shape
    xf = x.reshape(nb * t, d)
    overlap = later_shards is not None
    qs, saved = [], []
    big = _big_weights(w_in_all[0], w_out_all[0], 0)
    for l in range(DEPTH):
        q = dict(_layer_params(wts, tiny, l), **big)
        nxt = l + 1 < DEPTH
        side = _gather_plan(later_shards[l]) if overlap and nxt else None
        xf, sv, got = _layer_fwd(xf, q, nb, t, l, side)
        if nxt:
            big = _big_weights(*(got if overlap else (w_in_all[l + 1], w_out_all[l + 1])), l + 1)
        qs.append(q)
        saved.append(sv)
    dxf, lpart = _loss_grad(xf, tgt.reshape(nb * t, d), "loss")
    grads, d_win, d_wout = [None] * DEPTH, [None] * DEPTH, [None] * DEPTH
    for l in reversed(range(DEPTH)):
        side = _exchange_plan([d_win[l + 1], d_wout[l + 1]]) if overlap and l + 1 < DEPTH else None
        dxf, grads[l], d_win[l], d_wout[l], got = _layer_bwd(dxf, qs[l], saved[l], nb, t, l, side,
                                                             exchange_own=overlap and l == 0)
        if side is not None:
            d_win[l + 1], d_wout[l + 1] = got
    small = {k: jnp.stack([grads[l][k] for l in range(DEPTH)]) for k in grads[0]}
    return lpart[0, 0], dxf.reshape(nb, t, d), small, d_win, d_wout


_WEIGHTS = ("pre_norm_w", "w_in", "gdn_conv_w", "gdn_a_log", "gdn_dt_bias", "gdn_norm_w", "rwkv_mu", "rwkv_w0",
            "rwkv_w_up", "rwkv_a0", "rwkv_a_up", "rwkv_k_k", "rwkv_k_a", "rwkv_r_k", "rwkv_ln_w", "rwkv_ln_b",
            "sc_conv_w", "gla_a_up", "gla_a_bias", "gla_norm_w", "w_out", "post_norm_w")


def kernel(x, pre_norm_w, w_in, gdn_conv_w, gdn_a_log, gdn_dt_bias, gdn_norm_w, rwkv_mu, rwkv_w0, rwkv_w_up, rwkv_a0, rwkv_a_up, rwkv_k_k, rwkv_k_a, rwkv_r_k, rwkv_ln_w, rwkv_ln_b, sc_conv_w, gla_a_up, gla_a_bias, gla_norm_w, w_out, post_norm_w, loss_target, m_pre_norm_w, m_w_in, m_gdn_conv_w, m_gdn_a_log, m_gdn_dt_bias, m_gdn_norm_w, m_rwkv_mu, m_rwkv_w0, m_rwkv_w_up, m_rwkv_a0, m_rwkv_a_up, m_rwkv_k_k, m_rwkv_k_a, m_rwkv_r_k, m_rwkv_ln_w, m_rwkv_ln_b, m_sc_conv_w, m_gla_a_up, m_gla_a_bias, m_gla_norm_w, m_w_out, m_post_norm_w, v_pre_norm_w, v_w_in, v_gdn_conv_w, v_gdn_a_log, v_gdn_dt_bias, v_gdn_norm_w, v_rwkv_mu, v_rwkv_w0, v_rwkv_w_up, v_rwkv_a0, v_rwkv_a_up, v_rwkv_k_k, v_rwkv_k_a, v_rwkv_r_k, v_rwkv_ln_w, v_rwkv_ln_b, v_sc_conv_w, v_gla_a_up, v_gla_a_bias, v_gla_norm_w, v_w_out, v_post_norm_w):
    env = dict(locals())
    w = {n: env[n] for n in _WEIGHTS}
    m = {n: env["m_" + n] for n in _WEIGHTS}
    v = {n: env["v_" + n] for n in _WEIGHTS}
    tiny_names = [n for n, _ in _TINY]

    w_in_b, w_out_b = w_in.astype(BF16), w_out.astype(BF16)
    w_in_0, w_out_0, tiny_all = _run_comm(
        _gather_plan([w_in_b[0], w_out_b[0], _pack_rows([w[n] for n in tiny_names], _TINY_ROWS)]), "gather_weights")
    tiny = _unpack_rows(tiny_all, _TINY)

    lpart, grad_x, small, r_win, r_wout = _local_step(
        x, loss_target, w, tiny, [w_in_0], [w_out_0], later_shards=[(w_in_b[l], w_out_b[l]) for l in range(1, DEPTH)])

    tiny_send = jnp.stack([_pack_rows([_split_last(small[n][l])[d] for n in tiny_names for l in range(DEPTH)],
                                      _TINY_ROWS) for d in range(N_DEV)])
    (r_tiny,) = _run_comm(_exchange_plan([tiny_send]), "scatter_grads")
    grads, delta, new_m, new_v = {}, {}, {}, {}
    for n, parts in (("w_in", r_win), ("w_out", r_wout)):
        res = [_sum_adamw(parts[l], w[n][l], m[n][l], v[n][l], f"adamw_{n}{l}") for l in range(DEPTH)]
        grads[n], delta[n], new_m[n], new_v[n] = [jnp.stack(o) for o in zip(*res)]
    tiny_sum = _sum_slots(r_tiny, "sum_tiny").reshape(-1)
    o = 0
    for n, s in _TINY:
        size = int(np.prod(s))
        grads[n] = tiny_sum[o:o + size].reshape(s)
        o += size

    small = dict(small)
    small["loss"] = lpart
    red = _unpack_rows(_all_reduce_small(_pack_rows([small[n] for n, _ in _SMALL], _SMALL_ROWS), "reduce_small"),
                       _SMALL)
    loss = red.pop("loss")
    grads.update(red)

    rest = [n for n in _WEIGHTS if n not in ("w_in", "w_out")]
    rest_shapes = [(n, w[n].shape) for n in rest]
    rows = -(-sum(int(np.prod(s)) for _, s in rest_shapes) // 1024) * 8
    outs = _adamw(_pack_rows([w[n] for n in rest], rows), _pack_rows([grads[n] for n in rest], rows),
                  _pack_rows([m[n] for n in rest], rows), _pack_rows([v[n] for n in rest], rows, 1.0), "adamw_rest")
    for dst, packed in zip((delta, new_m, new_v), outs):
        dst.update(_unpack_rows(packed, rest_shapes))

    return (loss, grad_x, *[grads[n] for n in _WEIGHTS], *[delta[n] for n in _WEIGHTS],
            *[new_m[n] for n in _WEIGHTS], *[new_v[n] for n in _WEIGHTS])
```

```python
import collections
import functools
import math

import numpy as np
import jax
import jax.numpy as jnp
from jax import lax
from jax.experimental import pallas as pl
from jax.experimental.pallas import tpu as pltpu

F32 = jnp.float32
BF16 = jnp.bfloat16

D_MODEL = 1024
DEPTH = 2
NH = 4
DH = 64
CH = 64
EPS = 1e-6
RWKV_GN_EPS = 64e-5
GLA_HEAD_K = 32
GLA_TAU = 16.0
GDN_TAPS = 4
SC_TAPS = 3
D_IN = 3992
N_DEV = 8
SHARD_COLS = D_IN // N_DEV

G_GDN = 0
G_RWKV = 16
G_SC = 32
G_GLA = 48
G_GDN_AB, G_RWKV_WD, G_RWKV_AD, G_GLA_AD = 64, 65, 66, 67
N_GROUPS = 68
GROUPS_PER_STEP = 4
TIME_BLOCK = 256
RWKV_EXACT_STEPS = 16

C_GDN, C_RWKV, C_SC, C_GLA = 0, 1032, 2184, 3208

ADAM_LR, ADAM_B1, ADAM_B2, ADAM_EPS, ADAM_WD, ADAM_STEP = 0.001, 0.9, 0.999, 1e-08, 0.01, 10

VMEM_LIMIT = 56 * 1024 * 1024
MESH = pl.DeviceIdType.MESH

_pcall = pl.pallas_call

_Comm = collections.namedtuple("_Comm", "operands out_shapes copies")


def _cparams(sem=None):
    if sem is None:
        return pltpu.CompilerParams(vmem_limit_bytes=VMEM_LIMIT)
    return pltpu.CompilerParams(dimension_semantics=sem, vmem_limit_bytes=VMEM_LIMIT)


def _group_segments():
    table = [(G_GDN + i, C_GDN + DH * i, DH) for i in range(16)]
    table.append((G_GDN_AB, C_GDN + 1024, 8))
    table += [(G_RWKV + i, C_RWKV + DH * i, DH) for i in range(16)]
    table += [(G_RWKV_WD, C_RWKV + 1024, DH), (G_RWKV_AD, C_RWKV + 1088, DH)]
    table += [(G_SC + 4 * j + k, C_SC + 256 * k + DH * j, DH) for j in range(NH) for k in range(4)]
    for h in range(NH):
        table += [(G_GLA + h, C_GLA + GLA_HEAD_K * h, GLA_HEAD_K),
                  (G_GLA + 4 + h, C_GLA + 128 + GLA_HEAD_K * h, GLA_HEAD_K),
                  (G_GLA + 8 + h, C_GLA + 256 + DH * h, DH),
                  (G_GLA + 12 + h, C_GLA + 512 + DH * h, DH)]
    table.append((G_GLA_AD, C_GLA + 768, 16))
    segs, padded = [], []
    for g, c, n in table:
        if n < DH:
            padded.append(g)
        a = 0
        while n > 0:
            d, off = divmod(c, SHARD_COLS)
            ln = min(n, SHARD_COLS - off)
            segs.append((g, a, d, off, ln))
            c, a, n = c + ln, a + ln, n - ln
    return segs, padded


_SEGMENTS, _PADDED_GROUPS = _group_segments()


def _dn(ta, tb):
    return (((1 if ta else 2,), (2 if tb else 1,)), ((0,), (0,)))


def _hdot(a, b, ta=False, tb=False):
    return lax.dot_general(a, b, _dn(ta, tb), precision=lax.Precision.HIGH, preferred_element_type=F32)


def _r(x):
    return x.astype(BF16)


def _rdot(a, b, ta=False, tb=False):
    return lax.dot_general(_r(a), _r(b), _dn(ta, tb), preferred_element_type=F32)


@jax.custom_vjp
def _bmm(a, b):
    return _rdot(a, b)


def _bmm_fwd(a, b):
    return _rdot(a, b), (a, b)


def _bmm_bwd(res, g):
    a, b = res
    return _rdot(g, b, tb=True), _rdot(a, g, ta=True)


_bmm.defvjp(_bmm_fwd, _bmm_bwd)


@jax.custom_vjp
def _bmm_nt(a, b):
    return _rdot(a, b, tb=True)


def _bmm_nt_fwd(a, b):
    return _rdot(a, b, tb=True), (a, b)


def _bmm_nt_bwd(res, g):
    a, b = res
    return _rdot(g, b), _rdot(g, a, ta=True)


_bmm_nt.defvjp(_bmm_nt_fwd, _bmm_nt_bwd)


@jax.custom_vjp
def _bmm_tn(a, b):
    return _rdot(a, b, ta=True)


def _bmm_tn_fwd(a, b):
    return _rdot(a, b, ta=True), (a, b)


def _bmm_tn_bwd(res, g):
    a, b = res
    return _rdot(b, g, tb=True), _rdot(a, g)


_bmm_tn.defvjp(_bmm_tn_fwd, _bmm_tn_bwd)


def _tri(n):
    i = lax.broadcasted_iota(jnp.int32, (n, n), 0)
    j = lax.broadcasted_iota(jnp.int32, (n, n), 1)
    return i >= j, i > j, i == j


def _heads_of(x, like):
    n = like.shape[0]
    if x.ndim == 2:
        return jnp.broadcast_to(x[None], (n,) + x.shape)
    seqs = x.shape[0]
    return jnp.broadcast_to(x[:, None], (seqs, n // seqs) + x.shape[1:]).reshape((n,) + x.shape[1:])


def _cumsum_rows(x):
    incl, _, _ = _tri(x.shape[-2])
    return _hdot(_heads_of(incl.astype(F32), x), x)


@jax.custom_vjp
def _inv_unit_lower(a):
    n = a.shape[-1]
    _, _, eye = _tri(n)
    pw = -a
    inv = eye.astype(F32) + pw
    for _ in range(math.ceil(math.log2(n)) - 1):
        pw = _hdot(pw, pw)
        inv = inv + _hdot(inv, pw)
    return inv


def _inv_unit_lower_fwd(a):
    inv = _inv_unit_lower(a)
    return inv, inv


def _inv_unit_lower_bwd(inv, g):
    return (-_hdot(_hdot(inv, g, ta=True), inv, tb=True),)


_inv_unit_lower.defvjp(_inv_unit_lower_fwd, _inv_unit_lower_bwd)


@jax.custom_vjp
def _inv_reuse(a, inv):
    return inv


def _inv_reuse_fwd(a, inv):
    return inv, inv


def _inv_reuse_bwd(inv, g):
    return _inv_unit_lower_bwd(inv, g)[0], jnp.zeros_like(inv)


_inv_reuse.defvjp(_inv_reuse_fwd, _inv_reuse_bwd)


def _silu(x):
    return x * jax.nn.sigmoid(x)


def _t(x):
    return jnp.swapaxes(x, -1, -2)


def _gdn_chunk(prm, cst, ins, s, tinv=None):
    a_log, dt_b, nw = prm
    m_a, m_b = cst
    cq, ck, cv, z, ab = ins
    ab = _heads_of(ab, m_a)
    incl, strict, _ = _tri(CH)
    q = _silu(cq)
    k = _silu(ck)
    v = _silu(cv)
    q = q * lax.rsqrt(jnp.sum(q * q, -1, keepdims=True) + EPS) * (DH ** -0.5)
    k = k * lax.rsqrt(jnp.sum(k * k, -1, keepdims=True) + EPS)
    a_raw = jnp.sum(ab * m_a, -1, keepdims=True)
    b_raw = jnp.sum(ab * m_b, -1, keepdims=True)
    gstep = -jnp.exp(a_log) * jax.nn.softplus(a_raw + dt_b)
    beta = jax.nn.sigmoid(b_raw)
    gc = _cumsum_rows(gstep)
    gl = jnp.sum(gstep, -2, keepdims=True)
    dec = jnp.where(incl, jnp.exp(jnp.where(incl, gc - _t(gc), 0.0)), 0.0)
    kb = k * beta
    a_mat = jnp.where(strict, _bmm_nt(kb, k) * dec, 0.0)
    tinv = _inv_unit_lower(a_mat) if tinv is None else _inv_reuse(a_mat, tinv)
    eg = jnp.exp(gc)
    u = _hdot(tinv, v * beta)
    w = _hdot(tinv, kb * eg)
    attn = _bmm_nt(q, k) * dec
    v_new = u - _bmm(w, s)
    o = _bmm(q * eg, s) + _bmm(attn, v_new)
    s_next = s * jnp.exp(gl) + _bmm_tn(k * jnp.exp(gl - gc), v_new)
    on = o * lax.rsqrt(jnp.mean(o * o, -1, keepdims=True) + EPS) * nw
    return on * _silu(z), s_next, tinv


def _gla_chunk(prm, cst, ins, st):
    a_up, a_bias, nw = prm
    q, k, v, z, ad = ins
    incl, _, _ = _tri(CH)
    la = jax.nn.log_sigmoid(_bmm(_heads_of(ad, a_up), a_up) + a_bias) * (1.0 / GLA_TAU)
    bc = _cumsum_rows(la)
    bl = jnp.sum(la, -2, keepdims=True)
    qe = q * (GLA_HEAD_K ** -0.5) * jnp.exp(bc)
    ke = k * jnp.exp(-bc)
    attn = jnp.where(incl, _bmm_nt(qe, ke), 0.0)
    o = _bmm_nt(qe, st) + _bmm(attn, v)
    st_next = st * jnp.exp(bl) + _bmm_tn(v, k * jnp.exp(bl - bc))
    on = o * lax.rsqrt(jnp.mean(o * o, -1, keepdims=True) + EPS) * nw
    return on * _silu(z), st_next


def _rwkv_chunk(prm, cst, ins, s, inv=None):
    r, v = ins[0], ins[2]
    incl, strict, _ = _tri(r.shape[-2])
    lw, kk, k2, m = _rwkv_pre(prm, ins)
    cum = _cumsum_rows(lw)
    ltot = jnp.sum(lw, -2, keepdims=True)
    n_t = -kk * jnp.exp(cum - lw)
    einv = jnp.exp(-cum)
    m_t = m * einv
    k_t = k2 * einv
    r_t = r * jnp.exp(cum)
    a_nm = jnp.where(strict, _hdot(n_t, m_t, tb=True), 0.0)
    a_nk = jnp.where(strict, _hdot(n_t, k_t, tb=True), 0.0)
    inv = _inv_unit_lower(-a_nm) if inv is None else _inv_reuse(-a_nm, inv)
    cm = _hdot(inv, _hdot(n_t, s, tb=True) + _bmm(a_nk, v))
    y = (_bmm_nt(r_t, s) + _bmm(jnp.where(incl, _hdot(r_t, m_t, tb=True), 0.0), cm)
         + _bmm(jnp.where(incl, _hdot(r_t, k_t, tb=True), 0.0), v))
    eend = jnp.exp(ltot - cum)
    s_next = s * jnp.exp(ltot) + _bmm_tn(cm, m * eend) + _bmm_tn(v, k2 * eend)
    return _rwkv_post(prm, ins, y, k2), s_next, inv


def _rwkv_pre(prm, ins):
    w0, w_up, a0, a_up, k_k, k_a = prm[:6]
    k, wd, ad = ins[1], ins[4], ins[5]
    lw = -math.exp(-0.5) * jax.nn.sigmoid(w0 + _bmm(_heads_of(jnp.tanh(wd), w_up), w_up))
    a = jax.nn.sigmoid(a0 + _bmm(_heads_of(ad, a_up), a_up))
    kk = k * k_k
    kk = kk * lax.rsqrt(jnp.sum(kk * kk, -1, keepdims=True) + EPS)
    k2 = k * (1.0 + (a - 1.0) * k_a)
    return lw, kk, k2, kk * a


def _rwkv_post(prm, ins, y, k2):
    r_k, ln_w, ln_b = prm[6:]
    r, v, z = ins[0], ins[2], ins[3]
    mean = jnp.mean(y, -1, keepdims=True)
    yc = y - mean
    var = jnp.mean(yc * yc, -1, keepdims=True)
    yn = yc * lax.rsqrt(var + RWKV_GN_EPS) * ln_w + ln_b
    bonus = jnp.sum(r * k2 * r_k, -1, keepdims=True) * v
    return (yn + bonus) * _silu(z)


@jax.custom_vjp
def _bmv(s, x):
    return jnp.sum(_r(s).astype(F32) * _r(x).astype(F32), -1, keepdims=True)


def _bmv_fwd(s, x):
    return _bmv(s, x), (s, x)


def _bmv_bwd(res, g):
    s, x = res
    return g * x, jnp.sum(_r(s).astype(F32) * _r(g).astype(F32), -2, keepdims=True)


_bmv.defvjp(_bmv_fwd, _bmv_bwd)


def _rwkv_steps(prm, cst, ins, s, steps):
    r, v = ins[0], ins[2]
    lw, kk, k2, m = _rwkv_pre(prm, ins)
    w = jnp.exp(lw)
    v_t = _t(v)
    lane = lax.broadcasted_iota(jnp.int32, (1, 1, CH), 2)
    y_t = jnp.zeros((s.shape[0], DH, CH), F32)
    for t in range(steps):
        e_t = (lane == t).astype(F32)
        row = (slice(None), slice(t, t + 1))
        sa = _bmv(s, -kk[row])
        s = s * w[row] + sa * m[row] + jnp.sum(v_t * e_t, -1, keepdims=True) * k2[row]
        y_t = y_t + _bmv(s, r[row]) * e_t
    return _rwkv_post(prm, ins, _t(y_t), k2)[:, :steps], s


def _rwkv_first_chunk(prm, cst, ins, s):
    k = RWKV_EXACT_STEPS
    y_head, s = _rwkv_steps(prm, cst, ins, s, k)
    y_tail, s, _ = _rwkv_chunk(prm, cst, [x[..., k:, :] for x in ins], s)
    return jnp.concatenate([y_head, y_tail], axis=-2), s


def _time_block(t):
    return TIME_BLOCK if t % TIME_BLOCK == 0 else t


def _load_chunk(ref, i):
    nb = ref.shape[1]
    if ref.shape[0] == NH:
        return jnp.concatenate([ref[:, b, pl.ds(i, CH), :] for b in range(nb)], axis=0)
    return ref[0, :, pl.ds(i, CH), :]


def _load_chunk_of(ref, c, i):
    if ref.shape[0] == NH:
        return ref[pl.ds(c % NH, 1), c // NH, pl.ds(i, CH), :]
    return ref[0, pl.ds(c // NH, 1), pl.ds(i, CH), :]


def _each_chain(n, fn):
    def step(c, carry):
        fn(c)
        return carry

    lax.fori_loop(0, n, step, 0)


def _mixer_fwd(chunk_fn, name, ins, prm, cst, nb, t, first_fn=None, side=None, n_kept=0):
    tb = _time_block(t)
    nt, ncb, nch = t // tb, tb // CH, nb * NH
    n_in, n_prm, n_cst = len(ins), len(prm), len(cst)
    n_main, n_side = n_in + n_prm + n_cst, len(side.operands) if side else 0

    def body(*refs):
        in_refs = refs[:n_in]
        prm_refs = refs[n_in:n_in + n_prm]
        cst_refs = refs[n_in + n_prm:n_main]
        side_in = refs[n_main:n_main + n_side]
        y_ref, ck_ref = refs[n_main + n_side:n_main + n_side + 2]
        side_out = refs[n_main + n_side + 2:n_main + 2 * n_side + 2]
        s_scr = refs[n_main + 2 * n_side + 2]
        sems = refs[n_main + 2 * n_side + 3:]
        step_t = pl.program_id(0)

        if side is not None:
            @pl.when(step_t == 0)
            def _():
                _comm_start(side.copies(side_in, side_out, sems))

        @pl.when(step_t == 0)
        def _():
            s_scr[...] = jnp.zeros_like(s_scr)

        def chunk(c, i):
            s = s_scr[...]
            y, s_next, *kept = chunk_fn([jnp.tile(r[...], (nb, 1, 1)) for r in prm_refs],
                                        [jnp.tile(r[...], (nb, 1, 1)) for r in cst_refs],
                                        [_load_chunk(r, i) for r in in_refs], s)
            for e, a in enumerate([s] + kept):
                ck_ref[c, e] = a
            for b in range(nb):
                y_ref[:, b, pl.ds(i, CH), :] = y[b * NH:(b + 1) * NH].astype(BF16)
            s_scr[...] = s_next

        def first_chunk_of(c):
            one, h = pl.ds(c, 1), pl.ds(c % NH, 1)
            s = s_scr[one]
            ck_ref[0, 0, one] = s
            for e in range(n_kept):
                ck_ref[0, 1 + e, one] = jnp.zeros((1, DH, DH), F32)
            y, s_next = first_fn([r[h] for r in prm_refs], [r[h] for r in cst_refs],
                                 [_load_chunk_of(r, c, 0) for r in in_refs], s)
            y_ref[h, c // NH, pl.ds(0, CH), :] = y.astype(BF16)
            s_scr[one] = s_next

        def step(c, carry):
            chunk(c, pl.multiple_of(c * CH, CH))
            return carry

        if first_fn is None:
            lax.fori_loop(0, ncb, step, 0)
        else:
            @pl.when(step_t == 0)
            def _():
                _each_chain(nch, first_chunk_of)

            @pl.when(step_t != 0)
            def _():
                chunk(0, 0)

            lax.fori_loop(1, ncb, step, 0)

        if side is not None:
            @pl.when(step_t == nt - 1)
            def _():
                _comm_wait(side.copies(side_in, side_out, sems))

    hbm = pl.BlockSpec(memory_space=pl.ANY)
    in_specs = [pl.BlockSpec((ng, nb, tb, DH), (lambda j, bi=bi: (bi, 0, j, 0))) for _, ng, bi in ins]
    in_specs += [pl.BlockSpec(p.shape, lambda j: (0, 0, 0)) for p in list(prm) + list(cst)]
    y, ck, *side_res = _pcall(
        body, name=name, grid=(nt,),
        in_specs=in_specs + [hbm] * n_side,
        out_specs=[pl.BlockSpec((NH, nb, tb, DH), lambda j: (0, 0, j, 0)),
                   pl.BlockSpec((ncb, 1 + n_kept, nch, DH, DH), lambda j: (j, 0, 0, 0, 0))] + [hbm] * n_side,
        out_shape=[jax.ShapeDtypeStruct((NH, nb, t, DH), BF16),
                   jax.ShapeDtypeStruct((t // CH, 1 + n_kept, nch, DH, DH), F32)]
        + (side.out_shapes if side else []),
        scratch_shapes=[pltpu.VMEM((nch, DH, DH), F32)] + (_comm_scratch(side) if side else []),
        compiler_params=_cparams(("arbitrary",)),
    )(*[a.reshape(a.shape[0], nb, t, DH) for a, _, _ in ins], *prm, *cst, *(side.operands if side else []))
    return y.reshape(NH, nb * t, DH), ck, side_res


def _mixer_bwd(chunk_fn, name, ins, prm, cst, ck, dy, dy_block, outs, routes, nb, t, first_fn=None, side=None):
    tb = _time_block(t)
    nt, ncb, nch = t // tb, tb // CH, nb * NH
    n_in, n_prm, n_cst, n_out = len(ins), len(prm), len(cst), len(outs)
    n_main, n_side = n_in + n_prm + n_cst + 2, len(side.operands) if side else 0
    n_kept = ck.shape[1] - 1

    def body(*refs):
        in_refs = refs[:n_in]
        prm_refs = refs[n_in:n_in + n_prm]
        cst_refs = refs[n_in + n_prm:n_in + n_prm + n_cst]
        ck_ref, dy_ref = refs[n_main - 2:n_main]
        side_in = refs[n_main:n_main + n_side]
        rest = refs[n_main + n_side:]
        out_refs = rest[:n_out]
        dprm_refs = rest[n_out:n_out + n_prm]
        side_out = rest[n_out + n_prm:n_out + n_prm + n_side]
        ds_scr = rest[n_out + n_prm + n_side]
        sems = rest[n_out + n_prm + n_side + 1:]
        step_t = pl.program_id(0)

        if side is not None:
            @pl.when(step_t == 0)
            def _():
                _comm_start(side.copies(side_in, side_out, sems))

        @pl.when(step_t == 0)
        def _():
            ds_scr[...] = jnp.zeros_like(ds_scr)
            for r in dprm_refs:
                r[...] = jnp.zeros_like(r)

        def chunk(c, i):
            cst_v = [jnp.tile(r[...], (nb, 1, 1)) for r in cst_refs]
            kept = [ck_ref[c, 1 + e] for e in range(n_kept)]
            _, vjp = jax.vjp(lambda p, x, s: chunk_fn(p, cst_v, x, s, *kept)[:2],
                             [jnp.tile(r[...], (nb, 1, 1)) for r in prm_refs],
                             [_load_chunk(r, i) for r in in_refs], ck_ref[c, 0])
            dy_c = jnp.concatenate([dy_ref[:, b, pl.ds(i, CH), :] for b in range(nb)], axis=0)
            d_prm, d_ins, d_s = vjp((dy_c, ds_scr[...]))
            for (oi, g0), r, g in zip(routes, in_refs, d_ins):
                o_ref = out_refs[oi]
                if r.shape[0] == NH:
                    for b in range(nb):
                        o_ref[g0:g0 + NH, b, pl.ds(i, CH), :] = g[b * NH:(b + 1) * NH].astype(o_ref.dtype)
                else:
                    o_ref[g0, :, pl.ds(i, CH), :] = g.astype(o_ref.dtype)
            for r, g in zip(dprm_refs, d_prm):
                r[...] += g
            ds_scr[...] = d_s

        def first_chunk_of(c):
            one, h, b = pl.ds(c, 1), pl.ds(c % NH, 1), c // NH
            cst_v = [r[h] for r in cst_refs]
            _, vjp = jax.vjp(lambda p, x, s: first_fn(p, cst_v, x, s), [r[h] for r in prm_refs],
                             [_load_chunk_of(r, c, 0) for r in in_refs], ck_ref[0, 0, one])
            d_prm, d_ins, d_s = vjp((dy_ref[h, b, pl.ds(0, CH), :], ds_scr[one]))
            for (oi, g0), r, g in zip(routes, in_refs, d_ins):
                o_ref = out_refs[oi]
                if r.shape[0] == NH:
                    o_ref.at[g0:g0 + NH][h, b, pl.ds(0, CH), :] = g.astype(o_ref.dtype)
                else:
                    o_ref[g0, pl.ds(b, 1), pl.ds(0, CH), :] += g.astype(o_ref.dtype)
            for r, g in zip(dprm_refs, d_prm):
                r[one] += g
            ds_scr[one] = d_s

        def first_chunk():
            for (oi, g0), r in zip(routes, in_refs):
                if r.shape[0] != NH:
                    out_refs[oi][g0, :, pl.ds(0, CH), :] = jnp.zeros((nb, CH, DH), out_refs[oi].dtype)
            _each_chain(nch, first_chunk_of)

        def step(j, carry):
            c = ncb - 1 - j
            chunk(c, pl.multiple_of(c * CH, CH))
            return carry

        lax.fori_loop(0, ncb - 1, step, 0)
        if first_fn is None:
            chunk(0, 0)
        else:
            @pl.when(step_t == nt - 1)
            def _():
                first_chunk()

            @pl.when(step_t != nt - 1)
            def _():
                chunk(0, 0)

        if side is not None:
            @pl.when(step_t == nt - 1)
            def _():
                _comm_wait(side.copies(side_in, side_out, sems))

    def back(j):
        return nt - 1 - j

    hbm = pl.BlockSpec(memory_space=pl.ANY)
    in_specs = [pl.BlockSpec((ng, nb, tb, DH), (lambda j, bi=bi: (bi, 0, back(j), 0))) for _, ng, bi in ins]
    in_specs += [pl.BlockSpec(p.shape, lambda j: (0, 0, 0)) for p in list(prm) + list(cst)]
    in_specs += [pl.BlockSpec((ncb, 1 + n_kept, nch, DH, DH), lambda j: (back(j), 0, 0, 0, 0)),
                 pl.BlockSpec((NH, nb, tb, DH), lambda j: (dy_block, 0, back(j), 0))]
    out_specs = [pl.BlockSpec((ng, nb, tb, DH), lambda j: (0, 0, back(j), 0)) for ng, _ in outs]
    out_specs += [pl.BlockSpec((nch,) + p.shape[1:], lambda j: (0, 0, 0)) for p in prm]
    out_shape = [jax.ShapeDtypeStruct((ng, nb, t, DH), dt) for ng, dt in outs]
    out_shape += [jax.ShapeDtypeStruct((nch,) + p.shape[1:], F32) for p in prm]
    res = _pcall(
        body, name=name, grid=(nt,),
        in_specs=in_specs + [hbm] * n_side, out_specs=out_specs + [hbm] * n_side,
        out_shape=out_shape + (side.out_shapes if side else []),
        scratch_shapes=[pltpu.VMEM((nch, DH, DH), F32)] + (_comm_scratch(side) if side else []),
        compiler_params=_cparams(("arbitrary",)),
    )(*[a.reshape(a.shape[0], nb, t, DH) for a, _, _ in ins], *prm, *cst, ck, dy.reshape(dy.shape[0], nb, t, DH),
      *(side.operands if side else []))
    d_outs = [o.reshape(o.shape[0], nb * t, DH) for o in res[:n_out]]
    d_prm = [g.reshape((nb,) + p.shape) for g, p in zip(res[n_out:n_out + n_prm], prm)]
    return d_outs, d_prm, res[n_out + n_prm:]


def _shift_down(x, s):
    if s == 0:
        return x
    row = lax.broadcasted_iota(jnp.int32, x.shape, 0)
    return jnp.where(row < s, 0.0, pltpu.roll(x, s, 0))


def _shift_up(x, s):
    if s == 0:
        return x
    t = x.shape[0]
    row = lax.broadcasted_iota(jnp.int32, x.shape, 0)
    return jnp.where(row >= t - s, 0.0, pltpu.roll(x, t - s, 0))


def _conv_fwd(p, g0, ng, w, nb, t, name):
    taps = w.shape[1]

    def body(x_ref, w_ref, y_ref):
        x = x_ref[...]
        acc = w_ref[taps - 1:taps, :] * x
        for i in range(taps - 1):
            acc = acc + w_ref[i:i + 1, :] * _shift_down(x, taps - 1 - i)
        y_ref[...] = acc

    return _pcall(
        body, name=name, grid=(ng, nb),
        in_specs=[pl.BlockSpec((None, t, DH), lambda g, b: (g0 + g, b, 0)),
                  pl.BlockSpec((None, taps, DH), lambda g, b: (g, 0, 0))],
        out_specs=pl.BlockSpec((None, t, DH), lambda g, b: (g, b, 0)),
        out_shape=jax.ShapeDtypeStruct((ng, nb * t, DH), F32),
        compiler_params=_cparams(("parallel", "parallel")),
    )(p, w)


def _conv_bwd(p, g0, ng, w, dy, nb, t, name):
    taps = w.shape[1]

    def body(x_ref, w_ref, dy_ref, dx_ref, dw_ref):
        x = x_ref[...]
        d = dy_ref[...]
        acc = w_ref[taps - 1:taps, :] * d
        dw_ref[taps - 1:taps, :] = jnp.sum(d * x, 0, keepdims=True)
        for i in range(taps - 1):
            s = taps - 1 - i
            acc = acc + w_ref[i:i + 1, :] * _shift_up(d, s)
            dw_ref[i:i + 1, :] = jnp.sum(d * _shift_down(x, s), 0, keepdims=True)
        dx_ref[...] = acc.astype(BF16)

    return _pcall(
        body, name=name, grid=(ng, nb),
        in_specs=[pl.BlockSpec((None, t, DH), lambda g, b: (g0 + g, b, 0)),
                  pl.BlockSpec((None, taps, DH), lambda g, b: (g, 0, 0)),
                  pl.BlockSpec((None, t, DH), lambda g, b: (g, b, 0))],
        out_specs=[pl.BlockSpec((None, t, DH), lambda g, b: (g, b, 0)),
                   pl.BlockSpec((None, None, taps, DH), lambda g, b: (g, b, 0, 0))],
        out_shape=[jax.ShapeDtypeStruct((ng, nb * t, DH), BF16),
                   jax.ShapeDtypeStruct((ng, nb, taps, DH), F32)],
        compiler_params=_cparams(("parallel", "parallel")),
    )(p, w, dy)


def _mix_group(g):
    return jnp.where(g < 16, G_RWKV + g, G_RWKV_WD + g - 16)


def _mix_fwd(p, mu, nb, t, name):
    def body(x_ref, mu_ref, y_ref):
        x = x_ref[...]
        y_ref[...] = x + mu_ref[...] * (_shift_down(x, 1) - x)

    return _pcall(
        body, name=name, grid=(18, nb),
        in_specs=[pl.BlockSpec((None, t, DH), lambda g, b: (_mix_group(g), b, 0)),
                  pl.BlockSpec((None, 1, DH), lambda g, b: (g, 0, 0))],
        out_specs=pl.BlockSpec((None, t, DH), lambda g, b: (g, b, 0)),
        out_shape=jax.ShapeDtypeStruct((18, nb * t, DH), F32),
        compiler_params=_cparams(("parallel", "parallel")),
    )(p, mu)


def _mix_bwd(p, mu, dy, nb, t, name):
    def body(x_ref, mu_ref, dy_ref, dx_ref, dmu_ref):
        x = x_ref[...]
        muv = mu_ref[...]
        d = dy_ref[...]
        dx_ref[...] = (d * (1.0 - muv) + _shift_up(d * muv, 1)).astype(BF16)
        dmu_ref[...] = jnp.sum(d * (_shift_down(x, 1) - x), 0, keepdims=True)

    return _pcall(
        body, name=name, grid=(18, nb),
        in_specs=[pl.BlockSpec((None, t, DH), lambda g, b: (_mix_group(g), b, 0)),
                  pl.BlockSpec((None, 1, DH), lambda g, b: (g, 0, 0)),
                  pl.BlockSpec((None, t, DH), lambda g, b: (g, b, 0))],
        out_specs=[pl.BlockSpec((None, t, DH), lambda g, b: (g, b, 0)),
                   pl.BlockSpec((None, None, 1, DH), lambda g, b: (g, b, 0, 0))],
        out_shape=[jax.ShapeDtypeStruct((18, nb * t, DH), BF16),
                   jax.ShapeDtypeStruct((18, nb, 1, DH), F32)],
        compiler_params=_cparams(("parallel", "parallel")),
    )(p, mu, dy)


def _sc_fwd(p, w, nb, t, name):
    def body(p_ref, w_ref, y_ref):
        u = p_ref[1] * p_ref[2]
        conv = w_ref[2:3, :] * u + w_ref[1:2, :] * _shift_down(u, 1) + w_ref[0:1, :] * _shift_down(u, 2)
        y_ref[...] = (p_ref[0] * conv * _silu(p_ref[3])).astype(BF16)

    return _pcall(
        body, name=name, grid=(NH, nb),
        in_specs=[pl.BlockSpec((4, t, DH), lambda j, b: (G_SC // 4 + j, b, 0)),
                  pl.BlockSpec((None, SC_TAPS, DH), lambda j, b: (j, 0, 0))],
        out_specs=pl.BlockSpec((None, t, DH), lambda j, b: (j, b, 0)),
        out_shape=jax.ShapeDtypeStruct((NH, nb * t, DH), BF16),
        compiler_params=_cparams(("parallel", "parallel")),
    )(p, w)


def _sc_bwd(p, w, dy, nb, t, name):
    def body(p_ref, w_ref, dy_ref, dp_ref, dw_ref):
        bg, cg, xg, z = p_ref[0], p_ref[1], p_ref[2], p_ref[3]
        d = dy_ref[...]
        u = cg * xg
        u1 = _shift_down(u, 1)
        u2 = _shift_down(u, 2)
        conv = w_ref[2:3, :] * u + w_ref[1:2, :] * u1 + w_ref[0:1, :] * u2
        sg = jax.nn.sigmoid(z)
        sz = z * sg
        dp_ref[0] = (d * conv * sz).astype(BF16)
        dp_ref[3] = (d * bg * conv * (sg * (1.0 + z * (1.0 - sg)))).astype(BF16)
        dconv = d * bg * sz
        du = w_ref[2:3, :] * dconv + w_ref[1:2, :] * _shift_up(dconv, 1) + w_ref[0:1, :] * _shift_up(dconv, 2)
        dp_ref[1] = (du * xg).astype(BF16)
        dp_ref[2] = (du * cg).astype(BF16)
        dw_ref[2:3, :] = jnp.sum(dconv * u, 0, keepdims=True)
        dw_ref[1:2, :] = jnp.sum(dconv * u1, 0, keepdims=True)
        dw_ref[0:1, :] = jnp.sum(dconv * u2, 0, keepdims=True)

    return _pcall(
        body, name=name, grid=(NH, nb),
        in_specs=[pl.BlockSpec((4, t, DH), lambda j, b: (G_SC // 4 + j, b, 0)),
                  pl.BlockSpec((None, SC_TAPS, DH), lambda j, b: (j, 0, 0)),
                  pl.BlockSpec((None, t, DH), lambda j, b: (8 + j, b, 0))],
        out_specs=[pl.BlockSpec((4, t, DH), lambda j, b: (j, b, 0)),
                   pl.BlockSpec((None, None, SC_TAPS, DH), lambda j, b: (j, b, 0, 0))],
        out_shape=[jax.ShapeDtypeStruct((4 * NH, nb * t, DH), BF16),
                   jax.ShapeDtypeStruct((NH, nb, SC_TAPS, DH), F32)],
        compiler_params=_cparams(("parallel", "parallel")),
    )(p, w, dy)


def _row_tile(n):
    return 1024 if n % 1024 == 0 else n


def _regroup_in(w_all, name):
    tr = 256
    gs = GROUPS_PER_STEP

    def body(w_ref, o_ref):
        for g in _PADDED_GROUPS:
            o_ref[g // gs, :, DH * (g % gs):DH * (g % gs + 1)] = jnp.zeros((tr, DH), BF16)
        for g, a, d, off, ln in _SEGMENTS:
            lane = DH * (g % gs) + a
            o_ref[g // gs, :, lane:lane + ln] = w_ref[d, :, off:off + ln].astype(BF16)

    return _pcall(
        body, name=name, grid=(D_MODEL // tr,),
        in_specs=[pl.BlockSpec((N_DEV, tr, SHARD_COLS), lambda i: (0, i, 0))],
        out_specs=pl.BlockSpec((N_GROUPS // gs, tr, gs * DH), lambda i: (0, i, 0)),
        out_shape=jax.ShapeDtypeStruct((N_GROUPS // gs, D_MODEL, gs * DH), BF16),
        compiler_params=_cparams(("parallel",)),
    )(w_all)


def _regroup_out(dwg, name):
    tr = 256
    gs = GROUPS_PER_STEP

    def body(g_ref, o_ref):
        for g, a, d, off, ln in _SEGMENTS:
            lane = DH * (g % gs) + a
            o_ref[d, :, off:off + ln] = g_ref[g // gs, :, lane:lane + ln].astype(BF16)

    return _pcall(
        body, name=name, grid=(D_MODEL // tr,),
        in_specs=[pl.BlockSpec((N_GROUPS // gs, tr, gs * DH), lambda i: (0, i, 0))],
        out_specs=pl.BlockSpec((N_DEV, tr, SHARD_COLS), lambda i: (0, i, 0)),
        out_shape=jax.ShapeDtypeStruct((N_DEV, D_MODEL, SHARD_COLS), BF16),
        compiler_params=_cparams(("parallel",)),
    )(dwg)


def _norm_proj(x, pre_w, w_g, name):
    n = x.shape[0]
    tm = _row_tile(n)
    gs = GROUPS_PER_STEP

    def body(x_ref, pw_ref, w_ref, h_ref, p_ref):
        @pl.when(pl.program_id(1) == 0)
        def _():
            xv = x_ref[...]
            h = xv * lax.rsqrt(jnp.mean(xv * xv, -1, keepdims=True) + EPS) * pw_ref[...]
            h_ref[...] = h.astype(BF16)

        r = jnp.dot(h_ref[...], w_ref[...], preferred_element_type=F32)
        for k in range(gs):
            p_ref[k] = r[:, DH * k:DH * (k + 1)]

    return _pcall(
        body, name=name, grid=(n // tm, N_GROUPS // gs),
        in_specs=[pl.BlockSpec((tm, D_MODEL), lambda i, j: (i, 0)),
                  pl.BlockSpec((1, D_MODEL), lambda i, j: (0, 0)),
                  pl.BlockSpec((None, D_MODEL, gs * DH), lambda i, j: (j, 0, 0))],
        out_specs=[pl.BlockSpec((tm, D_MODEL), lambda i, j: (i, 0)),
                   pl.BlockSpec((gs, tm, DH), lambda i, j: (j, i, 0))],
        out_shape=[jax.ShapeDtypeStruct((n, D_MODEL), BF16),
                   jax.ShapeDtypeStruct((N_GROUPS, n, DH), F32)],
        compiler_params=_cparams(("parallel", "arbitrary")),
    )(x, pre_w, w_g)


def _out_proj_norm(ys, wout_g, x, post_w, name):
    n = x.shape[0]
    tm = _row_tile(n)

    def body(y0, y1, y2, y3, w_ref, x_ref, pw_ref, out_ref, xn_ref):
        acc = jnp.zeros((tm, D_MODEL), F32)
        for m, yr in enumerate((y0, y1, y2, y3)):
            for h in range(NH):
                acc = acc + jnp.dot(yr[h], w_ref[m * NH + h], preferred_element_type=F32)
        out_ref[...] = acc
        xn_ref[...] = x_ref[...] + acc * lax.rsqrt(jnp.mean(acc * acc, -1, keepdims=True) + EPS) * pw_ref[...]

    yspec = pl.BlockSpec((NH, tm, DH), lambda i: (0, i, 0))
    rows = pl.BlockSpec((tm, D_MODEL), lambda i: (i, 0))
    return _pcall(
        body, name=name, grid=(n // tm,),
        in_specs=[yspec] * 4 + [pl.BlockSpec((4 * NH, DH, D_MODEL), lambda i: (0, 0, 0)), rows,
                                pl.BlockSpec((1, D_MODEL), lambda i: (0, 0))],
        out_specs=[rows, rows],
        out_shape=[jax.ShapeDtypeStruct((n, D_MODEL), F32)] * 2,
        compiler_params=_cparams(("parallel",)),
    )(*ys, wout_g, x, post_w)


def _loss_grad(x, tgt, name):
    n = x.shape[0]
    tm = _row_tile(n)

    def body(x_ref, t_ref, dx_ref, l_ref):
        @pl.when(pl.program_id(0) == 0)
        def _():
            l_ref[...] = jnp.zeros_like(l_ref)

        e = x_ref[...] - t_ref[...]
        dx_ref[...] = e * (1.0 / D_MODEL)
        l_ref[...] += jnp.sum(jnp.sum(e * e, -1, keepdims=True), 0, keepdims=True) * (0.5 / D_MODEL)

    rows = pl.BlockSpec((tm, D_MODEL), lambda i: (i, 0))
    return _pcall(
        body, name=name, grid=(n // tm,),
        in_specs=[rows, rows],
        out_specs=[rows, pl.BlockSpec((1, 128), lambda i: (0, 0))],
        out_shape=[jax.ShapeDtypeStruct((n, D_MODEL), F32), jax.ShapeDtypeStruct((1, 128), F32)],
        compiler_params=_cparams(("arbitrary",)),
    )(x, tgt)


def _rmsnorm_bwd(xv, w, d):
    r = lax.rsqrt(jnp.mean(xv * xv, -1, keepdims=True) + EPS)
    xh = xv * r
    dxh = d * w
    dx = r * (dxh - xh * jnp.mean(dxh * xh, -1, keepdims=True))
    return dx, d * xh


def _post_bwd(dxn, out, post_w, wout_g, name):
    n = dxn.shape[0]
    tm = _row_tile(n)

    def body(d_ref, o_ref, pw_ref, w_ref, do_ref, dy_ref, dpw_ref):
        @pl.when(pl.program_id(0) == 0)
        def _():
            dpw_ref[...] = jnp.zeros_like(dpw_ref)

        dout, dw_rows = _rmsnorm_bwd(o_ref[...], pw_ref[...], d_ref[...])
        dpw_ref[...] += jnp.sum(dw_rows, 0, keepdims=True)
        db = dout.astype(BF16)
        do_ref[...] = db
        for g in range(4 * NH):
            dy_ref[g] = lax.dot_general(db, w_ref[g], (((1,), (1,)), ((), ())), preferred_element_type=F32)

    rows = pl.BlockSpec((tm, D_MODEL), lambda i: (i, 0))
    vec = pl.BlockSpec((1, D_MODEL), lambda i: (0, 0))
    return _pcall(
        body, name=name, grid=(n // tm,),
        in_specs=[rows, rows, vec, pl.BlockSpec((4 * NH, DH, D_MODEL), lambda i: (0, 0, 0))],
        out_specs=[rows, pl.BlockSpec((4 * NH, tm, DH), lambda i: (0, i, 0)), vec],
        out_shape=[jax.ShapeDtypeStruct((n, D_MODEL), BF16),
                   jax.ShapeDtypeStruct((4 * NH, n, DH), F32),
                   jax.ShapeDtypeStruct((1, D_MODEL), F32)],
        compiler_params=_cparams(("arbitrary",)),
    )(dxn, out, post_w, wout_g)


def _dwout(ys, dout, name):
    n = dout.shape[0]
    tm = _row_tile(n)

    def body(y0, y1, y2, y3, d_ref, dw_ref):
        @pl.when(pl.program_id(0) == 0)
        def _():
            dw_ref[...] = jnp.zeros_like(dw_ref)

        d = d_ref[...]
        for m, yr in enumerate((y0, y1, y2, y3)):
            for h in range(NH):
                dw_ref[m * NH + h] += lax.dot_general(yr[h], d, (((0,), (0,)), ((), ())),
                                                      preferred_element_type=F32)

    yspec = pl.BlockSpec((NH, tm, DH), lambda i: (0, i, 0))
    return _pcall(
        body, name=name, grid=(n // tm,),
        in_specs=[yspec] * 4 + [pl.BlockSpec((tm, D_MODEL), lambda i: (i, 0))],
        out_specs=pl.BlockSpec((4 * NH, DH, D_MODEL), lambda i: (0, 0, 0)),
        out_shape=jax.ShapeDtypeStruct((4 * NH, DH, D_MODEL), F32),
        compiler_params=_cparams(("arbitrary",)),
    )(*ys, dout)


def _source_specs(sources, rows_first):
    gs = GROUPS_PER_STEP
    spans, specs, j0 = [], [], 0
    for a in sources:
        nblk = a.shape[0] // gs
        spans.append((j0, j0 + nblk))
        shape = (gs, _row_tile(a.shape[1]), DH)

        def blk(j, j0=j0, nblk=nblk):
            return jnp.clip(j - j0, 0, nblk - 1)

        if rows_first:
            specs.append(pl.BlockSpec(shape, (lambda i, j, blk=blk: (blk(j), i, 0))))
        else:
            specs.append(pl.BlockSpec(shape, (lambda j, i, blk=blk: (blk(j), i, 0))))
        j0 += nblk
    return spans, specs


def _dh_prenorm_bwd(sources, w_g, x, pre_w, dxn, name, side=None):
    n = x.shape[0]
    tm = _row_tile(n)
    gs = GROUPS_PER_STEP
    nj = N_GROUPS // gs
    ni = n // tm
    spans, src_specs = _source_specs(sources, True)
    ns = len(sources)
    n_side = len(side.operands) if side else 0

    def body(*refs):
        src = refs[:ns]
        w_ref, x_ref, pw_ref, d_ref = refs[ns:ns + 4]
        side_in = refs[ns + 4:ns + 4 + n_side]
        dx_ref, dpw_ref = refs[ns + 4 + n_side:ns + 6 + n_side]
        side_out = refs[ns + 6 + n_side:ns + 6 + 2 * n_side]
        acc = refs[ns + 6 + 2 * n_side]
        sems = refs[ns + 7 + 2 * n_side:]
        i, j = pl.program_id(0), pl.program_id(1)

        if side is not None:
            @pl.when((i == 0) & (j == 0))
            def _():
                _comm_start(side.copies(side_in, side_out, sems))

        @pl.when((i == 0) & (j == 0))
        def _():
            dpw_ref[...] = jnp.zeros_like(dpw_ref)

        @pl.when(j == 0)
        def _():
            acc[...] = jnp.zeros_like(acc)

        for s_ref, (lo, hi) in zip(src, spans):
            @pl.when((j >= lo) & (j < hi))
            def _(s_ref=s_ref):
                four = jnp.concatenate([s_ref[k] for k in range(gs)], axis=-1)
                acc[...] += lax.dot_general(four, w_ref[...], (((1,), (1,)), ((), ())), preferred_element_type=F32)

        @pl.when(j == nj - 1)
        def _():
            dx, dw_rows = _rmsnorm_bwd(x_ref[...], pw_ref[...], acc[...])
            dx_ref[...] = d_ref[...] + dx
            dpw_ref[...] += jnp.sum(dw_rows, 0, keepdims=True)

        if side is not None:
            @pl.when((i == ni - 1) & (j == nj - 1))
            def _():
                _comm_wait(side.copies(side_in, side_out, sems))

    hbm = pl.BlockSpec(memory_space=pl.ANY)
    rows = pl.BlockSpec((tm, D_MODEL), lambda i, j: (i, 0))
    vec = pl.BlockSpec((1, D_MODEL), lambda i, j: (0, 0))
    dx, dpw, *side_res = _pcall(
        body, name=name, grid=(ni, nj),
        in_specs=src_specs + [pl.BlockSpec((None, D_MODEL, gs * DH), lambda i, j: (j, 0, 0)), rows, vec, rows]
        + [hbm] * n_side,
        out_specs=[rows, vec] + [hbm] * n_side,
        out_shape=[jax.ShapeDtypeStruct((n, D_MODEL), F32), jax.ShapeDtypeStruct((1, D_MODEL), F32)]
        + (side.out_shapes if side else []),
        scratch_shapes=[pltpu.VMEM((tm, D_MODEL), F32)] + (_comm_scratch(side) if side else []),
        compiler_params=_cparams(("arbitrary", "arbitrary")),
    )(*sources, w_g, x, pre_w, dxn, *(side.operands if side else []))
    return dx, dpw, side_res


def _dwin(hb, sources, name):
    n = hb.shape[0]
    tm = _row_tile(n)
    gs = GROUPS_PER_STEP
    ni, nj = n // tm, N_GROUPS // gs
    spans, src_specs = _source_specs(sources, True)
    ns = len(sources)

    def body(*refs):
        h_ref = refs[0]
        src = refs[1:1 + ns]
        out_ref, acc, sem = refs[1 + ns:]
        i, j = pl.program_id(0), pl.program_id(1)

        @pl.when((i == 0) & (j == 0))
        def _():
            acc[...] = jnp.zeros_like(acc)

        h = h_ref[...]
        for s_ref, (lo, hi) in zip(src, spans):
            @pl.when((j >= lo) & (j < hi))
            def _(s_ref=s_ref):
                four = jnp.concatenate([s_ref[k] for k in range(gs)], axis=-1)
                acc[j] += jnp.dot(h, four, preferred_element_type=F32)

        @pl.when((i == ni - 1) & (j == nj - 1))
        def _():
            done = pltpu.make_async_copy(acc, out_ref, sem)
            done.start()
            done.wait()

    return _pcall(
        body, name=name, grid=(ni, nj),
        in_specs=[pl.BlockSpec((D_MODEL, tm), lambda i, j: (0, i))] + src_specs,
        out_specs=pl.BlockSpec(memory_space=pl.ANY),
        out_shape=jax.ShapeDtypeStruct((nj, D_MODEL, gs * DH), F32),
        scratch_shapes=[pltpu.VMEM((nj, D_MODEL, gs * DH), F32), pltpu.SemaphoreType.DMA],
        compiler_params=_cparams(("arbitrary", "arbitrary")),
    )(jnp.transpose(hb), *sources)


def _adamw_math(w, g, m, v):
    c1 = 1.0 - ADAM_B1 ** ADAM_STEP
    c2 = 1.0 - ADAM_B2 ** ADAM_STEP
    nm = ADAM_B1 * m + (1.0 - ADAM_B1) * g
    nv = ADAM_B2 * v + (1.0 - ADAM_B2) * (g * g)
    return -ADAM_LR * ((nm / c1) / (jnp.sqrt(nv / c2) + ADAM_EPS) + ADAM_WD * w), nm, nv


def _adamw(w, g, m, v, name):
    r, c = w.shape
    tr = 256 if r % 256 == 0 else r

    def body(w_ref, g_ref, m_ref, v_ref, d_ref, nm_ref, nv_ref):
        d_ref[...], nm_ref[...], nv_ref[...] = _adamw_math(w_ref[...], g_ref[...], m_ref[...], v_ref[...])

    spec = pl.BlockSpec((tr, c), lambda i: (i, 0))
    return _pcall(
        body, name=name, grid=(r // tr,),
        in_specs=[spec] * 4, out_specs=[spec] * 3,
        out_shape=[jax.ShapeDtypeStruct((r, c), F32)] * 3,
        compiler_params=_cparams(("parallel",)),
    )(w, g, m, v)


def _sum_adamw(parts, w, m, v, name):
    r, c = w.shape
    tr = 128 if r % 128 == 0 else r

    def body(p_ref, w_ref, m_ref, v_ref, g_ref, d_ref, nm_ref, nv_ref):
        g = p_ref[0].astype(F32)
        for k in range(1, N_DEV):
            g = g + p_ref[k].astype(F32)
        g_ref[...] = g
        d_ref[...], nm_ref[...], nv_ref[...] = _adamw_math(w_ref[...], g, m_ref[...], v_ref[...])

    spec = pl.BlockSpec((tr, c), lambda i: (i, 0))
    return _pcall(
        body, name=name, grid=(r // tr,),
        in_specs=[pl.BlockSpec((N_DEV, tr, c), lambda i: (0, i, 0))] + [spec] * 3, out_specs=[spec] * 4,
        out_shape=[jax.ShapeDtypeStruct((r, c), F32)] * 4,
        compiler_params=_cparams(("parallel",)),
    )(parts, w, m, v)


def _me():
    return lax.axis_index("x"), lax.axis_index("y"), lax.axis_index("c")


def _flat(x, y, c):
    return 4 * x + 2 * y + c


def _peer(k):
    x, y, c = _me()
    return (x ^ ((k >> 2) & 1), y ^ ((k >> 1) & 1), c ^ (k & 1))


def _gather_plan(blocks):
    def copies(x_refs, out_refs, sems):
        send_sems, recv_sems, local_sems = sems
        me = _flat(*_me())
        local = [pltpu.make_async_copy(x, o.at[me], local_sems.at[a]) for a, (x, o) in enumerate(zip(x_refs, out_refs))]
        outgoing, incoming = [], []
        for k in range(1, N_DEV):
            src = _flat(*_peer(k))
            for a, (x, o) in enumerate(zip(x_refs, out_refs)):
                for slot, group in ((me, outgoing), (src, incoming)):
                    group.append(pltpu.make_async_remote_copy(
                        src_ref=x, dst_ref=o.at[slot], send_sem=send_sems.at[a, k - 1], recv_sem=recv_sems.at[a, k - 1],
                        device_id=_peer(k), device_id_type=MESH))
        return local, outgoing, incoming

    return _Comm(list(blocks), [jax.ShapeDtypeStruct((N_DEV,) + b.shape, b.dtype) for b in blocks], copies)


def _exchange_plan(sends):
    def copies(s_refs, out_refs, sems):
        send_sems, recv_sems, local_sems = sems
        me = _flat(*_me())
        local = [pltpu.make_async_copy(s.at[me], o.at[0], local_sems.at[i]) for i, (s, o) in enumerate(zip(s_refs, out_refs))]
        outgoing = []
        for k in range(1, N_DEV):
            to = _flat(*_peer(k))
            for i, (s, o) in enumerate(zip(s_refs, out_refs)):
                outgoing.append(pltpu.make_async_remote_copy(
                    src_ref=s.at[to], dst_ref=o.at[k], send_sem=send_sems.at[i, k - 1], recv_sem=recv_sems.at[i, k - 1],
                    device_id=_peer(k), device_id_type=MESH))
        return local, outgoing, outgoing

    return _Comm(list(sends), [jax.ShapeDtypeStruct(s.shape, s.dtype) for s in sends], copies)


def _comm_scratch(plan):
    n = len(plan.operands)
    return [pltpu.SemaphoreType.DMA((n, N_DEV - 1)), pltpu.SemaphoreType.DMA((n, N_DEV - 1)),
            pltpu.SemaphoreType.DMA((n,))]


def _comm_start(copies):
    local, outgoing, _ = copies
    for cp in local + outgoing:
        cp.start()


def _comm_wait(copies):
    local, outgoing, incoming = copies
    for cp in incoming:
        cp.wait_recv()
    for cp in outgoing:
        cp.wait_send()
    for cp in local:
        cp.wait()


def _run_comm(plan, name):
    n = len(plan.operands)

    def body(*refs):
        copies = plan.copies(refs[:n], refs[n:2 * n], refs[2 * n:])
        _comm_start(copies)
        _comm_wait(copies)

    return _pcall(
        body, name=name,
        in_specs=[pl.BlockSpec(memory_space=pl.ANY)] * n,
        out_specs=[pl.BlockSpec(memory_space=pl.ANY)] * n,
        out_shape=plan.out_shapes,
        scratch_shapes=_comm_scratch(plan),
    )(*plan.operands)


def _all_gather_two_level(blocks, name):
    na = len(blocks)

    def body(*refs):
        x_refs, out_refs = refs[:na], refs[na:2 * na]
        send_sems, recv_sems, local_sems = refs[2 * na:]
        x, y, c = _me()
        me, sibling = (x, y, c), (x, y, 1 - c)
        chips = [(1 - x, y), (x, 1 - y), (1 - x, 1 - y)]

        def copy(a, k, block, to, own=False):
            slot = out_refs[a].at[_flat(*block)]
            return pltpu.make_async_remote_copy(
                src_ref=x_refs[a] if own else slot, dst_ref=slot, send_sem=send_sems.at[a, k],
                recv_sem=recv_sems.at[a, k], device_id=to, device_id_type=MESH)

        mine = [pltpu.make_async_copy(x_refs[a], out_refs[a].at[_flat(*me)], local_sems.at[a]) for a in range(na)]
        first = [copy(a, 0, me, sibling, own=True) for a in range(na)]
        first += [copy(a, 1 + j, me, (*chip, c), own=True) for j, chip in enumerate(chips) for a in range(na)]
        for cp in mine + first:
            cp.start()
        passed = []
        for j, chip in enumerate(chips):
            for a in range(na):
                copy(a, 1 + j, (*chip, c), me).wait_recv()
                cp = copy(a, 4 + j, (*chip, c), sibling)
                cp.start()
                passed.append(cp)
        for a in range(na):
            copy(a, 0, sibling, me).wait_recv()
        for j, chip in enumerate(chips):
            for a in range(na):
                copy(a, 4 + j, (*chip, 1 - c), me).wait_recv()
        for cp in first + passed:
            cp.wait_send()
        for cp in mine:
            cp.wait()

    return _pcall(
        body, name=name,
        in_specs=[pl.BlockSpec(memory_space=pl.ANY)] * na,
        out_specs=[pl.BlockSpec(memory_space=pl.ANY)] * na,
        out_shape=[jax.ShapeDtypeStruct((N_DEV,) + b.shape, b.dtype) for b in blocks],
        scratch_shapes=[pltpu.SemaphoreType.DMA((na, N_DEV - 1)), pltpu.SemaphoreType.DMA((na, N_DEV - 1)),
                        pltpu.SemaphoreType.DMA((na,))],
    )(*blocks)


def _sum_slots(a, name):
    r = a.shape[1]

    def body(a_ref, o_ref):
        acc = a_ref[0]
        for d in range(1, N_DEV):
            acc = acc + a_ref[d]
        o_ref[...] = acc

    return _pcall(body, name=name, out_shape=jax.ShapeDtypeStruct((r, 128), F32), compiler_params=_cparams())(a)


def _all_reduce_small(blk, name):
    r = blk.shape[0]

    def body(x_ref, out_ref, gath, send_sems, recv_sems):
        me = _flat(*_me())
        gath[me] = x_ref[...]
        copies = []
        for k in range(1, N_DEV):
            cp = pltpu.make_async_remote_copy(
                src_ref=x_ref, dst_ref=gath.at[me],
                send_sem=send_sems.at[k - 1], recv_sem=recv_sems.at[k - 1],
                device_id=_peer(k), device_id_type=MESH)
            cp.start()
            copies.append(cp)
        for k in range(1, N_DEV):
            src = _flat(*_peer(k))
            pltpu.make_async_remote_copy(
                src_ref=x_ref, dst_ref=gath.at[src],
                send_sem=send_sems.at[k - 1], recv_sem=recv_sems.at[k - 1],
                device_id=_peer(k), device_id_type=MESH).wait_recv()
        for cp in copies:
            cp.wait_send()
        acc = gath[0]
        for d in range(1, N_DEV):
            acc = acc + gath[d]
        out_ref[...] = acc

    return _pcall(
        body, name=name,
        in_specs=[pl.BlockSpec(memory_space=pltpu.VMEM)],
        out_specs=pl.BlockSpec(memory_space=pltpu.VMEM),
        out_shape=jax.ShapeDtypeStruct((r, 128), F32),
        scratch_shapes=[pltpu.VMEM((N_DEV, r, 128), F32),
                        pltpu.SemaphoreType.DMA((N_DEV - 1,)), pltpu.SemaphoreType.DMA((N_DEV - 1,))],
    )(blk)


def _heads(vec):
    return vec.reshape(NH, 1, DH)


def _rep(vec4):
    return jnp.broadcast_to(vec4.reshape(NH, 1, 1), (NH, 1, DH))


def _onehot_lane(offset):
    m = np.zeros((NH, 1, DH), np.float32)
    for h in range(NH):
        m[h, 0, offset + h] = 1.0
    return jnp.asarray(m)


_TINY = (("gdn_conv_w", (DEPTH, 4, 96)), ("rwkv_w_up", (DEPTH, 64, 32)), ("rwkv_a_up", (DEPTH, 64, 32)),
         ("sc_conv_w", (DEPTH, 3, 32)))
_TINY_ROWS = -(-sum(int(np.prod(s)) for _, s in _TINY) // 1024) * 8


def _pack_rows(arrays, rows, fill=0.0):
    flat = jnp.concatenate([a.reshape(-1) for a in arrays])
    return jnp.pad(flat, (0, rows * 128 - flat.shape[0]), constant_values=fill).reshape(rows, 128)


def _unpack_rows(p, named_shapes):
    lead = p.shape[:-2]
    flat = p.reshape(lead + (-1,))
    out, o = {}, 0
    for n, s in named_shapes:
        size = int(np.prod(s))
        out[n] = flat[..., o:o + size].reshape(lead + tuple(s))
        o += size
    return out


def _gather_last(a):
    return jnp.transpose(a, (1, 0, 2)).reshape(a.shape[1], -1)


def _split_last(a):
    r, c8 = a.shape
    return jnp.transpose(a.reshape(r, N_DEV, c8 // N_DEV), (1, 0, 2))


_SMALL = (("pre_norm_w", (DEPTH, 1024)), ("gdn_a_log", (DEPTH, 4)), ("gdn_dt_bias", (DEPTH, 4)),
          ("gdn_norm_w", (DEPTH, 64)), ("rwkv_mu", (DEPTH, 1152)), ("rwkv_w0", (DEPTH, 256)),
          ("rwkv_a0", (DEPTH, 256)), ("rwkv_k_k", (DEPTH, 256)), ("rwkv_k_a", (DEPTH, 256)),
          ("rwkv_r_k", (DEPTH, 256)), ("rwkv_ln_w", (DEPTH, 256)), ("rwkv_ln_b", (DEPTH, 256)),
          ("gla_a_up", (DEPTH, 16, 128)), ("gla_a_bias", (DEPTH, 128)), ("gla_norm_w", (DEPTH, 64)),
          ("post_norm_w", (DEPTH, 1024)), ("loss", ()))
_SMALL_ROWS = -(-sum(int(np.prod(s)) for _, s in _SMALL) // 1024) * 8


def _big_weights(w_in_all, w_out_all, l):
    return dict(w_g=_regroup_in(w_in_all, f"regroup_in{l}"),
                wout_g=w_out_all.reshape(4 * NH, DH, D_MODEL).astype(BF16))


def _layer_params(wts, tiny, l):
    conv = _gather_last(tiny["gdn_conv_w"][:, l])
    q = {}
    q["gdn_conv"] = jnp.transpose(conv.reshape(GDN_TAPS, 12, DH), (1, 0, 2))
    q["gdn_prm"] = [_rep(wts["gdn_a_log"][l]), _rep(wts["gdn_dt_bias"][l]),
                    jnp.broadcast_to(wts["gdn_norm_w"][l].reshape(1, 1, DH), (NH, 1, DH))]
    q["gdn_cst"] = [_onehot_lane(0), _onehot_lane(NH)]
    q["rwkv_mu"] = wts["rwkv_mu"][l].reshape(18, 1, DH)
    w_up = jnp.transpose(_gather_last(tiny["rwkv_w_up"][:, l]).reshape(64, NH, DH), (1, 0, 2))
    a_up = jnp.transpose(_gather_last(tiny["rwkv_a_up"][:, l]).reshape(64, NH, DH), (1, 0, 2))
    q["rwkv_prm"] = [_heads(wts["rwkv_w0"][l]), w_up, _heads(wts["rwkv_a0"][l]), a_up,
                     _heads(wts["rwkv_k_k"][l]), _heads(wts["rwkv_k_a"][l]), _heads(wts["rwkv_r_k"][l]),
                     _heads(wts["rwkv_ln_w"][l]), _heads(wts["rwkv_ln_b"][l])]
    sc = _gather_last(tiny["sc_conv_w"][:, l])
    q["sc_conv"] = jnp.transpose(sc.reshape(SC_TAPS, NH, DH), (1, 0, 2))
    gla_up = jnp.transpose(wts["gla_a_up"][l].reshape(16, NH, GLA_HEAD_K), (1, 0, 2))
    gla_up = jnp.pad(gla_up, ((0, 0), (0, DH - 16), (0, DH - GLA_HEAD_K)))
    gla_b = jnp.pad(wts["gla_a_bias"][l].reshape(NH, 1, GLA_HEAD_K), ((0, 0), (0, 0), (0, DH - GLA_HEAD_K)))
    q["gla_prm"] = [gla_up, gla_b, jnp.broadcast_to(wts["gla_norm_w"][l].reshape(1, 1, DH), (NH, 1, DH))]
    q["pre_w"] = wts["pre_norm_w"][l].reshape(1, D_MODEL)
    q["post_w"] = wts["post_norm_w"][l].reshape(1, D_MODEL)
    return q


def _mixer_inputs(p, cq, pm):
    gdn = [(cq, 4, 0), (cq, 4, 1), (cq, 4, 2), (p, 4, G_GDN // 4 + 3), (p, 1, G_GDN_AB)]
    rwkv = [(pm, 4, 0), (pm, 4, 1), (pm, 4, 2), (pm, 4, 3), (pm, 1, 16), (pm, 1, 17)]
    gla = [(p, 4, G_GLA // 4 + k) for k in range(4)] + [(p, 1, G_GLA_AD)]
    return gdn, rwkv, gla


def _layer_fwd(x, q, nb, t, l, side=None):
    hb, p = _norm_proj(x, q["pre_w"], q["w_g"], f"norm_proj{l}")
    cq = _conv_fwd(p, G_GDN, 12, q["gdn_conv"], nb, t, f"gdn_conv{l}")
    pm = _mix_fwd(p, q["rwkv_mu"], nb, t, f"rwkv_mix{l}")
    gdn_in, rwkv_in, gla_in = _mixer_inputs(p, cq, pm)
    y_gdn, ck_gdn, _ = _mixer_fwd(_gdn_chunk, f"gdn_fwd{l}", gdn_in, q["gdn_prm"], q["gdn_cst"], nb, t, n_kept=1)
    y_rwkv, ck_rwkv, side_res = _mixer_fwd(_rwkv_chunk, f"rwkv_fwd{l}", rwkv_in, q["rwkv_prm"], [], nb, t,
                                           first_fn=_rwkv_first_chunk, side=side, n_kept=1)
    y_sc = _sc_fwd(p, q["sc_conv"], nb, t, f"sc_fwd{l}")
    y_gla, ck_gla, _ = _mixer_fwd(_gla_chunk, f"gla_fwd{l}", gla_in, q["gla_prm"], [], nb, t)
    ys = (y_gdn, y_rwkv, y_sc, y_gla)
    out, xn = _out_proj_norm(ys, q["wout_g"], x, q["post_w"], f"out_proj{l}")
    saved = dict(x=x, hb=hb, p=p, cq=cq, pm=pm, ys=ys, out=out, ck=(ck_gdn, ck_rwkv, ck_gla))
    return xn, saved, side_res


def _layer_bwd(dxn, q, sv, nb, t, l, side=None, exchange_own=False):
    p, cq, pm, ys = sv["p"], sv["cq"], sv["pm"], sv["ys"]
    dout, dy, d_post = _post_bwd(dxn, sv["out"], q["post_w"], q["wout_g"], f"post_bwd{l}")
    d_wout = _dwout(ys, dout, f"dwout{l}").reshape(N_DEV, 128, D_MODEL).astype(BF16)
    gdn_in, rwkv_in, gla_in = _mixer_inputs(p, cq, pm)
    ck_gdn, ck_rwkv, ck_gla = sv["ck"]
    g = {}

    (d_conv, dz, dab), (da_log, ddt, dnw), _ = _mixer_bwd(
        _gdn_chunk, f"gdn_bwd{l}", gdn_in, q["gdn_prm"], q["gdn_cst"], ck_gdn, dy, 0,
        [(12, F32), (4, BF16), (1, BF16)], [(0, 0), (0, 4), (0, 8), (1, 0), (2, 0)], nb, t)
    dconv_in, d_gconv = _conv_bwd(p, G_GDN, 12, q["gdn_conv"], d_conv, nb, t, f"gdn_conv_bwd{l}")
    g["gdn_conv_w"] = jnp.transpose(d_gconv.sum(1), (1, 0, 2)).reshape(GDN_TAPS, 768)
    g["gdn_a_log"] = da_log.sum((0, 2, 3))
    g["gdn_dt_bias"] = ddt.sum((0, 2, 3))
    g["gdn_norm_w"] = dnw.sum((0, 1, 2))

    (d_pm,), d_rprm, side_res = _mixer_bwd(
        _rwkv_chunk, f"rwkv_bwd{l}", rwkv_in, q["rwkv_prm"], [], ck_rwkv, dy, 1,
        [(18, F32)], [(0, 0), (0, 4), (0, 8), (0, 12), (0, 16), (0, 17)], nb, t, first_fn=_rwkv_first_chunk,
        side=side)
    dp_rwkv, d_mu = _mix_bwd(p, q["rwkv_mu"], d_pm, nb, t, f"rwkv_mix_bwd{l}")
    g["rwkv_mu"] = d_mu.sum(1).reshape(1152)
    rp = [a.sum(0) for a in d_rprm]
    g["rwkv_w0"] = rp[0].reshape(256)
    g["rwkv_w_up"] = jnp.transpose(rp[1], (1, 0, 2)).reshape(64, 256)
    g["rwkv_a0"] = rp[2].reshape(256)
    g["rwkv_a_up"] = jnp.transpose(rp[3], (1, 0, 2)).reshape(64, 256)
    for i, nme in enumerate(("rwkv_k_k", "rwkv_k_a", "rwkv_r_k", "rwkv_ln_w", "rwkv_ln_b")):
        g[nme] = rp[4 + i].reshape(256)

    dp_sc, d_scw = _sc_bwd(p, q["sc_conv"], dy, nb, t, f"sc_bwd{l}")
    g["sc_conv_w"] = jnp.transpose(d_scw.sum(1), (1, 0, 2)).reshape(SC_TAPS, 256)

    (dp_gla, dad), (d_aup, d_ab, d_gnw), _ = _mixer_bwd(
        _gla_chunk, f"gla_bwd{l}", gla_in, q["gla_prm"], [], ck_gla, dy, 3,
        [(16, BF16), (1, BF16)], [(0, 0), (0, 4), (0, 8), (0, 12), (1, 0)], nb, t)
    g["gla_a_up"] = jnp.transpose(d_aup.sum(0)[:, :16, :GLA_HEAD_K], (1, 0, 2)).reshape(16, 128)
    g["gla_a_bias"] = d_ab.sum(0)[:, 0, :GLA_HEAD_K].reshape(128)
    g["gla_norm_w"] = d_gnw.sum((0, 1, 2))

    singles = jnp.concatenate([dab, dp_rwkv[16:18], dad], axis=0)
    sources = [dconv_in, dz, dp_rwkv, dp_sc, dp_gla, singles]
    d_win = _regroup_out(_dwin(sv["hb"], sources, f"dwin{l}"), f"regroup_out{l}")
    own = _exchange_plan([d_win, d_wout]) if exchange_own else None
    dx, d_pre, got = _dh_prenorm_bwd(sources, q["w_g"], sv["x"], q["pre_w"], dxn, f"dh_bwd{l}", own)
    if exchange_own:
        d_win, d_wout = got
    g["pre_norm_w"] = d_pre.reshape(D_MODEL)
    g["post_norm_w"] = d_post.reshape(D_MODEL)
    return dx, g, d_win, d_wout, side_res


def _local_step(x, tgt, wts, tiny, w_in_all, w_out_all, later_shards=None):
    nb, t, d = x.shape
    xf = x.reshape(nb * t, d)
    overlap = later_shards is not None
    qs, saved = [], []
    big = _big_weights(w_in_all[0], w_out_all[0], 0)
    for l in range(DEPTH):
        q = dict(_layer_params(wts, tiny, l), **big)
        nxt = l + 1 < DEPTH
        side = _gather_plan(later_shards[l]) if overlap and nxt else None
        xf, sv, got = _layer_fwd(xf, q, nb, t, l, side)
        if nxt:
            big = _big_weights(*(got if overlap else (w_in_all[l + 1], w_out_all[l + 1])), l + 1)
        qs.append(q)
        saved.append(sv)
    dxf, lpart = _loss_grad(xf, tgt.reshape(nb * t, d), "loss")
    grads, d_win, d_wout = [None] * DEPTH, [None] * DEPTH, [None] * DEPTH
    for l in reversed(range(DEPTH)):
        side = _exchange_plan([d_win[l + 1], d_wout[l + 1]]) if overlap and l + 1 < DEPTH else None
        dxf, grads[l], d_win[l], d_wout[l], got = _layer_bwd(dxf, qs[l], saved[l], nb, t, l, side,
                                                             exchange_own=overlap and l == 0)
        if side is not None:
            d_win[l + 1], d_wout[l + 1] = got
    small = {k: jnp.stack([grads[l][k] for l in range(DEPTH)]) for k in grads[0]}
    return lpart[0, 0], dxf.reshape(nb, t, d), small, d_win, d_wout


_WEIGHTS = ("pre_norm_w", "w_in", "gdn_conv_w", "gdn_a_log", "gdn_dt_bias", "gdn_norm_w", "rwkv_mu", "rwkv_w0",
            "rwkv_w_up", "rwkv_a0", "rwkv_a_up", "rwkv_k_k", "rwkv_k_a", "rwkv_r_k", "rwkv_ln_w", "rwkv_ln_b",
            "sc_conv_w", "gla_a_up", "gla_a_bias", "gla_norm_w", "w_out", "post_norm_w")


def kernel(x, pre_norm_w, w_in, gdn_conv_w, gdn_a_log, gdn_dt_bias, gdn_norm_w, rwkv_mu, rwkv_w0, rwkv_w_up, rwkv_a0, rwkv_a_up, rwkv_k_k, rwkv_k_a, rwkv_r_k, rwkv_ln_w, rwkv_ln_b, sc_conv_w, gla_a_up, gla_a_bias, gla_norm_w, w_out, post_norm_w, loss_target, m_pre_norm_w, m_w_in, m_gdn_conv_w, m_gdn_a_log, m_gdn_dt_bias, m_gdn_norm_w, m_rwkv_mu, m_rwkv_w0, m_rwkv_w_up, m_rwkv_a0, m_rwkv_a_up, m_rwkv_k_k, m_rwkv_k_a, m_rwkv_r_k, m_rwkv_ln_w, m_rwkv_ln_b, m_sc_conv_w, m_gla_a_up, m_gla_a_bias, m_gla_norm_w, m_w_out, m_post_norm_w, v_pre_norm_w, v_w_in, v_gdn_conv_w, v_gdn_a_log, v_gdn_dt_bias, v_gdn_norm_w, v_rwkv_mu, v_rwkv_w0, v_rwkv_w_up, v_rwkv_a0, v_rwkv_a_up, v_rwkv_k_k, v_rwkv_k_a, v_rwkv_r_k, v_rwkv_ln_w, v_rwkv_ln_b, v_sc_conv_w, v_gla_a_up, v_gla_a_bias, v_gla_norm_w, v_w_out, v_post_norm_w):
    env = dict(locals())
    w = {n: env[n] for n in _WEIGHTS}
    m = {n: env["m_" + n] for n in _WEIGHTS}
    v = {n: env["v_" + n] for n in _WEIGHTS}
    tiny_names = [n for n, _ in _TINY]

    w_in_b, w_out_b = w_in.astype(BF16), w_out.astype(BF16)
    w_in_0, w_out_0, tiny_all = _all_gather_two_level(
        [w_in_b[0], w_out_b[0], _pack_rows([w[n] for n in tiny_names], _TINY_ROWS)], "gather_weights")
    tiny = _unpack_rows(tiny_all, _TINY)

    lpart, grad_x, small, r_win, r_wout = _local_step(
        x, loss_target, w, tiny, [w_in_0], [w_out_0], later_shards=[(w_in_b[l], w_out_b[l]) for l in range(1, DEPTH)])

    tiny_send = jnp.stack([_pack_rows([_split_last(small[n][l])[d] for n in tiny_names for l in range(DEPTH)],
                                      _TINY_ROWS) for d in range(N_DEV)])
    (r_tiny,) = _run_comm(_exchange_plan([tiny_send]), "scatter_grads")
    grads, delta, new_m, new_v = {}, {}, {}, {}
    for n, parts in (("w_in", r_win), ("w_out", r_wout)):
        res = [_sum_adamw(parts[l], w[n][l], m[n][l], v[n][l], f"adamw_{n}{l}") for l in range(DEPTH)]
        grads[n], delta[n], new_m[n], new_v[n] = [jnp.stack(o) for o in zip(*res)]
    tiny_sum = _sum_slots(r_tiny, "sum_tiny").reshape(-1)
    o = 0
    for n, s in _TINY:
        size = int(np.prod(s))
        grads[n] = tiny_sum[o:o + size].reshape(s)
        o += size

    small = dict(small)
    small["loss"] = lpart
    red = _unpack_rows(_all_reduce_small(_pack_rows([small[n] for n, _ in _SMALL], _SMALL_ROWS), "reduce_small"),
                       _SMALL)
    loss = red.pop("loss")
    grads.update(red)

    rest = [n for n in _WEIGHTS if n not in ("w_in", "w_out")]
    rest_shapes = [(n, w[n].shape) for n in rest]
    rows = -(-sum(int(np.prod(s)) for _, s in rest_shapes) // 1024) * 8
    outs = _adamw(_pack_rows([w[n] for n in rest], rows), _pack_rows([grads[n] for n in rest], rows),
                  _pack_rows([m[n] for n in rest], rows), _pack_rows([v[n] for n in rest], rows, 1.0), "adamw_rest")
    for dst, packed in zip((delta, new_m, new_v), outs):
        dst.update(_unpack_rows(packed, rest_shapes))

    return (loss, grad_x, *[grads[n] for n in _WEIGHTS], *[delta[n] for n in _WEIGHTS],
            *[new_m[n] for n in _WEIGHTS], *[new_v[n] for n in _WEIGHTS])
```

```python
import collections
import functools
import math

import numpy as np
import jax
import jax.numpy as jnp
from jax import lax
from jax.experimental import pallas as pl
from jax.experimental.pallas import tpu as pltpu

F32 = jnp.float32
BF16 = jnp.bfloat16

D_MODEL = 1024
DEPTH = 2
NH = 4
DH = 64
CH = 64
EPS = 1e-6
RWKV_GN_EPS = 64e-5
GLA_HEAD_K = 32
GLA_TAU = 16.0
GDN_TAPS = 4
SC_TAPS = 3
D_IN = 3992
N_DEV = 8
SHARD_COLS = D_IN // N_DEV

G_GDN = 0
G_RWKV = 16
G_SC = 32
G_GLA = 48
G_GDN_AB, G_RWKV_WD, G_RWKV_AD, G_GLA_AD = 64, 65, 66, 67
N_GROUPS = 68
GROUPS_PER_STEP = 4
TIME_BLOCK = 256
RWKV_EXACT_STEPS = 16

C_GDN, C_RWKV, C_SC, C_GLA = 0, 1032, 2184, 3208

ADAM_LR, ADAM_B1, ADAM_B2, ADAM_EPS, ADAM_WD, ADAM_STEP = 0.001, 0.9, 0.999, 1e-08, 0.01, 10

VMEM_LIMIT = 56 * 1024 * 1024
MESH = pl.DeviceIdType.MESH

_pcall = pl.pallas_call

_Comm = collections.namedtuple("_Comm", "operands out_shapes copies")


def _cparams(sem=None):
    if sem is None:
        return pltpu.CompilerParams(vmem_limit_bytes=VMEM_LIMIT)
    return pltpu.CompilerParams(dimension_semantics=sem, vmem_limit_bytes=VMEM_LIMIT)


def _group_segments():
    table = [(G_GDN + i, C_GDN + DH * i, DH) for i in range(16)]
    table.append((G_GDN_AB, C_GDN + 1024, 8))
    table += [(G_RWKV + i, C_RWKV + DH * i, DH) for i in range(16)]
    table += [(G_RWKV_WD, C_RWKV + 1024, DH), (G_RWKV_AD, C_RWKV + 1088, DH)]
    table += [(G_SC + 4 * j + k, C_SC + 256 * k + DH * j, DH) for j in range(NH) for k in range(4)]
    for h in range(NH):
        table += [(G_GLA + h, C_GLA + GLA_HEAD_K * h, GLA_HEAD_K),
                  (G_GLA + 4 + h, C_GLA + 128 + GLA_HEAD_K * h, GLA_HEAD_K),
                  (G_GLA + 8 + h, C_GLA + 256 + DH * h, DH),
                  (G_GLA + 12 + h, C_GLA + 512 + DH * h, DH)]
    table.append((G_GLA_AD, C_GLA + 768, 16))
    segs, padded = [], []
    for g, c, n in table:
        if n < DH:
            padded.append(g)
        a = 0
        while n > 0:
            d, off = divmod(c, SHARD_COLS)
            ln = min(n, SHARD_COLS - off)
            segs.append((g, a, d, off, ln))
            c, a, n = c + ln, a + ln, n - ln
    return segs, padded


_SEGMENTS, _PADDED_GROUPS = _group_segments()


def _dn(ta, tb):
    return (((1 if ta else 2,), (2 if tb else 1,)), ((0,), (0,)))


def _hdot(a, b, ta=False, tb=False):
    return lax.dot_general(a, b, _dn(ta, tb), precision=lax.Precision.HIGH, preferred_element_type=F32)


def _r(x):
    return x.astype(BF16)


def _rdot(a, b, ta=False, tb=False):
    return lax.dot_general(_r(a), _r(b), _dn(ta, tb), preferred_element_type=F32)


@jax.custom_vjp
def _bmm(a, b):
    return _rdot(a, b)


def _bmm_fwd(a, b):
    return _rdot(a, b), (a, b)


def _bmm_bwd(res, g):
    a, b = res
    return _rdot(g, b, tb=True), _rdot(a, g, ta=True)


_bmm.defvjp(_bmm_fwd, _bmm_bwd)


@jax.custom_vjp
def _bmm_nt(a, b):
    return _rdot(a, b, tb=True)


def _bmm_nt_fwd(a, b):
    return _rdot(a, b, tb=True), (a, b)


def _bmm_nt_bwd(res, g):
    a, b = res
    return _rdot(g, b), _rdot(g, a, ta=True)


_bmm_nt.defvjp(_bmm_nt_fwd, _bmm_nt_bwd)


@jax.custom_vjp
def _bmm_tn(a, b):
    return _rdot(a, b, ta=True)


def _bmm_tn_fwd(a, b):
    return _rdot(a, b, ta=True), (a, b)


def _bmm_tn_bwd(res, g):
    a, b = res
    return _rdot(b, g, tb=True), _rdot(a, g)


_bmm_tn.defvjp(_bmm_tn_fwd, _bmm_tn_bwd)


def _tri(n):
    i = lax.broadcasted_iota(jnp.int32, (n, n), 0)
    j = lax.broadcasted_iota(jnp.int32, (n, n), 1)
    return i >= j, i > j, i == j


def _heads_of(x, like):
    n = like.shape[0]
    if x.ndim == 2:
        return jnp.broadcast_to(x[None], (n,) + x.shape)
    seqs = x.shape[0]
    return jnp.broadcast_to(x[:, None], (seqs, n // seqs) + x.shape[1:]).reshape((n,) + x.shape[1:])


def _cumsum_rows(x):
    incl, _, _ = _tri(x.shape[-2])
    return _hdot(_heads_of(incl.astype(F32), x), x)


@jax.custom_vjp
def _inv_unit_lower(a):
    n = a.shape[-1]
    _, _, eye = _tri(n)
    pw = -a
    inv = eye.astype(F32) + pw
    for _ in range(math.ceil(math.log2(n)) - 1):
        pw = _hdot(pw, pw)
        inv = inv + _hdot(inv, pw)
    return inv


def _inv_unit_lower_fwd(a):
    inv = _inv_unit_lower(a)
    return inv, inv


def _inv_unit_lower_bwd(inv, g):
    return (-_hdot(_hdot(inv, g, ta=True), inv, tb=True),)


_inv_unit_lower.defvjp(_inv_unit_lower_fwd, _inv_unit_lower_bwd)


@jax.custom_vjp
def _inv_reuse(a, inv):
    return inv


def _inv_reuse_fwd(a, inv):
    return inv, inv


def _inv_reuse_bwd(inv, g):
    return _inv_unit_lower_bwd(inv, g)[0], jnp.zeros_like(inv)


_inv_reuse.defvjp(_inv_reuse_fwd, _inv_reuse_bwd)


def _silu(x):
    return x * jax.nn.sigmoid(x)


def _t(x):
    return jnp.swapaxes(x, -1, -2)


def _gdn_chunk(prm, cst, ins, s, tinv=None):
    a_log, dt_b, nw = prm
    m_a, m_b = cst
    cq, ck, cv, z, ab = ins
    ab = _heads_of(ab, m_a)
    incl, strict, _ = _tri(CH)
    q = _silu(cq)
    k = _silu(ck)
    v = _silu(cv)
    q = q * lax.rsqrt(jnp.sum(q * q, -1, keepdims=True) + EPS) * (DH ** -0.5)
    k = k * lax.rsqrt(jnp.sum(k * k, -1, keepdims=True) + EPS)
    a_raw = jnp.sum(ab * m_a, -1, keepdims=True)
    b_raw = jnp.sum(ab * m_b, -1, keepdims=True)
    gstep = -jnp.exp(a_log) * jax.nn.softplus(a_raw + dt_b)
    beta = jax.nn.sigmoid(b_raw)
    gc = _cumsum_rows(gstep)
    gl = jnp.sum(gstep, -2, keepdims=True)
    dec = jnp.where(incl, jnp.exp(jnp.where(incl, gc - _t(gc), 0.0)), 0.0)
    kb = k * beta
    a_mat = jnp.where(strict, _bmm_nt(kb, k) * dec, 0.0)
    tinv = _inv_unit_lower(a_mat) if tinv is None else _inv_reuse(a_mat, tinv)
    eg = jnp.exp(gc)
    u = _hdot(tinv, v * beta)
    w = _hdot(tinv, kb * eg)
    attn = _bmm_nt(q, k) * dec
    v_new = u - _bmm(w, s)
    o = _bmm(q * eg, s) + _bmm(attn, v_new)
    s_next = s * jnp.exp(gl) + _bmm_tn(k * jnp.exp(gl - gc), v_new)
    on = o * lax.rsqrt(jnp.mean(o * o, -1, keepdims=True) + EPS) * nw
    return on * _silu(z), s_next, tinv


def _gla_chunk(prm, cst, ins, st):
    a_up, a_bias, nw = prm
    q, k, v, z, ad = ins
    incl, _, _ = _tri(CH)
    la = jax.nn.log_sigmoid(_bmm(_heads_of(ad, a_up), a_up) + a_bias) * (1.0 / GLA_TAU)
    bc = _cumsum_rows(la)
    bl = jnp.sum(la, -2, keepdims=True)
    qe = q * (GLA_HEAD_K ** -0.5) * jnp.exp(bc)
    ke = k * jnp.exp(-bc)
    attn = jnp.where(incl, _bmm_nt(qe, ke), 0.0)
    o = _bmm_nt(qe, st) + _bmm(attn, v)
    st_next = st * jnp.exp(bl) + _bmm_tn(v, k * jnp.exp(bl - bc))
    on = o * lax.rsqrt(jnp.mean(o * o, -1, keepdims=True) + EPS) * nw
    return on * _silu(z), st_next


def _rwkv_chunk(prm, cst, ins, s, inv=None):
    r, v = ins[0], ins[2]
    incl, strict, _ = _tri(r.shape[-2])
    lw, kk, k2, m = _rwkv_pre(prm, ins)
    cum = _cumsum_rows(lw)
    ltot = jnp.sum(lw, -2, keepdims=True)
    n_t = -kk * jnp.exp(cum - lw)
    einv = jnp.exp(-cum)
    m_t = m * einv
    k_t = k2 * einv
    r_t = r * jnp.exp(cum)
    a_nm = jnp.where(strict, _hdot(n_t, m_t, tb=True), 0.0)
    a_nk = jnp.where(strict, _hdot(n_t, k_t, tb=True), 0.0)
    inv = _inv_unit_lower(-a_nm) if inv is None else _inv_reuse(-a_nm, inv)
    cm = _hdot(inv, _hdot(n_t, s, tb=True) + _bmm(a_nk, v))
    y = (_bmm_nt(r_t, s) + _bmm(jnp.where(incl, _hdot(r_t, m_t, tb=True), 0.0), cm)
         + _bmm(jnp.where(incl, _hdot(r_t, k_t, tb=True), 0.0), v))
    eend = jnp.exp(ltot - cum)
    s_next = s * jnp.exp(ltot) + _bmm_tn(cm, m * eend) + _bmm_tn(v, k2 * eend)
    return _rwkv_post(prm, ins, y, k2), s_next, inv


def _rwkv_pre(prm, ins):
    w0, w_up, a0, a_up, k_k, k_a = prm[:6]
    k, wd, ad = ins[1], ins[4], ins[5]
    lw = -math.exp(-0.5) * jax.nn.sigmoid(w0 + _bmm(_heads_of(jnp.tanh(wd), w_up), w_up))
    a = jax.nn.sigmoid(a0 + _bmm(_heads_of(ad, a_up), a_up))
    kk = k * k_k
    kk = kk * lax.rsqrt(jnp.sum(kk * kk, -1, keepdims=True) + EPS)
    k2 = k * (1.0 + (a - 1.0) * k_a)
    return lw, kk, k2, kk * a


def _rwkv_post(prm, ins, y, k2):
    r_k, ln_w, ln_b = prm[6:]
    r, v, z = ins[0], ins[2], ins[3]
    mean = jnp.mean(y, -1, keepdims=True)
    yc = y - mean
    var = jnp.mean(yc * yc, -1, keepdims=True)
    yn = yc * lax.rsqrt(var + RWKV_GN_EPS) * ln_w + ln_b
    bonus = jnp.sum(r * k2 * r_k, -1, keepdims=True) * v
    return (yn + bonus) * _silu(z)


@jax.custom_vjp
def _bmv(s, x):
    return jnp.sum(_r(s).astype(F32) * _r(x).astype(F32), -1, keepdims=True)


def _bmv_fwd(s, x):
    return _bmv(s, x), (s, x)


def _bmv_bwd(res, g):
    s, x = res
    return g * x, jnp.sum(_r(s).astype(F32) * _r(g).astype(F32), -2, keepdims=True)


_bmv.defvjp(_bmv_fwd, _bmv_bwd)


def _rwkv_steps(prm, cst, ins, s, steps):
    r, v = ins[0], ins[2]
    lw, kk, k2, m = _rwkv_pre(prm, ins)
    w = jnp.exp(lw)
    v_t = _t(v)
    lane = lax.broadcasted_iota(jnp.int32, (1, 1, CH), 2)
    y_t = jnp.zeros((s.shape[0], DH, CH), F32)
    for t in range(steps):
        e_t = (lane == t).astype(F32)
        row = (slice(None), slice(t, t + 1))
        sa = _bmv(s, -kk[row])
        s = s * w[row] + sa * m[row] + jnp.sum(v_t * e_t, -1, keepdims=True) * k2[row]
        y_t = y_t + _bmv(s, r[row]) * e_t
    return _rwkv_post(prm, ins, _t(y_t), k2)[:, :steps], s


def _rwkv_first_chunk(prm, cst, ins, s):
    k = RWKV_EXACT_STEPS
    y_head, s = _rwkv_steps(prm, cst, ins, s, k)
    y_tail, s, _ = _rwkv_chunk(prm, cst, [x[..., k:, :] for x in ins], s)
    return jnp.concatenate([y_head, y_tail], axis=-2), s


def _time_block(t):
    return TIME_BLOCK if t % TIME_BLOCK == 0 else t


def _load_chunk(ref, i):
    nb = ref.shape[1]
    if ref.shape[0] == NH:
        return jnp.concatenate([ref[:, b, pl.ds(i, CH), :] for b in range(nb)], axis=0)
    return ref[0, :, pl.ds(i, CH), :]


def _load_chunk_of(ref, c, i):
    if ref.shape[0] == NH:
        return ref[pl.ds(c % NH, 1), c // NH, pl.ds(i, CH), :]
    return ref[0, pl.ds(c // NH, 1), pl.ds(i, CH), :]


def _each_chain(n, fn):
    def step(c, carry):
        fn(c)
        return carry

    lax.fori_loop(0, n, step, 0)


def _mixer_fwd(chunk_fn, name, ins, prm, cst, nb, t, first_fn=None, side=None, n_kept=0):
    tb = _time_block(t)
    nt, ncb, nch = t // tb, tb // CH, nb * NH
    n_in, n_prm, n_cst = len(ins), len(prm), len(cst)
    n_main, n_side = n_in + n_prm + n_cst, len(side.operands) if side else 0

    def body(*refs):
        in_refs = refs[:n_in]
        prm_refs = refs[n_in:n_in + n_prm]
        cst_refs = refs[n_in + n_prm:n_main]
        side_in = refs[n_main:n_main + n_side]
        y_ref, ck_ref = refs[n_main + n_side:n_main + n_side + 2]
        side_out = refs[n_main + n_side + 2:n_main + 2 * n_side + 2]
        s_scr = refs[n_main + 2 * n_side + 2]
        sems = refs[n_main + 2 * n_side + 3:]
        step_t = pl.program_id(0)

        if side is not None:
            @pl.when(step_t == 0)
            def _():
                _comm_start(side.copies(side_in, side_out, sems))

        @pl.when(step_t == 0)
        def _():
            s_scr[...] = jnp.zeros_like(s_scr)

        def chunk(c, i):
            s = s_scr[...]
            y, s_next, *kept = chunk_fn([jnp.tile(r[...], (nb, 1, 1)) for r in prm_refs],
                                        [jnp.tile(r[...], (nb, 1, 1)) for r in cst_refs],
                                        [_load_chunk(r, i) for r in in_refs], s)
            for e, a in enumerate([s] + kept):
                ck_ref[c, e] = a
            for b in range(nb):
                y_ref[:, b, pl.ds(i, CH), :] = y[b * NH:(b + 1) * NH].astype(BF16)
            s_scr[...] = s_next

        def first_chunk_of(c):
            one, h = pl.ds(c, 1), pl.ds(c % NH, 1)
            s = s_scr[one]
            ck_ref[0, 0, one] = s
            for e in range(n_kept):
                ck_ref[0, 1 + e, one] = jnp.zeros((1, DH, DH), F32)
            y, s_next = first_fn([r[h] for r in prm_refs], [r[h] for r in cst_refs],
                                 [_load_chunk_of(r, c, 0) for r in in_refs], s)
            y_ref[h, c // NH, pl.ds(0, CH), :] = y.astype(BF16)
            s_scr[one] = s_next

        def step(c, carry):
            chunk(c, pl.multiple_of(c * CH, CH))
            return carry

        if first_fn is None:
            lax.fori_loop(0, ncb, step, 0)
        else:
            @pl.when(step_t == 0)
            def _():
                _each_chain(nch, first_chunk_of)

            @pl.when(step_t != 0)
            def _():
                chunk(0, 0)

            lax.fori_loop(1, ncb, step, 0)

        if side is not None:
            @pl.when(step_t == nt - 1)
            def _():
                _comm_wait(side.copies(side_in, side_out, sems))

    hbm = pl.BlockSpec(memory_space=pl.ANY)
    in_specs = [pl.BlockSpec((ng, nb, tb, DH), (lambda j, bi=bi: (bi, 0, j, 0))) for _, ng, bi in ins]
    in_specs += [pl.BlockSpec(p.shape, lambda j: (0, 0, 0)) for p in list(prm) + list(cst)]
    y, ck, *side_res = _pcall(
        body, name=name, grid=(nt,),
        in_specs=in_specs + [hbm] * n_side,
        out_specs=[pl.BlockSpec((NH, nb, tb, DH), lambda j: (0, 0, j, 0)),
                   pl.BlockSpec((ncb, 1 + n_kept, nch, DH, DH), lambda j: (j, 0, 0, 0, 0))] + [hbm] * n_side,
        out_shape=[jax.ShapeDtypeStruct((NH, nb, t, DH), BF16),
                   jax.ShapeDtypeStruct((t // CH, 1 + n_kept, nch, DH, DH), F32)]
        + (side.out_shapes if side else []),
        scratch_shapes=[pltpu.VMEM((nch, DH, DH), F32)] + (_comm_scratch(side) if side else []),
        compiler_params=_cparams(("arbitrary",)),
    )(*[a.reshape(a.shape[0], nb, t, DH) for a, _, _ in ins], *prm, *cst, *(side.operands if side else []))
    return y.reshape(NH, nb * t, DH), ck, side_res


def _mixer_bwd(chunk_fn, name, ins, prm, cst, ck, dy, dy_block, outs, routes, nb, t, first_fn=None, side=None):
    tb = _time_block(t)
    nt, ncb, nch = t // tb, tb // CH, nb * NH
    n_in, n_prm, n_cst, n_out = len(ins), len(prm), len(cst), len(outs)
    n_main, n_side = n_in + n_prm + n_cst + 2, len(side.operands) if side else 0
    n_kept = ck.shape[1] - 1

    def body(*refs):
        in_refs = refs[:n_in]
        prm_refs = refs[n_in:n_in + n_prm]
        cst_refs = refs[n_in + n_prm:n_in + n_prm + n_cst]
        ck_ref, dy_ref = refs[n_main - 2:n_main]
        side_in = refs[n_main:n_main + n_side]
        rest = refs[n_main + n_side:]
        out_refs = rest[:n_out]
        dprm_refs = rest[n_out:n_out + n_prm]
        side_out = rest[n_out + n_prm:n_out + n_prm + n_side]
        ds_scr = rest[n_out + n_prm + n_side]
        sems = rest[n_out + n_prm + n_side + 1:]
        step_t = pl.program_id(0)

        if side is not None:
            @pl.when(step_t == 0)
            def _():
                _comm_start(side.copies(side_in, side_out, sems))

        @pl.when(step_t == 0)
        def _():
            ds_scr[...] = jnp.zeros_like(ds_scr)
            for r in dprm_refs:
                r[...] = jnp.zeros_like(r)

        def chunk(c, i):
            cst_v = [jnp.tile(r[...], (nb, 1, 1)) for r in cst_refs]
            kept = [ck_ref[c, 1 + e] for e in range(n_kept)]
            _, vjp = jax.vjp(lambda p, x, s: chunk_fn(p, cst_v, x, s, *kept)[:2],
                             [jnp.tile(r[...], (nb, 1, 1)) for r in prm_refs],
                             [_load_chunk(r, i) for r in in_refs], ck_ref[c, 0])
            dy_c = jnp.concatenate([dy_ref[:, b, pl.ds(i, CH), :] for b in range(nb)], axis=0)
            d_prm, d_ins, d_s = vjp((dy_c, ds_scr[...]))
            for (oi, g0), r, g in zip(routes, in_refs, d_ins):
                o_ref = out_refs[oi]
                if r.shape[0] == NH:
                    for b in range(nb):
                        o_ref[g0:g0 + NH, b, pl.ds(i, CH), :] = g[b * NH:(b + 1) * NH].astype(o_ref.dtype)
                else:
                    o_ref[g0, :, pl.ds(i, CH), :] = g.astype(o_ref.dtype)
            for r, g in zip(dprm_refs, d_prm):
                r[...] += g
            ds_scr[...] = d_s

        def first_chunk_of(c):
            one, h, b = pl.ds(c, 1), pl.ds(c % NH, 1), c // NH
            cst_v = [r[h] for r in cst_refs]
            _, vjp = jax.vjp(lambda p, x, s: first_fn(p, cst_v, x, s), [r[h] for r in prm_refs],
                             [_load_chunk_of(r, c, 0) for r in in_refs], ck_ref[0, 0, one])
            d_prm, d_ins, d_s = vjp((dy_ref[h, b, pl.ds(0, CH), :], ds_scr[one]))
            for (oi, g0), r, g in zip(routes, in_refs, d_ins):
                o_ref = out_refs[oi]
                if r.shape[0] == NH:
                    o_ref.at[g0:g0 + NH][h, b, pl.ds(0, CH), :] = g.astype(o_ref.dtype)
                else:
                    o_ref[g0, pl.ds(b, 1), pl.ds(0, CH), :] += g.astype(o_ref.dtype)
            for r, g in zip(dprm_refs, d_prm):
                r[one] += g
            ds_scr[one] = d_s

        def first_chunk():
            for (oi, g0), r in zip(routes, in_refs):
                if r.shape[0] != NH:
                    out_refs[oi][g0, :, pl.ds(0, CH), :] = jnp.zeros((nb, CH, DH), out_refs[oi].dtype)
            _each_chain(nch, first_chunk_of)

        def step(j, carry):
            c = ncb - 1 - j
            chunk(c, pl.multiple_of(c * CH, CH))
            return carry

        lax.fori_loop(0, ncb - 1, step, 0)
        if first_fn is None:
            chunk(0, 0)
        else:
            @pl.when(step_t == nt - 1)
            def _():
                first_chunk()

            @pl.when(step_t != nt - 1)
            def _():
                chunk(0, 0)

        if side is not None:
            @pl.when(step_t == nt - 1)
            def _():
                _comm_wait(side.copies(side_in, side_out, sems))

    def back(j):
        return nt - 1 - j

    hbm = pl.BlockSpec(memory_space=pl.ANY)
    in_specs = [pl.BlockSpec((ng, nb, tb, DH), (lambda j, bi=bi: (bi, 0, back(j), 0))) for _, ng, bi in ins]
    in_specs += [pl.BlockSpec(p.shape, lambda j: (0, 0, 0)) for p in list(prm) + list(cst)]
    in_specs += [pl.BlockSpec((ncb, 1 + n_kept, nch, DH, DH), lambda j: (back(j), 0, 0, 0, 0)),
                 pl.BlockSpec((NH, nb, tb, DH), lambda j: (dy_block, 0, back(j), 0))]
    out_specs = [pl.BlockSpec((ng, nb, tb, DH), lambda j: (0, 0, back(j), 0)) for ng, _ in outs]
    out_specs += [pl.BlockSpec((nch,) + p.shape[1:], lambda j: (0, 0, 0)) for p in prm]
    out_shape = [jax.ShapeDtypeStruct((ng, nb, t, DH), dt) for ng, dt in outs]
    out_shape += [jax.ShapeDtypeStruct((nch,) + p.shape[1:], F32) for p in prm]
    res = _pcall(
        body, name=name, grid=(nt,),
        in_specs=in_specs + [hbm] * n_side, out_specs=out_specs + [hbm] * n_side,
        out_shape=out_shape + (side.out_shapes if side else []),
        scratch_shapes=[pltpu.VMEM((nch, DH, DH), F32)] + (_comm_scratch(side) if side else []),
        compiler_params=_cparams(("arbitrary",)),
    )(*[a.reshape(a.shape[0], nb, t, DH) for a, _, _ in ins], *prm, *cst, ck, dy.reshape(dy.shape[0], nb, t, DH),
      *(side.operands if side else []))
    d_outs = [o.reshape(o.shape[0], nb * t, DH) for o in res[:n_out]]
    d_prm = [g.reshape((nb,) + p.shape) for g, p in zip(res[n_out:n_out + n_prm], prm)]
    return d_outs, d_prm, res[n_out + n_prm:]


def _shift_down(x, s):
    if s == 0:
        return x
    row = lax.broadcasted_iota(jnp.int32, x.shape, 0)
    return jnp.where(row < s, 0.0, pltpu.roll(x, s, 0))


def _shift_up(x, s):
    if s == 0:
        return x
    t = x.shape[0]
    row = lax.broadcasted_iota(jnp.int32, x.shape, 0)
    return jnp.where(row >= t - s, 0.0, pltpu.roll(x, t - s, 0))


def _conv_fwd(p, g0, ng, w, nb, t, name):
    taps = w.shape[1]

    def body(x_ref, w_ref, y_ref):
        x = x_ref[...]
        acc = w_ref[taps - 1:taps, :] * x
        for i in range(taps - 1):
            acc = acc + w_ref[i:i + 1, :] * _shift_down(x, taps - 1 - i)
        y_ref[...] = acc

    return _pcall(
        body, name=name, grid=(ng, nb),
        in_specs=[pl.BlockSpec((None, t, DH), lambda g, b: (g0 + g, b, 0)),
                  pl.BlockSpec((None, taps, DH), lambda g, b: (g, 0, 0))],
        out_specs=pl.BlockSpec((None, t, DH), lambda g, b: (g, b, 0)),
        out_shape=jax.ShapeDtypeStruct((ng, nb * t, DH), F32),
        compiler_params=_cparams(("parallel", "parallel")),
    )(p, w)


def _conv_bwd(p, g0, ng, w, dy, nb, t, name):
    taps = w.shape[1]

    def body(x_ref, w_ref, dy_ref, dx_ref, dw_ref):
        x = x_ref[...]
        d = dy_ref[...]
        acc = w_ref[taps - 1:taps, :] * d
        dw_ref[taps - 1:taps, :] = jnp.sum(d * x, 0, keepdims=True)
        for i in range(taps - 1):
            s = taps - 1 - i
            acc = acc + w_ref[i:i + 1, :] * _shift_up(d, s)
            dw_ref[i:i + 1, :] = jnp.sum(d * _shift_down(x, s), 0, keepdims=True)
        dx_ref[...] = acc.astype(BF16)

    return _pcall(
        body, name=name, grid=(ng, nb),
        in_specs=[pl.BlockSpec((None, t, DH), lambda g, b: (g0 + g, b, 0)),
                  pl.BlockSpec((None, taps, DH), lambda g, b: (g, 0, 0)),
                  pl.BlockSpec((None, t, DH), lambda g, b: (g, b, 0))],
        out_specs=[pl.BlockSpec((None, t, DH), lambda g, b: (g, b, 0)),
                   pl.BlockSpec((None, None, taps, DH), lambda g, b: (g, b, 0, 0))],
        out_shape=[jax.ShapeDtypeStruct((ng, nb * t, DH), BF16),
                   jax.ShapeDtypeStruct((ng, nb, taps, DH), F32)],
        compiler_params=_cparams(("parallel", "parallel")),
    )(p, w, dy)


def _mix_group(g):
    return jnp.where(g < 16, G_RWKV + g, G_RWKV_WD + g - 16)


def _mix_fwd(p, mu, nb, t, name):
    def body(x_ref, mu_ref, y_ref):
        x = x_ref[...]
        y_ref[...] = x + mu_ref[...] * (_shift_down(x, 1) - x)

    return _pcall(
        body, name=name, grid=(18, nb),
        in_specs=[pl.BlockSpec((None, t, DH), lambda g, b: (_mix_group(g), b, 0)),
                  pl.BlockSpec((None, 1, DH), lambda g, b: (g, 0, 0))],
        out_specs=pl.BlockSpec((None, t, DH), lambda g, b: (g, b, 0)),
        out_shape=jax.ShapeDtypeStruct((18, nb * t, DH), F32),
        compiler_params=_cparams(("parallel", "parallel")),
    )(p, mu)


def _mix_bwd(p, mu, dy, nb, t, name):
    def body(x_ref, mu_ref, dy_ref, dx_ref, dmu_ref):
        x = x_ref[...]
        muv = mu_ref[...]
        d = dy_ref[...]
        dx_ref[...] = (d * (1.0 - muv) + _shift_up(d * muv, 1)).astype(BF16)
        dmu_ref[...] = jnp.sum(d * (_shift_down(x, 1) - x), 0, keepdims=True)

    return _pcall(
        body, name=name, grid=(18, nb),
        in_specs=[pl.BlockSpec((None, t, DH), lambda g, b: (_mix_group(g), b, 0)),
                  pl.BlockSpec((None, 1, DH), lambda g, b: (g, 0, 0)),
                  pl.BlockSpec((None, t, DH), lambda g, b: (g, b, 0))],
        out_specs=[pl.BlockSpec((None, t, DH), lambda g, b: (g, b, 0)),
                   pl.BlockSpec((None, None, 1, DH), lambda g, b: (g, b, 0, 0))],
        out_shape=[jax.ShapeDtypeStruct((18, nb * t, DH), BF16),
                   jax.ShapeDtypeStruct((18, nb, 1, DH), F32)],
        compiler_params=_cparams(("parallel", "parallel")),
    )(p, mu, dy)


def _sc_fwd(p, w, nb, t, name):
    def body(p_ref, w_ref, y_ref):
        u = p_ref[1] * p_ref[2]
        conv = w_ref[2:3, :] * u + w_ref[1:2, :] * _shift_down(u, 1) + w_ref[0:1, :] * _shift_down(u, 2)
        y_ref[...] = (p_ref[0] * conv * _silu(p_ref[3])).astype(BF16)

    return _pcall(
        body, name=name, grid=(NH, nb),
        in_specs=[pl.BlockSpec((4, t, DH), lambda j, b: (G_SC // 4 + j, b, 0)),
                  pl.BlockSpec((None, SC_TAPS, DH), lambda j, b: (j, 0, 0))],
        out_specs=pl.BlockSpec((None, t, DH), lambda j, b: (j, b, 0)),
        out_shape=jax.ShapeDtypeStruct((NH, nb * t, DH), BF16),
        compiler_params=_cparams(("parallel", "parallel")),
    )(p, w)


def _sc_bwd(p, w, dy, nb, t, name):
    def body(p_ref, w_ref, dy_ref, dp_ref, dw_ref):
        bg, cg, xg, z = p_ref[0], p_ref[1], p_ref[2], p_ref[3]
        d = dy_ref[...]
        u = cg * xg
        u1 = _shift_down(u, 1)
        u2 = _shift_down(u, 2)
        conv = w_ref[2:3, :] * u + w_ref[1:2, :] * u1 + w_ref[0:1, :] * u2
        sg = jax.nn.sigmoid(z)
        sz = z * sg
        dp_ref[0] = (d * conv * sz).astype(BF16)
        dp_ref[3] = (d * bg * conv * (sg * (1.0 + z * (1.0 - sg)))).astype(BF16)
        dconv = d * bg * sz
        du = w_ref[2:3, :] * dconv + w_ref[1:2, :] * _shift_up(dconv, 1) + w_ref[0:1, :] * _shift_up(dconv, 2)
        dp_ref[1] = (du * xg).astype(BF16)
        dp_ref[2] = (du * cg).astype(BF16)
        dw_ref[2:3, :] = jnp.sum(dconv * u, 0, keepdims=True)
        dw_ref[1:2, :] = jnp.sum(dconv * u1, 0, keepdims=True)
        dw_ref[0:1, :] = jnp.sum(dconv * u2, 0, keepdims=True)

    return _pcall(
        body, name=name, grid=(NH, nb),
        in_specs=[pl.BlockSpec((4, t, DH), lambda j, b: (G_SC // 4 + j, b, 0)),
                  pl.BlockSpec((None, SC_TAPS, DH), lambda j, b: (j, 0, 0)),
                  pl.BlockSpec((None, t, DH), lambda j, b: (8 + j, b, 0))],
        out_specs=[pl.BlockSpec((4, t, DH), lambda j, b: (j, b, 0)),
                   pl.BlockSpec((None, None, SC_TAPS, DH), lambda j, b: (j, b, 0, 0))],
        out_shape=[jax.ShapeDtypeStruct((4 * NH, nb * t, DH), BF16),
                   jax.ShapeDtypeStruct((NH, nb, SC_TAPS, DH), F32)],
        compiler_params=_cparams(("parallel", "parallel")),
    )(p, w, dy)


def _row_tile(n):
    return 1024 if n % 1024 == 0 else n


def _regroup_in(w_all, name):
    tr = 256
    gs = GROUPS_PER_STEP

    def body(w_ref, o_ref):
        for g in _PADDED_GROUPS:
            o_ref[g // gs, :, DH * (g % gs):DH * (g % gs + 1)] = jnp.zeros((tr, DH), BF16)
        for g, a, d, off, ln in _SEGMENTS:
            lane = DH * (g % gs) + a
            o_ref[g // gs, :, lane:lane + ln] = w_ref[d, :, off:off + ln].astype(BF16)

    return _pcall(
        body, name=name, grid=(D_MODEL // tr,),
        in_specs=[pl.BlockSpec((N_DEV, tr, SHARD_COLS), lambda i: (0, i, 0))],
        out_specs=pl.BlockSpec((N_GROUPS // gs, tr, gs * DH), lambda i: (0, i, 0)),
        out_shape=jax.ShapeDtypeStruct((N_GROUPS // gs, D_MODEL, gs * DH), BF16),
        compiler_params=_cparams(("parallel",)),
    )(w_all)


def _regroup_out(dwg, name):
    tr = 256
    gs = GROUPS_PER_STEP

    def body(g_ref, o_ref):
        for g, a, d, off, ln in _SEGMENTS:
            lane = DH * (g % gs) + a
            o_ref[d, :, off:off + ln] = g_ref[g // gs, :, lane:lane + ln].astype(BF16)

    return _pcall(
        body, name=name, grid=(D_MODEL // tr,),
        in_specs=[pl.BlockSpec((N_GROUPS // gs, tr, gs * DH), lambda i: (0, i, 0))],
        out_specs=pl.BlockSpec((N_DEV, tr, SHARD_COLS), lambda i: (0, i, 0)),
        out_shape=jax.ShapeDtypeStruct((N_DEV, D_MODEL, SHARD_COLS), BF16),
        compiler_params=_cparams(("parallel",)),
    )(dwg)


def _norm_proj(x, pre_w, w_g, name):
    n = x.shape[0]
    tm = _row_tile(n)
    gs = GROUPS_PER_STEP

    def body(x_ref, pw_ref, w_ref, h_ref, p_ref):
        @pl.when(pl.program_id(1) == 0)
        def _():
            xv = x_ref[...]
            h = xv * lax.rsqrt(jnp.mean(xv * xv, -1, keepdims=True) + EPS) * pw_ref[...]
            h_ref[...] = h.astype(BF16)

        r = jnp.dot(h_ref[...], w_ref[...], preferred_element_type=F32)
        for k in range(gs):
            p_ref[k] = r[:, DH * k:DH * (k + 1)]

    return _pcall(
        body, name=name, grid=(n // tm, N_GROUPS // gs),
        in_specs=[pl.BlockSpec((tm, D_MODEL), lambda i, j: (i, 0)),
                  pl.BlockSpec((1, D_MODEL), lambda i, j: (0, 0)),
                  pl.BlockSpec((None, D_MODEL, gs * DH), lambda i, j: (j, 0, 0))],
        out_specs=[pl.BlockSpec((tm, D_MODEL), lambda i, j: (i, 0)),
                   pl.BlockSpec((gs, tm, DH), lambda i, j: (j, i, 0))],
        out_shape=[jax.ShapeDtypeStruct((n, D_MODEL), BF16),
                   jax.ShapeDtypeStruct((N_GROUPS, n, DH), F32)],
        compiler_params=_cparams(("parallel", "arbitrary")),
    )(x, pre_w, w_g)


def _out_proj_norm(ys, wout_g, x, post_w, name):
    n = x.shape[0]
    tm = _row_tile(n)

    def body(y0, y1, y2, y3, w_ref, x_ref, pw_ref, out_ref, xn_ref):
        y = jnp.concatenate([yr[h] for yr in (y0, y1, y2, y3) for h in range(NH)], axis=-1)
        acc = jnp.dot(y, w_ref[...], preferred_element_type=F32)
        out_ref[...] = acc
        xn_ref[...] = x_ref[...] + acc * lax.rsqrt(jnp.mean(acc * acc, -1, keepdims=True) + EPS) * pw_ref[...]

    yspec = pl.BlockSpec((NH, tm, DH), lambda i: (0, i, 0))
    rows = pl.BlockSpec((tm, D_MODEL), lambda i: (i, 0))
    return _pcall(
        body, name=name, grid=(n // tm,),
        in_specs=[yspec] * 4 + [pl.BlockSpec((D_MODEL, D_MODEL), lambda i: (0, 0)), rows,
                                pl.BlockSpec((1, D_MODEL), lambda i: (0, 0))],
        out_specs=[rows, rows],
        out_shape=[jax.ShapeDtypeStruct((n, D_MODEL), F32)] * 2,
        compiler_params=_cparams(("parallel",)),
    )(*ys, wout_g.reshape(D_MODEL, D_MODEL), x, post_w)


def _loss_grad(x, tgt, name):
    n = x.shape[0]
    tm = _row_tile(n)

    def body(x_ref, t_ref, dx_ref, l_ref):
        @pl.when(pl.program_id(0) == 0)
        def _():
            l_ref[...] = jnp.zeros_like(l_ref)

        e = x_ref[...] - t_ref[...]
        dx_ref[...] = e * (1.0 / D_MODEL)
        l_ref[...] += jnp.sum(jnp.sum(e * e, -1, keepdims=True), 0, keepdims=True) * (0.5 / D_MODEL)

    rows = pl.BlockSpec((tm, D_MODEL), lambda i: (i, 0))
    return _pcall(
        body, name=name, grid=(n // tm,),
        in_specs=[rows, rows],
        out_specs=[rows, pl.BlockSpec((1, 128), lambda i: (0, 0))],
        out_shape=[jax.ShapeDtypeStruct((n, D_MODEL), F32), jax.ShapeDtypeStruct((1, 128), F32)],
        compiler_params=_cparams(("arbitrary",)),
    )(x, tgt)


def _rmsnorm_bwd(xv, w, d):
    r = lax.rsqrt(jnp.mean(xv * xv, -1, keepdims=True) + EPS)
    xh = xv * r
    dxh = d * w
    dx = r * (dxh - xh * jnp.mean(dxh * xh, -1, keepdims=True))
    return dx, d * xh


def _post_bwd(dxn, out, post_w, wout_g, name):
    n = dxn.shape[0]
    tm = _row_tile(n)

    def body(d_ref, o_ref, pw_ref, w_ref, do_ref, dy_ref, dpw_ref):
        @pl.when(pl.program_id(0) == 0)
        def _():
            dpw_ref[...] = jnp.zeros_like(dpw_ref)

        dout, dw_rows = _rmsnorm_bwd(o_ref[...], pw_ref[...], d_ref[...])
        dpw_ref[...] += jnp.sum(dw_rows, 0, keepdims=True)
        db = dout.astype(BF16)
        do_ref[...] = db
        dy = lax.dot_general(db, w_ref[...], (((1,), (1,)), ((), ())), preferred_element_type=F32)
        for g in range(4 * NH):
            dy_ref[g] = dy[:, DH * g:DH * (g + 1)]

    rows = pl.BlockSpec((tm, D_MODEL), lambda i: (i, 0))
    vec = pl.BlockSpec((1, D_MODEL), lambda i: (0, 0))
    return _pcall(
        body, name=name, grid=(n // tm,),
        in_specs=[rows, rows, vec, pl.BlockSpec((D_MODEL, D_MODEL), lambda i: (0, 0))],
        out_specs=[rows, pl.BlockSpec((4 * NH, tm, DH), lambda i: (0, i, 0)), vec],
        out_shape=[jax.ShapeDtypeStruct((n, D_MODEL), BF16),
                   jax.ShapeDtypeStruct((4 * NH, n, DH), F32),
                   jax.ShapeDtypeStruct((1, D_MODEL), F32)],
        compiler_params=_cparams(("arbitrary",)),
    )(dxn, out, post_w, wout_g.reshape(D_MODEL, D_MODEL))


def _dwout(ys, dout, name):
    n = dout.shape[0]
    tm = _row_tile(n)

    def body(y0, y1, y2, y3, d_ref, dw_ref):
        @pl.when(pl.program_id(0) == 0)
        def _():
            dw_ref[...] = jnp.zeros_like(dw_ref)

        y = jnp.concatenate([yr[h] for yr in (y0, y1, y2, y3) for h in range(NH)], axis=-1)
        dw_ref[...] += lax.dot_general(y, d_ref[...], (((0,), (0,)), ((), ())), preferred_element_type=F32)

    yspec = pl.BlockSpec((NH, tm, DH), lambda i: (0, i, 0))
    return _pcall(
        body, name=name, grid=(n // tm,),
        in_specs=[yspec] * 4 + [pl.BlockSpec((tm, D_MODEL), lambda i: (i, 0))],
        out_specs=pl.BlockSpec((D_MODEL, D_MODEL), lambda i: (0, 0)),
        out_shape=jax.ShapeDtypeStruct((D_MODEL, D_MODEL), F32),
        compiler_params=_cparams(("arbitrary",)),
    )(*ys, dout)


def _source_specs(sources, rows_first):
    gs = GROUPS_PER_STEP
    spans, specs, j0 = [], [], 0
    for a in sources:
        nblk = a.shape[0] // gs
        spans.append((j0, j0 + nblk))
        shape = (gs, _row_tile(a.shape[1]), DH)

        def blk(j, j0=j0, nblk=nblk):
            return jnp.clip(j - j0, 0, nblk - 1)

        if rows_first:
            specs.append(pl.BlockSpec(shape, (lambda i, j, blk=blk: (blk(j), i, 0))))
        else:
            specs.append(pl.BlockSpec(shape, (lambda j, i, blk=blk: (blk(j), i, 0))))
        j0 += nblk
    return spans, specs


def _dh_prenorm_bwd(sources, w_g, x, pre_w, dxn, name, side=None):
    n = x.shape[0]
    tm = _row_tile(n)
    gs = GROUPS_PER_STEP
    nj = N_GROUPS // gs
    ni = n // tm
    spans, src_specs = _source_specs(sources, True)
    ns = len(sources)
    n_side = len(side.operands) if side else 0

    def body(*refs):
        src = refs[:ns]
        w_ref, x_ref, pw_ref, d_ref = refs[ns:ns + 4]
        side_in = refs[ns + 4:ns + 4 + n_side]
        dx_ref, dpw_ref = refs[ns + 4 + n_side:ns + 6 + n_side]
        side_out = refs[ns + 6 + n_side:ns + 6 + 2 * n_side]
        acc = refs[ns + 6 + 2 * n_side]
        sems = refs[ns + 7 + 2 * n_side:]
        i, j = pl.program_id(0), pl.program_id(1)

        if side is not None:
            @pl.when((i == 0) & (j == 0))
            def _():
                _comm_start(side.copies(side_in, side_out, sems))

        @pl.when((i == 0) & (j == 0))
        def _():
            dpw_ref[...] = jnp.zeros_like(dpw_ref)

        @pl.when(j == 0)
        def _():
            acc[...] = jnp.zeros_like(acc)

        for s_ref, (lo, hi) in zip(src, spans):
            @pl.when((j >= lo) & (j < hi))
            def _(s_ref=s_ref):
                four = jnp.concatenate([s_ref[k] for k in range(gs)], axis=-1)
                acc[...] += lax.dot_general(four, w_ref[...], (((1,), (1,)), ((), ())), preferred_element_type=F32)

        @pl.when(j == nj - 1)
        def _():
            dx, dw_rows = _rmsnorm_bwd(x_ref[...], pw_ref[...], acc[...])
            dx_ref[...] = d_ref[...] + dx
            dpw_ref[...] += jnp.sum(dw_rows, 0, keepdims=True)

        if side is not None:
            @pl.when((i == ni - 1) & (j == nj - 1))
            def _():
                _comm_wait(side.copies(side_in, side_out, sems))

    hbm = pl.BlockSpec(memory_space=pl.ANY)
    rows = pl.BlockSpec((tm, D_MODEL), lambda i, j: (i, 0))
    vec = pl.BlockSpec((1, D_MODEL), lambda i, j: (0, 0))
    dx, dpw, *side_res = _pcall(
        body, name=name, grid=(ni, nj),
        in_specs=src_specs + [pl.BlockSpec((None, D_MODEL, gs * DH), lambda i, j: (j, 0, 0)), rows, vec, rows]
        + [hbm] * n_side,
        out_specs=[rows, vec] + [hbm] * n_side,
        out_shape=[jax.ShapeDtypeStruct((n, D_MODEL), F32), jax.ShapeDtypeStruct((1, D_MODEL), F32)]
        + (side.out_shapes if side else []),
        scratch_shapes=[pltpu.VMEM((tm, D_MODEL), F32)] + (_comm_scratch(side) if side else []),
        compiler_params=_cparams(("arbitrary", "arbitrary")),
    )(*sources, w_g, x, pre_w, dxn, *(side.operands if side else []))
    return dx, dpw, side_res


def _dwin(hb, sources, name):
    n = hb.shape[0]
    tm = _row_tile(n)
    gs = GROUPS_PER_STEP
    ni, nj = n // tm, N_GROUPS // gs
    spans, src_specs = _source_specs(sources, True)
    ns = len(sources)

    def body(*refs):
        h_ref = refs[0]
        src = refs[1:1 + ns]
        out_ref, acc, sem = refs[1 + ns:]
        i, j = pl.program_id(0), pl.program_id(1)

        @pl.when((i == 0) & (j == 0))
        def _():
            acc[...] = jnp.zeros_like(acc)

        h = h_ref[...]
        for s_ref, (lo, hi) in zip(src, spans):
            @pl.when((j >= lo) & (j < hi))
            def _(s_ref=s_ref):
                four = jnp.concatenate([s_ref[k] for k in range(gs)], axis=-1)
                acc[j] += jnp.dot(h, four, preferred_element_type=F32)

        @pl.when((i == ni - 1) & (j == nj - 1))
        def _():
            done = pltpu.make_async_copy(acc, out_ref, sem)
            done.start()
            done.wait()

    return _pcall(
        body, name=name, grid=(ni, nj),
        in_specs=[pl.BlockSpec((D_MODEL, tm), lambda i, j: (0, i))] + src_specs,
        out_specs=pl.BlockSpec(memory_space=pl.ANY),
        out_shape=jax.ShapeDtypeStruct((nj, D_MODEL, gs * DH), F32),
        scratch_shapes=[pltpu.VMEM((nj, D_MODEL, gs * DH), F32), pltpu.SemaphoreType.DMA],
        compiler_params=_cparams(("arbitrary", "arbitrary")),
    )(jnp.transpose(hb), *sources)


def _adamw_math(w, g, m, v):
    c1 = 1.0 - ADAM_B1 ** ADAM_STEP
    c2 = 1.0 - ADAM_B2 ** ADAM_STEP
    nm = ADAM_B1 * m + (1.0 - ADAM_B1) * g
    nv = ADAM_B2 * v + (1.0 - ADAM_B2) * (g * g)
    return -ADAM_LR * ((nm / c1) / (jnp.sqrt(nv / c2) + ADAM_EPS) + ADAM_WD * w), nm, nv


def _adamw(w, g, m, v, name):
    r, c = w.shape
    tr = 256 if r % 256 == 0 else r

    def body(w_ref, g_ref, m_ref, v_ref, d_ref, nm_ref, nv_ref):
        d_ref[...], nm_ref[...], nv_ref[...] = _adamw_math(w_ref[...], g_ref[...], m_ref[...], v_ref[...])

    spec = pl.BlockSpec((tr, c), lambda i: (i, 0))
    return _pcall(
        body, name=name, grid=(r // tr,),
        in_specs=[spec] * 4, out_specs=[spec] * 3,
        out_shape=[jax.ShapeDtypeStruct((r, c), F32)] * 3,
        compiler_params=_cparams(("parallel",)),
    )(w, g, m, v)


def _sum_adamw(parts, w, m, v, name):
    r, c = w.shape
    tr = 128 if r % 128 == 0 else r

    def body(p_ref, w_ref, m_ref, v_ref, g_ref, d_ref, nm_ref, nv_ref):
        g = p_ref[0].astype(F32)
        for k in range(1, N_DEV):
            g = g + p_ref[k].astype(F32)
        g_ref[...] = g
        d_ref[...], nm_ref[...], nv_ref[...] = _adamw_math(w_ref[...], g, m_ref[...], v_ref[...])

    spec = pl.BlockSpec((tr, c), lambda i: (i, 0))
    return _pcall(
        body, name=name, grid=(r // tr,),
        in_specs=[pl.BlockSpec((N_DEV, tr, c), lambda i: (0, i, 0))] + [spec] * 3, out_specs=[spec] * 4,
        out_shape=[jax.ShapeDtypeStruct((r, c), F32)] * 4,
        compiler_params=_cparams(("parallel",)),
    )(parts, w, m, v)


def _me():
    return lax.axis_index("x"), lax.axis_index("y"), lax.axis_index("c")


def _flat(x, y, c):
    return 4 * x + 2 * y + c


def _peer(k):
    x, y, c = _me()
    return (x ^ ((k >> 2) & 1), y ^ ((k >> 1) & 1), c ^ (k & 1))


def _gather_plan(blocks):
    def copies(x_refs, out_refs, sems):
        send_sems, recv_sems, local_sems = sems
        me = _flat(*_me())
        local = [pltpu.make_async_copy(x, o.at[me], local_sems.at[a]) for a, (x, o) in enumerate(zip(x_refs, out_refs))]
        outgoing, incoming = [], []
        for k in range(1, N_DEV):
            src = _flat(*_peer(k))
            for a, (x, o) in enumerate(zip(x_refs, out_refs)):
                for slot, group in ((me, outgoing), (src, incoming)):
                    group.append(pltpu.make_async_remote_copy(
                        src_ref=x, dst_ref=o.at[slot], send_sem=send_sems.at[a, k - 1], recv_sem=recv_sems.at[a, k - 1],
                        device_id=_peer(k), device_id_type=MESH))
        return local, outgoing, incoming

    return _Comm(list(blocks), [jax.ShapeDtypeStruct((N_DEV,) + b.shape, b.dtype) for b in blocks], copies)


def _exchange_plan(sends):
    def copies(s_refs, out_refs, sems):
        send_sems, recv_sems, local_sems = sems
        me = _flat(*_me())
        local = [pltpu.make_async_copy(s.at[me], o.at[0], local_sems.at[i]) for i, (s, o) in enumerate(zip(s_refs, out_refs))]
        outgoing = []
        for k in range(1, N_DEV):
            to = _flat(*_peer(k))
            for i, (s, o) in enumerate(zip(s_refs, out_refs)):
                outgoing.append(pltpu.make_async_remote_copy(
                    src_ref=s.at[to], dst_ref=o.at[k], send_sem=send_sems.at[i, k - 1], recv_sem=recv_sems.at[i, k - 1],
                    device_id=_peer(k), device_id_type=MESH))
        return local, outgoing, outgoing

    return _Comm(list(sends), [jax.ShapeDtypeStruct(s.shape, s.dtype) for s in sends], copies)


def _comm_scratch(plan):
    n = len(plan.operands)
    return [pltpu.SemaphoreType.DMA((n, N_DEV - 1)), pltpu.SemaphoreType.DMA((n, N_DEV - 1)),
            pltpu.SemaphoreType.DMA((n,))]


def _comm_start(copies):
    local, outgoing, _ = copies
    for cp in local + outgoing:
        cp.start()


def _comm_wait(copies):
    local, outgoing, incoming = copies
    for cp in incoming:
        cp.wait_recv()
    for cp in outgoing:
        cp.wait_send()
    for cp in local:
        cp.wait()


def _run_comm(plan, name):
    n = len(plan.operands)

    def body(*refs):
        copies = plan.copies(refs[:n], refs[n:2 * n], refs[2 * n:])
        _comm_start(copies)
        _comm_wait(copies)

    return _pcall(
        body, name=name,
        in_specs=[pl.BlockSpec(memory_space=pl.ANY)] * n,
        out_specs=[pl.BlockSpec(memory_space=pl.ANY)] * n,
        out_shape=plan.out_shapes,
        scratch_shapes=_comm_scratch(plan),
    )(*plan.operands)


def _all_gather_two_level(blocks, name):
    na = len(blocks)

    def body(*refs):
        x_refs, out_refs = refs[:na], refs[na:2 * na]
        send_sems, recv_sems, local_sems = refs[2 * na:]
        x, y, c = _me()
        me, sibling = (x, y, c), (x, y, 1 - c)
        chips = [(1 - x, y), (x, 1 - y), (1 - x, 1 - y)]

        def copy(a, k, block, to, own=False):
            slot = out_refs[a].at[_flat(*block)]
            return pltpu.make_async_remote_copy(
                src_ref=x_refs[a] if own else slot, dst_ref=slot, send_sem=send_sems.at[a, k],
                recv_sem=recv_sems.at[a, k], device_id=to, device_id_type=MESH)

        mine = [pltpu.make_async_copy(x_refs[a], out_refs[a].at[_flat(*me)], local_sems.at[a]) for a in range(na)]
        first = [copy(a, 0, me, sibling, own=True) for a in range(na)]
        first += [copy(a, 1 + j, me, (*chip, c), own=True) for j, chip in enumerate(chips) for a in range(na)]
        for cp in mine + first:
            cp.start()
        passed = []
        for j, chip in enumerate(chips):
            for a in range(na):
                copy(a, 1 + j, (*chip, c), me).wait_recv()
                cp = copy(a, 4 + j, (*chip, c), sibling)
                cp.start()
                passed.append(cp)
        for a in range(na):
            copy(a, 0, sibling, me).wait_recv()
        for j, chip in enumerate(chips):
            for a in range(na):
                copy(a, 4 + j, (*chip, 1 - c), me).wait_recv()
        for cp in first + passed:
            cp.wait_send()
        for cp in mine:
            cp.wait()

    return _pcall(
        body, name=name,
        in_specs=[pl.BlockSpec(memory_space=pl.ANY)] * na,
        out_specs=[pl.BlockSpec(memory_space=pl.ANY)] * na,
        out_shape=[jax.ShapeDtypeStruct((N_DEV,) + b.shape, b.dtype) for b in blocks],
        scratch_shapes=[pltpu.SemaphoreType.DMA((na, N_DEV - 1)), pltpu.SemaphoreType.DMA((na, N_DEV - 1)),
                        pltpu.SemaphoreType.DMA((na,))],
    )(*blocks)


def _sum_slots(a, name):
    r = a.shape[1]

    def body(a_ref, o_ref):
        acc = a_ref[0]
        for d in range(1, N_DEV):
            acc = acc + a_ref[d]
        o_ref[...] = acc

    return _pcall(body, name=name, out_shape=jax.ShapeDtypeStruct((r, 128), F32), compiler_params=_cparams())(a)


def _all_reduce_small(blk, name):
    r = blk.shape[0]

    def body(x_ref, out_ref, gath, send_sems, recv_sems):
        me = _flat(*_me())
        gath[me] = x_ref[...]
        copies = []
        for k in range(1, N_DEV):
            cp = pltpu.make_async_remote_copy(
                src_ref=x_ref, dst_ref=gath.at[me],
                send_sem=send_sems.at[k - 1], recv_sem=recv_sems.at[k - 1],
                device_id=_peer(k), device_id_type=MESH)
            cp.start()
            copies.append(cp)
        for k in range(1, N_DEV):
            src = _flat(*_peer(k))
            pltpu.make_async_remote_copy(
                src_ref=x_ref, dst_ref=gath.at[src],
                send_sem=send_sems.at[k - 1], recv_sem=recv_sems.at[k - 1],
                device_id=_peer(k), device_id_type=MESH).wait_recv()
        for cp in copies:
            cp.wait_send()
        acc = gath[0]
        for d in range(1, N_DEV):
            acc = acc + gath[d]
        out_ref[...] = acc

    return _pcall(
        body, name=name,
        in_specs=[pl.BlockSpec(memory_space=pltpu.VMEM)],
        out_specs=pl.BlockSpec(memory_space=pltpu.VMEM),
        out_shape=jax.ShapeDtypeStruct((r, 128), F32),
        scratch_shapes=[pltpu.VMEM((N_DEV, r, 128), F32),
                        pltpu.SemaphoreType.DMA((N_DEV - 1,)), pltpu.SemaphoreType.DMA((N_DEV - 1,))],
    )(blk)


def _heads(vec):
    return vec.reshape(NH, 1, DH)


def _rep(vec4):
    return jnp.broadcast_to(vec4.reshape(NH, 1, 1), (NH, 1, DH))


def _onehot_lane(offset):
    m = np.zeros((NH, 1, DH), np.float32)
    for h in range(NH):
        m[h, 0, offset + h] = 1.0
    return jnp.asarray(m)


_TINY = (("gdn_conv_w", (DEPTH, 4, 96)), ("rwkv_w_up", (DEPTH, 64, 32)), ("rwkv_a_up", (DEPTH, 64, 32)),
         ("sc_conv_w", (DEPTH, 3, 32)))
_TINY_ROWS = -(-sum(int(np.prod(s)) for _, s in _TINY) // 1024) * 8


def _pack_rows(arrays, rows, fill=0.0):
    flat = jnp.concatenate([a.reshape(-1) for a in arrays])
    return jnp.pad(flat, (0, rows * 128 - flat.shape[0]), constant_values=fill).reshape(rows, 128)


def _unpack_rows(p, named_shapes):
    lead = p.shape[:-2]
    flat = p.reshape(lead + (-1,))
    out, o = {}, 0
    for n, s in named_shapes:
        size = int(np.prod(s))
        out[n] = flat[..., o:o + size].reshape(lead + tuple(s))
        o += size
    return out


def _gather_last(a):
    return jnp.transpose(a, (1, 0, 2)).reshape(a.shape[1], -1)


def _split_last(a):
    r, c8 = a.shape
    return jnp.transpose(a.reshape(r, N_DEV, c8 // N_DEV), (1, 0, 2))


_SMALL = (("pre_norm_w", (DEPTH, 1024)), ("gdn_a_log", (DEPTH, 4)), ("gdn_dt_bias", (DEPTH, 4)),
          ("gdn_norm_w", (DEPTH, 64)), ("rwkv_mu", (DEPTH, 1152)), ("rwkv_w0", (DEPTH, 256)),
          ("rwkv_a0", (DEPTH, 256)), ("rwkv_k_k", (DEPTH, 256)), ("rwkv_k_a", (DEPTH, 256)),
          ("rwkv_r_k", (DEPTH, 256)), ("rwkv_ln_w", (DEPTH, 256)), ("rwkv_ln_b", (DEPTH, 256)),
          ("gla_a_up", (DEPTH, 16, 128)), ("gla_a_bias", (DEPTH, 128)), ("gla_norm_w", (DEPTH, 64)),
          ("post_norm_w", (DEPTH, 1024)), ("loss", ()))
_SMALL_ROWS = -(-sum(int(np.prod(s)) for _, s in _SMALL) // 1024) * 8


def _big_weights(w_in_all, w_out_all, l):
    return dict(w_g=_regroup_in(w_in_all, f"regroup_in{l}"),
                wout_g=w_out_all.reshape(4 * NH, DH, D_MODEL).astype(BF16))


def _layer_params(wts, tiny, l):
    conv = _gather_last(tiny["gdn_conv_w"][:, l])
    q = {}
    q["gdn_conv"] = jnp.transpose(conv.reshape(GDN_TAPS, 12, DH), (1, 0, 2))
    q["gdn_prm"] = [_rep(wts["gdn_a_log"][l]), _rep(wts["gdn_dt_bias"][l]),
                    jnp.broadcast_to(wts["gdn_norm_w"][l].reshape(1, 1, DH), (NH, 1, DH))]
    q["gdn_cst"] = [_onehot_lane(0), _onehot_lane(NH)]
    q["rwkv_mu"] = wts["rwkv_mu"][l].reshape(18, 1, DH)
    w_up = jnp.transpose(_gather_last(tiny["rwkv_w_up"][:, l]).reshape(64, NH, DH), (1, 0, 2))
    a_up = jnp.transpose(_gather_last(tiny["rwkv_a_up"][:, l]).reshape(64, NH, DH), (1, 0, 2))
    q["rwkv_prm"] = [_heads(wts["rwkv_w0"][l]), w_up, _heads(wts["rwkv_a0"][l]), a_up,
                     _heads(wts["rwkv_k_k"][l]), _heads(wts["rwkv_k_a"][l]), _heads(wts["rwkv_r_k"][l]),
                     _heads(wts["rwkv_ln_w"][l]), _heads(wts["rwkv_ln_b"][l])]
    sc = _gather_last(tiny["sc_conv_w"][:, l])
    q["sc_conv"] = jnp.transpose(sc.reshape(SC_TAPS, NH, DH), (1, 0, 2))
    gla_up = jnp.transpose(wts["gla_a_up"][l].reshape(16, NH, GLA_HEAD_K), (1, 0, 2))
    gla_up = jnp.pad(gla_up, ((0, 0), (0, DH - 16), (0, DH - GLA_HEAD_K)))
    gla_b = jnp.pad(wts["gla_a_bias"][l].reshape(NH, 1, GLA_HEAD_K), ((0, 0), (0, 0), (0, DH - GLA_HEAD_K)))
    q["gla_prm"] = [gla_up, gla_b, jnp.broadcast_to(wts["gla_norm_w"][l].reshape(1, 1, DH), (NH, 1, DH))]
    q["pre_w"] = wts["pre_norm_w"][l].reshape(1, D_MODEL)
    q["post_w"] = wts["post_norm_w"][l].reshape(1, D_MODEL)
    return q


def _mixer_inputs(p, cq, pm):
    gdn = [(cq, 4, 0), (cq, 4, 1), (cq, 4, 2), (p, 4, G_GDN // 4 + 3), (p, 1, G_GDN_AB)]
    rwkv = [(pm, 4, 0), (pm, 4, 1), (pm, 4, 2), (pm, 4, 3), (pm, 1, 16), (pm, 1, 17)]
    gla = [(p, 4, G_GLA // 4 + k) for k in range(4)] + [(p, 1, G_GLA_AD)]
    return gdn, rwkv, gla


def _layer_fwd(x, q, nb, t, l, side=None):
    hb, p = _norm_proj(x, q["pre_w"], q["w_g"], f"norm_proj{l}")
    cq = _conv_fwd(p, G_GDN, 12, q["gdn_conv"], nb, t, f"gdn_conv{l}")
    pm = _mix_fwd(p, q["rwkv_mu"], nb, t, f"rwkv_mix{l}")
    gdn_in, rwkv_in, gla_in = _mixer_inputs(p, cq, pm)
    y_gdn, ck_gdn, _ = _mixer_fwd(_gdn_chunk, f"gdn_fwd{l}", gdn_in, q["gdn_prm"], q["gdn_cst"], nb, t, n_kept=1)
    y_rwkv, ck_rwkv, side_res = _mixer_fwd(_rwkv_chunk, f"rwkv_fwd{l}", rwkv_in, q["rwkv_prm"], [], nb, t,
                                           first_fn=_rwkv_first_chunk, side=side, n_kept=1)
    y_sc = _sc_fwd(p, q["sc_conv"], nb, t, f"sc_fwd{l}")
    y_gla, ck_gla, _ = _mixer_fwd(_gla_chunk, f"gla_fwd{l}", gla_in, q["gla_prm"], [], nb, t)
    ys = (y_gdn, y_rwkv, y_sc, y_gla)
    out, xn = _out_proj_norm(ys, q["wout_g"], x, q["post_w"], f"out_proj{l}")
    saved = dict(x=x, hb=hb, p=p, cq=cq, pm=pm, ys=ys, out=out, ck=(ck_gdn, ck_rwkv, ck_gla))
    return xn, saved, side_res


def _layer_bwd(dxn, q, sv, nb, t, l, side=None, exchange_own=False):
    p, cq, pm, ys = sv["p"], sv["cq"], sv["pm"], sv["ys"]
    dout, dy, d_post = _post_bwd(dxn, sv["out"], q["post_w"], q["wout_g"], f"post_bwd{l}")
    d_wout = _dwout(ys, dout, f"dwout{l}").reshape(N_DEV, 128, D_MODEL).astype(BF16)
    gdn_in, rwkv_in, gla_in = _mixer_inputs(p, cq, pm)
    ck_gdn, ck_rwkv, ck_gla = sv["ck"]
    g = {}

    (d_conv, dz, dab), (da_log, ddt, dnw), _ = _mixer_bwd(
        _gdn_chunk, f"gdn_bwd{l}", gdn_in, q["gdn_prm"], q["gdn_cst"], ck_gdn, dy, 0,
        [(12, F32), (4, BF16), (1, BF16)], [(0, 0), (0, 4), (0, 8), (1, 0), (2, 0)], nb, t)
    dconv_in, d_gconv = _conv_bwd(p, G_GDN, 12, q["gdn_conv"], d_conv, nb, t, f"gdn_conv_bwd{l}")
    g["gdn_conv_w"] = jnp.transpose(d_gconv.sum(1), (1, 0, 2)).reshape(GDN_TAPS, 768)
    g["gdn_a_log"] = da_log.sum((0, 2, 3))
    g["gdn_dt_bias"] = ddt.sum((0, 2, 3))
    g["gdn_norm_w"] = dnw.sum((0, 1, 2))

    (d_pm,), d_rprm, side_res = _mixer_bwd(
        _rwkv_chunk, f"rwkv_bwd{l}", rwkv_in, q["rwkv_prm"], [], ck_rwkv, dy, 1,
        [(18, F32)], [(0, 0), (0, 4), (0, 8), (0, 12), (0, 16), (0, 17)], nb, t, first_fn=_rwkv_first_chunk,
        side=side)
    dp_rwkv, d_mu = _mix_bwd(p, q["rwkv_mu"], d_pm, nb, t, f"rwkv_mix_bwd{l}")
    g["rwkv_mu"] = d_mu.sum(1).reshape(1152)
    rp = [a.sum(0) for a in d_rprm]
    g["rwkv_w0"] = rp[0].reshape(256)
    g["rwkv_w_up"] = jnp.transpose(rp[1], (1, 0, 2)).reshape(64, 256)
    g["rwkv_a0"] = rp[2].reshape(256)
    g["rwkv_a_up"] = jnp.transpose(rp[3], (1, 0, 2)).reshape(64, 256)
    for i, nme in enumerate(("rwkv_k_k", "rwkv_k_a", "rwkv_r_k", "rwkv_ln_w", "rwkv_ln_b")):
        g[nme] = rp[4 + i].reshape(256)

    dp_sc, d_scw = _sc_bwd(p, q["sc_conv"], dy, nb, t, f"sc_bwd{l}")
    g["sc_conv_w"] = jnp.transpose(d_scw.sum(1), (1, 0, 2)).reshape(SC_TAPS, 256)

    (dp_gla, dad), (d_aup, d_ab, d_gnw), _ = _mixer_bwd(
        _gla_chunk, f"gla_bwd{l}", gla_in, q["gla_prm"], [], ck_gla, dy, 3,
        [(16, BF16), (1, BF16)], [(0, 0), (0, 4), (0, 8), (0, 12), (1, 0)], nb, t)
    g["gla_a_up"] = jnp.transpose(d_aup.sum(0)[:, :16, :GLA_HEAD_K], (1, 0, 2)).reshape(16, 128)
    g["gla_a_bias"] = d_ab.sum(0)[:, 0, :GLA_HEAD_K].reshape(128)
    g["gla_norm_w"] = d_gnw.sum((0, 1, 2))

    singles = jnp.concatenate([dab, dp_rwkv[16:18], dad], axis=0)
    sources = [dconv_in, dz, dp_rwkv, dp_sc, dp_gla, singles]
    d_win = _regroup_out(_dwin(sv["hb"], sources, f"dwin{l}"), f"regroup_out{l}")
    own = _exchange_plan([d_win, d_wout]) if exchange_own else None
    dx, d_pre, got = _dh_prenorm_bwd(sources, q["w_g"], sv["x"], q["pre_w"], dxn, f"dh_bwd{l}", own)
    if exchange_own:
        d_win, d_wout = got
    g["pre_norm_w"] = d_pre.reshape(D_MODEL)
    g["post_norm_w"] = d_post.reshape(D_MODEL)
    return dx, g, d_win, d_wout, side_res


def _local_step(x, tgt, wts, tiny, w_in_all, w_out_all, later_shards=None):
    nb, t, d = x.shape
    xf = x.reshape(nb * t, d)
    overlap = later_shards is not None
    qs, saved = [], []
    big = _big_weights(w_in_all[0], w_out_all[0], 0)
    for l in range(DEPTH):
        q = dict(_layer_params(wts, tiny, l), **big)
        nxt = l + 1 < DEPTH
        side = _gather_plan(later_shards[l]) if overlap and nxt else None
        xf, sv, got = _layer_fwd(xf, q, nb, t, l, side)
        if nxt:
            big = _big_weights(*(got if overlap else (w_in_all[l + 1], w_out_all[l + 1])), l + 1)
        qs.append(q)
        saved.append(sv)
    dxf, lpart = _loss_grad(xf, tgt.reshape(nb * t, d), "loss")
    grads, d_win, d_wout = [None] * DEPTH, [None] * DEPTH, [None] * DEPTH
    for l in reversed(range(DEPTH)):
        side = _exchange_plan([d_win[l + 1], d_wout[l + 1]]) if overlap and l + 1 < DEPTH else None
        dxf, grads[l], d_win[l], d_wout[l], got = _layer_bwd(dxf, qs[l], saved[l], nb, t, l, side,
                                                             exchange_own=overlap and l == 0)
        if side is not None:
            d_win[l + 1], d_wout[l + 1] = got
    small = {k: jnp.stack([grads[l][k] for l in range(DEPTH)]) for k in grads[0]}
    return lpart[0, 0], dxf.reshape(nb, t, d), small, d_win, d_wout


_WEIGHTS = ("pre_norm_w", "w_in", "gdn_conv_w", "gdn_a_log", "gdn_dt_bias", "gdn_norm_w", "rwkv_mu", "rwkv_w0",
            "rwkv_w_up", "rwkv_a0", "rwkv_a_up", "rwkv_k_k", "rwkv_k_a", "rwkv_r_k", "rwkv_ln_w", "rwkv_ln_b",
            "sc_conv_w", "gla_a_up", "gla_a_bias", "gla_norm_w", "w_out", "post_norm_w")


def kernel(x, pre_norm_w, w_in, gdn_conv_w, gdn_a_log, gdn_dt_bias, gdn_norm_w, rwkv_mu, rwkv_w0, rwkv_w_up, rwkv_a0, rwkv_a_up, rwkv_k_k, rwkv_k_a, rwkv_r_k, rwkv_ln_w, rwkv_ln_b, sc_conv_w, gla_a_up, gla_a_bias, gla_norm_w, w_out, post_norm_w, loss_target, m_pre_norm_w, m_w_in, m_gdn_conv_w, m_gdn_a_log, m_gdn_dt_bias, m_gdn_norm_w, m_rwkv_mu, m_rwkv_w0, m_rwkv_w_up, m_rwkv_a0, m_rwkv_a_up, m_rwkv_k_k, m_rwkv_k_a, m_rwkv_r_k, m_rwkv_ln_w, m_rwkv_ln_b, m_sc_conv_w, m_gla_a_up, m_gla_a_bias, m_gla_norm_w, m_w_out, m_post_norm_w, v_pre_norm_w, v_w_in, v_gdn_conv_w, v_gdn_a_log, v_gdn_dt_bias, v_gdn_norm_w, v_rwkv_mu, v_rwkv_w0, v_rwkv_w_up, v_rwkv_a0, v_rwkv_a_up, v_rwkv_k_k, v_rwkv_k_a, v_rwkv_r_k, v_rwkv_ln_w, v_rwkv_ln_b, v_sc_conv_w, v_gla_a_up, v_gla_a_bias, v_gla_norm_w, v_w_out, v_post_norm_w):
    env = dict(locals())
    w = {n: env[n] for n in _WEIGHTS}
    m = {n: env["m_" + n] for n in _WEIGHTS}
    v = {n: env["v_" + n] for n in _WEIGHTS}
    tiny_names = [n for n, _ in _TINY]

    w_in_b, w_out_b = w_in.astype(BF16), w_out.astype(BF16)
    w_in_0, w_out_0, tiny_all = _all_gather_two_level(
        [w_in_b[0], w_out_b[0], _pack_rows([w[n] for n in tiny_names], _TINY_ROWS)], "gather_weights")
    tiny = _unpack_rows(tiny_all, _TINY)

    lpart, grad_x, small, r_win, r_wout = _local_step(
        x, loss_target, w, tiny, [w_in_0], [w_out_0], later_shards=[(w_in_b[l], w_out_b[l]) for l in range(1, DEPTH)])

    tiny_send = jnp.stack([_pack_rows([_split_last(small[n][l])[d] for n in tiny_names for l in range(DEPTH)],
                                      _TINY_ROWS) for d in range(N_DEV)])
    (r_tiny,) = _run_comm(_exchange_plan([tiny_send]), "scatter_grads")
    grads, delta, new_m, new_v = {}, {}, {}, {}
    for n, parts in (("w_in", r_win), ("w_out", r_wout)):
        res = [_sum_adamw(parts[l], w[n][l], m[n][l], v[n][l], f"adamw_{n}{l}") for l in range(DEPTH)]
        grads[n], delta[n], new_m[n], new_v[n] = [jnp.stack(o) for o in zip(*res)]
    tiny_sum = _sum_slots(r_tiny, "sum_tiny").reshape(-1)
    o = 0
    for n, s in _TINY:
        size = int(np.prod(s))
        grads[n] = tiny_sum[o:o + size].reshape(s)
        o += size

    small = dict(small)
    small["loss"] = lpart
    red = _unpack_rows(_all_reduce_small(_pack_rows([small[n] for n, _ in _SMALL], _SMALL_ROWS), "reduce_small"),
                       _SMALL)
    loss = red.pop("loss")
    grads.update(red)

    rest = [n for n in _WEIGHTS if n not in ("w_in", "w_out")]
    rest_shapes = [(n, w[n].shape) for n in rest]
    rows = -(-sum(int(np.prod(s)) for _, s in rest_shapes) // 1024) * 8
    outs = _adamw(_pack_rows([w[n] for n in rest], rows), _pack_rows([grads[n] for n in rest], rows),
                  _pack_rows([m[n] for n in rest], rows), _pack_rows([v[n] for n in rest], rows, 1.0), "adamw_rest")
    for dst, packed in zip((delta, new_m, new_v), outs):
        dst.update(_unpack_rows(packed, rest_shapes))

    return (loss, grad_x, *[grads[n] for n in _WEIGHTS], *[delta[n] for n in _WEIGHTS],
            *[new_m[n] for n in _WEIGHTS], *[new_v[n] for n in _WEIGHTS])
```

```python
import collections
import functools
import math

import numpy as np
import jax
import jax.numpy as jnp
from jax import lax
from jax.experimental import pallas as pl
from jax.experimental.pallas import tpu as pltpu

F32 = jnp.float32
BF16 = jnp.bfloat16

D_MODEL = 1024
DEPTH = 2
NH = 4
DH = 64
CH = 64
EPS = 1e-6
RWKV_GN_EPS = 64e-5
GLA_HEAD_K = 32
GLA_TAU = 16.0
GDN_TAPS = 4
SC_TAPS = 3
D_IN = 3992
N_DEV = 8
SHARD_COLS = D_IN // N_DEV

G_GDN = 0
G_RWKV = 16
G_SC = 32
G_GLA = 48
G_GDN_AB, G_RWKV_WD, G_RWKV_AD, G_GLA_AD = 64, 65, 66, 67
N_GROUPS = 68
GROUPS_PER_STEP = 4
TIME_BLOCK = 256
RWKV_EXACT_STEPS = 16

C_GDN, C_RWKV, C_SC, C_GLA = 0, 1032, 2184, 3208

ADAM_LR, ADAM_B1, ADAM_B2, ADAM_EPS, ADAM_WD, ADAM_STEP = 0.001, 0.9, 0.999, 1e-08, 0.01, 10

VMEM_LIMIT = 56 * 1024 * 1024
MESH = pl.DeviceIdType.MESH

_pcall = pl.pallas_call

_Comm = collections.namedtuple("_Comm", "operands out_shapes copies")


def _cparams(sem=None):
    if sem is None:
        return pltpu.CompilerParams(vmem_limit_bytes=VMEM_LIMIT)
    return pltpu.CompilerParams(dimension_semantics=sem, vmem_limit_bytes=VMEM_LIMIT)


def _group_segments():
    table = [(G_GDN + i, C_GDN + DH * i, DH) for i in range(16)]
    table.append((G_GDN_AB, C_GDN + 1024, 8))
    table += [(G_RWKV + i, C_RWKV + DH * i, DH) for i in range(16)]
    table += [(G_RWKV_WD, C_RWKV + 1024, DH), (G_RWKV_AD, C_RWKV + 1088, DH)]
    table += [(G_SC + 4 * j + k, C_SC + 256 * k + DH * j, DH) for j in range(NH) for k in range(4)]
    for h in range(NH):
        table += [(G_GLA + h, C_GLA + GLA_HEAD_K * h, GLA_HEAD_K),
                  (G_GLA + 4 + h, C_GLA + 128 + GLA_HEAD_K * h, GLA_HEAD_K),
                  (G_GLA + 8 + h, C_GLA + 256 + DH * h, DH),
                  (G_GLA + 12 + h, C_GLA + 512 + DH * h, DH)]
    table.append((G_GLA_AD, C_GLA + 768, 16))
    segs, padded = [], []
    for g, c, n in table:
        if n < DH:
            padded.append(g)
        a = 0
        while n > 0:
            d, off = divmod(c, SHARD_COLS)
            ln = min(n, SHARD_COLS - off)
            segs.append((g, a, d, off, ln))
            c, a, n = c + ln, a + ln, n - ln
    return segs, padded


_SEGMENTS, _PADDED_GROUPS = _group_segments()


def _dn(ta, tb):
    return (((1 if ta else 2,), (2 if tb else 1,)), ((0,), (0,)))


def _hdot(a, b, ta=False, tb=False):
    return lax.dot_general(a, b, _dn(ta, tb), precision=lax.Precision.HIGH, preferred_element_type=F32)


def _r(x):
    return x.astype(BF16)


def _rdot(a, b, ta=False, tb=False):
    return lax.dot_general(_r(a), _r(b), _dn(ta, tb), preferred_element_type=F32)


@jax.custom_vjp
def _bmm(a, b):
    return _rdot(a, b)


def _bmm_fwd(a, b):
    return _rdot(a, b), (a, b)


def _bmm_bwd(res, g):
    a, b = res
    return _rdot(g, b, tb=True), _rdot(a, g, ta=True)


_bmm.defvjp(_bmm_fwd, _bmm_bwd)


@jax.custom_vjp
def _bmm_nt(a, b):
    return _rdot(a, b, tb=True)


def _bmm_nt_fwd(a, b):
    return _rdot(a, b, tb=True), (a, b)


def _bmm_nt_bwd(res, g):
    a, b = res
    return _rdot(g, b), _rdot(g, a, ta=True)


_bmm_nt.defvjp(_bmm_nt_fwd, _bmm_nt_bwd)


@jax.custom_vjp
def _bmm_tn(a, b):
    return _rdot(a, b, ta=True)


def _bmm_tn_fwd(a, b):
    return _rdot(a, b, ta=True), (a, b)


def _bmm_tn_bwd(res, g):
    a, b = res
    return _rdot(b, g, tb=True), _rdot(a, g)


_bmm_tn.defvjp(_bmm_tn_fwd, _bmm_tn_bwd)


def _tri(n):
    i = lax.broadcasted_iota(jnp.int32, (n, n), 0)
    j = lax.broadcasted_iota(jnp.int32, (n, n), 1)
    return i >= j, i > j, i == j


def _heads_of(x, like):
    n = like.shape[0]
    if x.ndim == 2:
        return jnp.broadcast_to(x[None], (n,) + x.shape)
    seqs = x.shape[0]
    return jnp.broadcast_to(x[:, None], (seqs, n // seqs) + x.shape[1:]).reshape((n,) + x.shape[1:])


def _cumsum_rows(x):
    incl, _, _ = _tri(x.shape[-2])
    return _hdot(_heads_of(incl.astype(F32), x), x)


@jax.custom_vjp
def _inv_unit_lower(a):
    n = a.shape[-1]
    _, _, eye = _tri(n)
    pw = -a
    inv = eye.astype(F32) + pw
    for _ in range(math.ceil(math.log2(n)) - 1):
        pw = _hdot(pw, pw)
        inv = inv + _hdot(inv, pw)
    return inv


def _inv_unit_lower_fwd(a):
    inv = _inv_unit_lower(a)
    return inv, inv


def _inv_unit_lower_bwd(inv, g):
    return (-_hdot(_hdot(inv, g, ta=True), inv, tb=True),)


_inv_unit_lower.defvjp(_inv_unit_lower_fwd, _inv_unit_lower_bwd)


@jax.custom_vjp
def _inv_reuse(a, inv):
    return inv


def _inv_reuse_fwd(a, inv):
    return inv, inv


def _inv_reuse_bwd(inv, g):
    return _inv_unit_lower_bwd(inv, g)[0], jnp.zeros_like(inv)


_inv_reuse.defvjp(_inv_reuse_fwd, _inv_reuse_bwd)


def _silu(x):
    return x * jax.nn.sigmoid(x)


def _t(x):
    return jnp.swapaxes(x, -1, -2)


def _gdn_chunk(prm, cst, ins, s, tinv=None):
    a_log, dt_b, nw = prm
    m_a, m_b = cst
    cq, ck, cv, z, ab = ins
    ab = _heads_of(ab, m_a)
    incl, strict, _ = _tri(CH)
    q = _silu(cq)
    k = _silu(ck)
    v = _silu(cv)
    q = q * lax.rsqrt(jnp.sum(q * q, -1, keepdims=True) + EPS) * (DH ** -0.5)
    k = k * lax.rsqrt(jnp.sum(k * k, -1, keepdims=True) + EPS)
    a_raw = jnp.sum(ab * m_a, -1, keepdims=True)
    b_raw = jnp.sum(ab * m_b, -1, keepdims=True)
    gstep = -jnp.exp(a_log) * jax.nn.softplus(a_raw + dt_b)
    beta = jax.nn.sigmoid(b_raw)
    gc = _cumsum_rows(gstep)
    gl = jnp.sum(gstep, -2, keepdims=True)
    dec = jnp.where(incl, jnp.exp(jnp.where(incl, gc - _t(gc), 0.0)), 0.0)
    kb = k * beta
    a_mat = jnp.where(strict, _bmm_nt(kb, k) * dec, 0.0)
    tinv = _inv_unit_lower(a_mat) if tinv is None else _inv_reuse(a_mat, tinv)
    eg = jnp.exp(gc)
    u = _hdot(tinv, v * beta)
    w = _hdot(tinv, kb * eg)
    attn = _bmm_nt(q, k) * dec
    v_new = u - _bmm(w, s)
    o = _bmm(q * eg, s) + _bmm(attn, v_new)
    s_next = s * jnp.exp(gl) + _bmm_tn(k * jnp.exp(gl - gc), v_new)
    on = o * lax.rsqrt(jnp.mean(o * o, -1, keepdims=True) + EPS) * nw
    return on * _silu(z), s_next, tinv


def _gla_chunk(prm, cst, ins, st):
    a_up, a_bias, nw = prm
    q, k, v, z, ad = ins
    incl, _, _ = _tri(CH)
    la = jax.nn.log_sigmoid(_bmm(_heads_of(ad, a_up), a_up) + a_bias) * (1.0 / GLA_TAU)
    bc = _cumsum_rows(la)
    bl = jnp.sum(la, -2, keepdims=True)
    qe = q * (GLA_HEAD_K ** -0.5) * jnp.exp(bc)
    ke = k * jnp.exp(-bc)
    attn = jnp.where(incl, _bmm_nt(qe, ke), 0.0)
    o = _bmm_nt(qe, st) + _bmm(attn, v)
    st_next = st * jnp.exp(bl) + _bmm_tn(v, k * jnp.exp(bl - bc))
    on = o * lax.rsqrt(jnp.mean(o * o, -1, keepdims=True) + EPS) * nw
    return on * _silu(z), st_next


def _rwkv_chunk(prm, cst, ins, s, inv=None):
    r, v = ins[0], ins[2]
    incl, strict, _ = _tri(r.shape[-2])
    lw, kk, k2, m = _rwkv_pre(prm, ins)
    cum = _cumsum_rows(lw)
    ltot = jnp.sum(lw, -2, keepdims=True)
    n_t = -kk * jnp.exp(cum - lw)
    einv = jnp.exp(-cum)
    m_t = m * einv
    k_t = k2 * einv
    r_t = r * jnp.exp(cum)
    a_nm = jnp.where(strict, _hdot(n_t, m_t, tb=True), 0.0)
    a_nk = jnp.where(strict, _hdot(n_t, k_t, tb=True), 0.0)
    inv = _inv_unit_lower(-a_nm) if inv is None else _inv_reuse(-a_nm, inv)
    cm = _hdot(inv, _hdot(n_t, s, tb=True) + _bmm(a_nk, v))
    y = (_bmm_nt(r_t, s) + _bmm(jnp.where(incl, _hdot(r_t, m_t, tb=True), 0.0), cm)
         + _bmm(jnp.where(incl, _hdot(r_t, k_t, tb=True), 0.0), v))
    eend = jnp.exp(ltot - cum)
    s_next = s * jnp.exp(ltot) + _bmm_tn(cm, m * eend) + _bmm_tn(v, k2 * eend)
    return _rwkv_post(prm, ins, y, k2), s_next, inv


def _rwkv_pre(prm, ins):
    w0, w_up, a0, a_up, k_k, k_a = prm[:6]
    k, wd, ad = ins[1], ins[4], ins[5]
    lw = -math.exp(-0.5) * jax.nn.sigmoid(w0 + _bmm(_heads_of(jnp.tanh(wd), w_up), w_up))
    a = jax.nn.sigmoid(a0 + _bmm(_heads_of(ad, a_up), a_up))
    kk = k * k_k
    kk = kk * lax.rsqrt(jnp.sum(kk * kk, -1, keepdims=True) + EPS)
    k2 = k * (1.0 + (a - 1.0) * k_a)
    return lw, kk, k2, kk * a


def _rwkv_post(prm, ins, y, k2):
    r_k, ln_w, ln_b = prm[6:]
    r, v, z = ins[0], ins[2], ins[3]
    mean = jnp.mean(y, -1, keepdims=True)
    yc = y - mean
    var = jnp.mean(yc * yc, -1, keepdims=True)
    yn = yc * lax.rsqrt(var + RWKV_GN_EPS) * ln_w + ln_b
    bonus = jnp.sum(r * k2 * r_k, -1, keepdims=True) * v
    return (yn + bonus) * _silu(z)


@jax.custom_vjp
def _bmv(s, x):
    return jnp.sum(_r(s).astype(F32) * _r(x).astype(F32), -1, keepdims=True)


def _bmv_fwd(s, x):
    return _bmv(s, x), (s, x)


def _bmv_bwd(res, g):
    s, x = res
    return g * x, jnp.sum(_r(s).astype(F32) * _r(g).astype(F32), -2, keepdims=True)


_bmv.defvjp(_bmv_fwd, _bmv_bwd)


def _rwkv_steps(prm, cst, ins, s, steps):
    r, v = ins[0], ins[2]
    lw, kk, k2, m = _rwkv_pre(prm, ins)
    w = jnp.exp(lw)
    v_t = _t(v)
    lane = lax.broadcasted_iota(jnp.int32, (1, 1, CH), 2)
    y_t = jnp.zeros((s.shape[0], DH, CH), F32)
    for t in range(steps):
        e_t = (lane == t).astype(F32)
        row = (slice(None), slice(t, t + 1))
        sa = _bmv(s, -kk[row])
        s = s * w[row] + sa * m[row] + jnp.sum(v_t * e_t, -1, keepdims=True) * k2[row]
        y_t = y_t + _bmv(s, r[row]) * e_t
    return _rwkv_post(prm, ins, _t(y_t), k2)[:, :steps], s


def _rwkv_first_chunk(prm, cst, ins, s):
    k = RWKV_EXACT_STEPS
    y_head, s = _rwkv_steps(prm, cst, ins, s, k)
    y_tail, s, _ = _rwkv_chunk(prm, cst, [x[..., k:, :] for x in ins], s)
    return jnp.concatenate([y_head, y_tail], axis=-2), s


def _time_block(t):
    return TIME_BLOCK if t % TIME_BLOCK == 0 else t


def _load_chunk(ref, i):
    nb = ref.shape[1]
    if ref.shape[0] == NH:
        return jnp.concatenate([ref[:, b, pl.ds(i, CH), :] for b in range(nb)], axis=0)
    return ref[0, :, pl.ds(i, CH), :]


def _mixer_fwd(chunk_fn, name, ins, prm, cst, nb, t, first_fn=None, side=None, n_kept=0):
    tb = _time_block(t)
    nt, ncb, nch = t // tb, tb // CH, nb * NH
    n_in, n_prm, n_cst = len(ins), len(prm), len(cst)
    n_main, n_side = n_in + n_prm + n_cst, len(side.operands) if side else 0

    def body(*refs):
        in_refs = refs[:n_in]
        prm_refs = refs[n_in:n_in + n_prm]
        cst_refs = refs[n_in + n_prm:n_main]
        side_in = refs[n_main:n_main + n_side]
        y_ref, ck_ref = refs[n_main + n_side:n_main + n_side + 2]
        side_out = refs[n_main + n_side + 2:n_main + 2 * n_side + 2]
        s_scr = refs[n_main + 2 * n_side + 2]
        sems = refs[n_main + 2 * n_side + 3:]
        step_t = pl.program_id(0)

        if side is not None:
            @pl.when(step_t == 0)
            def _():
                _comm_start(side.copies(side_in, side_out, sems))

        @pl.when(step_t == 0)
        def _():
            s_scr[...] = jnp.zeros_like(s_scr)

        def chunk(c, i, fn=chunk_fn):
            s = s_scr[...]
            y, s_next, *kept = fn([jnp.tile(r[...], (nb, 1, 1)) for r in prm_refs],
                                  [jnp.tile(r[...], (nb, 1, 1)) for r in cst_refs],
                                  [_load_chunk(r, i) for r in in_refs], s)
            kept += [jnp.zeros_like(s)] * (n_kept - len(kept))
            for e, a in enumerate([s] + kept):
                ck_ref[c, e] = a
            for b in range(nb):
                y_ref[:, b, pl.ds(i, CH), :] = y[b * NH:(b + 1) * NH].astype(BF16)
            s_scr[...] = s_next

        def step(c, carry):
            chunk(c, pl.multiple_of(c * CH, CH))
            return carry

        if first_fn is None:
            lax.fori_loop(0, ncb, step, 0)
        else:
            @pl.when(step_t == 0)
            def _():
                chunk(0, 0, first_fn)

            @pl.when(step_t != 0)
            def _():
                chunk(0, 0)

            lax.fori_loop(1, ncb, step, 0)

        if side is not None:
            @pl.when(step_t == nt - 1)
            def _():
                _comm_wait(side.copies(side_in, side_out, sems))

    hbm = pl.BlockSpec(memory_space=pl.ANY)
    in_specs = [pl.BlockSpec((ng, nb, tb, DH), (lambda j, bi=bi: (bi, 0, j, 0))) for _, ng, bi in ins]
    in_specs += [pl.BlockSpec(p.shape, lambda j: (0, 0, 0)) for p in list(prm) + list(cst)]
    y, ck, *side_res = _pcall(
        body, name=name, grid=(nt,),
        in_specs=in_specs + [hbm] * n_side,
        out_specs=[pl.BlockSpec((NH, nb, tb, DH), lambda j: (0, 0, j, 0)),
                   pl.BlockSpec((ncb, 1 + n_kept, nch, DH, DH), lambda j: (j, 0, 0, 0, 0))] + [hbm] * n_side,
        out_shape=[jax.ShapeDtypeStruct((NH, nb, t, DH), BF16),
                   jax.ShapeDtypeStruct((t // CH, 1 + n_kept, nch, DH, DH), F32)]
        + (side.out_shapes if side else []),
        scratch_shapes=[pltpu.VMEM((nch, DH, DH), F32)] + (_comm_scratch(side) if side else []),
        compiler_params=_cparams(("arbitrary",)),
    )(*[a.reshape(a.shape[0], nb, t, DH) for a, _, _ in ins], *prm, *cst, *(side.operands if side else []))
    return y.reshape(NH, nb * t, DH), ck, side_res


def _mixer_bwd(chunk_fn, name, ins, prm, cst, ck, dy, dy_block, outs, routes, nb, t, first_fn=None, side=None):
    tb = _time_block(t)
    nt, ncb, nch = t // tb, tb // CH, nb * NH
    n_in, n_prm, n_cst, n_out = len(ins), len(prm), len(cst), len(outs)
    n_main, n_side = n_in + n_prm + n_cst + 2, len(side.operands) if side else 0
    n_kept = ck.shape[1] - 1

    def body(*refs):
        in_refs = refs[:n_in]
        prm_refs = refs[n_in:n_in + n_prm]
        cst_refs = refs[n_in + n_prm:n_in + n_prm + n_cst]
        ck_ref, dy_ref = refs[n_main - 2:n_main]
        side_in = refs[n_main:n_main + n_side]
        rest = refs[n_main + n_side:]
        out_refs = rest[:n_out]
        dprm_refs = rest[n_out:n_out + n_prm]
        side_out = rest[n_out + n_prm:n_out + n_prm + n_side]
        ds_scr = rest[n_out + n_prm + n_side]
        sems = rest[n_out + n_prm + n_side + 1:]
        step_t = pl.program_id(0)

        if side is not None:
            @pl.when(step_t == 0)
            def _():
                _comm_start(side.copies(side_in, side_out, sems))

        @pl.when(step_t == 0)
        def _():
            ds_scr[...] = jnp.zeros_like(ds_scr)
            for r in dprm_refs:
                r[...] = jnp.zeros_like(r)

        def chunk(c, i, fn=chunk_fn):
            cst_v = [jnp.tile(r[...], (nb, 1, 1)) for r in cst_refs]
            kept = [ck_ref[c, 1 + e] for e in range(n_kept)] if fn is chunk_fn else []
            _, vjp = jax.vjp(lambda p, x, s: fn(p, cst_v, x, s, *kept)[:2],
                             [jnp.tile(r[...], (nb, 1, 1)) for r in prm_refs],
                             [_load_chunk(r, i) for r in in_refs], ck_ref[c, 0])
            dy_c = jnp.concatenate([dy_ref[:, b, pl.ds(i, CH), :] for b in range(nb)], axis=0)
            d_prm, d_ins, d_s = vjp((dy_c, ds_scr[...]))
            for (oi, g0), r, g in zip(routes, in_refs, d_ins):
                o_ref = out_refs[oi]
                if r.shape[0] == NH:
                    for b in range(nb):
                        o_ref[g0:g0 + NH, b, pl.ds(i, CH), :] = g[b * NH:(b + 1) * NH].astype(o_ref.dtype)
                else:
                    o_ref[g0, :, pl.ds(i, CH), :] = g.astype(o_ref.dtype)
            for r, g in zip(dprm_refs, d_prm):
                r[...] += g
            ds_scr[...] = d_s

        def step(j, carry):
            c = ncb - 1 - j
            chunk(c, pl.multiple_of(c * CH, CH))
            return carry

        lax.fori_loop(0, ncb - 1, step, 0)
        if first_fn is None:
            chunk(0, 0)
        else:
            @pl.when(step_t == nt - 1)
            def _():
                chunk(0, 0, first_fn)

            @pl.when(step_t != nt - 1)
            def _():
                chunk(0, 0)

        if side is not None:
            @pl.when(step_t == nt - 1)
            def _():
                _comm_wait(side.copies(side_in, side_out, sems))

    def back(j):
        return nt - 1 - j

    hbm = pl.BlockSpec(memory_space=pl.ANY)
    in_specs = [pl.BlockSpec((ng, nb, tb, DH), (lambda j, bi=bi: (bi, 0, back(j), 0))) for _, ng, bi in ins]
    in_specs += [pl.BlockSpec(p.shape, lambda j: (0, 0, 0)) for p in list(prm) + list(cst)]
    in_specs += [pl.BlockSpec((ncb, 1 + n_kept, nch, DH, DH), lambda j: (back(j), 0, 0, 0, 0)),
                 pl.BlockSpec((NH, nb, tb, DH), lambda j: (dy_block, 0, back(j), 0))]
    out_specs = [pl.BlockSpec((ng, nb, tb, DH), lambda j: (0, 0, back(j), 0)) for ng, _ in outs]
    out_specs += [pl.BlockSpec((nch,) + p.shape[1:], lambda j: (0, 0, 0)) for p in prm]
    out_shape = [jax.ShapeDtypeStruct((ng, nb, t, DH), dt) for ng, dt in outs]
    out_shape += [jax.ShapeDtypeStruct((nch,) + p.shape[1:], F32) for p in prm]
    res = _pcall(
        body, name=name, grid=(nt,),
        in_specs=in_specs + [hbm] * n_side, out_specs=out_specs + [hbm] * n_side,
        out_shape=out_shape + (side.out_shapes if side else []),
        scratch_shapes=[pltpu.VMEM((nch, DH, DH), F32)] + (_comm_scratch(side) if side else []),
        compiler_params=_cparams(("arbitrary",)),
    )(*[a.reshape(a.shape[0], nb, t, DH) for a, _, _ in ins], *prm, *cst, ck, dy.reshape(dy.shape[0], nb, t, DH),
      *(side.operands if side else []))
    d_outs = [o.reshape(o.shape[0], nb * t, DH) for o in res[:n_out]]
    d_prm = [g.reshape((nb,) + p.shape) for g, p in zip(res[n_out:n_out + n_prm], prm)]
    return d_outs, d_prm, res[n_out + n_prm:]


def _shift_down(x, s):
    if s == 0:
        return x
    row = lax.broadcasted_iota(jnp.int32, x.shape, 0)
    return jnp.where(row < s, 0.0, pltpu.roll(x, s, 0))


def _shift_up(x, s):
    if s == 0:
        return x
    t = x.shape[0]
    row = lax.broadcasted_iota(jnp.int32, x.shape, 0)
    return jnp.where(row >= t - s, 0.0, pltpu.roll(x, t - s, 0))


def _conv_fwd(p, g0, ng, w, nb, t, name):
    taps = w.shape[1]

    def body(x_ref, w_ref, y_ref):
        x = x_ref[...]
        acc = w_ref[taps - 1:taps, :] * x
        for i in range(taps - 1):
            acc = acc + w_ref[i:i + 1, :] * _shift_down(x, taps - 1 - i)
        y_ref[...] = acc

    return _pcall(
        body, name=name, grid=(ng, nb),
        in_specs=[pl.BlockSpec((None, t, DH), lambda g, b: (g0 + g, b, 0)),
                  pl.BlockSpec((None, taps, DH), lambda g, b: (g, 0, 0))],
        out_specs=pl.BlockSpec((None, t, DH), lambda g, b: (g, b, 0)),
        out_shape=jax.ShapeDtypeStruct((ng, nb * t, DH), F32),
        compiler_params=_cparams(("parallel", "parallel")),
    )(p, w)


def _conv_bwd(p, g0, ng, w, dy, nb, t, name):
    taps = w.shape[1]

    def body(x_ref, w_ref, dy_ref, dx_ref, dw_ref):
        x = x_ref[...]
        d = dy_ref[...]
        acc = w_ref[taps - 1:taps, :] * d
        dw_ref[taps - 1:taps, :] = jnp.sum(d * x, 0, keepdims=True)
        for i in range(taps - 1):
            s = taps - 1 - i
            acc = acc + w_ref[i:i + 1, :] * _shift_up(d, s)
            dw_ref[i:i + 1, :] = jnp.sum(d * _shift_down(x, s), 0, keepdims=True)
        dx_ref[...] = acc.astype(BF16)

    return _pcall(
        body, name=name, grid=(ng, nb),
        in_specs=[pl.BlockSpec((None, t, DH), lambda g, b: (g0 + g, b, 0)),
                  pl.BlockSpec((None, taps, DH), lambda g, b: (g, 0, 0)),
                  pl.BlockSpec((None, t, DH), lambda g, b: (g, b, 0))],
        out_specs=[pl.BlockSpec((None, t, DH), lambda g, b: (g, b, 0)),
                   pl.BlockSpec((None, None, taps, DH), lambda g, b: (g, b, 0, 0))],
        out_shape=[jax.ShapeDtypeStruct((ng, nb * t, DH), BF16),
                   jax.ShapeDtypeStruct((ng, nb, taps, DH), F32)],
        compiler_params=_cparams(("parallel", "parallel")),
    )(p, w, dy)


def _mix_group(g):
    return jnp.where(g < 16, G_RWKV + g, G_RWKV_WD + g - 16)


def _mix_fwd(p, mu, nb, t, name):
    def body(x_ref, mu_ref, y_ref):
        x = x_ref[...]
        y_ref[...] = x + mu_ref[...] * (_shift_down(x, 1) - x)

    return _pcall(
        body, name=name, grid=(18, nb),
        in_specs=[pl.BlockSpec((None, t, DH), lambda g, b: (_mix_group(g), b, 0)),
                  pl.BlockSpec((None, 1, DH), lambda g, b: (g, 0, 0))],
        out_specs=pl.BlockSpec((None, t, DH), lambda g, b: (g, b, 0)),
        out_shape=jax.ShapeDtypeStruct((18, nb * t, DH), F32),
        compiler_params=_cparams(("parallel", "parallel")),
    )(p, mu)


def _mix_bwd(p, mu, dy, nb, t, name):
    def body(x_ref, mu_ref, dy_ref, dx_ref, dmu_ref):
        x = x_ref[...]
        muv = mu_ref[...]
        d = dy_ref[...]
        dx_ref[...] = (d * (1.0 - muv) + _shift_up(d * muv, 1)).astype(BF16)
        dmu_ref[...] = jnp.sum(d * (_shift_down(x, 1) - x), 0, keepdims=True)

    return _pcall(
        body, name=name, grid=(18, nb),
        in_specs=[pl.BlockSpec((None, t, DH), lambda g, b: (_mix_group(g), b, 0)),
                  pl.BlockSpec((None, 1, DH), lambda g, b: (g, 0, 0)),
                  pl.BlockSpec((None, t, DH), lambda g, b: (g, b, 0))],
        out_specs=[pl.BlockSpec((None, t, DH), lambda g, b: (g, b, 0)),
                   pl.BlockSpec((None, None, 1, DH), lambda g, b: (g, b, 0, 0))],
        out_shape=[jax.ShapeDtypeStruct((18, nb * t, DH), BF16),
                   jax.ShapeDtypeStruct((18, nb, 1, DH), F32)],
        compiler_params=_cparams(("parallel", "parallel")),
    )(p, mu, dy)


def _sc_fwd(p, w, nb, t, name):
    def body(p_ref, w_ref, y_ref):
        u = p_ref[1] * p_ref[2]
        conv = w_ref[2:3, :] * u + w_ref[1:2, :] * _shift_down(u, 1) + w_ref[0:1, :] * _shift_down(u, 2)
        y_ref[...] = (p_ref[0] * conv * _silu(p_ref[3])).astype(BF16)

    return _pcall(
        body, name=name, grid=(NH, nb),
        in_specs=[pl.BlockSpec((4, t, DH), lambda j, b: (G_SC // 4 + j, b, 0)),
                  pl.BlockSpec((None, SC_TAPS, DH), lambda j, b: (j, 0, 0))],
        out_specs=pl.BlockSpec((None, t, DH), lambda j, b: (j, b, 0)),
        out_shape=jax.ShapeDtypeStruct((NH, nb * t, DH), BF16),
        compiler_params=_cparams(("parallel", "parallel")),
    )(p, w)


def _sc_bwd(p, w, dy, nb, t, name):
    def body(p_ref, w_ref, dy_ref, dp_ref, dw_ref):
        bg, cg, xg, z = p_ref[0], p_ref[1], p_ref[2], p_ref[3]
        d = dy_ref[...]
        u = cg * xg
        u1 = _shift_down(u, 1)
        u2 = _shift_down(u, 2)
        conv = w_ref[2:3, :] * u + w_ref[1:2, :] * u1 + w_ref[0:1, :] * u2
        sg = jax.nn.sigmoid(z)
        sz = z * sg
        dp_ref[0] = (d * conv * sz).astype(BF16)
        dp_ref[3] = (d * bg * conv * (sg * (1.0 + z * (1.0 - sg)))).astype(BF16)
        dconv = d * bg * sz
        du = w_ref[2:3, :] * dconv + w_ref[1:2, :] * _shift_up(dconv, 1) + w_ref[0:1, :] * _shift_up(dconv, 2)
        dp_ref[1] = (du * xg).astype(BF16)
        dp_ref[2] = (du * cg).astype(BF16)
        dw_ref[2:3, :] = jnp.sum(dconv * u, 0, keepdims=True)
        dw_ref[1:2, :] = jnp.sum(dconv * u1, 0, keepdims=True)
        dw_ref[0:1, :] = jnp.sum(dconv * u2, 0, keepdims=True)

    return _pcall(
        body, name=name, grid=(NH, nb),
        in_specs=[pl.BlockSpec((4, t, DH), lambda j, b: (G_SC // 4 + j, b, 0)),
                  pl.BlockSpec((None, SC_TAPS, DH), lambda j, b: (j, 0, 0)),
                  pl.BlockSpec((None, t, DH), lambda j, b: (8 + j, b, 0))],
        out_specs=[pl.BlockSpec((4, t, DH), lambda j, b: (j, b, 0)),
                   pl.BlockSpec((None, None, SC_TAPS, DH), lambda j, b: (j, b, 0, 0))],
        out_shape=[jax.ShapeDtypeStruct((4 * NH, nb * t, DH), BF16),
                   jax.ShapeDtypeStruct((NH, nb, SC_TAPS, DH), F32)],
        compiler_params=_cparams(("parallel", "parallel")),
    )(p, w, dy)


def _row_tile(n):
    return 1024 if n % 1024 == 0 else n


def _regroup_in(w_all, name):
    tr = 256
    gs = GROUPS_PER_STEP

    def body(w_ref, o_ref):
        for g in _PADDED_GROUPS:
            o_ref[g // gs, :, DH * (g % gs):DH * (g % gs + 1)] = jnp.zeros((tr, DH), BF16)
        for g, a, d, off, ln in _SEGMENTS:
            lane = DH * (g % gs) + a
            o_ref[g // gs, :, lane:lane + ln] = w_ref[d, :, off:off + ln].astype(BF16)

    return _pcall(
        body, name=name, grid=(D_MODEL // tr,),
        in_specs=[pl.BlockSpec((N_DEV, tr, SHARD_COLS), lambda i: (0, i, 0))],
        out_specs=pl.BlockSpec((N_GROUPS // gs, tr, gs * DH), lambda i: (0, i, 0)),
        out_shape=jax.ShapeDtypeStruct((N_GROUPS // gs, D_MODEL, gs * DH), BF16),
        compiler_params=_cparams(("parallel",)),
    )(w_all)


def _regroup_out(dwg, name):
    tr = 256
    gs = GROUPS_PER_STEP

    def body(g_ref, o_ref):
        for g, a, d, off, ln in _SEGMENTS:
            lane = DH * (g % gs) + a
            o_ref[d, :, off:off + ln] = g_ref[g // gs, :, lane:lane + ln].astype(BF16)

    return _pcall(
        body, name=name, grid=(D_MODEL // tr,),
        in_specs=[pl.BlockSpec((N_GROUPS // gs, tr, gs * DH), lambda i: (0, i, 0))],
        out_specs=pl.BlockSpec((N_DEV, tr, SHARD_COLS), lambda i: (0, i, 0)),
        out_shape=jax.ShapeDtypeStruct((N_DEV, D_MODEL, SHARD_COLS), BF16),
        compiler_params=_cparams(("parallel",)),
    )(dwg)


def _norm_proj(x, pre_w, w_g, name):
    n = x.shape[0]
    tm = _row_tile(n)
    gs = GROUPS_PER_STEP

    def body(x_ref, pw_ref, w_ref, h_ref, p_ref):
        @pl.when(pl.program_id(1) == 0)
        def _():
            xv = x_ref[...]
            h = xv * lax.rsqrt(jnp.mean(xv * xv, -1, keepdims=True) + EPS) * pw_ref[...]
            h_ref[...] = h.astype(BF16)

        r = jnp.dot(h_ref[...], w_ref[...], preferred_element_type=F32)
        for k in range(gs):
            p_ref[k] = r[:, DH * k:DH * (k + 1)]

    return _pcall(
        body, name=name, grid=(n // tm, N_GROUPS // gs),
        in_specs=[pl.BlockSpec((tm, D_MODEL), lambda i, j: (i, 0)),
                  pl.BlockSpec((1, D_MODEL), lambda i, j: (0, 0)),
                  pl.BlockSpec((None, D_MODEL, gs * DH), lambda i, j: (j, 0, 0))],
        out_specs=[pl.BlockSpec((tm, D_MODEL), lambda i, j: (i, 0)),
                   pl.BlockSpec((gs, tm, DH), lambda i, j: (j, i, 0))],
        out_shape=[jax.ShapeDtypeStruct((n, D_MODEL), BF16),
                   jax.ShapeDtypeStruct((N_GROUPS, n, DH), F32)],
        compiler_params=_cparams(("parallel", "arbitrary")),
    )(x, pre_w, w_g)


def _out_proj_norm(ys, wout_g, x, post_w, name):
    n = x.shape[0]
    tm = _row_tile(n)

    def body(y0, y1, y2, y3, w_ref, x_ref, pw_ref, out_ref, xn_ref):
        y = jnp.concatenate([yr[h] for yr in (y0, y1, y2, y3) for h in range(NH)], axis=-1)
        acc = jnp.dot(y, w_ref[...], preferred_element_type=F32)
        out_ref[...] = acc
        xn_ref[...] = x_ref[...] + acc * lax.rsqrt(jnp.mean(acc * acc, -1, keepdims=True) + EPS) * pw_ref[...]

    yspec = pl.BlockSpec((NH, tm, DH), lambda i: (0, i, 0))
    rows = pl.BlockSpec((tm, D_MODEL), lambda i: (i, 0))
    return _pcall(
        body, name=name, grid=(n // tm,),
        in_specs=[yspec] * 4 + [pl.BlockSpec((D_MODEL, D_MODEL), lambda i: (0, 0)), rows,
                                pl.BlockSpec((1, D_MODEL), lambda i: (0, 0))],
        out_specs=[rows, rows],
        out_shape=[jax.ShapeDtypeStruct((n, D_MODEL), F32)] * 2,
        compiler_params=_cparams(("parallel",)),
    )(*ys, wout_g.reshape(D_MODEL, D_MODEL), x, post_w)


def _loss_grad(x, tgt, name):
    n = x.shape[0]
    tm = _row_tile(n)

    def body(x_ref, t_ref, dx_ref, l_ref):
        @pl.when(pl.program_id(0) == 0)
        def _():
            l_ref[...] = jnp.zeros_like(l_ref)

        e = x_ref[...] - t_ref[...]
        dx_ref[...] = e * (1.0 / D_MODEL)
        l_ref[...] += jnp.sum(jnp.sum(e * e, -1, keepdims=True), 0, keepdims=True) * (0.5 / D_MODEL)

    rows = pl.BlockSpec((tm, D_MODEL), lambda i: (i, 0))
    return _pcall(
        body, name=name, grid=(n // tm,),
        in_specs=[rows, rows],
        out_specs=[rows, pl.BlockSpec((1, 128), lambda i: (0, 0))],
        out_shape=[jax.ShapeDtypeStruct((n, D_MODEL), F32), jax.ShapeDtypeStruct((1, 128), F32)],
        compiler_params=_cparams(("arbitrary",)),
    )(x, tgt)


def _rmsnorm_bwd(xv, w, d):
    r = lax.rsqrt(jnp.mean(xv * xv, -1, keepdims=True) + EPS)
    xh = xv * r
    dxh = d * w
    dx = r * (dxh - xh * jnp.mean(dxh * xh, -1, keepdims=True))
    return dx, d * xh


def _post_bwd(dxn, out, post_w, wout_g, name):
    n = dxn.shape[0]
    tm = _row_tile(n)

    def body(d_ref, o_ref, pw_ref, w_ref, do_ref, dy_ref, dpw_ref):
        @pl.when(pl.program_id(0) == 0)
        def _():
            dpw_ref[...] = jnp.zeros_like(dpw_ref)

        dout, dw_rows = _rmsnorm_bwd(o_ref[...], pw_ref[...], d_ref[...])
        dpw_ref[...] += jnp.sum(dw_rows, 0, keepdims=True)
        db = dout.astype(BF16)
        do_ref[...] = db
        dy = lax.dot_general(db, w_ref[...], (((1,), (1,)), ((), ())), preferred_element_type=F32)
        for g in range(4 * NH):
            dy_ref[g] = dy[:, DH * g:DH * (g + 1)]

    rows = pl.BlockSpec((tm, D_MODEL), lambda i: (i, 0))
    vec = pl.BlockSpec((1, D_MODEL), lambda i: (0, 0))
    return _pcall(
        body, name=name, grid=(n // tm,),
        in_specs=[rows, rows, vec, pl.BlockSpec((D_MODEL, D_MODEL), lambda i: (0, 0))],
        out_specs=[rows, pl.BlockSpec((4 * NH, tm, DH), lambda i: (0, i, 0)), vec],
        out_shape=[jax.ShapeDtypeStruct((n, D_MODEL), BF16),
                   jax.ShapeDtypeStruct((4 * NH, n, DH), F32),
                   jax.ShapeDtypeStruct((1, D_MODEL), F32)],
        compiler_params=_cparams(("arbitrary",)),
    )(dxn, out, post_w, wout_g.reshape(D_MODEL, D_MODEL))


def _dwout(ys, dout, name):
    n = dout.shape[0]
    tm = _row_tile(n)

    def body(y0, y1, y2, y3, d_ref, dw_ref):
        @pl.when(pl.program_id(0) == 0)
        def _():
            dw_ref[...] = jnp.zeros_like(dw_ref)

        y = jnp.concatenate([yr[h] for yr in (y0, y1, y2, y3) for h in range(NH)], axis=-1)
        dw_ref[...] += lax.dot_general(y, d_ref[...], (((0,), (0,)), ((), ())), preferred_element_type=F32)

    yspec = pl.BlockSpec((NH, tm, DH), lambda i: (0, i, 0))
    return _pcall(
        body, name=name, grid=(n // tm,),
        in_specs=[yspec] * 4 + [pl.BlockSpec((tm, D_MODEL), lambda i: (i, 0))],
        out_specs=pl.BlockSpec((D_MODEL, D_MODEL), lambda i: (0, 0)),
        out_shape=jax.ShapeDtypeStruct((D_MODEL, D_MODEL), F32),
        compiler_params=_cparams(("arbitrary",)),
    )(*ys, dout)


def _source_specs(sources, rows_first):
    gs = GROUPS_PER_STEP
    spans, specs, j0 = [], [], 0
    for a in sources:
        nblk = a.shape[0] // gs
        spans.append((j0, j0 + nblk))
        shape = (gs, _row_tile(a.shape[1]), DH)

        def blk(j, j0=j0, nblk=nblk):
            return jnp.clip(j - j0, 0, nblk - 1)

        if rows_first:
            specs.append(pl.BlockSpec(shape, (lambda i, j, blk=blk: (blk(j), i, 0))))
        else:
            specs.append(pl.BlockSpec(shape, (lambda j, i, blk=blk: (blk(j), i, 0))))
        j0 += nblk
    return spans, specs


def _dh_prenorm_bwd(sources, w_g, x, pre_w, dxn, name, side=None):
    n = x.shape[0]
    tm = _row_tile(n)
    gs = GROUPS_PER_STEP
    nj = N_GROUPS // gs
    ni = n // tm
    spans, src_specs = _source_specs(sources, True)
    ns = len(sources)
    n_side = len(side.operands) if side else 0

    def body(*refs):
        src = refs[:ns]
        w_ref, x_ref, pw_ref, d_ref = refs[ns:ns + 4]
        side_in = refs[ns + 4:ns + 4 + n_side]
        dx_ref, dpw_ref = refs[ns + 4 + n_side:ns + 6 + n_side]
        side_out = refs[ns + 6 + n_side:ns + 6 + 2 * n_side]
        acc = refs[ns + 6 + 2 * n_side]
        sems = refs[ns + 7 + 2 * n_side:]
        i, j = pl.program_id(0), pl.program_id(1)

        if side is not None:
            @pl.when((i == 0) & (j == 0))
            def _():
                _comm_start(side.copies(side_in, side_out, sems))

        @pl.when((i == 0) & (j == 0))
        def _():
            dpw_ref[...] = jnp.zeros_like(dpw_ref)

        @pl.when(j == 0)
        def _():
            acc[...] = jnp.zeros_like(acc)

        for s_ref, (lo, hi) in zip(src, spans):
            @pl.when((j >= lo) & (j < hi))
            def _(s_ref=s_ref):
                four = jnp.concatenate([s_ref[k] for k in range(gs)], axis=-1)
                acc[...] += lax.dot_general(four, w_ref[...], (((1,), (1,)), ((), ())), preferred_element_type=F32)

        @pl.when(j == nj - 1)
        def _():
            dx, dw_rows = _rmsnorm_bwd(x_ref[...], pw_ref[...], acc[...])
            dx_ref[...] = d_ref[...] + dx
            dpw_ref[...] += jnp.sum(dw_rows, 0, keepdims=True)

        if side is not None:
            @pl.when((i == ni - 1) & (j == nj - 1))
            def _():
                _comm_wait(side.copies(side_in, side_out, sems))

    hbm = pl.BlockSpec(memory_space=pl.ANY)
    rows = pl.BlockSpec((tm, D_MODEL), lambda i, j: (i, 0))
    vec = pl.BlockSpec((1, D_MODEL), lambda i, j: (0, 0))
    dx, dpw, *side_res = _pcall(
        body, name=name, grid=(ni, nj),
        in_specs=src_specs + [pl.BlockSpec((None, D_MODEL, gs * DH), lambda i, j: (j, 0, 0)), rows, vec, rows]
        + [hbm] * n_side,
        out_specs=[rows, vec] + [hbm] * n_side,
        out_shape=[jax.ShapeDtypeStruct((n, D_MODEL), F32), jax.ShapeDtypeStruct((1, D_MODEL), F32)]
        + (side.out_shapes if side else []),
        scratch_shapes=[pltpu.VMEM((tm, D_MODEL), F32)] + (_comm_scratch(side) if side else []),
        compiler_params=_cparams(("arbitrary", "arbitrary")),
    )(*sources, w_g, x, pre_w, dxn, *(side.operands if side else []))
    return dx, dpw, side_res


def _dwin(hb, sources, name):
    n = hb.shape[0]
    tm = _row_tile(n)
    gs = GROUPS_PER_STEP
    ni, nj = n // tm, N_GROUPS // gs
    spans, src_specs = _source_specs(sources, True)
    ns = len(sources)

    def body(*refs):
        h_ref = refs[0]
        src = refs[1:1 + ns]
        out_ref, acc, sem = refs[1 + ns:]
        i, j = pl.program_id(0), pl.program_id(1)

        @pl.when((i == 0) & (j == 0))
        def _():
            acc[...] = jnp.zeros_like(acc)

        h = h_ref[...]
        for s_ref, (lo, hi) in zip(src, spans):
            @pl.when((j >= lo) & (j < hi))
            def _(s_ref=s_ref):
                four = jnp.concatenate([s_ref[k] for k in range(gs)], axis=-1)
                acc[j] += jnp.dot(h, four, preferred_element_type=F32)

        @pl.when((i == ni - 1) & (j == nj - 1))
        def _():
            done = pltpu.make_async_copy(acc, out_ref, sem)
            done.start()
            done.wait()

    return _pcall(
        body, name=name, grid=(ni, nj),
        in_specs=[pl.BlockSpec((D_MODEL, tm), lambda i, j: (0, i))] + src_specs,
        out_specs=pl.BlockSpec(memory_space=pl.ANY),
        out_shape=jax.ShapeDtypeStruct((nj, D_MODEL, gs * DH), F32),
        scratch_shapes=[pltpu.VMEM((nj, D_MODEL, gs * DH), F32), pltpu.SemaphoreType.DMA],
        compiler_params=_cparams(("arbitrary", "arbitrary")),
    )(jnp.transpose(hb), *sources)


def _adamw_math(w, g, m, v):
    c1 = 1.0 - ADAM_B1 ** ADAM_STEP
    c2 = 1.0 - ADAM_B2 ** ADAM_STEP
    nm = ADAM_B1 * m + (1.0 - ADAM_B1) * g
    nv = ADAM_B2 * v + (1.0 - ADAM_B2) * (g * g)
    return -ADAM_LR * ((nm / c1) / (jnp.sqrt(nv / c2) + ADAM_EPS) + ADAM_WD * w), nm, nv


def _adamw(w, g, m, v, name):
    r, c = w.shape
    tr = 256 if r % 256 == 0 else r

    def body(w_ref, g_ref, m_ref, v_ref, d_ref, nm_ref, nv_ref):
        d_ref[...], nm_ref[...], nv_ref[...] = _adamw_math(w_ref[...], g_ref[...], m_ref[...], v_ref[...])

    spec = pl.BlockSpec((tr, c), lambda i: (i, 0))
    return _pcall(
        body, name=name, grid=(r // tr,),
        in_specs=[spec] * 4, out_specs=[spec] * 3,
        out_shape=[jax.ShapeDtypeStruct((r, c), F32)] * 3,
        compiler_params=_cparams(("parallel",)),
    )(w, g, m, v)


def _sum_adamw(parts, w, m, v, name):
    r, c = w.shape
    tr = 128 if r % 128 == 0 else r

    def body(p_ref, w_ref, m_ref, v_ref, g_ref, d_ref, nm_ref, nv_ref):
        g = p_ref[0].astype(F32)
        for k in range(1, N_DEV):
            g = g + p_ref[k].astype(F32)
        g_ref[...] = g
        d_ref[...], nm_ref[...], nv_ref[...] = _adamw_math(w_ref[...], g, m_ref[...], v_ref[...])

    spec = pl.BlockSpec((tr, c), lambda i: (i, 0))
    return _pcall(
        body, name=name, grid=(r // tr,),
        in_specs=[pl.BlockSpec((N_DEV, tr, c), lambda i: (0, i, 0))] + [spec] * 3, out_specs=[spec] * 4,
        out_shape=[jax.ShapeDtypeStruct((r, c), F32)] * 4,
        compiler_params=_cparams(("parallel",)),
    )(parts, w, m, v)


def _me():
    return lax.axis_index("x"), lax.axis_index("y"), lax.axis_index("c")


def _flat(x, y, c):
    return 4 * x + 2 * y + c


def _peer(k):
    x, y, c = _me()
    return (x ^ ((k >> 2) & 1), y ^ ((k >> 1) & 1), c ^ (k & 1))


def _gather_plan(blocks):
    def copies(x_refs, out_refs, sems):
        send_sems, recv_sems, local_sems = sems
        me = _flat(*_me())
        local = [pltpu.make_async_copy(x, o.at[me], local_sems.at[a]) for a, (x, o) in enumerate(zip(x_refs, out_refs))]
        outgoing, incoming = [], []
        for k in range(1, N_DEV):
            src = _flat(*_peer(k))
            for a, (x, o) in enumerate(zip(x_refs, out_refs)):
                for slot, group in ((me, outgoing), (src, incoming)):
                    group.append(pltpu.make_async_remote_copy(
                        src_ref=x, dst_ref=o.at[slot], send_sem=send_sems.at[a, k - 1], recv_sem=recv_sems.at[a, k - 1],
                        device_id=_peer(k), device_id_type=MESH))
        return local, outgoing, incoming

    return _Comm(list(blocks), [jax.ShapeDtypeStruct((N_DEV,) + b.shape, b.dtype) for b in blocks], copies)


def _exchange_plan(sends):
    def copies(s_refs, out_refs, sems):
        send_sems, recv_sems, local_sems = sems
        me = _flat(*_me())
        local = [pltpu.make_async_copy(s.at[me], o.at[0], local_sems.at[i]) for i, (s, o) in enumerate(zip(s_refs, out_refs))]
        outgoing = []
        for k in range(1, N_DEV):
            to = _flat(*_peer(k))
            for i, (s, o) in enumerate(zip(s_refs, out_refs)):
                outgoing.append(pltpu.make_async_remote_copy(
                    src_ref=s.at[to], dst_ref=o.at[k], send_sem=send_sems.at[i, k - 1], recv_sem=recv_sems.at[i, k - 1],
                    device_id=_peer(k), device_id_type=MESH))
        return local, outgoing, outgoing

    return _Comm(list(sends), [jax.ShapeDtypeStruct(s.shape, s.dtype) for s in sends], copies)


def _comm_scratch(plan):
    n = len(plan.operands)
    return [pltpu.SemaphoreType.DMA((n, N_DEV - 1)), pltpu.SemaphoreType.DMA((n, N_DEV - 1)),
            pltpu.SemaphoreType.DMA((n,))]


def _comm_start(copies):
    local, outgoing, _ = copies
    for cp in local + outgoing:
        cp.start()


def _comm_wait(copies):
    local, outgoing, incoming = copies
    for cp in incoming:
        cp.wait_recv()
    for cp in outgoing:
        cp.wait_send()
    for cp in local:
        cp.wait()


def _run_comm(plan, name):
    n = len(plan.operands)

    def body(*refs):
        copies = plan.copies(refs[:n], refs[n:2 * n], refs[2 * n:])
        _comm_start(copies)
        _comm_wait(copies)

    return _pcall(
        body, name=name,
        in_specs=[pl.BlockSpec(memory_space=pl.ANY)] * n,
        out_specs=[pl.BlockSpec(memory_space=pl.ANY)] * n,
        out_shape=plan.out_shapes,
        scratch_shapes=_comm_scratch(plan),
    )(*plan.operands)


def _all_gather_two_level(blocks, name):
    na = len(blocks)

    def body(*refs):
        x_refs, out_refs = refs[:na], refs[na:2 * na]
        send_sems, recv_sems, local_sems = refs[2 * na:]
        x, y, c = _me()
        me, sibling = (x, y, c), (x, y, 1 - c)
        chips = [(1 - x, y), (x, 1 - y), (1 - x, 1 - y)]

        def copy(a, k, block, to, own=False):
            slot = out_refs[a].at[_flat(*block)]
            return pltpu.make_async_remote_copy(
                src_ref=x_refs[a] if own else slot, dst_ref=slot, send_sem=send_sems.at[a, k],
                recv_sem=recv_sems.at[a, k], device_id=to, device_id_type=MESH)

        mine = [pltpu.make_async_copy(x_refs[a], out_refs[a].at[_flat(*me)], local_sems.at[a]) for a in range(na)]
        first = [copy(a, 0, me, sibling, own=True) for a in range(na)]
        first += [copy(a, 1 + j, me, (*chip, c), own=True) for j, chip in enumerate(chips) for a in range(na)]
        for cp in mine + first:
            cp.start()
        passed = []
        for j, chip in enumerate(chips):
            for a in range(na):
                copy(a, 1 + j, (*chip, c), me).wait_recv()
                cp = copy(a, 4 + j, (*chip, c), sibling)
                cp.start()
                passed.append(cp)
        for a in range(na):
            copy(a, 0, sibling, me).wait_recv()
        for j, chip in enumerate(chips):
            for a in range(na):
                copy(a, 4 + j, (*chip, 1 - c), me).wait_recv()
        for cp in first + passed:
            cp.wait_send()
        for cp in mine:
            cp.wait()

    return _pcall(
        body, name=name,
        in_specs=[pl.BlockSpec(memory_space=pl.ANY)] * na,
        out_specs=[pl.BlockSpec(memory_space=pl.ANY)] * na,
        out_shape=[jax.ShapeDtypeStruct((N_DEV,) + b.shape, b.dtype) for b in blocks],
        scratch_shapes=[pltpu.SemaphoreType.DMA((na, N_DEV - 1)), pltpu.SemaphoreType.DMA((na, N_DEV - 1)),
                        pltpu.SemaphoreType.DMA((na,))],
    )(*blocks)


def _sum_slots(a, name):
    r = a.shape[1]

    def body(a_ref, o_ref):
        acc = a_ref[0]
        for d in range(1, N_DEV):
            acc = acc + a_ref[d]
        o_ref[...] = acc

    return _pcall(body, name=name, out_shape=jax.ShapeDtypeStruct((r, 128), F32), compiler_params=_cparams())(a)


def _all_reduce_small(blk, name):
    r = blk.shape[0]

    def body(x_ref, out_ref, gath, send_sems, recv_sems):
        me = _flat(*_me())
        gath[me] = x_ref[...]
        copies = []
        for k in range(1, N_DEV):
            cp = pltpu.make_async_remote_copy(
                src_ref=x_ref, dst_ref=gath.at[me],
                send_sem=send_sems.at[k - 1], recv_sem=recv_sems.at[k - 1],
                device_id=_peer(k), device_id_type=MESH)
            cp.start()
            copies.append(cp)
        for k in range(1, N_DEV):
            src = _flat(*_peer(k))
            pltpu.make_async_remote_copy(
                src_ref=x_ref, dst_ref=gath.at[src],
                send_sem=send_sems.at[k - 1], recv_sem=recv_sems.at[k - 1],
                device_id=_peer(k), device_id_type=MESH).wait_recv()
        for cp in copies:
            cp.wait_send()
        acc = gath[0]
        for d in range(1, N_DEV):
            acc = acc + gath[d]
        out_ref[...] = acc

    return _pcall(
        body, name=name,
        in_specs=[pl.BlockSpec(memory_space=pltpu.VMEM)],
        out_specs=pl.BlockSpec(memory_space=pltpu.VMEM),
        out_shape=jax.ShapeDtypeStruct((r, 128), F32),
        scratch_shapes=[pltpu.VMEM((N_DEV, r, 128), F32),
                        pltpu.SemaphoreType.DMA((N_DEV - 1,)), pltpu.SemaphoreType.DMA((N_DEV - 1,))],
    )(blk)


def _heads(vec):
    return vec.reshape(NH, 1, DH)


def _rep(vec4):
    return jnp.broadcast_to(vec4.reshape(NH, 1, 1), (NH, 1, DH))


def _onehot_lane(offset):
    m = np.zeros((NH, 1, DH), np.float32)
    for h in range(NH):
        m[h, 0, offset + h] = 1.0
    return jnp.asarray(m)


_TINY = (("gdn_conv_w", (DEPTH, 4, 96)), ("rwkv_w_up", (DEPTH, 64, 32)), ("rwkv_a_up", (DEPTH, 64, 32)),
         ("sc_conv_w", (DEPTH, 3, 32)))
_TINY_ROWS = -(-sum(int(np.prod(s)) for _, s in _TINY) // 1024) * 8


def _pack_rows(arrays, rows, fill=0.0):
    flat = jnp.concatenate([a.reshape(-1) for a in arrays])
    return jnp.pad(flat, (0, rows * 128 - flat.shape[0]), constant_values=fill).reshape(rows, 128)


def _unpack_rows(p, named_shapes):
    lead = p.shape[:-2]
    flat = p.reshape(lead + (-1,))
    out, o = {}, 0
    for n, s in named_shapes:
        size = int(np.prod(s))
        out[n] = flat[..., o:o + size].reshape(lead + tuple(s))
        o += size
    return out


def _gather_last(a):
    return jnp.transpose(a, (1, 0, 2)).reshape(a.shape[1], -1)


def _split_last(a):
    r, c8 = a.shape
    return jnp.transpose(a.reshape(r, N_DEV, c8 // N_DEV), (1, 0, 2))


_SMALL = (("pre_norm_w", (DEPTH, 1024)), ("gdn_a_log", (DEPTH, 4)), ("gdn_dt_bias", (DEPTH, 4)),
          ("gdn_norm_w", (DEPTH, 64)), ("rwkv_mu", (DEPTH, 1152)), ("rwkv_w0", (DEPTH, 256)),
          ("rwkv_a0", (DEPTH, 256)), ("rwkv_k_k", (DEPTH, 256)), ("rwkv_k_a", (DEPTH, 256)),
          ("rwkv_r_k", (DEPTH, 256)), ("rwkv_ln_w", (DEPTH, 256)), ("rwkv_ln_b", (DEPTH, 256)),
          ("gla_a_up", (DEPTH, 16, 128)), ("gla_a_bias", (DEPTH, 128)), ("gla_norm_w", (DEPTH, 64)),
          ("post_norm_w", (DEPTH, 1024)), ("loss", ()))
_SMALL_ROWS = -(-sum(int(np.prod(s)) for _, s in _SMALL) // 1024) * 8


def _big_weights(w_in_all, w_out_all, l):
    return dict(w_g=_regroup_in(w_in_all, f"regroup_in{l}"),
                wout_g=w_out_all.reshape(4 * NH, DH, D_MODEL).astype(BF16))


def _layer_params(wts, tiny, l):
    conv = _gather_last(tiny["gdn_conv_w"][:, l])
    q = {}
    q["gdn_conv"] = jnp.transpose(conv.reshape(GDN_TAPS, 12, DH), (1, 0, 2))
    q["gdn_prm"] = [_rep(wts["gdn_a_log"][l]), _rep(wts["gdn_dt_bias"][l]),
                    jnp.broadcast_to(wts["gdn_norm_w"][l].reshape(1, 1, DH), (NH, 1, DH))]
    q["gdn_cst"] = [_onehot_lane(0), _onehot_lane(NH)]
    q["rwkv_mu"] = wts["rwkv_mu"][l].reshape(18, 1, DH)
    w_up = jnp.transpose(_gather_last(tiny["rwkv_w_up"][:, l]).reshape(64, NH, DH), (1, 0, 2))
    a_up = jnp.transpose(_gather_last(tiny["rwkv_a_up"][:, l]).reshape(64, NH, DH), (1, 0, 2))
    q["rwkv_prm"] = [_heads(wts["rwkv_w0"][l]), w_up, _heads(wts["rwkv_a0"][l]), a_up,
                     _heads(wts["rwkv_k_k"][l]), _heads(wts["rwkv_k_a"][l]), _heads(wts["rwkv_r_k"][l]),
                     _heads(wts["rwkv_ln_w"][l]), _heads(wts["rwkv_ln_b"][l])]
    sc = _gather_last(tiny["sc_conv_w"][:, l])
    q["sc_conv"] = jnp.transpose(sc.reshape(SC_TAPS, NH, DH), (1, 0, 2))
    gla_up = jnp.transpose(wts["gla_a_up"][l].reshape(16, NH, GLA_HEAD_K), (1, 0, 2))
    gla_up = jnp.pad(gla_up, ((0, 0), (0, DH - 16), (0, DH - GLA_HEAD_K)))
    gla_b = jnp.pad(wts["gla_a_bias"][l].reshape(NH, 1, GLA_HEAD_K), ((0, 0), (0, 0), (0, DH - GLA_HEAD_K)))
    q["gla_prm"] = [gla_up, gla_b, jnp.broadcast_to(wts["gla_norm_w"][l].reshape(1, 1, DH), (NH, 1, DH))]
    q["pre_w"] = wts["pre_norm_w"][l].reshape(1, D_MODEL)
    q["post_w"] = wts["post_norm_w"][l].reshape(1, D_MODEL)
    return q


def _mixer_inputs(p, cq, pm):
    gdn = [(cq, 4, 0), (cq, 4, 1), (cq, 4, 2), (p, 4, G_GDN // 4 + 3), (p, 1, G_GDN_AB)]
    rwkv = [(pm, 4, 0), (pm, 4, 1), (pm, 4, 2), (pm, 4, 3), (pm, 1, 16), (pm, 1, 17)]
    gla = [(p, 4, G_GLA // 4 + k) for k in range(4)] + [(p, 1, G_GLA_AD)]
    return gdn, rwkv, gla


def _layer_fwd(x, q, nb, t, l, side=None):
    hb, p = _norm_proj(x, q["pre_w"], q["w_g"], f"norm_proj{l}")
    cq = _conv_fwd(p, G_GDN, 12, q["gdn_conv"], nb, t, f"gdn_conv{l}")
    pm = _mix_fwd(p, q["rwkv_mu"], nb, t, f"rwkv_mix{l}")
    gdn_in, rwkv_in, gla_in = _mixer_inputs(p, cq, pm)
    y_gdn, ck_gdn, _ = _mixer_fwd(_gdn_chunk, f"gdn_fwd{l}", gdn_in, q["gdn_prm"], q["gdn_cst"], nb, t, n_kept=1)
    y_rwkv, ck_rwkv, side_res = _mixer_fwd(_rwkv_chunk, f"rwkv_fwd{l}", rwkv_in, q["rwkv_prm"], [], nb, t,
                                           first_fn=_rwkv_first_chunk, side=side, n_kept=1)
    y_sc = _sc_fwd(p, q["sc_conv"], nb, t, f"sc_fwd{l}")
    y_gla, ck_gla, _ = _mixer_fwd(_gla_chunk, f"gla_fwd{l}", gla_in, q["gla_prm"], [], nb, t)
    ys = (y_gdn, y_rwkv, y_sc, y_gla)
    out, xn = _out_proj_norm(ys, q["wout_g"], x, q["post_w"], f"out_proj{l}")
    saved = dict(x=x, hb=hb, p=p, cq=cq, pm=pm, ys=ys, out=out, ck=(ck_gdn, ck_rwkv, ck_gla))
    return xn, saved, side_res


def _layer_bwd(dxn, q, sv, nb, t, l, side=None, exchange_own=False):
    p, cq, pm, ys = sv["p"], sv["cq"], sv["pm"], sv["ys"]
    dout, dy, d_post = _post_bwd(dxn, sv["out"], q["post_w"], q["wout_g"], f"post_bwd{l}")
    d_wout = _dwout(ys, dout, f"dwout{l}").reshape(N_DEV, 128, D_MODEL).astype(BF16)
    gdn_in, rwkv_in, gla_in = _mixer_inputs(p, cq, pm)
    ck_gdn, ck_rwkv, ck_gla = sv["ck"]
    g = {}

    (d_conv, dz, dab), (da_log, ddt, dnw), _ = _mixer_bwd(
        _gdn_chunk, f"gdn_bwd{l}", gdn_in, q["gdn_prm"], q["gdn_cst"], ck_gdn, dy, 0,
        [(12, F32), (4, BF16), (1, BF16)], [(0, 0), (0, 4), (0, 8), (1, 0), (2, 0)], nb, t)
    dconv_in, d_gconv = _conv_bwd(p, G_GDN, 12, q["gdn_conv"], d_conv, nb, t, f"gdn_conv_bwd{l}")
    g["gdn_conv_w"] = jnp.transpose(d_gconv.sum(1), (1, 0, 2)).reshape(GDN_TAPS, 768)
    g["gdn_a_log"] = da_log.sum((0, 2, 3))
    g["gdn_dt_bias"] = ddt.sum((0, 2, 3))
    g["gdn_norm_w"] = dnw.sum((0, 1, 2))

    (d_pm,), d_rprm, side_res = _mixer_bwd(
        _rwkv_chunk, f"rwkv_bwd{l}", rwkv_in, q["rwkv_prm"], [], ck_rwkv, dy, 1,
        [(18, F32)], [(0, 0), (0, 4), (0, 8), (0, 12), (0, 16), (0, 17)], nb, t, first_fn=_rwkv_first_chunk,
        side=side)
    dp_rwkv, d_mu = _mix_bwd(p, q["rwkv_mu"], d_pm, nb, t, f"rwkv_mix_bwd{l}")
    g["rwkv_mu"] = d_mu.sum(1).reshape(1152)
    rp = [a.sum(0) for a in d_rprm]
    g["rwkv_w0"] = rp[0].reshape(256)
    g["rwkv_w_up"] = jnp.transpose(rp[1], (1, 0, 2)).reshape(64, 256)
    g["rwkv_a0"] = rp[2].reshape(256)
    g["rwkv_a_up"] = jnp.transpose(rp[3], (1, 0, 2)).reshape(64, 256)
    for i, nme in enumerate(("rwkv_k_k", "rwkv_k_a", "rwkv_r_k", "rwkv_ln_w", "rwkv_ln_b")):
        g[nme] = rp[4 + i].reshape(256)

    dp_sc, d_scw = _sc_bwd(p, q["sc_conv"], dy, nb, t, f"sc_bwd{l}")
    g["sc_conv_w"] = jnp.transpose(d_scw.sum(1), (1, 0, 2)).reshape(SC_TAPS, 256)

    (dp_gla, dad), (d_aup, d_ab, d_gnw), _ = _mixer_bwd(
        _gla_chunk, f"gla_bwd{l}", gla_in, q["gla_prm"], [], ck_gla, dy, 3,
        [(16, BF16), (1, BF16)], [(0, 0), (0, 4), (0, 8), (0, 12), (1, 0)], nb, t)
    g["gla_a_up"] = jnp.transpose(d_aup.sum(0)[:, :16, :GLA_HEAD_K], (1, 0, 2)).reshape(16, 128)
    g["gla_a_bias"] = d_ab.sum(0)[:, 0, :GLA_HEAD_K].reshape(128)
    g["gla_norm_w"] = d_gnw.sum((0, 1, 2))

    singles = jnp.concatenate([dab, dp_rwkv[16:18], dad], axis=0)
    sources = [dconv_in, dz, dp_rwkv, dp_sc, dp_gla, singles]
    d_win = _regroup_out(_dwin(sv["hb"], sources, f"dwin{l}"), f"regroup_out{l}")
    own = _exchange_plan([d_win, d_wout]) if exchange_own else None
    dx, d_pre, got = _dh_prenorm_bwd(sources, q["w_g"], sv["x"], q["pre_w"], dxn, f"dh_bwd{l}", own)
    if exchange_own:
        d_win, d_wout = got
    g["pre_norm_w"] = d_pre.reshape(D_MODEL)
    g["post_norm_w"] = d_post.reshape(D_MODEL)
    return dx, g, d_win, d_wout, side_res


def _local_step(x, tgt, wts, tiny, w_in_all, w_out_all, later_shards=None):
    nb, t, d = x.shape
    xf = x.reshape(nb * t, d)
    overlap = later_shards is not None
    qs, saved = [], []
    big = _big_weights(w_in_all[0], w_out_all[0], 0)
    for l in range(DEPTH):
        q = dict(_layer_params(wts, tiny, l), **big)
        nxt = l + 1 < DEPTH
        side = _gather_plan(later_shards[l]) if overlap and nxt else None
        xf, sv, got = _layer_fwd(xf, q, nb, t, l, side)
        if nxt:
            big = _big_weights(*(got if overlap else (w_in_all[l + 1], w_out_all[l + 1])), l + 1)
        qs.append(q)
        saved.append(sv)
    dxf, lpart = _loss_grad(xf, tgt.reshape(nb * t, d), "loss")
    grads, d_win, d_wout = [None] * DEPTH, [None] * DEPTH, [None] * DEPTH
    for l in reversed(range(DEPTH)):
        side = _exchange_plan([d_win[l + 1], d_wout[l + 1]]) if overlap and l + 1 < DEPTH else None
        dxf, grads[l], d_win[l], d_wout[l], got = _layer_bwd(dxf, qs[l], saved[l], nb, t, l, side,
                                                             exchange_own=overlap and l == 0)
        if side is not None:
            d_win[l + 1], d_wout[l + 1] = got
    small = {k: jnp.stack([grads[l][k] for l in range(DEPTH)]) for k in grads[0]}
    return lpart[0, 0], dxf.reshape(nb, t, d), small, d_win, d_wout


_WEIGHTS = ("pre_norm_w", "w_in", "gdn_conv_w", "gdn_a_log", "gdn_dt_bias", "gdn_norm_w", "rwkv_mu", "rwkv_w0",
            "rwkv_w_up", "rwkv_a0", "rwkv_a_up", "rwkv_k_k", "rwkv_k_a", "rwkv_r_k", "rwkv_ln_w", "rwkv_ln_b",
            "sc_conv_w", "gla_a_up", "gla_a_bias", "gla_norm_w", "w_out", "post_norm_w")


def kernel(x, pre_norm_w, w_in, gdn_conv_w, gdn_a_log, gdn_dt_bias, gdn_norm_w, rwkv_mu, rwkv_w0, rwkv_w_up, rwkv_a0, rwkv_a_up, rwkv_k_k, rwkv_k_a, rwkv_r_k, rwkv_ln_w, rwkv_ln_b, sc_conv_w, gla_a_up, gla_a_bias, gla_norm_w, w_out, post_norm_w, loss_target, m_pre_norm_w, m_w_in, m_gdn_conv_w, m_gdn_a_log, m_gdn_dt_bias, m_gdn_norm_w, m_rwkv_mu, m_rwkv_w0, m_rwkv_w_up, m_rwkv_a0, m_rwkv_a_up, m_rwkv_k_k, m_rwkv_k_a, m_rwkv_r_k, m_rwkv_ln_w, m_rwkv_ln_b, m_sc_conv_w, m_gla_a_up, m_gla_a_bias, m_gla_norm_w, m_w_out, m_post_norm_w, v_pre_norm_w, v_w_in, v_gdn_conv_w, v_gdn_a_log, v_gdn_dt_bias, v_gdn_norm_w, v_rwkv_mu, v_rwkv_w0, v_rwkv_w_up, v_rwkv_a0, v_rwkv_a_up, v_rwkv_k_k, v_rwkv_k_a, v_rwkv_r_k, v_rwkv_ln_w, v_rwkv_ln_b, v_sc_conv_w, v_gla_a_up, v_gla_a_bias, v_gla_norm_w, v_w_out, v_post_norm_w):
    env = dict(locals())
    w = {n: env[n] for n in _WEIGHTS}
    m = {n: env["m_" + n] for n in _WEIGHTS}
    v = {n: env["v_" + n] for n in _WEIGHTS}
    tiny_names = [n for n, _ in _TINY]

    w_in_b, w_out_b = w_in.astype(BF16), w_out.astype(BF16)
    w_in_0, w_out_0, tiny_all = _all_gather_two_level(
        [w_in_b[0], w_out_b[0], _pack_rows([w[n] for n in tiny_names], _TINY_ROWS)], "gather_weights")
    tiny = _unpack_rows(tiny_all, _TINY)

    lpart, grad_x, small, r_win, r_wout = _local_step(
        x, loss_target, w, tiny, [w_in_0], [w_out_0], later_shards=[(w_in_b[l], w_out_b[l]) for l in range(1, DEPTH)])

    tiny_send = jnp.stack([_pack_rows([_split_last(small[n][l])[d] for n in tiny_names for l in range(DEPTH)],
                                      _TINY_ROWS) for d in range(N_DEV)])
    (r_tiny,) = _run_comm(_exchange_plan([tiny_send]), "scatter_grads")
    grads, delta, new_m, new_v = {}, {}, {}, {}
    for n, parts in (("w_in", r_win), ("w_out", r_wout)):
        res = [_sum_adamw(parts[l], w[n][l], m[n][l], v[n][l], f"adamw_{n}{l}") for l in range(DEPTH)]
        grads[n], delta[n], new_m[n], new_v[n] = [jnp.stack(o) for o in zip(*res)]
    tiny_sum = _sum_slots(r_tiny, "sum_tiny").reshape(-1)
    o = 0
    for n, s in _TINY:
        size = int(np.prod(s))
        grads[n] = tiny_sum[o:o + size].reshape(s)
        o += size

    small = dict(small)
    small["loss"] = lpart
    red = _unpack_rows(_all_reduce_small(_pack_rows([small[n] for n, _ in _SMALL], _SMALL_ROWS), "reduce_small"),
                       _SMALL)
    loss = red.pop("loss")
    grads.update(red)

    rest = [n for n in _WEIGHTS if n not in ("w_in", "w_out")]
    rest_shapes = [(n, w[n].shape) for n in rest]
    rows = -(-sum(int(np.prod(s)) for _, s in rest_shapes) // 1024) * 8
    outs = _adamw(_pack_rows([w[n] for n in rest], rows), _pack_rows([grads[n] for n in rest], rows),
                  _pack_rows([m[n] for n in rest], rows), _pack_rows([v[n] for n in rest], rows, 1.0), "adamw_rest")
    for dst, packed in zip((delta, new_m, new_v), outs):
        dst.update(_unpack_rows(packed, rest_shapes))

    return (loss, grad_x, *[grads[n] for n in _WEIGHTS], *[delta[n] for n in _WEIGHTS],
            *[new_m[n] for n in _WEIGHTS], *[new_v[n] for n in _WEIGHTS])
```

```python
import collections
import functools
import math

import numpy as np
import jax
import jax.numpy as jnp
from jax import lax
from jax.experimental import pallas as pl
from jax.experimental.pallas import tpu as pltpu

F32 = jnp.float32
BF16 = jnp.bfloat16

D_MODEL = 1024
DEPTH = 2
NH = 4
DH = 64
CH = 64
EPS = 1e-6
RWKV_GN_EPS = 64e-5
GLA_HEAD_K = 32
GLA_TAU = 16.0
GDN_TAPS = 4
SC_TAPS = 3
D_IN = 3992
N_DEV = 8
SHARD_COLS = D_IN // N_DEV

G_GDN = 0
G_RWKV = 16
G_SC = 32
G_GLA = 48
G_GDN_AB, G_RWKV_WD, G_RWKV_AD, G_GLA_AD = 64, 65, 66, 67
N_GROUPS = 68
GROUPS_PER_STEP = 4
TIME_BLOCK = 256
RWKV_EXACT_STEPS = 16

C_GDN, C_RWKV, C_SC, C_GLA = 0, 1032, 2184, 3208

ADAM_LR, ADAM_B1, ADAM_B2, ADAM_EPS, ADAM_WD, ADAM_STEP = 0.001, 0.9, 0.999, 1e-08, 0.01, 10

VMEM_LIMIT = 56 * 1024 * 1024
MESH = pl.DeviceIdType.MESH

_pcall = pl.pallas_call

_Comm = collections.namedtuple("_Comm", "operands out_shapes copies")


def _cparams(sem=None):
    if sem is None:
        return pltpu.CompilerParams(vmem_limit_bytes=VMEM_LIMIT)
    return pltpu.CompilerParams(dimension_semantics=sem, vmem_limit_bytes=VMEM_LIMIT)


def _group_segments():
    table = [(G_GDN + i, C_GDN + DH * i, DH) for i in range(16)]
    table.append((G_GDN_AB, C_GDN + 1024, 8))
    table += [(G_RWKV + i, C_RWKV + DH * i, DH) for i in range(16)]
    table += [(G_RWKV_WD, C_RWKV + 1024, DH), (G_RWKV_AD, C_RWKV + 1088, DH)]
    table += [(G_SC + 4 * j + k, C_SC + 256 * k + DH * j, DH) for j in range(NH) for k in range(4)]
    for h in range(NH):
        table += [(G_GLA + h, C_GLA + GLA_HEAD_K * h, GLA_HEAD_K),
                  (G_GLA + 4 + h, C_GLA + 128 + GLA_HEAD_K * h, GLA_HEAD_K),
                  (G_GLA + 8 + h, C_GLA + 256 + DH * h, DH),
                  (G_GLA + 12 + h, C_GLA + 512 + DH * h, DH)]
    table.append((G_GLA_AD, C_GLA + 768, 16))
    segs, padded = [], []
    for g, c, n in table:
        if n < DH:
            padded.append(g)
        a = 0
        while n > 0:
            d, off = divmod(c, SHARD_COLS)
            ln = min(n, SHARD_COLS - off)
            segs.append((g, a, d, off, ln))
            c, a, n = c + ln, a + ln, n - ln
    return segs, padded


_SEGMENTS, _PADDED_GROUPS = _group_segments()


def _dn(ta, tb):
    return (((1 if ta else 2,), (2 if tb else 1,)), ((0,), (0,)))


def _hdot(a, b, ta=False, tb=False):
    return lax.dot_general(a, b, _dn(ta, tb), precision=lax.Precision.HIGH, preferred_element_type=F32)


def _r(x):
    return x.astype(BF16)


def _rdot(a, b, ta=False, tb=False):
    return lax.dot_general(_r(a), _r(b), _dn(ta, tb), preferred_element_type=F32)


@jax.custom_vjp
def _bmm(a, b):
    return _rdot(a, b)


def _bmm_fwd(a, b):
    return _rdot(a, b), (a, b)


def _bmm_bwd(res, g):
    a, b = res
    return _rdot(g, b, tb=True), _rdot(a, g, ta=True)


_bmm.defvjp(_bmm_fwd, _bmm_bwd)


@jax.custom_vjp
def _bmm_nt(a, b):
    return _rdot(a, b, tb=True)


def _bmm_nt_fwd(a, b):
    return _rdot(a, b, tb=True), (a, b)


def _bmm_nt_bwd(res, g):
    a, b = res
    return _rdot(g, b), _rdot(g, a, ta=True)


_bmm_nt.defvjp(_bmm_nt_fwd, _bmm_nt_bwd)


@jax.custom_vjp
def _bmm_tn(a, b):
    return _rdot(a, b, ta=True)


def _bmm_tn_fwd(a, b):
    return _rdot(a, b, ta=True), (a, b)


def _bmm_tn_bwd(res, g):
    a, b = res
    return _rdot(b, g, tb=True), _rdot(a, g)


_bmm_tn.defvjp(_bmm_tn_fwd, _bmm_tn_bwd)


def _tri(n):
    i = lax.broadcasted_iota(jnp.int32, (n, n), 0)
    j = lax.broadcasted_iota(jnp.int32, (n, n), 1)
    return i >= j, i > j, i == j


def _heads_of(x, like):
    n = like.shape[0]
    if x.ndim == 2:
        return jnp.broadcast_to(x[None], (n,) + x.shape)
    seqs = x.shape[0]
    return jnp.broadcast_to(x[:, None], (seqs, n // seqs) + x.shape[1:]).reshape((n,) + x.shape[1:])


def _cumsum_rows(x):
    incl, _, _ = _tri(x.shape[-2])
    return _hdot(_heads_of(incl.astype(F32), x), x)


@jax.custom_vjp
def _inv_unit_lower(a):
    n = a.shape[-1]
    _, _, eye = _tri(n)
    pw = -a
    inv = eye.astype(F32) + pw
    for _ in range(math.ceil(math.log2(n)) - 1):
        pw = _hdot(pw, pw)
        inv = inv + _hdot(inv, pw)
    return inv


def _inv_unit_lower_fwd(a):
    inv = _inv_unit_lower(a)
    return inv, inv


def _inv_unit_lower_bwd(inv, g):
    return (-_hdot(_hdot(inv, g, ta=True), inv, tb=True),)


_inv_unit_lower.defvjp(_inv_unit_lower_fwd, _inv_unit_lower_bwd)


@jax.custom_vjp
def _inv_reuse(a, inv):
    return inv


def _inv_reuse_fwd(a, inv):
    return inv, inv


def _inv_reuse_bwd(inv, g):
    return _inv_unit_lower_bwd(inv, g)[0], jnp.zeros_like(inv)


_inv_reuse.defvjp(_inv_reuse_fwd, _inv_reuse_bwd)


def _silu(x):
    return x * (0.5 + 0.5 * jnp.tanh(0.5 * x))


def _t(x):
    return jnp.swapaxes(x, -1, -2)


def _gdn_chunk(prm, cst, ins, s, tinv=None):
    a_log, dt_b, nw = prm
    m_a, m_b = cst
    cq, ck, cv, z, ab = ins
    ab = _heads_of(ab, m_a)
    incl, strict, _ = _tri(CH)
    q = _silu(cq)
    k = _silu(ck)
    v = _silu(cv)
    q = q * lax.rsqrt(jnp.sum(q * q, -1, keepdims=True) + EPS) * (DH ** -0.5)
    k = k * lax.rsqrt(jnp.sum(k * k, -1, keepdims=True) + EPS)
    a_raw = jnp.sum(ab * m_a, -1, keepdims=True)
    b_raw = jnp.sum(ab * m_b, -1, keepdims=True)
    gstep = -jnp.exp(a_log) * jax.nn.softplus(a_raw + dt_b)
    beta = jax.nn.sigmoid(b_raw)
    gc = _cumsum_rows(gstep)
    gl = jnp.sum(gstep, -2, keepdims=True)
    dec = jnp.where(incl, jnp.exp(jnp.where(incl, gc - _t(gc), 0.0)), 0.0)
    kb = k * beta
    a_mat = jnp.where(strict, _bmm_nt(kb, k) * dec, 0.0)
    tinv = _inv_unit_lower(a_mat) if tinv is None else _inv_reuse(a_mat, tinv)
    eg = jnp.exp(gc)
    u = _hdot(tinv, v * beta)
    w = _hdot(tinv, kb * eg)
    attn = _bmm_nt(q, k) * dec
    v_new = u - _bmm(w, s)
    o = _bmm(q * eg, s) + _bmm(attn, v_new)
    s_next = s * jnp.exp(gl) + _bmm_tn(k * jnp.exp(gl - gc), v_new)
    on = o * lax.rsqrt(jnp.mean(o * o, -1, keepdims=True) + EPS) * nw
    return on * _silu(z), s_next, tinv


def _gla_chunk(prm, cst, ins, st):
    a_up, a_bias, nw = prm
    q, k, v, z, ad = ins
    incl, _, _ = _tri(CH)
    la = jax.nn.log_sigmoid(_bmm(_heads_of(ad, a_up), a_up) + a_bias) * (1.0 / GLA_TAU)
    bc = _cumsum_rows(la)
    bl = jnp.sum(la, -2, keepdims=True)
    qe = q * (GLA_HEAD_K ** -0.5) * jnp.exp(bc)
    ke = k * jnp.exp(-bc)
    attn = jnp.where(incl, _bmm_nt(qe, ke), 0.0)
    o = _bmm_nt(qe, st) + _bmm(attn, v)
    st_next = st * jnp.exp(bl) + _bmm_tn(v, k * jnp.exp(bl - bc))
    on = o * lax.rsqrt(jnp.mean(o * o, -1, keepdims=True) + EPS) * nw
    return on * _silu(z), st_next


def _rwkv_chunk(prm, cst, ins, s, inv=None):
    r, v = ins[0], ins[2]
    incl, strict, _ = _tri(r.shape[-2])
    lw, kk, k2, m = _rwkv_pre(prm, ins)
    cum = _cumsum_rows(lw)
    ltot = jnp.sum(lw, -2, keepdims=True)
    n_t = -kk * jnp.exp(cum - lw)
    einv = jnp.exp(-cum)
    m_t = m * einv
    k_t = k2 * einv
    r_t = r * jnp.exp(cum)
    a_nm = jnp.where(strict, _hdot(n_t, m_t, tb=True), 0.0)
    a_nk = jnp.where(strict, _hdot(n_t, k_t, tb=True), 0.0)
    inv = _inv_unit_lower(-a_nm) if inv is None else _inv_reuse(-a_nm, inv)
    cm = _hdot(inv, _hdot(n_t, s, tb=True) + _bmm(a_nk, v))
    y = (_bmm_nt(r_t, s) + _bmm(jnp.where(incl, _hdot(r_t, m_t, tb=True), 0.0), cm)
         + _bmm(jnp.where(incl, _hdot(r_t, k_t, tb=True), 0.0), v))
    eend = jnp.exp(ltot - cum)
    s_next = s * jnp.exp(ltot) + _bmm_tn(cm, m * eend) + _bmm_tn(v, k2 * eend)
    return _rwkv_post(prm, ins, y, k2), s_next, inv


def _rwkv_pre(prm, ins):
    w0, w_up, a0, a_up, k_k, k_a = prm[:6]
    k, wd, ad = ins[1], ins[4], ins[5]
    lw = -math.exp(-0.5) * jax.nn.sigmoid(w0 + _bmm(_heads_of(jnp.tanh(wd), w_up), w_up))
    a = jax.nn.sigmoid(a0 + _bmm(_heads_of(ad, a_up), a_up))
    kk = k * k_k
    kk = kk * lax.rsqrt(jnp.sum(kk * kk, -1, keepdims=True) + EPS)
    k2 = k * (1.0 + (a - 1.0) * k_a)
    return lw, kk, k2, kk * a


def _rwkv_post(prm, ins, y, k2):
    r_k, ln_w, ln_b = prm[6:]
    r, v, z = ins[0], ins[2], ins[3]
    mean = jnp.mean(y, -1, keepdims=True)
    yc = y - mean
    var = jnp.mean(yc * yc, -1, keepdims=True)
    yn = yc * lax.rsqrt(var + RWKV_GN_EPS) * ln_w + ln_b
    bonus = jnp.sum(r * k2 * r_k, -1, keepdims=True) * v
    return (yn + bonus) * _silu(z)


@jax.custom_vjp
def _bmv(s, x):
    return jnp.sum(_r(s).astype(F32) * _r(x).astype(F32), -1, keepdims=True)


def _bmv_fwd(s, x):
    return _bmv(s, x), (s, x)


def _bmv_bwd(res, g):
    s, x = res
    return g * x, jnp.sum(_r(s).astype(F32) * _r(g).astype(F32), -2, keepdims=True)


_bmv.defvjp(_bmv_fwd, _bmv_bwd)


def _rwkv_steps(prm, cst, ins, s, steps):
    r, v = ins[0], ins[2]
    lw, kk, k2, m = _rwkv_pre(prm, ins)
    w = jnp.exp(lw)
    v_t = _t(v)
    lane = lax.broadcasted_iota(jnp.int32, (1, 1, CH), 2)
    y_t = jnp.zeros((s.shape[0], DH, CH), F32)
    for t in range(steps):
        e_t = (lane == t).astype(F32)
        row = (slice(None), slice(t, t + 1))
        sa = _bmv(s, -kk[row])
        s = s * w[row] + sa * m[row] + jnp.sum(v_t * e_t, -1, keepdims=True) * k2[row]
        y_t = y_t + _bmv(s, r[row]) * e_t
    return _rwkv_post(prm, ins, _t(y_t), k2)[:, :steps], s


def _rwkv_first_chunk(prm, cst, ins, s):
    k = RWKV_EXACT_STEPS
    y_head, s = _rwkv_steps(prm, cst, ins, s, k)
    y_tail, s, _ = _rwkv_chunk(prm, cst, [x[..., k:, :] for x in ins], s)
    return jnp.concatenate([y_head, y_tail], axis=-2), s


def _time_block(t):
    return TIME_BLOCK if t % TIME_BLOCK == 0 else t


def _load_chunk(ref, i):
    nb = ref.shape[1]
    if ref.shape[0] == NH:
        return jnp.concatenate([ref[:, b, pl.ds(i, CH), :] for b in range(nb)], axis=0)
    return ref[0, :, pl.ds(i, CH), :]


def _mixer_fwd(chunk_fn, name, ins, prm, cst, nb, t, first_fn=None, side=None, n_kept=0):
    tb = _time_block(t)
    nt, ncb, nch = t // tb, tb // CH, nb * NH
    n_in, n_prm, n_cst = len(ins), len(prm), len(cst)
    n_main, n_side = n_in + n_prm + n_cst, len(side.operands) if side else 0

    def body(*refs):
        in_refs = refs[:n_in]
        prm_refs = refs[n_in:n_in + n_prm]
        cst_refs = refs[n_in + n_prm:n_main]
        side_in = refs[n_main:n_main + n_side]
        y_ref, ck_ref = refs[n_main + n_side:n_main + n_side + 2]
        side_out = refs[n_main + n_side + 2:n_main + 2 * n_side + 2]
        s_scr = refs[n_main + 2 * n_side + 2]
        sems = refs[n_main + 2 * n_side + 3:]
        step_t = pl.program_id(0)

        if side is not None:
            @pl.when(step_t == 0)
            def _():
                _comm_start(side.copies(side_in, side_out, sems))

        @pl.when(step_t == 0)
        def _():
            s_scr[...] = jnp.zeros_like(s_scr)

        def chunk(c, i, fn=chunk_fn):
            s = s_scr[...]
            y, s_next, *kept = fn([jnp.tile(r[...], (nb, 1, 1)) for r in prm_refs],
                                  [jnp.tile(r[...], (nb, 1, 1)) for r in cst_refs],
                                  [_load_chunk(r, i) for r in in_refs], s)
            kept += [jnp.zeros_like(s)] * (n_kept - len(kept))
            for e, a in enumerate([s] + kept):
                ck_ref[c, e] = a
            for b in range(nb):
                y_ref[:, b, pl.ds(i, CH), :] = y[b * NH:(b + 1) * NH].astype(BF16)
            s_scr[...] = s_next

        def step(c, carry):
            chunk(c, pl.multiple_of(c * CH, CH))
            return carry

        if first_fn is None:
            lax.fori_loop(0, ncb, step, 0)
        else:
            @pl.when(step_t == 0)
            def _():
                chunk(0, 0, first_fn)

            @pl.when(step_t != 0)
            def _():
                chunk(0, 0)

            lax.fori_loop(1, ncb, step, 0)

        if side is not None:
            @pl.when(step_t == nt - 1)
            def _():
                _comm_wait(side.copies(side_in, side_out, sems))

    hbm = pl.BlockSpec(memory_space=pl.ANY)
    in_specs = [pl.BlockSpec((ng, nb, tb, DH), (lambda j, bi=bi: (bi, 0, j, 0))) for _, ng, bi in ins]
    in_specs += [pl.BlockSpec(p.shape, lambda j: (0, 0, 0)) for p in list(prm) + list(cst)]
    y, ck, *side_res = _pcall(
        body, name=name, grid=(nt,),
        in_specs=in_specs + [hbm] * n_side,
        out_specs=[pl.BlockSpec((NH, nb, tb, DH), lambda j: (0, 0, j, 0)),
                   pl.BlockSpec((ncb, 1 + n_kept, nch, DH, DH), lambda j: (j, 0, 0, 0, 0))] + [hbm] * n_side,
        out_shape=[jax.ShapeDtypeStruct((NH, nb, t, DH), BF16),
                   jax.ShapeDtypeStruct((t // CH, 1 + n_kept, nch, DH, DH), F32)]
        + (side.out_shapes if side else []),
        scratch_shapes=[pltpu.VMEM((nch, DH, DH), F32)] + (_comm_scratch(side) if side else []),
        compiler_params=_cparams(("arbitrary",)),
    )(*[a.reshape(a.shape[0], nb, t, DH) for a, _, _ in ins], *prm, *cst, *(side.operands if side else []))
    return y.reshape(NH, nb * t, DH), ck, side_res


def _mixer_bwd(chunk_fn, name, ins, prm, cst, ck, dy, dy_block, outs, routes, nb, t, first_fn=None, side=None):
    tb = _time_block(t)
    nt, ncb, nch = t // tb, tb // CH, nb * NH
    n_in, n_prm, n_cst, n_out = len(ins), len(prm), len(cst), len(outs)
    n_main, n_side = n_in + n_prm + n_cst + 2, len(side.operands) if side else 0
    n_kept = ck.shape[1] - 1

    def body(*refs):
        in_refs = refs[:n_in]
        prm_refs = refs[n_in:n_in + n_prm]
        cst_refs = refs[n_in + n_prm:n_in + n_prm + n_cst]
        ck_ref, dy_ref = refs[n_main - 2:n_main]
        side_in = refs[n_main:n_main + n_side]
        rest = refs[n_main + n_side:]
        out_refs = rest[:n_out]
        dprm_refs = rest[n_out:n_out + n_prm]
        side_out = rest[n_out + n_prm:n_out + n_prm + n_side]
        ds_scr = rest[n_out + n_prm + n_side]
        sems = rest[n_out + n_prm + n_side + 1:]
        step_t = pl.program_id(0)

        if side is not None:
            @pl.when(step_t == 0)
            def _():
                _comm_start(side.copies(side_in, side_out, sems))

        @pl.when(step_t == 0)
        def _():
            ds_scr[...] = jnp.zeros_like(ds_scr)
            for r in dprm_refs:
                r[...] = jnp.zeros_like(r)

        def chunk(c, i, fn=chunk_fn):
            cst_v = [jnp.tile(r[...], (nb, 1, 1)) for r in cst_refs]
            kept = [ck_ref[c, 1 + e] for e in range(n_kept)] if fn is chunk_fn else []
            _, vjp = jax.vjp(lambda p, x, s: fn(p, cst_v, x, s, *kept)[:2],
                             [jnp.tile(r[...], (nb, 1, 1)) for r in prm_refs],
                             [_load_chunk(r, i) for r in in_refs], ck_ref[c, 0])
            dy_c = jnp.concatenate([dy_ref[:, b, pl.ds(i, CH), :] for b in range(nb)], axis=0)
            d_prm, d_ins, d_s = vjp((dy_c, ds_scr[...]))
            for (oi, g0), r, g in zip(routes, in_refs, d_ins):
                o_ref = out_refs[oi]
                if r.shape[0] == NH:
                    for b in range(nb):
                        o_ref[g0:g0 + NH, b, pl.ds(i, CH), :] = g[b * NH:(b + 1) * NH].astype(o_ref.dtype)
                else:
                    o_ref[g0, :, pl.ds(i, CH), :] = g.astype(o_ref.dtype)
            for r, g in zip(dprm_refs, d_prm):
                r[...] += g
            ds_scr[...] = d_s

        def step(j, carry):
            c = ncb - 1 - j
            chunk(c, pl.multiple_of(c * CH, CH))
            return carry

        lax.fori_loop(0, ncb - 1, step, 0)
        if first_fn is None:
            chunk(0, 0)
        else:
            @pl.when(step_t == nt - 1)
            def _():
                chunk(0, 0, first_fn)

            @pl.when(step_t != nt - 1)
            def _():
                chunk(0, 0)

        if side is not None:
            @pl.when(step_t == nt - 1)
            def _():
                _comm_wait(side.copies(side_in, side_out, sems))

    def back(j):
        return nt - 1 - j

    hbm = pl.BlockSpec(memory_space=pl.ANY)
    in_specs = [pl.BlockSpec((ng, nb, tb, DH), (lambda j, bi=bi: (bi, 0, back(j), 0))) for _, ng, bi in ins]
    in_specs += [pl.BlockSpec(p.shape, lambda j: (0, 0, 0)) for p in list(prm) + list(cst)]
    in_specs += [pl.BlockSpec((ncb, 1 + n_kept, nch, DH, DH), lambda j: (back(j), 0, 0, 0, 0)),
                 pl.BlockSpec((NH, nb, tb, DH), lambda j: (dy_block, 0, back(j), 0))]
    out_specs = [pl.BlockSpec((ng, nb, tb, DH), lambda j: (0, 0, back(j), 0)) for ng, _ in outs]
    out_specs += [pl.BlockSpec((nch,) + p.shape[1:], lambda j: (0, 0, 0)) for p in prm]
    out_shape = [jax.ShapeDtypeStruct((ng, nb, t, DH), dt) for ng, dt in outs]
    out_shape += [jax.ShapeDtypeStruct((nch,) + p.shape[1:], F32) for p in prm]
    res = _pcall(
        body, name=name, grid=(nt,),
        in_specs=in_specs + [hbm] * n_side, out_specs=out_specs + [hbm] * n_side,
        out_shape=out_shape + (side.out_shapes if side else []),
        scratch_shapes=[pltpu.VMEM((nch, DH, DH), F32)] + (_comm_scratch(side) if side else []),
        compiler_params=_cparams(("arbitrary",)),
    )(*[a.reshape(a.shape[0], nb, t, DH) for a, _, _ in ins], *prm, *cst, ck, dy.reshape(dy.shape[0], nb, t, DH),
      *(side.operands if side else []))
    d_outs = [o.reshape(o.shape[0], nb * t, DH) for o in res[:n_out]]
    d_prm = [g.reshape((nb,) + p.shape) for g, p in zip(res[n_out:n_out + n_prm], prm)]
    return d_outs, d_prm, res[n_out + n_prm:]


def _shift_down(x, s):
    if s == 0:
        return x
    row = lax.broadcasted_iota(jnp.int32, x.shape, 0)
    return jnp.where(row < s, 0.0, pltpu.roll(x, s, 0))


def _shift_up(x, s):
    if s == 0:
        return x
    t = x.shape[0]
    row = lax.broadcasted_iota(jnp.int32, x.shape, 0)
    return jnp.where(row >= t - s, 0.0, pltpu.roll(x, t - s, 0))


def _conv_fwd(p, g0, ng, w, nb, t, name):
    taps = w.shape[1]

    def body(x_ref, w_ref, y_ref):
        x = x_ref[...]
        acc = w_ref[taps - 1:taps, :] * x
        for i in range(taps - 1):
            acc = acc + w_ref[i:i + 1, :] * _shift_down(x, taps - 1 - i)
        y_ref[...] = acc

    return _pcall(
        body, name=name, grid=(ng, nb),
        in_specs=[pl.BlockSpec((None, t, DH), lambda g, b: (g0 + g, b, 0)),
                  pl.BlockSpec((None, taps, DH), lambda g, b: (g, 0, 0))],
        out_specs=pl.BlockSpec((None, t, DH), lambda g, b: (g, b, 0)),
        out_shape=jax.ShapeDtypeStruct((ng, nb * t, DH), F32),
        compiler_params=_cparams(("parallel", "parallel")),
    )(p, w)


def _conv_bwd(p, g0, ng, w, dy, nb, t, name):
    taps = w.shape[1]

    def body(x_ref, w_ref, dy_ref, dx_ref, dw_ref):
        x = x_ref[...]
        d = dy_ref[...]
        acc = w_ref[taps - 1:taps, :] * d
        dw_ref[taps - 1:taps, :] = jnp.sum(d * x, 0, keepdims=True)
        for i in range(taps - 1):
            s = taps - 1 - i
            acc = acc + w_ref[i:i + 1, :] * _shift_up(d, s)
            dw_ref[i:i + 1, :] = jnp.sum(d * _shift_down(x, s), 0, keepdims=True)
        dx_ref[...] = acc.astype(BF16)

    return _pcall(
        body, name=name, grid=(ng, nb),
        in_specs=[pl.BlockSpec((None, t, DH), lambda g, b: (g0 + g, b, 0)),
                  pl.BlockSpec((None, taps, DH), lambda g, b: (g, 0, 0)),
                  pl.BlockSpec((None, t, DH), lambda g, b: (g, b, 0))],
        out_specs=[pl.BlockSpec((None, t, DH), lambda g, b: (g, b, 0)),
                   pl.BlockSpec((None, None, taps, DH), lambda g, b: (g, b, 0, 0))],
        out_shape=[jax.ShapeDtypeStruct((ng, nb * t, DH), BF16),
                   jax.ShapeDtypeStruct((ng, nb, taps, DH), F32)],
        compiler_params=_cparams(("parallel", "parallel")),
    )(p, w, dy)


def _mix_group(g):
    return jnp.where(g < 16, G_RWKV + g, G_RWKV_WD + g - 16)


def _mix_fwd(p, mu, nb, t, name):
    def body(x_ref, mu_ref, y_ref):
        x = x_ref[...]
        y_ref[...] = x + mu_ref[...] * (_shift_down(x, 1) - x)

    return _pcall(
        body, name=name, grid=(18, nb),
        in_specs=[pl.BlockSpec((None, t, DH), lambda g, b: (_mix_group(g), b, 0)),
                  pl.BlockSpec((None, 1, DH), lambda g, b: (g, 0, 0))],
        out_specs=pl.BlockSpec((None, t, DH), lambda g, b: (g, b, 0)),
        out_shape=jax.ShapeDtypeStruct((18, nb * t, DH), F32),
        compiler_params=_cparams(("parallel", "parallel")),
    )(p, mu)


def _mix_bwd(p, mu, dy, nb, t, name):
    def body(x_ref, mu_ref, dy_ref, dx_ref, dmu_ref):
        x = x_ref[...]
        muv = mu_ref[...]
        d = dy_ref[...]
        dx_ref[...] = (d * (1.0 - muv) + _shift_up(d * muv, 1)).astype(BF16)
        dmu_ref[...] = jnp.sum(d * (_shift_down(x, 1) - x), 0, keepdims=True)

    return _pcall(
        body, name=name, grid=(18, nb),
        in_specs=[pl.BlockSpec((None, t, DH), lambda g, b: (_mix_group(g), b, 0)),
                  pl.BlockSpec((None, 1, DH), lambda g, b: (g, 0, 0)),
                  pl.BlockSpec((None, t, DH), lambda g, b: (g, b, 0))],
        out_specs=[pl.BlockSpec((None, t, DH), lambda g, b: (g, b, 0)),
                   pl.BlockSpec((None, None, 1, DH), lambda g, b: (g, b, 0, 0))],
        out_shape=[jax.ShapeDtypeStruct((18, nb * t, DH), BF16),
                   jax.ShapeDtypeStruct((18, nb, 1, DH), F32)],
        compiler_params=_cparams(("parallel", "parallel")),
    )(p, mu, dy)


def _sc_fwd(p, w, nb, t, name):
    def body(p_ref, w_ref, y_ref):
        u = p_ref[1] * p_ref[2]
        conv = w_ref[2:3, :] * u + w_ref[1:2, :] * _shift_down(u, 1) + w_ref[0:1, :] * _shift_down(u, 2)
        y_ref[...] = (p_ref[0] * conv * _silu(p_ref[3])).astype(BF16)

    return _pcall(
        body, name=name, grid=(NH, nb),
        in_specs=[pl.BlockSpec((4, t, DH), lambda j, b: (G_SC // 4 + j, b, 0)),
                  pl.BlockSpec((None, SC_TAPS, DH), lambda j, b: (j, 0, 0))],
        out_specs=pl.BlockSpec((None, t, DH), lambda j, b: (j, b, 0)),
        out_shape=jax.ShapeDtypeStruct((NH, nb * t, DH), BF16),
        compiler_params=_cparams(("parallel", "parallel")),
    )(p, w)


def _sc_bwd(p, w, dy, nb, t, name):
    def body(p_ref, w_ref, dy_ref, dp_ref, dw_ref):
        bg, cg, xg, z = p_ref[0], p_ref[1], p_ref[2], p_ref[3]
        d = dy_ref[...]
        u = cg * xg
        u1 = _shift_down(u, 1)
        u2 = _shift_down(u, 2)
        conv = w_ref[2:3, :] * u + w_ref[1:2, :] * u1 + w_ref[0:1, :] * u2
        sg = jax.nn.sigmoid(z)
        sz = z * sg
        dp_ref[0] = (d * conv * sz).astype(BF16)
        dp_ref[3] = (d * bg * conv * (sg * (1.0 + z * (1.0 - sg)))).astype(BF16)
        dconv = d * bg * sz
        du = w_ref[2:3, :] * dconv + w_ref[1:2, :] * _shift_up(dconv, 1) + w_ref[0:1, :] * _shift_up(dconv, 2)
        dp_ref[1] = (du * xg).astype(BF16)
        dp_ref[2] = (du * cg).astype(BF16)
        dw_ref[2:3, :] = jnp.sum(dconv * u, 0, keepdims=True)
        dw_ref[1:2, :] = jnp.sum(dconv * u1, 0, keepdims=True)
        dw_ref[0:1, :] = jnp.sum(dconv * u2, 0, keepdims=True)

    return _pcall(
        body, name=name, grid=(NH, nb),
        in_specs=[pl.BlockSpec((4, t, DH), lambda j, b: (G_SC // 4 + j, b, 0)),
                  pl.BlockSpec((None, SC_TAPS, DH), lambda j, b: (j, 0, 0)),
                  pl.BlockSpec((None, t, DH), lambda j, b: (8 + j, b, 0))],
        out_specs=[pl.BlockSpec((4, t, DH), lambda j, b: (j, b, 0)),
                   pl.BlockSpec((None, None, SC_TAPS, DH), lambda j, b: (j, b, 0, 0))],
        out_shape=[jax.ShapeDtypeStruct((4 * NH, nb * t, DH), BF16),
                   jax.ShapeDtypeStruct((NH, nb, SC_TAPS, DH), F32)],
        compiler_params=_cparams(("parallel", "parallel")),
    )(p, w, dy)


def _row_tile(n):
    return 1024 if n % 1024 == 0 else n


def _regroup_in(w_all, name):
    tr = 256
    gs = GROUPS_PER_STEP

    def body(w_ref, o_ref):
        for g in _PADDED_GROUPS:
            o_ref[g // gs, :, DH * (g % gs):DH * (g % gs + 1)] = jnp.zeros((tr, DH), BF16)
        for g, a, d, off, ln in _SEGMENTS:
            lane = DH * (g % gs) + a
            o_ref[g // gs, :, lane:lane + ln] = w_ref[d, :, off:off + ln].astype(BF16)

    return _pcall(
        body, name=name, grid=(D_MODEL // tr,),
        in_specs=[pl.BlockSpec((N_DEV, tr, SHARD_COLS), lambda i: (0, i, 0))],
        out_specs=pl.BlockSpec((N_GROUPS // gs, tr, gs * DH), lambda i: (0, i, 0)),
        out_shape=jax.ShapeDtypeStruct((N_GROUPS // gs, D_MODEL, gs * DH), BF16),
        compiler_params=_cparams(("parallel",)),
    )(w_all)


def _regroup_out(dwg, name):
    tr = 256
    gs = GROUPS_PER_STEP

    def body(g_ref, o_ref):
        for g, a, d, off, ln in _SEGMENTS:
            lane = DH * (g % gs) + a
            o_ref[d, :, off:off + ln] = g_ref[g // gs, :, lane:lane + ln].astype(BF16)

    return _pcall(
        body, name=name, grid=(D_MODEL // tr,),
        in_specs=[pl.BlockSpec((N_GROUPS // gs, tr, gs * DH), lambda i: (0, i, 0))],
        out_specs=pl.BlockSpec((N_DEV, tr, SHARD_COLS), lambda i: (0, i, 0)),
        out_shape=jax.ShapeDtypeStruct((N_DEV, D_MODEL, SHARD_COLS), BF16),
        compiler_params=_cparams(("parallel",)),
    )(dwg)


def _norm_proj(x, pre_w, w_g, name):
    n = x.shape[0]
    tm = _row_tile(n)
    gs = GROUPS_PER_STEP

    def body(x_ref, pw_ref, w_ref, h_ref, p_ref):
        @pl.when(pl.program_id(1) == 0)
        def _():
            xv = x_ref[...]
            h = xv * lax.rsqrt(jnp.mean(xv * xv, -1, keepdims=True) + EPS) * pw_ref[...]
            h_ref[...] = h.astype(BF16)

        r = jnp.dot(h_ref[...], w_ref[...], preferred_element_type=F32)
        for k in range(gs):
            p_ref[k] = r[:, DH * k:DH * (k + 1)]

    return _pcall(
        body, name=name, grid=(n // tm, N_GROUPS // gs),
        in_specs=[pl.BlockSpec((tm, D_MODEL), lambda i, j: (i, 0)),
                  pl.BlockSpec((1, D_MODEL), lambda i, j: (0, 0)),
                  pl.BlockSpec((None, D_MODEL, gs * DH), lambda i, j: (j, 0, 0))],
        out_specs=[pl.BlockSpec((tm, D_MODEL), lambda i, j: (i, 0)),
                   pl.BlockSpec((gs, tm, DH), lambda i, j: (j, i, 0))],
        out_shape=[jax.ShapeDtypeStruct((n, D_MODEL), BF16),
                   jax.ShapeDtypeStruct((N_GROUPS, n, DH), F32)],
        compiler_params=_cparams(("parallel", "arbitrary")),
    )(x, pre_w, w_g)


def _out_proj_norm(ys, wout_g, x, post_w, name):
    n = x.shape[0]
    tm = _row_tile(n)

    def body(y0, y1, y2, y3, w_ref, x_ref, pw_ref, out_ref, xn_ref):
        y = jnp.concatenate([yr[h] for yr in (y0, y1, y2, y3) for h in range(NH)], axis=-1)
        acc = jnp.dot(y, w_ref[...], preferred_element_type=F32)
        out_ref[...] = acc
        xn_ref[...] = x_ref[...] + acc * lax.rsqrt(jnp.mean(acc * acc, -1, keepdims=True) + EPS) * pw_ref[...]

    yspec = pl.BlockSpec((NH, tm, DH), lambda i: (0, i, 0))
    rows = pl.BlockSpec((tm, D_MODEL), lambda i: (i, 0))
    return _pcall(
        body, name=name, grid=(n // tm,),
        in_specs=[yspec] * 4 + [pl.BlockSpec((D_MODEL, D_MODEL), lambda i: (0, 0)), rows,
                                pl.BlockSpec((1, D_MODEL), lambda i: (0, 0))],
        out_specs=[rows, rows],
        out_shape=[jax.ShapeDtypeStruct((n, D_MODEL), F32)] * 2,
        compiler_params=_cparams(("parallel",)),
    )(*ys, wout_g.reshape(D_MODEL, D_MODEL), x, post_w)


def _loss_grad(x, tgt, name):
    n = x.shape[0]
    tm = _row_tile(n)

    def body(x_ref, t_ref, dx_ref, l_ref):
        @pl.when(pl.program_id(0) == 0)
        def _():
            l_ref[...] = jnp.zeros_like(l_ref)

        e = x_ref[...] - t_ref[...]
        dx_ref[...] = e * (1.0 / D_MODEL)
        l_ref[...] += jnp.sum(jnp.sum(e * e, -1, keepdims=True), 0, keepdims=True) * (0.5 / D_MODEL)

    rows = pl.BlockSpec((tm, D_MODEL), lambda i: (i, 0))
    return _pcall(
        body, name=name, grid=(n // tm,),
        in_specs=[rows, rows],
        out_specs=[rows, pl.BlockSpec((1, 128), lambda i: (0, 0))],
        out_shape=[jax.ShapeDtypeStruct((n, D_MODEL), F32), jax.ShapeDtypeStruct((1, 128), F32)],
        compiler_params=_cparams(("arbitrary",)),
    )(x, tgt)


def _rmsnorm_bwd(xv, w, d):
    r = lax.rsqrt(jnp.mean(xv * xv, -1, keepdims=True) + EPS)
    xh = xv * r
    dxh = d * w
    dx = r * (dxh - xh * jnp.mean(dxh * xh, -1, keepdims=True))
    return dx, d * xh


def _post_bwd(dxn, out, post_w, wout_g, name):
    n = dxn.shape[0]
    tm = _row_tile(n)

    def body(d_ref, o_ref, pw_ref, w_ref, do_ref, dy_ref, dpw_ref):
        @pl.when(pl.program_id(0) == 0)
        def _():
            dpw_ref[...] = jnp.zeros_like(dpw_ref)

        dout, dw_rows = _rmsnorm_bwd(o_ref[...], pw_ref[...], d_ref[...])
        dpw_ref[...] += jnp.sum(dw_rows, 0, keepdims=True)
        db = dout.astype(BF16)
        do_ref[...] = db
        dy = lax.dot_general(db, w_ref[...], (((1,), (1,)), ((), ())), preferred_element_type=F32)
        for g in range(4 * NH):
            dy_ref[g] = dy[:, DH * g:DH * (g + 1)]

    rows = pl.BlockSpec((tm, D_MODEL), lambda i: (i, 0))
    vec = pl.BlockSpec((1, D_MODEL), lambda i: (0, 0))
    return _pcall(
        body, name=name, grid=(n // tm,),
        in_specs=[rows, rows, vec, pl.BlockSpec((D_MODEL, D_MODEL), lambda i: (0, 0))],
        out_specs=[rows, pl.BlockSpec((4 * NH, tm, DH), lambda i: (0, i, 0)), vec],
        out_shape=[jax.ShapeDtypeStruct((n, D_MODEL), BF16),
                   jax.ShapeDtypeStruct((4 * NH, n, DH), F32),
                   jax.ShapeDtypeStruct((1, D_MODEL), F32)],
        compiler_params=_cparams(("arbitrary",)),
    )(dxn, out, post_w, wout_g.reshape(D_MODEL, D_MODEL))


def _dwout(ys, dout, name):
    n = dout.shape[0]
    tm = _row_tile(n)

    def body(y0, y1, y2, y3, d_ref, dw_ref):
        @pl.when(pl.program_id(0) == 0)
        def _():
            dw_ref[...] = jnp.zeros_like(dw_ref)

        y = jnp.concatenate([yr[h] for yr in (y0, y1, y2, y3) for h in range(NH)], axis=-1)
        dw_ref[...] += lax.dot_general(y, d_ref[...], (((0,), (0,)), ((), ())), preferred_element_type=F32)

    yspec = pl.BlockSpec((NH, tm, DH), lambda i: (0, i, 0))
    return _pcall(
        body, name=name, grid=(n // tm,),
        in_specs=[yspec] * 4 + [pl.BlockSpec((tm, D_MODEL), lambda i: (i, 0))],
        out_specs=pl.BlockSpec((D_MODEL, D_MODEL), lambda i: (0, 0)),
        out_shape=jax.ShapeDtypeStruct((D_MODEL, D_MODEL), F32),
        compiler_params=_cparams(("arbitrary",)),
    )(*ys, dout)


def _source_specs(sources, rows_first):
    gs = GROUPS_PER_STEP
    spans, specs, j0 = [], [], 0
    for a in sources:
        nblk = a.shape[0] // gs
        spans.append((j0, j0 + nblk))
        shape = (gs, _row_tile(a.shape[1]), DH)

        def blk(j, j0=j0, nblk=nblk):
            return jnp.clip(j - j0, 0, nblk - 1)

        if rows_first:
            specs.append(pl.BlockSpec(shape, (lambda i, j, blk=blk: (blk(j), i, 0))))
        else:
            specs.append(pl.BlockSpec(shape, (lambda j, i, blk=blk: (blk(j), i, 0))))
        j0 += nblk
    return spans, specs


def _dh_prenorm_bwd(sources, w_g, x, pre_w, dxn, name, side=None):
    n = x.shape[0]
    tm = _row_tile(n)
    gs = GROUPS_PER_STEP
    nj = N_GROUPS // gs
    ni = n // tm
    spans, src_specs = _source_specs(sources, True)
    ns = len(sources)
    n_side = len(side.operands) if side else 0

    def body(*refs):
        src = refs[:ns]
        w_ref, x_ref, pw_ref, d_ref = refs[ns:ns + 4]
        side_in = refs[ns + 4:ns + 4 + n_side]
        dx_ref, dpw_ref = refs[ns + 4 + n_side:ns + 6 + n_side]
        side_out = refs[ns + 6 + n_side:ns + 6 + 2 * n_side]
        acc = refs[ns + 6 + 2 * n_side]
        sems = refs[ns + 7 + 2 * n_side:]
        i, j = pl.program_id(0), pl.program_id(1)

        if side is not None:
            @pl.when((i == 0) & (j == 0))
            def _():
                _comm_start(side.copies(side_in, side_out, sems))

        @pl.when((i == 0) & (j == 0))
        def _():
            dpw_ref[...] = jnp.zeros_like(dpw_ref)

        @pl.when(j == 0)
        def _():
            acc[...] = jnp.zeros_like(acc)

        for s_ref, (lo, hi) in zip(src, spans):
            @pl.when((j >= lo) & (j < hi))
            def _(s_ref=s_ref):
                four = jnp.concatenate([s_ref[k] for k in range(gs)], axis=-1)
                acc[...] += lax.dot_general(four, w_ref[...], (((1,), (1,)), ((), ())), preferred_element_type=F32)

        @pl.when(j == nj - 1)
        def _():
            dx, dw_rows = _rmsnorm_bwd(x_ref[...], pw_ref[...], acc[...])
            dx_ref[...] = d_ref[...] + dx
            dpw_ref[...] += jnp.sum(dw_rows, 0, keepdims=True)

        if side is not None:
            @pl.when((i == ni - 1) & (j == nj - 1))
            def _():
                _comm_wait(side.copies(side_in, side_out, sems))

    hbm = pl.BlockSpec(memory_space=pl.ANY)
    rows = pl.BlockSpec((tm, D_MODEL), lambda i, j: (i, 0))
    vec = pl.BlockSpec((1, D_MODEL), lambda i, j: (0, 0))
    dx, dpw, *side_res = _pcall(
        body, name=name, grid=(ni, nj),
        in_specs=src_specs + [pl.BlockSpec((None, D_MODEL, gs * DH), lambda i, j: (j, 0, 0)), rows, vec, rows]
        + [hbm] * n_side,
        out_specs=[rows, vec] + [hbm] * n_side,
        out_shape=[jax.ShapeDtypeStruct((n, D_MODEL), F32), jax.ShapeDtypeStruct((1, D_MODEL), F32)]
        + (side.out_shapes if side else []),
        scratch_shapes=[pltpu.VMEM((tm, D_MODEL), F32)] + (_comm_scratch(side) if side else []),
        compiler_params=_cparams(("arbitrary", "arbitrary")),
    )(*sources, w_g, x, pre_w, dxn, *(side.operands if side else []))
    return dx, dpw, side_res


def _dwin(hb, sources, name):
    n = hb.shape[0]
    tm = _row_tile(n)
    gs = GROUPS_PER_STEP
    ni, nj = n // tm, N_GROUPS // gs
    spans, src_specs = _source_specs(sources, True)
    ns = len(sources)

    def body(*refs):
        h_ref = refs[0]
        src = refs[1:1 + ns]
        out_ref, acc, sem = refs[1 + ns:]
        i, j = pl.program_id(0), pl.program_id(1)

        @pl.when((i == 0) & (j == 0))
        def _():
            acc[...] = jnp.zeros_like(acc)

        h = h_ref[...]
        for s_ref, (lo, hi) in zip(src, spans):
            @pl.when((j >= lo) & (j < hi))
            def _(s_ref=s_ref):
                four = jnp.concatenate([s_ref[k] for k in range(gs)], axis=-1)
                acc[j] += jnp.dot(h, four, preferred_element_type=F32)

        @pl.when((i == ni - 1) & (j == nj - 1))
        def _():
            done = pltpu.make_async_copy(acc, out_ref, sem)
            done.start()
            done.wait()

    return _pcall(
        body, name=name, grid=(ni, nj),
        in_specs=[pl.BlockSpec((D_MODEL, tm), lambda i, j: (0, i))] + src_specs,
        out_specs=pl.BlockSpec(memory_space=pl.ANY),
        out_shape=jax.ShapeDtypeStruct((nj, D_MODEL, gs * DH), F32),
        scratch_shapes=[pltpu.VMEM((nj, D_MODEL, gs * DH), F32), pltpu.SemaphoreType.DMA],
        compiler_params=_cparams(("arbitrary", "arbitrary")),
    )(jnp.transpose(hb), *sources)


def _adamw_math(w, g, m, v):
    c1 = 1.0 - ADAM_B1 ** ADAM_STEP
    c2 = 1.0 - ADAM_B2 ** ADAM_STEP
    nm = ADAM_B1 * m + (1.0 - ADAM_B1) * g
    nv = ADAM_B2 * v + (1.0 - ADAM_B2) * (g * g)
    return -ADAM_LR * ((nm / c1) / (jnp.sqrt(nv / c2) + ADAM_EPS) + ADAM_WD * w), nm, nv


def _adamw(w, g, m, v, name):
    r, c = w.shape
    tr = 256 if r % 256 == 0 else r

    def body(w_ref, g_ref, m_ref, v_ref, d_ref, nm_ref, nv_ref):
        d_ref[...], nm_ref[...], nv_ref[...] = _adamw_math(w_ref[...], g_ref[...], m_ref[...], v_ref[...])

    spec = pl.BlockSpec((tr, c), lambda i: (i, 0))
    return _pcall(
        body, name=name, grid=(r // tr,),
        in_specs=[spec] * 4, out_specs=[spec] * 3,
        out_shape=[jax.ShapeDtypeStruct((r, c), F32)] * 3,
        compiler_params=_cparams(("parallel",)),
    )(w, g, m, v)


def _sum_adamw(parts, w, m, v, name):
    r, c = w.shape
    tr = 128 if r % 128 == 0 else r

    def body(p_ref, w_ref, m_ref, v_ref, g_ref, d_ref, nm_ref, nv_ref):
        g = p_ref[0].astype(F32)
        for k in range(1, N_DEV):
            g = g + p_ref[k].astype(F32)
        g_ref[...] = g
        d_ref[...], nm_ref[...], nv_ref[...] = _adamw_math(w_ref[...], g, m_ref[...], v_ref[...])

    spec = pl.BlockSpec((tr, c), lambda i: (i, 0))
    return _pcall(
        body, name=name, grid=(r // tr,),
        in_specs=[pl.BlockSpec((N_DEV, tr, c), lambda i: (0, i, 0))] + [spec] * 3, out_specs=[spec] * 4,
        out_shape=[jax.ShapeDtypeStruct((r, c), F32)] * 4,
        compiler_params=_cparams(("parallel",)),
    )(parts, w, m, v)


def _me():
    return lax.axis_index("x"), lax.axis_index("y"), lax.axis_index("c")


def _flat(x, y, c):
    return 4 * x + 2 * y + c


def _peer(k):
    x, y, c = _me()
    return (x ^ ((k >> 2) & 1), y ^ ((k >> 1) & 1), c ^ (k & 1))


def _gather_plan(blocks):
    def copies(x_refs, out_refs, sems):
        send_sems, recv_sems, local_sems = sems
        me = _flat(*_me())
        local = [pltpu.make_async_copy(x, o.at[me], local_sems.at[a]) for a, (x, o) in enumerate(zip(x_refs, out_refs))]
        outgoing, incoming = [], []
        for k in range(1, N_DEV):
            src = _flat(*_peer(k))
            for a, (x, o) in enumerate(zip(x_refs, out_refs)):
                for slot, group in ((me, outgoing), (src, incoming)):
                    group.append(pltpu.make_async_remote_copy(
                        src_ref=x, dst_ref=o.at[slot], send_sem=send_sems.at[a, k - 1], recv_sem=recv_sems.at[a, k - 1],
                        device_id=_peer(k), device_id_type=MESH))
        return local, outgoing, incoming

    return _Comm(list(blocks), [jax.ShapeDtypeStruct((N_DEV,) + b.shape, b.dtype) for b in blocks], copies)


def _exchange_plan(sends):
    def copies(s_refs, out_refs, sems):
        send_sems, recv_sems, local_sems = sems
        me = _flat(*_me())
        local = [pltpu.make_async_copy(s.at[me], o.at[0], local_sems.at[i]) for i, (s, o) in enumerate(zip(s_refs, out_refs))]
        outgoing = []
        for k in range(1, N_DEV):
            to = _flat(*_peer(k))
            for i, (s, o) in enumerate(zip(s_refs, out_refs)):
                outgoing.append(pltpu.make_async_remote_copy(
                    src_ref=s.at[to], dst_ref=o.at[k], send_sem=send_sems.at[i, k - 1], recv_sem=recv_sems.at[i, k - 1],
                    device_id=_peer(k), device_id_type=MESH))
        return local, outgoing, outgoing

    return _Comm(list(sends), [jax.ShapeDtypeStruct(s.shape, s.dtype) for s in sends], copies)


def _comm_scratch(plan):
    n = len(plan.operands)
    return [pltpu.SemaphoreType.DMA((n, N_DEV - 1)), pltpu.SemaphoreType.DMA((n, N_DEV - 1)),
            pltpu.SemaphoreType.DMA((n,))]


def _comm_start(copies):
    local, outgoing, _ = copies
    for cp in local + outgoing:
        cp.start()


def _comm_wait(copies):
    local, outgoing, incoming = copies
    for cp in incoming:
        cp.wait_recv()
    for cp in outgoing:
        cp.wait_send()
    for cp in local:
        cp.wait()


def _run_comm(plan, name):
    n = len(plan.operands)

    def body(*refs):
        copies = plan.copies(refs[:n], refs[n:2 * n], refs[2 * n:])
        _comm_start(copies)
        _comm_wait(copies)

    return _pcall(
        body, name=name,
        in_specs=[pl.BlockSpec(memory_space=pl.ANY)] * n,
        out_specs=[pl.BlockSpec(memory_space=pl.ANY)] * n,
        out_shape=plan.out_shapes,
        scratch_shapes=_comm_scratch(plan),
    )(*plan.operands)


def _all_gather_two_level(blocks, name):
    na = len(blocks)

    def body(*refs):
        x_refs, out_refs = refs[:na], refs[na:2 * na]
        send_sems, recv_sems, local_sems = refs[2 * na:]
        x, y, c = _me()
        me, sibling = (x, y, c), (x, y, 1 - c)
        chips = [(1 - x, y), (x, 1 - y), (1 - x, 1 - y)]

        def copy(a, k, block, to, own=False):
            slot = out_refs[a].at[_flat(*block)]
            return pltpu.make_async_remote_copy(
                src_ref=x_refs[a] if own else slot, dst_ref=slot, send_sem=send_sems.at[a, k],
                recv_sem=recv_sems.at[a, k], device_id=to, device_id_type=MESH)

        mine = [pltpu.make_async_copy(x_refs[a], out_refs[a].at[_flat(*me)], local_sems.at[a]) for a in range(na)]
        first = [copy(a, 0, me, sibling, own=True) for a in range(na)]
        first += [copy(a, 1 + j, me, (*chip, c), own=True) for j, chip in enumerate(chips) for a in range(na)]
        for cp in mine + first:
            cp.start()
        passed = []
        for j, chip in enumerate(chips):
            for a in range(na):
                copy(a, 1 + j, (*chip, c), me).wait_recv()
                cp = copy(a, 4 + j, (*chip, c), sibling)
                cp.start()
                passed.append(cp)
        for a in range(na):
            copy(a, 0, sibling, me).wait_recv()
        for j, chip in enumerate(chips):
            for a in range(na):
                copy(a, 4 + j, (*chip, 1 - c), me).wait_recv()
        for cp in first + passed:
            cp.wait_send()
        for cp in mine:
            cp.wait()

    return _pcall(
        body, name=name,
        in_specs=[pl.BlockSpec(memory_space=pl.ANY)] * na,
        out_specs=[pl.BlockSpec(memory_space=pl.ANY)] * na,
        out_shape=[jax.ShapeDtypeStruct((N_DEV,) + b.shape, b.dtype) for b in blocks],
        scratch_shapes=[pltpu.SemaphoreType.DMA((na, N_DEV - 1)), pltpu.SemaphoreType.DMA((na, N_DEV - 1)),
                        pltpu.SemaphoreType.DMA((na,))],
    )(*blocks)


def _sum_slots(a, name):
    r = a.shape[1]

    def body(a_ref, o_ref):
        acc = a_ref[0]
        for d in range(1, N_DEV):
            acc = acc + a_ref[d]
        o_ref[...] = acc

    return _pcall(body, name=name, out_shape=jax.ShapeDtypeStruct((r, 128), F32), compiler_params=_cparams())(a)


def _all_reduce_small(blk, name):
    r = blk.shape[0]

    def body(x_ref, out_ref, gath, send_sems, recv_sems):
        me = _flat(*_me())
        gath[me] = x_ref[...]
        copies = []
        for k in range(1, N_DEV):
            cp = pltpu.make_async_remote_copy(
                src_ref=x_ref, dst_ref=gath.at[me],
                send_sem=send_sems.at[k - 1], recv_sem=recv_sems.at[k - 1],
                device_id=_peer(k), device_id_type=MESH)
            cp.start()
            copies.append(cp)
        for k in range(1, N_DEV):
            src = _flat(*_peer(k))
            pltpu.make_async_remote_copy(
                src_ref=x_ref, dst_ref=gath.at[src],
                send_sem=send_sems.at[k - 1], recv_sem=recv_sems.at[k - 1],
                device_id=_peer(k), device_id_type=MESH).wait_recv()
        for cp in copies:
            cp.wait_send()
        acc = gath[0]
        for d in range(1, N_DEV):
            acc = acc + gath[d]
        out_ref[...] = acc

    return _pcall(
        body, name=name,
        in_specs=[pl.BlockSpec(memory_space=pltpu.VMEM)],
        out_specs=pl.BlockSpec(memory_space=pltpu.VMEM),
        out_shape=jax.ShapeDtypeStruct((r, 128), F32),
        scratch_shapes=[pltpu.VMEM((N_DEV, r, 128), F32),
                        pltpu.SemaphoreType.DMA((N_DEV - 1,)), pltpu.SemaphoreType.DMA((N_DEV - 1,))],
    )(blk)


def _heads(vec):
    return vec.reshape(NH, 1, DH)


def _rep(vec4):
    return jnp.broadcast_to(vec4.reshape(NH, 1, 1), (NH, 1, DH))


def _onehot_lane(offset):
    m = np.zeros((NH, 1, DH), np.float32)
    for h in range(NH):
        m[h, 0, offset + h] = 1.0
    return jnp.asarray(m)


_TINY = (("gdn_conv_w", (DEPTH, 4, 96)), ("rwkv_w_up", (DEPTH, 64, 32)), ("rwkv_a_up", (DEPTH, 64, 32)),
         ("sc_conv_w", (DEPTH, 3, 32)))
_TINY_ROWS = -(-sum(int(np.prod(s)) for _, s in _TINY) // 1024) * 8


def _pack_rows(arrays, rows, fill=0.0):
    flat = jnp.concatenate([a.reshape(-1) for a in arrays])
    return jnp.pad(flat, (0, rows * 128 - flat.shape[0]), constant_values=fill).reshape(rows, 128)


def _unpack_rows(p, named_shapes):
    lead = p.shape[:-2]
    flat = p.reshape(lead + (-1,))
    out, o = {}, 0
    for n, s in named_shapes:
        size = int(np.prod(s))
        out[n] = flat[..., o:o + size].reshape(lead + tuple(s))
        o += size
    return out


def _gather_last(a):
    return jnp.transpose(a, (1, 0, 2)).reshape(a.shape[1], -1)


def _split_last(a):
    r, c8 = a.shape
    return jnp.transpose(a.reshape(r, N_DEV, c8 // N_DEV), (1, 0, 2))


_SMALL = (("pre_norm_w", (DEPTH, 1024)), ("gdn_a_log", (DEPTH, 4)), ("gdn_dt_bias", (DEPTH, 4)),
          ("gdn_norm_w", (DEPTH, 64)), ("rwkv_mu", (DEPTH, 1152)), ("rwkv_w0", (DEPTH, 256)),
          ("rwkv_a0", (DEPTH, 256)), ("rwkv_k_k", (DEPTH, 256)), ("rwkv_k_a", (DEPTH, 256)),
          ("rwkv_r_k", (DEPTH, 256)), ("rwkv_ln_w", (DEPTH, 256)), ("rwkv_ln_b", (DEPTH, 256)),
          ("gla_a_up", (DEPTH, 16, 128)), ("gla_a_bias", (DEPTH, 128)), ("gla_norm_w", (DEPTH, 64)),
          ("post_norm_w", (DEPTH, 1024)), ("loss", ()))
_SMALL_ROWS = -(-sum(int(np.prod(s)) for _, s in _SMALL) // 1024) * 8


def _big_weights(w_in_all, w_out_all, l):
    return dict(w_g=_regroup_in(w_in_all, f"regroup_in{l}"),
                wout_g=w_out_all.reshape(4 * NH, DH, D_MODEL).astype(BF16))


def _layer_params(wts, tiny, l):
    conv = _gather_last(tiny["gdn_conv_w"][:, l])
    q = {}
    q["gdn_conv"] = jnp.transpose(conv.reshape(GDN_TAPS, 12, DH), (1, 0, 2))
    q["gdn_prm"] = [_rep(wts["gdn_a_log"][l]), _rep(wts["gdn_dt_bias"][l]),
                    jnp.broadcast_to(wts["gdn_norm_w"][l].reshape(1, 1, DH), (NH, 1, DH))]
    q["gdn_cst"] = [_onehot_lane(0), _onehot_lane(NH)]
    q["rwkv_mu"] = wts["rwkv_mu"][l].reshape(18, 1, DH)
    w_up = jnp.transpose(_gather_last(tiny["rwkv_w_up"][:, l]).reshape(64, NH, DH), (1, 0, 2))
    a_up = jnp.transpose(_gather_last(tiny["rwkv_a_up"][:, l]).reshape(64, NH, DH), (1, 0, 2))
    q["rwkv_prm"] = [_heads(wts["rwkv_w0"][l]), w_up, _heads(wts["rwkv_a0"][l]), a_up,
                     _heads(wts["rwkv_k_k"][l]), _heads(wts["rwkv_k_a"][l]), _heads(wts["rwkv_r_k"][l]),
                     _heads(wts["rwkv_ln_w"][l]), _heads(wts["rwkv_ln_b"][l])]
    sc = _gather_last(tiny["sc_conv_w"][:, l])
    q["sc_conv"] = jnp.transpose(sc.reshape(SC_TAPS, NH, DH), (1, 0, 2))
    gla_up = jnp.transpose(wts["gla_a_up"][l].reshape(16, NH, GLA_HEAD_K), (1, 0, 2))
    gla_up = jnp.pad(gla_up, ((0, 0), (0, DH - 16), (0, DH - GLA_HEAD_K)))
    gla_b = jnp.pad(wts["gla_a_bias"][l].reshape(NH, 1, GLA_HEAD_K), ((0, 0), (0, 0), (0, DH - GLA_HEAD_K)))
    q["gla_prm"] = [gla_up, gla_b, jnp.broadcast_to(wts["gla_norm_w"][l].reshape(1, 1, DH), (NH, 1, DH))]
    q["pre_w"] = wts["pre_norm_w"][l].reshape(1, D_MODEL)
    q["post_w"] = wts["post_norm_w"][l].reshape(1, D_MODEL)
    return q


def _mixer_inputs(p, cq, pm):
    gdn = [(cq, 4, 0), (cq, 4, 1), (cq, 4, 2), (p, 4, G_GDN // 4 + 3), (p, 1, G_GDN_AB)]
    rwkv = [(pm, 4, 0), (pm, 4, 1), (pm, 4, 2), (pm, 4, 3), (pm, 1, 16), (pm, 1, 17)]
    gla = [(p, 4, G_GLA // 4 + k) for k in range(4)] + [(p, 1, G_GLA_AD)]
    return gdn, rwkv, gla


def _layer_fwd(x, q, nb, t, l, side=None):
    hb, p = _norm_proj(x, q["pre_w"], q["w_g"], f"norm_proj{l}")
    cq = _conv_fwd(p, G_GDN, 12, q["gdn_conv"], nb, t, f"gdn_conv{l}")
    pm = _mix_fwd(p, q["rwkv_mu"], nb, t, f"rwkv_mix{l}")
    gdn_in, rwkv_in, gla_in = _mixer_inputs(p, cq, pm)
    y_gdn, ck_gdn, _ = _mixer_fwd(_gdn_chunk, f"gdn_fwd{l}", gdn_in, q["gdn_prm"], q["gdn_cst"], nb, t, n_kept=1)
    y_rwkv, ck_rwkv, side_res = _mixer_fwd(_rwkv_chunk, f"rwkv_fwd{l}", rwkv_in, q["rwkv_prm"], [], nb, t,
                                           first_fn=_rwkv_first_chunk, side=side, n_kept=1)
    y_sc = _sc_fwd(p, q["sc_conv"], nb, t, f"sc_fwd{l}")
    y_gla, ck_gla, _ = _mixer_fwd(_gla_chunk, f"gla_fwd{l}", gla_in, q["gla_prm"], [], nb, t)
    ys = (y_gdn, y_rwkv, y_sc, y_gla)
    out, xn = _out_proj_norm(ys, q["wout_g"], x, q["post_w"], f"out_proj{l}")
    saved = dict(x=x, hb=hb, p=p, cq=cq, pm=pm, ys=ys, out=out, ck=(ck_gdn, ck_rwkv, ck_gla))
    return xn, saved, side_res


def _layer_bwd(dxn, q, sv, nb, t, l, side=None, exchange_own=False):
    p, cq, pm, ys = sv["p"], sv["cq"], sv["pm"], sv["ys"]
    dout, dy, d_post = _post_bwd(dxn, sv["out"], q["post_w"], q["wout_g"], f"post_bwd{l}")
    d_wout = _dwout(ys, dout, f"dwout{l}").reshape(N_DEV, 128, D_MODEL).astype(BF16)
    gdn_in, rwkv_in, gla_in = _mixer_inputs(p, cq, pm)
    ck_gdn, ck_rwkv, ck_gla = sv["ck"]
    g = {}

    (d_conv, dz, dab), (da_log, ddt, dnw), _ = _mixer_bwd(
        _gdn_chunk, f"gdn_bwd{l}", gdn_in, q["gdn_prm"], q["gdn_cst"], ck_gdn, dy, 0,
        [(12, F32), (4, BF16), (1, BF16)], [(0, 0), (0, 4), (0, 8), (1, 0), (2, 0)], nb, t)
    dconv_in, d_gconv = _conv_bwd(p, G_GDN, 12, q["gdn_conv"], d_conv, nb, t, f"gdn_conv_bwd{l}")
    g["gdn_conv_w"] = jnp.transpose(d_gconv.sum(1), (1, 0, 2)).reshape(GDN_TAPS, 768)
    g["gdn_a_log"] = da_log.sum((0, 2, 3))
    g["gdn_dt_bias"] = ddt.sum((0, 2, 3))
    g["gdn_norm_w"] = dnw.sum((0, 1, 2))

    (d_pm,), d_rprm, side_res = _mixer_bwd(
        _rwkv_chunk, f"rwkv_bwd{l}", rwkv_in, q["rwkv_prm"], [], ck_rwkv, dy, 1,
        [(18, F32)], [(0, 0), (0, 4), (0, 8), (0, 12), (0, 16), (0, 17)], nb, t, first_fn=_rwkv_first_chunk,
        side=side)
    dp_rwkv, d_mu = _mix_bwd(p, q["rwkv_mu"], d_pm, nb, t, f"rwkv_mix_bwd{l}")
    g["rwkv_mu"] = d_mu.sum(1).reshape(1152)
    rp = [a.sum(0) for a in d_rprm]
    g["rwkv_w0"] = rp[0].reshape(256)
    g["rwkv_w_up"] = jnp.transpose(rp[1], (1, 0, 2)).reshape(64, 256)
    g["rwkv_a0"] = rp[2].reshape(256)
    g["rwkv_a_up"] = jnp.transpose(rp[3], (1, 0, 2)).reshape(64, 256)
    for i, nme in enumerate(("rwkv_k_k", "rwkv_k_a", "rwkv_r_k", "rwkv_ln_w", "rwkv_ln_b")):
        g[nme] = rp[4 + i].reshape(256)

    dp_sc, d_scw = _sc_bwd(p, q["sc_conv"], dy, nb, t, f"sc_bwd{l}")
    g["sc_conv_w"] = jnp.transpose(d_scw.sum(1), (1, 0, 2)).reshape(SC_TAPS, 256)

    (dp_gla, dad), (d_aup, d_ab, d_gnw), _ = _mixer_bwd(
        _gla_chunk, f"gla_bwd{l}", gla_in, q["gla_prm"], [], ck_gla, dy, 3,
        [(16, BF16), (1, BF16)], [(0, 0), (0, 4), (0, 8), (0, 12), (1, 0)], nb, t)
    g["gla_a_up"] = jnp.transpose(d_aup.sum(0)[:, :16, :GLA_HEAD_K], (1, 0, 2)).reshape(16, 128)
    g["gla_a_bias"] = d_ab.sum(0)[:, 0, :GLA_HEAD_K].reshape(128)
    g["gla_norm_w"] = d_gnw.sum((0, 1, 2))

    singles = jnp.concatenate([dab, dp_rwkv[16:18], dad], axis=0)
    sources = [dconv_in, dz, dp_rwkv, dp_sc, dp_gla, singles]
    d_win = _regroup_out(_dwin(sv["hb"], sources, f"dwin{l}"), f"regroup_out{l}")
    own = _exchange_plan([d_win, d_wout]) if exchange_own else None
    dx, d_pre, got = _dh_prenorm_bwd(sources, q["w_g"], sv["x"], q["pre_w"], dxn, f"dh_bwd{l}", own)
    if exchange_own:
        d_win, d_wout = got
    g["pre_norm_w"] = d_pre.reshape(D_MODEL)
    g["post_norm_w"] = d_post.reshape(D_MODEL)
    return dx, g, d_win, d_wout, side_res


def _local_step(x, tgt, wts, tiny, w_in_all, w_out_all, later_shards=None):
    nb, t, d = x.shape
    xf = x.reshape(nb * t, d)
    overlap = later_shards is not None
    qs, saved = [], []
    big = _big_weights(w_in_all[0], w_out_all[0], 0)
    for l in range(DEPTH):
        q = dict(_layer_params(wts, tiny, l), **big)
        nxt = l + 1 < DEPTH
        side = _gather_plan(later_shards[l]) if overlap and nxt else None
        xf, sv, got = _layer_fwd(xf, q, nb, t, l, side)
        if nxt:
            big = _big_weights(*(got if overlap else (w_in_all[l + 1], w_out_all[l + 1])), l + 1)
        qs.append(q)
        saved.append(sv)
    dxf, lpart = _loss_grad(xf, tgt.reshape(nb * t, d), "loss")
    grads, d_win, d_wout = [None] * DEPTH, [None] * DEPTH, [None] * DEPTH
    for l in reversed(range(DEPTH)):
        side = _exchange_plan([d_win[l + 1], d_wout[l + 1]]) if overlap and l + 1 < DEPTH else None
        dxf, grads[l], d_win[l], d_wout[l], got = _layer_bwd(dxf, qs[l], saved[l], nb, t, l, side,
                                                             exchange_own=overlap and l == 0)
        if side is not None:
            d_win[l + 1], d_wout[l + 1] = got
    small = {k: jnp.stack([grads[l][k] for l in range(DEPTH)]) for k in grads[0]}
    return lpart[0, 0], dxf.reshape(nb, t, d), small, d_win, d_wout


_WEIGHTS = ("pre_norm_w", "w_in", "gdn_conv_w", "gdn_a_log", "gdn_dt_bias", "gdn_norm_w", "rwkv_mu", "rwkv_w0",
            "rwkv_w_up", "rwkv_a0", "rwkv_a_up", "rwkv_k_k", "rwkv_k_a", "rwkv_r_k", "rwkv_ln_w", "rwkv_ln_b",
            "sc_conv_w", "gla_a_up", "gla_a_bias", "gla_norm_w", "w_out", "post_norm_w")


def kernel(x, pre_norm_w, w_in, gdn_conv_w, gdn_a_log, gdn_dt_bias, gdn_norm_w, rwkv_mu, rwkv_w0, rwkv_w_up, rwkv_a0, rwkv_a_up, rwkv_k_k, rwkv_k_a, rwkv_r_k, rwkv_ln_w, rwkv_ln_b, sc_conv_w, gla_a_up, gla_a_bias, gla_norm_w, w_out, post_norm_w, loss_target, m_pre_norm_w, m_w_in, m_gdn_conv_w, m_gdn_a_log, m_gdn_dt_bias, m_gdn_norm_w, m_rwkv_mu, m_rwkv_w0, m_rwkv_w_up, m_rwkv_a0, m_rwkv_a_up, m_rwkv_k_k, m_rwkv_k_a, m_rwkv_r_k, m_rwkv_ln_w, m_rwkv_ln_b, m_sc_conv_w, m_gla_a_up, m_gla_a_bias, m_gla_norm_w, m_w_out, m_post_norm_w, v_pre_norm_w, v_w_in, v_gdn_conv_w, v_gdn_a_log, v_gdn_dt_bias, v_gdn_norm_w, v_rwkv_mu, v_rwkv_w0, v_rwkv_w_up, v_rwkv_a0, v_rwkv_a_up, v_rwkv_k_k, v_rwkv_k_a, v_rwkv_r_k, v_rwkv_ln_w, v_rwkv_ln_b, v_sc_conv_w, v_gla_a_up, v_gla_a_bias, v_gla_norm_w, v_w_out, v_post_norm_w):
    env = dict(locals())
    w = {n: env[n] for n in _WEIGHTS}
    m = {n: env["m_" + n] for n in _WEIGHTS}
    v = {n: env["v_" + n] for n in _WEIGHTS}
    tiny_names = [n for n, _ in _TINY]

    w_in_b, w_out_b = w_in.astype(BF16), w_out.astype(BF16)
    w_in_0, w_out_0, tiny_all = _all_gather_two_level(
        [w_in_b[0], w_out_b[0], _pack_rows([w[n] for n in tiny_names], _TINY_ROWS)], "gather_weights")
    tiny = _unpack_rows(tiny_all, _TINY)

    lpart, grad_x, small, r_win, r_wout = _local_step(
        x, loss_target, w, tiny, [w_in_0], [w_out_0], later_shards=[(w_in_b[l], w_out_b[l]) for l in range(1, DEPTH)])

    tiny_send = jnp.stack([_pack_rows([_split_last(small[n][l])[d] for n in tiny_names for l in range(DEPTH)],
                                      _TINY_ROWS) for d in range(N_DEV)])
    (r_tiny,) = _run_comm(_exchange_plan([tiny_send]), "scatter_grads")
    grads, delta, new_m, new_v = {}, {}, {}, {}
    for n, parts in (("w_in", r_win), ("w_out", r_wout)):
        res = [_sum_adamw(parts[l], w[n][l], m[n][l], v[n][l], f"adamw_{n}{l}") for l in range(DEPTH)]
        grads[n], delta[n], new_m[n], new_v[n] = [jnp.stack(o) for o in zip(*res)]
    tiny_sum = _sum_slots(r_tiny, "sum_tiny").reshape(-1)
    o = 0
    for n, s in _TINY:
        size = int(np.prod(s))
        grads[n] = tiny_sum[o:o + size].reshape(s)
        o += size

    small = dict(small)
    small["loss"] = lpart
    red = _unpack_rows(_all_reduce_small(_pack_rows([small[n] for n, _ in _SMALL], _SMALL_ROWS), "reduce_small"),
                       _SMALL)
    loss = red.pop("loss")
    grads.update(red)

    rest = [n for n in _WEIGHTS if n not in ("w_in", "w_out")]
    rest_shapes = [(n, w[n].shape) for n in rest]
    rows = -(-sum(int(np.prod(s)) for _, s in rest_shapes) // 1024) * 8
    outs = _adamw(_pack_rows([w[n] for n in rest], rows), _pack_rows([grads[n] for n in rest], rows),
                  _pack_rows([m[n] for n in rest], rows), _pack_rows([v[n] for n in rest], rows, 1.0), "adamw_rest")
    for dst, packed in zip((delta, new_m, new_v), outs):
        dst.update(_unpack_rows(packed, rest_shapes))

    return (loss, grad_x, *[grads[n] for n in _WEIGHTS], *[delta[n] for n in _WEIGHTS],
            *[new_m[n] for n in _WEIGHTS], *[new_v[n] for n in _WEIGHTS])
```

```python
import collections
import functools
import math

import numpy as np
import jax
import jax.numpy as jnp
from jax import lax
from jax.experimental import pallas as pl
from jax.experimental.pallas import tpu as pltpu

F32 = jnp.float32
BF16 = jnp.bfloat16

D_MODEL = 1024
DEPTH = 2
NH = 4
DH = 64
CH = 64
EPS = 1e-6
RWKV_GN_EPS = 64e-5
GLA_HEAD_K = 32
GLA_TAU = 16.0
GDN_TAPS = 4
SC_TAPS = 3
D_IN = 3992
N_DEV = 8
SHARD_COLS = D_IN // N_DEV

G_GDN = 0
G_RWKV = 16
G_SC = 32
G_GLA = 48
G_GDN_AB, G_RWKV_WD, G_RWKV_AD, G_GLA_AD = 64, 65, 66, 67
N_GROUPS = 68
GROUPS_PER_STEP = 4
TIME_BLOCK = 256
RWKV_EXACT_STEPS = 16

C_GDN, C_RWKV, C_SC, C_GLA = 0, 1032, 2184, 3208

ADAM_LR, ADAM_B1, ADAM_B2, ADAM_EPS, ADAM_WD, ADAM_STEP = 0.001, 0.9, 0.999, 1e-08, 0.01, 10

VMEM_LIMIT = 56 * 1024 * 1024
MESH = pl.DeviceIdType.MESH

_pcall = pl.pallas_call

_Comm = collections.namedtuple("_Comm", "operands out_shapes copies")


def _cparams(sem=None):
    if sem is None:
        return pltpu.CompilerParams(vmem_limit_bytes=VMEM_LIMIT)
    return pltpu.CompilerParams(dimension_semantics=sem, vmem_limit_bytes=VMEM_LIMIT)


def _group_segments():
    table = [(G_GDN + i, C_GDN + DH * i, DH) for i in range(16)]
    table.append((G_GDN_AB, C_GDN + 1024, 8))
    table += [(G_RWKV + i, C_RWKV + DH * i, DH) for i in range(16)]
    table += [(G_RWKV_WD, C_RWKV + 1024, DH), (G_RWKV_AD, C_RWKV + 1088, DH)]
    table += [(G_SC + 4 * j + k, C_SC + 256 * k + DH * j, DH) for j in range(NH) for k in range(4)]
    for h in range(NH):
        table += [(G_GLA + h, C_GLA + GLA_HEAD_K * h, GLA_HEAD_K),
                  (G_GLA + 4 + h, C_GLA + 128 + GLA_HEAD_K * h, GLA_HEAD_K),
                  (G_GLA + 8 + h, C_GLA + 256 + DH * h, DH),
                  (G_GLA + 12 + h, C_GLA + 512 + DH * h, DH)]
    table.append((G_GLA_AD, C_GLA + 768, 16))
    segs, padded = [], []
    for g, c, n in table:
        if n < DH:
            padded.append(g)
        a = 0
        while n > 0:
            d, off = divmod(c, SHARD_COLS)
            ln = min(n, SHARD_COLS - off)
            segs.append((g, a, d, off, ln))
            c, a, n = c + ln, a + ln, n - ln
    return segs, padded


_SEGMENTS, _PADDED_GROUPS = _group_segments()


def _dn(ta, tb):
    return (((1 if ta else 2,), (2 if tb else 1,)), ((0,), (0,)))


def _hdot(a, b, ta=False, tb=False):
    return lax.dot_general(a, b, _dn(ta, tb), precision=lax.Precision.HIGH, preferred_element_type=F32)


def _r(x):
    return x.astype(BF16)


def _rdot(a, b, ta=False, tb=False):
    return lax.dot_general(_r(a), _r(b), _dn(ta, tb), preferred_element_type=F32)


@jax.custom_vjp
def _bmm(a, b):
    return _rdot(a, b)


def _bmm_fwd(a, b):
    return _rdot(a, b), (a, b)


def _bmm_bwd(res, g):
    a, b = res
    return _rdot(g, b, tb=True), _rdot(a, g, ta=True)


_bmm.defvjp(_bmm_fwd, _bmm_bwd)


@jax.custom_vjp
def _bmm_nt(a, b):
    return _rdot(a, b, tb=True)


def _bmm_nt_fwd(a, b):
    return _rdot(a, b, tb=True), (a, b)


def _bmm_nt_bwd(res, g):
    a, b = res
    return _rdot(g, b), _rdot(g, a, ta=True)


_bmm_nt.defvjp(_bmm_nt_fwd, _bmm_nt_bwd)


@jax.custom_vjp
def _bmm_tn(a, b):
    return _rdot(a, b, ta=True)


def _bmm_tn_fwd(a, b):
    return _rdot(a, b, ta=True), (a, b)


def _bmm_tn_bwd(res, g):
    a, b = res
    return _rdot(b, g, tb=True), _rdot(a, g)


_bmm_tn.defvjp(_bmm_tn_fwd, _bmm_tn_bwd)


def _tri(n):
    i = lax.broadcasted_iota(jnp.int32, (n, n), 0)
    j = lax.broadcasted_iota(jnp.int32, (n, n), 1)
    return i >= j, i > j, i == j


def _heads_of(x, like):
    n = like.shape[0]
    if x.ndim == 2:
        return jnp.broadcast_to(x[None], (n,) + x.shape)
    seqs = x.shape[0]
    return jnp.broadcast_to(x[:, None], (seqs, n // seqs) + x.shape[1:]).reshape((n,) + x.shape[1:])


def _cumsum_rows(x):
    incl, _, _ = _tri(x.shape[-2])
    return _hdot(_heads_of(incl.astype(F32), x), x)


@jax.custom_vjp
def _inv_unit_lower(a):
    n = a.shape[-1]
    _, _, eye = _tri(n)
    pw = -a
    inv = eye.astype(F32) + pw
    for _ in range(math.ceil(math.log2(n)) - 1):
        pw = _hdot(pw, pw)
        inv = inv + _hdot(inv, pw)
    return inv


def _inv_unit_lower_fwd(a):
    inv = _inv_unit_lower(a)
    return inv, inv


def _inv_unit_lower_bwd(inv, g):
    return (-_hdot(_hdot(inv, g, ta=True), inv, tb=True),)


_inv_unit_lower.defvjp(_inv_unit_lower_fwd, _inv_unit_lower_bwd)


@jax.custom_vjp
def _inv_reuse(a, inv):
    return inv


def _inv_reuse_fwd(a, inv):
    return inv, inv


def _inv_reuse_bwd(inv, g):
    return _inv_unit_lower_bwd(inv, g)[0], jnp.zeros_like(inv)


_inv_reuse.defvjp(_inv_reuse_fwd, _inv_reuse_bwd)


def _silu(x):
    return x * jax.nn.sigmoid(x)


def _t(x):
    return jnp.swapaxes(x, -1, -2)


def _gdn_chunk(prm, cst, ins, s, tinv=None):
    a_log, dt_b, nw = prm
    m_a, m_b = cst
    cq, ck, cv, z, ab = ins
    ab = _heads_of(ab, m_a)
    incl, strict, _ = _tri(CH)
    q = _silu(cq)
    k = _silu(ck)
    v = _silu(cv)
    q = q * lax.rsqrt(jnp.sum(q * q, -1, keepdims=True) + EPS) * (DH ** -0.5)
    k = k * lax.rsqrt(jnp.sum(k * k, -1, keepdims=True) + EPS)
    a_raw = jnp.sum(ab * m_a, -1, keepdims=True)
    b_raw = jnp.sum(ab * m_b, -1, keepdims=True)
    gstep = -jnp.exp(a_log) * jax.nn.softplus(a_raw + dt_b)
    beta = jax.nn.sigmoid(b_raw)
    gc = _cumsum_rows(gstep)
    gl = jnp.sum(gstep, -2, keepdims=True)
    dec = jnp.where(incl, jnp.exp(jnp.where(incl, gc - _t(gc), 0.0)), 0.0)
    kb = k * beta
    a_mat = jnp.where(strict, _bmm_nt(kb, k) * dec, 0.0)
    tinv = _inv_unit_lower(a_mat) if tinv is None else _inv_reuse(a_mat, tinv)
    eg = jnp.exp(gc)
    u = _hdot(tinv, v * beta)
    w = _hdot(tinv, kb * eg)
    attn = _bmm_nt(q, k) * dec
    v_new = u - _bmm(w, s)
    o = _bmm(q * eg, s) + _bmm(attn, v_new)
    s_next = s * jnp.exp(gl) + _bmm_tn(k * jnp.exp(gl - gc), v_new)
    on = o * lax.rsqrt(jnp.mean(o * o, -1, keepdims=True) + EPS) * nw
    return on * _silu(z), s_next, tinv


def _gla_chunk(prm, cst, ins, st):
    a_up, a_bias, nw = prm
    q, k, v, z, ad = ins
    incl, _, _ = _tri(CH)
    la = jax.nn.log_sigmoid(_bmm(_heads_of(ad, a_up), a_up) + a_bias) * (1.0 / GLA_TAU)
    bc = _cumsum_rows(la)
    bl = jnp.sum(la, -2, keepdims=True)
    qe = q * (GLA_HEAD_K ** -0.5) * jnp.exp(bc)
    ke = k * jnp.exp(-bc)
    attn = jnp.where(incl, _bmm_nt(qe, ke), 0.0)
    o = _bmm_nt(qe, st) + _bmm(attn, v)
    st_next = st * jnp.exp(bl) + _bmm_tn(v, k * jnp.exp(bl - bc))
    on = o * lax.rsqrt(jnp.mean(o * o, -1, keepdims=True) + EPS) * nw
    return on * _silu(z), st_next


def _rwkv_chunk(prm, cst, ins, s, inv=None):
    r, v = ins[0], ins[2]
    incl, strict, _ = _tri(r.shape[-2])
    lw, kk, k2, m = _rwkv_pre(prm, ins)
    cum = _cumsum_rows(lw)
    ltot = jnp.sum(lw, -2, keepdims=True)
    n_t = -kk * jnp.exp(cum - lw)
    einv = jnp.exp(-cum)
    m_t = m * einv
    k_t = k2 * einv
    r_t = r * jnp.exp(cum)
    a_nm = jnp.where(strict, _hdot(n_t, m_t, tb=True), 0.0)
    a_nk = jnp.where(strict, _hdot(n_t, k_t, tb=True), 0.0)
    inv = _inv_unit_lower(-a_nm) if inv is None else _inv_reuse(-a_nm, inv)
    cm = _hdot(inv, _hdot(n_t, s, tb=True) + _bmm(a_nk, v))
    y = (_bmm_nt(r_t, s) + _bmm(jnp.where(incl, _hdot(r_t, m_t, tb=True), 0.0), cm)
         + _bmm(jnp.where(incl, _hdot(r_t, k_t, tb=True), 0.0), v))
    eend = jnp.exp(ltot - cum)
    s_next = s * jnp.exp(ltot) + _bmm_tn(cm, m * eend) + _bmm_tn(v, k2 * eend)
    return _rwkv_post(prm, ins, y, k2), s_next, inv


def _rwkv_pre(prm, ins):
    w0, w_up, a0, a_up, k_k, k_a = prm[:6]
    k, wd, ad = ins[1], ins[4], ins[5]
    lw = -math.exp(-0.5) * jax.nn.sigmoid(w0 + _bmm(_heads_of(jnp.tanh(wd), w_up), w_up))
    a = jax.nn.sigmoid(a0 + _bmm(_heads_of(ad, a_up), a_up))
    kk = k * k_k
    kk = kk * lax.rsqrt(jnp.sum(kk * kk, -1, keepdims=True) + EPS)
    k2 = k * (1.0 + (a - 1.0) * k_a)
    return lw, kk, k2, kk * a


def _rwkv_post(prm, ins, y, k2):
    r_k, ln_w, ln_b = prm[6:]
    r, v, z = ins[0], ins[2], ins[3]
    mean = jnp.mean(y, -1, keepdims=True)
    yc = y - mean
    var = jnp.mean(yc * yc, -1, keepdims=True)
    yn = yc * lax.rsqrt(var + RWKV_GN_EPS) * ln_w + ln_b
    bonus = jnp.sum(r * k2 * r_k, -1, keepdims=True) * v
    return (yn + bonus) * _silu(z)


@jax.custom_vjp
def _bmv(s, x):
    return jnp.sum(_r(s).astype(F32) * _r(x).astype(F32), -1, keepdims=True)


def _bmv_fwd(s, x):
    return _bmv(s, x), (s, x)


def _bmv_bwd(res, g):
    s, x = res
    return g * x, jnp.sum(_r(s).astype(F32) * _r(g).astype(F32), -2, keepdims=True)


_bmv.defvjp(_bmv_fwd, _bmv_bwd)


def _rwkv_steps(prm, cst, ins, s, steps):
    r, v = ins[0], ins[2]
    lw, kk, k2, m = _rwkv_pre(prm, ins)
    w = jnp.exp(lw)
    v_t = _t(v)
    lane = lax.broadcasted_iota(jnp.int32, (1, 1, CH), 2)
    y_t = jnp.zeros((s.shape[0], DH, CH), F32)
    for t in range(steps):
        e_t = (lane == t).astype(F32)
        row = (slice(None), slice(t, t + 1))
        sa = _bmv(s, -kk[row])
        s = s * w[row] + sa * m[row] + jnp.sum(v_t * e_t, -1, keepdims=True) * k2[row]
        y_t = y_t + _bmv(s, r[row]) * e_t
    return _rwkv_post(prm, ins, _t(y_t), k2)[:, :steps], s


def _rwkv_first_chunk(prm, cst, ins, s):
    k = RWKV_EXACT_STEPS
    y_head, s = _rwkv_steps(prm, cst, ins, s, k)
    y_tail, s, _ = _rwkv_chunk(prm, cst, [x[..., k:, :] for x in ins], s)
    return jnp.concatenate([y_head, y_tail], axis=-2), s


def _time_block(t):
    return TIME_BLOCK if t % TIME_BLOCK == 0 else t


def _load_chunk(ref, i):
    nb = ref.shape[1]
    if ref.shape[0] == NH:
        return jnp.concatenate([ref[:, b, pl.ds(i, CH), :] for b in range(nb)], axis=0)
    return ref[0, :, pl.ds(i, CH), :]


def _mixer_fwd(chunk_fn, name, ins, prm, cst, nb, t, first_fn=None, side=None, n_kept=0):
    tb = _time_block(t)
    nt, ncb, nch = t // tb, tb // CH, nb * NH
    n_in, n_prm, n_cst = len(ins), len(prm), len(cst)
    n_main, n_side = n_in + n_prm + n_cst, len(side.operands) if side else 0

    def body(*refs):
        in_refs = refs[:n_in]
        prm_refs = refs[n_in:n_in + n_prm]
        cst_refs = refs[n_in + n_prm:n_main]
        side_in = refs[n_main:n_main + n_side]
        y_ref, ck_ref = refs[n_main + n_side:n_main + n_side + 2]
        side_out = refs[n_main + n_side + 2:n_main + 2 * n_side + 2]
        s_scr = refs[n_main + 2 * n_side + 2]
        sems = refs[n_main + 2 * n_side + 3:]
        step_t = pl.program_id(0)

        if side is not None:
            @pl.when(step_t == 0)
            def _():
                _comm_start(side.copies(side_in, side_out, sems))

        @pl.when(step_t == 0)
        def _():
            s_scr[...] = jnp.zeros_like(s_scr)

        def chunk(c, i, fn=chunk_fn):
            s = s_scr[...]
            y, s_next, *kept = fn([jnp.tile(r[...], (nb, 1, 1)) for r in prm_refs],
                                  [jnp.tile(r[...], (nb, 1, 1)) for r in cst_refs],
                                  [_load_chunk(r, i) for r in in_refs], s)
            kept += [jnp.zeros_like(s)] * (n_kept - len(kept))
            for e, a in enumerate([s] + kept):
                ck_ref[c, e] = a
            for b in range(nb):
                y_ref[:, b, pl.ds(i, CH), :] = y[b * NH:(b + 1) * NH].astype(BF16)
            s_scr[...] = s_next

        def step(c, carry):
            chunk(c, pl.multiple_of(c * CH, CH))
            return carry

        if first_fn is None:
            lax.fori_loop(0, ncb, step, 0)
        else:
            @pl.when(step_t == 0)
            def _():
                chunk(0, 0, first_fn)

            @pl.when(step_t != 0)
            def _():
                chunk(0, 0)

            lax.fori_loop(1, ncb, step, 0)

        if side is not None:
            @pl.when(step_t == nt - 1)
            def _():
                _comm_wait(side.copies(side_in, side_out, sems))

    hbm = pl.BlockSpec(memory_space=pl.ANY)
    in_specs = [pl.BlockSpec((ng, nb, tb, DH), (lambda j, bi=bi: (bi, 0, j, 0))) for _, ng, bi in ins]
    in_specs += [pl.BlockSpec(p.shape, lambda j: (0, 0, 0)) for p in list(prm) + list(cst)]
    y, ck, *side_res = _pcall(
        body, name=name, grid=(nt,),
        in_specs=in_specs + [hbm] * n_side,
        out_specs=[pl.BlockSpec((NH, nb, tb, DH), lambda j: (0, 0, j, 0)),
                   pl.BlockSpec((ncb, 1 + n_kept, nch, DH, DH), lambda j: (j, 0, 0, 0, 0))] + [hbm] * n_side,
        out_shape=[jax.ShapeDtypeStruct((NH, nb, t, DH), BF16),
                   jax.ShapeDtypeStruct((t // CH, 1 + n_kept, nch, DH, DH), F32)]
        + (side.out_shapes if side else []),
        scratch_shapes=[pltpu.VMEM((nch, DH, DH), F32)] + (_comm_scratch(side) if side else []),
        compiler_params=_cparams(("arbitrary",)),
    )(*[a.reshape(a.shape[0], nb, t, DH) for a, _, _ in ins], *prm, *cst, *(side.operands if side else []))
    return y.reshape(NH, nb * t, DH), ck, side_res


def _mixer_bwd(chunk_fn, name, ins, prm, cst, ck, dy, dy_block, outs, routes, nb, t, first_fn=None, side=None):
    tb = _time_block(t)
    nt, ncb, nch = t // tb, tb // CH, nb * NH
    n_in, n_prm, n_cst, n_out = len(ins), len(prm), len(cst), len(outs)
    n_main, n_side = n_in + n_prm + n_cst + 2, len(side.operands) if side else 0
    n_kept = ck.shape[1] - 1

    def body(*refs):
        in_refs = refs[:n_in]
        prm_refs = refs[n_in:n_in + n_prm]
        cst_refs = refs[n_in + n_prm:n_in + n_prm + n_cst]
        ck_ref, dy_ref = refs[n_main - 2:n_main]
        side_in = refs[n_main:n_main + n_side]
        rest = refs[n_main + n_side:]
        out_refs = rest[:n_out]
        dprm_refs = rest[n_out:n_out + n_prm]
        side_out = rest[n_out + n_prm:n_out + n_prm + n_side]
        ds_scr = rest[n_out + n_prm + n_side]
        sems = rest[n_out + n_prm + n_side + 1:]
        step_t = pl.program_id(0)

        if side is not None:
            @pl.when(step_t == 0)
            def _():
                _comm_start(side.copies(side_in, side_out, sems))

        @pl.when(step_t == 0)
        def _():
            ds_scr[...] = jnp.zeros_like(ds_scr)
            for r in dprm_refs:
                r[...] = jnp.zeros_like(r)

        def chunk(c, i, fn=chunk_fn):
            cst_v = [jnp.tile(r[...], (nb, 1, 1)) for r in cst_refs]
            kept = [ck_ref[c, 1 + e] for e in range(n_kept)] if fn is chunk_fn else []
            _, vjp = jax.vjp(lambda p, x, s: fn(p, cst_v, x, s, *kept)[:2],
                             [jnp.tile(r[...], (nb, 1, 1)) for r in prm_refs],
                             [_load_chunk(r, i) for r in in_refs], ck_ref[c, 0])
            dy_c = jnp.concatenate([dy_ref[:, b, pl.ds(i, CH), :] for b in range(nb)], axis=0)
            d_prm, d_ins, d_s = vjp((dy_c, ds_scr[...]))
            for (oi, g0), r, g in zip(routes, in_refs, d_ins):
                o_ref = out_refs[oi]
                if r.shape[0] == NH:
                    for b in range(nb):
                        o_ref[g0:g0 + NH, b, pl.ds(i, CH), :] = g[b * NH:(b + 1) * NH].astype(o_ref.dtype)
                else:
                    o_ref[g0, :, pl.ds(i, CH), :] = g.astype(o_ref.dtype)
            for r, g in zip(dprm_refs, d_prm):
                r[...] += g
            ds_scr[...] = d_s

        def step(j, carry):
            c = ncb - 1 - j
            chunk(c, pl.multiple_of(c * CH, CH))
            return carry

        lax.fori_loop(0, ncb - 1, step, 0)
        if first_fn is None:
            chunk(0, 0)
        else:
            @pl.when(step_t == nt - 1)
            def _():
                chunk(0, 0, first_fn)

            @pl.when(step_t != nt - 1)
            def _():
                chunk(0, 0)

        if side is not None:
            @pl.when(step_t == nt - 1)
            def _():
                _comm_wait(side.copies(side_in, side_out, sems))

    def back(j):
        return nt - 1 - j

    hbm = pl.BlockSpec(memory_space=pl.ANY)
    in_specs = [pl.BlockSpec((ng, nb, tb, DH), (lambda j, bi=bi: (bi, 0, back(j), 0))) for _, ng, bi in ins]
    in_specs += [pl.BlockSpec(p.shape, lambda j: (0, 0, 0)) for p in list(prm) + list(cst)]
    in_specs += [pl.BlockSpec((ncb, 1 + n_kept, nch, DH, DH), lambda j: (back(j), 0, 0, 0, 0)),
                 pl.BlockSpec((NH, nb, tb, DH), lambda j: (dy_block, 0, back(j), 0))]
    out_specs = [pl.BlockSpec((ng, nb, tb, DH), lambda j: (0, 0, back(j), 0)) for ng, _ in outs]
    out_specs += [pl.BlockSpec((nch,) + p.shape[1:], lambda j: (0, 0, 0)) for p in prm]
    out_shape = [jax.ShapeDtypeStruct((ng, nb, t, DH), dt) for ng, dt in outs]
    out_shape += [jax.ShapeDtypeStruct((nch,) + p.shape[1:], F32) for p in prm]
    res = _pcall(
        body, name=name, grid=(nt,),
        in_specs=in_specs + [hbm] * n_side, out_specs=out_specs + [hbm] * n_side,
        out_shape=out_shape + (side.out_shapes if side else []),
        scratch_shapes=[pltpu.VMEM((nch, DH, DH), F32)] + (_comm_scratch(side) if side else []),
        compiler_params=_cparams(("arbitrary",)),
    )(*[a.reshape(a.shape[0], nb, t, DH) for a, _, _ in ins], *prm, *cst, ck, dy.reshape(dy.shape[0], nb, t, DH),
      *(side.operands if side else []))
    d_outs = [o.reshape(o.shape[0], nb * t, DH) for o in res[:n_out]]
    d_prm = [g.reshape((nb,) + p.shape) for g, p in zip(res[n_out:n_out + n_prm], prm)]
    return d_outs, d_prm, res[n_out + n_prm:]


def _shift_down(x, s):
    if s == 0:
        return x
    row = lax.broadcasted_iota(jnp.int32, x.shape, 0)
    return jnp.where(row < s, 0.0, pltpu.roll(x, s, 0))


def _shift_up(x, s):
    if s == 0:
        return x
    t = x.shape[0]
    row = lax.broadcasted_iota(jnp.int32, x.shape, 0)
    return jnp.where(row >= t - s, 0.0, pltpu.roll(x, t - s, 0))


def _conv_bwd(p, g0, ng, w, dy, nb, t, name):
    taps = w.shape[1]

    def body(x_ref, w_ref, dy_ref, dx_ref, dw_ref):
        x = x_ref[...]
        d = dy_ref[...]
        acc = w_ref[taps - 1:taps, :] * d
        dw_ref[taps - 1:taps, :] = jnp.sum(d * x, 0, keepdims=True)
        for i in range(taps - 1):
            s = taps - 1 - i
            acc = acc + w_ref[i:i + 1, :] * _shift_up(d, s)
            dw_ref[i:i + 1, :] = jnp.sum(d * _shift_down(x, s), 0, keepdims=True)
        dx_ref[...] = acc.astype(BF16)

    return _pcall(
        body, name=name, grid=(ng, nb),
        in_specs=[pl.BlockSpec((None, t, DH), lambda g, b: (g0 + g, b, 0)),
                  pl.BlockSpec((None, taps, DH), lambda g, b: (g, 0, 0)),
                  pl.BlockSpec((None, t, DH), lambda g, b: (g, b, 0))],
        out_specs=[pl.BlockSpec((None, t, DH), lambda g, b: (g, b, 0)),
                   pl.BlockSpec((None, None, taps, DH), lambda g, b: (g, b, 0, 0))],
        out_shape=[jax.ShapeDtypeStruct((ng, nb * t, DH), BF16),
                   jax.ShapeDtypeStruct((ng, nb, taps, DH), F32)],
        compiler_params=_cparams(("parallel", "parallel")),
    )(p, w, dy)


def _mix_group(g):
    return jnp.where(g < 16, G_RWKV + g, G_RWKV_WD + g - 16)


def _mix_bwd(p, mu, dy, nb, t, name):
    def body(x_ref, mu_ref, dy_ref, dx_ref, dmu_ref):
        x = x_ref[...]
        muv = mu_ref[...]
        d = dy_ref[...]
        dx_ref[...] = (d * (1.0 - muv) + _shift_up(d * muv, 1)).astype(BF16)
        dmu_ref[...] = jnp.sum(d * (_shift_down(x, 1) - x), 0, keepdims=True)

    return _pcall(
        body, name=name, grid=(18, nb),
        in_specs=[pl.BlockSpec((None, t, DH), lambda g, b: (_mix_group(g), b, 0)),
                  pl.BlockSpec((None, 1, DH), lambda g, b: (g, 0, 0)),
                  pl.BlockSpec((None, t, DH), lambda g, b: (g, b, 0))],
        out_specs=[pl.BlockSpec((None, t, DH), lambda g, b: (g, b, 0)),
                   pl.BlockSpec((None, None, 1, DH), lambda g, b: (g, b, 0, 0))],
        out_shape=[jax.ShapeDtypeStruct((18, nb * t, DH), BF16),
                   jax.ShapeDtypeStruct((18, nb, 1, DH), F32)],
        compiler_params=_cparams(("parallel", "parallel")),
    )(p, mu, dy)


def _sc_fwd(p, w, nb, t, name):
    def body(p_ref, w_ref, y_ref):
        u = p_ref[1] * p_ref[2]
        conv = w_ref[2:3, :] * u + w_ref[1:2, :] * _shift_down(u, 1) + w_ref[0:1, :] * _shift_down(u, 2)
        y_ref[...] = (p_ref[0] * conv * _silu(p_ref[3])).astype(BF16)

    return _pcall(
        body, name=name, grid=(NH, nb),
        in_specs=[pl.BlockSpec((4, t, DH), lambda j, b: (G_SC // 4 + j, b, 0)),
                  pl.BlockSpec((None, SC_TAPS, DH), lambda j, b: (j, 0, 0))],
        out_specs=pl.BlockSpec((None, t, DH), lambda j, b: (j, b, 0)),
        out_shape=jax.ShapeDtypeStruct((NH, nb * t, DH), BF16),
        compiler_params=_cparams(("parallel", "parallel")),
    )(p, w)


def _sc_bwd(p, w, dy, nb, t, name):
    def body(p_ref, w_ref, dy_ref, dp_ref, dw_ref):
        bg, cg, xg, z = p_ref[0], p_ref[1], p_ref[2], p_ref[3]
        d = dy_ref[...]
        u = cg * xg
        u1 = _shift_down(u, 1)
        u2 = _shift_down(u, 2)
        conv = w_ref[2:3, :] * u + w_ref[1:2, :] * u1 + w_ref[0:1, :] * u2
        sg = jax.nn.sigmoid(z)
        sz = z * sg
        dp_ref[0] = (d * conv * sz).astype(BF16)
        dp_ref[3] = (d * bg * conv * (sg * (1.0 + z * (1.0 - sg)))).astype(BF16)
        dconv = d * bg * sz
        du = w_ref[2:3, :] * dconv + w_ref[1:2, :] * _shift_up(dconv, 1) + w_ref[0:1, :] * _shift_up(dconv, 2)
        dp_ref[1] = (du * xg).astype(BF16)
        dp_ref[2] = (du * cg).astype(BF16)
        dw_ref[2:3, :] = jnp.sum(dconv * u, 0, keepdims=True)
        dw_ref[1:2, :] = jnp.sum(dconv * u1, 0, keepdims=True)
        dw_ref[0:1, :] = jnp.sum(dconv * u2, 0, keepdims=True)

    return _pcall(
        body, name=name, grid=(NH, nb),
        in_specs=[pl.BlockSpec((4, t, DH), lambda j, b: (G_SC // 4 + j, b, 0)),
                  pl.BlockSpec((None, SC_TAPS, DH), lambda j, b: (j, 0, 0)),
                  pl.BlockSpec((None, t, DH), lambda j, b: (8 + j, b, 0))],
        out_specs=[pl.BlockSpec((4, t, DH), lambda j, b: (j, b, 0)),
                   pl.BlockSpec((None, None, SC_TAPS, DH), lambda j, b: (j, b, 0, 0))],
        out_shape=[jax.ShapeDtypeStruct((4 * NH, nb * t, DH), BF16),
                   jax.ShapeDtypeStruct((NH, nb, SC_TAPS, DH), F32)],
        compiler_params=_cparams(("parallel", "parallel")),
    )(p, w, dy)


def _row_tile(n):
    return 1024 if n % 1024 == 0 else n


def _regroup_in(w_all, name):
    tr = 256
    gs = GROUPS_PER_STEP

    def body(w_ref, o_ref):
        for g in _PADDED_GROUPS:
            o_ref[g // gs, :, DH * (g % gs):DH * (g % gs + 1)] = jnp.zeros((tr, DH), BF16)
        for g, a, d, off, ln in _SEGMENTS:
            lane = DH * (g % gs) + a
            o_ref[g // gs, :, lane:lane + ln] = w_ref[d, :, off:off + ln].astype(BF16)

    return _pcall(
        body, name=name, grid=(D_MODEL // tr,),
        in_specs=[pl.BlockSpec((N_DEV, tr, SHARD_COLS), lambda i: (0, i, 0))],
        out_specs=pl.BlockSpec((N_GROUPS // gs, tr, gs * DH), lambda i: (0, i, 0)),
        out_shape=jax.ShapeDtypeStruct((N_GROUPS // gs, D_MODEL, gs * DH), BF16),
        compiler_params=_cparams(("parallel",)),
    )(w_all)


def _regroup_out(dwg, name):
    tr = 256
    gs = GROUPS_PER_STEP

    def body(g_ref, o_ref):
        for g, a, d, off, ln in _SEGMENTS:
            lane = DH * (g % gs) + a
            o_ref[d, :, off:off + ln] = g_ref[g // gs, :, lane:lane + ln].astype(BF16)

    return _pcall(
        body, name=name, grid=(D_MODEL // tr,),
        in_specs=[pl.BlockSpec((N_GROUPS // gs, tr, gs * DH), lambda i: (0, i, 0))],
        out_specs=pl.BlockSpec((N_DEV, tr, SHARD_COLS), lambda i: (0, i, 0)),
        out_shape=jax.ShapeDtypeStruct((N_DEV, D_MODEL, SHARD_COLS), BF16),
        compiler_params=_cparams(("parallel",)),
    )(dwg)


CONV_BLOCKS = (0, 1, 2)
MIX_BLOCKS = (4, 5, 6, 7, 16)
SHIFTED_BLOCKS = {0: 0, 1: 1, 2: 2, 4: 3, 5: 4, 6: 5, 7: 6, 16: 7}
G_PM = 12
G_PM_WD, G_PM_AD = 29, 30


def _norm_proj(x, pre_w, w_g, conv_w, mix_mu, t, name):
    n = x.shape[0]
    tm = _row_tile(n)
    gs = GROUPS_PER_STEP
    nj = N_GROUPS // gs
    assert t % tm == 0, (t, tm)
    per_seq = t // tm

    def shifted_block(j):
        out = jnp.int32(len(SHIFTED_BLOCKS) - 1)
        for jj in sorted(SHIFTED_BLOCKS, reverse=True):
            out = jnp.where(j < jj, max(SHIFTED_BLOCKS[jj] - 1, 0), out)
        for jj, b in SHIFTED_BLOCKS.items():
            out = jnp.where(j == jj, b, out)
        return out

    def body(x_ref, pw_ref, w_ref, cw_ref, mu_ref, h_ref, p_ref, s_ref, carry):
        i, j = pl.program_id(0), pl.program_id(1)

        @pl.when(j == 0)
        def _():
            xv = x_ref[...]
            h = xv * lax.rsqrt(jnp.mean(xv * xv, -1, keepdims=True) + EPS) * pw_ref[...]
            h_ref[...] = h.astype(BF16)

        r = jnp.dot(h_ref[...], w_ref[...], preferred_element_type=F32)
        for k in range(gs):
            p_ref[k] = r[:, DH * k:DH * (k + 1)]

        def shifts():
            first = (i % per_seq) == 0
            tail = jnp.where(first, 0.0, carry[j])
            above = jnp.concatenate([tail, jnp.zeros((tm - 8, gs * DH), F32)], axis=0)
            row = lax.broadcasted_iota(jnp.int32, r.shape, 0)
            out = [jnp.where(row < s, pltpu.roll(above, tm - 8 + s, 0), pltpu.roll(r, s, 0)) for s in (1, 2, 3)]
            carry[j] = r[tm - 8:, :]
            return out

        def store(v):
            for k in range(gs):
                s_ref[k] = v[:, DH * k:DH * (k + 1)]

        @pl.when(functools.reduce(jnp.logical_or, [j == b for b in CONV_BLOCKS]))
        def _():
            s1, s2, s3 = shifts()
            store(cw_ref[3:4, :] * r + cw_ref[0:1, :] * s3 + cw_ref[1:2, :] * s2 + cw_ref[2:3, :] * s1)

        @pl.when(functools.reduce(jnp.logical_or, [j == b for b in MIX_BLOCKS]))
        def _():
            s1 = shifts()[0]
            store(r + mu_ref[...] * (s1 - r))

    n_shifted = len(SHIFTED_BLOCKS) * gs
    return _pcall(
        body, name=name, grid=(n // tm, nj),
        in_specs=[pl.BlockSpec((tm, D_MODEL), lambda i, j: (i, 0)),
                  pl.BlockSpec((1, D_MODEL), lambda i, j: (0, 0)),
                  pl.BlockSpec((None, D_MODEL, gs * DH), lambda i, j: (j, 0, 0)),
                  pl.BlockSpec((None, GDN_TAPS, gs * DH), lambda i, j: (j, 0, 0)),
                  pl.BlockSpec((None, 1, gs * DH), lambda i, j: (j, 0, 0))],
        out_specs=[pl.BlockSpec((tm, D_MODEL), lambda i, j: (i, 0)),
                   pl.BlockSpec((gs, tm, DH), lambda i, j: (j, i, 0)),
                   pl.BlockSpec((gs, tm, DH), lambda i, j: (shifted_block(j), i, 0))],
        out_shape=[jax.ShapeDtypeStruct((n, D_MODEL), BF16),
                   jax.ShapeDtypeStruct((N_GROUPS, n, DH), F32),
                   jax.ShapeDtypeStruct((n_shifted, n, DH), F32)],
        scratch_shapes=[pltpu.VMEM((nj, 8, gs * DH), F32)],
        compiler_params=_cparams(("arbitrary", "arbitrary")),
    )(x, pre_w, w_g, conv_w, mix_mu)


def _out_proj_norm(ys, wout_g, x, post_w, name):
    n = x.shape[0]
    tm = _row_tile(n)

    def body(y0, y1, y2, y3, w_ref, x_ref, pw_ref, out_ref, xn_ref):
        y = jnp.concatenate([yr[h] for yr in (y0, y1, y2, y3) for h in range(NH)], axis=-1)
        acc = jnp.dot(y, w_ref[...], preferred_element_type=F32)
        out_ref[...] = acc
        xn_ref[...] = x_ref[...] + acc * lax.rsqrt(jnp.mean(acc * acc, -1, keepdims=True) + EPS) * pw_ref[...]

    yspec = pl.BlockSpec((NH, tm, DH), lambda i: (0, i, 0))
    rows = pl.BlockSpec((tm, D_MODEL), lambda i: (i, 0))
    return _pcall(
        body, name=name, grid=(n // tm,),
        in_specs=[yspec] * 4 + [pl.BlockSpec((D_MODEL, D_MODEL), lambda i: (0, 0)), rows,
                                pl.BlockSpec((1, D_MODEL), lambda i: (0, 0))],
        out_specs=[rows, rows],
        out_shape=[jax.ShapeDtypeStruct((n, D_MODEL), F32)] * 2,
        compiler_params=_cparams(("parallel",)),
    )(*ys, wout_g.reshape(D_MODEL, D_MODEL), x, post_w)


def _loss_grad(x, tgt, name):
    n = x.shape[0]
    tm = _row_tile(n)

    def body(x_ref, t_ref, dx_ref, l_ref):
        @pl.when(pl.program_id(0) == 0)
        def _():
            l_ref[...] = jnp.zeros_like(l_ref)

        e = x_ref[...] - t_ref[...]
        dx_ref[...] = e * (1.0 / D_MODEL)
        l_ref[...] += jnp.sum(jnp.sum(e * e, -1, keepdims=True), 0, keepdims=True) * (0.5 / D_MODEL)

    rows = pl.BlockSpec((tm, D_MODEL), lambda i: (i, 0))
    return _pcall(
        body, name=name, grid=(n // tm,),
        in_specs=[rows, rows],
        out_specs=[rows, pl.BlockSpec((1, 128), lambda i: (0, 0))],
        out_shape=[jax.ShapeDtypeStruct((n, D_MODEL), F32), jax.ShapeDtypeStruct((1, 128), F32)],
        compiler_params=_cparams(("arbitrary",)),
    )(x, tgt)


def _rmsnorm_bwd(xv, w, d):
    r = lax.rsqrt(jnp.mean(xv * xv, -1, keepdims=True) + EPS)
    xh = xv * r
    dxh = d * w
    dx = r * (dxh - xh * jnp.mean(dxh * xh, -1, keepdims=True))
    return dx, d * xh


def _post_bwd(dxn, out, post_w, wout_g, name):
    n = dxn.shape[0]
    tm = _row_tile(n)

    def body(d_ref, o_ref, pw_ref, w_ref, do_ref, dy_ref, dpw_ref):
        @pl.when(pl.program_id(0) == 0)
        def _():
            dpw_ref[...] = jnp.zeros_like(dpw_ref)

        dout, dw_rows = _rmsnorm_bwd(o_ref[...], pw_ref[...], d_ref[...])
        dpw_ref[...] += jnp.sum(dw_rows, 0, keepdims=True)
        db = dout.astype(BF16)
        do_ref[...] = db
        dy = lax.dot_general(db, w_ref[...], (((1,), (1,)), ((), ())), preferred_element_type=F32)
        for g in range(4 * NH):
            dy_ref[g] = dy[:, DH * g:DH * (g + 1)]

    rows = pl.BlockSpec((tm, D_MODEL), lambda i: (i, 0))
    vec = pl.BlockSpec((1, D_MODEL), lambda i: (0, 0))
    return _pcall(
        body, name=name, grid=(n // tm,),
        in_specs=[rows, rows, vec, pl.BlockSpec((D_MODEL, D_MODEL), lambda i: (0, 0))],
        out_specs=[rows, pl.BlockSpec((4 * NH, tm, DH), lambda i: (0, i, 0)), vec],
        out_shape=[jax.ShapeDtypeStruct((n, D_MODEL), BF16),
                   jax.ShapeDtypeStruct((4 * NH, n, DH), F32),
                   jax.ShapeDtypeStruct((1, D_MODEL), F32)],
        compiler_params=_cparams(("arbitrary",)),
    )(dxn, out, post_w, wout_g.reshape(D_MODEL, D_MODEL))


def _dwout(ys, dout, name):
    n = dout.shape[0]
    tm = _row_tile(n)

    def body(y0, y1, y2, y3, d_ref, dw_ref):
        @pl.when(pl.program_id(0) == 0)
        def _():
            dw_ref[...] = jnp.zeros_like(dw_ref)

        y = jnp.concatenate([yr[h] for yr in (y0, y1, y2, y3) for h in range(NH)], axis=-1)
        dw_ref[...] += lax.dot_general(y, d_ref[...], (((0,), (0,)), ((), ())), preferred_element_type=F32)

    yspec = pl.BlockSpec((NH, tm, DH), lambda i: (0, i, 0))
    return _pcall(
        body, name=name, grid=(n // tm,),
        in_specs=[yspec] * 4 + [pl.BlockSpec((tm, D_MODEL), lambda i: (i, 0))],
        out_specs=pl.BlockSpec((D_MODEL, D_MODEL), lambda i: (0, 0)),
        out_shape=jax.ShapeDtypeStruct((D_MODEL, D_MODEL), F32),
        compiler_params=_cparams(("arbitrary",)),
    )(*ys, dout)


def _source_specs(sources, rows_first):
    gs = GROUPS_PER_STEP
    spans, specs, j0 = [], [], 0
    for a in sources:
        nblk = a.shape[0] // gs
        spans.append((j0, j0 + nblk))
        shape = (gs, _row_tile(a.shape[1]), DH)

        def blk(j, j0=j0, nblk=nblk):
            return jnp.clip(j - j0, 0, nblk - 1)

        if rows_first:
            specs.append(pl.BlockSpec(shape, (lambda i, j, blk=blk: (blk(j), i, 0))))
        else:
            specs.append(pl.BlockSpec(shape, (lambda j, i, blk=blk: (blk(j), i, 0))))
        j0 += nblk
    return spans, specs


def _dh_prenorm_bwd(sources, w_g, x, pre_w, dxn, name, side=None):
    n = x.shape[0]
    tm = _row_tile(n)
    gs = GROUPS_PER_STEP
    nj = N_GROUPS // gs
    ni = n // tm
    spans, src_specs = _source_specs(sources, True)
    ns = len(sources)
    n_side = len(side.operands) if side else 0

    def body(*refs):
        src = refs[:ns]
        w_ref, x_ref, pw_ref, d_ref = refs[ns:ns + 4]
        side_in = refs[ns + 4:ns + 4 + n_side]
        dx_ref, dpw_ref = refs[ns + 4 + n_side:ns + 6 + n_side]
        side_out = refs[ns + 6 + n_side:ns + 6 + 2 * n_side]
        acc = refs[ns + 6 + 2 * n_side]
        sems = refs[ns + 7 + 2 * n_side:]
        i, j = pl.program_id(0), pl.program_id(1)

        if side is not None:
            @pl.when((i == 0) & (j == 0))
            def _():
                _comm_start(side.copies(side_in, side_out, sems))

        @pl.when((i == 0) & (j == 0))
        def _():
            dpw_ref[...] = jnp.zeros_like(dpw_ref)

        @pl.when(j == 0)
        def _():
            acc[...] = jnp.zeros_like(acc)

        for s_ref, (lo, hi) in zip(src, spans):
            @pl.when((j >= lo) & (j < hi))
            def _(s_ref=s_ref):
                four = jnp.concatenate([s_ref[k] for k in range(gs)], axis=-1)
                acc[...] += lax.dot_general(four, w_ref[...], (((1,), (1,)), ((), ())), preferred_element_type=F32)

        @pl.when(j == nj - 1)
        def _():
            dx, dw_rows = _rmsnorm_bwd(x_ref[...], pw_ref[...], acc[...])
            dx_ref[...] = d_ref[...] + dx
            dpw_ref[...] += jnp.sum(dw_rows, 0, keepdims=True)

        if side is not None:
            @pl.when((i == ni - 1) & (j == nj - 1))
            def _():
                _comm_wait(side.copies(side_in, side_out, sems))

    hbm = pl.BlockSpec(memory_space=pl.ANY)
    rows = pl.BlockSpec((tm, D_MODEL), lambda i, j: (i, 0))
    vec = pl.BlockSpec((1, D_MODEL), lambda i, j: (0, 0))
    dx, dpw, *side_res = _pcall(
        body, name=name, grid=(ni, nj),
        in_specs=src_specs + [pl.BlockSpec((None, D_MODEL, gs * DH), lambda i, j: (j, 0, 0)), rows, vec, rows]
        + [hbm] * n_side,
        out_specs=[rows, vec] + [hbm] * n_side,
        out_shape=[jax.ShapeDtypeStruct((n, D_MODEL), F32), jax.ShapeDtypeStruct((1, D_MODEL), F32)]
        + (side.out_shapes if side else []),
        scratch_shapes=[pltpu.VMEM((tm, D_MODEL), F32)] + (_comm_scratch(side) if side else []),
        compiler_params=_cparams(("arbitrary", "arbitrary")),
    )(*sources, w_g, x, pre_w, dxn, *(side.operands if side else []))
    return dx, dpw, side_res


def _dwin(hb, sources, name):
    n = hb.shape[0]
    tm = _row_tile(n)
    gs = GROUPS_PER_STEP
    ni, nj = n // tm, N_GROUPS // gs
    spans, src_specs = _source_specs(sources, True)
    ns = len(sources)

    def body(*refs):
        h_ref = refs[0]
        src = refs[1:1 + ns]
        out_ref, acc, sem = refs[1 + ns:]
        i, j = pl.program_id(0), pl.program_id(1)

        @pl.when((i == 0) & (j == 0))
        def _():
            acc[...] = jnp.zeros_like(acc)

        h = h_ref[...]
        for s_ref, (lo, hi) in zip(src, spans):
            @pl.when((j >= lo) & (j < hi))
            def _(s_ref=s_ref):
                four = jnp.concatenate([s_ref[k] for k in range(gs)], axis=-1)
                acc[j] += jnp.dot(h, four, preferred_element_type=F32)

        @pl.when((i == ni - 1) & (j == nj - 1))
        def _():
            done = pltpu.make_async_copy(acc, out_ref, sem)
            done.start()
            done.wait()

    return _pcall(
        body, name=name, grid=(ni, nj),
        in_specs=[pl.BlockSpec((D_MODEL, tm), lambda i, j: (0, i))] + src_specs,
        out_specs=pl.BlockSpec(memory_space=pl.ANY),
        out_shape=jax.ShapeDtypeStruct((nj, D_MODEL, gs * DH), F32),
        scratch_shapes=[pltpu.VMEM((nj, D_MODEL, gs * DH), F32), pltpu.SemaphoreType.DMA],
        compiler_params=_cparams(("arbitrary", "arbitrary")),
    )(jnp.transpose(hb), *sources)


def _adamw_math(w, g, m, v):
    c1 = 1.0 - ADAM_B1 ** ADAM_STEP
    c2 = 1.0 - ADAM_B2 ** ADAM_STEP
    nm = ADAM_B1 * m + (1.0 - ADAM_B1) * g
    nv = ADAM_B2 * v + (1.0 - ADAM_B2) * (g * g)
    return -ADAM_LR * ((nm / c1) / (jnp.sqrt(nv / c2) + ADAM_EPS) + ADAM_WD * w), nm, nv


def _adamw(w, g, m, v, name):
    r, c = w.shape
    tr = 256 if r % 256 == 0 else r

    def body(w_ref, g_ref, m_ref, v_ref, d_ref, nm_ref, nv_ref):
        d_ref[...], nm_ref[...], nv_ref[...] = _adamw_math(w_ref[...], g_ref[...], m_ref[...], v_ref[...])

    spec = pl.BlockSpec((tr, c), lambda i: (i, 0))
    return _pcall(
        body, name=name, grid=(r // tr,),
        in_specs=[spec] * 4, out_specs=[spec] * 3,
        out_shape=[jax.ShapeDtypeStruct((r, c), F32)] * 3,
        compiler_params=_cparams(("parallel",)),
    )(w, g, m, v)


def _sum_adamw(parts, w, m, v, name):
    r, c = w.shape
    tr = 128 if r % 128 == 0 else r

    def body(p_ref, w_ref, m_ref, v_ref, g_ref, d_ref, nm_ref, nv_ref):
        g = p_ref[0].astype(F32)
        for k in range(1, N_DEV):
            g = g + p_ref[k].astype(F32)
        g_ref[...] = g
        d_ref[...], nm_ref[...], nv_ref[...] = _adamw_math(w_ref[...], g, m_ref[...], v_ref[...])

    spec = pl.BlockSpec((tr, c), lambda i: (i, 0))
    return _pcall(
        body, name=name, grid=(r // tr,),
        in_specs=[pl.BlockSpec((N_DEV, tr, c), lambda i: (0, i, 0))] + [spec] * 3, out_specs=[spec] * 4,
        out_shape=[jax.ShapeDtypeStruct((r, c), F32)] * 4,
        compiler_params=_cparams(("parallel",)),
    )(parts, w, m, v)


def _me():
    return lax.axis_index("x"), lax.axis_index("y"), lax.axis_index("c")


def _flat(x, y, c):
    return 4 * x + 2 * y + c


def _peer(k):
    x, y, c = _me()
    return (x ^ ((k >> 2) & 1), y ^ ((k >> 1) & 1), c ^ (k & 1))


def _gather_plan(blocks):
    def copies(x_refs, out_refs, sems):
        send_sems, recv_sems, local_sems = sems
        me = _flat(*_me())
        local = [pltpu.make_async_copy(x, o.at[me], local_sems.at[a]) for a, (x, o) in enumerate(zip(x_refs, out_refs))]
        outgoing, incoming = [], []
        for k in range(1, N_DEV):
            src = _flat(*_peer(k))
            for a, (x, o) in enumerate(zip(x_refs, out_refs)):
                for slot, group in ((me, outgoing), (src, incoming)):
                    group.append(pltpu.make_async_remote_copy(
                        src_ref=x, dst_ref=o.at[slot], send_sem=send_sems.at[a, k - 1], recv_sem=recv_sems.at[a, k - 1],
                        device_id=_peer(k), device_id_type=MESH))
        return local, outgoing, incoming

    return _Comm(list(blocks), [jax.ShapeDtypeStruct((N_DEV,) + b.shape, b.dtype) for b in blocks], copies)


def _exchange_plan(sends):
    def copies(s_refs, out_refs, sems):
        send_sems, recv_sems, local_sems = sems
        me = _flat(*_me())
        local = [pltpu.make_async_copy(s.at[me], o.at[0], local_sems.at[i]) for i, (s, o) in enumerate(zip(s_refs, out_refs))]
        outgoing = []
        for k in range(1, N_DEV):
            to = _flat(*_peer(k))
            for i, (s, o) in enumerate(zip(s_refs, out_refs)):
                outgoing.append(pltpu.make_async_remote_copy(
                    src_ref=s.at[to], dst_ref=o.at[k], send_sem=send_sems.at[i, k - 1], recv_sem=recv_sems.at[i, k - 1],
                    device_id=_peer(k), device_id_type=MESH))
        return local, outgoing, outgoing

    return _Comm(list(sends), [jax.ShapeDtypeStruct(s.shape, s.dtype) for s in sends], copies)


def _comm_scratch(plan):
    n = len(plan.operands)
    return [pltpu.SemaphoreType.DMA((n, N_DEV - 1)), pltpu.SemaphoreType.DMA((n, N_DEV - 1)),
            pltpu.SemaphoreType.DMA((n,))]


def _comm_start(copies):
    local, outgoing, _ = copies
    for cp in local + outgoing:
        cp.start()


def _comm_wait(copies):
    local, outgoing, incoming = copies
    for cp in incoming:
        cp.wait_recv()
    for cp in outgoing:
        cp.wait_send()
    for cp in local:
        cp.wait()


def _run_comm(plan, name):
    n = len(plan.operands)

    def body(*refs):
        copies = plan.copies(refs[:n], refs[n:2 * n], refs[2 * n:])
        _comm_start(copies)
        _comm_wait(copies)

    return _pcall(
        body, name=name,
        in_specs=[pl.BlockSpec(memory_space=pl.ANY)] * n,
        out_specs=[pl.BlockSpec(memory_space=pl.ANY)] * n,
        out_shape=plan.out_shapes,
        scratch_shapes=_comm_scratch(plan),
    )(*plan.operands)


def _all_gather_two_level(blocks, name):
    na = len(blocks)

    def body(*refs):
        x_refs, out_refs = refs[:na], refs[na:2 * na]
        send_sems, recv_sems, local_sems = refs[2 * na:]
        x, y, c = _me()
        me, sibling = (x, y, c), (x, y, 1 - c)
        chips = [(1 - x, y), (x, 1 - y), (1 - x, 1 - y)]

        def copy(a, k, block, to, own=False):
            slot = out_refs[a].at[_flat(*block)]
            return pltpu.make_async_remote_copy(
                src_ref=x_refs[a] if own else slot, dst_ref=slot, send_sem=send_sems.at[a, k],
                recv_sem=recv_sems.at[a, k], device_id=to, device_id_type=MESH)

        mine = [pltpu.make_async_copy(x_refs[a], out_refs[a].at[_flat(*me)], local_sems.at[a]) for a in range(na)]
        first = [copy(a, 0, me, sibling, own=True) for a in range(na)]
        first += [copy(a, 1 + j, me, (*chip, c), own=True) for j, chip in enumerate(chips) for a in range(na)]
        for cp in mine + first:
            cp.start()
        passed = []
        for j, chip in enumerate(chips):
            for a in range(na):
                copy(a, 1 + j, (*chip, c), me).wait_recv()
                cp = copy(a, 4 + j, (*chip, c), sibling)
                cp.start()
                passed.append(cp)
        for a in range(na):
            copy(a, 0, sibling, me).wait_recv()
        for j, chip in enumerate(chips):
            for a in range(na):
                copy(a, 4 + j, (*chip, 1 - c), me).wait_recv()
        for cp in first + passed:
            cp.wait_send()
        for cp in mine:
            cp.wait()

    return _pcall(
        body, name=name,
        in_specs=[pl.BlockSpec(memory_space=pl.ANY)] * na,
        out_specs=[pl.BlockSpec(memory_space=pl.ANY)] * na,
        out_shape=[jax.ShapeDtypeStruct((N_DEV,) + b.shape, b.dtype) for b in blocks],
        scratch_shapes=[pltpu.SemaphoreType.DMA((na, N_DEV - 1)), pltpu.SemaphoreType.DMA((na, N_DEV - 1)),
                        pltpu.SemaphoreType.DMA((na,))],
    )(*blocks)


def _sum_slots(a, name):
    r = a.shape[1]

    def body(a_ref, o_ref):
        acc = a_ref[0]
        for d in range(1, N_DEV):
            acc = acc + a_ref[d]
        o_ref[...] = acc

    return _pcall(body, name=name, out_shape=jax.ShapeDtypeStruct((r, 128), F32), compiler_params=_cparams())(a)


def _all_reduce_small(blk, name):
    r = blk.shape[0]

    def body(x_ref, out_ref, gath, send_sems, recv_sems):
        me = _flat(*_me())
        gath[me] = x_ref[...]
        copies = []
        for k in range(1, N_DEV):
            cp = pltpu.make_async_remote_copy(
                src_ref=x_ref, dst_ref=gath.at[me],
                send_sem=send_sems.at[k - 1], recv_sem=recv_sems.at[k - 1],
                device_id=_peer(k), device_id_type=MESH)
            cp.start()
            copies.append(cp)
        for k in range(1, N_DEV):
            src = _flat(*_peer(k))
            pltpu.make_async_remote_copy(
                src_ref=x_ref, dst_ref=gath.at[src],
                send_sem=send_sems.at[k - 1], recv_sem=recv_sems.at[k - 1],
                device_id=_peer(k), device_id_type=MESH).wait_recv()
        for cp in copies:
            cp.wait_send()
        acc = gath[0]
        for d in range(1, N_DEV):
            acc = acc + gath[d]
        out_ref[...] = acc

    return _pcall(
        body, name=name,
        in_specs=[pl.BlockSpec(memory_space=pltpu.VMEM)],
        out_specs=pl.BlockSpec(memory_space=pltpu.VMEM),
        out_shape=jax.ShapeDtypeStruct((r, 128), F32),
        scratch_shapes=[pltpu.VMEM((N_DEV, r, 128), F32),
                        pltpu.SemaphoreType.DMA((N_DEV - 1,)), pltpu.SemaphoreType.DMA((N_DEV - 1,))],
    )(blk)


def _heads(vec):
    return vec.reshape(NH, 1, DH)


def _rep(vec4):
    return jnp.broadcast_to(vec4.reshape(NH, 1, 1), (NH, 1, DH))


def _onehot_lane(offset):
    m = np.zeros((NH, 1, DH), np.float32)
    for h in range(NH):
        m[h, 0, offset + h] = 1.0
    return jnp.asarray(m)


_TINY = (("gdn_conv_w", (DEPTH, 4, 96)), ("rwkv_w_up", (DEPTH, 64, 32)), ("rwkv_a_up", (DEPTH, 64, 32)),
         ("sc_conv_w", (DEPTH, 3, 32)))
_TINY_ROWS = -(-sum(int(np.prod(s)) for _, s in _TINY) // 1024) * 8


def _pack_rows(arrays, rows, fill=0.0):
    flat = jnp.concatenate([a.reshape(-1) for a in arrays])
    return jnp.pad(flat, (0, rows * 128 - flat.shape[0]), constant_values=fill).reshape(rows, 128)


def _unpack_rows(p, named_shapes):
    lead = p.shape[:-2]
    flat = p.reshape(lead + (-1,))
    out, o = {}, 0
    for n, s in named_shapes:
        size = int(np.prod(s))
        out[n] = flat[..., o:o + size].reshape(lead + tuple(s))
        o += size
    return out


def _gather_last(a):
    return jnp.transpose(a, (1, 0, 2)).reshape(a.shape[1], -1)


def _split_last(a):
    r, c8 = a.shape
    return jnp.transpose(a.reshape(r, N_DEV, c8 // N_DEV), (1, 0, 2))


_SMALL = (("pre_norm_w", (DEPTH, 1024)), ("gdn_a_log", (DEPTH, 4)), ("gdn_dt_bias", (DEPTH, 4)),
          ("gdn_norm_w", (DEPTH, 64)), ("rwkv_mu", (DEPTH, 1152)), ("rwkv_w0", (DEPTH, 256)),
          ("rwkv_a0", (DEPTH, 256)), ("rwkv_k_k", (DEPTH, 256)), ("rwkv_k_a", (DEPTH, 256)),
          ("rwkv_r_k", (DEPTH, 256)), ("rwkv_ln_w", (DEPTH, 256)), ("rwkv_ln_b", (DEPTH, 256)),
          ("gla_a_up", (DEPTH, 16, 128)), ("gla_a_bias", (DEPTH, 128)), ("gla_norm_w", (DEPTH, 64)),
          ("post_norm_w", (DEPTH, 1024)), ("loss", ()))
_SMALL_ROWS = -(-sum(int(np.prod(s)) for _, s in _SMALL) // 1024) * 8


def _big_weights(w_in_all, w_out_all, l):
    return dict(w_g=_regroup_in(w_in_all, f"regroup_in{l}"),
                wout_g=w_out_all.reshape(4 * NH, DH, D_MODEL).astype(BF16))


def _layer_params(wts, tiny, l):
    conv = _gather_last(tiny["gdn_conv_w"][:, l])
    q = {}
    q["gdn_conv"] = jnp.transpose(conv.reshape(GDN_TAPS, 12, DH), (1, 0, 2))
    q["gdn_prm"] = [_rep(wts["gdn_a_log"][l]), _rep(wts["gdn_dt_bias"][l]),
                    jnp.broadcast_to(wts["gdn_norm_w"][l].reshape(1, 1, DH), (NH, 1, DH))]
    q["gdn_cst"] = [_onehot_lane(0), _onehot_lane(NH)]
    q["rwkv_mu"] = wts["rwkv_mu"][l].reshape(18, 1, DH)
    gs, nj = GROUPS_PER_STEP, N_GROUPS // GROUPS_PER_STEP
    side_by_side = lambda a: jnp.transpose(a.reshape(-1, gs, a.shape[1], DH), (0, 2, 1, 3)).reshape(-1, a.shape[1], gs * DH)
    q["conv_blocks"] = jnp.pad(side_by_side(q["gdn_conv"]), ((0, nj - len(CONV_BLOCKS)), (0, 0), (0, 0)))
    singles = jnp.pad(q["rwkv_mu"][16:18].reshape(1, 1, 2 * DH), ((0, 0), (0, 0), (DH, DH)))
    q["mix_blocks"] = jnp.concatenate([jnp.zeros((4, 1, gs * DH), F32), side_by_side(q["rwkv_mu"][:16]),
                                       jnp.zeros((8, 1, gs * DH), F32), singles], axis=0)
    w_up = jnp.transpose(_gather_last(tiny["rwkv_w_up"][:, l]).reshape(64, NH, DH), (1, 0, 2))
    a_up = jnp.transpose(_gather_last(tiny["rwkv_a_up"][:, l]).reshape(64, NH, DH), (1, 0, 2))
    q["rwkv_prm"] = [_heads(wts["rwkv_w0"][l]), w_up, _heads(wts["rwkv_a0"][l]), a_up,
                     _heads(wts["rwkv_k_k"][l]), _heads(wts["rwkv_k_a"][l]), _heads(wts["rwkv_r_k"][l]),
                     _heads(wts["rwkv_ln_w"][l]), _heads(wts["rwkv_ln_b"][l])]
    sc = _gather_last(tiny["sc_conv_w"][:, l])
    q["sc_conv"] = jnp.transpose(sc.reshape(SC_TAPS, NH, DH), (1, 0, 2))
    gla_up = jnp.transpose(wts["gla_a_up"][l].reshape(16, NH, GLA_HEAD_K), (1, 0, 2))
    gla_up = jnp.pad(gla_up, ((0, 0), (0, DH - 16), (0, DH - GLA_HEAD_K)))
    gla_b = jnp.pad(wts["gla_a_bias"][l].reshape(NH, 1, GLA_HEAD_K), ((0, 0), (0, 0), (0, DH - GLA_HEAD_K)))
    q["gla_prm"] = [gla_up, gla_b, jnp.broadcast_to(wts["gla_norm_w"][l].reshape(1, 1, DH), (NH, 1, DH))]
    q["pre_w"] = wts["pre_norm_w"][l].reshape(1, D_MODEL)
    q["post_w"] = wts["post_norm_w"][l].reshape(1, D_MODEL)
    return q


def _mixer_inputs(p, ps):
    gdn = [(ps, 4, 0), (ps, 4, 1), (ps, 4, 2), (p, 4, G_GDN // 4 + 3), (p, 1, G_GDN_AB)]
    rwkv = [(ps, 4, G_PM // 4 + k) for k in range(4)] + [(ps, 1, G_PM_WD), (ps, 1, G_PM_AD)]
    gla = [(p, 4, G_GLA // 4 + k) for k in range(4)] + [(p, 1, G_GLA_AD)]
    return gdn, rwkv, gla


def _layer_fwd(x, q, nb, t, l, side=None):
    hb, p, ps = _norm_proj(x, q["pre_w"], q["w_g"], q["conv_blocks"], q["mix_blocks"], t, f"norm_proj{l}")
    gdn_in, rwkv_in, gla_in = _mixer_inputs(p, ps)
    y_gdn, ck_gdn, _ = _mixer_fwd(_gdn_chunk, f"gdn_fwd{l}", gdn_in, q["gdn_prm"], q["gdn_cst"], nb, t, n_kept=1)
    y_rwkv, ck_rwkv, side_res = _mixer_fwd(_rwkv_chunk, f"rwkv_fwd{l}", rwkv_in, q["rwkv_prm"], [], nb, t,
                                           first_fn=_rwkv_first_chunk, side=side, n_kept=1)
    y_sc = _sc_fwd(p, q["sc_conv"], nb, t, f"sc_fwd{l}")
    y_gla, ck_gla, _ = _mixer_fwd(_gla_chunk, f"gla_fwd{l}", gla_in, q["gla_prm"], [], nb, t)
    ys = (y_gdn, y_rwkv, y_sc, y_gla)
    out, xn = _out_proj_norm(ys, q["wout_g"], x, q["post_w"], f"out_proj{l}")
    saved = dict(x=x, hb=hb, p=p, ps=ps, ys=ys, out=out, ck=(ck_gdn, ck_rwkv, ck_gla))
    return xn, saved, side_res


def _layer_bwd(dxn, q, sv, nb, t, l, side=None, exchange_own=False):
    p, ys = sv["p"], sv["ys"]
    dout, dy, d_post = _post_bwd(dxn, sv["out"], q["post_w"], q["wout_g"], f"post_bwd{l}")
    d_wout = _dwout(ys, dout, f"dwout{l}").reshape(N_DEV, 128, D_MODEL).astype(BF16)
    gdn_in, rwkv_in, gla_in = _mixer_inputs(p, sv["ps"])
    ck_gdn, ck_rwkv, ck_gla = sv["ck"]
    g = {}

    (d_conv, dz, dab), (da_log, ddt, dnw), _ = _mixer_bwd(
        _gdn_chunk, f"gdn_bwd{l}", gdn_in, q["gdn_prm"], q["gdn_cst"], ck_gdn, dy, 0,
        [(12, F32), (4, BF16), (1, BF16)], [(0, 0), (0, 4), (0, 8), (1, 0), (2, 0)], nb, t)
    dconv_in, d_gconv = _conv_bwd(p, G_GDN, 12, q["gdn_conv"], d_conv, nb, t, f"gdn_conv_bwd{l}")
    g["gdn_conv_w"] = jnp.transpose(d_gconv.sum(1), (1, 0, 2)).reshape(GDN_TAPS, 768)
    g["gdn_a_log"] = da_log.sum((0, 2, 3))
    g["gdn_dt_bias"] = ddt.sum((0, 2, 3))
    g["gdn_norm_w"] = dnw.sum((0, 1, 2))

    (d_pm,), d_rprm, side_res = _mixer_bwd(
        _rwkv_chunk, f"rwkv_bwd{l}", rwkv_in, q["rwkv_prm"], [], ck_rwkv, dy, 1,
        [(18, F32)], [(0, 0), (0, 4), (0, 8), (0, 12), (0, 16), (0, 17)], nb, t, first_fn=_rwkv_first_chunk,
        side=side)
    dp_rwkv, d_mu = _mix_bwd(p, q["rwkv_mu"], d_pm, nb, t, f"rwkv_mix_bwd{l}")
    g["rwkv_mu"] = d_mu.sum(1).reshape(1152)
    rp = [a.sum(0) for a in d_rprm]
    g["rwkv_w0"] = rp[0].reshape(256)
    g["rwkv_w_up"] = jnp.transpose(rp[1], (1, 0, 2)).reshape(64, 256)
    g["rwkv_a0"] = rp[2].reshape(256)
    g["rwkv_a_up"] = jnp.transpose(rp[3], (1, 0, 2)).reshape(64, 256)
    for i, nme in enumerate(("rwkv_k_k", "rwkv_k_a", "rwkv_r_k", "rwkv_ln_w", "rwkv_ln_b")):
        g[nme] = rp[4 + i].reshape(256)

    dp_sc, d_scw = _sc_bwd(p, q["sc_conv"], dy, nb, t, f"sc_bwd{l}")
    g["sc_conv_w"] = jnp.transpose(d_scw.sum(1), (1, 0, 2)).reshape(SC_TAPS, 256)

    (dp_gla, dad), (d_aup, d_ab, d_gnw), _ = _mixer_bwd(
        _gla_chunk, f"gla_bwd{l}", gla_in, q["gla_prm"], [], ck_gla, dy, 3,
        [(16, BF16), (1, BF16)], [(0, 0), (0, 4), (0, 8), (0, 12), (1, 0)], nb, t)
    g["gla_a_up"] = jnp.transpose(d_aup.sum(0)[:, :16, :GLA_HEAD_K], (1, 0, 2)).reshape(16, 128)
    g["gla_a_bias"] = d_ab.sum(0)[:, 0, :GLA_HEAD_K].reshape(128)
    g["gla_norm_w"] = d_gnw.sum((0, 1, 2))

    singles = jnp.concatenate([dab, dp_rwkv[16:18], dad], axis=0)
    sources = [dconv_in, dz, dp_rwkv, dp_sc, dp_gla, singles]
    d_win = _regroup_out(_dwin(sv["hb"], sources, f"dwin{l}"), f"regroup_out{l}")
    own = _exchange_plan([d_win, d_wout]) if exchange_own else None
    dx, d_pre, got = _dh_prenorm_bwd(sources, q["w_g"], sv["x"], q["pre_w"], dxn, f"dh_bwd{l}", own)
    if exchange_own:
        d_win, d_wout = got
    g["pre_norm_w"] = d_pre.reshape(D_MODEL)
    g["post_norm_w"] = d_post.reshape(D_MODEL)
    return dx, g, d_win, d_wout, side_res


def _local_step(x, tgt, wts, tiny, w_in_all, w_out_all, later_shards=None):
    nb, t, d = x.shape
    xf = x.reshape(nb * t, d)
    overlap = later_shards is not None
    qs, saved = [], []
    big = _big_weights(w_in_all[0], w_out_all[0], 0)
    for l in range(DEPTH):
        q = dict(_layer_params(wts, tiny, l), **big)
        nxt = l + 1 < DEPTH
        side = _gather_plan(later_shards[l]) if overlap and nxt else None
        xf, sv, got = _layer_fwd(xf, q, nb, t, l, side)
        if nxt:
            big = _big_weights(*(got if overlap else (w_in_all[l + 1], w_out_all[l + 1])), l + 1)
        qs.append(q)
        saved.append(sv)
    dxf, lpart = _loss_grad(xf, tgt.reshape(nb * t, d), "loss")
    grads, d_win, d_wout = [None] * DEPTH, [None] * DEPTH, [None] * DEPTH
    for l in reversed(range(DEPTH)):
        side = _exchange_plan([d_win[l + 1], d_wout[l + 1]]) if overlap and l + 1 < DEPTH else None
        dxf, grads[l], d_win[l], d_wout[l], got = _layer_bwd(dxf, qs[l], saved[l], nb, t, l, side,
                                                             exchange_own=overlap and l == 0)
        if side is not None:
            d_win[l + 1], d_wout[l + 1] = got
    small = {k: jnp.stack([grads[l][k] for l in range(DEPTH)]) for k in grads[0]}
    return lpart[0, 0], dxf.reshape(nb, t, d), small, d_win, d_wout


_WEIGHTS = ("pre_norm_w", "w_in", "gdn_conv_w", "gdn_a_log", "gdn_dt_bias", "gdn_norm_w", "rwkv_mu", "rwkv_w0",
            "rwkv_w_up", "rwkv_a0", "rwkv_a_up", "rwkv_k_k", "rwkv_k_a", "rwkv_r_k", "rwkv_ln_w", "rwkv_ln_b",
            "sc_conv_w", "gla_a_up", "gla_a_bias", "gla_norm_w", "w_out", "post_norm_w")


def kernel(x, pre_norm_w, w_in, gdn_conv_w, gdn_a_log, gdn_dt_bias, gdn_norm_w, rwkv_mu, rwkv_w0, rwkv_w_up, rwkv_a0, rwkv_a_up, rwkv_k_k, rwkv_k_a, rwkv_r_k, rwkv_ln_w, rwkv_ln_b, sc_conv_w, gla_a_up, gla_a_bias, gla_norm_w, w_out, post_norm_w, loss_target, m_pre_norm_w, m_w_in, m_gdn_conv_w, m_gdn_a_log, m_gdn_dt_bias, m_gdn_norm_w, m_rwkv_mu, m_rwkv_w0, m_rwkv_w_up, m_rwkv_a0, m_rwkv_a_up, m_rwkv_k_k, m_rwkv_k_a, m_rwkv_r_k, m_rwkv_ln_w, m_rwkv_ln_b, m_sc_conv_w, m_gla_a_up, m_gla_a_bias, m_gla_norm_w, m_w_out, m_post_norm_w, v_pre_norm_w, v_w_in, v_gdn_conv_w, v_gdn_a_log, v_gdn_dt_bias, v_gdn_norm_w, v_rwkv_mu, v_rwkv_w0, v_rwkv_w_up, v_rwkv_a0, v_rwkv_a_up, v_rwkv_k_k, v_rwkv_k_a, v_rwkv_r_k, v_rwkv_ln_w, v_rwkv_ln_b, v_sc_conv_w, v_gla_a_up, v_gla_a_bias, v_gla_norm_w, v_w_out, v_post_norm_w):
    env = dict(locals())
    w = {n: env[n] for n in _WEIGHTS}
    m = {n: env["m_" + n] for n in _WEIGHTS}
    v = {n: env["v_" + n] for n in _WEIGHTS}
    tiny_names = [n for n, _ in _TINY]

    w_in_b, w_out_b = w_in.astype(BF16), w_out.astype(BF16)
    w_in_0, w_out_0, tiny_all = _all_gather_two_level(
        [w_in_b[0], w_out_b[0], _pack_rows([w[n] for n in tiny_names], _TINY_ROWS)], "gather_weights")
    tiny = _unpack_rows(tiny_all, _TINY)

    lpart, grad_x, small, r_win, r_wout = _local_step(
        x, loss_target, w, tiny, [w_in_0], [w_out_0], later_shards=[(w_in_b[l], w_out_b[l]) for l in range(1, DEPTH)])

    tiny_send = jnp.stack([_pack_rows([_split_last(small[n][l])[d] for n in tiny_names for l in range(DEPTH)],
                                      _TINY_ROWS) for d in range(N_DEV)])
    (r_tiny,) = _run_comm(_exchange_plan([tiny_send]), "scatter_grads")
    grads, delta, new_m, new_v = {}, {}, {}, {}
    for n, parts in (("w_in", r_win), ("w_out", r_wout)):
        res = [_sum_adamw(parts[l], w[n][l], m[n][l], v[n][l], f"adamw_{n}{l}") for l in range(DEPTH)]
        grads[n], delta[n], new_m[n], new_v[n] = [jnp.stack(o) for o in zip(*res)]
    tiny_sum = _sum_slots(r_tiny, "sum_tiny").reshape(-1)
    o = 0
    for n, s in _TINY:
        size = int(np.prod(s))
        grads[n] = tiny_sum[o:o + size].reshape(s)
        o += size

    small = dict(small)
    small["loss"] = lpart
    red = _unpack_rows(_all_reduce_small(_pack_rows([small[n] for n, _ in _SMALL], _SMALL_ROWS), "reduce_small"),
                       _SMALL)
    loss = red.pop("loss")
    grads.update(red)

    rest = [n for n in _WEIGHTS if n not in ("w_in", "w_out")]
    rest_shapes = [(n, w[n].shape) for n in rest]
    rows = -(-sum(int(np.prod(s)) for _, s in rest_shapes) // 1024) * 8
    outs = _adamw(_pack_rows([w[n] for n in rest], rows), _pack_rows([grads[n] for n in rest], rows),
                  _pack_rows([m[n] for n in rest], rows), _pack_rows([v[n] for n in rest], rows, 1.0), "adamw_rest")
    for dst, packed in zip((delta, new_m, new_v), outs):
        dst.update(_unpack_rows(packed, rest_shapes))

    return (loss, grad_x, *[grads[n] for n in _WEIGHTS], *[delta[n] for n in _WEIGHTS],
            *[new_m[n] for n in _WEIGHTS], *[new_v[n] for n in _WEIGHTS])
```

```python
import collections
import functools
import math

import numpy as np
import jax
import jax.numpy as jnp
from jax import lax
from jax.experimental import pallas as pl
from jax.experimental.pallas import tpu as pltpu

F32 = jnp.float32
BF16 = jnp.bfloat16

D_MODEL = 1024
DEPTH = 2
NH = 4
DH = 64
CH = 64
EPS = 1e-6
RWKV_GN_EPS = 64e-5
GLA_HEAD_K = 32
GLA_TAU = 16.0
GDN_TAPS = 4
SC_TAPS = 3
D_IN = 3992
N_DEV = 8
SHARD_COLS = D_IN // N_DEV

G_GDN = 0
G_RWKV = 16
G_SC = 32
G_GLA = 48
G_GDN_AB, G_RWKV_WD, G_RWKV_AD, G_GLA_AD = 64, 65, 66, 67
N_GROUPS = 68
GROUPS_PER_STEP = 4
TIME_BLOCK = 256
RWKV_EXACT_STEPS = 16

C_GDN, C_RWKV, C_SC, C_GLA = 0, 1032, 2184, 3208

ADAM_LR, ADAM_B1, ADAM_B2, ADAM_EPS, ADAM_WD, ADAM_STEP = 0.001, 0.9, 0.999, 1e-08, 0.01, 10

VMEM_LIMIT = 56 * 1024 * 1024
MESH = pl.DeviceIdType.MESH

_pcall = pl.pallas_call

_Comm = collections.namedtuple("_Comm", "operands out_shapes copies")


def _cparams(sem=None):
    if sem is None:
        return pltpu.CompilerParams(vmem_limit_bytes=VMEM_LIMIT)
    return pltpu.CompilerParams(dimension_semantics=sem, vmem_limit_bytes=VMEM_LIMIT)


def _group_segments():
    table = [(G_GDN + i, C_GDN + DH * i, DH) for i in range(16)]
    table.append((G_GDN_AB, C_GDN + 1024, 8))
    table += [(G_RWKV + i, C_RWKV + DH * i, DH) for i in range(16)]
    table += [(G_RWKV_WD, C_RWKV + 1024, DH), (G_RWKV_AD, C_RWKV + 1088, DH)]
    table += [(G_SC + 4 * j + k, C_SC + 256 * k + DH * j, DH) for j in range(NH) for k in range(4)]
    for h in range(NH):
        table += [(G_GLA + h, C_GLA + GLA_HEAD_K * h, GLA_HEAD_K),
                  (G_GLA + 4 + h, C_GLA + 128 + GLA_HEAD_K * h, GLA_HEAD_K),
                  (G_GLA + 8 + h, C_GLA + 256 + DH * h, DH),
                  (G_GLA + 12 + h, C_GLA + 512 + DH * h, DH)]
    table.append((G_GLA_AD, C_GLA + 768, 16))
    segs, padded = [], []
    for g, c, n in table:
        if n < DH:
            padded.append(g)
        a = 0
        while n > 0:
            d, off = divmod(c, SHARD_COLS)
            ln = min(n, SHARD_COLS - off)
            segs.append((g, a, d, off, ln))
            c, a, n = c + ln, a + ln, n - ln
    return segs, padded


_SEGMENTS, _PADDED_GROUPS = _group_segments()


def _dn(ta, tb):
    return (((1 if ta else 2,), (2 if tb else 1,)), ((0,), (0,)))


def _hdot(a, b, ta=False, tb=False):
    return lax.dot_general(a, b, _dn(ta, tb), precision=lax.Precision.HIGH, preferred_element_type=F32)


def _r(x):
    return x.astype(BF16)


def _rdot(a, b, ta=False, tb=False):
    return lax.dot_general(_r(a), _r(b), _dn(ta, tb), preferred_element_type=F32)


@jax.custom_vjp
def _bmm(a, b):
    return _rdot(a, b)


def _bmm_fwd(a, b):
    return _rdot(a, b), (a, b)


def _bmm_bwd(res, g):
    a, b = res
    return _rdot(g, b, tb=True), _rdot(a, g, ta=True)


_bmm.defvjp(_bmm_fwd, _bmm_bwd)


@jax.custom_vjp
def _bmm_nt(a, b):
    return _rdot(a, b, tb=True)


def _bmm_nt_fwd(a, b):
    return _rdot(a, b, tb=True), (a, b)


def _bmm_nt_bwd(res, g):
    a, b = res
    return _rdot(g, b), _rdot(g, a, ta=True)


_bmm_nt.defvjp(_bmm_nt_fwd, _bmm_nt_bwd)


@jax.custom_vjp
def _bmm_tn(a, b):
    return _rdot(a, b, ta=True)


def _bmm_tn_fwd(a, b):
    return _rdot(a, b, ta=True), (a, b)


def _bmm_tn_bwd(res, g):
    a, b = res
    return _rdot(b, g, tb=True), _rdot(a, g)


_bmm_tn.defvjp(_bmm_tn_fwd, _bmm_tn_bwd)


def _tri(n):
    i = lax.broadcasted_iota(jnp.int32, (n, n), 0)
    j = lax.broadcasted_iota(jnp.int32, (n, n), 1)
    return i >= j, i > j, i == j


def _heads_of(x, like):
    n = like.shape[0]
    if x.ndim == 2:
        return jnp.broadcast_to(x[None], (n,) + x.shape)
    seqs = x.shape[0]
    return jnp.broadcast_to(x[:, None], (seqs, n // seqs) + x.shape[1:]).reshape((n,) + x.shape[1:])


def _cumsum_rows(x):
    incl, _, _ = _tri(x.shape[-2])
    return _hdot(_heads_of(incl.astype(F32), x), x)


@jax.custom_vjp
def _inv_unit_lower(a):
    n = a.shape[-1]
    _, _, eye = _tri(n)
    pw = -a
    inv = eye.astype(F32) + pw
    for _ in range(math.ceil(math.log2(n)) - 1):
        pw = _hdot(pw, pw)
        inv = inv + _hdot(inv, pw)
    return inv


def _inv_unit_lower_fwd(a):
    inv = _inv_unit_lower(a)
    return inv, inv


def _inv_unit_lower_bwd(inv, g):
    return (-_hdot(_hdot(inv, g, ta=True), inv, tb=True),)


_inv_unit_lower.defvjp(_inv_unit_lower_fwd, _inv_unit_lower_bwd)


@jax.custom_vjp
def _inv_reuse(a, inv):
    return inv


def _inv_reuse_fwd(a, inv):
    return inv, inv


def _inv_reuse_bwd(inv, g):
    return _inv_unit_lower_bwd(inv, g)[0], jnp.zeros_like(inv)


_inv_reuse.defvjp(_inv_reuse_fwd, _inv_reuse_bwd)


def _silu(x):
    return x * jax.nn.sigmoid(x)


def _t(x):
    return jnp.swapaxes(x, -1, -2)


def _gdn_chunk(prm, cst, ins, s, tinv=None):
    a_log, dt_b, nw = prm
    m_a, m_b = cst
    cq, ck, cv, z, ab = ins
    ab = _heads_of(ab, m_a)
    incl, strict, _ = _tri(CH)
    q = _silu(cq)
    k = _silu(ck)
    v = _silu(cv)
    q = q * lax.rsqrt(jnp.sum(q * q, -1, keepdims=True) + EPS) * (DH ** -0.5)
    k = k * lax.rsqrt(jnp.sum(k * k, -1, keepdims=True) + EPS)
    a_raw = jnp.sum(ab * m_a, -1, keepdims=True)
    b_raw = jnp.sum(ab * m_b, -1, keepdims=True)
    gstep = -jnp.exp(a_log) * jax.nn.softplus(a_raw + dt_b)
    beta = jax.nn.sigmoid(b_raw)
    gc = _cumsum_rows(gstep)
    gl = jnp.sum(gstep, -2, keepdims=True)
    dec = jnp.where(incl, jnp.exp(jnp.where(incl, gc - _t(gc), 0.0)), 0.0)
    kb = k * beta
    a_mat = jnp.where(strict, _bmm_nt(kb, k) * dec, 0.0)
    tinv = _inv_unit_lower(a_mat) if tinv is None else _inv_reuse(a_mat, tinv)
    eg = jnp.exp(gc)
    u = _hdot(tinv, v * beta)
    w = _hdot(tinv, kb * eg)
    attn = _bmm_nt(q, k) * dec
    v_new = u - _bmm(w, s)
    o = _bmm(q * eg, s) + _bmm(attn, v_new)
    s_next = s * jnp.exp(gl) + _bmm_tn(k * jnp.exp(gl - gc), v_new)
    on = o * lax.rsqrt(jnp.mean(o * o, -1, keepdims=True) + EPS) * nw
    return on * _silu(z), s_next, tinv


def _gla_chunk(prm, cst, ins, st):
    a_up, a_bias, nw = prm
    q, k, v, z, ad = ins
    incl, _, _ = _tri(CH)
    la = jax.nn.log_sigmoid(_bmm(_heads_of(ad, a_up), a_up) + a_bias) * (1.0 / GLA_TAU)
    bc = _cumsum_rows(la)
    bl = jnp.sum(la, -2, keepdims=True)
    qe = q * (GLA_HEAD_K ** -0.5) * jnp.exp(bc)
    ke = k * jnp.exp(-bc)
    attn = jnp.where(incl, _bmm_nt(qe, ke), 0.0)
    o = _bmm_nt(qe, st) + _bmm(attn, v)
    st_next = st * jnp.exp(bl) + _bmm_tn(v, k * jnp.exp(bl - bc))
    on = o * lax.rsqrt(jnp.mean(o * o, -1, keepdims=True) + EPS) * nw
    return on * _silu(z), st_next


def _rwkv_chunk(prm, cst, ins, s, inv=None):
    r, v = ins[0], ins[2]
    incl, strict, _ = _tri(r.shape[-2])
    lw, kk, k2, m = _rwkv_pre(prm, ins)
    cum = _cumsum_rows(lw)
    ltot = jnp.sum(lw, -2, keepdims=True)
    n_t = -kk * jnp.exp(cum - lw)
    einv = jnp.exp(-cum)
    m_t = m * einv
    k_t = k2 * einv
    r_t = r * jnp.exp(cum)
    a_nm = jnp.where(strict, _hdot(n_t, m_t, tb=True), 0.0)
    a_nk = jnp.where(strict, _hdot(n_t, k_t, tb=True), 0.0)
    inv = _inv_unit_lower(-a_nm) if inv is None else _inv_reuse(-a_nm, inv)
    cm = _hdot(inv, _hdot(n_t, s, tb=True) + _bmm(a_nk, v))
    y = (_bmm_nt(r_t, s) + _bmm(jnp.where(incl, _hdot(r_t, m_t, tb=True), 0.0), cm)
         + _bmm(jnp.where(incl, _hdot(r_t, k_t, tb=True), 0.0), v))
    eend = jnp.exp(ltot - cum)
    s_next = s * jnp.exp(ltot) + _bmm_tn(cm, m * eend) + _bmm_tn(v, k2 * eend)
    return _rwkv_post(prm, ins, y, k2), s_next, inv


def _rwkv_pre(prm, ins):
    w0, w_up, a0, a_up, k_k, k_a = prm[:6]
    k, wd, ad = ins[1], ins[4], ins[5]
    lw = -math.exp(-0.5) * jax.nn.sigmoid(w0 + _bmm(_heads_of(jnp.tanh(wd), w_up), w_up))
    a = jax.nn.sigmoid(a0 + _bmm(_heads_of(ad, a_up), a_up))
    kk = k * k_k
    kk = kk * lax.rsqrt(jnp.sum(kk * kk, -1, keepdims=True) + EPS)
    k2 = k * (1.0 + (a - 1.0) * k_a)
    return lw, kk, k2, kk * a


def _rwkv_post(prm, ins, y, k2):
    r_k, ln_w, ln_b = prm[6:]
    r, v, z = ins[0], ins[2], ins[3]
    mean = jnp.mean(y, -1, keepdims=True)
    yc = y - mean
    var = jnp.mean(yc * yc, -1, keepdims=True)
    yn = yc * lax.rsqrt(var + RWKV_GN_EPS) * ln_w + ln_b
    bonus = jnp.sum(r * k2 * r_k, -1, keepdims=True) * v
    return (yn + bonus) * _silu(z)


@jax.custom_vjp
def _bmv(s, x):
    return jnp.sum(_r(s).astype(F32) * _r(x).astype(F32), -1, keepdims=True)


def _bmv_fwd(s, x):
    return _bmv(s, x), (s, x)


def _bmv_bwd(res, g):
    s, x = res
    return g * x, jnp.sum(_r(s).astype(F32) * _r(g).astype(F32), -2, keepdims=True)


_bmv.defvjp(_bmv_fwd, _bmv_bwd)


def _rwkv_steps(prm, cst, ins, s, steps):
    r, v = ins[0], ins[2]
    lw, kk, k2, m = _rwkv_pre(prm, ins)
    w = jnp.exp(lw)
    v_t = _t(v)
    lane = lax.broadcasted_iota(jnp.int32, (1, 1, CH), 2)
    y_t = jnp.zeros((s.shape[0], DH, CH), F32)
    for t in range(steps):
        e_t = (lane == t).astype(F32)
        row = (slice(None), slice(t, t + 1))
        sa = _bmv(s, -kk[row])
        s = s * w[row] + sa * m[row] + jnp.sum(v_t * e_t, -1, keepdims=True) * k2[row]
        y_t = y_t + _bmv(s, r[row]) * e_t
    return _rwkv_post(prm, ins, _t(y_t), k2)[:, :steps], s


def _rwkv_first_chunk(prm, cst, ins, s):
    k = RWKV_EXACT_STEPS
    y_head, s = _rwkv_steps(prm, cst, ins, s, k)
    y_tail, s, _ = _rwkv_chunk(prm, cst, [x[..., k:, :] for x in ins], s)
    return jnp.concatenate([y_head, y_tail], axis=-2), s


def _time_block(t):
    return TIME_BLOCK if t % TIME_BLOCK == 0 else t


def _load_chunk(ref, i):
    nb = ref.shape[1]
    if ref.shape[0] == NH:
        return jnp.concatenate([ref[:, b, pl.ds(i, CH), :] for b in range(nb)], axis=0)
    return ref[0, :, pl.ds(i, CH), :]


def _mixer_fwd(chunk_fn, name, ins, prm, cst, nb, t, first_fn=None, side=None, n_kept=0):
    tb = _time_block(t)
    nt, ncb, nch = t // tb, tb // CH, nb * NH
    n_in, n_prm, n_cst = len(ins), len(prm), len(cst)
    n_main, n_side = n_in + n_prm + n_cst, len(side.operands) if side else 0

    def body(*refs):
        in_refs = refs[:n_in]
        prm_refs = refs[n_in:n_in + n_prm]
        cst_refs = refs[n_in + n_prm:n_main]
        side_in = refs[n_main:n_main + n_side]
        y_ref, ck_ref = refs[n_main + n_side:n_main + n_side + 2]
        side_out = refs[n_main + n_side + 2:n_main + 2 * n_side + 2]
        s_scr = refs[n_main + 2 * n_side + 2]
        sems = refs[n_main + 2 * n_side + 3:]
        step_t = pl.program_id(0)

        if side is not None:
            @pl.when(step_t == 0)
            def _():
                _comm_start(side.copies(side_in, side_out, sems))

        @pl.when(step_t == 0)
        def _():
            s_scr[...] = jnp.zeros_like(s_scr)

        def chunk(c, i, fn=chunk_fn):
            s = s_scr[...]
            y, s_next, *kept = fn([jnp.tile(r[...], (nb, 1, 1)) for r in prm_refs],
                                  [jnp.tile(r[...], (nb, 1, 1)) for r in cst_refs],
                                  [_load_chunk(r, i) for r in in_refs], s)
            kept += [jnp.zeros_like(s)] * (n_kept - len(kept))
            for e, a in enumerate([s] + kept):
                ck_ref[c, e] = a
            for b in range(nb):
                y_ref[:, b, pl.ds(i, CH), :] = y[b * NH:(b + 1) * NH].astype(BF16)
            s_scr[...] = s_next

        def step(c, carry):
            chunk(c, pl.multiple_of(c * CH, CH))
            return carry

        if first_fn is None:
            lax.fori_loop(0, ncb, step, 0)
        else:
            @pl.when(step_t == 0)
            def _():
                chunk(0, 0, first_fn)

            @pl.when(step_t != 0)
            def _():
                chunk(0, 0)

            lax.fori_loop(1, ncb, step, 0)

        if side is not None:
            @pl.when(step_t == nt - 1)
            def _():
                _comm_wait(side.copies(side_in, side_out, sems))

    hbm = pl.BlockSpec(memory_space=pl.ANY)
    in_specs = [pl.BlockSpec((ng, nb, tb, DH), (lambda j, bi=bi: (bi, 0, j, 0))) for _, ng, bi in ins]
    in_specs += [pl.BlockSpec(p.shape, lambda j: (0, 0, 0)) for p in list(prm) + list(cst)]
    y, ck, *side_res = _pcall(
        body, name=name, grid=(nt,),
        in_specs=in_specs + [hbm] * n_side,
        out_specs=[pl.BlockSpec((NH, nb, tb, DH), lambda j: (0, 0, j, 0)),
                   pl.BlockSpec((ncb, 1 + n_kept, nch, DH, DH), lambda j: (j, 0, 0, 0, 0))] + [hbm] * n_side,
        out_shape=[jax.ShapeDtypeStruct((NH, nb, t, DH), BF16),
                   jax.ShapeDtypeStruct((t // CH, 1 + n_kept, nch, DH, DH), F32)]
        + (side.out_shapes if side else []),
        scratch_shapes=[pltpu.VMEM((nch, DH, DH), F32)] + (_comm_scratch(side) if side else []),
        compiler_params=_cparams(("arbitrary",)),
    )(*[a.reshape(a.shape[0], nb, t, DH) for a, _, _ in ins], *prm, *cst, *(side.operands if side else []))
    return y.reshape(NH, nb * t, DH), ck, side_res


def _mixer_bwd(chunk_fn, name, ins, prm, cst, ck, dy, dy_block, outs, routes, nb, t, first_fn=None, side=None):
    tb = _time_block(t)
    nt, ncb, nch = t // tb, tb // CH, nb * NH
    n_in, n_prm, n_cst, n_out = len(ins), len(prm), len(cst), len(outs)
    n_main, n_side = n_in + n_prm + n_cst + 2, len(side.operands) if side else 0
    n_kept = ck.shape[1] - 1

    def body(*refs):
        in_refs = refs[:n_in]
        prm_refs = refs[n_in:n_in + n_prm]
        cst_refs = refs[n_in + n_prm:n_in + n_prm + n_cst]
        ck_ref, dy_ref = refs[n_main - 2:n_main]
        side_in = refs[n_main:n_main + n_side]
        rest = refs[n_main + n_side:]
        out_refs = rest[:n_out]
        dprm_refs = rest[n_out:n_out + n_prm]
        side_out = rest[n_out + n_prm:n_out + n_prm + n_side]
        ds_scr = rest[n_out + n_prm + n_side]
        sems = rest[n_out + n_prm + n_side + 1:]
        step_t = pl.program_id(0)

        if side is not None:
            @pl.when(step_t == 0)
            def _():
                _comm_start(side.copies(side_in, side_out, sems))

        @pl.when(step_t == 0)
        def _():
            ds_scr[...] = jnp.zeros_like(ds_scr)
            for r in dprm_refs:
                r[...] = jnp.zeros_like(r)

        def chunk(c, i, fn=chunk_fn):
            cst_v = [jnp.tile(r[...], (nb, 1, 1)) for r in cst_refs]
            kept = [ck_ref[c, 1 + e] for e in range(n_kept)] if fn is chunk_fn else []
            _, vjp = jax.vjp(lambda p, x, s: fn(p, cst_v, x, s, *kept)[:2],
                             [jnp.tile(r[...], (nb, 1, 1)) for r in prm_refs],
                             [_load_chunk(r, i) for r in in_refs], ck_ref[c, 0])
            dy_c = jnp.concatenate([dy_ref[:, b, pl.ds(i, CH), :] for b in range(nb)], axis=0)
            d_prm, d_ins, d_s = vjp((dy_c, ds_scr[...]))
            for (oi, g0), r, g in zip(routes, in_refs, d_ins):
                o_ref = out_refs[oi]
                if r.shape[0] == NH:
                    for b in range(nb):
                        o_ref[g0:g0 + NH, b, pl.ds(i, CH), :] = g[b * NH:(b + 1) * NH].astype(o_ref.dtype)
                else:
                    o_ref[g0, :, pl.ds(i, CH), :] = g.astype(o_ref.dtype)
            for r, g in zip(dprm_refs, d_prm):
                r[...] += g
            ds_scr[...] = d_s

        def step(j, carry):
            c = ncb - 1 - j
            chunk(c, pl.multiple_of(c * CH, CH))
            return carry

        lax.fori_loop(0, ncb - 1, step, 0)
        if first_fn is None:
            chunk(0, 0)
        else:
            @pl.when(step_t == nt - 1)
            def _():
                chunk(0, 0, first_fn)

            @pl.when(step_t != nt - 1)
            def _():
                chunk(0, 0)

        if side is not None:
            @pl.when(step_t == nt - 1)
            def _():
                _comm_wait(side.copies(side_in, side_out, sems))

    def back(j):
        return nt - 1 - j

    hbm = pl.BlockSpec(memory_space=pl.ANY)
    in_specs = [pl.BlockSpec((ng, nb, tb, DH), (lambda j, bi=bi: (bi, 0, back(j), 0))) for _, ng, bi in ins]
    in_specs += [pl.BlockSpec(p.shape, lambda j: (0, 0, 0)) for p in list(prm) + list(cst)]
    in_specs += [pl.BlockSpec((ncb, 1 + n_kept, nch, DH, DH), lambda j: (back(j), 0, 0, 0, 0)),
                 pl.BlockSpec((NH, nb, tb, DH), lambda j: (dy_block, 0, back(j), 0))]
    out_specs = [pl.BlockSpec((ng, nb, tb, DH), lambda j: (0, 0, back(j), 0)) for ng, _ in outs]
    out_specs += [pl.BlockSpec((nch,) + p.shape[1:], lambda j: (0, 0, 0)) for p in prm]
    out_shape = [jax.ShapeDtypeStruct((ng, nb, t, DH), dt) for ng, dt in outs]
    out_shape += [jax.ShapeDtypeStruct((nch,) + p.shape[1:], F32) for p in prm]
    res = _pcall(
        body, name=name, grid=(nt,),
        in_specs=in_specs + [hbm] * n_side, out_specs=out_specs + [hbm] * n_side,
        out_shape=out_shape + (side.out_shapes if side else []),
        scratch_shapes=[pltpu.VMEM((nch, DH, DH), F32)] + (_comm_scratch(side) if side else []),
        compiler_params=_cparams(("arbitrary",)),
    )(*[a.reshape(a.shape[0], nb, t, DH) for a, _, _ in ins], *prm, *cst, ck, dy.reshape(dy.shape[0], nb, t, DH),
      *(side.operands if side else []))
    d_outs = [o.reshape(o.shape[0], nb * t, DH) for o in res[:n_out]]
    d_prm = [g.reshape((nb,) + p.shape) for g, p in zip(res[n_out:n_out + n_prm], prm)]
    return d_outs, d_prm, res[n_out + n_prm:]


def _shift_down(x, s):
    if s == 0:
        return x
    row = lax.broadcasted_iota(jnp.int32, x.shape, 0)
    return jnp.where(row < s, 0.0, pltpu.roll(x, s, 0))


def _shift_up(x, s):
    if s == 0:
        return x
    t = x.shape[0]
    row = lax.broadcasted_iota(jnp.int32, x.shape, 0)
    return jnp.where(row >= t - s, 0.0, pltpu.roll(x, t - s, 0))


def _conv_bwd(p, g0, ng, w, dy, nb, t, name):
    taps = w.shape[1]

    def body(x_ref, w_ref, dy_ref, dx_ref, dw_ref):
        x = x_ref[...]
        d = dy_ref[...]
        acc = w_ref[taps - 1:taps, :] * d
        dw_ref[taps - 1:taps, :] = jnp.sum(d * x, 0, keepdims=True)
        for i in range(taps - 1):
            s = taps - 1 - i
            acc = acc + w_ref[i:i + 1, :] * _shift_up(d, s)
            dw_ref[i:i + 1, :] = jnp.sum(d * _shift_down(x, s), 0, keepdims=True)
        dx_ref[...] = acc.astype(BF16)

    return _pcall(
        body, name=name, grid=(ng, nb),
        in_specs=[pl.BlockSpec((None, t, DH), lambda g, b: (g0 + g, b, 0)),
                  pl.BlockSpec((None, taps, DH), lambda g, b: (g, 0, 0)),
                  pl.BlockSpec((None, t, DH), lambda g, b: (g, b, 0))],
        out_specs=[pl.BlockSpec((None, t, DH), lambda g, b: (g, b, 0)),
                   pl.BlockSpec((None, None, taps, DH), lambda g, b: (g, b, 0, 0))],
        out_shape=[jax.ShapeDtypeStruct((ng, nb * t, DH), BF16),
                   jax.ShapeDtypeStruct((ng, nb, taps, DH), F32)],
        compiler_params=_cparams(("parallel", "parallel")),
    )(p, w, dy)


def _mix_group(g):
    return jnp.where(g < 16, G_RWKV + g, G_RWKV_WD + g - 16)


def _mix_bwd(p, mu, dy, nb, t, name):
    def body(x_ref, mu_ref, dy_ref, dx_ref, dmu_ref):
        x = x_ref[...]
        muv = mu_ref[...]
        d = dy_ref[...]
        dx_ref[...] = (d * (1.0 - muv) + _shift_up(d * muv, 1)).astype(BF16)
        dmu_ref[...] = jnp.sum(d * (_shift_down(x, 1) - x), 0, keepdims=True)

    return _pcall(
        body, name=name, grid=(18, nb),
        in_specs=[pl.BlockSpec((None, t, DH), lambda g, b: (_mix_group(g), b, 0)),
                  pl.BlockSpec((None, 1, DH), lambda g, b: (g, 0, 0)),
                  pl.BlockSpec((None, t, DH), lambda g, b: (g, b, 0))],
        out_specs=[pl.BlockSpec((None, t, DH), lambda g, b: (g, b, 0)),
                   pl.BlockSpec((None, None, 1, DH), lambda g, b: (g, b, 0, 0))],
        out_shape=[jax.ShapeDtypeStruct((18, nb * t, DH), BF16),
                   jax.ShapeDtypeStruct((18, nb, 1, DH), F32)],
        compiler_params=_cparams(("parallel", "parallel")),
    )(p, mu, dy)


def _sc_bwd(p, w, dy, nb, t, name):
    def body(p_ref, w_ref, dy_ref, dp_ref, dw_ref):
        bg, cg, xg, z = p_ref[0], p_ref[1], p_ref[2], p_ref[3]
        d = dy_ref[...]
        u = cg * xg
        u1 = _shift_down(u, 1)
        u2 = _shift_down(u, 2)
        conv = w_ref[2:3, :] * u + w_ref[1:2, :] * u1 + w_ref[0:1, :] * u2
        sg = jax.nn.sigmoid(z)
        sz = z * sg
        dp_ref[0] = (d * conv * sz).astype(BF16)
        dp_ref[3] = (d * bg * conv * (sg * (1.0 + z * (1.0 - sg)))).astype(BF16)
        dconv = d * bg * sz
        du = w_ref[2:3, :] * dconv + w_ref[1:2, :] * _shift_up(dconv, 1) + w_ref[0:1, :] * _shift_up(dconv, 2)
        dp_ref[1] = (du * xg).astype(BF16)
        dp_ref[2] = (du * cg).astype(BF16)
        dw_ref[2:3, :] = jnp.sum(dconv * u, 0, keepdims=True)
        dw_ref[1:2, :] = jnp.sum(dconv * u1, 0, keepdims=True)
        dw_ref[0:1, :] = jnp.sum(dconv * u2, 0, keepdims=True)

    return _pcall(
        body, name=name, grid=(NH, nb),
        in_specs=[pl.BlockSpec((4, t, DH), lambda j, b: (G_SC // 4 + j, b, 0)),
                  pl.BlockSpec((None, SC_TAPS, DH), lambda j, b: (j, 0, 0)),
                  pl.BlockSpec((None, t, DH), lambda j, b: (8 + j, b, 0))],
        out_specs=[pl.BlockSpec((4, t, DH), lambda j, b: (j, b, 0)),
                   pl.BlockSpec((None, None, SC_TAPS, DH), lambda j, b: (j, b, 0, 0))],
        out_shape=[jax.ShapeDtypeStruct((4 * NH, nb * t, DH), BF16),
                   jax.ShapeDtypeStruct((NH, nb, SC_TAPS, DH), F32)],
        compiler_params=_cparams(("parallel", "parallel")),
    )(p, w, dy)


def _row_tile(n):
    return 1024 if n % 1024 == 0 else n


def _regroup_in(w_all, name):
    tr = 256
    gs = GROUPS_PER_STEP

    def body(w_ref, o_ref):
        for g in _PADDED_GROUPS:
            o_ref[g // gs, :, DH * (g % gs):DH * (g % gs + 1)] = jnp.zeros((tr, DH), BF16)
        for g, a, d, off, ln in _SEGMENTS:
            lane = DH * (g % gs) + a
            o_ref[g // gs, :, lane:lane + ln] = w_ref[d, :, off:off + ln].astype(BF16)

    return _pcall(
        body, name=name, grid=(D_MODEL // tr,),
        in_specs=[pl.BlockSpec((N_DEV, tr, SHARD_COLS), lambda i: (0, i, 0))],
        out_specs=pl.BlockSpec((N_GROUPS // gs, tr, gs * DH), lambda i: (0, i, 0)),
        out_shape=jax.ShapeDtypeStruct((N_GROUPS // gs, D_MODEL, gs * DH), BF16),
        compiler_params=_cparams(("parallel",)),
    )(w_all)


def _regroup_out(dwg, name):
    tr = 256
    gs = GROUPS_PER_STEP

    def body(g_ref, o_ref):
        for g, a, d, off, ln in _SEGMENTS:
            lane = DH * (g % gs) + a
            o_ref[d, :, off:off + ln] = g_ref[g // gs, :, lane:lane + ln].astype(BF16)

    return _pcall(
        body, name=name, grid=(D_MODEL // tr,),
        in_specs=[pl.BlockSpec((N_GROUPS // gs, tr, gs * DH), lambda i: (0, i, 0))],
        out_specs=pl.BlockSpec((N_DEV, tr, SHARD_COLS), lambda i: (0, i, 0)),
        out_shape=jax.ShapeDtypeStruct((N_DEV, D_MODEL, SHARD_COLS), BF16),
        compiler_params=_cparams(("parallel",)),
    )(dwg)


CONV_BLOCKS = (0, 1, 2)
MIX_BLOCKS = (4, 5, 6, 7, 16)
SHIFTED_BLOCKS = {0: 0, 1: 1, 2: 2, 4: 3, 5: 4, 6: 5, 7: 6, 16: 7}
G_PM = 12
G_PM_WD, G_PM_AD = 29, 30
SC_BLOCK0 = G_SC // GROUPS_PER_STEP


def _norm_proj(x, pre_w, w_g, conv_w, mix_mu, sc_w, t, name):
    n = x.shape[0]
    tm = _row_tile(n)
    gs = GROUPS_PER_STEP
    nj = N_GROUPS // gs
    assert t % tm == 0, (t, tm)
    per_seq = t // tm

    def shifted_block(j):
        out = jnp.int32(len(SHIFTED_BLOCKS) - 1)
        for jj in sorted(SHIFTED_BLOCKS, reverse=True):
            out = jnp.where(j < jj, max(SHIFTED_BLOCKS[jj] - 1, 0), out)
        for jj, b in SHIFTED_BLOCKS.items():
            out = jnp.where(j == jj, b, out)
        return out

    def body(x_ref, pw_ref, w_ref, cw_ref, mu_ref, scw_ref, h_ref, p_ref, s_ref, ysc_ref, carry):
        i, j = pl.program_id(0), pl.program_id(1)

        @pl.when(j == 0)
        def _():
            xv = x_ref[...]
            h = xv * lax.rsqrt(jnp.mean(xv * xv, -1, keepdims=True) + EPS) * pw_ref[...]
            h_ref[...] = h.astype(BF16)

        r = jnp.dot(h_ref[...], w_ref[...], preferred_element_type=F32)
        for k in range(gs):
            p_ref[k] = r[:, DH * k:DH * (k + 1)]

        def shifts():
            first = (i % per_seq) == 0
            tail = jnp.where(first, 0.0, carry[j])
            above = jnp.concatenate([tail, jnp.zeros((tm - 8, gs * DH), F32)], axis=0)
            row = lax.broadcasted_iota(jnp.int32, r.shape, 0)
            out = [jnp.where(row < s, pltpu.roll(above, tm - 8 + s, 0), pltpu.roll(r, s, 0)) for s in (1, 2, 3)]
            carry[j] = r[tm - 8:, :]
            return out

        def store(v):
            for k in range(gs):
                s_ref[k] = v[:, DH * k:DH * (k + 1)]

        @pl.when(functools.reduce(jnp.logical_or, [j == b for b in CONV_BLOCKS]))
        def _():
            s1, s2, s3 = shifts()
            store(cw_ref[3:4, :] * r + cw_ref[0:1, :] * s3 + cw_ref[1:2, :] * s2 + cw_ref[2:3, :] * s1)

        @pl.when(functools.reduce(jnp.logical_or, [j == b for b in MIX_BLOCKS]))
        def _():
            s1 = shifts()[0]
            store(r + mu_ref[...] * (s1 - r))

        @pl.when((j >= SC_BLOCK0) & (j < SC_BLOCK0 + NH))
        def _():
            first = (i % per_seq) == 0
            tail = jnp.where(first, 0.0, carry[j])
            u = r[:, DH:2 * DH] * r[:, 2 * DH:3 * DH]
            above = jnp.concatenate([tail[:, DH:2 * DH] * tail[:, 2 * DH:3 * DH], jnp.zeros((tm - 8, DH), F32)], axis=0)
            row = lax.broadcasted_iota(jnp.int32, u.shape, 0)
            u1, u2 = [jnp.where(row < s, pltpu.roll(above, tm - 8 + s, 0), pltpu.roll(u, s, 0)) for s in (1, 2)]
            conv = scw_ref[2:3, :] * u + scw_ref[1:2, :] * u1 + scw_ref[0:1, :] * u2
            ysc_ref[...] = (r[:, :DH] * conv * _silu(r[:, 3 * DH:])).astype(BF16)
            carry[j] = r[tm - 8:, :]

    n_shifted = len(SHIFTED_BLOCKS) * gs
    return _pcall(
        body, name=name, grid=(n // tm, nj),
        in_specs=[pl.BlockSpec((tm, D_MODEL), lambda i, j: (i, 0)),
                  pl.BlockSpec((1, D_MODEL), lambda i, j: (0, 0)),
                  pl.BlockSpec((None, D_MODEL, gs * DH), lambda i, j: (j, 0, 0)),
                  pl.BlockSpec((None, GDN_TAPS, gs * DH), lambda i, j: (j, 0, 0)),
                  pl.BlockSpec((None, 1, gs * DH), lambda i, j: (j, 0, 0)),
                  pl.BlockSpec((None, SC_TAPS, DH), lambda i, j: (jnp.clip(j - SC_BLOCK0, 0, NH - 1), 0, 0))],
        out_specs=[pl.BlockSpec((tm, D_MODEL), lambda i, j: (i, 0)),
                   pl.BlockSpec((gs, tm, DH), lambda i, j: (j, i, 0)),
                   pl.BlockSpec((gs, tm, DH), lambda i, j: (shifted_block(j), i, 0)),
                   pl.BlockSpec((None, tm, DH), lambda i, j: (jnp.clip(j - SC_BLOCK0, 0, NH - 1), i, 0))],
        out_shape=[jax.ShapeDtypeStruct((n, D_MODEL), BF16),
                   jax.ShapeDtypeStruct((N_GROUPS, n, DH), F32),
                   jax.ShapeDtypeStruct((n_shifted, n, DH), F32),
                   jax.ShapeDtypeStruct((NH, n, DH), BF16)],
        scratch_shapes=[pltpu.VMEM((nj, 8, gs * DH), F32)],
        compiler_params=_cparams(("arbitrary", "arbitrary")),
    )(x, pre_w, w_g, conv_w, mix_mu, sc_w)


def _out_proj_norm(ys, wout_g, x, post_w, name):
    n = x.shape[0]
    tm = _row_tile(n)

    def body(y0, y1, y2, y3, w_ref, x_ref, pw_ref, out_ref, xn_ref):
        y = jnp.concatenate([yr[h] for yr in (y0, y1, y2, y3) for h in range(NH)], axis=-1)
        acc = jnp.dot(y, w_ref[...], preferred_element_type=F32)
        out_ref[...] = acc
        xn_ref[...] = x_ref[...] + acc * lax.rsqrt(jnp.mean(acc * acc, -1, keepdims=True) + EPS) * pw_ref[...]

    yspec = pl.BlockSpec((NH, tm, DH), lambda i: (0, i, 0))
    rows = pl.BlockSpec((tm, D_MODEL), lambda i: (i, 0))
    return _pcall(
        body, name=name, grid=(n // tm,),
        in_specs=[yspec] * 4 + [pl.BlockSpec((D_MODEL, D_MODEL), lambda i: (0, 0)), rows,
                                pl.BlockSpec((1, D_MODEL), lambda i: (0, 0))],
        out_specs=[rows, rows],
        out_shape=[jax.ShapeDtypeStruct((n, D_MODEL), F32)] * 2,
        compiler_params=_cparams(("parallel",)),
    )(*ys, wout_g.reshape(D_MODEL, D_MODEL), x, post_w)


def _loss_grad(x, tgt, name):
    n = x.shape[0]
    tm = _row_tile(n)

    def body(x_ref, t_ref, dx_ref, l_ref):
        @pl.when(pl.program_id(0) == 0)
        def _():
            l_ref[...] = jnp.zeros_like(l_ref)

        e = x_ref[...] - t_ref[...]
        dx_ref[...] = e * (1.0 / D_MODEL)
        l_ref[...] += jnp.sum(jnp.sum(e * e, -1, keepdims=True), 0, keepdims=True) * (0.5 / D_MODEL)

    rows = pl.BlockSpec((tm, D_MODEL), lambda i: (i, 0))
    return _pcall(
        body, name=name, grid=(n // tm,),
        in_specs=[rows, rows],
        out_specs=[rows, pl.BlockSpec((1, 128), lambda i: (0, 0))],
        out_shape=[jax.ShapeDtypeStruct((n, D_MODEL), F32), jax.ShapeDtypeStruct((1, 128), F32)],
        compiler_params=_cparams(("arbitrary",)),
    )(x, tgt)


def _rmsnorm_bwd(xv, w, d):
    r = lax.rsqrt(jnp.mean(xv * xv, -1, keepdims=True) + EPS)
    xh = xv * r
    dxh = d * w
    dx = r * (dxh - xh * jnp.mean(dxh * xh, -1, keepdims=True))
    return dx, d * xh


def _post_bwd(dxn, out, post_w, wout_g, name):
    n = dxn.shape[0]
    tm = _row_tile(n)

    def body(d_ref, o_ref, pw_ref, w_ref, do_ref, dy_ref, dpw_ref):
        @pl.when(pl.program_id(0) == 0)
        def _():
            dpw_ref[...] = jnp.zeros_like(dpw_ref)

        dout, dw_rows = _rmsnorm_bwd(o_ref[...], pw_ref[...], d_ref[...])
        dpw_ref[...] += jnp.sum(dw_rows, 0, keepdims=True)
        db = dout.astype(BF16)
        do_ref[...] = db
        dy = lax.dot_general(db, w_ref[...], (((1,), (1,)), ((), ())), preferred_element_type=F32)
        for g in range(4 * NH):
            dy_ref[g] = dy[:, DH * g:DH * (g + 1)]

    rows = pl.BlockSpec((tm, D_MODEL), lambda i: (i, 0))
    vec = pl.BlockSpec((1, D_MODEL), lambda i: (0, 0))
    return _pcall(
        body, name=name, grid=(n // tm,),
        in_specs=[rows, rows, vec, pl.BlockSpec((D_MODEL, D_MODEL), lambda i: (0, 0))],
        out_specs=[rows, pl.BlockSpec((4 * NH, tm, DH), lambda i: (0, i, 0)), vec],
        out_shape=[jax.ShapeDtypeStruct((n, D_MODEL), BF16),
                   jax.ShapeDtypeStruct((4 * NH, n, DH), F32),
                   jax.ShapeDtypeStruct((1, D_MODEL), F32)],
        compiler_params=_cparams(("arbitrary",)),
    )(dxn, out, post_w, wout_g.reshape(D_MODEL, D_MODEL))


def _dwout(ys, dout, name):
    n = dout.shape[0]
    tm = _row_tile(n)

    def body(y0, y1, y2, y3, d_ref, dw_ref):
        @pl.when(pl.program_id(0) == 0)
        def _():
            dw_ref[...] = jnp.zeros_like(dw_ref)

        y = jnp.concatenate([yr[h] for yr in (y0, y1, y2, y3) for h in range(NH)], axis=-1)
        dw_ref[...] += lax.dot_general(y, d_ref[...], (((0,), (0,)), ((), ())), preferred_element_type=F32)

    yspec = pl.BlockSpec((NH, tm, DH), lambda i: (0, i, 0))
    return _pcall(
        body, name=name, grid=(n // tm,),
        in_specs=[yspec] * 4 + [pl.BlockSpec((tm, D_MODEL), lambda i: (i, 0))],
        out_specs=pl.BlockSpec((D_MODEL, D_MODEL), lambda i: (0, 0)),
        out_shape=jax.ShapeDtypeStruct((D_MODEL, D_MODEL), F32),
        compiler_params=_cparams(("arbitrary",)),
    )(*ys, dout)


def _source_specs(sources, rows_first):
    gs = GROUPS_PER_STEP
    spans, specs, j0 = [], [], 0
    for a in sources:
        nblk = a.shape[0] // gs
        spans.append((j0, j0 + nblk))
        shape = (gs, _row_tile(a.shape[1]), DH)

        def blk(j, j0=j0, nblk=nblk):
            return jnp.clip(j - j0, 0, nblk - 1)

        if rows_first:
            specs.append(pl.BlockSpec(shape, (lambda i, j, blk=blk: (blk(j), i, 0))))
        else:
            specs.append(pl.BlockSpec(shape, (lambda j, i, blk=blk: (blk(j), i, 0))))
        j0 += nblk
    return spans, specs


def _dh_prenorm_bwd(sources, w_g, x, pre_w, dxn, name, side=None):
    n = x.shape[0]
    tm = _row_tile(n)
    gs = GROUPS_PER_STEP
    nj = N_GROUPS // gs
    ni = n // tm
    spans, src_specs = _source_specs(sources, True)
    ns = len(sources)
    n_side = len(side.operands) if side else 0

    def body(*refs):
        src = refs[:ns]
        w_ref, x_ref, pw_ref, d_ref = refs[ns:ns + 4]
        side_in = refs[ns + 4:ns + 4 + n_side]
        dx_ref, dpw_ref = refs[ns + 4 + n_side:ns + 6 + n_side]
        side_out = refs[ns + 6 + n_side:ns + 6 + 2 * n_side]
        acc = refs[ns + 6 + 2 * n_side]
        sems = refs[ns + 7 + 2 * n_side:]
        i, j = pl.program_id(0), pl.program_id(1)

        if side is not None:
            @pl.when((i == 0) & (j == 0))
            def _():
                _comm_start(side.copies(side_in, side_out, sems))

        @pl.when((i == 0) & (j == 0))
        def _():
            dpw_ref[...] = jnp.zeros_like(dpw_ref)

        @pl.when(j == 0)
        def _():
            acc[...] = jnp.zeros_like(acc)

        for s_ref, (lo, hi) in zip(src, spans):
            @pl.when((j >= lo) & (j < hi))
            def _(s_ref=s_ref):
                four = jnp.concatenate([s_ref[k] for k in range(gs)], axis=-1)
                acc[...] += lax.dot_general(four, w_ref[...], (((1,), (1,)), ((), ())), preferred_element_type=F32)

        @pl.when(j == nj - 1)
        def _():
            dx, dw_rows = _rmsnorm_bwd(x_ref[...], pw_ref[...], acc[...])
            dx_ref[...] = d_ref[...] + dx
            dpw_ref[...] += jnp.sum(dw_rows, 0, keepdims=True)

        if side is not None:
            @pl.when((i == ni - 1) & (j == nj - 1))
            def _():
                _comm_wait(side.copies(side_in, side_out, sems))

    hbm = pl.BlockSpec(memory_space=pl.ANY)
    rows = pl.BlockSpec((tm, D_MODEL), lambda i, j: (i, 0))
    vec = pl.BlockSpec((1, D_MODEL), lambda i, j: (0, 0))
    dx, dpw, *side_res = _pcall(
        body, name=name, grid=(ni, nj),
        in_specs=src_specs + [pl.BlockSpec((None, D_MODEL, gs * DH), lambda i, j: (j, 0, 0)), rows, vec, rows]
        + [hbm] * n_side,
        out_specs=[rows, vec] + [hbm] * n_side,
        out_shape=[jax.ShapeDtypeStruct((n, D_MODEL), F32), jax.ShapeDtypeStruct((1, D_MODEL), F32)]
        + (side.out_shapes if side else []),
        scratch_shapes=[pltpu.VMEM((tm, D_MODEL), F32)] + (_comm_scratch(side) if side else []),
        compiler_params=_cparams(("arbitrary", "arbitrary")),
    )(*sources, w_g, x, pre_w, dxn, *(side.operands if side else []))
    return dx, dpw, side_res


def _dwin(hb, sources, name):
    n = hb.shape[0]
    tm = _row_tile(n)
    gs = GROUPS_PER_STEP
    ni, nj = n // tm, N_GROUPS // gs
    spans, src_specs = _source_specs(sources, True)
    ns = len(sources)

    def body(*refs):
        h_ref = refs[0]
        src = refs[1:1 + ns]
        out_ref, acc, sem = refs[1 + ns:]
        i, j = pl.program_id(0), pl.program_id(1)

        @pl.when((i == 0) & (j == 0))
        def _():
            acc[...] = jnp.zeros_like(acc)

        h = h_ref[...]
        for s_ref, (lo, hi) in zip(src, spans):
            @pl.when((j >= lo) & (j < hi))
            def _(s_ref=s_ref):
                four = jnp.concatenate([s_ref[k] for k in range(gs)], axis=-1)
                acc[j] += jnp.dot(h, four, preferred_element_type=F32)

        @pl.when((i == ni - 1) & (j == nj - 1))
        def _():
            done = pltpu.make_async_copy(acc, out_ref, sem)
            done.start()
            done.wait()

    return _pcall(
        body, name=name, grid=(ni, nj),
        in_specs=[pl.BlockSpec((D_MODEL, tm), lambda i, j: (0, i))] + src_specs,
        out_specs=pl.BlockSpec(memory_space=pl.ANY),
        out_shape=jax.ShapeDtypeStruct((nj, D_MODEL, gs * DH), F32),
        scratch_shapes=[pltpu.VMEM((nj, D_MODEL, gs * DH), F32), pltpu.SemaphoreType.DMA],
        compiler_params=_cparams(("arbitrary", "arbitrary")),
    )(jnp.transpose(hb), *sources)


def _adamw_math(w, g, m, v):
    c1 = 1.0 - ADAM_B1 ** ADAM_STEP
    c2 = 1.0 - ADAM_B2 ** ADAM_STEP
    nm = ADAM_B1 * m + (1.0 - ADAM_B1) * g
    nv = ADAM_B2 * v + (1.0 - ADAM_B2) * (g * g)
    return -ADAM_LR * ((nm / c1) / (jnp.sqrt(nv / c2) + ADAM_EPS) + ADAM_WD * w), nm, nv


def _adamw(w, g, m, v, name):
    r, c = w.shape
    tr = 256 if r % 256 == 0 else r

    def body(w_ref, g_ref, m_ref, v_ref, d_ref, nm_ref, nv_ref):
        d_ref[...], nm_ref[...], nv_ref[...] = _adamw_math(w_ref[...], g_ref[...], m_ref[...], v_ref[...])

    spec = pl.BlockSpec((tr, c), lambda i: (i, 0))
    return _pcall(
        body, name=name, grid=(r // tr,),
        in_specs=[spec] * 4, out_specs=[spec] * 3,
        out_shape=[jax.ShapeDtypeStruct((r, c), F32)] * 3,
        compiler_params=_cparams(("parallel",)),
    )(w, g, m, v)


def _sum_adamw(parts, w, m, v, name):
    r, c = w.shape
    tr = 128 if r % 128 == 0 else r

    def body(p_ref, w_ref, m_ref, v_ref, g_ref, d_ref, nm_ref, nv_ref):
        g = p_ref[0].astype(F32)
        for k in range(1, N_DEV):
            g = g + p_ref[k].astype(F32)
        g_ref[...] = g
        d_ref[...], nm_ref[...], nv_ref[...] = _adamw_math(w_ref[...], g, m_ref[...], v_ref[...])

    spec = pl.BlockSpec((tr, c), lambda i: (i, 0))
    return _pcall(
        body, name=name, grid=(r // tr,),
        in_specs=[pl.BlockSpec((N_DEV, tr, c), lambda i: (0, i, 0))] + [spec] * 3, out_specs=[spec] * 4,
        out_shape=[jax.ShapeDtypeStruct((r, c), F32)] * 4,
        compiler_params=_cparams(("parallel",)),
    )(parts, w, m, v)


def _me():
    return lax.axis_index("x"), lax.axis_index("y"), lax.axis_index("c")


def _flat(x, y, c):
    return 4 * x + 2 * y + c


def _peer(k):
    x, y, c = _me()
    return (x ^ ((k >> 2) & 1), y ^ ((k >> 1) & 1), c ^ (k & 1))


def _gather_plan(blocks):
    def copies(x_refs, out_refs, sems):
        send_sems, recv_sems, local_sems = sems
        me = _flat(*_me())
        local = [pltpu.make_async_copy(x, o.at[me], local_sems.at[a]) for a, (x, o) in enumerate(zip(x_refs, out_refs))]
        outgoing, incoming = [], []
        for k in range(1, N_DEV):
            src = _flat(*_peer(k))
            for a, (x, o) in enumerate(zip(x_refs, out_refs)):
                for slot, group in ((me, outgoing), (src, incoming)):
                    group.append(pltpu.make_async_remote_copy(
                        src_ref=x, dst_ref=o.at[slot], send_sem=send_sems.at[a, k - 1], recv_sem=recv_sems.at[a, k - 1],
                        device_id=_peer(k), device_id_type=MESH))
        return local, outgoing, incoming

    return _Comm(list(blocks), [jax.ShapeDtypeStruct((N_DEV,) + b.shape, b.dtype) for b in blocks], copies)


def _exchange_plan(sends):
    def copies(s_refs, out_refs, sems):
        send_sems, recv_sems, local_sems = sems
        me = _flat(*_me())
        local = [pltpu.make_async_copy(s.at[me], o.at[0], local_sems.at[i]) for i, (s, o) in enumerate(zip(s_refs, out_refs))]
        outgoing = []
        for k in range(1, N_DEV):
            to = _flat(*_peer(k))
            for i, (s, o) in enumerate(zip(s_refs, out_refs)):
                outgoing.append(pltpu.make_async_remote_copy(
                    src_ref=s.at[to], dst_ref=o.at[k], send_sem=send_sems.at[i, k - 1], recv_sem=recv_sems.at[i, k - 1],
                    device_id=_peer(k), device_id_type=MESH))
        return local, outgoing, outgoing

    return _Comm(list(sends), [jax.ShapeDtypeStruct(s.shape, s.dtype) for s in sends], copies)


def _comm_scratch(plan):
    n = len(plan.operands)
    return [pltpu.SemaphoreType.DMA((n, N_DEV - 1)), pltpu.SemaphoreType.DMA((n, N_DEV - 1)),
            pltpu.SemaphoreType.DMA((n,))]


def _comm_start(copies):
    local, outgoing, _ = copies
    for cp in local + outgoing:
        cp.start()


def _comm_wait(copies):
    local, outgoing, incoming = copies
    for cp in incoming:
        cp.wait_recv()
    for cp in outgoing:
        cp.wait_send()
    for cp in local:
        cp.wait()


def _run_comm(plan, name):
    n = len(plan.operands)

    def body(*refs):
        copies = plan.copies(refs[:n], refs[n:2 * n], refs[2 * n:])
        _comm_start(copies)
        _comm_wait(copies)

    return _pcall(
        body, name=name,
        in_specs=[pl.BlockSpec(memory_space=pl.ANY)] * n,
        out_specs=[pl.BlockSpec(memory_space=pl.ANY)] * n,
        out_shape=plan.out_shapes,
        scratch_shapes=_comm_scratch(plan),
    )(*plan.operands)


def _all_gather_two_level(blocks, name):
    na = len(blocks)

    def body(*refs):
        x_refs, out_refs = refs[:na], refs[na:2 * na]
        send_sems, recv_sems, local_sems = refs[2 * na:]
        x, y, c = _me()
        me, sibling = (x, y, c), (x, y, 1 - c)
        chips = [(1 - x, y), (x, 1 - y), (1 - x, 1 - y)]

        def copy(a, k, block, to, own=False):
            slot = out_refs[a].at[_flat(*block)]
            return pltpu.make_async_remote_copy(
                src_ref=x_refs[a] if own else slot, dst_ref=slot, send_sem=send_sems.at[a, k],
                recv_sem=recv_sems.at[a, k], device_id=to, device_id_type=MESH)

        mine = [pltpu.make_async_copy(x_refs[a], out_refs[a].at[_flat(*me)], local_sems.at[a]) for a in range(na)]
        first = [copy(a, 0, me, sibling, own=True) for a in range(na)]
        first += [copy(a, 1 + j, me, (*chip, c), own=True) for j, chip in enumerate(chips) for a in range(na)]
        for cp in mine + first:
            cp.start()
        passed = []
        for j, chip in enumerate(chips):
            for a in range(na):
                copy(a, 1 + j, (*chip, c), me).wait_recv()
                cp = copy(a, 4 + j, (*chip, c), sibling)
                cp.start()
                passed.append(cp)
        for a in range(na):
            copy(a, 0, sibling, me).wait_recv()
        for j, chip in enumerate(chips):
            for a in range(na):
                copy(a, 4 + j, (*chip, 1 - c), me).wait_recv()
        for cp in first + passed:
            cp.wait_send()
        for cp in mine:
            cp.wait()

    return _pcall(
        body, name=name,
        in_specs=[pl.BlockSpec(memory_space=pl.ANY)] * na,
        out_specs=[pl.BlockSpec(memory_space=pl.ANY)] * na,
        out_shape=[jax.ShapeDtypeStruct((N_DEV,) + b.shape, b.dtype) for b in blocks],
        scratch_shapes=[pltpu.SemaphoreType.DMA((na, N_DEV - 1)), pltpu.SemaphoreType.DMA((na, N_DEV - 1)),
                        pltpu.SemaphoreType.DMA((na,))],
    )(*blocks)


def _sum_slots(a, name):
    r = a.shape[1]

    def body(a_ref, o_ref):
        acc = a_ref[0]
        for d in range(1, N_DEV):
            acc = acc + a_ref[d]
        o_ref[...] = acc

    return _pcall(body, name=name, out_shape=jax.ShapeDtypeStruct((r, 128), F32), compiler_params=_cparams())(a)


def _all_reduce_small(blk, name):
    r = blk.shape[0]

    def body(x_ref, out_ref, gath, send_sems, recv_sems):
        me = _flat(*_me())
        gath[me] = x_ref[...]
        copies = []
        for k in range(1, N_DEV):
            cp = pltpu.make_async_remote_copy(
                src_ref=x_ref, dst_ref=gath.at[me],
                send_sem=send_sems.at[k - 1], recv_sem=recv_sems.at[k - 1],
                device_id=_peer(k), device_id_type=MESH)
            cp.start()
            copies.append(cp)
        for k in range(1, N_DEV):
            src = _flat(*_peer(k))
            pltpu.make_async_remote_copy(
                src_ref=x_ref, dst_ref=gath.at[src],
                send_sem=send_sems.at[k - 1], recv_sem=recv_sems.at[k - 1],
                device_id=_peer(k), device_id_type=MESH).wait_recv()
        for cp in copies:
            cp.wait_send()
        acc = gath[0]
        for d in range(1, N_DEV):
            acc = acc + gath[d]
        out_ref[...] = acc

    return _pcall(
        body, name=name,
        in_specs=[pl.BlockSpec(memory_space=pltpu.VMEM)],
        out_specs=pl.BlockSpec(memory_space=pltpu.VMEM),
        out_shape=jax.ShapeDtypeStruct((r, 128), F32),
        scratch_shapes=[pltpu.VMEM((N_DEV, r, 128), F32),
                        pltpu.SemaphoreType.DMA((N_DEV - 1,)), pltpu.SemaphoreType.DMA((N_DEV - 1,))],
    )(blk)


def _heads(vec):
    return vec.reshape(NH, 1, DH)


def _rep(vec4):
    return jnp.broadcast_to(vec4.reshape(NH, 1, 1), (NH, 1, DH))


def _onehot_lane(offset):
    m = np.zeros((NH, 1, DH), np.float32)
    for h in range(NH):
        m[h, 0, offset + h] = 1.0
    return jnp.asarray(m)


_TINY = (("gdn_conv_w", (DEPTH, 4, 96)), ("rwkv_w_up", (DEPTH, 64, 32)), ("rwkv_a_up", (DEPTH, 64, 32)),
         ("sc_conv_w", (DEPTH, 3, 32)))
_TINY_ROWS = -(-sum(int(np.prod(s)) for _, s in _TINY) // 1024) * 8


def _pack_rows(arrays, rows, fill=0.0):
    flat = jnp.concatenate([a.reshape(-1) for a in arrays])
    return jnp.pad(flat, (0, rows * 128 - flat.shape[0]), constant_values=fill).reshape(rows, 128)


def _unpack_rows(p, named_shapes):
    lead = p.shape[:-2]
    flat = p.reshape(lead + (-1,))
    out, o = {}, 0
    for n, s in named_shapes:
        size = int(np.prod(s))
        out[n] = flat[..., o:o + size].reshape(lead + tuple(s))
        o += size
    return out


def _gather_last(a):
    return jnp.transpose(a, (1, 0, 2)).reshape(a.shape[1], -1)


def _split_last(a):
    r, c8 = a.shape
    return jnp.transpose(a.reshape(r, N_DEV, c8 // N_DEV), (1, 0, 2))


_SMALL = (("pre_norm_w", (DEPTH, 1024)), ("gdn_a_log", (DEPTH, 4)), ("gdn_dt_bias", (DEPTH, 4)),
          ("gdn_norm_w", (DEPTH, 64)), ("rwkv_mu", (DEPTH, 1152)), ("rwkv_w0", (DEPTH, 256)),
          ("rwkv_a0", (DEPTH, 256)), ("rwkv_k_k", (DEPTH, 256)), ("rwkv_k_a", (DEPTH, 256)),
          ("rwkv_r_k", (DEPTH, 256)), ("rwkv_ln_w", (DEPTH, 256)), ("rwkv_ln_b", (DEPTH, 256)),
          ("gla_a_up", (DEPTH, 16, 128)), ("gla_a_bias", (DEPTH, 128)), ("gla_norm_w", (DEPTH, 64)),
          ("post_norm_w", (DEPTH, 1024)), ("loss", ()))
_SMALL_ROWS = -(-sum(int(np.prod(s)) for _, s in _SMALL) // 1024) * 8


def _big_weights(w_in_all, w_out_all, l):
    return dict(w_g=_regroup_in(w_in_all, f"regroup_in{l}"),
                wout_g=w_out_all.reshape(4 * NH, DH, D_MODEL).astype(BF16))


def _layer_params(wts, tiny, l):
    conv = _gather_last(tiny["gdn_conv_w"][:, l])
    q = {}
    q["gdn_conv"] = jnp.transpose(conv.reshape(GDN_TAPS, 12, DH), (1, 0, 2))
    q["gdn_prm"] = [_rep(wts["gdn_a_log"][l]), _rep(wts["gdn_dt_bias"][l]),
                    jnp.broadcast_to(wts["gdn_norm_w"][l].reshape(1, 1, DH), (NH, 1, DH))]
    q["gdn_cst"] = [_onehot_lane(0), _onehot_lane(NH)]
    q["rwkv_mu"] = wts["rwkv_mu"][l].reshape(18, 1, DH)
    gs, nj = GROUPS_PER_STEP, N_GROUPS // GROUPS_PER_STEP
    side_by_side = lambda a: jnp.transpose(a.reshape(-1, gs, a.shape[1], DH), (0, 2, 1, 3)).reshape(-1, a.shape[1], gs * DH)
    q["conv_blocks"] = jnp.pad(side_by_side(q["gdn_conv"]), ((0, nj - len(CONV_BLOCKS)), (0, 0), (0, 0)))
    singles = jnp.pad(q["rwkv_mu"][16:18].reshape(1, 1, 2 * DH), ((0, 0), (0, 0), (DH, DH)))
    q["mix_blocks"] = jnp.concatenate([jnp.zeros((4, 1, gs * DH), F32), side_by_side(q["rwkv_mu"][:16]),
                                       jnp.zeros((8, 1, gs * DH), F32), singles], axis=0)
    w_up = jnp.transpose(_gather_last(tiny["rwkv_w_up"][:, l]).reshape(64, NH, DH), (1, 0, 2))
    a_up = jnp.transpose(_gather_last(tiny["rwkv_a_up"][:, l]).reshape(64, NH, DH), (1, 0, 2))
    q["rwkv_prm"] = [_heads(wts["rwkv_w0"][l]), w_up, _heads(wts["rwkv_a0"][l]), a_up,
                     _heads(wts["rwkv_k_k"][l]), _heads(wts["rwkv_k_a"][l]), _heads(wts["rwkv_r_k"][l]),
                     _heads(wts["rwkv_ln_w"][l]), _heads(wts["rwkv_ln_b"][l])]
    sc = _gather_last(tiny["sc_conv_w"][:, l])
    q["sc_conv"] = jnp.transpose(sc.reshape(SC_TAPS, NH, DH), (1, 0, 2))
    gla_up = jnp.transpose(wts["gla_a_up"][l].reshape(16, NH, GLA_HEAD_K), (1, 0, 2))
    gla_up = jnp.pad(gla_up, ((0, 0), (0, DH - 16), (0, DH - GLA_HEAD_K)))
    gla_b = jnp.pad(wts["gla_a_bias"][l].reshape(NH, 1, GLA_HEAD_K), ((0, 0), (0, 0), (0, DH - GLA_HEAD_K)))
    q["gla_prm"] = [gla_up, gla_b, jnp.broadcast_to(wts["gla_norm_w"][l].reshape(1, 1, DH), (NH, 1, DH))]
    q["pre_w"] = wts["pre_norm_w"][l].reshape(1, D_MODEL)
    q["post_w"] = wts["post_norm_w"][l].reshape(1, D_MODEL)
    return q


def _mixer_inputs(p, ps):
    gdn = [(ps, 4, 0), (ps, 4, 1), (ps, 4, 2), (p, 4, G_GDN // 4 + 3), (p, 1, G_GDN_AB)]
    rwkv = [(ps, 4, G_PM // 4 + k) for k in range(4)] + [(ps, 1, G_PM_WD), (ps, 1, G_PM_AD)]
    gla = [(p, 4, G_GLA // 4 + k) for k in range(4)] + [(p, 1, G_GLA_AD)]
    return gdn, rwkv, gla


def _layer_fwd(x, q, nb, t, l, side=None):
    hb, p, ps, y_sc = _norm_proj(x, q["pre_w"], q["w_g"], q["conv_blocks"], q["mix_blocks"], q["sc_conv"], t,
                                 f"norm_proj{l}")
    gdn_in, rwkv_in, gla_in = _mixer_inputs(p, ps)
    y_gdn, ck_gdn, _ = _mixer_fwd(_gdn_chunk, f"gdn_fwd{l}", gdn_in, q["gdn_prm"], q["gdn_cst"], nb, t, n_kept=1)
    y_rwkv, ck_rwkv, side_res = _mixer_fwd(_rwkv_chunk, f"rwkv_fwd{l}", rwkv_in, q["rwkv_prm"], [], nb, t,
                                           first_fn=_rwkv_first_chunk, side=side, n_kept=1)
    y_gla, ck_gla, _ = _mixer_fwd(_gla_chunk, f"gla_fwd{l}", gla_in, q["gla_prm"], [], nb, t)
    ys = (y_gdn, y_rwkv, y_sc, y_gla)
    out, xn = _out_proj_norm(ys, q["wout_g"], x, q["post_w"], f"out_proj{l}")
    saved = dict(x=x, hb=hb, p=p, ps=ps, ys=ys, out=out, ck=(ck_gdn, ck_rwkv, ck_gla))
    return xn, saved, side_res


def _layer_bwd(dxn, q, sv, nb, t, l, side=None, exchange_own=False):
    p, ys = sv["p"], sv["ys"]
    dout, dy, d_post = _post_bwd(dxn, sv["out"], q["post_w"], q["wout_g"], f"post_bwd{l}")
    d_wout = _dwout(ys, dout, f"dwout{l}").reshape(N_DEV, 128, D_MODEL).astype(BF16)
    gdn_in, rwkv_in, gla_in = _mixer_inputs(p, sv["ps"])
    ck_gdn, ck_rwkv, ck_gla = sv["ck"]
    g = {}

    (d_conv, dz, dab), (da_log, ddt, dnw), _ = _mixer_bwd(
        _gdn_chunk, f"gdn_bwd{l}", gdn_in, q["gdn_prm"], q["gdn_cst"], ck_gdn, dy, 0,
        [(12, F32), (4, BF16), (1, BF16)], [(0, 0), (0, 4), (0, 8), (1, 0), (2, 0)], nb, t)
    dconv_in, d_gconv = _conv_bwd(p, G_GDN, 12, q["gdn_conv"], d_conv, nb, t, f"gdn_conv_bwd{l}")
    g["gdn_conv_w"] = jnp.transpose(d_gconv.sum(1), (1, 0, 2)).reshape(GDN_TAPS, 768)
    g["gdn_a_log"] = da_log.sum((0, 2, 3))
    g["gdn_dt_bias"] = ddt.sum((0, 2, 3))
    g["gdn_norm_w"] = dnw.sum((0, 1, 2))

    (d_pm,), d_rprm, side_res = _mixer_bwd(
        _rwkv_chunk, f"rwkv_bwd{l}", rwkv_in, q["rwkv_prm"], [], ck_rwkv, dy, 1,
        [(18, F32)], [(0, 0), (0, 4), (0, 8), (0, 12), (0, 16), (0, 17)], nb, t, first_fn=_rwkv_first_chunk,
        side=side)
    dp_rwkv, d_mu = _mix_bwd(p, q["rwkv_mu"], d_pm, nb, t, f"rwkv_mix_bwd{l}")
    g["rwkv_mu"] = d_mu.sum(1).reshape(1152)
    rp = [a.sum(0) for a in d_rprm]
    g["rwkv_w0"] = rp[0].reshape(256)
    g["rwkv_w_up"] = jnp.transpose(rp[1], (1, 0, 2)).reshape(64, 256)
    g["rwkv_a0"] = rp[2].reshape(256)
    g["rwkv_a_up"] = jnp.transpose(rp[3], (1, 0, 2)).reshape(64, 256)
    for i, nme in enumerate(("rwkv_k_k", "rwkv_k_a", "rwkv_r_k", "rwkv_ln_w", "rwkv_ln_b")):
        g[nme] = rp[4 + i].reshape(256)

    dp_sc, d_scw = _sc_bwd(p, q["sc_conv"], dy, nb, t, f"sc_bwd{l}")
    g["sc_conv_w"] = jnp.transpose(d_scw.sum(1), (1, 0, 2)).reshape(SC_TAPS, 256)

    (dp_gla, dad), (d_aup, d_ab, d_gnw), _ = _mixer_bwd(
        _gla_chunk, f"gla_bwd{l}", gla_in, q["gla_prm"], [], ck_gla, dy, 3,
        [(16, BF16), (1, BF16)], [(0, 0), (0, 4), (0, 8), (0, 12), (1, 0)], nb, t)
    g["gla_a_up"] = jnp.transpose(d_aup.sum(0)[:, :16, :GLA_HEAD_K], (1, 0, 2)).reshape(16, 128)
    g["gla_a_bias"] = d_ab.sum(0)[:, 0, :GLA_HEAD_K].reshape(128)
    g["gla_norm_w"] = d_gnw.sum((0, 1, 2))

    singles = jnp.concatenate([dab, dp_rwkv[16:18], dad], axis=0)
    sources = [dconv_in, dz, dp_rwkv, dp_sc, dp_gla, singles]
    d_win = _regroup_out(_dwin(sv["hb"], sources, f"dwin{l}"), f"regroup_out{l}")
    own = _exchange_plan([d_win, d_wout]) if exchange_own else None
    dx, d_pre, got = _dh_prenorm_bwd(sources, q["w_g"], sv["x"], q["pre_w"], dxn, f"dh_bwd{l}", own)
    if exchange_own:
        d_win, d_wout = got
    g["pre_norm_w"] = d_pre.reshape(D_MODEL)
    g["post_norm_w"] = d_post.reshape(D_MODEL)
    return dx, g, d_win, d_wout, side_res


def _local_step(x, tgt, wts, tiny, w_in_all, w_out_all, later_shards=None):
    nb, t, d = x.shape
    xf = x.reshape(nb * t, d)
    overlap = later_shards is not None
    qs, saved = [], []
    big = _big_weights(w_in_all[0], w_out_all[0], 0)
    for l in range(DEPTH):
        q = dict(_layer_params(wts, tiny, l), **big)
        nxt = l + 1 < DEPTH
        side = _gather_plan(later_shards[l]) if overlap and nxt else None
        xf, sv, got = _layer_fwd(xf, q, nb, t, l, side)
        if nxt:
            big = _big_weights(*(got if overlap else (w_in_all[l + 1], w_out_all[l + 1])), l + 1)
        qs.append(q)
        saved.append(sv)
    dxf, lpart = _loss_grad(xf, tgt.reshape(nb * t, d), "loss")
    grads, d_win, d_wout = [None] * DEPTH, [None] * DEPTH, [None] * DEPTH
    for l in reversed(range(DEPTH)):
        side = _exchange_plan([d_win[l + 1], d_wout[l + 1]]) if overlap and l + 1 < DEPTH else None
        dxf, grads[l], d_win[l], d_wout[l], got = _layer_bwd(dxf, qs[l], saved[l], nb, t, l, side,
                                                             exchange_own=overlap and l == 0)
        if side is not None:
            d_win[l + 1], d_wout[l + 1] = got
    small = {k: jnp.stack([grads[l][k] for l in range(DEPTH)]) for k in grads[0]}
    return lpart[0, 0], dxf.reshape(nb, t, d), small, d_win, d_wout


_WEIGHTS = ("pre_norm_w", "w_in", "gdn_conv_w", "gdn_a_log", "gdn_dt_bias", "gdn_norm_w", "rwkv_mu", "rwkv_w0",
            "rwkv_w_up", "rwkv_a0", "rwkv_a_up", "rwkv_k_k", "rwkv_k_a", "rwkv_r_k", "rwkv_ln_w", "rwkv_ln_b",
            "sc_conv_w", "gla_a_up", "gla_a_bias", "gla_norm_w", "w_out", "post_norm_w")


def kernel(x, pre_norm_w, w_in, gdn_conv_w, gdn_a_log, gdn_dt_bias, gdn_norm_w, rwkv_mu, rwkv_w0, rwkv_w_up, rwkv_a0, rwkv_a_up, rwkv_k_k, rwkv_k_a, rwkv_r_k, rwkv_ln_w, rwkv_ln_b, sc_conv_w, gla_a_up, gla_a_bias, gla_norm_w, w_out, post_norm_w, loss_target, m_pre_norm_w, m_w_in, m_gdn_conv_w, m_gdn_a_log, m_gdn_dt_bias, m_gdn_norm_w, m_rwkv_mu, m_rwkv_w0, m_rwkv_w_up, m_rwkv_a0, m_rwkv_a_up, m_rwkv_k_k, m_rwkv_k_a, m_rwkv_r_k, m_rwkv_ln_w, m_rwkv_ln_b, m_sc_conv_w, m_gla_a_up, m_gla_a_bias, m_gla_norm_w, m_w_out, m_post_norm_w, v_pre_norm_w, v_w_in, v_gdn_conv_w, v_gdn_a_log, v_gdn_dt_bias, v_gdn_norm_w, v_rwkv_mu, v_rwkv_w0, v_rwkv_w_up, v_rwkv_a0, v_rwkv_a_up, v_rwkv_k_k, v_rwkv_k_a, v_rwkv_r_k, v_rwkv_ln_w, v_rwkv_ln_b, v_sc_conv_w, v_gla_a_up, v_gla_a_bias, v_gla_norm_w, v_w_out, v_post_norm_w):
    env = dict(locals())
    w = {n: env[n] for n in _WEIGHTS}
    m = {n: env["m_" + n] for n in _WEIGHTS}
    v = {n: env["v_" + n] for n in _WEIGHTS}
    tiny_names = [n for n, _ in _TINY]

    w_in_b, w_out_b = w_in.astype(BF16), w_out.astype(BF16)
    w_in_0, w_out_0, tiny_all = _all_gather_two_level(
        [w_in_b[0], w_out_b[0], _pack_rows([w[n] for n in tiny_names], _TINY_ROWS)], "gather_weights")
    tiny = _unpack_rows(tiny_all, _TINY)

    lpart, grad_x, small, r_win, r_wout = _local_step(
        x, loss_target, w, tiny, [w_in_0], [w_out_0], later_shards=[(w_in_b[l], w_out_b[l]) for l in range(1, DEPTH)])

    tiny_send = jnp.stack([_pack_rows([_split_last(small[n][l])[d] for n in tiny_names for l in range(DEPTH)],
                                      _TINY_ROWS) for d in range(N_DEV)])
    (r_tiny,) = _run_comm(_exchange_plan([tiny_send]), "scatter_grads")
    grads, delta, new_m, new_v = {}, {}, {}, {}
    for n, parts in (("w_in", r_win), ("w_out", r_wout)):
        res = [_sum_adamw(parts[l], w[n][l], m[n][l], v[n][l], f"adamw_{n}{l}") for l in range(DEPTH)]
        grads[n], delta[n], new_m[n], new_v[n] = [jnp.stack(o) for o in zip(*res)]
    tiny_sum = _sum_slots(r_tiny, "sum_tiny").reshape(-1)
    o = 0
    for n, s in _TINY:
        size = int(np.prod(s))
        grads[n] = tiny_sum[o:o + size].reshape(s)
        o += size

    small = dict(small)
    small["loss"] = lpart
    red = _unpack_rows(_all_reduce_small(_pack_rows([small[n] for n, _ in _SMALL], _SMALL_ROWS), "reduce_small"),
                       _SMALL)
    loss = red.pop("loss")
    grads.update(red)

    rest = [n for n in _WEIGHTS if n not in ("w_in", "w_out")]
    rest_shapes = [(n, w[n].shape) for n in rest]
    rows = -(-sum(int(np.prod(s)) for _, s in rest_shapes) // 1024) * 8
    outs = _adamw(_pack_rows([w[n] for n in rest], rows), _pack_rows([grads[n] for n in rest], rows),
                  _pack_rows([m[n] for n in rest], rows), _pack_rows([v[n] for n in rest], rows, 1.0), "adamw_rest")
    for dst, packed in zip((delta, new_m, new_v), outs):
        dst.update(_unpack_rows(packed, rest_shapes))

    return (loss, grad_x, *[grads[n] for n in _WEIGHTS], *[delta[n] for n in _WEIGHTS],
            *[new_m[n] for n in _WEIGHTS], *[new_v[n] for n in _WEIGHTS])
```

```python
import collections
import functools
import math

import numpy as np
import jax
import jax.numpy as jnp
from jax import lax
from jax.experimental import pallas as pl
from jax.experimental.pallas import tpu as pltpu

F32 = jnp.float32
BF16 = jnp.bfloat16

D_MODEL = 1024
DEPTH = 2
NH = 4
DH = 64
CH = 64
EPS = 1e-6
RWKV_GN_EPS = 64e-5
GLA_HEAD_K = 32
GLA_TAU = 16.0
GDN_TAPS = 4
SC_TAPS = 3
D_IN = 3992
N_DEV = 8
SHARD_COLS = D_IN // N_DEV

G_GDN = 0
G_RWKV = 16
G_SC = 32
G_GLA = 48
G_GDN_AB, G_RWKV_WD, G_RWKV_AD, G_GLA_AD = 64, 65, 66, 67
N_GROUPS = 68
GROUPS_PER_STEP = 4
TIME_BLOCK = 256
RWKV_EXACT_STEPS = 16

C_GDN, C_RWKV, C_SC, C_GLA = 0, 1032, 2184, 3208

ADAM_LR, ADAM_B1, ADAM_B2, ADAM_EPS, ADAM_WD, ADAM_STEP = 0.001, 0.9, 0.999, 1e-08, 0.01, 10

VMEM_LIMIT = 56 * 1024 * 1024
MESH = pl.DeviceIdType.MESH

_pcall = pl.pallas_call

_Comm = collections.namedtuple("_Comm", "operands out_shapes copies")


def _cparams(sem=None):
    if sem is None:
        return pltpu.CompilerParams(vmem_limit_bytes=VMEM_LIMIT)
    return pltpu.CompilerParams(dimension_semantics=sem, vmem_limit_bytes=VMEM_LIMIT)


def _group_segments():
    table = [(G_GDN + i, C_GDN + DH * i, DH) for i in range(16)]
    table.append((G_GDN_AB, C_GDN + 1024, 8))
    table += [(G_RWKV + i, C_RWKV + DH * i, DH) for i in range(16)]
    table += [(G_RWKV_WD, C_RWKV + 1024, DH), (G_RWKV_AD, C_RWKV + 1088, DH)]
    table += [(G_SC + 4 * j + k, C_SC + 256 * k + DH * j, DH) for j in range(NH) for k in range(4)]
    for h in range(NH):
        table += [(G_GLA + h, C_GLA + GLA_HEAD_K * h, GLA_HEAD_K),
                  (G_GLA + 4 + h, C_GLA + 128 + GLA_HEAD_K * h, GLA_HEAD_K),
                  (G_GLA + 8 + h, C_GLA + 256 + DH * h, DH),
                  (G_GLA + 12 + h, C_GLA + 512 + DH * h, DH)]
    table.append((G_GLA_AD, C_GLA + 768, 16))
    segs, padded = [], []
    for g, c, n in table:
        if n < DH:
            padded.append(g)
        a = 0
        while n > 0:
            d, off = divmod(c, SHARD_COLS)
            ln = min(n, SHARD_COLS - off)
            segs.append((g, a, d, off, ln))
            c, a, n = c + ln, a + ln, n - ln
    return segs, padded


_SEGMENTS, _PADDED_GROUPS = _group_segments()


def _dn(ta, tb):
    return (((1 if ta else 2,), (2 if tb else 1,)), ((0,), (0,)))


def _hdot(a, b, ta=False, tb=False):
    return lax.dot_general(a, b, _dn(ta, tb), precision=lax.Precision.HIGH, preferred_element_type=F32)


def _r(x):
    return x.astype(BF16)


def _rdot(a, b, ta=False, tb=False):
    return lax.dot_general(_r(a), _r(b), _dn(ta, tb), preferred_element_type=F32)


@jax.custom_vjp
def _bmm(a, b):
    return _rdot(a, b)


def _bmm_fwd(a, b):
    return _rdot(a, b), (a, b)


def _bmm_bwd(res, g):
    a, b = res
    return _rdot(g, b, tb=True), _rdot(a, g, ta=True)


_bmm.defvjp(_bmm_fwd, _bmm_bwd)


@jax.custom_vjp
def _bmm_nt(a, b):
    return _rdot(a, b, tb=True)


def _bmm_nt_fwd(a, b):
    return _rdot(a, b, tb=True), (a, b)


def _bmm_nt_bwd(res, g):
    a, b = res
    return _rdot(g, b), _rdot(g, a, ta=True)


_bmm_nt.defvjp(_bmm_nt_fwd, _bmm_nt_bwd)


@jax.custom_vjp
def _bmm_tn(a, b):
    return _rdot(a, b, ta=True)


def _bmm_tn_fwd(a, b):
    return _rdot(a, b, ta=True), (a, b)


def _bmm_tn_bwd(res, g):
    a, b = res
    return _rdot(b, g, tb=True), _rdot(a, g)


_bmm_tn.defvjp(_bmm_tn_fwd, _bmm_tn_bwd)


def _tri(n):
    i = lax.broadcasted_iota(jnp.int32, (n, n), 0)
    j = lax.broadcasted_iota(jnp.int32, (n, n), 1)
    return i >= j, i > j, i == j


def _heads_of(x, like):
    n = like.shape[0]
    if x.ndim == 2:
        return jnp.broadcast_to(x[None], (n,) + x.shape)
    seqs = x.shape[0]
    return jnp.broadcast_to(x[:, None], (seqs, n // seqs) + x.shape[1:]).reshape((n,) + x.shape[1:])


def _cumsum_rows(x):
    incl, _, _ = _tri(x.shape[-2])
    return _hdot(_heads_of(incl.astype(F32), x), x)


@jax.custom_vjp
def _inv_unit_lower(a):
    n = a.shape[-1]
    _, _, eye = _tri(n)
    pw = -a
    inv = eye.astype(F32) + pw
    for _ in range(math.ceil(math.log2(n)) - 1):
        pw = _hdot(pw, pw)
        inv = inv + _hdot(inv, pw)
    return inv


def _inv_unit_lower_fwd(a):
    inv = _inv_unit_lower(a)
    return inv, inv


def _inv_unit_lower_bwd(inv, g):
    return (-_hdot(_hdot(inv, g, ta=True), inv, tb=True),)


_inv_unit_lower.defvjp(_inv_unit_lower_fwd, _inv_unit_lower_bwd)


@jax.custom_vjp
def _inv_reuse(a, inv):
    return inv


def _inv_reuse_fwd(a, inv):
    return inv, inv


def _inv_reuse_bwd(inv, g):
    return _inv_unit_lower_bwd(inv, g)[0], jnp.zeros_like(inv)


_inv_reuse.defvjp(_inv_reuse_fwd, _inv_reuse_bwd)


def _silu(x):
    return x * jax.nn.sigmoid(x)


def _t(x):
    return jnp.swapaxes(x, -1, -2)


def _gdn_chunk(prm, cst, ins, s, tinv=None):
    a_log, dt_b, nw = prm
    m_a, m_b = cst
    cq, ck, cv, z, ab = ins
    ab = _heads_of(ab, m_a)
    incl, strict, _ = _tri(CH)
    q = _silu(cq)
    k = _silu(ck)
    v = _silu(cv)
    q = q * lax.rsqrt(jnp.sum(q * q, -1, keepdims=True) + EPS) * (DH ** -0.5)
    k = k * lax.rsqrt(jnp.sum(k * k, -1, keepdims=True) + EPS)
    a_raw = jnp.sum(ab * m_a, -1, keepdims=True)
    b_raw = jnp.sum(ab * m_b, -1, keepdims=True)
    gstep = -jnp.exp(a_log) * jax.nn.softplus(a_raw + dt_b)
    beta = jax.nn.sigmoid(b_raw)
    gc = _cumsum_rows(gstep)
    gl = jnp.sum(gstep, -2, keepdims=True)
    dec = jnp.where(incl, jnp.exp(jnp.where(incl, gc - _t(gc), 0.0)), 0.0)
    kb = k * beta
    a_mat = jnp.where(strict, _bmm_nt(kb, k) * dec, 0.0)
    tinv = _inv_unit_lower(a_mat) if tinv is None else _inv_reuse(a_mat, tinv)
    eg = jnp.exp(gc)
    u = _hdot(tinv, v * beta)
    w = _hdot(tinv, kb * eg)
    attn = _bmm_nt(q, k) * dec
    v_new = u - _bmm(w, s)
    o = _bmm(q * eg, s) + _bmm(attn, v_new)
    s_next = s * jnp.exp(gl) + _bmm_tn(k * jnp.exp(gl - gc), v_new)
    on = o * lax.rsqrt(jnp.mean(o * o, -1, keepdims=True) + EPS) * nw
    return on * _silu(z), s_next, tinv


def _gla_chunk(prm, cst, ins, st):
    a_up, a_bias, nw = prm
    q, k, v, z, ad = ins
    incl, _, _ = _tri(CH)
    la = jax.nn.log_sigmoid(_bmm(_heads_of(ad, a_up), a_up) + a_bias) * (1.0 / GLA_TAU)
    bc = _cumsum_rows(la)
    bl = jnp.sum(la, -2, keepdims=True)
    qe = q * (GLA_HEAD_K ** -0.5) * jnp.exp(bc)
    ke = k * jnp.exp(-bc)
    attn = jnp.where(incl, _bmm_nt(qe, ke), 0.0)
    o = _bmm_nt(qe, st) + _bmm(attn, v)
    st_next = st * jnp.exp(bl) + _bmm_tn(v, k * jnp.exp(bl - bc))
    on = o * lax.rsqrt(jnp.mean(o * o, -1, keepdims=True) + EPS) * nw
    return on * _silu(z), st_next


def _rwkv_chunk(prm, cst, ins, s, inv=None):
    r, v = ins[0], ins[2]
    incl, strict, _ = _tri(r.shape[-2])
    lw, kk, k2, m = _rwkv_pre(prm, ins)
    cum = _cumsum_rows(lw)
    ltot = jnp.sum(lw, -2, keepdims=True)
    n_t = -kk * jnp.exp(cum - lw)
    einv = jnp.exp(-cum)
    m_t = m * einv
    k_t = k2 * einv
    r_t = r * jnp.exp(cum)
    a_nm = jnp.where(strict, _hdot(n_t, m_t, tb=True), 0.0)
    a_nk = jnp.where(strict, _hdot(n_t, k_t, tb=True), 0.0)
    inv = _inv_unit_lower(-a_nm) if inv is None else _inv_reuse(-a_nm, inv)
    cm = _hdot(inv, _hdot(n_t, s, tb=True) + _bmm(a_nk, v))
    y = (_bmm_nt(r_t, s) + _bmm(jnp.where(incl, _hdot(r_t, m_t, tb=True), 0.0), cm)
         + _bmm(jnp.where(incl, _hdot(r_t, k_t, tb=True), 0.0), v))
    eend = jnp.exp(ltot - cum)
    s_next = s * jnp.exp(ltot) + _bmm_tn(cm, m * eend) + _bmm_tn(v, k2 * eend)
    return _rwkv_post(prm, ins, y, k2), s_next, inv


def _rwkv_pre(prm, ins):
    w0, w_up, a0, a_up, k_k, k_a = prm[:6]
    k, wd, ad = ins[1], ins[4], ins[5]
    lw = -math.exp(-0.5) * jax.nn.sigmoid(w0 + _bmm(_heads_of(jnp.tanh(wd), w_up), w_up))
    a = jax.nn.sigmoid(a0 + _bmm(_heads_of(ad, a_up), a_up))
    kk = k * k_k
    kk = kk * lax.rsqrt(jnp.sum(kk * kk, -1, keepdims=True) + EPS)
    k2 = k * (1.0 + (a - 1.0) * k_a)
    return lw, kk, k2, kk * a


def _rwkv_post(prm, ins, y, k2):
    r_k, ln_w, ln_b = prm[6:]
    r, v, z = ins[0], ins[2], ins[3]
    mean = jnp.mean(y, -1, keepdims=True)
    yc = y - mean
    var = jnp.mean(yc * yc, -1, keepdims=True)
    yn = yc * lax.rsqrt(var + RWKV_GN_EPS) * ln_w + ln_b
    bonus = jnp.sum(r * k2 * r_k, -1, keepdims=True) * v
    return (yn + bonus) * _silu(z)


@jax.custom_vjp
def _bmv(s, x):
    return jnp.sum(_r(s).astype(F32) * _r(x).astype(F32), -1, keepdims=True)


def _bmv_fwd(s, x):
    return _bmv(s, x), (s, x)


def _bmv_bwd(res, g):
    s, x = res
    return g * x, jnp.sum(_r(s).astype(F32) * _r(g).astype(F32), -2, keepdims=True)


_bmv.defvjp(_bmv_fwd, _bmv_bwd)


def _rwkv_steps(prm, cst, ins, s, steps):
    r, v = ins[0], ins[2]
    lw, kk, k2, m = _rwkv_pre(prm, ins)
    w = jnp.exp(lw)
    v_t = _t(v)
    lane = lax.broadcasted_iota(jnp.int32, (1, 1, CH), 2)
    y_t = jnp.zeros((s.shape[0], DH, CH), F32)
    for t in range(steps):
        e_t = (lane == t).astype(F32)
        row = (slice(None), slice(t, t + 1))
        sa = _bmv(s, -kk[row])
        s = s * w[row] + sa * m[row] + jnp.sum(v_t * e_t, -1, keepdims=True) * k2[row]
        y_t = y_t + _bmv(s, r[row]) * e_t
    return _rwkv_post(prm, ins, _t(y_t), k2)[:, :steps], s


def _rwkv_first_chunk(prm, cst, ins, s):
    k = RWKV_EXACT_STEPS
    y_head, s = _rwkv_steps(prm, cst, ins, s, k)
    y_tail, s, _ = _rwkv_chunk(prm, cst, [x[..., k:, :] for x in ins], s)
    return jnp.concatenate([y_head, y_tail], axis=-2), s


def _time_block(t):
    return TIME_BLOCK if t % TIME_BLOCK == 0 else t


def _load_chunk(ref, i):
    nb = ref.shape[1]
    if ref.shape[0] == NH:
        return jnp.concatenate([ref[:, b, pl.ds(i, CH), :] for b in range(nb)], axis=0)
    return ref[0, :, pl.ds(i, CH), :]


def _mixer_fwd(chunk_fn, name, ins, prm, cst, nb, t, first_fn=None, side=None, n_kept=0):
    tb = _time_block(t)
    nt, ncb, nch = t // tb, tb // CH, nb * NH
    n_in, n_prm, n_cst = len(ins), len(prm), len(cst)
    n_main, n_side = n_in + n_prm + n_cst, len(side.operands) if side else 0

    def body(*refs):
        in_refs = refs[:n_in]
        prm_refs = refs[n_in:n_in + n_prm]
        cst_refs = refs[n_in + n_prm:n_main]
        side_in = refs[n_main:n_main + n_side]
        y_ref, ck_ref = refs[n_main + n_side:n_main + n_side + 2]
        side_out = refs[n_main + n_side + 2:n_main + 2 * n_side + 2]
        s_scr = refs[n_main + 2 * n_side + 2]
        sems = refs[n_main + 2 * n_side + 3:]
        step_t = pl.program_id(0)

        if side is not None:
            @pl.when(step_t == 0)
            def _():
                _comm_start(side.copies(side_in, side_out, sems))

        @pl.when(step_t == 0)
        def _():
            s_scr[...] = jnp.zeros_like(s_scr)

        def chunk(c, i, fn=chunk_fn):
            s = s_scr[...]
            y, s_next, *kept = fn([jnp.tile(r[...], (nb, 1, 1)) for r in prm_refs],
                                  [jnp.tile(r[...], (nb, 1, 1)) for r in cst_refs],
                                  [_load_chunk(r, i) for r in in_refs], s)
            kept += [jnp.zeros_like(s)] * (n_kept - len(kept))
            for e, a in enumerate([s] + kept):
                ck_ref[c, e] = a
            for b in range(nb):
                y_ref[:, b, pl.ds(i, CH), :] = y[b * NH:(b + 1) * NH].astype(BF16)
            s_scr[...] = s_next

        def step(c, carry):
            chunk(c, pl.multiple_of(c * CH, CH))
            return carry

        if first_fn is None:
            lax.fori_loop(0, ncb, step, 0)
        else:
            @pl.when(step_t == 0)
            def _():
                chunk(0, 0, first_fn)

            @pl.when(step_t != 0)
            def _():
                chunk(0, 0)

            lax.fori_loop(1, ncb, step, 0)

        if side is not None:
            @pl.when(step_t == nt - 1)
            def _():
                _comm_wait(side.copies(side_in, side_out, sems))

    hbm = pl.BlockSpec(memory_space=pl.ANY)
    in_specs = [pl.BlockSpec((ng, nb, tb, DH), (lambda j, bi=bi: (bi, 0, j, 0))) for _, ng, bi in ins]
    in_specs += [pl.BlockSpec(p.shape, lambda j: (0, 0, 0)) for p in list(prm) + list(cst)]
    y, ck, *side_res = _pcall(
        body, name=name, grid=(nt,),
        in_specs=in_specs + [hbm] * n_side,
        out_specs=[pl.BlockSpec((NH, nb, tb, DH), lambda j: (0, 0, j, 0)),
                   pl.BlockSpec((ncb, 1 + n_kept, nch, DH, DH), lambda j: (j, 0, 0, 0, 0))] + [hbm] * n_side,
        out_shape=[jax.ShapeDtypeStruct((NH, nb, t, DH), BF16),
                   jax.ShapeDtypeStruct((t // CH, 1 + n_kept, nch, DH, DH), F32)]
        + (side.out_shapes if side else []),
        scratch_shapes=[pltpu.VMEM((nch, DH, DH), F32)] + (_comm_scratch(side) if side else []),
        compiler_params=_cparams(("arbitrary",)),
    )(*[a.reshape(a.shape[0], nb, t, DH) for a, _, _ in ins], *prm, *cst, *(side.operands if side else []))
    return y.reshape(NH, nb * t, DH), ck, side_res


def _mixer_bwd(chunk_fn, name, ins, prm, cst, ck, dy, dy_block, outs, routes, nb, t, first_fn=None, side=None):
    tb = _time_block(t)
    nt, ncb, nch = t // tb, tb // CH, nb * NH
    n_in, n_prm, n_cst, n_out = len(ins), len(prm), len(cst), len(outs)
    n_main, n_side = n_in + n_prm + n_cst + 2, len(side.operands) if side else 0
    n_kept = ck.shape[1] - 1

    def body(*refs):
        in_refs = refs[:n_in]
        prm_refs = refs[n_in:n_in + n_prm]
        cst_refs = refs[n_in + n_prm:n_in + n_prm + n_cst]
        ck_ref, dy_ref = refs[n_main - 2:n_main]
        side_in = refs[n_main:n_main + n_side]
        rest = refs[n_main + n_side:]
        out_refs = rest[:n_out]
        dprm_refs = rest[n_out:n_out + n_prm]
        side_out = rest[n_out + n_prm:n_out + n_prm + n_side]
        ds_scr = rest[n_out + n_prm + n_side]
        sems = rest[n_out + n_prm + n_side + 1:]
        step_t = pl.program_id(0)

        if side is not None:
            @pl.when(step_t == 0)
            def _():
                _comm_start(side.copies(side_in, side_out, sems))

        @pl.when(step_t == 0)
        def _():
            ds_scr[...] = jnp.zeros_like(ds_scr)
            for r in dprm_refs:
                r[...] = jnp.zeros_like(r)

        def chunk(c, i, fn=chunk_fn):
            cst_v = [jnp.tile(r[...], (nb, 1, 1)) for r in cst_refs]
            kept = [ck_ref[c, 1 + e] for e in range(n_kept)] if fn is chunk_fn else []
            _, vjp = jax.vjp(lambda p, x, s: fn(p, cst_v, x, s, *kept)[:2],
                             [jnp.tile(r[...], (nb, 1, 1)) for r in prm_refs],
                             [_load_chunk(r, i) for r in in_refs], ck_ref[c, 0])
            dy_c = jnp.concatenate([dy_ref[:, b, pl.ds(i, CH), :] for b in range(nb)], axis=0)
            d_prm, d_ins, d_s = vjp((dy_c, ds_scr[...]))
            for (oi, g0), r, g in zip(routes, in_refs, d_ins):
                o_ref = out_refs[oi]
                if r.shape[0] == NH:
                    for b in range(nb):
                        o_ref[g0:g0 + NH, b, pl.ds(i, CH), :] = g[b * NH:(b + 1) * NH].astype(o_ref.dtype)
                else:
                    o_ref[g0, :, pl.ds(i, CH), :] = g.astype(o_ref.dtype)
            for r, g in zip(dprm_refs, d_prm):
                r[...] += g
            ds_scr[...] = d_s

        def step(j, carry):
            c = ncb - 1 - j
            chunk(c, pl.multiple_of(c * CH, CH))
            return carry

        lax.fori_loop(0, ncb - 1, step, 0)
        if first_fn is None:
            chunk(0, 0)
        else:
            @pl.when(step_t == nt - 1)
            def _():
                chunk(0, 0, first_fn)

            @pl.when(step_t != nt - 1)
            def _():
                chunk(0, 0)

        if side is not None:
            @pl.when(step_t == nt - 1)
            def _():
                _comm_wait(side.copies(side_in, side_out, sems))

    def back(j):
        return nt - 1 - j

    hbm = pl.BlockSpec(memory_space=pl.ANY)
    in_specs = [pl.BlockSpec((ng, nb, tb, DH), (lambda j, bi=bi: (bi, 0, back(j), 0))) for _, ng, bi in ins]
    in_specs += [pl.BlockSpec(p.shape, lambda j: (0, 0, 0)) for p in list(prm) + list(cst)]
    in_specs += [pl.BlockSpec((ncb, 1 + n_kept, nch, DH, DH), lambda j: (back(j), 0, 0, 0, 0)),
                 pl.BlockSpec((NH, nb, tb, DH), lambda j: (dy_block, 0, back(j), 0))]
    out_specs = [pl.BlockSpec((ng, nb, tb, DH), lambda j: (0, 0, back(j), 0)) for ng, _ in outs]
    out_specs += [pl.BlockSpec((nch,) + p.shape[1:], lambda j: (0, 0, 0)) for p in prm]
    out_shape = [jax.ShapeDtypeStruct((ng, nb, t, DH), dt) for ng, dt in outs]
    out_shape += [jax.ShapeDtypeStruct((nch,) + p.shape[1:], F32) for p in prm]
    res = _pcall(
        body, name=name, grid=(nt,),
        in_specs=in_specs + [hbm] * n_side, out_specs=out_specs + [hbm] * n_side,
        out_shape=out_shape + (side.out_shapes if side else []),
        scratch_shapes=[pltpu.VMEM((nch, DH, DH), F32)] + (_comm_scratch(side) if side else []),
        compiler_params=_cparams(("arbitrary",)),
    )(*[a.reshape(a.shape[0], nb, t, DH) for a, _, _ in ins], *prm, *cst, ck, dy.reshape(dy.shape[0], nb, t, DH),
      *(side.operands if side else []))
    d_outs = [o.reshape(o.shape[0], nb * t, DH) for o in res[:n_out]]
    d_prm = [g.reshape((nb,) + p.shape) for g, p in zip(res[n_out:n_out + n_prm], prm)]
    return d_outs, d_prm, res[n_out + n_prm:]


def _shift_down(x, s):
    if s == 0:
        return x
    row = lax.broadcasted_iota(jnp.int32, x.shape, 0)
    return jnp.where(row < s, 0.0, pltpu.roll(x, s, 0))


def _shift_up(x, s):
    if s == 0:
        return x
    t = x.shape[0]
    row = lax.broadcasted_iota(jnp.int32, x.shape, 0)
    return jnp.where(row >= t - s, 0.0, pltpu.roll(x, t - s, 0))


def _conv_bwd(p, g0, ng, w, dy, nb, t, name):
    taps = w.shape[1]

    def body(x_ref, w_ref, dy_ref, dx_ref, dw_ref):
        x = x_ref[...]
        d = dy_ref[...].astype(F32)
        acc = w_ref[taps - 1:taps, :] * d
        dw_ref[taps - 1:taps, :] = jnp.sum(d * x, 0, keepdims=True)
        for i in range(taps - 1):
            s = taps - 1 - i
            acc = acc + w_ref[i:i + 1, :] * _shift_up(d, s)
            dw_ref[i:i + 1, :] = jnp.sum(d * _shift_down(x, s), 0, keepdims=True)
        dx_ref[...] = acc.astype(BF16)

    return _pcall(
        body, name=name, grid=(ng, nb),
        in_specs=[pl.BlockSpec((None, t, DH), lambda g, b: (g0 + g, b, 0)),
                  pl.BlockSpec((None, taps, DH), lambda g, b: (g, 0, 0)),
                  pl.BlockSpec((None, t, DH), lambda g, b: (g, b, 0))],
        out_specs=[pl.BlockSpec((None, t, DH), lambda g, b: (g, b, 0)),
                   pl.BlockSpec((None, None, taps, DH), lambda g, b: (g, b, 0, 0))],
        out_shape=[jax.ShapeDtypeStruct((ng, nb * t, DH), BF16),
                   jax.ShapeDtypeStruct((ng, nb, taps, DH), F32)],
        compiler_params=_cparams(("parallel", "parallel")),
    )(p, w, dy)


def _mix_group(g):
    return jnp.where(g < 16, G_RWKV + g, G_RWKV_WD + g - 16)


def _mix_bwd(p, mu, dy, nb, t, name):
    def body(x_ref, mu_ref, dy_ref, dx_ref, dmu_ref):
        x = x_ref[...]
        muv = mu_ref[...]
        d = dy_ref[...].astype(F32)
        dx_ref[...] = (d * (1.0 - muv) + _shift_up(d * muv, 1)).astype(BF16)
        dmu_ref[...] = jnp.sum(d * (_shift_down(x, 1) - x), 0, keepdims=True)

    return _pcall(
        body, name=name, grid=(18, nb),
        in_specs=[pl.BlockSpec((None, t, DH), lambda g, b: (_mix_group(g), b, 0)),
                  pl.BlockSpec((None, 1, DH), lambda g, b: (g, 0, 0)),
                  pl.BlockSpec((None, t, DH), lambda g, b: (g, b, 0))],
        out_specs=[pl.BlockSpec((None, t, DH), lambda g, b: (g, b, 0)),
                   pl.BlockSpec((None, None, 1, DH), lambda g, b: (g, b, 0, 0))],
        out_shape=[jax.ShapeDtypeStruct((18, nb * t, DH), BF16),
                   jax.ShapeDtypeStruct((18, nb, 1, DH), F32)],
        compiler_params=_cparams(("parallel", "parallel")),
    )(p, mu, dy)


def _sc_bwd(p, w, dy, nb, t, name):
    def body(p_ref, w_ref, dy_ref, dp_ref, dw_ref):
        bg, cg, xg, z = p_ref[0], p_ref[1], p_ref[2], p_ref[3]
        d = dy_ref[...]
        u = cg * xg
        u1 = _shift_down(u, 1)
        u2 = _shift_down(u, 2)
        conv = w_ref[2:3, :] * u + w_ref[1:2, :] * u1 + w_ref[0:1, :] * u2
        sg = jax.nn.sigmoid(z)
        sz = z * sg
        dp_ref[0] = (d * conv * sz).astype(BF16)
        dp_ref[3] = (d * bg * conv * (sg * (1.0 + z * (1.0 - sg)))).astype(BF16)
        dconv = d * bg * sz
        du = w_ref[2:3, :] * dconv + w_ref[1:2, :] * _shift_up(dconv, 1) + w_ref[0:1, :] * _shift_up(dconv, 2)
        dp_ref[1] = (du * xg).astype(BF16)
        dp_ref[2] = (du * cg).astype(BF16)
        dw_ref[2:3, :] = jnp.sum(dconv * u, 0, keepdims=True)
        dw_ref[1:2, :] = jnp.sum(dconv * u1, 0, keepdims=True)
        dw_ref[0:1, :] = jnp.sum(dconv * u2, 0, keepdims=True)

    return _pcall(
        body, name=name, grid=(NH, nb),
        in_specs=[pl.BlockSpec((4, t, DH), lambda j, b: (G_SC // 4 + j, b, 0)),
                  pl.BlockSpec((None, SC_TAPS, DH), lambda j, b: (j, 0, 0)),
                  pl.BlockSpec((None, t, DH), lambda j, b: (8 + j, b, 0))],
        out_specs=[pl.BlockSpec((4, t, DH), lambda j, b: (j, b, 0)),
                   pl.BlockSpec((None, None, SC_TAPS, DH), lambda j, b: (j, b, 0, 0))],
        out_shape=[jax.ShapeDtypeStruct((4 * NH, nb * t, DH), BF16),
                   jax.ShapeDtypeStruct((NH, nb, SC_TAPS, DH), F32)],
        compiler_params=_cparams(("parallel", "parallel")),
    )(p, w, dy)


def _row_tile(n):
    return 1024 if n % 1024 == 0 else n


def _regroup_in(w_all, name):
    tr = 256
    gs = GROUPS_PER_STEP

    def body(w_ref, o_ref):
        for g in _PADDED_GROUPS:
            o_ref[g // gs, :, DH * (g % gs):DH * (g % gs + 1)] = jnp.zeros((tr, DH), BF16)
        for g, a, d, off, ln in _SEGMENTS:
            lane = DH * (g % gs) + a
            o_ref[g // gs, :, lane:lane + ln] = w_ref[d, :, off:off + ln].astype(BF16)

    return _pcall(
        body, name=name, grid=(D_MODEL // tr,),
        in_specs=[pl.BlockSpec((N_DEV, tr, SHARD_COLS), lambda i: (0, i, 0))],
        out_specs=pl.BlockSpec((N_GROUPS // gs, tr, gs * DH), lambda i: (0, i, 0)),
        out_shape=jax.ShapeDtypeStruct((N_GROUPS // gs, D_MODEL, gs * DH), BF16),
        compiler_params=_cparams(("parallel",)),
    )(w_all)


def _regroup_out(dwg, name):
    tr = 256
    gs = GROUPS_PER_STEP

    def body(g_ref, o_ref):
        for g, a, d, off, ln in _SEGMENTS:
            lane = DH * (g % gs) + a
            o_ref[d, :, off:off + ln] = g_ref[g // gs, :, lane:lane + ln].astype(BF16)

    return _pcall(
        body, name=name, grid=(D_MODEL // tr,),
        in_specs=[pl.BlockSpec((N_GROUPS // gs, tr, gs * DH), lambda i: (0, i, 0))],
        out_specs=pl.BlockSpec((N_DEV, tr, SHARD_COLS), lambda i: (0, i, 0)),
        out_shape=jax.ShapeDtypeStruct((N_DEV, D_MODEL, SHARD_COLS), BF16),
        compiler_params=_cparams(("parallel",)),
    )(dwg)


CONV_BLOCKS = (0, 1, 2)
MIX_BLOCKS = (4, 5, 6, 7, 16)
SHIFTED_BLOCKS = {0: 0, 1: 1, 2: 2, 4: 3, 5: 4, 6: 5, 7: 6, 16: 7}
G_PM = 12
G_PM_WD, G_PM_AD = 29, 30
SC_BLOCK0 = G_SC // GROUPS_PER_STEP


def _norm_proj(x, pre_w, w_g, conv_w, mix_mu, sc_w, t, name):
    n = x.shape[0]
    tm = _row_tile(n)
    gs = GROUPS_PER_STEP
    nj = N_GROUPS // gs
    assert t % tm == 0, (t, tm)
    per_seq = t // tm

    def shifted_block(j):
        out = jnp.int32(len(SHIFTED_BLOCKS) - 1)
        for jj in sorted(SHIFTED_BLOCKS, reverse=True):
            out = jnp.where(j < jj, max(SHIFTED_BLOCKS[jj] - 1, 0), out)
        for jj, b in SHIFTED_BLOCKS.items():
            out = jnp.where(j == jj, b, out)
        return out

    def body(x_ref, pw_ref, w_ref, cw_ref, mu_ref, scw_ref, h_ref, p_ref, s_ref, ysc_ref, carry):
        i, j = pl.program_id(0), pl.program_id(1)

        @pl.when(j == 0)
        def _():
            xv = x_ref[...]
            h = xv * lax.rsqrt(jnp.mean(xv * xv, -1, keepdims=True) + EPS) * pw_ref[...]
            h_ref[...] = h.astype(BF16)

        r = jnp.dot(h_ref[...], w_ref[...], preferred_element_type=F32)
        for k in range(gs):
            p_ref[k] = r[:, DH * k:DH * (k + 1)]

        def shifts():
            first = (i % per_seq) == 0
            tail = jnp.where(first, 0.0, carry[j])
            above = jnp.concatenate([tail, jnp.zeros((tm - 8, gs * DH), F32)], axis=0)
            row = lax.broadcasted_iota(jnp.int32, r.shape, 0)
            out = [jnp.where(row < s, pltpu.roll(above, tm - 8 + s, 0), pltpu.roll(r, s, 0)) for s in (1, 2, 3)]
            carry[j] = r[tm - 8:, :]
            return out

        def store(v):
            for k in range(gs):
                s_ref[k] = v[:, DH * k:DH * (k + 1)]

        @pl.when(functools.reduce(jnp.logical_or, [j == b for b in CONV_BLOCKS]))
        def _():
            s1, s2, s3 = shifts()
            store(cw_ref[3:4, :] * r + cw_ref[0:1, :] * s3 + cw_ref[1:2, :] * s2 + cw_ref[2:3, :] * s1)

        @pl.when(functools.reduce(jnp.logical_or, [j == b for b in MIX_BLOCKS]))
        def _():
            s1 = shifts()[0]
            store(r + mu_ref[...] * (s1 - r))

        @pl.when((j >= SC_BLOCK0) & (j < SC_BLOCK0 + NH))
        def _():
            first = (i % per_seq) == 0
            tail = jnp.where(first, 0.0, carry[j])
            u = r[:, DH:2 * DH] * r[:, 2 * DH:3 * DH]
            above = jnp.concatenate([tail[:, DH:2 * DH] * tail[:, 2 * DH:3 * DH], jnp.zeros((tm - 8, DH), F32)], axis=0)
            row = lax.broadcasted_iota(jnp.int32, u.shape, 0)
            u1, u2 = [jnp.where(row < s, pltpu.roll(above, tm - 8 + s, 0), pltpu.roll(u, s, 0)) for s in (1, 2)]
            conv = scw_ref[2:3, :] * u + scw_ref[1:2, :] * u1 + scw_ref[0:1, :] * u2
            ysc_ref[...] = (r[:, :DH] * conv * _silu(r[:, 3 * DH:])).astype(BF16)
            carry[j] = r[tm - 8:, :]

    n_shifted = len(SHIFTED_BLOCKS) * gs
    return _pcall(
        body, name=name, grid=(n // tm, nj),
        in_specs=[pl.BlockSpec((tm, D_MODEL), lambda i, j: (i, 0)),
                  pl.BlockSpec((1, D_MODEL), lambda i, j: (0, 0)),
                  pl.BlockSpec((None, D_MODEL, gs * DH), lambda i, j: (j, 0, 0)),
                  pl.BlockSpec((None, GDN_TAPS, gs * DH), lambda i, j: (j, 0, 0)),
                  pl.BlockSpec((None, 1, gs * DH), lambda i, j: (j, 0, 0)),
                  pl.BlockSpec((None, SC_TAPS, DH), lambda i, j: (jnp.clip(j - SC_BLOCK0, 0, NH - 1), 0, 0))],
        out_specs=[pl.BlockSpec((tm, D_MODEL), lambda i, j: (i, 0)),
                   pl.BlockSpec((gs, tm, DH), lambda i, j: (j, i, 0)),
                   pl.BlockSpec((gs, tm, DH), lambda i, j: (shifted_block(j), i, 0)),
                   pl.BlockSpec((None, tm, DH), lambda i, j: (jnp.clip(j - SC_BLOCK0, 0, NH - 1), i, 0))],
        out_shape=[jax.ShapeDtypeStruct((n, D_MODEL), BF16),
                   jax.ShapeDtypeStruct((N_GROUPS, n, DH), F32),
                   jax.ShapeDtypeStruct((n_shifted, n, DH), F32),
                   jax.ShapeDtypeStruct((NH, n, DH), BF16)],
        scratch_shapes=[pltpu.VMEM((nj, 8, gs * DH), F32)],
        compiler_params=_cparams(("arbitrary", "arbitrary")),
    )(x, pre_w, w_g, conv_w, mix_mu, sc_w)


def _out_proj_norm(ys, wout_g, x, post_w, name):
    n = x.shape[0]
    tm = _row_tile(n)

    def body(y0, y1, y2, y3, w_ref, x_ref, pw_ref, out_ref, xn_ref):
        y = jnp.concatenate([yr[h] for yr in (y0, y1, y2, y3) for h in range(NH)], axis=-1)
        acc = jnp.dot(y, w_ref[...], preferred_element_type=F32)
        out_ref[...] = acc
        xn_ref[...] = x_ref[...] + acc * lax.rsqrt(jnp.mean(acc * acc, -1, keepdims=True) + EPS) * pw_ref[...]

    yspec = pl.BlockSpec((NH, tm, DH), lambda i: (0, i, 0))
    rows = pl.BlockSpec((tm, D_MODEL), lambda i: (i, 0))
    return _pcall(
        body, name=name, grid=(n // tm,),
        in_specs=[yspec] * 4 + [pl.BlockSpec((D_MODEL, D_MODEL), lambda i: (0, 0)), rows,
                                pl.BlockSpec((1, D_MODEL), lambda i: (0, 0))],
        out_specs=[rows, rows],
        out_shape=[jax.ShapeDtypeStruct((n, D_MODEL), F32)] * 2,
        compiler_params=_cparams(("parallel",)),
    )(*ys, wout_g.reshape(D_MODEL, D_MODEL), x, post_w)


def _loss_grad(x, tgt, name):
    n = x.shape[0]
    tm = _row_tile(n)

    def body(x_ref, t_ref, dx_ref, l_ref):
        @pl.when(pl.program_id(0) == 0)
        def _():
            l_ref[...] = jnp.zeros_like(l_ref)

        e = x_ref[...] - t_ref[...]
        dx_ref[...] = e * (1.0 / D_MODEL)
        l_ref[...] += jnp.sum(jnp.sum(e * e, -1, keepdims=True), 0, keepdims=True) * (0.5 / D_MODEL)

    rows = pl.BlockSpec((tm, D_MODEL), lambda i: (i, 0))
    return _pcall(
        body, name=name, grid=(n // tm,),
        in_specs=[rows, rows],
        out_specs=[rows, pl.BlockSpec((1, 128), lambda i: (0, 0))],
        out_shape=[jax.ShapeDtypeStruct((n, D_MODEL), F32), jax.ShapeDtypeStruct((1, 128), F32)],
        compiler_params=_cparams(("arbitrary",)),
    )(x, tgt)


def _rmsnorm_bwd(xv, w, d):
    r = lax.rsqrt(jnp.mean(xv * xv, -1, keepdims=True) + EPS)
    xh = xv * r
    dxh = d * w
    dx = r * (dxh - xh * jnp.mean(dxh * xh, -1, keepdims=True))
    return dx, d * xh


def _post_bwd(dxn, out, post_w, wout_g, name):
    n = dxn.shape[0]
    tm = _row_tile(n)

    def body(d_ref, o_ref, pw_ref, w_ref, do_ref, dy_ref, dpw_ref):
        @pl.when(pl.program_id(0) == 0)
        def _():
            dpw_ref[...] = jnp.zeros_like(dpw_ref)

        dout, dw_rows = _rmsnorm_bwd(o_ref[...], pw_ref[...], d_ref[...])
        dpw_ref[...] += jnp.sum(dw_rows, 0, keepdims=True)
        db = dout.astype(BF16)
        do_ref[...] = db
        dy = lax.dot_general(db, w_ref[...], (((1,), (1,)), ((), ())), preferred_element_type=F32)
        for g in range(4 * NH):
            dy_ref[g] = dy[:, DH * g:DH * (g + 1)]

    rows = pl.BlockSpec((tm, D_MODEL), lambda i: (i, 0))
    vec = pl.BlockSpec((1, D_MODEL), lambda i: (0, 0))
    return _pcall(
        body, name=name, grid=(n // tm,),
        in_specs=[rows, rows, vec, pl.BlockSpec((D_MODEL, D_MODEL), lambda i: (0, 0))],
        out_specs=[rows, pl.BlockSpec((4 * NH, tm, DH), lambda i: (0, i, 0)), vec],
        out_shape=[jax.ShapeDtypeStruct((n, D_MODEL), BF16),
                   jax.ShapeDtypeStruct((4 * NH, n, DH), F32),
                   jax.ShapeDtypeStruct((1, D_MODEL), F32)],
        compiler_params=_cparams(("arbitrary",)),
    )(dxn, out, post_w, wout_g.reshape(D_MODEL, D_MODEL))


def _dwout(ys, dout, name):
    n = dout.shape[0]
    tm = _row_tile(n)

    def body(y0, y1, y2, y3, d_ref, dw_ref):
        @pl.when(pl.program_id(0) == 0)
        def _():
            dw_ref[...] = jnp.zeros_like(dw_ref)

        y = jnp.concatenate([yr[h] for yr in (y0, y1, y2, y3) for h in range(NH)], axis=-1)
        dw_ref[...] += lax.dot_general(y, d_ref[...], (((0,), (0,)), ((), ())), preferred_element_type=F32)

    yspec = pl.BlockSpec((NH, tm, DH), lambda i: (0, i, 0))
    return _pcall(
        body, name=name, grid=(n // tm,),
        in_specs=[yspec] * 4 + [pl.BlockSpec((tm, D_MODEL), lambda i: (i, 0))],
        out_specs=pl.BlockSpec((D_MODEL, D_MODEL), lambda i: (0, 0)),
        out_shape=jax.ShapeDtypeStruct((D_MODEL, D_MODEL), F32),
        compiler_params=_cparams(("arbitrary",)),
    )(*ys, dout)


def _source_specs(sources, rows_first):
    gs = GROUPS_PER_STEP
    spans, specs, j0 = [], [], 0
    for a in sources:
        nblk = a.shape[0] // gs
        spans.append((j0, j0 + nblk))
        shape = (gs, _row_tile(a.shape[1]), DH)

        def blk(j, j0=j0, nblk=nblk):
            return jnp.clip(j - j0, 0, nblk - 1)

        if rows_first:
            specs.append(pl.BlockSpec(shape, (lambda i, j, blk=blk: (blk(j), i, 0))))
        else:
            specs.append(pl.BlockSpec(shape, (lambda j, i, blk=blk: (blk(j), i, 0))))
        j0 += nblk
    return spans, specs


def _dh_prenorm_bwd(sources, w_g, x, pre_w, dxn, name, side=None):
    n = x.shape[0]
    tm = _row_tile(n)
    gs = GROUPS_PER_STEP
    nj = N_GROUPS // gs
    ni = n // tm
    spans, src_specs = _source_specs(sources, True)
    ns = len(sources)
    n_side = len(side.operands) if side else 0

    def body(*refs):
        src = refs[:ns]
        w_ref, x_ref, pw_ref, d_ref = refs[ns:ns + 4]
        side_in = refs[ns + 4:ns + 4 + n_side]
        dx_ref, dpw_ref = refs[ns + 4 + n_side:ns + 6 + n_side]
        side_out = refs[ns + 6 + n_side:ns + 6 + 2 * n_side]
        acc = refs[ns + 6 + 2 * n_side]
        sems = refs[ns + 7 + 2 * n_side:]
        i, j = pl.program_id(0), pl.program_id(1)

        if side is not None:
            @pl.when((i == 0) & (j == 0))
            def _():
                _comm_start(side.copies(side_in, side_out, sems))

        @pl.when((i == 0) & (j == 0))
        def _():
            dpw_ref[...] = jnp.zeros_like(dpw_ref)

        @pl.when(j == 0)
        def _():
            acc[...] = jnp.zeros_like(acc)

        for s_ref, (lo, hi) in zip(src, spans):
            @pl.when((j >= lo) & (j < hi))
            def _(s_ref=s_ref):
                four = jnp.concatenate([s_ref[k] for k in range(gs)], axis=-1)
                acc[...] += lax.dot_general(four, w_ref[...], (((1,), (1,)), ((), ())), preferred_element_type=F32)

        @pl.when(j == nj - 1)
        def _():
            dx, dw_rows = _rmsnorm_bwd(x_ref[...], pw_ref[...], acc[...])
            dx_ref[...] = d_ref[...] + dx
            dpw_ref[...] += jnp.sum(dw_rows, 0, keepdims=True)

        if side is not None:
            @pl.when((i == ni - 1) & (j == nj - 1))
            def _():
                _comm_wait(side.copies(side_in, side_out, sems))

    hbm = pl.BlockSpec(memory_space=pl.ANY)
    rows = pl.BlockSpec((tm, D_MODEL), lambda i, j: (i, 0))
    vec = pl.BlockSpec((1, D_MODEL), lambda i, j: (0, 0))
    dx, dpw, *side_res = _pcall(
        body, name=name, grid=(ni, nj),
        in_specs=src_specs + [pl.BlockSpec((None, D_MODEL, gs * DH), lambda i, j: (j, 0, 0)), rows, vec, rows]
        + [hbm] * n_side,
        out_specs=[rows, vec] + [hbm] * n_side,
        out_shape=[jax.ShapeDtypeStruct((n, D_MODEL), F32), jax.ShapeDtypeStruct((1, D_MODEL), F32)]
        + (side.out_shapes if side else []),
        scratch_shapes=[pltpu.VMEM((tm, D_MODEL), F32)] + (_comm_scratch(side) if side else []),
        compiler_params=_cparams(("arbitrary", "arbitrary")),
    )(*sources, w_g, x, pre_w, dxn, *(side.operands if side else []))
    return dx, dpw, side_res


def _dwin(hb, sources, name):
    n = hb.shape[0]
    tm = _row_tile(n)
    gs = GROUPS_PER_STEP
    ni, nj = n // tm, N_GROUPS // gs
    spans, src_specs = _source_specs(sources, True)
    ns = len(sources)

    def body(*refs):
        h_ref = refs[0]
        src = refs[1:1 + ns]
        out_ref, acc, sem = refs[1 + ns:]
        i, j = pl.program_id(0), pl.program_id(1)

        @pl.when((i == 0) & (j == 0))
        def _():
            acc[...] = jnp.zeros_like(acc)

        h = h_ref[...]
        for s_ref, (lo, hi) in zip(src, spans):
            @pl.when((j >= lo) & (j < hi))
            def _(s_ref=s_ref):
                four = jnp.concatenate([s_ref[k] for k in range(gs)], axis=-1)
                acc[j] += jnp.dot(h, four, preferred_element_type=F32)

        @pl.when((i == ni - 1) & (j == nj - 1))
        def _():
            done = pltpu.make_async_copy(acc, out_ref, sem)
            done.start()
            done.wait()

    return _pcall(
        body, name=name, grid=(ni, nj),
        in_specs=[pl.BlockSpec((D_MODEL, tm), lambda i, j: (0, i))] + src_specs,
        out_specs=pl.BlockSpec(memory_space=pl.ANY),
        out_shape=jax.ShapeDtypeStruct((nj, D_MODEL, gs * DH), F32),
        scratch_shapes=[pltpu.VMEM((nj, D_MODEL, gs * DH), F32), pltpu.SemaphoreType.DMA],
        compiler_params=_cparams(("arbitrary", "arbitrary")),
    )(jnp.transpose(hb), *sources)


def _adamw_math(w, g, m, v):
    c1 = 1.0 - ADAM_B1 ** ADAM_STEP
    c2 = 1.0 - ADAM_B2 ** ADAM_STEP
    nm = ADAM_B1 * m + (1.0 - ADAM_B1) * g
    nv = ADAM_B2 * v + (1.0 - ADAM_B2) * (g * g)
    return -ADAM_LR * ((nm / c1) / (jnp.sqrt(nv / c2) + ADAM_EPS) + ADAM_WD * w), nm, nv


def _adamw(w, g, m, v, name):
    r, c = w.shape
    tr = 256 if r % 256 == 0 else r

    def body(w_ref, g_ref, m_ref, v_ref, d_ref, nm_ref, nv_ref):
        d_ref[...], nm_ref[...], nv_ref[...] = _adamw_math(w_ref[...], g_ref[...], m_ref[...], v_ref[...])

    spec = pl.BlockSpec((tr, c), lambda i: (i, 0))
    return _pcall(
        body, name=name, grid=(r // tr,),
        in_specs=[spec] * 4, out_specs=[spec] * 3,
        out_shape=[jax.ShapeDtypeStruct((r, c), F32)] * 3,
        compiler_params=_cparams(("parallel",)),
    )(w, g, m, v)


def _sum_adamw(parts, w, m, v, name):
    r, c = w.shape
    tr = 128 if r % 128 == 0 else r

    def body(p_ref, w_ref, m_ref, v_ref, g_ref, d_ref, nm_ref, nv_ref):
        g = p_ref[0].astype(F32)
        for k in range(1, N_DEV):
            g = g + p_ref[k].astype(F32)
        g_ref[...] = g
        d_ref[...], nm_ref[...], nv_ref[...] = _adamw_math(w_ref[...], g, m_ref[...], v_ref[...])

    spec = pl.BlockSpec((tr, c), lambda i: (i, 0))
    return _pcall(
        body, name=name, grid=(r // tr,),
        in_specs=[pl.BlockSpec((N_DEV, tr, c), lambda i: (0, i, 0))] + [spec] * 3, out_specs=[spec] * 4,
        out_shape=[jax.ShapeDtypeStruct((r, c), F32)] * 4,
        compiler_params=_cparams(("parallel",)),
    )(parts, w, m, v)


def _me():
    return lax.axis_index("x"), lax.axis_index("y"), lax.axis_index("c")


def _flat(x, y, c):
    return 4 * x + 2 * y + c


def _peer(k):
    x, y, c = _me()
    return (x ^ ((k >> 2) & 1), y ^ ((k >> 1) & 1), c ^ (k & 1))


def _gather_plan(blocks):
    def copies(x_refs, out_refs, sems):
        send_sems, recv_sems, local_sems = sems
        me = _flat(*_me())
        local = [pltpu.make_async_copy(x, o.at[me], local_sems.at[a]) for a, (x, o) in enumerate(zip(x_refs, out_refs))]
        outgoing, incoming = [], []
        for k in range(1, N_DEV):
            src = _flat(*_peer(k))
            for a, (x, o) in enumerate(zip(x_refs, out_refs)):
                for slot, group in ((me, outgoing), (src, incoming)):
                    group.append(pltpu.make_async_remote_copy(
                        src_ref=x, dst_ref=o.at[slot], send_sem=send_sems.at[a, k - 1], recv_sem=recv_sems.at[a, k - 1],
                        device_id=_peer(k), device_id_type=MESH))
        return local, outgoing, incoming

    return _Comm(list(blocks), [jax.ShapeDtypeStruct((N_DEV,) + b.shape, b.dtype) for b in blocks], copies)


def _exchange_plan(sends):
    def copies(s_refs, out_refs, sems):
        send_sems, recv_sems, local_sems = sems
        me = _flat(*_me())
        local = [pltpu.make_async_copy(s.at[me], o.at[0], local_sems.at[i]) for i, (s, o) in enumerate(zip(s_refs, out_refs))]
        outgoing = []
        for k in range(1, N_DEV):
            to = _flat(*_peer(k))
            for i, (s, o) in enumerate(zip(s_refs, out_refs)):
                outgoing.append(pltpu.make_async_remote_copy(
                    src_ref=s.at[to], dst_ref=o.at[k], send_sem=send_sems.at[i, k - 1], recv_sem=recv_sems.at[i, k - 1],
                    device_id=_peer(k), device_id_type=MESH))
        return local, outgoing, outgoing

    return _Comm(list(sends), [jax.ShapeDtypeStruct(s.shape, s.dtype) for s in sends], copies)


def _comm_scratch(plan):
    n = len(plan.operands)
    return [pltpu.SemaphoreType.DMA((n, N_DEV - 1)), pltpu.SemaphoreType.DMA((n, N_DEV - 1)),
            pltpu.SemaphoreType.DMA((n,))]


def _comm_start(copies):
    local, outgoing, _ = copies
    for cp in local + outgoing:
        cp.start()


def _comm_wait(copies):
    local, outgoing, incoming = copies
    for cp in incoming:
        cp.wait_recv()
    for cp in outgoing:
        cp.wait_send()
    for cp in local:
        cp.wait()


def _run_comm(plan, name):
    n = len(plan.operands)

    def body(*refs):
        copies = plan.copies(refs[:n], refs[n:2 * n], refs[2 * n:])
        _comm_start(copies)
        _comm_wait(copies)

    return _pcall(
        body, name=name,
        in_specs=[pl.BlockSpec(memory_space=pl.ANY)] * n,
        out_specs=[pl.BlockSpec(memory_space=pl.ANY)] * n,
        out_shape=plan.out_shapes,
        scratch_shapes=_comm_scratch(plan),
    )(*plan.operands)


def _all_gather_two_level(blocks, name):
    na = len(blocks)

    def body(*refs):
        x_refs, out_refs = refs[:na], refs[na:2 * na]
        send_sems, recv_sems, local_sems = refs[2 * na:]
        x, y, c = _me()
        me, sibling = (x, y, c), (x, y, 1 - c)
        chips = [(1 - x, y), (x, 1 - y), (1 - x, 1 - y)]

        def copy(a, k, block, to, own=False):
            slot = out_refs[a].at[_flat(*block)]
            return pltpu.make_async_remote_copy(
                src_ref=x_refs[a] if own else slot, dst_ref=slot, send_sem=send_sems.at[a, k],
                recv_sem=recv_sems.at[a, k], device_id=to, device_id_type=MESH)

        mine = [pltpu.make_async_copy(x_refs[a], out_refs[a].at[_flat(*me)], local_sems.at[a]) for a in range(na)]
        first = [copy(a, 0, me, sibling, own=True) for a in range(na)]
        first += [copy(a, 1 + j, me, (*chip, c), own=True) for j, chip in enumerate(chips) for a in range(na)]
        for cp in mine + first:
            cp.start()
        passed = []
        for j, chip in enumerate(chips):
            for a in range(na):
                copy(a, 1 + j, (*chip, c), me).wait_recv()
                cp = copy(a, 4 + j, (*chip, c), sibling)
                cp.start()
                passed.append(cp)
        for a in range(na):
            copy(a, 0, sibling, me).wait_recv()
        for j, chip in enumerate(chips):
            for a in range(na):
                copy(a, 4 + j, (*chip, 1 - c), me).wait_recv()
        for cp in first + passed:
            cp.wait_send()
        for cp in mine:
            cp.wait()

    return _pcall(
        body, name=name,
        in_specs=[pl.BlockSpec(memory_space=pl.ANY)] * na,
        out_specs=[pl.BlockSpec(memory_space=pl.ANY)] * na,
        out_shape=[jax.ShapeDtypeStruct((N_DEV,) + b.shape, b.dtype) for b in blocks],
        scratch_shapes=[pltpu.SemaphoreType.DMA((na, N_DEV - 1)), pltpu.SemaphoreType.DMA((na, N_DEV - 1)),
                        pltpu.SemaphoreType.DMA((na,))],
    )(*blocks)


def _sum_slots(a, name):
    r = a.shape[1]

    def body(a_ref, o_ref):
        acc = a_ref[0]
        for d in range(1, N_DEV):
            acc = acc + a_ref[d]
        o_ref[...] = acc

    return _pcall(body, name=name, out_shape=jax.ShapeDtypeStruct((r, 128), F32), compiler_params=_cparams())(a)


def _all_reduce_small(blk, name):
    r = blk.shape[0]

    def body(x_ref, out_ref, gath, send_sems, recv_sems):
        me = _flat(*_me())
        gath[me] = x_ref[...]
        copies = []
        for k in range(1, N_DEV):
            cp = pltpu.make_async_remote_copy(
                src_ref=x_ref, dst_ref=gath.at[me],
                send_sem=send_sems.at[k - 1], recv_sem=recv_sems.at[k - 1],
                device_id=_peer(k), device_id_type=MESH)
            cp.start()
            copies.append(cp)
        for k in range(1, N_DEV):
            src = _flat(*_peer(k))
            pltpu.make_async_remote_copy(
                src_ref=x_ref, dst_ref=gath.at[src],
                send_sem=send_sems.at[k - 1], recv_sem=recv_sems.at[k - 1],
                device_id=_peer(k), device_id_type=MESH).wait_recv()
        for cp in copies:
            cp.wait_send()
        acc = gath[0]
        for d in range(1, N_DEV):
            acc = acc + gath[d]
        out_ref[...] = acc

    return _pcall(
        body, name=name,
        in_specs=[pl.BlockSpec(memory_space=pltpu.VMEM)],
        out_specs=pl.BlockSpec(memory_space=pltpu.VMEM),
        out_shape=jax.ShapeDtypeStruct((r, 128), F32),
        scratch_shapes=[pltpu.VMEM((N_DEV, r, 128), F32),
                        pltpu.SemaphoreType.DMA((N_DEV - 1,)), pltpu.SemaphoreType.DMA((N_DEV - 1,))],
    )(blk)


def _heads(vec):
    return vec.reshape(NH, 1, DH)


def _rep(vec4):
    return jnp.broadcast_to(vec4.reshape(NH, 1, 1), (NH, 1, DH))


def _onehot_lane(offset):
    m = np.zeros((NH, 1, DH), np.float32)
    for h in range(NH):
        m[h, 0, offset + h] = 1.0
    return jnp.asarray(m)


_TINY = (("gdn_conv_w", (DEPTH, 4, 96)), ("rwkv_w_up", (DEPTH, 64, 32)), ("rwkv_a_up", (DEPTH, 64, 32)),
         ("sc_conv_w", (DEPTH, 3, 32)))
_TINY_ROWS = -(-sum(int(np.prod(s)) for _, s in _TINY) // 1024) * 8


def _pack_rows(arrays, rows, fill=0.0):
    flat = jnp.concatenate([a.reshape(-1) for a in arrays])
    return jnp.pad(flat, (0, rows * 128 - flat.shape[0]), constant_values=fill).reshape(rows, 128)


def _unpack_rows(p, named_shapes):
    lead = p.shape[:-2]
    flat = p.reshape(lead + (-1,))
    out, o = {}, 0
    for n, s in named_shapes:
        size = int(np.prod(s))
        out[n] = flat[..., o:o + size].reshape(lead + tuple(s))
        o += size
    return out


def _gather_last(a):
    return jnp.transpose(a, (1, 0, 2)).reshape(a.shape[1], -1)


def _split_last(a):
    r, c8 = a.shape
    return jnp.transpose(a.reshape(r, N_DEV, c8 // N_DEV), (1, 0, 2))


_SMALL = (("pre_norm_w", (DEPTH, 1024)), ("gdn_a_log", (DEPTH, 4)), ("gdn_dt_bias", (DEPTH, 4)),
          ("gdn_norm_w", (DEPTH, 64)), ("rwkv_mu", (DEPTH, 1152)), ("rwkv_w0", (DEPTH, 256)),
          ("rwkv_a0", (DEPTH, 256)), ("rwkv_k_k", (DEPTH, 256)), ("rwkv_k_a", (DEPTH, 256)),
          ("rwkv_r_k", (DEPTH, 256)), ("rwkv_ln_w", (DEPTH, 256)), ("rwkv_ln_b", (DEPTH, 256)),
          ("gla_a_up", (DEPTH, 16, 128)), ("gla_a_bias", (DEPTH, 128)), ("gla_norm_w", (DEPTH, 64)),
          ("post_norm_w", (DEPTH, 1024)), ("loss", ()))
_SMALL_ROWS = -(-sum(int(np.prod(s)) for _, s in _SMALL) // 1024) * 8


def _big_weights(w_in_all, w_out_all, l):
    return dict(w_g=_regroup_in(w_in_all, f"regroup_in{l}"),
                wout_g=w_out_all.reshape(4 * NH, DH, D_MODEL).astype(BF16))


def _layer_params(wts, tiny, l):
    conv = _gather_last(tiny["gdn_conv_w"][:, l])
    q = {}
    q["gdn_conv"] = jnp.transpose(conv.reshape(GDN_TAPS, 12, DH), (1, 0, 2))
    q["gdn_prm"] = [_rep(wts["gdn_a_log"][l]), _rep(wts["gdn_dt_bias"][l]),
                    jnp.broadcast_to(wts["gdn_norm_w"][l].reshape(1, 1, DH), (NH, 1, DH))]
    q["gdn_cst"] = [_onehot_lane(0), _onehot_lane(NH)]
    q["rwkv_mu"] = wts["rwkv_mu"][l].reshape(18, 1, DH)
    gs, nj = GROUPS_PER_STEP, N_GROUPS // GROUPS_PER_STEP
    side_by_side = lambda a: jnp.transpose(a.reshape(-1, gs, a.shape[1], DH), (0, 2, 1, 3)).reshape(-1, a.shape[1], gs * DH)
    q["conv_blocks"] = jnp.pad(side_by_side(q["gdn_conv"]), ((0, nj - len(CONV_BLOCKS)), (0, 0), (0, 0)))
    singles = jnp.pad(q["rwkv_mu"][16:18].reshape(1, 1, 2 * DH), ((0, 0), (0, 0), (DH, DH)))
    q["mix_blocks"] = jnp.concatenate([jnp.zeros((4, 1, gs * DH), F32), side_by_side(q["rwkv_mu"][:16]),
                                       jnp.zeros((8, 1, gs * DH), F32), singles], axis=0)
    w_up = jnp.transpose(_gather_last(tiny["rwkv_w_up"][:, l]).reshape(64, NH, DH), (1, 0, 2))
    a_up = jnp.transpose(_gather_last(tiny["rwkv_a_up"][:, l]).reshape(64, NH, DH), (1, 0, 2))
    q["rwkv_prm"] = [_heads(wts["rwkv_w0"][l]), w_up, _heads(wts["rwkv_a0"][l]), a_up,
                     _heads(wts["rwkv_k_k"][l]), _heads(wts["rwkv_k_a"][l]), _heads(wts["rwkv_r_k"][l]),
                     _heads(wts["rwkv_ln_w"][l]), _heads(wts["rwkv_ln_b"][l])]
    sc = _gather_last(tiny["sc_conv_w"][:, l])
    q["sc_conv"] = jnp.transpose(sc.reshape(SC_TAPS, NH, DH), (1, 0, 2))
    gla_up = jnp.transpose(wts["gla_a_up"][l].reshape(16, NH, GLA_HEAD_K), (1, 0, 2))
    gla_up = jnp.pad(gla_up, ((0, 0), (0, DH - 16), (0, DH - GLA_HEAD_K)))
    gla_b = jnp.pad(wts["gla_a_bias"][l].reshape(NH, 1, GLA_HEAD_K), ((0, 0), (0, 0), (0, DH - GLA_HEAD_K)))
    q["gla_prm"] = [gla_up, gla_b, jnp.broadcast_to(wts["gla_norm_w"][l].reshape(1, 1, DH), (NH, 1, DH))]
    q["pre_w"] = wts["pre_norm_w"][l].reshape(1, D_MODEL)
    q["post_w"] = wts["post_norm_w"][l].reshape(1, D_MODEL)
    return q


def _mixer_inputs(p, ps):
    gdn = [(ps, 4, 0), (ps, 4, 1), (ps, 4, 2), (p, 4, G_GDN // 4 + 3), (p, 1, G_GDN_AB)]
    rwkv = [(ps, 4, G_PM // 4 + k) for k in range(4)] + [(ps, 1, G_PM_WD), (ps, 1, G_PM_AD)]
    gla = [(p, 4, G_GLA // 4 + k) for k in range(4)] + [(p, 1, G_GLA_AD)]
    return gdn, rwkv, gla


def _layer_fwd(x, q, nb, t, l, side=None):
    hb, p, ps, y_sc = _norm_proj(x, q["pre_w"], q["w_g"], q["conv_blocks"], q["mix_blocks"], q["sc_conv"], t,
                                 f"norm_proj{l}")
    gdn_in, rwkv_in, gla_in = _mixer_inputs(p, ps)
    y_gdn, ck_gdn, _ = _mixer_fwd(_gdn_chunk, f"gdn_fwd{l}", gdn_in, q["gdn_prm"], q["gdn_cst"], nb, t, n_kept=1)
    y_rwkv, ck_rwkv, side_res = _mixer_fwd(_rwkv_chunk, f"rwkv_fwd{l}", rwkv_in, q["rwkv_prm"], [], nb, t,
                                           first_fn=_rwkv_first_chunk, side=side, n_kept=1)
    y_gla, ck_gla, _ = _mixer_fwd(_gla_chunk, f"gla_fwd{l}", gla_in, q["gla_prm"], [], nb, t)
    ys = (y_gdn, y_rwkv, y_sc, y_gla)
    out, xn = _out_proj_norm(ys, q["wout_g"], x, q["post_w"], f"out_proj{l}")
    saved = dict(x=x, hb=hb, p=p, ps=ps, ys=ys, out=out, ck=(ck_gdn, ck_rwkv, ck_gla))
    return xn, saved, side_res


def _layer_bwd(dxn, q, sv, nb, t, l, side=None, exchange_own=False):
    p, ys = sv["p"], sv["ys"]
    dout, dy, d_post = _post_bwd(dxn, sv["out"], q["post_w"], q["wout_g"], f"post_bwd{l}")
    d_wout = _dwout(ys, dout, f"dwout{l}").reshape(N_DEV, 128, D_MODEL).astype(BF16)
    gdn_in, rwkv_in, gla_in = _mixer_inputs(p, sv["ps"])
    ck_gdn, ck_rwkv, ck_gla = sv["ck"]
    g = {}

    (d_conv, dz, dab), (da_log, ddt, dnw), _ = _mixer_bwd(
        _gdn_chunk, f"gdn_bwd{l}", gdn_in, q["gdn_prm"], q["gdn_cst"], ck_gdn, dy, 0,
        [(12, BF16), (4, BF16), (1, BF16)], [(0, 0), (0, 4), (0, 8), (1, 0), (2, 0)], nb, t)
    dconv_in, d_gconv = _conv_bwd(p, G_GDN, 12, q["gdn_conv"], d_conv, nb, t, f"gdn_conv_bwd{l}")
    g["gdn_conv_w"] = jnp.transpose(d_gconv.sum(1), (1, 0, 2)).reshape(GDN_TAPS, 768)
    g["gdn_a_log"] = da_log.sum((0, 2, 3))
    g["gdn_dt_bias"] = ddt.sum((0, 2, 3))
    g["gdn_norm_w"] = dnw.sum((0, 1, 2))

    (d_pm,), d_rprm, side_res = _mixer_bwd(
        _rwkv_chunk, f"rwkv_bwd{l}", rwkv_in, q["rwkv_prm"], [], ck_rwkv, dy, 1,
        [(18, BF16)], [(0, 0), (0, 4), (0, 8), (0, 12), (0, 16), (0, 17)], nb, t, first_fn=_rwkv_first_chunk,
        side=side)
    dp_rwkv, d_mu = _mix_bwd(p, q["rwkv_mu"], d_pm, nb, t, f"rwkv_mix_bwd{l}")
    g["rwkv_mu"] = d_mu.sum(1).reshape(1152)
    rp = [a.sum(0) for a in d_rprm]
    g["rwkv_w0"] = rp[0].reshape(256)
    g["rwkv_w_up"] = jnp.transpose(rp[1], (1, 0, 2)).reshape(64, 256)
    g["rwkv_a0"] = rp[2].reshape(256)
    g["rwkv_a_up"] = jnp.transpose(rp[3], (1, 0, 2)).reshape(64, 256)
    for i, nme in enumerate(("rwkv_k_k", "rwkv_k_a", "rwkv_r_k", "rwkv_ln_w", "rwkv_ln_b")):
        g[nme] = rp[4 + i].reshape(256)

    dp_sc, d_scw = _sc_bwd(p, q["sc_conv"], dy, nb, t, f"sc_bwd{l}")
    g["sc_conv_w"] = jnp.transpose(d_scw.sum(1), (1, 0, 2)).reshape(SC_TAPS, 256)

    (dp_gla, dad), (d_aup, d_ab, d_gnw), _ = _mixer_bwd(
        _gla_chunk, f"gla_bwd{l}", gla_in, q["gla_prm"], [], ck_gla, dy, 3,
        [(16, BF16), (1, BF16)], [(0, 0), (0, 4), (0, 8), (0, 12), (1, 0)], nb, t)
    g["gla_a_up"] = jnp.transpose(d_aup.sum(0)[:, :16, :GLA_HEAD_K], (1, 0, 2)).reshape(16, 128)
    g["gla_a_bias"] = d_ab.sum(0)[:, 0, :GLA_HEAD_K].reshape(128)
    g["gla_norm_w"] = d_gnw.sum((0, 1, 2))

    singles = jnp.concatenate([dab, dp_rwkv[16:18], dad], axis=0)
    sources = [dconv_in, dz, dp_rwkv, dp_sc, dp_gla, singles]
    d_win = _regroup_out(_dwin(sv["hb"], sources, f"dwin{l}"), f"regroup_out{l}")
    own = _exchange_plan([d_win, d_wout]) if exchange_own else None
    dx, d_pre, got = _dh_prenorm_bwd(sources, q["w_g"], sv["x"], q["pre_w"], dxn, f"dh_bwd{l}", own)
    if exchange_own:
        d_win, d_wout = got
    g["pre_norm_w"] = d_pre.reshape(D_MODEL)
    g["post_norm_w"] = d_post.reshape(D_MODEL)
    return dx, g, d_win, d_wout, side_res


def _local_step(x, tgt, wts, tiny, w_in_all, w_out_all, later_shards=None):
    nb, t, d = x.shape
    xf = x.reshape(nb * t, d)
    overlap = later_shards is not None
    qs, saved = [], []
    big = _big_weights(w_in_all[0], w_out_all[0], 0)
    for l in range(DEPTH):
        q = dict(_layer_params(wts, tiny, l), **big)
        nxt = l + 1 < DEPTH
        side = _gather_plan(later_shards[l]) if overlap and nxt else None
        xf, sv, got = _layer_fwd(xf, q, nb, t, l, side)
        if nxt:
            big = _big_weights(*(got if overlap else (w_in_all[l + 1], w_out_all[l + 1])), l + 1)
        qs.append(q)
        saved.append(sv)
    dxf, lpart = _loss_grad(xf, tgt.reshape(nb * t, d), "loss")
    grads, d_win, d_wout = [None] * DEPTH, [None] * DEPTH, [None] * DEPTH
    for l in reversed(range(DEPTH)):
        side = _exchange_plan([d_win[l + 1], d_wout[l + 1]]) if overlap and l + 1 < DEPTH else None
        dxf, grads[l], d_win[l], d_wout[l], got = _layer_bwd(dxf, qs[l], saved[l], nb, t, l, side,
                                                             exchange_own=overlap and l == 0)
        if side is not None:
            d_win[l + 1], d_wout[l + 1] = got
    small = {k: jnp.stack([grads[l][k] for l in range(DEPTH)]) for k in grads[0]}
    return lpart[0, 0], dxf.reshape(nb, t, d), small, d_win, d_wout


_WEIGHTS = ("pre_norm_w", "w_in", "gdn_conv_w", "gdn_a_log", "gdn_dt_bias", "gdn_norm_w", "rwkv_mu", "rwkv_w0",
            "rwkv_w_up", "rwkv_a0", "rwkv_a_up", "rwkv_k_k", "rwkv_k_a", "rwkv_r_k", "rwkv_ln_w", "rwkv_ln_b",
            "sc_conv_w", "gla_a_up", "gla_a_bias", "gla_norm_w", "w_out", "post_norm_w")


def kernel(x, pre_norm_w, w_in, gdn_conv_w, gdn_a_log, gdn_dt_bias, gdn_norm_w, rwkv_mu, rwkv_w0, rwkv_w_up, rwkv_a0, rwkv_a_up, rwkv_k_k, rwkv_k_a, rwkv_r_k, rwkv_ln_w, rwkv_ln_b, sc_conv_w, gla_a_up, gla_a_bias, gla_norm_w, w_out, post_norm_w, loss_target, m_pre_norm_w, m_w_in, m_gdn_conv_w, m_gdn_a_log, m_gdn_dt_bias, m_gdn_norm_w, m_rwkv_mu, m_rwkv_w0, m_rwkv_w_up, m_rwkv_a0, m_rwkv_a_up, m_rwkv_k_k, m_rwkv_k_a, m_rwkv_r_k, m_rwkv_ln_w, m_rwkv_ln_b, m_sc_conv_w, m_gla_a_up, m_gla_a_bias, m_gla_norm_w, m_w_out, m_post_norm_w, v_pre_norm_w, v_w_in, v_gdn_conv_w, v_gdn_a_log, v_gdn_dt_bias, v_gdn_norm_w, v_rwkv_mu, v_rwkv_w0, v_rwkv_w_up, v_rwkv_a0, v_rwkv_a_up, v_rwkv_k_k, v_rwkv_k_a, v_rwkv_r_k, v_rwkv_ln_w, v_rwkv_ln_b, v_sc_conv_w, v_gla_a_up, v_gla_a_bias, v_gla_norm_w, v_w_out, v_post_norm_w):
    env = dict(locals())
    w = {n: env[n] for n in _WEIGHTS}
    m = {n: env["m_" + n] for n in _WEIGHTS}
    v = {n: env["v_" + n] for n in _WEIGHTS}
    tiny_names = [n for n, _ in _TINY]

    w_in_b, w_out_b = w_in.astype(BF16), w_out.astype(BF16)
    w_in_0, w_out_0, tiny_all = _all_gather_two_level(
        [w_in_b[0], w_out_b[0], _pack_rows([w[n] for n in tiny_names], _TINY_ROWS)], "gather_weights")
    tiny = _unpack_rows(tiny_all, _TINY)

    lpart, grad_x, small, r_win, r_wout = _local_step(
        x, loss_target, w, tiny, [w_in_0], [w_out_0], later_shards=[(w_in_b[l], w_out_b[l]) for l in range(1, DEPTH)])

    tiny_send = jnp.stack([_pack_rows([_split_last(small[n][l])[d] for n in tiny_names for l in range(DEPTH)],
                                      _TINY_ROWS) for d in range(N_DEV)])
    (r_tiny,) = _run_comm(_exchange_plan([tiny_send]), "scatter_grads")
    grads, delta, new_m, new_v = {}, {}, {}, {}
    for n, parts in (("w_in", r_win), ("w_out", r_wout)):
        res = [_sum_adamw(parts[l], w[n][l], m[n][l], v[n][l], f"adamw_{n}{l}") for l in range(DEPTH)]
        grads[n], delta[n], new_m[n], new_v[n] = [jnp.stack(o) for o in zip(*res)]
    tiny_sum = _sum_slots(r_tiny, "sum_tiny").reshape(-1)
    o = 0
    for n, s in _TINY:
        size = int(np.prod(s))
        grads[n] = tiny_sum[o:o + size].reshape(s)
        o += size

    small = dict(small)
    small["loss"] = lpart
    red = _unpack_rows(_all_reduce_small(_pack_rows([small[n] for n, _ in _SMALL], _SMALL_ROWS), "reduce_small"),
                       _SMALL)
    loss = red.pop("loss")
    grads.update(red)

    rest = [n for n in _WEIGHTS if n not in ("w_in", "w_out")]
    rest_shapes = [(n, w[n].shape) for n in rest]
    rows = -(-sum(int(np.prod(s)) for _, s in rest_shapes) // 1024) * 8
    outs = _adamw(_pack_rows([w[n] for n in rest], rows), _pack_rows([grads[n] for n in rest], rows),
                  _pack_rows([m[n] for n in rest], rows), _pack_rows([v[n] for n in rest], rows, 1.0), "adamw_rest")
    for dst, packed in zip((delta, new_m, new_v), outs):
        dst.update(_unpack_rows(packed, rest_shapes))

    return (loss, grad_x, *[grads[n] for n in _WEIGHTS], *[delta[n] for n in _WEIGHTS],
            *[new_m[n] for n in _WEIGHTS], *[new_v[n] for n in _WEIGHTS])
```

```python
import collections
import functools
import math

import numpy as np
import jax
import jax.numpy as jnp
from jax import lax
from jax.experimental import pallas as pl
from jax.experimental.pallas import tpu as pltpu

F32 = jnp.float32
BF16 = jnp.bfloat16

D_MODEL = 1024
DEPTH = 2
NH = 4
DH = 64
CH = 64
EPS = 1e-6
RWKV_GN_EPS = 64e-5
GLA_HEAD_K = 32
GLA_TAU = 16.0
GDN_TAPS = 4
SC_TAPS = 3
D_IN = 3992
N_DEV = 8
SHARD_COLS = D_IN // N_DEV

G_GDN = 0
G_RWKV = 16
G_SC = 32
G_GLA = 48
G_GDN_AB, G_RWKV_WD, G_RWKV_AD, G_GLA_AD = 64, 65, 66, 67
N_GROUPS = 68
GROUPS_PER_STEP = 4
TIME_BLOCK = 256
RWKV_EXACT_STEPS = 16

C_GDN, C_RWKV, C_SC, C_GLA = 0, 1032, 2184, 3208

ADAM_LR, ADAM_B1, ADAM_B2, ADAM_EPS, ADAM_WD, ADAM_STEP = 0.001, 0.9, 0.999, 1e-08, 0.01, 10

VMEM_LIMIT = 56 * 1024 * 1024
MESH = pl.DeviceIdType.MESH

_pcall = pl.pallas_call

_Comm = collections.namedtuple("_Comm", "operands out_shapes copies")


def _cparams(sem=None):
    if sem is None:
        return pltpu.CompilerParams(vmem_limit_bytes=VMEM_LIMIT)
    return pltpu.CompilerParams(dimension_semantics=sem, vmem_limit_bytes=VMEM_LIMIT)


def _group_segments():
    table = [(G_GDN + i, C_GDN + DH * i, DH) for i in range(16)]
    table.append((G_GDN_AB, C_GDN + 1024, 8))
    table += [(G_RWKV + i, C_RWKV + DH * i, DH) for i in range(16)]
    table += [(G_RWKV_WD, C_RWKV + 1024, DH), (G_RWKV_AD, C_RWKV + 1088, DH)]
    table += [(G_SC + 4 * j + k, C_SC + 256 * k + DH * j, DH) for j in range(NH) for k in range(4)]
    for h in range(NH):
        table += [(G_GLA + h, C_GLA + GLA_HEAD_K * h, GLA_HEAD_K),
                  (G_GLA + 4 + h, C_GLA + 128 + GLA_HEAD_K * h, GLA_HEAD_K),
                  (G_GLA + 8 + h, C_GLA + 256 + DH * h, DH),
                  (G_GLA + 12 + h, C_GLA + 512 + DH * h, DH)]
    table.append((G_GLA_AD, C_GLA + 768, 16))
    segs, padded = [], []
    for g, c, n in table:
        if n < DH:
            padded.append(g)
        a = 0
        while n > 0:
            d, off = divmod(c, SHARD_COLS)
            ln = min(n, SHARD_COLS - off)
            segs.append((g, a, d, off, ln))
            c, a, n = c + ln, a + ln, n - ln
    return segs, padded


_SEGMENTS, _PADDED_GROUPS = _group_segments()


def _dn(ta, tb):
    return (((1 if ta else 2,), (2 if tb else 1,)), ((0,), (0,)))


def _hdot(a, b, ta=False, tb=False):
    return lax.dot_general(a, b, _dn(ta, tb), precision=lax.Precision.HIGH, preferred_element_type=F32)


def _r(x):
    return x.astype(BF16)


def _rdot(a, b, ta=False, tb=False):
    return lax.dot_general(_r(a), _r(b), _dn(ta, tb), preferred_element_type=F32)


@jax.custom_vjp
def _bmm(a, b):
    return _rdot(a, b)


def _bmm_fwd(a, b):
    return _rdot(a, b), (a, b)


def _bmm_bwd(res, g):
    a, b = res
    return _rdot(g, b, tb=True), _rdot(a, g, ta=True)


_bmm.defvjp(_bmm_fwd, _bmm_bwd)


@jax.custom_vjp
def _bmm_nt(a, b):
    return _rdot(a, b, tb=True)


def _bmm_nt_fwd(a, b):
    return _rdot(a, b, tb=True), (a, b)


def _bmm_nt_bwd(res, g):
    a, b = res
    return _rdot(g, b), _rdot(g, a, ta=True)


_bmm_nt.defvjp(_bmm_nt_fwd, _bmm_nt_bwd)


@jax.custom_vjp
def _bmm_tn(a, b):
    return _rdot(a, b, ta=True)


def _bmm_tn_fwd(a, b):
    return _rdot(a, b, ta=True), (a, b)


def _bmm_tn_bwd(res, g):
    a, b = res
    return _rdot(b, g, tb=True), _rdot(a, g)


_bmm_tn.defvjp(_bmm_tn_fwd, _bmm_tn_bwd)


def _tri(n):
    i = lax.broadcasted_iota(jnp.int32, (n, n), 0)
    j = lax.broadcasted_iota(jnp.int32, (n, n), 1)
    return i >= j, i > j, i == j


def _heads_of(x, like):
    n = like.shape[0]
    if x.ndim == 2:
        return jnp.broadcast_to(x[None], (n,) + x.shape)
    seqs = x.shape[0]
    return jnp.broadcast_to(x[:, None], (seqs, n // seqs) + x.shape[1:]).reshape((n,) + x.shape[1:])


def _cumsum_rows(x):
    incl, _, _ = _tri(x.shape[-2])
    return _hdot(_heads_of(incl.astype(F32), x), x)


@jax.custom_vjp
def _inv_unit_lower(a):
    n = a.shape[-1]
    _, _, eye = _tri(n)
    pw = -a
    inv = eye.astype(F32) + pw
    for _ in range(math.ceil(math.log2(n)) - 1):
        pw = _hdot(pw, pw)
        inv = inv + _hdot(inv, pw)
    return inv


def _inv_unit_lower_fwd(a):
    inv = _inv_unit_lower(a)
    return inv, inv


def _inv_unit_lower_bwd(inv, g):
    return (-_hdot(_hdot(inv, g, ta=True), inv, tb=True),)


_inv_unit_lower.defvjp(_inv_unit_lower_fwd, _inv_unit_lower_bwd)


@jax.custom_vjp
def _inv_reuse(a, inv):
    return inv


def _inv_reuse_fwd(a, inv):
    return inv, inv


def _inv_reuse_bwd(inv, g):
    return _inv_unit_lower_bwd(inv, g)[0], jnp.zeros_like(inv)


_inv_reuse.defvjp(_inv_reuse_fwd, _inv_reuse_bwd)


def _silu(x):
    return x * jax.nn.sigmoid(x)


def _t(x):
    return jnp.swapaxes(x, -1, -2)


def _gdn_chunk(prm, cst, ins, s, tinv=None):
    a_log, dt_b, nw = prm
    m_a, m_b = cst
    cq, ck, cv, z, ab = ins
    ab = _heads_of(ab, m_a)
    incl, strict, _ = _tri(CH)
    q = _silu(cq)
    k = _silu(ck)
    v = _silu(cv)
    q = q * lax.rsqrt(jnp.sum(q * q, -1, keepdims=True) + EPS) * (DH ** -0.5)
    k = k * lax.rsqrt(jnp.sum(k * k, -1, keepdims=True) + EPS)
    a_raw = jnp.sum(ab * m_a, -1, keepdims=True)
    b_raw = jnp.sum(ab * m_b, -1, keepdims=True)
    gstep = -jnp.exp(a_log) * jax.nn.softplus(a_raw + dt_b)
    beta = jax.nn.sigmoid(b_raw)
    gc = _cumsum_rows(gstep)
    gl = jnp.sum(gstep, -2, keepdims=True)
    dec = jnp.where(incl, jnp.exp(jnp.where(incl, gc - _t(gc), 0.0)), 0.0)
    kb = k * beta
    a_mat = jnp.where(strict, _bmm_nt(kb, k) * dec, 0.0)
    tinv = _inv_unit_lower(a_mat) if tinv is None else _inv_reuse(a_mat, tinv)
    eg = jnp.exp(gc)
    u = _hdot(tinv, v * beta)
    w = _hdot(tinv, kb * eg)
    attn = _bmm_nt(q, k) * dec
    v_new = u - _bmm(w, s)
    o = _bmm(q * eg, s) + _bmm(attn, v_new)
    s_next = s * jnp.exp(gl) + _bmm_tn(k * jnp.exp(gl - gc), v_new)
    on = o * lax.rsqrt(jnp.mean(o * o, -1, keepdims=True) + EPS) * nw
    return on * _silu(z), s_next, tinv


def _gla_chunk(prm, cst, ins, st):
    a_up, a_bias, nw = prm
    q, k, v, z, ad = ins
    incl, _, _ = _tri(CH)
    la = jax.nn.log_sigmoid(_bmm(_heads_of(ad, a_up), a_up) + a_bias) * (1.0 / GLA_TAU)
    bc = _cumsum_rows(la)
    bl = jnp.sum(la, -2, keepdims=True)
    qe = q * (GLA_HEAD_K ** -0.5) * jnp.exp(bc)
    ke = k * jnp.exp(-bc)
    attn = jnp.where(incl, _bmm_nt(qe, ke), 0.0)
    o = _bmm_nt(qe, st) + _bmm(attn, v)
    st_next = st * jnp.exp(bl) + _bmm_tn(v, k * jnp.exp(bl - bc))
    on = o * lax.rsqrt(jnp.mean(o * o, -1, keepdims=True) + EPS) * nw
    return on * _silu(z), st_next


def _rwkv_chunk(prm, cst, ins, s, inv=None):
    r, v = ins[0], ins[2]
    incl, strict, _ = _tri(r.shape[-2])
    lw, kk, k2, m = _rwkv_pre(prm, ins)
    cum = _cumsum_rows(lw)
    ltot = jnp.sum(lw, -2, keepdims=True)
    n_t = -kk * jnp.exp(cum - lw)
    einv = jnp.exp(-cum)
    m_t = m * einv
    k_t = k2 * einv
    r_t = r * jnp.exp(cum)
    a_nm = jnp.where(strict, _hdot(n_t, m_t, tb=True), 0.0)
    a_nk = jnp.where(strict, _hdot(n_t, k_t, tb=True), 0.0)
    inv = _inv_unit_lower(-a_nm) if inv is None else _inv_reuse(-a_nm, inv)
    cm = _hdot(inv, _hdot(n_t, s, tb=True) + _bmm(a_nk, v))
    y = (_bmm_nt(r_t, s) + _bmm(jnp.where(incl, _hdot(r_t, m_t, tb=True), 0.0), cm)
         + _bmm(jnp.where(incl, _hdot(r_t, k_t, tb=True), 0.0), v))
    eend = jnp.exp(ltot - cum)
    s_next = s * jnp.exp(ltot) + _bmm_tn(cm, m * eend) + _bmm_tn(v, k2 * eend)
    return _rwkv_post(prm, ins, y, k2), s_next, inv


def _rwkv_pre(prm, ins):
    w0, w_up, a0, a_up, k_k, k_a = prm[:6]
    k, wd, ad = ins[1], ins[4], ins[5]
    lw = -math.exp(-0.5) * jax.nn.sigmoid(w0 + _bmm(_heads_of(jnp.tanh(wd), w_up), w_up))
    a = jax.nn.sigmoid(a0 + _bmm(_heads_of(ad, a_up), a_up))
    kk = k * k_k
    kk = kk * lax.rsqrt(jnp.sum(kk * kk, -1, keepdims=True) + EPS)
    k2 = k * (1.0 + (a - 1.0) * k_a)
    return lw, kk, k2, kk * a


def _rwkv_post(prm, ins, y, k2):
    r_k, ln_w, ln_b = prm[6:]
    r, v, z = ins[0], ins[2], ins[3]
    mean = jnp.mean(y, -1, keepdims=True)
    yc = y - mean
    var = jnp.mean(yc * yc, -1, keepdims=True)
    yn = yc * lax.rsqrt(var + RWKV_GN_EPS) * ln_w + ln_b
    bonus = jnp.sum(r * k2 * r_k, -1, keepdims=True) * v
    return (yn + bonus) * _silu(z)


@jax.custom_vjp
def _bmv(s, x):
    return jnp.sum(_r(s).astype(F32) * _r(x).astype(F32), -1, keepdims=True)


def _bmv_fwd(s, x):
    return _bmv(s, x), (s, x)


def _bmv_bwd(res, g):
    s, x = res
    return g * x, jnp.sum(_r(s).astype(F32) * _r(g).astype(F32), -2, keepdims=True)


_bmv.defvjp(_bmv_fwd, _bmv_bwd)


def _rwkv_steps(prm, cst, ins, s, steps):
    r, v = ins[0], ins[2]
    lw, kk, k2, m = _rwkv_pre(prm, ins)
    w = jnp.exp(lw)
    v_t = _t(v)
    lane = lax.broadcasted_iota(jnp.int32, (1, 1, CH), 2)
    y_t = jnp.zeros((s.shape[0], DH, CH), F32)
    for t in range(steps):
        e_t = (lane == t).astype(F32)
        row = (slice(None), slice(t, t + 1))
        sa = _bmv(s, -kk[row])
        s = s * w[row] + sa * m[row] + jnp.sum(v_t * e_t, -1, keepdims=True) * k2[row]
        y_t = y_t + _bmv(s, r[row]) * e_t
    return _rwkv_post(prm, ins, _t(y_t), k2)[:, :steps], s


def _rwkv_first_chunk(prm, cst, ins, s):
    k = RWKV_EXACT_STEPS
    y_head, s = _rwkv_steps(prm, cst, ins, s, k)
    y_tail, s, _ = _rwkv_chunk(prm, cst, [x[..., k:, :] for x in ins], s)
    return jnp.concatenate([y_head, y_tail], axis=-2), s


def _time_block(t):
    return TIME_BLOCK if t % TIME_BLOCK == 0 else t


def _load_chunk(ref, i):
    nb = ref.shape[1]
    if ref.shape[0] == NH:
        return jnp.concatenate([ref[:, b, pl.ds(i, CH), :] for b in range(nb)], axis=0)
    return ref[0, :, pl.ds(i, CH), :]


def _mixer_fwd(chunk_fn, name, ins, prm, cst, nb, t, first_fn=None, side=None, n_kept=0):
    tb = _time_block(t)
    nt, ncb, nch = t // tb, tb // CH, nb * NH
    n_in, n_prm, n_cst = len(ins), len(prm), len(cst)
    n_main, n_side = n_in + n_prm + n_cst, len(side.operands) if side else 0

    def body(*refs):
        in_refs = refs[:n_in]
        prm_refs = refs[n_in:n_in + n_prm]
        cst_refs = refs[n_in + n_prm:n_main]
        side_in = refs[n_main:n_main + n_side]
        y_ref, ck_ref = refs[n_main + n_side:n_main + n_side + 2]
        side_out = refs[n_main + n_side + 2:n_main + 2 * n_side + 2]
        s_scr = refs[n_main + 2 * n_side + 2]
        sems = refs[n_main + 2 * n_side + 3:]
        step_t = pl.program_id(0)

        if side is not None:
            @pl.when(step_t == 0)
            def _():
                _comm_start(side.copies(side_in, side_out, sems))

        @pl.when(step_t == 0)
        def _():
            s_scr[...] = jnp.zeros_like(s_scr)

        def chunk(c, i, fn=chunk_fn):
            s = s_scr[...]
            y, s_next, *kept = fn([jnp.tile(r[...], (nb, 1, 1)) for r in prm_refs],
                                  [jnp.tile(r[...], (nb, 1, 1)) for r in cst_refs],
                                  [_load_chunk(r, i) for r in in_refs], s)
            kept += [jnp.zeros_like(s)] * (n_kept - len(kept))
            for e, a in enumerate([s] + kept):
                ck_ref[c, e] = a
            for b in range(nb):
                y_ref[:, b, pl.ds(i, CH), :] = y[b * NH:(b + 1) * NH].astype(BF16)
            s_scr[...] = s_next

        def step(c, carry):
            chunk(c, pl.multiple_of(c * CH, CH))
            return carry

        if first_fn is None:
            lax.fori_loop(0, ncb, step, 0)
        else:
            @pl.when(step_t == 0)
            def _():
                chunk(0, 0, first_fn)

            @pl.when(step_t != 0)
            def _():
                chunk(0, 0)

            lax.fori_loop(1, ncb, step, 0)

        if side is not None:
            @pl.when(step_t == nt - 1)
            def _():
                _comm_wait(side.copies(side_in, side_out, sems))

    hbm = pl.BlockSpec(memory_space=pl.ANY)
    in_specs = [pl.BlockSpec((ng, nb, tb, DH), (lambda j, bi=bi: (bi, 0, j, 0))) for _, ng, bi in ins]
    in_specs += [pl.BlockSpec(p.shape, lambda j: (0, 0, 0)) for p in list(prm) + list(cst)]
    y, ck, *side_res = _pcall(
        body, name=name, grid=(nt,),
        in_specs=in_specs + [hbm] * n_side,
        out_specs=[pl.BlockSpec((NH, nb, tb, DH), lambda j: (0, 0, j, 0)),
                   pl.BlockSpec((ncb, 1 + n_kept, nch, DH, DH), lambda j: (j, 0, 0, 0, 0))] + [hbm] * n_side,
        out_shape=[jax.ShapeDtypeStruct((NH, nb, t, DH), BF16),
                   jax.ShapeDtypeStruct((t // CH, 1 + n_kept, nch, DH, DH), F32)]
        + (side.out_shapes if side else []),
        scratch_shapes=[pltpu.VMEM((nch, DH, DH), F32)] + (_comm_scratch(side) if side else []),
        compiler_params=_cparams(("arbitrary",)),
    )(*[a.reshape(a.shape[0], nb, t, DH) for a, _, _ in ins], *prm, *cst, *(side.operands if side else []))
    return y.reshape(NH, nb * t, DH), ck, side_res


def _mixer_bwd(chunk_fn, name, ins, prm, cst, ck, dy, dy_block, outs, routes, nb, t, first_fn=None, side=None):
    tb = _time_block(t)
    nt, ncb, nch = t // tb, tb // CH, nb * NH
    n_in, n_prm, n_cst, n_out = len(ins), len(prm), len(cst), len(outs)
    n_main, n_side = n_in + n_prm + n_cst + 2, len(side.operands) if side else 0
    n_kept = ck.shape[1] - 1

    def body(*refs):
        in_refs = refs[:n_in]
        prm_refs = refs[n_in:n_in + n_prm]
        cst_refs = refs[n_in + n_prm:n_in + n_prm + n_cst]
        ck_ref, dy_ref = refs[n_main - 2:n_main]
        side_in = refs[n_main:n_main + n_side]
        rest = refs[n_main + n_side:]
        out_refs = rest[:n_out]
        dprm_refs = rest[n_out:n_out + n_prm]
        side_out = rest[n_out + n_prm:n_out + n_prm + n_side]
        ds_scr = rest[n_out + n_prm + n_side]
        sems = rest[n_out + n_prm + n_side + 1:]
        step_t = pl.program_id(0)

        if side is not None:
            @pl.when(step_t == 0)
            def _():
                _comm_start(side.copies(side_in, side_out, sems))

        @pl.when(step_t == 0)
        def _():
            ds_scr[...] = jnp.zeros_like(ds_scr)
            for r in dprm_refs:
                r[...] = jnp.zeros_like(r)

        def chunk(c, i, fn=chunk_fn):
            cst_v = [jnp.tile(r[...], (nb, 1, 1)) for r in cst_refs]
            kept = [ck_ref[c, 1 + e] for e in range(n_kept)] if fn is chunk_fn else []
            _, vjp = jax.vjp(lambda p, x, s: fn(p, cst_v, x, s, *kept)[:2],
                             [jnp.tile(r[...], (nb, 1, 1)) for r in prm_refs],
                             [_load_chunk(r, i) for r in in_refs], ck_ref[c, 0])
            dy_c = jnp.concatenate([dy_ref[:, b, pl.ds(i, CH), :] for b in range(nb)], axis=0)
            d_prm, d_ins, d_s = vjp((dy_c, ds_scr[...]))
            for (oi, g0), r, g in zip(routes, in_refs, d_ins):
                o_ref = out_refs[oi]
                if r.shape[0] == NH:
                    for b in range(nb):
                        o_ref[g0:g0 + NH, b, pl.ds(i, CH), :] = g[b * NH:(b + 1) * NH].astype(o_ref.dtype)
                else:
                    o_ref[g0, :, pl.ds(i, CH), :] = g.astype(o_ref.dtype)
            for r, g in zip(dprm_refs, d_prm):
                r[...] += g
            ds_scr[...] = d_s

        def step(j, carry):
            c = ncb - 1 - j
            chunk(c, pl.multiple_of(c * CH, CH))
            return carry

        lax.fori_loop(0, ncb - 1, step, 0)
        if first_fn is None:
            chunk(0, 0)
        else:
            @pl.when(step_t == nt - 1)
            def _():
                chunk(0, 0, first_fn)

            @pl.when(step_t != nt - 1)
            def _():
                chunk(0, 0)

        if side is not None:
            @pl.when(step_t == nt - 1)
            def _():
                _comm_wait(side.copies(side_in, side_out, sems))

    def back(j):
        return nt - 1 - j

    hbm = pl.BlockSpec(memory_space=pl.ANY)
    in_specs = [pl.BlockSpec((ng, nb, tb, DH), (lambda j, bi=bi: (bi, 0, back(j), 0))) for _, ng, bi in ins]
    in_specs += [pl.BlockSpec(p.shape, lambda j: (0, 0, 0)) for p in list(prm) + list(cst)]
    in_specs += [pl.BlockSpec((ncb, 1 + n_kept, nch, DH, DH), lambda j: (back(j), 0, 0, 0, 0)),
                 pl.BlockSpec((NH, nb, tb, DH), lambda j: (dy_block, 0, back(j), 0))]
    out_specs = [pl.BlockSpec((ng, nb, tb, DH), lambda j: (0, 0, back(j), 0)) for ng, _ in outs]
    out_specs += [pl.BlockSpec((nch,) + p.shape[1:], lambda j: (0, 0, 0)) for p in prm]
    out_shape = [jax.ShapeDtypeStruct((ng, nb, t, DH), dt) for ng, dt in outs]
    out_shape += [jax.ShapeDtypeStruct((nch,) + p.shape[1:], F32) for p in prm]
    res = _pcall(
        body, name=name, grid=(nt,),
        in_specs=in_specs + [hbm] * n_side, out_specs=out_specs + [hbm] * n_side,
        out_shape=out_shape + (side.out_shapes if side else []),
        scratch_shapes=[pltpu.VMEM((nch, DH, DH), F32)] + (_comm_scratch(side) if side else []),
        compiler_params=_cparams(("arbitrary",)),
    )(*[a.reshape(a.shape[0], nb, t, DH) for a, _, _ in ins], *prm, *cst, ck, dy.reshape(dy.shape[0], nb, t, DH),
      *(side.operands if side else []))
    d_outs = [o.reshape(o.shape[0], nb * t, DH) for o in res[:n_out]]
    d_prm = [g.reshape((nb,) + p.shape) for g, p in zip(res[n_out:n_out + n_prm], prm)]
    return d_outs, d_prm, res[n_out + n_prm:]


def _shift_down(x, s):
    if s == 0:
        return x
    row = lax.broadcasted_iota(jnp.int32, x.shape, 0)
    return jnp.where(row < s, 0.0, pltpu.roll(x, s, 0))


def _shift_up(x, s):
    if s == 0:
        return x
    t = x.shape[0]
    row = lax.broadcasted_iota(jnp.int32, x.shape, 0)
    return jnp.where(row >= t - s, 0.0, pltpu.roll(x, t - s, 0))


def _conv_bwd(p, g0, ng, w, dy, nb, t, name):
    taps = w.shape[1]

    def body(x_ref, w_ref, dy_ref, dx_ref, dw_ref):
        x = x_ref[...]
        d = dy_ref[...]
        acc = w_ref[taps - 1:taps, :] * d
        dw_ref[taps - 1:taps, :] = jnp.sum(d * x, 0, keepdims=True)
        for i in range(taps - 1):
            s = taps - 1 - i
            acc = acc + w_ref[i:i + 1, :] * _shift_up(d, s)
            dw_ref[i:i + 1, :] = jnp.sum(d * _shift_down(x, s), 0, keepdims=True)
        dx_ref[...] = acc.astype(BF16)

    return _pcall(
        body, name=name, grid=(ng, nb),
        in_specs=[pl.BlockSpec((None, t, DH), lambda g, b: (g0 + g, b, 0)),
                  pl.BlockSpec((None, taps, DH), lambda g, b: (g, 0, 0)),
                  pl.BlockSpec((None, t, DH), lambda g, b: (g, b, 0))],
        out_specs=[pl.BlockSpec((None, t, DH), lambda g, b: (g, b, 0)),
                   pl.BlockSpec((None, None, taps, DH), lambda g, b: (g, b, 0, 0))],
        out_shape=[jax.ShapeDtypeStruct((ng, nb * t, DH), BF16),
                   jax.ShapeDtypeStruct((ng, nb, taps, DH), F32)],
        compiler_params=_cparams(("parallel", "parallel")),
    )(p, w, dy)


def _mix_group(g):
    return jnp.where(g < 16, G_RWKV + g, G_RWKV_WD + g - 16)


def _mix_bwd(p, mu, dy, nb, t, name):
    def body(x_ref, mu_ref, dy_ref, dx_ref, dmu_ref):
        x = x_ref[...]
        muv = mu_ref[...]
        d = dy_ref[...]
        dx_ref[...] = (d * (1.0 - muv) + _shift_up(d * muv, 1)).astype(BF16)
        dmu_ref[...] = jnp.sum(d * (_shift_down(x, 1) - x), 0, keepdims=True)

    return _pcall(
        body, name=name, grid=(18, nb),
        in_specs=[pl.BlockSpec((None, t, DH), lambda g, b: (_mix_group(g), b, 0)),
                  pl.BlockSpec((None, 1, DH), lambda g, b: (g, 0, 0)),
                  pl.BlockSpec((None, t, DH), lambda g, b: (g, b, 0))],
        out_specs=[pl.BlockSpec((None, t, DH), lambda g, b: (g, b, 0)),
                   pl.BlockSpec((None, None, 1, DH), lambda g, b: (g, b, 0, 0))],
        out_shape=[jax.ShapeDtypeStruct((18, nb * t, DH), BF16),
                   jax.ShapeDtypeStruct((18, nb, 1, DH), F32)],
        compiler_params=_cparams(("parallel", "parallel")),
    )(p, mu, dy)


def _sc_bwd(p, w, dy, nb, t, name):
    def body(p_ref, w_ref, dy_ref, dp_ref, dw_ref):
        bg, cg, xg, z = p_ref[0], p_ref[1], p_ref[2], p_ref[3]
        d = dy_ref[...]
        u = cg * xg
        u1 = _shift_down(u, 1)
        u2 = _shift_down(u, 2)
        conv = w_ref[2:3, :] * u + w_ref[1:2, :] * u1 + w_ref[0:1, :] * u2
        sg = jax.nn.sigmoid(z)
        sz = z * sg
        dp_ref[0] = (d * conv * sz).astype(BF16)
        dp_ref[3] = (d * bg * conv * (sg * (1.0 + z * (1.0 - sg)))).astype(BF16)
        dconv = d * bg * sz
        du = w_ref[2:3, :] * dconv + w_ref[1:2, :] * _shift_up(dconv, 1) + w_ref[0:1, :] * _shift_up(dconv, 2)
        dp_ref[1] = (du * xg).astype(BF16)
        dp_ref[2] = (du * cg).astype(BF16)
        dw_ref[2:3, :] = jnp.sum(dconv * u, 0, keepdims=True)
        dw_ref[1:2, :] = jnp.sum(dconv * u1, 0, keepdims=True)
        dw_ref[0:1, :] = jnp.sum(dconv * u2, 0, keepdims=True)

    return _pcall(
        body, name=name, grid=(NH, nb),
        in_specs=[pl.BlockSpec((4, t, DH), lambda j, b: (G_SC // 4 + j, b, 0)),
                  pl.BlockSpec((None, SC_TAPS, DH), lambda j, b: (j, 0, 0)),
                  pl.BlockSpec((None, t, DH), lambda j, b: (8 + j, b, 0))],
        out_specs=[pl.BlockSpec((4, t, DH), lambda j, b: (j, b, 0)),
                   pl.BlockSpec((None, None, SC_TAPS, DH), lambda j, b: (j, b, 0, 0))],
        out_shape=[jax.ShapeDtypeStruct((4 * NH, nb * t, DH), BF16),
                   jax.ShapeDtypeStruct((NH, nb, SC_TAPS, DH), F32)],
        compiler_params=_cparams(("parallel", "parallel")),
    )(p, w, dy)


def _row_tile(n):
    return 1024 if n % 1024 == 0 else n


def _regroup_in(w_all, name):
    tr = 256
    gs = GROUPS_PER_STEP

    def body(w_ref, o_ref):
        for g in _PADDED_GROUPS:
            o_ref[g // gs, :, DH * (g % gs):DH * (g % gs + 1)] = jnp.zeros((tr, DH), BF16)
        for g, a, d, off, ln in _SEGMENTS:
            lane = DH * (g % gs) + a
            o_ref[g // gs, :, lane:lane + ln] = w_ref[d, :, off:off + ln].astype(BF16)

    return _pcall(
        body, name=name, grid=(D_MODEL // tr,),
        in_specs=[pl.BlockSpec((N_DEV, tr, SHARD_COLS), lambda i: (0, i, 0))],
        out_specs=pl.BlockSpec((N_GROUPS // gs, tr, gs * DH), lambda i: (0, i, 0)),
        out_shape=jax.ShapeDtypeStruct((N_GROUPS // gs, D_MODEL, gs * DH), BF16),
        compiler_params=_cparams(("parallel",)),
    )(w_all)


def _regroup_out(dwg, name):
    tr = 256
    gs = GROUPS_PER_STEP

    def body(g_ref, o_ref):
        for g, a, d, off, ln in _SEGMENTS:
            lane = DH * (g % gs) + a
            o_ref[d, :, off:off + ln] = g_ref[g // gs, :, lane:lane + ln].astype(BF16)

    return _pcall(
        body, name=name, grid=(D_MODEL // tr,),
        in_specs=[pl.BlockSpec((N_GROUPS // gs, tr, gs * DH), lambda i: (0, i, 0))],
        out_specs=pl.BlockSpec((N_DEV, tr, SHARD_COLS), lambda i: (0, i, 0)),
        out_shape=jax.ShapeDtypeStruct((N_DEV, D_MODEL, SHARD_COLS), BF16),
        compiler_params=_cparams(("parallel",)),
    )(dwg)


CONV_BLOCKS = (0, 1, 2)
MIX_BLOCKS = (4, 5, 6, 7, 16)
SHIFTED_BLOCKS = {0: 0, 1: 1, 2: 2, 4: 3, 5: 4, 6: 5, 7: 6, 16: 7}
G_PM = 12
G_PM_WD, G_PM_AD = 29, 30
SC_BLOCK0 = G_SC // GROUPS_PER_STEP


def _norm_proj(x, pre_w, w_g, conv_w, mix_mu, sc_w, t, name):
    n = x.shape[0]
    tm = _row_tile(n)
    gs = GROUPS_PER_STEP
    nj = N_GROUPS // gs
    assert t % tm == 0, (t, tm)
    per_seq = t // tm

    def shifted_block(j):
        out = jnp.int32(len(SHIFTED_BLOCKS) - 1)
        for jj in sorted(SHIFTED_BLOCKS, reverse=True):
            out = jnp.where(j < jj, max(SHIFTED_BLOCKS[jj] - 1, 0), out)
        for jj, b in SHIFTED_BLOCKS.items():
            out = jnp.where(j == jj, b, out)
        return out

    def body(x_ref, pw_ref, w_ref, cw_ref, mu_ref, scw_ref, h_ref, p_ref, s_ref, ysc_ref, carry):
        i, j = pl.program_id(0), pl.program_id(1)

        @pl.when(j == 0)
        def _():
            xv = x_ref[...]
            h = xv * lax.rsqrt(jnp.mean(xv * xv, -1, keepdims=True) + EPS) * pw_ref[...]
            h_ref[...] = h.astype(BF16)

        r = jnp.dot(h_ref[...], w_ref[...], preferred_element_type=F32)
        for k in range(gs):
            p_ref[k] = r[:, DH * k:DH * (k + 1)]

        def shifts():
            first = (i % per_seq) == 0
            tail = jnp.where(first, 0.0, carry[j])
            above = jnp.concatenate([tail, jnp.zeros((tm - 8, gs * DH), F32)], axis=0)
            row = lax.broadcasted_iota(jnp.int32, r.shape, 0)
            out = [jnp.where(row < s, pltpu.roll(above, tm - 8 + s, 0), pltpu.roll(r, s, 0)) for s in (1, 2, 3)]
            carry[j] = r[tm - 8:, :]
            return out

        def store(v):
            for k in range(gs):
                s_ref[k] = v[:, DH * k:DH * (k + 1)]

        @pl.when(functools.reduce(jnp.logical_or, [j == b for b in CONV_BLOCKS]))
        def _():
            s1, s2, s3 = shifts()
            store(cw_ref[3:4, :] * r + cw_ref[0:1, :] * s3 + cw_ref[1:2, :] * s2 + cw_ref[2:3, :] * s1)

        @pl.when(functools.reduce(jnp.logical_or, [j == b for b in MIX_BLOCKS]))
        def _():
            s1 = shifts()[0]
            store(r + mu_ref[...] * (s1 - r))

        @pl.when((j >= SC_BLOCK0) & (j < SC_BLOCK0 + NH))
        def _():
            first = (i % per_seq) == 0
            tail = jnp.where(first, 0.0, carry[j])
            u = r[:, DH:2 * DH] * r[:, 2 * DH:3 * DH]
            above = jnp.concatenate([tail[:, DH:2 * DH] * tail[:, 2 * DH:3 * DH], jnp.zeros((tm - 8, DH), F32)], axis=0)
            row = lax.broadcasted_iota(jnp.int32, u.shape, 0)
            u1, u2 = [jnp.where(row < s, pltpu.roll(above, tm - 8 + s, 0), pltpu.roll(u, s, 0)) for s in (1, 2)]
            conv = scw_ref[2:3, :] * u + scw_ref[1:2, :] * u1 + scw_ref[0:1, :] * u2
            ysc_ref[...] = (r[:, :DH] * conv * _silu(r[:, 3 * DH:])).astype(BF16)
            carry[j] = r[tm - 8:, :]

    n_shifted = len(SHIFTED_BLOCKS) * gs
    return _pcall(
        body, name=name, grid=(n // tm, nj),
        in_specs=[pl.BlockSpec((tm, D_MODEL), lambda i, j: (i, 0)),
                  pl.BlockSpec((1, D_MODEL), lambda i, j: (0, 0)),
                  pl.BlockSpec((None, D_MODEL, gs * DH), lambda i, j: (j, 0, 0)),
                  pl.BlockSpec((None, GDN_TAPS, gs * DH), lambda i, j: (j, 0, 0)),
                  pl.BlockSpec((None, 1, gs * DH), lambda i, j: (j, 0, 0)),
                  pl.BlockSpec((None, SC_TAPS, DH), lambda i, j: (jnp.clip(j - SC_BLOCK0, 0, NH - 1), 0, 0))],
        out_specs=[pl.BlockSpec((tm, D_MODEL), lambda i, j: (i, 0)),
                   pl.BlockSpec((gs, tm, DH), lambda i, j: (j, i, 0)),
                   pl.BlockSpec((gs, tm, DH), lambda i, j: (shifted_block(j), i, 0)),
                   pl.BlockSpec((None, tm, DH), lambda i, j: (jnp.clip(j - SC_BLOCK0, 0, NH - 1), i, 0))],
        out_shape=[jax.ShapeDtypeStruct((n, D_MODEL), BF16),
                   jax.ShapeDtypeStruct((N_GROUPS, n, DH), F32),
                   jax.ShapeDtypeStruct((n_shifted, n, DH), F32),
                   jax.ShapeDtypeStruct((NH, n, DH), BF16)],
        scratch_shapes=[pltpu.VMEM((nj, 8, gs * DH), F32)],
        compiler_params=_cparams(("arbitrary", "arbitrary")),
    )(x, pre_w, w_g, conv_w, mix_mu, sc_w)


def _out_proj_norm(ys, wout_g, x, post_w, name, tgt=None):
    n = x.shape[0]
    tm = _row_tile(n)

    def body(y0, y1, y2, y3, w_ref, x_ref, pw_ref, *rest):
        y = jnp.concatenate([yr[h] for yr in (y0, y1, y2, y3) for h in range(NH)], axis=-1)
        acc = jnp.dot(y, w_ref[...], preferred_element_type=F32)
        xn = x_ref[...] + acc * lax.rsqrt(jnp.mean(acc * acc, -1, keepdims=True) + EPS) * pw_ref[...]
        if tgt is None:
            out_ref, xn_ref = rest
            xn_ref[...] = xn
        else:
            t_ref, out_ref, dx_ref, l_ref = rest

            @pl.when(pl.program_id(0) == 0)
            def _():
                l_ref[...] = jnp.zeros_like(l_ref)

            e = xn - t_ref[...]
            dx_ref[...] = e * (1.0 / D_MODEL)
            l_ref[...] += jnp.sum(jnp.sum(e * e, -1, keepdims=True), 0, keepdims=True) * (0.5 / D_MODEL)
        out_ref[...] = acc

    yspec = pl.BlockSpec((NH, tm, DH), lambda i: (0, i, 0))
    rows = pl.BlockSpec((tm, D_MODEL), lambda i: (i, 0))
    full = jax.ShapeDtypeStruct((n, D_MODEL), F32)
    head = tgt is not None
    return _pcall(
        body, name=name, grid=(n // tm,),
        in_specs=[yspec] * 4 + [pl.BlockSpec((D_MODEL, D_MODEL), lambda i: (0, 0)), rows,
                                pl.BlockSpec((1, D_MODEL), lambda i: (0, 0))] + [rows] * head,
        out_specs=[rows, rows] + [pl.BlockSpec((1, 128), lambda i: (0, 0))] * head,
        out_shape=[full, full] + [jax.ShapeDtypeStruct((1, 128), F32)] * head,
        compiler_params=_cparams(("arbitrary",) if head else ("parallel",)),
    )(*ys, wout_g.reshape(D_MODEL, D_MODEL), x, post_w, *([tgt] if head else []))


def _rmsnorm_bwd(xv, w, d):
    r = lax.rsqrt(jnp.mean(xv * xv, -1, keepdims=True) + EPS)
    xh = xv * r
    dxh = d * w
    dx = r * (dxh - xh * jnp.mean(dxh * xh, -1, keepdims=True))
    return dx, d * xh


def _post_bwd(dxn, out, post_w, wout_g, name):
    n = dxn.shape[0]
    tm = _row_tile(n)

    def body(d_ref, o_ref, pw_ref, w_ref, do_ref, dy_ref, dpw_ref):
        @pl.when(pl.program_id(0) == 0)
        def _():
            dpw_ref[...] = jnp.zeros_like(dpw_ref)

        dout, dw_rows = _rmsnorm_bwd(o_ref[...], pw_ref[...], d_ref[...])
        dpw_ref[...] += jnp.sum(dw_rows, 0, keepdims=True)
        db = dout.astype(BF16)
        do_ref[...] = db
        dy = lax.dot_general(db, w_ref[...], (((1,), (1,)), ((), ())), preferred_element_type=F32)
        for g in range(4 * NH):
            dy_ref[g] = dy[:, DH * g:DH * (g + 1)]

    rows = pl.BlockSpec((tm, D_MODEL), lambda i: (i, 0))
    vec = pl.BlockSpec((1, D_MODEL), lambda i: (0, 0))
    return _pcall(
        body, name=name, grid=(n // tm,),
        in_specs=[rows, rows, vec, pl.BlockSpec((D_MODEL, D_MODEL), lambda i: (0, 0))],
        out_specs=[rows, pl.BlockSpec((4 * NH, tm, DH), lambda i: (0, i, 0)), vec],
        out_shape=[jax.ShapeDtypeStruct((n, D_MODEL), BF16),
                   jax.ShapeDtypeStruct((4 * NH, n, DH), F32),
                   jax.ShapeDtypeStruct((1, D_MODEL), F32)],
        compiler_params=_cparams(("arbitrary",)),
    )(dxn, out, post_w, wout_g.reshape(D_MODEL, D_MODEL))


def _dwout(ys, dout, name):
    n = dout.shape[0]
    tm = _row_tile(n)

    def body(y0, y1, y2, y3, d_ref, dw_ref):
        @pl.when(pl.program_id(0) == 0)
        def _():
            dw_ref[...] = jnp.zeros_like(dw_ref)

        y = jnp.concatenate([yr[h] for yr in (y0, y1, y2, y3) for h in range(NH)], axis=-1)
        dw_ref[...] += lax.dot_general(y, d_ref[...], (((0,), (0,)), ((), ())), preferred_element_type=F32)

    yspec = pl.BlockSpec((NH, tm, DH), lambda i: (0, i, 0))
    return _pcall(
        body, name=name, grid=(n // tm,),
        in_specs=[yspec] * 4 + [pl.BlockSpec((tm, D_MODEL), lambda i: (i, 0))],
        out_specs=pl.BlockSpec((D_MODEL, D_MODEL), lambda i: (0, 0)),
        out_shape=jax.ShapeDtypeStruct((D_MODEL, D_MODEL), F32),
        compiler_params=_cparams(("arbitrary",)),
    )(*ys, dout)


def _source_specs(sources, rows_first):
    gs = GROUPS_PER_STEP
    spans, specs, j0 = [], [], 0
    for a in sources:
        nblk = a.shape[0] // gs
        spans.append((j0, j0 + nblk))
        shape = (gs, _row_tile(a.shape[1]), DH)

        def blk(j, j0=j0, nblk=nblk):
            return jnp.clip(j - j0, 0, nblk - 1)

        if rows_first:
            specs.append(pl.BlockSpec(shape, (lambda i, j, blk=blk: (blk(j), i, 0))))
        else:
            specs.append(pl.BlockSpec(shape, (lambda j, i, blk=blk: (blk(j), i, 0))))
        j0 += nblk
    return spans, specs


def _dh_prenorm_bwd(sources, w_g, x, pre_w, dxn, name, side=None):
    n = x.shape[0]
    tm = _row_tile(n)
    gs = GROUPS_PER_STEP
    nj = N_GROUPS // gs
    ni = n // tm
    spans, src_specs = _source_specs(sources, True)
    ns = len(sources)
    n_side = len(side.operands) if side else 0

    def body(*refs):
        src = refs[:ns]
        w_ref, x_ref, pw_ref, d_ref = refs[ns:ns + 4]
        side_in = refs[ns + 4:ns + 4 + n_side]
        dx_ref, dpw_ref = refs[ns + 4 + n_side:ns + 6 + n_side]
        side_out = refs[ns + 6 + n_side:ns + 6 + 2 * n_side]
        acc = refs[ns + 6 + 2 * n_side]
        sems = refs[ns + 7 + 2 * n_side:]
        i, j = pl.program_id(0), pl.program_id(1)

        if side is not None:
            @pl.when((i == 0) & (j == 0))
            def _():
                _comm_start(side.copies(side_in, side_out, sems))

        @pl.when((i == 0) & (j == 0))
        def _():
            dpw_ref[...] = jnp.zeros_like(dpw_ref)

        @pl.when(j == 0)
        def _():
            acc[...] = jnp.zeros_like(acc)

        for s_ref, (lo, hi) in zip(src, spans):
            @pl.when((j >= lo) & (j < hi))
            def _(s_ref=s_ref):
                four = jnp.concatenate([s_ref[k] for k in range(gs)], axis=-1)
                acc[...] += lax.dot_general(four, w_ref[...], (((1,), (1,)), ((), ())), preferred_element_type=F32)

        @pl.when(j == nj - 1)
        def _():
            dx, dw_rows = _rmsnorm_bwd(x_ref[...], pw_ref[...], acc[...])
            dx_ref[...] = d_ref[...] + dx
            dpw_ref[...] += jnp.sum(dw_rows, 0, keepdims=True)

        if side is not None:
            @pl.when((i == ni - 1) & (j == nj - 1))
            def _():
                _comm_wait(side.copies(side_in, side_out, sems))

    hbm = pl.BlockSpec(memory_space=pl.ANY)
    rows = pl.BlockSpec((tm, D_MODEL), lambda i, j: (i, 0))
    vec = pl.BlockSpec((1, D_MODEL), lambda i, j: (0, 0))
    dx, dpw, *side_res = _pcall(
        body, name=name, grid=(ni, nj),
        in_specs=src_specs + [pl.BlockSpec((None, D_MODEL, gs * DH), lambda i, j: (j, 0, 0)), rows, vec, rows]
        + [hbm] * n_side,
        out_specs=[rows, vec] + [hbm] * n_side,
        out_shape=[jax.ShapeDtypeStruct((n, D_MODEL), F32), jax.ShapeDtypeStruct((1, D_MODEL), F32)]
        + (side.out_shapes if side else []),
        scratch_shapes=[pltpu.VMEM((tm, D_MODEL), F32)] + (_comm_scratch(side) if side else []),
        compiler_params=_cparams(("arbitrary", "arbitrary")),
    )(*sources, w_g, x, pre_w, dxn, *(side.operands if side else []))
    return dx, dpw, side_res


def _dwin(hb, sources, name):
    n = hb.shape[0]
    tm = _row_tile(n)
    gs = GROUPS_PER_STEP
    ni, nj = n // tm, N_GROUPS // gs
    spans, src_specs = _source_specs(sources, True)
    ns = len(sources)

    def body(*refs):
        h_ref = refs[0]
        src = refs[1:1 + ns]
        out_ref, acc, sem = refs[1 + ns:]
        i, j = pl.program_id(0), pl.program_id(1)

        @pl.when((i == 0) & (j == 0))
        def _():
            acc[...] = jnp.zeros_like(acc)

        h = h_ref[...]
        for s_ref, (lo, hi) in zip(src, spans):
            @pl.when((j >= lo) & (j < hi))
            def _(s_ref=s_ref):
                four = jnp.concatenate([s_ref[k] for k in range(gs)], axis=-1)
                acc[j] += jnp.dot(h, four, preferred_element_type=F32)

        @pl.when((i == ni - 1) & (j == nj - 1))
        def _():
            done = pltpu.make_async_copy(acc, out_ref, sem)
            done.start()
            done.wait()

    return _pcall(
        body, name=name, grid=(ni, nj),
        in_specs=[pl.BlockSpec((D_MODEL, tm), lambda i, j: (0, i))] + src_specs,
        out_specs=pl.BlockSpec(memory_space=pl.ANY),
        out_shape=jax.ShapeDtypeStruct((nj, D_MODEL, gs * DH), F32),
        scratch_shapes=[pltpu.VMEM((nj, D_MODEL, gs * DH), F32), pltpu.SemaphoreType.DMA],
        compiler_params=_cparams(("arbitrary", "arbitrary")),
    )(jnp.transpose(hb), *sources)


def _adamw_math(w, g, m, v):
    c1 = 1.0 - ADAM_B1 ** ADAM_STEP
    c2 = 1.0 - ADAM_B2 ** ADAM_STEP
    nm = ADAM_B1 * m + (1.0 - ADAM_B1) * g
    nv = ADAM_B2 * v + (1.0 - ADAM_B2) * (g * g)
    return -ADAM_LR * ((nm / c1) / (jnp.sqrt(nv / c2) + ADAM_EPS) + ADAM_WD * w), nm, nv


def _adamw(w, g, m, v, name):
    r, c = w.shape
    tr = 256 if r % 256 == 0 else r

    def body(w_ref, g_ref, m_ref, v_ref, d_ref, nm_ref, nv_ref):
        d_ref[...], nm_ref[...], nv_ref[...] = _adamw_math(w_ref[...], g_ref[...], m_ref[...], v_ref[...])

    spec = pl.BlockSpec((tr, c), lambda i: (i, 0))
    return _pcall(
        body, name=name, grid=(r // tr,),
        in_specs=[spec] * 4, out_specs=[spec] * 3,
        out_shape=[jax.ShapeDtypeStruct((r, c), F32)] * 3,
        compiler_params=_cparams(("parallel",)),
    )(w, g, m, v)


def _sum_adamw(parts, w, m, v, name):
    r, c = w.shape
    tr = 128 if r % 128 == 0 else r

    def body(p_ref, w_ref, m_ref, v_ref, g_ref, d_ref, nm_ref, nv_ref):
        g = p_ref[0].astype(F32)
        for k in range(1, N_DEV):
            g = g + p_ref[k].astype(F32)
        g_ref[...] = g
        d_ref[...], nm_ref[...], nv_ref[...] = _adamw_math(w_ref[...], g, m_ref[...], v_ref[...])

    spec = pl.BlockSpec((tr, c), lambda i: (i, 0))
    return _pcall(
        body, name=name, grid=(r // tr,),
        in_specs=[pl.BlockSpec((N_DEV, tr, c), lambda i: (0, i, 0))] + [spec] * 3, out_specs=[spec] * 4,
        out_shape=[jax.ShapeDtypeStruct((r, c), F32)] * 4,
        compiler_params=_cparams(("parallel",)),
    )(parts, w, m, v)


def _me():
    return lax.axis_index("x"), lax.axis_index("y"), lax.axis_index("c")


def _flat(x, y, c):
    return 4 * x + 2 * y + c


def _peer(k):
    x, y, c = _me()
    return (x ^ ((k >> 2) & 1), y ^ ((k >> 1) & 1), c ^ (k & 1))


def _gather_plan(blocks):
    def copies(x_refs, out_refs, sems):
        send_sems, recv_sems, local_sems = sems
        me = _flat(*_me())
        local = [pltpu.make_async_copy(x, o.at[me], local_sems.at[a]) for a, (x, o) in enumerate(zip(x_refs, out_refs))]
        outgoing, incoming = [], []
        for k in range(1, N_DEV):
            src = _flat(*_peer(k))
            for a, (x, o) in enumerate(zip(x_refs, out_refs)):
                for slot, group in ((me, outgoing), (src, incoming)):
                    group.append(pltpu.make_async_remote_copy(
                        src_ref=x, dst_ref=o.at[slot], send_sem=send_sems.at[a, k - 1], recv_sem=recv_sems.at[a, k - 1],
                        device_id=_peer(k), device_id_type=MESH))
        return local, outgoing, incoming

    return _Comm(list(blocks), [jax.ShapeDtypeStruct((N_DEV,) + b.shape, b.dtype) for b in blocks], copies)


def _exchange_plan(sends):
    def copies(s_refs, out_refs, sems):
        send_sems, recv_sems, local_sems = sems
        me = _flat(*_me())
        local = [pltpu.make_async_copy(s.at[me], o.at[0], local_sems.at[i]) for i, (s, o) in enumerate(zip(s_refs, out_refs))]
        outgoing = []
        for k in range(1, N_DEV):
            to = _flat(*_peer(k))
            for i, (s, o) in enumerate(zip(s_refs, out_refs)):
                outgoing.append(pltpu.make_async_remote_copy(
                    src_ref=s.at[to], dst_ref=o.at[k], send_sem=send_sems.at[i, k - 1], recv_sem=recv_sems.at[i, k - 1],
                    device_id=_peer(k), device_id_type=MESH))
        return local, outgoing, outgoing

    return _Comm(list(sends), [jax.ShapeDtypeStruct(s.shape, s.dtype) for s in sends], copies)


def _comm_scratch(plan):
    n = len(plan.operands)
    return [pltpu.SemaphoreType.DMA((n, N_DEV - 1)), pltpu.SemaphoreType.DMA((n, N_DEV - 1)),
            pltpu.SemaphoreType.DMA((n,))]


def _comm_start(copies):
    local, outgoing, _ = copies
    for cp in local + outgoing:
        cp.start()


def _comm_wait(copies):
    local, outgoing, incoming = copies
    for cp in incoming:
        cp.wait_recv()
    for cp in outgoing:
        cp.wait_send()
    for cp in local:
        cp.wait()


def _run_comm(plan, name):
    n = len(plan.operands)

    def body(*refs):
        copies = plan.copies(refs[:n], refs[n:2 * n], refs[2 * n:])
        _comm_start(copies)
        _comm_wait(copies)

    return _pcall(
        body, name=name,
        in_specs=[pl.BlockSpec(memory_space=pl.ANY)] * n,
        out_specs=[pl.BlockSpec(memory_space=pl.ANY)] * n,
        out_shape=plan.out_shapes,
        scratch_shapes=_comm_scratch(plan),
    )(*plan.operands)


def _all_gather_two_level(blocks, name):
    na = len(blocks)

    def body(*refs):
        x_refs, out_refs = refs[:na], refs[na:2 * na]
        send_sems, recv_sems, local_sems = refs[2 * na:]
        x, y, c = _me()
        me, sibling = (x, y, c), (x, y, 1 - c)
        chips = [(1 - x, y), (x, 1 - y), (1 - x, 1 - y)]

        def copy(a, k, block, to, own=False):
            slot = out_refs[a].at[_flat(*block)]
            return pltpu.make_async_remote_copy(
                src_ref=x_refs[a] if own else slot, dst_ref=slot, send_sem=send_sems.at[a, k],
                recv_sem=recv_sems.at[a, k], device_id=to, device_id_type=MESH)

        mine = [pltpu.make_async_copy(x_refs[a], out_refs[a].at[_flat(*me)], local_sems.at[a]) for a in range(na)]
        first = [copy(a, 0, me, sibling, own=True) for a in range(na)]
        first += [copy(a, 1 + j, me, (*chip, c), own=True) for j, chip in enumerate(chips) for a in range(na)]
        for cp in mine + first:
            cp.start()
        passed = []
        for j, chip in enumerate(chips):
            for a in range(na):
                copy(a, 1 + j, (*chip, c), me).wait_recv()
                cp = copy(a, 4 + j, (*chip, c), sibling)
                cp.start()
                passed.append(cp)
        for a in range(na):
            copy(a, 0, sibling, me).wait_recv()
        for j, chip in enumerate(chips):
            for a in range(na):
                copy(a, 4 + j, (*chip, 1 - c), me).wait_recv()
        for cp in first + passed:
            cp.wait_send()
        for cp in mine:
            cp.wait()

    return _pcall(
        body, name=name,
        in_specs=[pl.BlockSpec(memory_space=pl.ANY)] * na,
        out_specs=[pl.BlockSpec(memory_space=pl.ANY)] * na,
        out_shape=[jax.ShapeDtypeStruct((N_DEV,) + b.shape, b.dtype) for b in blocks],
        scratch_shapes=[pltpu.SemaphoreType.DMA((na, N_DEV - 1)), pltpu.SemaphoreType.DMA((na, N_DEV - 1)),
                        pltpu.SemaphoreType.DMA((na,))],
    )(*blocks)


def _sum_slots(a, name):
    r = a.shape[1]

    def body(a_ref, o_ref):
        acc = a_ref[0]
        for d in range(1, N_DEV):
            acc = acc + a_ref[d]
        o_ref[...] = acc

    return _pcall(body, name=name, out_shape=jax.ShapeDtypeStruct((r, 128), F32), compiler_params=_cparams())(a)


def _all_reduce_small(blk, name):
    r = blk.shape[0]

    def body(x_ref, out_ref, gath, send_sems, recv_sems):
        me = _flat(*_me())
        gath[me] = x_ref[...]
        copies = []
        for k in range(1, N_DEV):
            cp = pltpu.make_async_remote_copy(
                src_ref=x_ref, dst_ref=gath.at[me],
                send_sem=send_sems.at[k - 1], recv_sem=recv_sems.at[k - 1],
                device_id=_peer(k), device_id_type=MESH)
            cp.start()
            copies.append(cp)
        for k in range(1, N_DEV):
            src = _flat(*_peer(k))
            pltpu.make_async_remote_copy(
                src_ref=x_ref, dst_ref=gath.at[src],
                send_sem=send_sems.at[k - 1], recv_sem=recv_sems.at[k - 1],
                device_id=_peer(k), device_id_type=MESH).wait_recv()
        for cp in copies:
            cp.wait_send()
        acc = gath[0]
        for d in range(1, N_DEV):
            acc = acc + gath[d]
        out_ref[...] = acc

    return _pcall(
        body, name=name,
        in_specs=[pl.BlockSpec(memory_space=pltpu.VMEM)],
        out_specs=pl.BlockSpec(memory_space=pltpu.VMEM),
        out_shape=jax.ShapeDtypeStruct((r, 128), F32),
        scratch_shapes=[pltpu.VMEM((N_DEV, r, 128), F32),
                        pltpu.SemaphoreType.DMA((N_DEV - 1,)), pltpu.SemaphoreType.DMA((N_DEV - 1,))],
    )(blk)


def _heads(vec):
    return vec.reshape(NH, 1, DH)


def _rep(vec4):
    return jnp.broadcast_to(vec4.reshape(NH, 1, 1), (NH, 1, DH))


def _onehot_lane(offset):
    m = np.zeros((NH, 1, DH), np.float32)
    for h in range(NH):
        m[h, 0, offset + h] = 1.0
    return jnp.asarray(m)


_TINY = (("gdn_conv_w", (DEPTH, 4, 96)), ("rwkv_w_up", (DEPTH, 64, 32)), ("rwkv_a_up", (DEPTH, 64, 32)),
         ("sc_conv_w", (DEPTH, 3, 32)))
_TINY_ROWS = -(-sum(int(np.prod(s)) for _, s in _TINY) // 1024) * 8


def _pack_rows(arrays, rows, fill=0.0):
    flat = jnp.concatenate([a.reshape(-1) for a in arrays])
    return jnp.pad(flat, (0, rows * 128 - flat.shape[0]), constant_values=fill).reshape(rows, 128)


def _unpack_rows(p, named_shapes):
    lead = p.shape[:-2]
    flat = p.reshape(lead + (-1,))
    out, o = {}, 0
    for n, s in named_shapes:
        size = int(np.prod(s))
        out[n] = flat[..., o:o + size].reshape(lead + tuple(s))
        o += size
    return out


def _gather_last(a):
    return jnp.transpose(a, (1, 0, 2)).reshape(a.shape[1], -1)


def _split_last(a):
    r, c8 = a.shape
    return jnp.transpose(a.reshape(r, N_DEV, c8 // N_DEV), (1, 0, 2))


_SMALL = (("pre_norm_w", (DEPTH, 1024)), ("gdn_a_log", (DEPTH, 4)), ("gdn_dt_bias", (DEPTH, 4)),
          ("gdn_norm_w", (DEPTH, 64)), ("rwkv_mu", (DEPTH, 1152)), ("rwkv_w0", (DEPTH, 256)),
          ("rwkv_a0", (DEPTH, 256)), ("rwkv_k_k", (DEPTH, 256)), ("rwkv_k_a", (DEPTH, 256)),
          ("rwkv_r_k", (DEPTH, 256)), ("rwkv_ln_w", (DEPTH, 256)), ("rwkv_ln_b", (DEPTH, 256)),
          ("gla_a_up", (DEPTH, 16, 128)), ("gla_a_bias", (DEPTH, 128)), ("gla_norm_w", (DEPTH, 64)),
          ("post_norm_w", (DEPTH, 1024)), ("loss", ()))
_SMALL_ROWS = -(-sum(int(np.prod(s)) for _, s in _SMALL) // 1024) * 8


def _big_weights(w_in_all, w_out_all, l):
    return dict(w_g=_regroup_in(w_in_all, f"regroup_in{l}"),
                wout_g=w_out_all.reshape(4 * NH, DH, D_MODEL).astype(BF16))


def _layer_params(wts, tiny, l):
    conv = _gather_last(tiny["gdn_conv_w"][:, l])
    q = {}
    q["gdn_conv"] = jnp.transpose(conv.reshape(GDN_TAPS, 12, DH), (1, 0, 2))
    q["gdn_prm"] = [_rep(wts["gdn_a_log"][l]), _rep(wts["gdn_dt_bias"][l]),
                    jnp.broadcast_to(wts["gdn_norm_w"][l].reshape(1, 1, DH), (NH, 1, DH))]
    q["gdn_cst"] = [_onehot_lane(0), _onehot_lane(NH)]
    q["rwkv_mu"] = wts["rwkv_mu"][l].reshape(18, 1, DH)
    gs, nj = GROUPS_PER_STEP, N_GROUPS // GROUPS_PER_STEP
    side_by_side = lambda a: jnp.transpose(a.reshape(-1, gs, a.shape[1], DH), (0, 2, 1, 3)).reshape(-1, a.shape[1], gs * DH)
    q["conv_blocks"] = jnp.pad(side_by_side(q["gdn_conv"]), ((0, nj - len(CONV_BLOCKS)), (0, 0), (0, 0)))
    singles = jnp.pad(q["rwkv_mu"][16:18].reshape(1, 1, 2 * DH), ((0, 0), (0, 0), (DH, DH)))
    q["mix_blocks"] = jnp.concatenate([jnp.zeros((4, 1, gs * DH), F32), side_by_side(q["rwkv_mu"][:16]),
                                       jnp.zeros((8, 1, gs * DH), F32), singles], axis=0)
    w_up = jnp.transpose(_gather_last(tiny["rwkv_w_up"][:, l]).reshape(64, NH, DH), (1, 0, 2))
    a_up = jnp.transpose(_gather_last(tiny["rwkv_a_up"][:, l]).reshape(64, NH, DH), (1, 0, 2))
    q["rwkv_prm"] = [_heads(wts["rwkv_w0"][l]), w_up, _heads(wts["rwkv_a0"][l]), a_up,
                     _heads(wts["rwkv_k_k"][l]), _heads(wts["rwkv_k_a"][l]), _heads(wts["rwkv_r_k"][l]),
                     _heads(wts["rwkv_ln_w"][l]), _heads(wts["rwkv_ln_b"][l])]
    sc = _gather_last(tiny["sc_conv_w"][:, l])
    q["sc_conv"] = jnp.transpose(sc.reshape(SC_TAPS, NH, DH), (1, 0, 2))
    gla_up = jnp.transpose(wts["gla_a_up"][l].reshape(16, NH, GLA_HEAD_K), (1, 0, 2))
    gla_up = jnp.pad(gla_up, ((0, 0), (0, DH - 16), (0, DH - GLA_HEAD_K)))
    gla_b = jnp.pad(wts["gla_a_bias"][l].reshape(NH, 1, GLA_HEAD_K), ((0, 0), (0, 0), (0, DH - GLA_HEAD_K)))
    q["gla_prm"] = [gla_up, gla_b, jnp.broadcast_to(wts["gla_norm_w"][l].reshape(1, 1, DH), (NH, 1, DH))]
    q["pre_w"] = wts["pre_norm_w"][l].reshape(1, D_MODEL)
    q["post_w"] = wts["post_norm_w"][l].reshape(1, D_MODEL)
    return q


def _mixer_inputs(p, ps):
    gdn = [(ps, 4, 0), (ps, 4, 1), (ps, 4, 2), (p, 4, G_GDN // 4 + 3), (p, 1, G_GDN_AB)]
    rwkv = [(ps, 4, G_PM // 4 + k) for k in range(4)] + [(ps, 1, G_PM_WD), (ps, 1, G_PM_AD)]
    gla = [(p, 4, G_GLA // 4 + k) for k in range(4)] + [(p, 1, G_GLA_AD)]
    return gdn, rwkv, gla


def _layer_fwd(x, q, nb, t, l, side=None, tgt=None):
    hb, p, ps, y_sc = _norm_proj(x, q["pre_w"], q["w_g"], q["conv_blocks"], q["mix_blocks"], q["sc_conv"], t,
                                 f"norm_proj{l}")
    gdn_in, rwkv_in, gla_in = _mixer_inputs(p, ps)
    y_gdn, ck_gdn, _ = _mixer_fwd(_gdn_chunk, f"gdn_fwd{l}", gdn_in, q["gdn_prm"], q["gdn_cst"], nb, t, n_kept=1)
    y_rwkv, ck_rwkv, side_res = _mixer_fwd(_rwkv_chunk, f"rwkv_fwd{l}", rwkv_in, q["rwkv_prm"], [], nb, t,
                                           first_fn=_rwkv_first_chunk, side=side, n_kept=1)
    y_gla, ck_gla, _ = _mixer_fwd(_gla_chunk, f"gla_fwd{l}", gla_in, q["gla_prm"], [], nb, t)
    ys = (y_gdn, y_rwkv, y_sc, y_gla)
    out, *res = _out_proj_norm(ys, q["wout_g"], x, q["post_w"], f"out_proj{l}", tgt)
    saved = dict(x=x, hb=hb, p=p, ps=ps, ys=ys, out=out, ck=(ck_gdn, ck_rwkv, ck_gla))
    return (res[0] if tgt is None else res), saved, side_res


def _layer_bwd(dxn, q, sv, nb, t, l, side=None, exchange_own=False):
    p, ys = sv["p"], sv["ys"]
    dout, dy, d_post = _post_bwd(dxn, sv["out"], q["post_w"], q["wout_g"], f"post_bwd{l}")
    d_wout = _dwout(ys, dout, f"dwout{l}").reshape(N_DEV, 128, D_MODEL).astype(BF16)
    gdn_in, rwkv_in, gla_in = _mixer_inputs(p, sv["ps"])
    ck_gdn, ck_rwkv, ck_gla = sv["ck"]
    g = {}

    (d_conv, dz, dab), (da_log, ddt, dnw), _ = _mixer_bwd(
        _gdn_chunk, f"gdn_bwd{l}", gdn_in, q["gdn_prm"], q["gdn_cst"], ck_gdn, dy, 0,
        [(12, F32), (4, BF16), (1, BF16)], [(0, 0), (0, 4), (0, 8), (1, 0), (2, 0)], nb, t)
    dconv_in, d_gconv = _conv_bwd(p, G_GDN, 12, q["gdn_conv"], d_conv, nb, t, f"gdn_conv_bwd{l}")
    g["gdn_conv_w"] = jnp.transpose(d_gconv.sum(1), (1, 0, 2)).reshape(GDN_TAPS, 768)
    g["gdn_a_log"] = da_log.sum((0, 2, 3))
    g["gdn_dt_bias"] = ddt.sum((0, 2, 3))
    g["gdn_norm_w"] = dnw.sum((0, 1, 2))

    (d_pm,), d_rprm, side_res = _mixer_bwd(
        _rwkv_chunk, f"rwkv_bwd{l}", rwkv_in, q["rwkv_prm"], [], ck_rwkv, dy, 1,
        [(18, F32)], [(0, 0), (0, 4), (0, 8), (0, 12), (0, 16), (0, 17)], nb, t, first_fn=_rwkv_first_chunk,
        side=side)
    dp_rwkv, d_mu = _mix_bwd(p, q["rwkv_mu"], d_pm, nb, t, f"rwkv_mix_bwd{l}")
    g["rwkv_mu"] = d_mu.sum(1).reshape(1152)
    rp = [a.sum(0) for a in d_rprm]
    g["rwkv_w0"] = rp[0].reshape(256)
    g["rwkv_w_up"] = jnp.transpose(rp[1], (1, 0, 2)).reshape(64, 256)
    g["rwkv_a0"] = rp[2].reshape(256)
    g["rwkv_a_up"] = jnp.transpose(rp[3], (1, 0, 2)).reshape(64, 256)
    for i, nme in enumerate(("rwkv_k_k", "rwkv_k_a", "rwkv_r_k", "rwkv_ln_w", "rwkv_ln_b")):
        g[nme] = rp[4 + i].reshape(256)

    dp_sc, d_scw = _sc_bwd(p, q["sc_conv"], dy, nb, t, f"sc_bwd{l}")
    g["sc_conv_w"] = jnp.transpose(d_scw.sum(1), (1, 0, 2)).reshape(SC_TAPS, 256)

    (dp_gla, dad), (d_aup, d_ab, d_gnw), _ = _mixer_bwd(
        _gla_chunk, f"gla_bwd{l}", gla_in, q["gla_prm"], [], ck_gla, dy, 3,
        [(16, BF16), (1, BF16)], [(0, 0), (0, 4), (0, 8), (0, 12), (1, 0)], nb, t)
    g["gla_a_up"] = jnp.transpose(d_aup.sum(0)[:, :16, :GLA_HEAD_K], (1, 0, 2)).reshape(16, 128)
    g["gla_a_bias"] = d_ab.sum(0)[:, 0, :GLA_HEAD_K].reshape(128)
    g["gla_norm_w"] = d_gnw.sum((0, 1, 2))

    singles = jnp.concatenate([dab, dp_rwkv[16:18], dad], axis=0)
    sources = [dconv_in, dz, dp_rwkv, dp_sc, dp_gla, singles]
    d_win = _regroup_out(_dwin(sv["hb"], sources, f"dwin{l}"), f"regroup_out{l}")
    own = _exchange_plan([d_win, d_wout]) if exchange_own else None
    dx, d_pre, got = _dh_prenorm_bwd(sources, q["w_g"], sv["x"], q["pre_w"], dxn, f"dh_bwd{l}", own)
    if exchange_own:
        d_win, d_wout = got
    g["pre_norm_w"] = d_pre.reshape(D_MODEL)
    g["post_norm_w"] = d_post.reshape(D_MODEL)
    return dx, g, d_win, d_wout, side_res


def _local_step(x, tgt, wts, tiny, w_in_all, w_out_all, later_shards=None):
    nb, t, d = x.shape
    xf = x.reshape(nb * t, d)
    overlap = later_shards is not None
    qs, saved = [], []
    big = _big_weights(w_in_all[0], w_out_all[0], 0)
    for l in range(DEPTH):
        q = dict(_layer_params(wts, tiny, l), **big)
        nxt = l + 1 < DEPTH
        side = _gather_plan(later_shards[l]) if overlap and nxt else None
        xf, sv, got = _layer_fwd(xf, q, nb, t, l, side, None if nxt else tgt.reshape(nb * t, d))
        if nxt:
            big = _big_weights(*(got if overlap else (w_in_all[l + 1], w_out_all[l + 1])), l + 1)
        qs.append(q)
        saved.append(sv)
    dxf, lpart = xf
    grads, d_win, d_wout = [None] * DEPTH, [None] * DEPTH, [None] * DEPTH
    for l in reversed(range(DEPTH)):
        side = _exchange_plan([d_win[l + 1], d_wout[l + 1]]) if overlap and l + 1 < DEPTH else None
        dxf, grads[l], d_win[l], d_wout[l], got = _layer_bwd(dxf, qs[l], saved[l], nb, t, l, side,
                                                             exchange_own=overlap and l == 0)
        if side is not None:
            d_win[l + 1], d_wout[l + 1] = got
    small = {k: jnp.stack([grads[l][k] for l in range(DEPTH)]) for k in grads[0]}
    return lpart[0, 0], dxf.reshape(nb, t, d), small, d_win, d_wout


_WEIGHTS = ("pre_norm_w", "w_in", "gdn_conv_w", "gdn_a_log", "gdn_dt_bias", "gdn_norm_w", "rwkv_mu", "rwkv_w0",
            "rwkv_w_up", "rwkv_a0", "rwkv_a_up", "rwkv_k_k", "rwkv_k_a", "rwkv_r_k", "rwkv_ln_w", "rwkv_ln_b",
            "sc_conv_w", "gla_a_up", "gla_a_bias", "gla_norm_w", "w_out", "post_norm_w")


def kernel(x, pre_norm_w, w_in, gdn_conv_w, gdn_a_log, gdn_dt_bias, gdn_norm_w, rwkv_mu, rwkv_w0, rwkv_w_up, rwkv_a0, rwkv_a_up, rwkv_k_k, rwkv_k_a, rwkv_r_k, rwkv_ln_w, rwkv_ln_b, sc_conv_w, gla_a_up, gla_a_bias, gla_norm_w, w_out, post_norm_w, loss_target, m_pre_norm_w, m_w_in, m_gdn_conv_w, m_gdn_a_log, m_gdn_dt_bias, m_gdn_norm_w, m_rwkv_mu, m_rwkv_w0, m_rwkv_w_up, m_rwkv_a0, m_rwkv_a_up, m_rwkv_k_k, m_rwkv_k_a, m_rwkv_r_k, m_rwkv_ln_w, m_rwkv_ln_b, m_sc_conv_w, m_gla_a_up, m_gla_a_bias, m_gla_norm_w, m_w_out, m_post_norm_w, v_pre_norm_w, v_w_in, v_gdn_conv_w, v_gdn_a_log, v_gdn_dt_bias, v_gdn_norm_w, v_rwkv_mu, v_rwkv_w0, v_rwkv_w_up, v_rwkv_a0, v_rwkv_a_up, v_rwkv_k_k, v_rwkv_k_a, v_rwkv_r_k, v_rwkv_ln_w, v_rwkv_ln_b, v_sc_conv_w, v_gla_a_up, v_gla_a_bias, v_gla_norm_w, v_w_out, v_post_norm_w):
    env = dict(locals())
    w = {n: env[n] for n in _WEIGHTS}
    m = {n: env["m_" + n] for n in _WEIGHTS}
    v = {n: env["v_" + n] for n in _WEIGHTS}
    tiny_names = [n for n, _ in _TINY]

    w_in_b, w_out_b = w_in.astype(BF16), w_out.astype(BF16)
    w_in_0, w_out_0, tiny_all = _all_gather_two_level(
        [w_in_b[0], w_out_b[0], _pack_rows([w[n] for n in tiny_names], _TINY_ROWS)], "gather_weights")
    tiny = _unpack_rows(tiny_all, _TINY)

    lpart, grad_x, small, r_win, r_wout = _local_step(
        x, loss_target, w, tiny, [w_in_0], [w_out_0], later_shards=[(w_in_b[l], w_out_b[l]) for l in range(1, DEPTH)])

    tiny_send = jnp.stack([_pack_rows([_split_last(small[n][l])[d] for n in tiny_names for l in range(DEPTH)],
                                      _TINY_ROWS) for d in range(N_DEV)])
    (r_tiny,) = _run_comm(_exchange_plan([tiny_send]), "scatter_grads")
    grads, delta, new_m, new_v = {}, {}, {}, {}
    for n, parts in (("w_in", r_win), ("w_out", r_wout)):
        res = [_sum_adamw(parts[l], w[n][l], m[n][l], v[n][l], f"adamw_{n}{l}") for l in range(DEPTH)]
        grads[n], delta[n], new_m[n], new_v[n] = [jnp.stack(o) for o in zip(*res)]
    tiny_sum = _sum_slots(r_tiny, "sum_tiny").reshape(-1)
    o = 0
    for n, s in _TINY:
        size = int(np.prod(s))
        grads[n] = tiny_sum[o:o + size].reshape(s)
        o += size

    small = dict(small)
    small["loss"] = lpart
    red = _unpack_rows(_all_reduce_small(_pack_rows([small[n] for n, _ in _SMALL], _SMALL_ROWS), "reduce_small"),
                       _SMALL)
    loss = red.pop("loss")
    grads.update(red)

    rest = [n for n in _WEIGHTS if n not in ("w_in", "w_out")]
    rest_shapes = [(n, w[n].shape) for n in rest]
    rows = -(-sum(int(np.prod(s)) for _, s in rest_shapes) // 1024) * 8
    outs = _adamw(_pack_rows([w[n] for n in rest], rows), _pack_rows([grads[n] for n in rest], rows),
                  _pack_rows([m[n] for n in rest], rows), _pack_rows([v[n] for n in rest], rows, 1.0), "adamw_rest")
    for dst, packed in zip((delta, new_m, new_v), outs):
        dst.update(_unpack_rows(packed, rest_shapes))

    return (loss, grad_x, *[grads[n] for n in _WEIGHTS], *[delta[n] for n in _WEIGHTS],
            *[new_m[n] for n in _WEIGHTS], *[new_v[n] for n in _WEIGHTS])
```

```python
import collections
import functools
import math

import numpy as np
import jax
import jax.numpy as jnp
from jax import lax
from jax.experimental import pallas as pl
from jax.experimental.pallas import tpu as pltpu

F32 = jnp.float32
BF16 = jnp.bfloat16

D_MODEL = 1024
DEPTH = 2
NH = 4
DH = 64
CH = 64
EPS = 1e-6
RWKV_GN_EPS = 64e-5
GLA_HEAD_K = 32
GLA_TAU = 16.0
GDN_TAPS = 4
SC_TAPS = 3
D_IN = 3992
N_DEV = 8
SHARD_COLS = D_IN // N_DEV

G_GDN = 0
G_RWKV = 16
G_SC = 32
G_GLA = 48
G_GDN_AB, G_RWKV_WD, G_RWKV_AD, G_GLA_AD = 64, 65, 66, 67
N_GROUPS = 68
GROUPS_PER_STEP = 4
TIME_BLOCK = 256
RWKV_EXACT_STEPS = 16

C_GDN, C_RWKV, C_SC, C_GLA = 0, 1032, 2184, 3208

ADAM_LR, ADAM_B1, ADAM_B2, ADAM_EPS, ADAM_WD, ADAM_STEP = 0.001, 0.9, 0.999, 1e-08, 0.01, 10

VMEM_LIMIT = 56 * 1024 * 1024
MESH = pl.DeviceIdType.MESH

_pcall = pl.pallas_call

_Comm = collections.namedtuple("_Comm", "operands out_shapes copies")


def _cparams(sem=None):
    if sem is None:
        return pltpu.CompilerParams(vmem_limit_bytes=VMEM_LIMIT)
    return pltpu.CompilerParams(dimension_semantics=sem, vmem_limit_bytes=VMEM_LIMIT)


def _group_segments():
    table = [(G_GDN + i, C_GDN + DH * i, DH) for i in range(16)]
    table.append((G_GDN_AB, C_GDN + 1024, 8))
    table += [(G_RWKV + i, C_RWKV + DH * i, DH) for i in range(16)]
    table += [(G_RWKV_WD, C_RWKV + 1024, DH), (G_RWKV_AD, C_RWKV + 1088, DH)]
    table += [(G_SC + 4 * j + k, C_SC + 256 * k + DH * j, DH) for j in range(NH) for k in range(4)]
    for h in range(NH):
        table += [(G_GLA + h, C_GLA + GLA_HEAD_K * h, GLA_HEAD_K),
                  (G_GLA + 4 + h, C_GLA + 128 + GLA_HEAD_K * h, GLA_HEAD_K),
                  (G_GLA + 8 + h, C_GLA + 256 + DH * h, DH),
                  (G_GLA + 12 + h, C_GLA + 512 + DH * h, DH)]
    table.append((G_GLA_AD, C_GLA + 768, 16))
    segs, padded = [], []
    for g, c, n in table:
        if n < DH:
            padded.append(g)
        a = 0
        while n > 0:
            d, off = divmod(c, SHARD_COLS)
            ln = min(n, SHARD_COLS - off)
            segs.append((g, a, d, off, ln))
            c, a, n = c + ln, a + ln, n - ln
    return segs, padded


_SEGMENTS, _PADDED_GROUPS = _group_segments()


def _dn(ta, tb):
    return (((1 if ta else 2,), (2 if tb else 1,)), ((0,), (0,)))


def _hdot(a, b, ta=False, tb=False):
    return lax.dot_general(a, b, _dn(ta, tb), precision=lax.Precision.HIGH, preferred_element_type=F32)


def _r(x):
    return x.astype(BF16)


def _rdot(a, b, ta=False, tb=False):
    return lax.dot_general(_r(a), _r(b), _dn(ta, tb), preferred_element_type=F32)


@jax.custom_vjp
def _bmm(a, b):
    return _rdot(a, b)


def _bmm_fwd(a, b):
    return _rdot(a, b), (a, b)


def _bmm_bwd(res, g):
    a, b = res
    return _rdot(g, b, tb=True), _rdot(a, g, ta=True)


_bmm.defvjp(_bmm_fwd, _bmm_bwd)


@jax.custom_vjp
def _bmm_nt(a, b):
    return _rdot(a, b, tb=True)


def _bmm_nt_fwd(a, b):
    return _rdot(a, b, tb=True), (a, b)


def _bmm_nt_bwd(res, g):
    a, b = res
    return _rdot(g, b), _rdot(g, a, ta=True)


_bmm_nt.defvjp(_bmm_nt_fwd, _bmm_nt_bwd)


@jax.custom_vjp
def _bmm_tn(a, b):
    return _rdot(a, b, ta=True)


def _bmm_tn_fwd(a, b):
    return _rdot(a, b, ta=True), (a, b)


def _bmm_tn_bwd(res, g):
    a, b = res
    return _rdot(b, g, tb=True), _rdot(a, g)


_bmm_tn.defvjp(_bmm_tn_fwd, _bmm_tn_bwd)


def _tri(n):
    i = lax.broadcasted_iota(jnp.int32, (n, n), 0)
    j = lax.broadcasted_iota(jnp.int32, (n, n), 1)
    return i >= j, i > j, i == j


def _heads_of(x, like):
    n = like.shape[0]
    if x.ndim == 2:
        return jnp.broadcast_to(x[None], (n,) + x.shape)
    seqs = x.shape[0]
    return jnp.broadcast_to(x[:, None], (seqs, n // seqs) + x.shape[1:]).reshape((n,) + x.shape[1:])


def _cumsum_rows(x):
    incl, _, _ = _tri(x.shape[-2])
    return _hdot(_heads_of(incl.astype(F32), x), x)


@jax.custom_vjp
def _inv_unit_lower(a):
    n = a.shape[-1]
    _, _, eye = _tri(n)
    pw = -a
    inv = eye.astype(F32) + pw
    for _ in range(math.ceil(math.log2(n)) - 1):
        pw = _hdot(pw, pw)
        inv = inv + _hdot(inv, pw)
    return inv


def _inv_unit_lower_fwd(a):
    inv = _inv_unit_lower(a)
    return inv, inv


def _inv_unit_lower_bwd(inv, g):
    return (-_hdot(_hdot(inv, g, ta=True), inv, tb=True),)


_inv_unit_lower.defvjp(_inv_unit_lower_fwd, _inv_unit_lower_bwd)


@jax.custom_vjp
def _inv_reuse(a, inv):
    return inv


def _inv_reuse_fwd(a, inv):
    return inv, inv


def _inv_reuse_bwd(inv, g):
    return _inv_unit_lower_bwd(inv, g)[0], jnp.zeros_like(inv)


_inv_reuse.defvjp(_inv_reuse_fwd, _inv_reuse_bwd)


def _silu(x):
    return x * jax.nn.sigmoid(x)


def _t(x):
    return jnp.swapaxes(x, -1, -2)


def _gdn_chunk(prm, cst, ins, s, tinv=None):
    a_log, dt_b, nw = prm
    m_a, m_b = cst
    cq, ck, cv, z, ab = ins
    ab = _heads_of(ab, m_a)
    incl, strict, _ = _tri(CH)
    q = _silu(cq)
    k = _silu(ck)
    v = _silu(cv)
    q = q * lax.rsqrt(jnp.sum(q * q, -1, keepdims=True) + EPS) * (DH ** -0.5)
    k = k * lax.rsqrt(jnp.sum(k * k, -1, keepdims=True) + EPS)
    a_raw = jnp.sum(ab * m_a, -1, keepdims=True)
    b_raw = jnp.sum(ab * m_b, -1, keepdims=True)
    gstep = -jnp.exp(a_log) * jax.nn.softplus(a_raw + dt_b)
    beta = jax.nn.sigmoid(b_raw)
    gc = _cumsum_rows(gstep)
    gl = jnp.sum(gstep, -2, keepdims=True)
    dec = jnp.where(incl, jnp.exp(jnp.where(incl, gc - _t(gc), 0.0)), 0.0)
    kb = k * beta
    a_mat = jnp.where(strict, _bmm_nt(kb, k) * dec, 0.0)
    tinv = _inv_unit_lower(a_mat) if tinv is None else _inv_reuse(a_mat, tinv)
    eg = jnp.exp(gc)
    u = _hdot(tinv, v * beta)
    w = _hdot(tinv, kb * eg)
    attn = _bmm_nt(q, k) * dec
    v_new = u - _bmm(w, s)
    o = _bmm(q * eg, s) + _bmm(attn, v_new)
    s_next = s * jnp.exp(gl) + _bmm_tn(k * jnp.exp(gl - gc), v_new)
    on = o * lax.rsqrt(jnp.mean(o * o, -1, keepdims=True) + EPS) * nw
    return on * _silu(z), s_next, tinv


def _gla_chunk(prm, cst, ins, st):
    a_up, a_bias, nw = prm
    q, k, v, z, ad = ins
    incl, _, _ = _tri(CH)
    la = jax.nn.log_sigmoid(_bmm(_heads_of(ad, a_up), a_up) + a_bias) * (1.0 / GLA_TAU)
    bc = _cumsum_rows(la)
    bl = jnp.sum(la, -2, keepdims=True)
    qe = q * (GLA_HEAD_K ** -0.5) * jnp.exp(bc)
    ke = k * jnp.exp(-bc)
    attn = jnp.where(incl, _bmm_nt(qe, ke), 0.0)
    o = _bmm_nt(qe, st) + _bmm(attn, v)
    st_next = st * jnp.exp(bl) + _bmm_tn(v, k * jnp.exp(bl - bc))
    on = o * lax.rsqrt(jnp.mean(o * o, -1, keepdims=True) + EPS) * nw
    return on * _silu(z), st_next


def _rwkv_chunk(prm, cst, ins, s, inv=None):
    r, v = ins[0], ins[2]
    incl, strict, _ = _tri(r.shape[-2])
    lw, kk, k2, m = _rwkv_pre(prm, ins)
    cum = _cumsum_rows(lw)
    ltot = jnp.sum(lw, -2, keepdims=True)
    n_t = -kk * jnp.exp(cum - lw)
    einv = jnp.exp(-cum)
    m_t = m * einv
    k_t = k2 * einv
    r_t = r * jnp.exp(cum)
    a_nm = jnp.where(strict, _hdot(n_t, m_t, tb=True), 0.0)
    a_nk = jnp.where(strict, _hdot(n_t, k_t, tb=True), 0.0)
    inv = _inv_unit_lower(-a_nm) if inv is None else _inv_reuse(-a_nm, inv)
    cm = _hdot(inv, _hdot(n_t, s, tb=True) + _bmm(a_nk, v))
    y = (_bmm_nt(r_t, s) + _bmm(jnp.where(incl, _hdot(r_t, m_t, tb=True), 0.0), cm)
         + _bmm(jnp.where(incl, _hdot(r_t, k_t, tb=True), 0.0), v))
    eend = jnp.exp(ltot - cum)
    s_next = s * jnp.exp(ltot) + _bmm_tn(cm, m * eend) + _bmm_tn(v, k2 * eend)
    return _rwkv_post(prm, ins, y, k2), s_next, inv


def _rwkv_pre(prm, ins):
    w0, w_up, a0, a_up, k_k, k_a = prm[:6]
    k, wd, ad = ins[1], ins[4], ins[5]
    lw = -math.exp(-0.5) * jax.nn.sigmoid(w0 + _bmm(_heads_of(jnp.tanh(wd), w_up), w_up))
    a = jax.nn.sigmoid(a0 + _bmm(_heads_of(ad, a_up), a_up))
    kk = k * k_k
    kk = kk * lax.rsqrt(jnp.sum(kk * kk, -1, keepdims=True) + EPS)
    k2 = k * (1.0 + (a - 1.0) * k_a)
    return lw, kk, k2, kk * a


def _rwkv_post(prm, ins, y, k2):
    r_k, ln_w, ln_b = prm[6:]
    r, v, z = ins[0], ins[2], ins[3]
    mean = jnp.mean(y, -1, keepdims=True)
    yc = y - mean
    var = jnp.mean(yc * yc, -1, keepdims=True)
    yn = yc * lax.rsqrt(var + RWKV_GN_EPS) * ln_w + ln_b
    bonus = jnp.sum(r * k2 * r_k, -1, keepdims=True) * v
    return (yn + bonus) * _silu(z)


@jax.custom_vjp
def _bmv(s, x):
    return jnp.sum(_r(s).astype(F32) * _r(x).astype(F32), -1, keepdims=True)


def _bmv_fwd(s, x):
    return _bmv(s, x), (s, x)


def _bmv_bwd(res, g):
    s, x = res
    return g * x, jnp.sum(_r(s).astype(F32) * _r(g).astype(F32), -2, keepdims=True)


_bmv.defvjp(_bmv_fwd, _bmv_bwd)


def _rwkv_steps(prm, cst, ins, s, steps):
    r, v = ins[0], ins[2]
    lw, kk, k2, m = _rwkv_pre(prm, ins)
    w = jnp.exp(lw)
    v_t = _t(v)
    lane = lax.broadcasted_iota(jnp.int32, (1, 1, CH), 2)
    y_t = jnp.zeros((s.shape[0], DH, CH), F32)
    for t in range(steps):
        e_t = (lane == t).astype(F32)
        row = (slice(None), slice(t, t + 1))
        sa = _bmv(s, -kk[row])
        s = s * w[row] + sa * m[row] + jnp.sum(v_t * e_t, -1, keepdims=True) * k2[row]
        y_t = y_t + _bmv(s, r[row]) * e_t
    return _rwkv_post(prm, ins, _t(y_t), k2)[:, :steps], s


def _rwkv_first_chunk(prm, cst, ins, s):
    k = RWKV_EXACT_STEPS
    y_head, s = _rwkv_steps(prm, cst, ins, s, k)
    y_tail, s, _ = _rwkv_chunk(prm, cst, [x[..., k:, :] for x in ins], s)
    return jnp.concatenate([y_head, y_tail], axis=-2), s


def _time_block(t):
    return TIME_BLOCK if t % TIME_BLOCK == 0 else t


def _load_chunk(ref, i):
    nb = ref.shape[1]
    if ref.shape[0] == NH:
        return jnp.concatenate([ref[:, b, pl.ds(i, CH), :] for b in range(nb)], axis=0)
    return ref[0, :, pl.ds(i, CH), :]


def _mixer_fwd(chunk_fn, name, ins, prm, cst, nb, t, first_fn=None, side=None, n_kept=0):
    tb = _time_block(t)
    nt, ncb, nch = t // tb, tb // CH, nb * NH
    n_in, n_prm, n_cst = len(ins), len(prm), len(cst)
    n_main, n_side = n_in + n_prm + n_cst, len(side.operands) if side else 0

    def body(*refs):
        in_refs = refs[:n_in]
        prm_refs = refs[n_in:n_in + n_prm]
        cst_refs = refs[n_in + n_prm:n_main]
        side_in = refs[n_main:n_main + n_side]
        y_ref, ck_ref = refs[n_main + n_side:n_main + n_side + 2]
        side_out = refs[n_main + n_side + 2:n_main + 2 * n_side + 2]
        s_scr = refs[n_main + 2 * n_side + 2]
        sems = refs[n_main + 2 * n_side + 3:]
        step_t = pl.program_id(0)

        if side is not None:
            @pl.when(step_t == 0)
            def _():
                _comm_start(side.copies(side_in, side_out, sems))

        @pl.when(step_t == 0)
        def _():
            s_scr[...] = jnp.zeros_like(s_scr)

        def chunk(c, i, fn=chunk_fn):
            s = s_scr[...]
            y, s_next, *kept = fn([jnp.tile(r[...], (nb, 1, 1)) for r in prm_refs],
                                  [jnp.tile(r[...], (nb, 1, 1)) for r in cst_refs],
                                  [_load_chunk(r, i) for r in in_refs], s)
            kept += [jnp.zeros_like(s)] * (n_kept - len(kept))
            for e, a in enumerate([s] + kept):
                ck_ref[c, e] = a
            for b in range(nb):
                y_ref[:, b, pl.ds(i, CH), :] = y[b * NH:(b + 1) * NH].astype(BF16)
            s_scr[...] = s_next

        def step(c, carry):
            chunk(c, pl.multiple_of(c * CH, CH))
            return carry

        if first_fn is None:
            lax.fori_loop(0, ncb, step, 0)
        else:
            @pl.when(step_t == 0)
            def _():
                chunk(0, 0, first_fn)

            @pl.when(step_t != 0)
            def _():
                chunk(0, 0)

            lax.fori_loop(1, ncb, step, 0)

        if side is not None:
            @pl.when(step_t == nt - 1)
            def _():
                _comm_wait(side.copies(side_in, side_out, sems))

    hbm = pl.BlockSpec(memory_space=pl.ANY)
    in_specs = [pl.BlockSpec((ng, nb, tb, DH), (lambda j, bi=bi: (bi, 0, j, 0))) for _, ng, bi in ins]
    in_specs += [pl.BlockSpec(p.shape, lambda j: (0, 0, 0)) for p in list(prm) + list(cst)]
    y, ck, *side_res = _pcall(
        body, name=name, grid=(nt,),
        in_specs=in_specs + [hbm] * n_side,
        out_specs=[pl.BlockSpec((NH, nb, tb, DH), lambda j: (0, 0, j, 0)),
                   pl.BlockSpec((ncb, 1 + n_kept, nch, DH, DH), lambda j: (j, 0, 0, 0, 0))] + [hbm] * n_side,
        out_shape=[jax.ShapeDtypeStruct((NH, nb, t, DH), BF16),
                   jax.ShapeDtypeStruct((t // CH, 1 + n_kept, nch, DH, DH), F32)]
        + (side.out_shapes if side else []),
        scratch_shapes=[pltpu.VMEM((nch, DH, DH), F32)] + (_comm_scratch(side) if side else []),
        compiler_params=_cparams(("arbitrary",)),
    )(*[a.reshape(a.shape[0], nb, t, DH) for a, _, _ in ins], *prm, *cst, *(side.operands if side else []))
    return y.reshape(NH, nb * t, DH), ck, side_res


def _mixer_bwd(chunk_fn, name, ins, prm, cst, ck, dy, dy_block, outs, routes, nb, t, first_fn=None, side=None):
    tb = _time_block(t)
    nt, ncb, nch = t // tb, tb // CH, nb * NH
    n_in, n_prm, n_cst, n_out = len(ins), len(prm), len(cst), len(outs)
    n_main, n_side = n_in + n_prm + n_cst + 2, len(side.operands) if side else 0
    n_kept = ck.shape[1] - 1

    def body(*refs):
        in_refs = refs[:n_in]
        prm_refs = refs[n_in:n_in + n_prm]
        cst_refs = refs[n_in + n_prm:n_in + n_prm + n_cst]
        ck_ref, dy_ref = refs[n_main - 2:n_main]
        side_in = refs[n_main:n_main + n_side]
        rest = refs[n_main + n_side:]
        out_refs = rest[:n_out]
        dprm_refs = rest[n_out:n_out + n_prm]
        side_out = rest[n_out + n_prm:n_out + n_prm + n_side]
        ds_scr = rest[n_out + n_prm + n_side]
        sems = rest[n_out + n_prm + n_side + 1:]
        step_t = pl.program_id(0)

        if side is not None:
            @pl.when(step_t == 0)
            def _():
                _comm_start(side.copies(side_in, side_out, sems))

        @pl.when(step_t == 0)
        def _():
            ds_scr[...] = jnp.zeros_like(ds_scr)
            for r in dprm_refs:
                r[...] = jnp.zeros_like(r)

        def chunk(c, i, fn=chunk_fn):
            cst_v = [jnp.tile(r[...], (nb, 1, 1)) for r in cst_refs]
            kept = [ck_ref[c, 1 + e] for e in range(n_kept)] if fn is chunk_fn else []
            _, vjp = jax.vjp(lambda p, x, s: fn(p, cst_v, x, s, *kept)[:2],
                             [jnp.tile(r[...], (nb, 1, 1)) for r in prm_refs],
                             [_load_chunk(r, i) for r in in_refs], ck_ref[c, 0])
            dy_c = jnp.concatenate([dy_ref[:, b, pl.ds(i, CH), :] for b in range(nb)], axis=0)
            d_prm, d_ins, d_s = vjp((dy_c, ds_scr[...]))
            for (oi, g0), r, g in zip(routes, in_refs, d_ins):
                o_ref = out_refs[oi]
                if r.shape[0] == NH:
                    for b in range(nb):
                        o_ref[g0:g0 + NH, b, pl.ds(i, CH), :] = g[b * NH:(b + 1) * NH].astype(o_ref.dtype)
                else:
                    o_ref[g0, :, pl.ds(i, CH), :] = g.astype(o_ref.dtype)
            for r, g in zip(dprm_refs, d_prm):
                r[...] += g
            ds_scr[...] = d_s

        def step(j, carry):
            c = ncb - 1 - j
            chunk(c, pl.multiple_of(c * CH, CH))
            return carry

        lax.fori_loop(0, ncb - 1, step, 0)
        if first_fn is None:
            chunk(0, 0)
        else:
            @pl.when(step_t == nt - 1)
            def _():
                chunk(0, 0, first_fn)

            @pl.when(step_t != nt - 1)
            def _():
                chunk(0, 0)

        if side is not None:
            @pl.when(step_t == nt - 1)
            def _():
                _comm_wait(side.copies(side_in, side_out, sems))

    def back(j):
        return nt - 1 - j

    hbm = pl.BlockSpec(memory_space=pl.ANY)
    in_specs = [pl.BlockSpec((ng, nb, tb, DH), (lambda j, bi=bi: (bi, 0, back(j), 0))) for _, ng, bi in ins]
    in_specs += [pl.BlockSpec(p.shape, lambda j: (0, 0, 0)) for p in list(prm) + list(cst)]
    in_specs += [pl.BlockSpec((ncb, 1 + n_kept, nch, DH, DH), lambda j: (back(j), 0, 0, 0, 0)),
                 pl.BlockSpec((NH, nb, tb, DH), lambda j: (dy_block, 0, back(j), 0))]
    out_specs = [pl.BlockSpec((ng, nb, tb, DH), lambda j: (0, 0, back(j), 0)) for ng, _ in outs]
    out_specs += [pl.BlockSpec((nch,) + p.shape[1:], lambda j: (0, 0, 0)) for p in prm]
    out_shape = [jax.ShapeDtypeStruct((ng, nb, t, DH), dt) for ng, dt in outs]
    out_shape += [jax.ShapeDtypeStruct((nch,) + p.shape[1:], F32) for p in prm]
    res = _pcall(
        body, name=name, grid=(nt,),
        in_specs=in_specs + [hbm] * n_side, out_specs=out_specs + [hbm] * n_side,
        out_shape=out_shape + (side.out_shapes if side else []),
        scratch_shapes=[pltpu.VMEM((nch, DH, DH), F32)] + (_comm_scratch(side) if side else []),
        compiler_params=_cparams(("arbitrary",)),
    )(*[a.reshape(a.shape[0], nb, t, DH) for a, _, _ in ins], *prm, *cst, ck, dy.reshape(dy.shape[0], nb, t, DH),
      *(side.operands if side else []))
    d_outs = [o.reshape(o.shape[0], nb * t, DH) for o in res[:n_out]]
    d_prm = [g.reshape((nb,) + p.shape) for g, p in zip(res[n_out:n_out + n_prm], prm)]
    return d_outs, d_prm, res[n_out + n_prm:]


def _shift_down(x, s):
    if s == 0:
        return x
    row = lax.broadcasted_iota(jnp.int32, x.shape, 0)
    return jnp.where(row < s, 0.0, pltpu.roll(x, s, 0))


def _shift_up(x, s):
    if s == 0:
        return x
    t = x.shape[0]
    row = lax.broadcasted_iota(jnp.int32, x.shape, 0)
    return jnp.where(row >= t - s, 0.0, pltpu.roll(x, t - s, 0))


def _conv_bwd(p, g0, ng, w, dy, nb, t, name):
    taps = w.shape[1]

    def body(x_ref, w_ref, dy_ref, dx_ref, dw_ref):
        x = x_ref[...]
        d = dy_ref[...]
        acc = w_ref[taps - 1:taps, :] * d
        dw_ref[taps - 1:taps, :] = jnp.sum(d * x, 0, keepdims=True)
        for i in range(taps - 1):
            s = taps - 1 - i
            acc = acc + w_ref[i:i + 1, :] * _shift_up(d, s)
            dw_ref[i:i + 1, :] = jnp.sum(d * _shift_down(x, s), 0, keepdims=True)
        dx_ref[...] = acc.astype(BF16)

    return _pcall(
        body, name=name, grid=(ng, nb),
        in_specs=[pl.BlockSpec((None, t, DH), lambda g, b: (g0 + g, b, 0)),
                  pl.BlockSpec((None, taps, DH), lambda g, b: (g, 0, 0)),
                  pl.BlockSpec((None, t, DH), lambda g, b: (g, b, 0))],
        out_specs=[pl.BlockSpec((None, t, DH), lambda g, b: (g, b, 0)),
                   pl.BlockSpec((None, None, taps, DH), lambda g, b: (g, b, 0, 0))],
        out_shape=[jax.ShapeDtypeStruct((ng, nb * t, DH), BF16),
                   jax.ShapeDtypeStruct((ng, nb, taps, DH), F32)],
        compiler_params=_cparams(("parallel", "parallel")),
    )(p, w, dy)


def _mix_group(g):
    return jnp.where(g < 16, G_RWKV + g, G_RWKV_WD + g - 16)


def _mix_bwd(p, mu, dy, nb, t, name):
    def body(x_ref, mu_ref, dy_ref, dx_ref, dmu_ref):
        x = x_ref[...]
        muv = mu_ref[...]
        d = dy_ref[...]
        dx_ref[...] = (d * (1.0 - muv) + _shift_up(d * muv, 1)).astype(BF16)
        dmu_ref[...] = jnp.sum(d * (_shift_down(x, 1) - x), 0, keepdims=True)

    return _pcall(
        body, name=name, grid=(18, nb),
        in_specs=[pl.BlockSpec((None, t, DH), lambda g, b: (_mix_group(g), b, 0)),
                  pl.BlockSpec((None, 1, DH), lambda g, b: (g, 0, 0)),
                  pl.BlockSpec((None, t, DH), lambda g, b: (g, b, 0))],
        out_specs=[pl.BlockSpec((None, t, DH), lambda g, b: (g, b, 0)),
                   pl.BlockSpec((None, None, 1, DH), lambda g, b: (g, b, 0, 0))],
        out_shape=[jax.ShapeDtypeStruct((18, nb * t, DH), BF16),
                   jax.ShapeDtypeStruct((18, nb, 1, DH), F32)],
        compiler_params=_cparams(("parallel", "parallel")),
    )(p, mu, dy)


def _sc_bwd(p, w, dy, nb, t, name):
    def body(p_ref, w_ref, dy_ref, dp_ref, dw_ref):
        bg, cg, xg, z = p_ref[0], p_ref[1], p_ref[2], p_ref[3]
        d = dy_ref[...]
        u = cg * xg
        u1 = _shift_down(u, 1)
        u2 = _shift_down(u, 2)
        conv = w_ref[2:3, :] * u + w_ref[1:2, :] * u1 + w_ref[0:1, :] * u2
        sg = jax.nn.sigmoid(z)
        sz = z * sg
        dp_ref[0] = (d * conv * sz).astype(BF16)
        dp_ref[3] = (d * bg * conv * (sg * (1.0 + z * (1.0 - sg)))).astype(BF16)
        dconv = d * bg * sz
        du = w_ref[2:3, :] * dconv + w_ref[1:2, :] * _shift_up(dconv, 1) + w_ref[0:1, :] * _shift_up(dconv, 2)
        dp_ref[1] = (du * xg).astype(BF16)
        dp_ref[2] = (du * cg).astype(BF16)
        dw_ref[2:3, :] = jnp.sum(dconv * u, 0, keepdims=True)
        dw_ref[1:2, :] = jnp.sum(dconv * u1, 0, keepdims=True)
        dw_ref[0:1, :] = jnp.sum(dconv * u2, 0, keepdims=True)

    return _pcall(
        body, name=name, grid=(NH, nb),
        in_specs=[pl.BlockSpec((4, t, DH), lambda j, b: (G_SC // 4 + j, b, 0)),
                  pl.BlockSpec((None, SC_TAPS, DH), lambda j, b: (j, 0, 0)),
                  pl.BlockSpec((None, t, DH), lambda j, b: (8 + j, b, 0))],
        out_specs=[pl.BlockSpec((4, t, DH), lambda j, b: (j, b, 0)),
                   pl.BlockSpec((None, None, SC_TAPS, DH), lambda j, b: (j, b, 0, 0))],
        out_shape=[jax.ShapeDtypeStruct((4 * NH, nb * t, DH), BF16),
                   jax.ShapeDtypeStruct((NH, nb, SC_TAPS, DH), F32)],
        compiler_params=_cparams(("parallel", "parallel")),
    )(p, w, dy)


def _row_tile(n):
    return 1024 if n % 1024 == 0 else n


def _regroup_in(w_all, name):
    tr = 256
    gs = GROUPS_PER_STEP

    def body(w_ref, o_ref):
        for g in _PADDED_GROUPS:
            o_ref[g // gs, :, DH * (g % gs):DH * (g % gs + 1)] = jnp.zeros((tr, DH), BF16)
        for g, a, d, off, ln in _SEGMENTS:
            lane = DH * (g % gs) + a
            o_ref[g // gs, :, lane:lane + ln] = w_ref[d, :, off:off + ln].astype(BF16)

    return _pcall(
        body, name=name, grid=(D_MODEL // tr,),
        in_specs=[pl.BlockSpec((N_DEV, tr, SHARD_COLS), lambda i: (0, i, 0))],
        out_specs=pl.BlockSpec((N_GROUPS // gs, tr, gs * DH), lambda i: (0, i, 0)),
        out_shape=jax.ShapeDtypeStruct((N_GROUPS // gs, D_MODEL, gs * DH), BF16),
        compiler_params=_cparams(("parallel",)),
    )(w_all)


def _regroup_out(dwg, name):
    tr = 256
    gs = GROUPS_PER_STEP

    def body(g_ref, o_ref):
        for g, a, d, off, ln in _SEGMENTS:
            lane = DH * (g % gs) + a
            o_ref[d, :, off:off + ln] = g_ref[g // gs, :, lane:lane + ln].astype(BF16)

    return _pcall(
        body, name=name, grid=(D_MODEL // tr,),
        in_specs=[pl.BlockSpec((N_GROUPS // gs, tr, gs * DH), lambda i: (0, i, 0))],
        out_specs=pl.BlockSpec((N_DEV, tr, SHARD_COLS), lambda i: (0, i, 0)),
        out_shape=jax.ShapeDtypeStruct((N_DEV, D_MODEL, SHARD_COLS), BF16),
        compiler_params=_cparams(("parallel",)),
    )(dwg)


CONV_BLOCKS = (0, 1, 2)
MIX_BLOCKS = (4, 5, 6, 7, 16)
SHIFTED_BLOCKS = {0: 0, 1: 1, 2: 2, 4: 3, 5: 4, 6: 5, 7: 6, 16: 7}
G_PM = 12
G_PM_WD, G_PM_AD = 29, 30
SC_BLOCK0 = G_SC // GROUPS_PER_STEP


def _norm_proj(x, pre_w, w_g, conv_w, mix_mu, sc_w, t, name):
    n = x.shape[0]
    tm = _row_tile(n)
    gs = GROUPS_PER_STEP
    nj = N_GROUPS // gs
    assert t % tm == 0, (t, tm)
    per_seq = t // tm

    def shifted_block(j):
        out = jnp.int32(len(SHIFTED_BLOCKS) - 1)
        for jj in sorted(SHIFTED_BLOCKS, reverse=True):
            out = jnp.where(j < jj, max(SHIFTED_BLOCKS[jj] - 1, 0), out)
        for jj, b in SHIFTED_BLOCKS.items():
            out = jnp.where(j == jj, b, out)
        return out

    def body(x_ref, pw_ref, w_ref, cw_ref, mu_ref, scw_ref, h_ref, p_ref, s_ref, ysc_ref, carry):
        i, j = pl.program_id(0), pl.program_id(1)

        @pl.when(j == 0)
        def _():
            xv = x_ref[...]
            h = xv * lax.rsqrt(jnp.mean(xv * xv, -1, keepdims=True) + EPS) * pw_ref[...]
            h_ref[...] = h.astype(BF16)

        r = jnp.dot(h_ref[...], w_ref[...], preferred_element_type=F32)
        for k in range(gs):
            p_ref[k] = r[:, DH * k:DH * (k + 1)]

        def shifts():
            first = (i % per_seq) == 0
            tail = jnp.where(first, 0.0, carry[j])
            above = jnp.concatenate([tail, jnp.zeros((tm - 8, gs * DH), F32)], axis=0)
            row = lax.broadcasted_iota(jnp.int32, r.shape, 0)
            out = [jnp.where(row < s, pltpu.roll(above, tm - 8 + s, 0), pltpu.roll(r, s, 0)) for s in (1, 2, 3)]
            carry[j] = r[tm - 8:, :]
            return out

        def store(v):
            for k in range(gs):
                s_ref[k] = v[:, DH * k:DH * (k + 1)]

        @pl.when(functools.reduce(jnp.logical_or, [j == b for b in CONV_BLOCKS]))
        def _():
            s1, s2, s3 = shifts()
            store(cw_ref[3:4, :] * r + cw_ref[0:1, :] * s3 + cw_ref[1:2, :] * s2 + cw_ref[2:3, :] * s1)

        @pl.when(functools.reduce(jnp.logical_or, [j == b for b in MIX_BLOCKS]))
        def _():
            s1 = shifts()[0]
            store(r + mu_ref[...] * (s1 - r))

        @pl.when((j >= SC_BLOCK0) & (j < SC_BLOCK0 + NH))
        def _():
            first = (i % per_seq) == 0
            tail = jnp.where(first, 0.0, carry[j])
            u = r[:, DH:2 * DH] * r[:, 2 * DH:3 * DH]
            above = jnp.concatenate([tail[:, DH:2 * DH] * tail[:, 2 * DH:3 * DH], jnp.zeros((tm - 8, DH), F32)], axis=0)
            row = lax.broadcasted_iota(jnp.int32, u.shape, 0)
            u1, u2 = [jnp.where(row < s, pltpu.roll(above, tm - 8 + s, 0), pltpu.roll(u, s, 0)) for s in (1, 2)]
            conv = scw_ref[2:3, :] * u + scw_ref[1:2, :] * u1 + scw_ref[0:1, :] * u2
            ysc_ref[...] = (r[:, :DH] * conv * _silu(r[:, 3 * DH:])).astype(BF16)
            carry[j] = r[tm - 8:, :]

    n_shifted = len(SHIFTED_BLOCKS) * gs
    return _pcall(
        body, name=name, grid=(n // tm, nj),
        in_specs=[pl.BlockSpec((tm, D_MODEL), lambda i, j: (i, 0)),
                  pl.BlockSpec((1, D_MODEL), lambda i, j: (0, 0)),
                  pl.BlockSpec((None, D_MODEL, gs * DH), lambda i, j: (j, 0, 0)),
                  pl.BlockSpec((None, GDN_TAPS, gs * DH), lambda i, j: (j, 0, 0)),
                  pl.BlockSpec((None, 1, gs * DH), lambda i, j: (j, 0, 0)),
                  pl.BlockSpec((None, SC_TAPS, DH), lambda i, j: (jnp.clip(j - SC_BLOCK0, 0, NH - 1), 0, 0))],
        out_specs=[pl.BlockSpec((tm, D_MODEL), lambda i, j: (i, 0)),
                   pl.BlockSpec((gs, tm, DH), lambda i, j: (j, i, 0)),
                   pl.BlockSpec((gs, tm, DH), lambda i, j: (shifted_block(j), i, 0)),
                   pl.BlockSpec((None, tm, DH), lambda i, j: (jnp.clip(j - SC_BLOCK0, 0, NH - 1), i, 0))],
        out_shape=[jax.ShapeDtypeStruct((n, D_MODEL), BF16),
                   jax.ShapeDtypeStruct((N_GROUPS, n, DH), F32),
                   jax.ShapeDtypeStruct((n_shifted, n, DH), F32),
                   jax.ShapeDtypeStruct((NH, n, DH), BF16)],
        scratch_shapes=[pltpu.VMEM((nj, 8, gs * DH), F32)],
        compiler_params=_cparams(("arbitrary", "arbitrary")),
    )(x, pre_w, w_g, conv_w, mix_mu, sc_w)


def _out_proj_norm(ys, wout_g, x, post_w, name, tgt=None):
    n = x.shape[0]
    tm = _row_tile(n)

    def body(y0, y1, y2, y3, w_ref, x_ref, pw_ref, *rest):
        y = jnp.concatenate([yr[h] for yr in (y0, y1, y2, y3) for h in range(NH)], axis=-1)
        acc = jnp.dot(y, w_ref[...], preferred_element_type=F32)
        xn = x_ref[...] + acc * lax.rsqrt(jnp.mean(acc * acc, -1, keepdims=True) + EPS) * pw_ref[...]
        if tgt is None:
            out_ref, xn_ref = rest
            xn_ref[...] = xn
        else:
            t_ref, out_ref, dx_ref, l_ref = rest

            @pl.when(pl.program_id(0) == 0)
            def _():
                l_ref[...] = jnp.zeros_like(l_ref)

            e = xn - t_ref[...]
            dx_ref[...] = e * (1.0 / D_MODEL)
            l_ref[...] += jnp.sum(jnp.sum(e * e, -1, keepdims=True), 0, keepdims=True) * (0.5 / D_MODEL)
        out_ref[...] = acc

    yspec = pl.BlockSpec((NH, tm, DH), lambda i: (0, i, 0))
    rows = pl.BlockSpec((tm, D_MODEL), lambda i: (i, 0))
    full = jax.ShapeDtypeStruct((n, D_MODEL), F32)
    head = tgt is not None
    return _pcall(
        body, name=name, grid=(n // tm,),
        in_specs=[yspec] * 4 + [pl.BlockSpec((D_MODEL, D_MODEL), lambda i: (0, 0)), rows,
                                pl.BlockSpec((1, D_MODEL), lambda i: (0, 0))] + [rows] * head,
        out_specs=[rows, rows] + [pl.BlockSpec((1, 128), lambda i: (0, 0))] * head,
        out_shape=[full, full] + [jax.ShapeDtypeStruct((1, 128), F32)] * head,
        compiler_params=_cparams(("arbitrary",) if head else ("parallel",)),
    )(*ys, wout_g.reshape(D_MODEL, D_MODEL), x, post_w, *([tgt] if head else []))


def _rmsnorm_bwd(xv, w, d):
    r = lax.rsqrt(jnp.mean(xv * xv, -1, keepdims=True) + EPS)
    xh = xv * r
    dxh = d * w
    dx = r * (dxh - xh * jnp.mean(dxh * xh, -1, keepdims=True))
    return dx, d * xh


def _post_bwd(dxn, out, post_w, wout_g, name):
    n = dxn.shape[0]
    tm = _row_tile(n)

    def body(d_ref, o_ref, pw_ref, w_ref, do_ref, dy_ref, dpw_ref):
        @pl.when(pl.program_id(0) == 0)
        def _():
            dpw_ref[...] = jnp.zeros_like(dpw_ref)

        dout, dw_rows = _rmsnorm_bwd(o_ref[...], pw_ref[...], d_ref[...])
        dpw_ref[...] += jnp.sum(dw_rows, 0, keepdims=True)
        db = dout.astype(BF16)
        do_ref[...] = db
        dy = lax.dot_general(db, w_ref[...], (((1,), (1,)), ((), ())), preferred_element_type=F32)
        for g in range(4 * NH):
            dy_ref[g] = dy[:, DH * g:DH * (g + 1)]

    rows = pl.BlockSpec((tm, D_MODEL), lambda i: (i, 0))
    vec = pl.BlockSpec((1, D_MODEL), lambda i: (0, 0))
    return _pcall(
        body, name=name, grid=(n // tm,),
        in_specs=[rows, rows, vec, pl.BlockSpec((D_MODEL, D_MODEL), lambda i: (0, 0))],
        out_specs=[rows, pl.BlockSpec((4 * NH, tm, DH), lambda i: (0, i, 0)), vec],
        out_shape=[jax.ShapeDtypeStruct((n, D_MODEL), BF16),
                   jax.ShapeDtypeStruct((4 * NH, n, DH), F32),
                   jax.ShapeDtypeStruct((1, D_MODEL), F32)],
        compiler_params=_cparams(("arbitrary",)),
    )(dxn, out, post_w, wout_g.reshape(D_MODEL, D_MODEL))


def _dwout(ys, dout, name):
    n = dout.shape[0]
    tm = _row_tile(n)

    def body(y0, y1, y2, y3, d_ref, dw_ref):
        @pl.when(pl.program_id(0) == 0)
        def _():
            dw_ref[...] = jnp.zeros_like(dw_ref)

        y = jnp.concatenate([yr[h] for yr in (y0, y1, y2, y3) for h in range(NH)], axis=-1)
        dw_ref[...] += lax.dot_general(y, d_ref[...], (((0,), (0,)), ((), ())), preferred_element_type=F32)

    yspec = pl.BlockSpec((NH, tm, DH), lambda i: (0, i, 0))
    return _pcall(
        body, name=name, grid=(n // tm,),
        in_specs=[yspec] * 4 + [pl.BlockSpec((tm, D_MODEL), lambda i: (i, 0))],
        out_specs=pl.BlockSpec((D_MODEL, D_MODEL), lambda i: (0, 0)),
        out_shape=jax.ShapeDtypeStruct((D_MODEL, D_MODEL), F32),
        compiler_params=_cparams(("arbitrary",)),
    )(*ys, dout)


def _source_specs(sources, rows_first):
    gs = GROUPS_PER_STEP
    spans, specs, j0 = [], [], 0
    for a in sources:
        nblk = a.shape[0] // gs
        spans.append((j0, j0 + nblk))
        shape = (gs, _row_tile(a.shape[1]), DH)

        def blk(j, j0=j0, nblk=nblk):
            return jnp.clip(j - j0, 0, nblk - 1)

        if rows_first:
            specs.append(pl.BlockSpec(shape, (lambda i, j, blk=blk: (blk(j), i, 0))))
        else:
            specs.append(pl.BlockSpec(shape, (lambda j, i, blk=blk: (blk(j), i, 0))))
        j0 += nblk
    return spans, specs


def _dh_prenorm_bwd(sources, w_g, x, pre_w, dxn, name, side=None):
    n = x.shape[0]
    tm = _row_tile(n)
    gs = GROUPS_PER_STEP
    nj = N_GROUPS // gs
    ni = n // tm
    spans, src_specs = _source_specs(sources, True)
    ns = len(sources)
    n_side = len(side.operands) if side else 0

    def body(*refs):
        src = refs[:ns]
        w_ref, x_ref, pw_ref, d_ref = refs[ns:ns + 4]
        side_in = refs[ns + 4:ns + 4 + n_side]
        dx_ref, dpw_ref = refs[ns + 4 + n_side:ns + 6 + n_side]
        side_out = refs[ns + 6 + n_side:ns + 6 + 2 * n_side]
        acc = refs[ns + 6 + 2 * n_side]
        sems = refs[ns + 7 + 2 * n_side:]
        i, j = pl.program_id(0), pl.program_id(1)

        if side is not None:
            @pl.when((i == 0) & (j == 0))
            def _():
                _comm_start(side.copies(side_in, side_out, sems))

        @pl.when((i == 0) & (j == 0))
        def _():
            dpw_ref[...] = jnp.zeros_like(dpw_ref)

        @pl.when(j == 0)
        def _():
            acc[...] = jnp.zeros_like(acc)

        for s_ref, (lo, hi) in zip(src, spans):
            @pl.when((j >= lo) & (j < hi))
            def _(s_ref=s_ref):
                four = jnp.concatenate([s_ref[k] for k in range(gs)], axis=-1)
                acc[...] += lax.dot_general(four, w_ref[...], (((1,), (1,)), ((), ())), preferred_element_type=F32)

        @pl.when(j == nj - 1)
        def _():
            dx, dw_rows = _rmsnorm_bwd(x_ref[...], pw_ref[...], acc[...])
            dx_ref[...] = d_ref[...] + dx
            dpw_ref[...] += jnp.sum(dw_rows, 0, keepdims=True)

        if side is not None:
            @pl.when((i == ni - 1) & (j == nj - 1))
            def _():
                _comm_wait(side.copies(side_in, side_out, sems))

    hbm = pl.BlockSpec(memory_space=pl.ANY)
    rows = pl.BlockSpec((tm, D_MODEL), lambda i, j: (i, 0))
    vec = pl.BlockSpec((1, D_MODEL), lambda i, j: (0, 0))
    dx, dpw, *side_res = _pcall(
        body, name=name, grid=(ni, nj),
        in_specs=src_specs + [pl.BlockSpec((None, D_MODEL, gs * DH), lambda i, j: (j, 0, 0)), rows, vec, rows]
        + [hbm] * n_side,
        out_specs=[rows, vec] + [hbm] * n_side,
        out_shape=[jax.ShapeDtypeStruct((n, D_MODEL), F32), jax.ShapeDtypeStruct((1, D_MODEL), F32)]
        + (side.out_shapes if side else []),
        scratch_shapes=[pltpu.VMEM((tm, D_MODEL), F32)] + (_comm_scratch(side) if side else []),
        compiler_params=_cparams(("arbitrary", "arbitrary")),
    )(*sources, w_g, x, pre_w, dxn, *(side.operands if side else []))
    return dx, dpw, side_res


def _dwin(hb, sources, name):
    n = hb.shape[0]
    tm = _row_tile(n)
    gs = GROUPS_PER_STEP
    ni, nj = n // tm, N_GROUPS // gs
    spans, src_specs = _source_specs(sources, True)
    ns = len(sources)

    def body(*refs):
        h_ref = refs[0]
        src = refs[1:1 + ns]
        out_ref, acc, sem = refs[1 + ns:]
        i, j = pl.program_id(0), pl.program_id(1)

        @pl.when((i == 0) & (j == 0))
        def _():
            acc[...] = jnp.zeros_like(acc)

        h = h_ref[...]
        for s_ref, (lo, hi) in zip(src, spans):
            @pl.when((j >= lo) & (j < hi))
            def _(s_ref=s_ref):
                four = jnp.concatenate([s_ref[k] for k in range(gs)], axis=-1)
                acc[j] += jnp.dot(h, four, preferred_element_type=F32)

        @pl.when((i == ni - 1) & (j == nj - 1))
        def _():
            done = pltpu.make_async_copy(acc, out_ref, sem)
            done.start()
            done.wait()

    return _pcall(
        body, name=name, grid=(ni, nj),
        in_specs=[pl.BlockSpec((D_MODEL, tm), lambda i, j: (0, i))] + src_specs,
        out_specs=pl.BlockSpec(memory_space=pl.ANY),
        out_shape=jax.ShapeDtypeStruct((nj, D_MODEL, gs * DH), F32),
        scratch_shapes=[pltpu.VMEM((nj, D_MODEL, gs * DH), F32), pltpu.SemaphoreType.DMA],
        compiler_params=_cparams(("arbitrary", "arbitrary")),
    )(jnp.transpose(hb), *sources)


def _adamw_math(w, g, m, v):
    c1 = 1.0 - ADAM_B1 ** ADAM_STEP
    c2 = 1.0 - ADAM_B2 ** ADAM_STEP
    nm = ADAM_B1 * m + (1.0 - ADAM_B1) * g
    nv = ADAM_B2 * v + (1.0 - ADAM_B2) * (g * g)
    return -ADAM_LR * ((nm / c1) / (jnp.sqrt(nv / c2) + ADAM_EPS) + ADAM_WD * w), nm, nv


def _adamw(w, g, m, v, name):
    r, c = w.shape
    tr = 256 if r % 256 == 0 else r

    def body(w_ref, g_ref, m_ref, v_ref, d_ref, nm_ref, nv_ref):
        d_ref[...], nm_ref[...], nv_ref[...] = _adamw_math(w_ref[...], g_ref[...], m_ref[...], v_ref[...])

    spec = pl.BlockSpec((tr, c), lambda i: (i, 0))
    return _pcall(
        body, name=name, grid=(r // tr,),
        in_specs=[spec] * 4, out_specs=[spec] * 3,
        out_shape=[jax.ShapeDtypeStruct((r, c), F32)] * 3,
        compiler_params=_cparams(("parallel",)),
    )(w, g, m, v)


def _sum_adamw(parts, w, m, v, name):
    r, c = w.shape
    tr = 128 if r % 128 == 0 else r

    def body(p_ref, w_ref, m_ref, v_ref, g_ref, d_ref, nm_ref, nv_ref):
        g = p_ref[0].astype(F32)
        for k in range(1, N_DEV):
            g = g + p_ref[k].astype(F32)
        g_ref[...] = g
        d_ref[...], nm_ref[...], nv_ref[...] = _adamw_math(w_ref[...], g, m_ref[...], v_ref[...])

    spec = pl.BlockSpec((tr, c), lambda i: (i, 0))
    return _pcall(
        body, name=name, grid=(r // tr,),
        in_specs=[pl.BlockSpec((N_DEV, tr, c), lambda i: (0, i, 0))] + [spec] * 3, out_specs=[spec] * 4,
        out_shape=[jax.ShapeDtypeStruct((r, c), F32)] * 4,
        compiler_params=_cparams(("parallel",)),
    )(parts, w, m, v)


def _me():
    return lax.axis_index("x"), lax.axis_index("y"), lax.axis_index("c")


def _flat(x, y, c):
    return 4 * x + 2 * y + c


def _peer(k):
    x, y, c = _me()
    return (x ^ ((k >> 2) & 1), y ^ ((k >> 1) & 1), c ^ (k & 1))


def _gather_plan(blocks):
    def copies(x_refs, out_refs, sems):
        send_sems, recv_sems, local_sems = sems
        me = _flat(*_me())
        local = [pltpu.make_async_copy(x, o.at[me], local_sems.at[a]) for a, (x, o) in enumerate(zip(x_refs, out_refs))]
        outgoing, incoming = [], []
        for k in range(1, N_DEV):
            src = _flat(*_peer(k))
            for a, (x, o) in enumerate(zip(x_refs, out_refs)):
                for slot, group in ((me, outgoing), (src, incoming)):
                    group.append(pltpu.make_async_remote_copy(
                        src_ref=x, dst_ref=o.at[slot], send_sem=send_sems.at[a, k - 1], recv_sem=recv_sems.at[a, k - 1],
                        device_id=_peer(k), device_id_type=MESH))
        return local, outgoing, incoming

    return _Comm(list(blocks), [jax.ShapeDtypeStruct((N_DEV,) + b.shape, b.dtype) for b in blocks], copies)


def _exchange_plan(sends):
    def copies(s_refs, out_refs, sems):
        send_sems, recv_sems, local_sems = sems
        me = _flat(*_me())
        local = [pltpu.make_async_copy(s.at[me], o.at[0], local_sems.at[i]) for i, (s, o) in enumerate(zip(s_refs, out_refs))]
        outgoing = []
        for k in range(1, N_DEV):
            to = _flat(*_peer(k))
            for i, (s, o) in enumerate(zip(s_refs, out_refs)):
                outgoing.append(pltpu.make_async_remote_copy(
                    src_ref=s.at[to], dst_ref=o.at[k], send_sem=send_sems.at[i, k - 1], recv_sem=recv_sems.at[i, k - 1],
                    device_id=_peer(k), device_id_type=MESH))
        return local, outgoing, outgoing

    return _Comm(list(sends), [jax.ShapeDtypeStruct(s.shape, s.dtype) for s in sends], copies)


def _comm_scratch(plan):
    n = len(plan.operands)
    return [pltpu.SemaphoreType.DMA((n, N_DEV - 1)), pltpu.SemaphoreType.DMA((n, N_DEV - 1)),
            pltpu.SemaphoreType.DMA((n,))]


def _comm_start(copies):
    local, outgoing, _ = copies
    for cp in local + outgoing:
        cp.start()


def _comm_wait(copies):
    local, outgoing, incoming = copies
    for cp in incoming:
        cp.wait_recv()
    for cp in outgoing:
        cp.wait_send()
    for cp in local:
        cp.wait()


def _run_comm(plan, name):
    n = len(plan.operands)

    def body(*refs):
        copies = plan.copies(refs[:n], refs[n:2 * n], refs[2 * n:])
        _comm_start(copies)
        _comm_wait(copies)

    return _pcall(
        body, name=name,
        in_specs=[pl.BlockSpec(memory_space=pl.ANY)] * n,
        out_specs=[pl.BlockSpec(memory_space=pl.ANY)] * n,
        out_shape=plan.out_shapes,
        scratch_shapes=_comm_scratch(plan),
    )(*plan.operands)


def _all_gather_two_level(blocks, name):
    na = len(blocks)

    def body(*refs):
        x_refs, out_refs = refs[:na], refs[na:2 * na]
        send_sems, recv_sems, local_sems = refs[2 * na:]
        x, y, c = _me()
        me, sibling = (x, y, c), (x, y, 1 - c)
        chips = [(1 - x, y), (x, 1 - y), (1 - x, 1 - y)]

        def copy(a, k, block, to, own=False):
            slot = out_refs[a].at[_flat(*block)]
            return pltpu.make_async_remote_copy(
                src_ref=x_refs[a] if own else slot, dst_ref=slot, send_sem=send_sems.at[a, k],
                recv_sem=recv_sems.at[a, k], device_id=to, device_id_type=MESH)

        mine = [pltpu.make_async_copy(x_refs[a], out_refs[a].at[_flat(*me)], local_sems.at[a]) for a in range(na)]
        first = [copy(a, 0, me, sibling, own=True) for a in range(na)]
        first += [copy(a, 1 + j, me, (*chip, c), own=True) for j, chip in enumerate(chips) for a in range(na)]
        for cp in mine + first:
            cp.start()
        passed = []
        for j, chip in enumerate(chips):
            for a in range(na):
                copy(a, 1 + j, (*chip, c), me).wait_recv()
                cp = copy(a, 4 + j, (*chip, c), sibling)
                cp.start()
                passed.append(cp)
        for a in range(na):
            copy(a, 0, sibling, me).wait_recv()
        for j, chip in enumerate(chips):
            for a in range(na):
                copy(a, 4 + j, (*chip, 1 - c), me).wait_recv()
        for cp in first + passed:
            cp.wait_send()
        for cp in mine:
            cp.wait()

    return _pcall(
        body, name=name,
        in_specs=[pl.BlockSpec(memory_space=pl.ANY)] * na,
        out_specs=[pl.BlockSpec(memory_space=pl.ANY)] * na,
        out_shape=[jax.ShapeDtypeStruct((N_DEV,) + b.shape, b.dtype) for b in blocks],
        scratch_shapes=[pltpu.SemaphoreType.DMA((na, N_DEV - 1)), pltpu.SemaphoreType.DMA((na, N_DEV - 1)),
                        pltpu.SemaphoreType.DMA((na,))],
    )(*blocks)


def _sum_slots(a, name):
    r = a.shape[1]

    def body(a_ref, o_ref):
        acc = a_ref[0]
        for d in range(1, N_DEV):
            acc = acc + a_ref[d]
        o_ref[...] = acc

    return _pcall(body, name=name, out_shape=jax.ShapeDtypeStruct((r, 128), F32), compiler_params=_cparams())(a)


def _all_reduce_small(blk, name):
    r = blk.shape[0]

    def body(x_ref, out_ref, gath, send_sems, recv_sems):
        me = _flat(*_me())
        gath[me] = x_ref[...]
        copies = []
        for k in range(1, N_DEV):
            cp = pltpu.make_async_remote_copy(
                src_ref=x_ref, dst_ref=gath.at[me],
                send_sem=send_sems.at[k - 1], recv_sem=recv_sems.at[k - 1],
                device_id=_peer(k), device_id_type=MESH)
            cp.start()
            copies.append(cp)
        for k in range(1, N_DEV):
            src = _flat(*_peer(k))
            pltpu.make_async_remote_copy(
                src_ref=x_ref, dst_ref=gath.at[src],
                send_sem=send_sems.at[k - 1], recv_sem=recv_sems.at[k - 1],
                device_id=_peer(k), device_id_type=MESH).wait_recv()
        for cp in copies:
            cp.wait_send()
        acc = gath[0]
        for d in range(1, N_DEV):
            acc = acc + gath[d]
        out_ref[...] = acc

    return _pcall(
        body, name=name,
        in_specs=[pl.BlockSpec(memory_space=pltpu.VMEM)],
        out_specs=pl.BlockSpec(memory_space=pltpu.VMEM),
        out_shape=jax.ShapeDtypeStruct((r, 128), F32),
        scratch_shapes=[pltpu.VMEM((N_DEV, r, 128), F32),
                        pltpu.SemaphoreType.DMA((N_DEV - 1,)), pltpu.SemaphoreType.DMA((N_DEV - 1,))],
    )(blk)


def _heads(vec):
    return vec.reshape(NH, 1, DH)


def _rep(vec4):
    return jnp.broadcast_to(vec4.reshape(NH, 1, 1), (NH, 1, DH))


def _onehot_lane(offset):
    m = np.zeros((NH, 1, DH), np.float32)
    for h in range(NH):
        m[h, 0, offset + h] = 1.0
    return jnp.asarray(m)


_TINY = (("gdn_conv_w", (DEPTH, 4, 96)), ("rwkv_w_up", (DEPTH, 64, 32)), ("rwkv_a_up", (DEPTH, 64, 32)),
         ("sc_conv_w", (DEPTH, 3, 32)))
_TINY_ROWS = -(-sum(int(np.prod(s)) for _, s in _TINY) // 1024) * 8


def _pack_rows(arrays, rows, fill=0.0):
    flat = jnp.concatenate([a.reshape(-1) for a in arrays])
    return jnp.pad(flat, (0, rows * 128 - flat.shape[0]), constant_values=fill).reshape(rows, 128)


def _unpack_rows(p, named_shapes):
    lead = p.shape[:-2]
    flat = p.reshape(lead + (-1,))
    out, o = {}, 0
    for n, s in named_shapes:
        size = int(np.prod(s))
        out[n] = flat[..., o:o + size].reshape(lead + tuple(s))
        o += size
    return out


def _gather_last(a):
    return jnp.transpose(a, (1, 0, 2)).reshape(a.shape[1], -1)


def _split_last(a):
    r, c8 = a.shape
    return jnp.transpose(a.reshape(r, N_DEV, c8 // N_DEV), (1, 0, 2))


_SMALL = (("pre_norm_w", (DEPTH, 1024)), ("gdn_a_log", (DEPTH, 4)), ("gdn_dt_bias", (DEPTH, 4)),
          ("gdn_norm_w", (DEPTH, 64)), ("rwkv_mu", (DEPTH, 1152)), ("rwkv_w0", (DEPTH, 256)),
          ("rwkv_a0", (DEPTH, 256)), ("rwkv_k_k", (DEPTH, 256)), ("rwkv_k_a", (DEPTH, 256)),
          ("rwkv_r_k", (DEPTH, 256)), ("rwkv_ln_w", (DEPTH, 256)), ("rwkv_ln_b", (DEPTH, 256)),
          ("gla_a_up", (DEPTH, 16, 128)), ("gla_a_bias", (DEPTH, 128)), ("gla_norm_w", (DEPTH, 64)),
          ("post_norm_w", (DEPTH, 1024)), ("loss", ()))
_SMALL_ROWS = -(-sum(int(np.prod(s)) for _, s in _SMALL) // 1024) * 8


def _wout_operand(w_out_all):
    return w_out_all.reshape(4 * NH, DH, D_MODEL).astype(BF16)


def _big_weights(w_in_all, w_out_all, l):
    big = dict(w_g=_regroup_in(w_in_all, f"regroup_in{l}"))
    if w_out_all is not None:
        big["wout_g"] = _wout_operand(w_out_all)
    return big


def _layer_params(wts, tiny, l):
    conv = _gather_last(tiny["gdn_conv_w"][:, l])
    q = {}
    q["gdn_conv"] = jnp.transpose(conv.reshape(GDN_TAPS, 12, DH), (1, 0, 2))
    q["gdn_prm"] = [_rep(wts["gdn_a_log"][l]), _rep(wts["gdn_dt_bias"][l]),
                    jnp.broadcast_to(wts["gdn_norm_w"][l].reshape(1, 1, DH), (NH, 1, DH))]
    q["gdn_cst"] = [_onehot_lane(0), _onehot_lane(NH)]
    q["rwkv_mu"] = wts["rwkv_mu"][l].reshape(18, 1, DH)
    gs, nj = GROUPS_PER_STEP, N_GROUPS // GROUPS_PER_STEP
    side_by_side = lambda a: jnp.transpose(a.reshape(-1, gs, a.shape[1], DH), (0, 2, 1, 3)).reshape(-1, a.shape[1], gs * DH)
    q["conv_blocks"] = jnp.pad(side_by_side(q["gdn_conv"]), ((0, nj - len(CONV_BLOCKS)), (0, 0), (0, 0)))
    singles = jnp.pad(q["rwkv_mu"][16:18].reshape(1, 1, 2 * DH), ((0, 0), (0, 0), (DH, DH)))
    q["mix_blocks"] = jnp.concatenate([jnp.zeros((4, 1, gs * DH), F32), side_by_side(q["rwkv_mu"][:16]),
                                       jnp.zeros((8, 1, gs * DH), F32), singles], axis=0)
    w_up = jnp.transpose(_gather_last(tiny["rwkv_w_up"][:, l]).reshape(64, NH, DH), (1, 0, 2))
    a_up = jnp.transpose(_gather_last(tiny["rwkv_a_up"][:, l]).reshape(64, NH, DH), (1, 0, 2))
    q["rwkv_prm"] = [_heads(wts["rwkv_w0"][l]), w_up, _heads(wts["rwkv_a0"][l]), a_up,
                     _heads(wts["rwkv_k_k"][l]), _heads(wts["rwkv_k_a"][l]), _heads(wts["rwkv_r_k"][l]),
                     _heads(wts["rwkv_ln_w"][l]), _heads(wts["rwkv_ln_b"][l])]
    sc = _gather_last(tiny["sc_conv_w"][:, l])
    q["sc_conv"] = jnp.transpose(sc.reshape(SC_TAPS, NH, DH), (1, 0, 2))
    gla_up = jnp.transpose(wts["gla_a_up"][l].reshape(16, NH, GLA_HEAD_K), (1, 0, 2))
    gla_up = jnp.pad(gla_up, ((0, 0), (0, DH - 16), (0, DH - GLA_HEAD_K)))
    gla_b = jnp.pad(wts["gla_a_bias"][l].reshape(NH, 1, GLA_HEAD_K), ((0, 0), (0, 0), (0, DH - GLA_HEAD_K)))
    q["gla_prm"] = [gla_up, gla_b, jnp.broadcast_to(wts["gla_norm_w"][l].reshape(1, 1, DH), (NH, 1, DH))]
    q["pre_w"] = wts["pre_norm_w"][l].reshape(1, D_MODEL)
    q["post_w"] = wts["post_norm_w"][l].reshape(1, D_MODEL)
    return q


def _mixer_inputs(p, ps):
    gdn = [(ps, 4, 0), (ps, 4, 1), (ps, 4, 2), (p, 4, G_GDN // 4 + 3), (p, 1, G_GDN_AB)]
    rwkv = [(ps, 4, G_PM // 4 + k) for k in range(4)] + [(ps, 1, G_PM_WD), (ps, 1, G_PM_AD)]
    gla = [(p, 4, G_GLA // 4 + k) for k in range(4)] + [(p, 1, G_GLA_AD)]
    return gdn, rwkv, gla


def _layer_fwd(x, q, nb, t, l, side=None, tgt=None, w_out_shard=None):
    hb, p, ps, y_sc = _norm_proj(x, q["pre_w"], q["w_g"], q["conv_blocks"], q["mix_blocks"], q["sc_conv"], t,
                                 f"norm_proj{l}")
    gdn_in, rwkv_in, gla_in = _mixer_inputs(p, ps)
    y_gdn, ck_gdn, got = _mixer_fwd(_gdn_chunk, f"gdn_fwd{l}", gdn_in, q["gdn_prm"], q["gdn_cst"], nb, t, n_kept=1,
                                    side=None if w_out_shard is None else _gather_plan([w_out_shard]))
    wout_g = q["wout_g"] if w_out_shard is None else _wout_operand(got[0])
    y_rwkv, ck_rwkv, side_res = _mixer_fwd(_rwkv_chunk, f"rwkv_fwd{l}", rwkv_in, q["rwkv_prm"], [], nb, t,
                                           first_fn=_rwkv_first_chunk, side=side, n_kept=1)
    y_gla, ck_gla, _ = _mixer_fwd(_gla_chunk, f"gla_fwd{l}", gla_in, q["gla_prm"], [], nb, t)
    ys = (y_gdn, y_rwkv, y_sc, y_gla)
    out, *res = _out_proj_norm(ys, wout_g, x, q["post_w"], f"out_proj{l}", tgt)
    saved = dict(x=x, hb=hb, p=p, ps=ps, ys=ys, out=out, ck=(ck_gdn, ck_rwkv, ck_gla), wout_g=wout_g)
    return (res[0] if tgt is None else res), saved, side_res


def _layer_bwd(dxn, q, sv, nb, t, l, side=None, exchange_own=False):
    p, ys = sv["p"], sv["ys"]
    dout, dy, d_post = _post_bwd(dxn, sv["out"], q["post_w"], sv["wout_g"], f"post_bwd{l}")
    d_wout = _dwout(ys, dout, f"dwout{l}").reshape(N_DEV, 128, D_MODEL).astype(BF16)
    gdn_in, rwkv_in, gla_in = _mixer_inputs(p, sv["ps"])
    ck_gdn, ck_rwkv, ck_gla = sv["ck"]
    g = {}

    (d_conv, dz, dab), (da_log, ddt, dnw), _ = _mixer_bwd(
        _gdn_chunk, f"gdn_bwd{l}", gdn_in, q["gdn_prm"], q["gdn_cst"], ck_gdn, dy, 0,
        [(12, F32), (4, BF16), (1, BF16)], [(0, 0), (0, 4), (0, 8), (1, 0), (2, 0)], nb, t)
    dconv_in, d_gconv = _conv_bwd(p, G_GDN, 12, q["gdn_conv"], d_conv, nb, t, f"gdn_conv_bwd{l}")
    g["gdn_conv_w"] = jnp.transpose(d_gconv.sum(1), (1, 0, 2)).reshape(GDN_TAPS, 768)
    g["gdn_a_log"] = da_log.sum((0, 2, 3))
    g["gdn_dt_bias"] = ddt.sum((0, 2, 3))
    g["gdn_norm_w"] = dnw.sum((0, 1, 2))

    (d_pm,), d_rprm, side_res = _mixer_bwd(
        _rwkv_chunk, f"rwkv_bwd{l}", rwkv_in, q["rwkv_prm"], [], ck_rwkv, dy, 1,
        [(18, F32)], [(0, 0), (0, 4), (0, 8), (0, 12), (0, 16), (0, 17)], nb, t, first_fn=_rwkv_first_chunk,
        side=side)
    dp_rwkv, d_mu = _mix_bwd(p, q["rwkv_mu"], d_pm, nb, t, f"rwkv_mix_bwd{l}")
    g["rwkv_mu"] = d_mu.sum(1).reshape(1152)
    rp = [a.sum(0) for a in d_rprm]
    g["rwkv_w0"] = rp[0].reshape(256)
    g["rwkv_w_up"] = jnp.transpose(rp[1], (1, 0, 2)).reshape(64, 256)
    g["rwkv_a0"] = rp[2].reshape(256)
    g["rwkv_a_up"] = jnp.transpose(rp[3], (1, 0, 2)).reshape(64, 256)
    for i, nme in enumerate(("rwkv_k_k", "rwkv_k_a", "rwkv_r_k", "rwkv_ln_w", "rwkv_ln_b")):
        g[nme] = rp[4 + i].reshape(256)

    dp_sc, d_scw = _sc_bwd(p, q["sc_conv"], dy, nb, t, f"sc_bwd{l}")
    g["sc_conv_w"] = jnp.transpose(d_scw.sum(1), (1, 0, 2)).reshape(SC_TAPS, 256)

    (dp_gla, dad), (d_aup, d_ab, d_gnw), _ = _mixer_bwd(
        _gla_chunk, f"gla_bwd{l}", gla_in, q["gla_prm"], [], ck_gla, dy, 3,
        [(16, BF16), (1, BF16)], [(0, 0), (0, 4), (0, 8), (0, 12), (1, 0)], nb, t)
    g["gla_a_up"] = jnp.transpose(d_aup.sum(0)[:, :16, :GLA_HEAD_K], (1, 0, 2)).reshape(16, 128)
    g["gla_a_bias"] = d_ab.sum(0)[:, 0, :GLA_HEAD_K].reshape(128)
    g["gla_norm_w"] = d_gnw.sum((0, 1, 2))

    singles = jnp.concatenate([dab, dp_rwkv[16:18], dad], axis=0)
    sources = [dconv_in, dz, dp_rwkv, dp_sc, dp_gla, singles]
    d_win = _regroup_out(_dwin(sv["hb"], sources, f"dwin{l}"), f"regroup_out{l}")
    own = _exchange_plan([d_win, d_wout]) if exchange_own else None
    dx, d_pre, got = _dh_prenorm_bwd(sources, q["w_g"], sv["x"], q["pre_w"], dxn, f"dh_bwd{l}", own)
    if exchange_own:
        d_win, d_wout = got
    g["pre_norm_w"] = d_pre.reshape(D_MODEL)
    g["post_norm_w"] = d_post.reshape(D_MODEL)
    return dx, g, d_win, d_wout, side_res


def _local_step(x, tgt, wts, tiny, w_in_all, w_out_all, later_shards=None, first_w_out=None):
    nb, t, d = x.shape
    xf = x.reshape(nb * t, d)
    overlap = later_shards is not None
    qs, saved = [], []
    big = _big_weights(w_in_all[0], w_out_all[0] if first_w_out is None else None, 0)
    for l in range(DEPTH):
        q = dict(_layer_params(wts, tiny, l), **big)
        nxt = l + 1 < DEPTH
        side = _gather_plan(later_shards[l]) if overlap and nxt else None
        xf, sv, got = _layer_fwd(xf, q, nb, t, l, side, None if nxt else tgt.reshape(nb * t, d),
                                 w_out_shard=first_w_out if l == 0 else None)
        if nxt:
            big = _big_weights(*(got if overlap else (w_in_all[l + 1], w_out_all[l + 1])), l + 1)
        qs.append(q)
        saved.append(sv)
    dxf, lpart = xf
    grads, d_win, d_wout = [None] * DEPTH, [None] * DEPTH, [None] * DEPTH
    for l in reversed(range(DEPTH)):
        side = _exchange_plan([d_win[l + 1], d_wout[l + 1]]) if overlap and l + 1 < DEPTH else None
        dxf, grads[l], d_win[l], d_wout[l], got = _layer_bwd(dxf, qs[l], saved[l], nb, t, l, side,
                                                             exchange_own=overlap and l == 0)
        if side is not None:
            d_win[l + 1], d_wout[l + 1] = got
    small = {k: jnp.stack([grads[l][k] for l in range(DEPTH)]) for k in grads[0]}
    return lpart[0, 0], dxf.reshape(nb, t, d), small, d_win, d_wout


_WEIGHTS = ("pre_norm_w", "w_in", "gdn_conv_w", "gdn_a_log", "gdn_dt_bias", "gdn_norm_w", "rwkv_mu", "rwkv_w0",
            "rwkv_w_up", "rwkv_a0", "rwkv_a_up", "rwkv_k_k", "rwkv_k_a", "rwkv_r_k", "rwkv_ln_w", "rwkv_ln_b",
            "sc_conv_w", "gla_a_up", "gla_a_bias", "gla_norm_w", "w_out", "post_norm_w")


def kernel(x, pre_norm_w, w_in, gdn_conv_w, gdn_a_log, gdn_dt_bias, gdn_norm_w, rwkv_mu, rwkv_w0, rwkv_w_up, rwkv_a0, rwkv_a_up, rwkv_k_k, rwkv_k_a, rwkv_r_k, rwkv_ln_w, rwkv_ln_b, sc_conv_w, gla_a_up, gla_a_bias, gla_norm_w, w_out, post_norm_w, loss_target, m_pre_norm_w, m_w_in, m_gdn_conv_w, m_gdn_a_log, m_gdn_dt_bias, m_gdn_norm_w, m_rwkv_mu, m_rwkv_w0, m_rwkv_w_up, m_rwkv_a0, m_rwkv_a_up, m_rwkv_k_k, m_rwkv_k_a, m_rwkv_r_k, m_rwkv_ln_w, m_rwkv_ln_b, m_sc_conv_w, m_gla_a_up, m_gla_a_bias, m_gla_norm_w, m_w_out, m_post_norm_w, v_pre_norm_w, v_w_in, v_gdn_conv_w, v_gdn_a_log, v_gdn_dt_bias, v_gdn_norm_w, v_rwkv_mu, v_rwkv_w0, v_rwkv_w_up, v_rwkv_a0, v_rwkv_a_up, v_rwkv_k_k, v_rwkv_k_a, v_rwkv_r_k, v_rwkv_ln_w, v_rwkv_ln_b, v_sc_conv_w, v_gla_a_up, v_gla_a_bias, v_gla_norm_w, v_w_out, v_post_norm_w):
    env = dict(locals())
    w = {n: env[n] for n in _WEIGHTS}
    m = {n: env["m_" + n] for n in _WEIGHTS}
    v = {n: env["v_" + n] for n in _WEIGHTS}
    tiny_names = [n for n, _ in _TINY]

    w_in_b, w_out_b = w_in.astype(BF16), w_out.astype(BF16)
    w_in_0, tiny_all = _all_gather_two_level(
        [w_in_b[0], _pack_rows([w[n] for n in tiny_names], _TINY_ROWS)], "gather_weights")
    tiny = _unpack_rows(tiny_all, _TINY)

    lpart, grad_x, small, r_win, r_wout = _local_step(
        x, loss_target, w, tiny, [w_in_0], [None], later_shards=[(w_in_b[l], w_out_b[l]) for l in range(1, DEPTH)],
        first_w_out=w_out_b[0])

    tiny_send = jnp.stack([_pack_rows([_split_last(small[n][l])[d] for n in tiny_names for l in range(DEPTH)],
                                      _TINY_ROWS) for d in range(N_DEV)])
    (r_tiny,) = _run_comm(_exchange_plan([tiny_send]), "scatter_grads")
    grads, delta, new_m, new_v = {}, {}, {}, {}
    for n, parts in (("w_in", r_win), ("w_out", r_wout)):
        res = [_sum_adamw(parts[l], w[n][l], m[n][l], v[n][l], f"adamw_{n}{l}") for l in range(DEPTH)]
        grads[n], delta[n], new_m[n], new_v[n] = [jnp.stack(o) for o in zip(*res)]
    tiny_sum = _sum_slots(r_tiny, "sum_tiny").reshape(-1)
    o = 0
    for n, s in _TINY:
        size = int(np.prod(s))
        grads[n] = tiny_sum[o:o + size].reshape(s)
        o += size

    small = dict(small)
    small["loss"] = lpart
    red = _unpack_rows(_all_reduce_small(_pack_rows([small[n] for n, _ in _SMALL], _SMALL_ROWS), "reduce_small"),
                       _SMALL)
    loss = red.pop("loss")
    grads.update(red)

    rest = [n for n in _WEIGHTS if n not in ("w_in", "w_out")]
    rest_shapes = [(n, w[n].shape) for n in rest]
    rows = -(-sum(int(np.prod(s)) for _, s in rest_shapes) // 1024) * 8
    outs = _adamw(_pack_rows([w[n] for n in rest], rows), _pack_rows([grads[n] for n in rest], rows),
                  _pack_rows([m[n] for n in rest], rows), _pack_rows([v[n] for n in rest], rows, 1.0), "adamw_rest")
    for dst, packed in zip((delta, new_m, new_v), outs):
        dst.update(_unpack_rows(packed, rest_shapes))

    return (loss, grad_x, *[grads[n] for n in _WEIGHTS], *[delta[n] for n in _WEIGHTS],
            *[new_m[n] for n in _WEIGHTS], *[new_v[n] for n in _WEIGHTS])
```

```python
import collections
import functools
import math

import numpy as np
import jax
import jax.numpy as jnp
from jax import lax
from jax.experimental import pallas as pl
from jax.experimental.pallas import tpu as pltpu

F32 = jnp.float32
BF16 = jnp.bfloat16

D_MODEL = 1024
DEPTH = 2
NH = 4
DH = 64
CH = 64
EPS = 1e-6
RWKV_GN_EPS = 64e-5
GLA_HEAD_K = 32
GLA_TAU = 16.0
GDN_TAPS = 4
SC_TAPS = 3
D_IN = 3992
N_DEV = 8
SHARD_COLS = D_IN // N_DEV

G_GDN = 0
G_RWKV = 16
G_SC = 32
G_GLA = 48
G_GDN_AB, G_RWKV_WD, G_RWKV_AD, G_GLA_AD = 64, 65, 66, 67
N_GROUPS = 68
GROUPS_PER_STEP = 4
TIME_BLOCK = 256
RWKV_EXACT_STEPS = 16

C_GDN, C_RWKV, C_SC, C_GLA = 0, 1032, 2184, 3208

ADAM_LR, ADAM_B1, ADAM_B2, ADAM_EPS, ADAM_WD, ADAM_STEP = 0.001, 0.9, 0.999, 1e-08, 0.01, 10

VMEM_LIMIT = 56 * 1024 * 1024
MESH = pl.DeviceIdType.MESH

_pcall = pl.pallas_call

_Comm = collections.namedtuple("_Comm", "operands out_shapes copies")


def _cparams(sem=None):
    if sem is None:
        return pltpu.CompilerParams(vmem_limit_bytes=VMEM_LIMIT)
    return pltpu.CompilerParams(dimension_semantics=sem, vmem_limit_bytes=VMEM_LIMIT)


def _group_segments():
    table = [(G_GDN + i, C_GDN + DH * i, DH) for i in range(16)]
    table.append((G_GDN_AB, C_GDN + 1024, 8))
    table += [(G_RWKV + i, C_RWKV + DH * i, DH) for i in range(16)]
    table += [(G_RWKV_WD, C_RWKV + 1024, DH), (G_RWKV_AD, C_RWKV + 1088, DH)]
    table += [(G_SC + 4 * j + k, C_SC + 256 * k + DH * j, DH) for j in range(NH) for k in range(4)]
    for h in range(NH):
        table += [(G_GLA + h, C_GLA + GLA_HEAD_K * h, GLA_HEAD_K),
                  (G_GLA + 4 + h, C_GLA + 128 + GLA_HEAD_K * h, GLA_HEAD_K),
                  (G_GLA + 8 + h, C_GLA + 256 + DH * h, DH),
                  (G_GLA + 12 + h, C_GLA + 512 + DH * h, DH)]
    table.append((G_GLA_AD, C_GLA + 768, 16))
    segs, padded = [], []
    for g, c, n in table:
        if n < DH:
            padded.append(g)
        a = 0
        while n > 0:
            d, off = divmod(c, SHARD_COLS)
            ln = min(n, SHARD_COLS - off)
            segs.append((g, a, d, off, ln))
            c, a, n = c + ln, a + ln, n - ln
    return segs, padded


_SEGMENTS, _PADDED_GROUPS = _group_segments()


def _dn(ta, tb):
    return (((1 if ta else 2,), (2 if tb else 1,)), ((0,), (0,)))


def _hdot(a, b, ta=False, tb=False):
    return lax.dot_general(a, b, _dn(ta, tb), precision=lax.Precision.HIGH, preferred_element_type=F32)


def _r(x):
    return x.astype(BF16)


def _rdot(a, b, ta=False, tb=False):
    return lax.dot_general(_r(a), _r(b), _dn(ta, tb), preferred_element_type=F32)


@jax.custom_vjp
def _bmm(a, b):
    return _rdot(a, b)


def _bmm_fwd(a, b):
    return _rdot(a, b), (a, b)


def _bmm_bwd(res, g):
    a, b = res
    return _rdot(g, b, tb=True), _rdot(a, g, ta=True)


_bmm.defvjp(_bmm_fwd, _bmm_bwd)


@jax.custom_vjp
def _bmm_nt(a, b):
    return _rdot(a, b, tb=True)


def _bmm_nt_fwd(a, b):
    return _rdot(a, b, tb=True), (a, b)


def _bmm_nt_bwd(res, g):
    a, b = res
    return _rdot(g, b), _rdot(g, a, ta=True)


_bmm_nt.defvjp(_bmm_nt_fwd, _bmm_nt_bwd)


@jax.custom_vjp
def _bmm_tn(a, b):
    return _rdot(a, b, ta=True)


def _bmm_tn_fwd(a, b):
    return _rdot(a, b, ta=True), (a, b)


def _bmm_tn_bwd(res, g):
    a, b = res
    return _rdot(b, g, tb=True), _rdot(a, g)


_bmm_tn.defvjp(_bmm_tn_fwd, _bmm_tn_bwd)


def _tri(n):
    i = lax.broadcasted_iota(jnp.int32, (n, n), 0)
    j = lax.broadcasted_iota(jnp.int32, (n, n), 1)
    return i >= j, i > j, i == j


def _heads_of(x, like):
    n = like.shape[0]
    if x.ndim == 2:
        return jnp.broadcast_to(x[None], (n,) + x.shape)
    seqs = x.shape[0]
    return jnp.broadcast_to(x[:, None], (seqs, n // seqs) + x.shape[1:]).reshape((n,) + x.shape[1:])


def _cumsum_rows(x):
    incl, _, _ = _tri(x.shape[-2])
    return _hdot(_heads_of(incl.astype(F32), x), x)


@jax.custom_vjp
def _inv_unit_lower(a):
    n = a.shape[-1]
    _, _, eye = _tri(n)
    pw = -a
    inv = eye.astype(F32) + pw
    for _ in range(math.ceil(math.log2(n)) - 1):
        pw = _hdot(pw, pw)
        inv = inv + _hdot(inv, pw)
    return inv


def _inv_unit_lower_fwd(a):
    inv = _inv_unit_lower(a)
    return inv, inv


def _inv_unit_lower_bwd(inv, g):
    return (-_hdot(_hdot(inv, g, ta=True), inv, tb=True),)


_inv_unit_lower.defvjp(_inv_unit_lower_fwd, _inv_unit_lower_bwd)


@jax.custom_vjp
def _inv_reuse(a, inv):
    return inv


def _inv_reuse_fwd(a, inv):
    return inv, inv


def _inv_reuse_bwd(inv, g):
    return _inv_unit_lower_bwd(inv, g)[0], jnp.zeros_like(inv)


_inv_reuse.defvjp(_inv_reuse_fwd, _inv_reuse_bwd)


def _silu(x):
    return x * jax.nn.sigmoid(x)


def _t(x):
    return jnp.swapaxes(x, -1, -2)


def _gdn_chunk(prm, cst, ins, s, tinv=None):
    a_log, dt_b, nw = prm
    m_a, m_b = cst
    cq, ck, cv, z, ab = ins
    ab = _heads_of(ab, m_a)
    incl, strict, _ = _tri(CH)
    q = _silu(cq)
    k = _silu(ck)
    v = _silu(cv)
    q = q * lax.rsqrt(jnp.sum(q * q, -1, keepdims=True) + EPS) * (DH ** -0.5)
    k = k * lax.rsqrt(jnp.sum(k * k, -1, keepdims=True) + EPS)
    a_raw = jnp.sum(ab * m_a, -1, keepdims=True)
    b_raw = jnp.sum(ab * m_b, -1, keepdims=True)
    gstep = -jnp.exp(a_log) * jax.nn.softplus(a_raw + dt_b)
    beta = jax.nn.sigmoid(b_raw)
    gc = _cumsum_rows(gstep)
    gl = jnp.sum(gstep, -2, keepdims=True)
    dec = jnp.where(incl, jnp.exp(jnp.where(incl, gc - _t(gc), 0.0)), 0.0)
    kb = k * beta
    a_mat = jnp.where(strict, _bmm_nt(kb, k) * dec, 0.0)
    tinv = _inv_unit_lower(a_mat) if tinv is None else _inv_reuse(a_mat, tinv)
    eg = jnp.exp(gc)
    u = _hdot(tinv, v * beta)
    w = _hdot(tinv, kb * eg)
    attn = _bmm_nt(q, k) * dec
    v_new = u - _bmm(w, s)
    o = _bmm(q * eg, s) + _bmm(attn, v_new)
    s_next = s * jnp.exp(gl) + _bmm_tn(k * jnp.exp(gl - gc), v_new)
    on = o * lax.rsqrt(jnp.mean(o * o, -1, keepdims=True) + EPS) * nw
    return on * _silu(z), s_next, tinv


def _gla_chunk(prm, cst, ins, st):
    a_up, a_bias, nw = prm
    q, k, v, z, ad = ins
    incl, _, _ = _tri(CH)
    la = jax.nn.log_sigmoid(_bmm(_heads_of(ad, a_up), a_up) + a_bias) * (1.0 / GLA_TAU)
    bc = _cumsum_rows(la)
    bl = jnp.sum(la, -2, keepdims=True)
    qe = q * (GLA_HEAD_K ** -0.5) * jnp.exp(bc)
    ke = k * jnp.exp(-bc)
    attn = jnp.where(incl, _bmm_nt(qe, ke), 0.0)
    o = _bmm_nt(qe, st) + _bmm(attn, v)
    st_next = st * jnp.exp(bl) + _bmm_tn(v, k * jnp.exp(bl - bc))
    on = o * lax.rsqrt(jnp.mean(o * o, -1, keepdims=True) + EPS) * nw
    return on * _silu(z), st_next


def _rwkv_chunk(prm, cst, ins, s, inv=None):
    r, v = ins[0], ins[2]
    incl, strict, _ = _tri(r.shape[-2])
    lw, kk, k2, m = _rwkv_pre(prm, ins)
    cum = _cumsum_rows(lw)
    ltot = jnp.sum(lw, -2, keepdims=True)
    n_t = -kk * jnp.exp(cum - lw)
    einv = jnp.exp(-cum)
    m_t = m * einv
    k_t = k2 * einv
    r_t = r * jnp.exp(cum)
    a_nm = jnp.where(strict, _hdot(n_t, m_t, tb=True), 0.0)
    a_nk = jnp.where(strict, _hdot(n_t, k_t, tb=True), 0.0)
    inv = _inv_unit_lower(-a_nm) if inv is None else _inv_reuse(-a_nm, inv)
    cm = _hdot(inv, _hdot(n_t, s, tb=True) + _bmm(a_nk, v))
    y = (_bmm_nt(r_t, s) + _bmm(jnp.where(incl, _hdot(r_t, m_t, tb=True), 0.0), cm)
         + _bmm(jnp.where(incl, _hdot(r_t, k_t, tb=True), 0.0), v))
    eend = jnp.exp(ltot - cum)
    s_next = s * jnp.exp(ltot) + _bmm_tn(cm, m * eend) + _bmm_tn(v, k2 * eend)
    return _rwkv_post(prm, ins, y, k2), s_next, inv


def _rwkv_pre(prm, ins):
    w0, w_up, a0, a_up, k_k, k_a = prm[:6]
    k, wd, ad = ins[1], ins[4], ins[5]
    lw = -math.exp(-0.5) * jax.nn.sigmoid(w0 + _bmm(_heads_of(jnp.tanh(wd), w_up), w_up))
    a = jax.nn.sigmoid(a0 + _bmm(_heads_of(ad, a_up), a_up))
    kk = k * k_k
    kk = kk * lax.rsqrt(jnp.sum(kk * kk, -1, keepdims=True) + EPS)
    k2 = k * (1.0 + (a - 1.0) * k_a)
    return lw, kk, k2, kk * a


def _rwkv_post(prm, ins, y, k2):
    r_k, ln_w, ln_b = prm[6:]
    r, v, z = ins[0], ins[2], ins[3]
    mean = jnp.mean(y, -1, keepdims=True)
    yc = y - mean
    var = jnp.mean(yc * yc, -1, keepdims=True)
    yn = yc * lax.rsqrt(var + RWKV_GN_EPS) * ln_w + ln_b
    bonus = jnp.sum(r * k2 * r_k, -1, keepdims=True) * v
    return (yn + bonus) * _silu(z)


@jax.custom_vjp
def _bmv(s, x):
    return jnp.sum(_r(s).astype(F32) * _r(x).astype(F32), -1, keepdims=True)


def _bmv_fwd(s, x):
    return _bmv(s, x), (s, x)


def _bmv_bwd(res, g):
    s, x = res
    return g * x, jnp.sum(_r(s).astype(F32) * _r(g).astype(F32), -2, keepdims=True)


_bmv.defvjp(_bmv_fwd, _bmv_bwd)


def _rwkv_steps(prm, cst, ins, s, steps):
    r, v = ins[0], ins[2]
    lw, kk, k2, m = _rwkv_pre(prm, ins)
    w = jnp.exp(lw)
    v_t = _t(v)
    lane = lax.broadcasted_iota(jnp.int32, (1, 1, CH), 2)
    y_t = jnp.zeros((s.shape[0], DH, CH), F32)
    for t in range(steps):
        e_t = (lane == t).astype(F32)
        row = (slice(None), slice(t, t + 1))
        sa = _bmv(s, -kk[row])
        s = s * w[row] + sa * m[row] + jnp.sum(v_t * e_t, -1, keepdims=True) * k2[row]
        y_t = y_t + _bmv(s, r[row]) * e_t
    return _rwkv_post(prm, ins, _t(y_t), k2)[:, :steps], s


def _rwkv_first_chunk(prm, cst, ins, s):
    k = RWKV_EXACT_STEPS
    y_head, s = _rwkv_steps(prm, cst, ins, s, k)
    y_tail, s, _ = _rwkv_chunk(prm, cst, [x[..., k:, :] for x in ins], s)
    return jnp.concatenate([y_head, y_tail], axis=-2), s


def _time_block(t):
    return TIME_BLOCK if t % TIME_BLOCK == 0 else t


def _load_chunk(ref, i):
    nb = ref.shape[1]
    if ref.shape[0] == NH:
        return jnp.concatenate([ref[:, b, pl.ds(i, CH), :] for b in range(nb)], axis=0)
    return ref[0, :, pl.ds(i, CH), :]


def _mixer_fwd(chunk_fn, name, ins, prm, cst, nb, t, first_fn=None, side=None, n_kept=0):
    tb = _time_block(t)
    nt, ncb, nch = t // tb, tb // CH, nb * NH
    n_in, n_prm, n_cst = len(ins), len(prm), len(cst)
    n_main, n_side = n_in + n_prm + n_cst, len(side.operands) if side else 0

    def body(*refs):
        in_refs = refs[:n_in]
        prm_refs = refs[n_in:n_in + n_prm]
        cst_refs = refs[n_in + n_prm:n_main]
        side_in = refs[n_main:n_main + n_side]
        y_ref, ck_ref = refs[n_main + n_side:n_main + n_side + 2]
        side_out = refs[n_main + n_side + 2:n_main + 2 * n_side + 2]
        s_scr = refs[n_main + 2 * n_side + 2]
        sems = refs[n_main + 2 * n_side + 3:]
        step_t = pl.program_id(0)

        if side is not None:
            @pl.when(step_t == 0)
            def _():
                _comm_start(side.copies(side_in, side_out, sems))

        @pl.when(step_t == 0)
        def _():
            s_scr[...] = jnp.zeros_like(s_scr)

        def chunk(c, i, fn=chunk_fn):
            s = s_scr[...]
            y, s_next, *kept = fn([jnp.tile(r[...], (nb, 1, 1)) for r in prm_refs],
                                  [jnp.tile(r[...], (nb, 1, 1)) for r in cst_refs],
                                  [_load_chunk(r, i) for r in in_refs], s)
            kept += [jnp.zeros_like(s)] * (n_kept - len(kept))
            for e, a in enumerate([s] + kept):
                ck_ref[c, e] = a
            for b in range(nb):
                y_ref[:, b, pl.ds(i, CH), :] = y[b * NH:(b + 1) * NH].astype(BF16)
            s_scr[...] = s_next

        def step(c, carry):
            chunk(c, pl.multiple_of(c * CH, CH))
            return carry

        if first_fn is None:
            lax.fori_loop(0, ncb, step, 0)
        else:
            @pl.when(step_t == 0)
            def _():
                chunk(0, 0, first_fn)

            @pl.when(step_t != 0)
            def _():
                chunk(0, 0)

            lax.fori_loop(1, ncb, step, 0)

        if side is not None:
            @pl.when(step_t == nt - 1)
            def _():
                _comm_wait(side.copies(side_in, side_out, sems))

    hbm = pl.BlockSpec(memory_space=pl.ANY)
    in_specs = [pl.BlockSpec((ng, nb, tb, DH), (lambda j, bi=bi: (bi, 0, j, 0))) for _, ng, bi in ins]
    in_specs += [pl.BlockSpec(p.shape, lambda j: (0, 0, 0)) for p in list(prm) + list(cst)]
    y, ck, *side_res = _pcall(
        body, name=name, grid=(nt,),
        in_specs=in_specs + [hbm] * n_side,
        out_specs=[pl.BlockSpec((NH, nb, tb, DH), lambda j: (0, 0, j, 0)),
                   pl.BlockSpec((ncb, 1 + n_kept, nch, DH, DH), lambda j: (j, 0, 0, 0, 0))] + [hbm] * n_side,
        out_shape=[jax.ShapeDtypeStruct((NH, nb, t, DH), BF16),
                   jax.ShapeDtypeStruct((t // CH, 1 + n_kept, nch, DH, DH), F32)]
        + (side.out_shapes if side else []),
        scratch_shapes=[pltpu.VMEM((nch, DH, DH), F32)] + (_comm_scratch(side) if side else []),
        compiler_params=_cparams(("arbitrary",)),
    )(*[a.reshape(a.shape[0], nb, t, DH) for a, _, _ in ins], *prm, *cst, *(side.operands if side else []))
    return y.reshape(NH, nb * t, DH), ck, side_res


def _mixer_bwd(chunk_fn, name, ins, prm, cst, ck, dy, dy_block, outs, routes, nb, t, first_fn=None, side=None):
    tb = _time_block(t)
    nt, ncb, nch = t // tb, tb // CH, nb * NH
    n_in, n_prm, n_cst, n_out = len(ins), len(prm), len(cst), len(outs)
    n_main, n_side = n_in + n_prm + n_cst + 2, len(side.operands) if side else 0
    n_kept = ck.shape[1] - 1

    def body(*refs):
        in_refs = refs[:n_in]
        prm_refs = refs[n_in:n_in + n_prm]
        cst_refs = refs[n_in + n_prm:n_in + n_prm + n_cst]
        ck_ref, dy_ref = refs[n_main - 2:n_main]
        side_in = refs[n_main:n_main + n_side]
        rest = refs[n_main + n_side:]
        out_refs = rest[:n_out]
        dprm_refs = rest[n_out:n_out + n_prm]
        side_out = rest[n_out + n_prm:n_out + n_prm + n_side]
        ds_scr = rest[n_out + n_prm + n_side]
        sems = rest[n_out + n_prm + n_side + 1:]
        step_t = pl.program_id(0)

        if side is not None:
            @pl.when(step_t == 0)
            def _():
                _comm_start(side.copies(side_in, side_out, sems))

        @pl.when(step_t == 0)
        def _():
            ds_scr[...] = jnp.zeros_like(ds_scr)
            for r in dprm_refs:
                r[...] = jnp.zeros_like(r)

        def chunk(c, i, fn=chunk_fn):
            cst_v = [jnp.tile(r[...], (nb, 1, 1)) for r in cst_refs]
            kept = [ck_ref[c, 1 + e] for e in range(n_kept)] if fn is chunk_fn else []
            _, vjp = jax.vjp(lambda p, x, s: fn(p, cst_v, x, s, *kept)[:2],
                             [jnp.tile(r[...], (nb, 1, 1)) for r in prm_refs],
                             [_load_chunk(r, i) for r in in_refs], ck_ref[c, 0])
            dy_c = jnp.concatenate([dy_ref[:, b, pl.ds(i, CH), :] for b in range(nb)], axis=0)
            d_prm, d_ins, d_s = vjp((dy_c, ds_scr[...]))
            for (oi, g0), r, g in zip(routes, in_refs, d_ins):
                o_ref = out_refs[oi]
                if r.shape[0] == NH:
                    for b in range(nb):
                        o_ref[g0:g0 + NH, b, pl.ds(i, CH), :] = g[b * NH:(b + 1) * NH].astype(o_ref.dtype)
                else:
                    o_ref[g0, :, pl.ds(i, CH), :] = g.astype(o_ref.dtype)
            for r, g in zip(dprm_refs, d_prm):
                r[...] += g
            ds_scr[...] = d_s

        def step(j, carry):
            c = ncb - 1 - j
            chunk(c, pl.multiple_of(c * CH, CH))
            return carry

        lax.fori_loop(0, ncb - 1, step, 0)
        if first_fn is None:
            chunk(0, 0)
        else:
            @pl.when(step_t == nt - 1)
            def _():
                chunk(0, 0, first_fn)

            @pl.when(step_t != nt - 1)
            def _():
                chunk(0, 0)

        if side is not None:
            @pl.when(step_t == nt - 1)
            def _():
                _comm_wait(side.copies(side_in, side_out, sems))

    def back(j):
        return nt - 1 - j

    hbm = pl.BlockSpec(memory_space=pl.ANY)
    in_specs = [pl.BlockSpec((ng, nb, tb, DH), (lambda j, bi=bi: (bi, 0, back(j), 0))) for _, ng, bi in ins]
    in_specs += [pl.BlockSpec(p.shape, lambda j: (0, 0, 0)) for p in list(prm) + list(cst)]
    in_specs += [pl.BlockSpec((ncb, 1 + n_kept, nch, DH, DH), lambda j: (back(j), 0, 0, 0, 0)),
                 pl.BlockSpec((NH, nb, tb, DH), lambda j: (dy_block, 0, back(j), 0))]
    out_specs = [pl.BlockSpec((ng, nb, tb, DH), lambda j: (0, 0, back(j), 0)) for ng, _ in outs]
    out_specs += [pl.BlockSpec((nch,) + p.shape[1:], lambda j: (0, 0, 0)) for p in prm]
    out_shape = [jax.ShapeDtypeStruct((ng, nb, t, DH), dt) for ng, dt in outs]
    out_shape += [jax.ShapeDtypeStruct((nch,) + p.shape[1:], F32) for p in prm]
    res = _pcall(
        body, name=name, grid=(nt,),
        in_specs=in_specs + [hbm] * n_side, out_specs=out_specs + [hbm] * n_side,
        out_shape=out_shape + (side.out_shapes if side else []),
        scratch_shapes=[pltpu.VMEM((nch, DH, DH), F32)] + (_comm_scratch(side) if side else []),
        compiler_params=_cparams(("arbitrary",)),
    )(*[a.reshape(a.shape[0], nb, t, DH) for a, _, _ in ins], *prm, *cst, ck, dy.reshape(dy.shape[0], nb, t, DH),
      *(side.operands if side else []))
    d_outs = [o.reshape(o.shape[0], nb * t, DH) for o in res[:n_out]]
    d_prm = [g.reshape((nb,) + p.shape) for g, p in zip(res[n_out:n_out + n_prm], prm)]
    return d_outs, d_prm, res[n_out + n_prm:]


def _shift_down(x, s):
    if s == 0:
        return x
    row = lax.broadcasted_iota(jnp.int32, x.shape, 0)
    return jnp.where(row < s, 0.0, pltpu.roll(x, s, 0))


def _shift_up(x, s):
    if s == 0:
        return x
    t = x.shape[0]
    row = lax.broadcasted_iota(jnp.int32, x.shape, 0)
    return jnp.where(row >= t - s, 0.0, pltpu.roll(x, t - s, 0))


def _conv_bwd(p, g0, ng, w, dy, nb, t, name):
    taps = w.shape[1]

    def body(x_ref, w_ref, dy_ref, dx_ref, dw_ref):
        x = x_ref[...]
        d = dy_ref[...]
        acc = w_ref[taps - 1:taps, :] * d
        dw_ref[taps - 1:taps, :] = jnp.sum(d * x, 0, keepdims=True)
        for i in range(taps - 1):
            s = taps - 1 - i
            acc = acc + w_ref[i:i + 1, :] * _shift_up(d, s)
            dw_ref[i:i + 1, :] = jnp.sum(d * _shift_down(x, s), 0, keepdims=True)
        dx_ref[...] = acc.astype(BF16)

    return _pcall(
        body, name=name, grid=(ng, nb),
        in_specs=[pl.BlockSpec((None, t, DH), lambda g, b: (g0 + g, b, 0)),
                  pl.BlockSpec((None, taps, DH), lambda g, b: (g, 0, 0)),
                  pl.BlockSpec((None, t, DH), lambda g, b: (g, b, 0))],
        out_specs=[pl.BlockSpec((None, t, DH), lambda g, b: (g, b, 0)),
                   pl.BlockSpec((None, None, taps, DH), lambda g, b: (g, b, 0, 0))],
        out_shape=[jax.ShapeDtypeStruct((ng, nb * t, DH), BF16),
                   jax.ShapeDtypeStruct((ng, nb, taps, DH), F32)],
        compiler_params=_cparams(("parallel", "parallel")),
    )(p, w, dy)


def _mix_group(g):
    return jnp.where(g < 16, G_RWKV + g, G_RWKV_WD + g - 16)


def _mix_bwd(p, mu, dy, nb, t, name):
    def body(x_ref, mu_ref, dy_ref, dx_ref, dmu_ref):
        x = x_ref[...]
        muv = mu_ref[...]
        d = dy_ref[...]
        dx_ref[...] = (d * (1.0 - muv) + _shift_up(d * muv, 1)).astype(BF16)
        dmu_ref[...] = jnp.sum(d * (_shift_down(x, 1) - x), 0, keepdims=True)

    return _pcall(
        body, name=name, grid=(18, nb),
        in_specs=[pl.BlockSpec((None, t, DH), lambda g, b: (_mix_group(g), b, 0)),
                  pl.BlockSpec((None, 1, DH), lambda g, b: (g, 0, 0)),
                  pl.BlockSpec((None, t, DH), lambda g, b: (g, b, 0))],
        out_specs=[pl.BlockSpec((None, t, DH), lambda g, b: (g, b, 0)),
                   pl.BlockSpec((None, None, 1, DH), lambda g, b: (g, b, 0, 0))],
        out_shape=[jax.ShapeDtypeStruct((18, nb * t, DH), BF16),
                   jax.ShapeDtypeStruct((18, nb, 1, DH), F32)],
        compiler_params=_cparams(("parallel", "parallel")),
    )(p, mu, dy)


def _sc_bwd(p, w, dy, nb, t, name):
    def body(p_ref, w_ref, dy_ref, dp_ref, dw_ref):
        bg, cg, xg, z = p_ref[0], p_ref[1], p_ref[2], p_ref[3]
        d = dy_ref[...]
        u = cg * xg
        u1 = _shift_down(u, 1)
        u2 = _shift_down(u, 2)
        conv = w_ref[2:3, :] * u + w_ref[1:2, :] * u1 + w_ref[0:1, :] * u2
        sg = jax.nn.sigmoid(z)
        sz = z * sg
        dp_ref[0] = (d * conv * sz).astype(BF16)
        dp_ref[3] = (d * bg * conv * (sg * (1.0 + z * (1.0 - sg)))).astype(BF16)
        dconv = d * bg * sz
        du = w_ref[2:3, :] * dconv + w_ref[1:2, :] * _shift_up(dconv, 1) + w_ref[0:1, :] * _shift_up(dconv, 2)
        dp_ref[1] = (du * xg).astype(BF16)
        dp_ref[2] = (du * cg).astype(BF16)
        dw_ref[2:3, :] = jnp.sum(dconv * u, 0, keepdims=True)
        dw_ref[1:2, :] = jnp.sum(dconv * u1, 0, keepdims=True)
        dw_ref[0:1, :] = jnp.sum(dconv * u2, 0, keepdims=True)

    return _pcall(
        body, name=name, grid=(NH, nb),
        in_specs=[pl.BlockSpec((4, t, DH), lambda j, b: (G_SC // 4 + j, b, 0)),
                  pl.BlockSpec((None, SC_TAPS, DH), lambda j, b: (j, 0, 0)),
                  pl.BlockSpec((None, t, DH), lambda j, b: (8 + j, b, 0))],
        out_specs=[pl.BlockSpec((4, t, DH), lambda j, b: (j, b, 0)),
                   pl.BlockSpec((None, None, SC_TAPS, DH), lambda j, b: (j, b, 0, 0))],
        out_shape=[jax.ShapeDtypeStruct((4 * NH, nb * t, DH), BF16),
                   jax.ShapeDtypeStruct((NH, nb, SC_TAPS, DH), F32)],
        compiler_params=_cparams(("parallel", "parallel")),
    )(p, w, dy)


def _row_tile(n):
    return 1024 if n % 1024 == 0 else n


def _regroup_in(w_all, name):
    tr = 256
    gs = GROUPS_PER_STEP

    def body(w_ref, o_ref):
        for g in _PADDED_GROUPS:
            o_ref[g // gs, :, DH * (g % gs):DH * (g % gs + 1)] = jnp.zeros((tr, DH), BF16)
        for g, a, d, off, ln in _SEGMENTS:
            lane = DH * (g % gs) + a
            o_ref[g // gs, :, lane:lane + ln] = w_ref[d, :, off:off + ln].astype(BF16)

    return _pcall(
        body, name=name, grid=(D_MODEL // tr,),
        in_specs=[pl.BlockSpec((N_DEV, tr, SHARD_COLS), lambda i: (0, i, 0))],
        out_specs=pl.BlockSpec((N_GROUPS // gs, tr, gs * DH), lambda i: (0, i, 0)),
        out_shape=jax.ShapeDtypeStruct((N_GROUPS // gs, D_MODEL, gs * DH), BF16),
        compiler_params=_cparams(("parallel",)),
    )(w_all)


def _regroup_out(dwg, name):
    tr = 256
    gs = GROUPS_PER_STEP

    def body(g_ref, o_ref):
        for g, a, d, off, ln in _SEGMENTS:
            lane = DH * (g % gs) + a
            o_ref[d, :, off:off + ln] = g_ref[g // gs, :, lane:lane + ln].astype(BF16)

    return _pcall(
        body, name=name, grid=(D_MODEL // tr,),
        in_specs=[pl.BlockSpec((N_GROUPS // gs, tr, gs * DH), lambda i: (0, i, 0))],
        out_specs=pl.BlockSpec((N_DEV, tr, SHARD_COLS), lambda i: (0, i, 0)),
        out_shape=jax.ShapeDtypeStruct((N_DEV, D_MODEL, SHARD_COLS), BF16),
        compiler_params=_cparams(("parallel",)),
    )(dwg)


CONV_BLOCKS = (0, 1, 2)
MIX_BLOCKS = (4, 5, 6, 7, 16)
SHIFTED_BLOCKS = {0: 0, 1: 1, 2: 2, 4: 3, 5: 4, 6: 5, 7: 6, 16: 7}
G_PM = 12
G_PM_WD, G_PM_AD = 29, 30
SC_BLOCK0 = G_SC // GROUPS_PER_STEP


def _norm_proj(x, pre_w, w_g, conv_w, mix_mu, sc_w, t, name):
    n = x.shape[0]
    tm = _row_tile(n)
    gs = GROUPS_PER_STEP
    nj = N_GROUPS // gs
    assert t % tm == 0, (t, tm)
    per_seq = t // tm

    def shifted_block(j):
        out = jnp.int32(len(SHIFTED_BLOCKS) - 1)
        for jj in sorted(SHIFTED_BLOCKS, reverse=True):
            out = jnp.where(j < jj, max(SHIFTED_BLOCKS[jj] - 1, 0), out)
        for jj, b in SHIFTED_BLOCKS.items():
            out = jnp.where(j == jj, b, out)
        return out

    def body(x_ref, pw_ref, w_ref, cw_ref, mu_ref, scw_ref, h_ref, p_ref, s_ref, ysc_ref, carry):
        i, j = pl.program_id(0), pl.program_id(1)

        @pl.when(j == 0)
        def _():
            xv = x_ref[...]
            h = xv * lax.rsqrt(jnp.mean(xv * xv, -1, keepdims=True) + EPS) * pw_ref[...]
            h_ref[...] = h.astype(BF16)

        r = jnp.dot(h_ref[...], w_ref[...], preferred_element_type=F32)
        for k in range(gs):
            p_ref[k] = r[:, DH * k:DH * (k + 1)]

        def shifts():
            first = (i % per_seq) == 0
            tail = jnp.where(first, 0.0, carry[j])
            above = jnp.concatenate([tail, jnp.zeros((tm - 8, gs * DH), F32)], axis=0)
            row = lax.broadcasted_iota(jnp.int32, r.shape, 0)
            out = [jnp.where(row < s, pltpu.roll(above, tm - 8 + s, 0), pltpu.roll(r, s, 0)) for s in (1, 2, 3)]
            carry[j] = r[tm - 8:, :]
            return out

        def store(v):
            for k in range(gs):
                s_ref[k] = v[:, DH * k:DH * (k + 1)]

        @pl.when(functools.reduce(jnp.logical_or, [j == b for b in CONV_BLOCKS]))
        def _():
            s1, s2, s3 = shifts()
            store(cw_ref[3:4, :] * r + cw_ref[0:1, :] * s3 + cw_ref[1:2, :] * s2 + cw_ref[2:3, :] * s1)

        @pl.when(functools.reduce(jnp.logical_or, [j == b for b in MIX_BLOCKS]))
        def _():
            s1 = shifts()[0]
            store(r + mu_ref[...] * (s1 - r))

        @pl.when((j >= SC_BLOCK0) & (j < SC_BLOCK0 + NH))
        def _():
            first = (i % per_seq) == 0
            tail = jnp.where(first, 0.0, carry[j])
            u = r[:, DH:2 * DH] * r[:, 2 * DH:3 * DH]
            above = jnp.concatenate([tail[:, DH:2 * DH] * tail[:, 2 * DH:3 * DH], jnp.zeros((tm - 8, DH), F32)], axis=0)
            row = lax.broadcasted_iota(jnp.int32, u.shape, 0)
            u1, u2 = [jnp.where(row < s, pltpu.roll(above, tm - 8 + s, 0), pltpu.roll(u, s, 0)) for s in (1, 2)]
            conv = scw_ref[2:3, :] * u + scw_ref[1:2, :] * u1 + scw_ref[0:1, :] * u2
            ysc_ref[...] = (r[:, :DH] * conv * _silu(r[:, 3 * DH:])).astype(BF16)
            carry[j] = r[tm - 8:, :]

    n_shifted = len(SHIFTED_BLOCKS) * gs
    return _pcall(
        body, name=name, grid=(n // tm, nj),
        in_specs=[pl.BlockSpec((tm, D_MODEL), lambda i, j: (i, 0)),
                  pl.BlockSpec((1, D_MODEL), lambda i, j: (0, 0)),
                  pl.BlockSpec((None, D_MODEL, gs * DH), lambda i, j: (j, 0, 0)),
                  pl.BlockSpec((None, GDN_TAPS, gs * DH), lambda i, j: (j, 0, 0)),
                  pl.BlockSpec((None, 1, gs * DH), lambda i, j: (j, 0, 0)),
                  pl.BlockSpec((None, SC_TAPS, DH), lambda i, j: (jnp.clip(j - SC_BLOCK0, 0, NH - 1), 0, 0))],
        out_specs=[pl.BlockSpec((tm, D_MODEL), lambda i, j: (i, 0)),
                   pl.BlockSpec((gs, tm, DH), lambda i, j: (j, i, 0)),
                   pl.BlockSpec((gs, tm, DH), lambda i, j: (shifted_block(j), i, 0)),
                   pl.BlockSpec((None, tm, DH), lambda i, j: (jnp.clip(j - SC_BLOCK0, 0, NH - 1), i, 0))],
        out_shape=[jax.ShapeDtypeStruct((n, D_MODEL), BF16),
                   jax.ShapeDtypeStruct((N_GROUPS, n, DH), F32),
                   jax.ShapeDtypeStruct((n_shifted, n, DH), F32),
                   jax.ShapeDtypeStruct((NH, n, DH), BF16)],
        scratch_shapes=[pltpu.VMEM((nj, 8, gs * DH), F32)],
        compiler_params=_cparams(("arbitrary", "arbitrary")),
    )(x, pre_w, w_g, conv_w, mix_mu, sc_w)


def _out_proj_norm(ys, wout_g, x, post_w, name, tgt=None):
    n = x.shape[0]
    tm = _row_tile(n)

    def body(y0, y1, y2, y3, w_ref, x_ref, pw_ref, *rest):
        y = jnp.concatenate([yr[h] for yr in (y0, y1, y2, y3) for h in range(NH)], axis=-1)
        acc = jnp.dot(y, w_ref[...], preferred_element_type=F32)
        xn = x_ref[...] + acc * lax.rsqrt(jnp.mean(acc * acc, -1, keepdims=True) + EPS) * pw_ref[...]
        if tgt is None:
            out_ref, xn_ref = rest
            xn_ref[...] = xn
        else:
            t_ref, out_ref, dx_ref, l_ref = rest

            @pl.when(pl.program_id(0) == 0)
            def _():
                l_ref[...] = jnp.zeros_like(l_ref)

            e = xn - t_ref[...]
            dx_ref[...] = e * (1.0 / D_MODEL)
            l_ref[...] += jnp.sum(jnp.sum(e * e, -1, keepdims=True), 0, keepdims=True) * (0.5 / D_MODEL)
        out_ref[...] = acc

    yspec = pl.BlockSpec((NH, tm, DH), lambda i: (0, i, 0))
    rows = pl.BlockSpec((tm, D_MODEL), lambda i: (i, 0))
    full = jax.ShapeDtypeStruct((n, D_MODEL), F32)
    head = tgt is not None
    return _pcall(
        body, name=name, grid=(n // tm,),
        in_specs=[yspec] * 4 + [pl.BlockSpec((D_MODEL, D_MODEL), lambda i: (0, 0)), rows,
                                pl.BlockSpec((1, D_MODEL), lambda i: (0, 0))] + [rows] * head,
        out_specs=[rows, rows] + [pl.BlockSpec((1, 128), lambda i: (0, 0))] * head,
        out_shape=[full, full] + [jax.ShapeDtypeStruct((1, 128), F32)] * head,
        compiler_params=_cparams(("arbitrary",) if head else ("parallel",)),
    )(*ys, wout_g.reshape(D_MODEL, D_MODEL), x, post_w, *([tgt] if head else []))


def _rmsnorm_bwd(xv, w, d):
    r = lax.rsqrt(jnp.mean(xv * xv, -1, keepdims=True) + EPS)
    xh = xv * r
    dxh = d * w
    dx = r * (dxh - xh * jnp.mean(dxh * xh, -1, keepdims=True))
    return dx, d * xh


def _post_bwd(dxn, out, post_w, wout_g, name):
    n = dxn.shape[0]
    tm = _row_tile(n)

    def body(d_ref, o_ref, pw_ref, w_ref, do_ref, dy_ref, dpw_ref):
        @pl.when(pl.program_id(0) == 0)
        def _():
            dpw_ref[...] = jnp.zeros_like(dpw_ref)

        dout, dw_rows = _rmsnorm_bwd(o_ref[...], pw_ref[...], d_ref[...])
        dpw_ref[...] += jnp.sum(dw_rows, 0, keepdims=True)
        db = dout.astype(BF16)
        do_ref[...] = db
        dy = lax.dot_general(db, w_ref[...], (((1,), (1,)), ((), ())), preferred_element_type=F32)
        for g in range(4 * NH):
            dy_ref[g] = dy[:, DH * g:DH * (g + 1)]

    rows = pl.BlockSpec((tm, D_MODEL), lambda i: (i, 0))
    vec = pl.BlockSpec((1, D_MODEL), lambda i: (0, 0))
    return _pcall(
        body, name=name, grid=(n // tm,),
        in_specs=[rows, rows, vec, pl.BlockSpec((D_MODEL, D_MODEL), lambda i: (0, 0))],
        out_specs=[rows, pl.BlockSpec((4 * NH, tm, DH), lambda i: (0, i, 0)), vec],
        out_shape=[jax.ShapeDtypeStruct((n, D_MODEL), BF16),
                   jax.ShapeDtypeStruct((4 * NH, n, DH), F32),
                   jax.ShapeDtypeStruct((1, D_MODEL), F32)],
        compiler_params=_cparams(("arbitrary",)),
    )(dxn, out, post_w, wout_g.reshape(D_MODEL, D_MODEL))


def _dwout(ys, dout, name):
    n = dout.shape[0]
    tm = _row_tile(n)

    def body(y0, y1, y2, y3, d_ref, dw_ref):
        @pl.when(pl.program_id(0) == 0)
        def _():
            dw_ref[...] = jnp.zeros_like(dw_ref)

        y = jnp.concatenate([yr[h] for yr in (y0, y1, y2, y3) for h in range(NH)], axis=-1)
        dw_ref[...] += lax.dot_general(y, d_ref[...], (((0,), (0,)), ((), ())), preferred_element_type=F32)

    yspec = pl.BlockSpec((NH, tm, DH), lambda i: (0, i, 0))
    return _pcall(
        body, name=name, grid=(n // tm,),
        in_specs=[yspec] * 4 + [pl.BlockSpec((tm, D_MODEL), lambda i: (i, 0))],
        out_specs=pl.BlockSpec((D_MODEL, D_MODEL), lambda i: (0, 0)),
        out_shape=jax.ShapeDtypeStruct((D_MODEL, D_MODEL), F32),
        compiler_params=_cparams(("arbitrary",)),
    )(*ys, dout)


def _source_specs(sources, rows_first):
    gs = GROUPS_PER_STEP
    spans, specs, j0 = [], [], 0
    for a in sources:
        nblk = a.shape[0] // gs
        spans.append((j0, j0 + nblk))
        shape = (gs, _row_tile(a.shape[1]), DH)

        def blk(j, j0=j0, nblk=nblk):
            return jnp.clip(j - j0, 0, nblk - 1)

        if rows_first:
            specs.append(pl.BlockSpec(shape, (lambda i, j, blk=blk: (blk(j), i, 0))))
        else:
            specs.append(pl.BlockSpec(shape, (lambda j, i, blk=blk: (blk(j), i, 0))))
        j0 += nblk
    return spans, specs


def _dh_prenorm_bwd(sources, w_g, x, pre_w, dxn, name, side=None):
    n = x.shape[0]
    tm = _row_tile(n)
    gs = GROUPS_PER_STEP
    nj = N_GROUPS // gs
    ni = n // tm
    spans, src_specs = _source_specs(sources, True)
    ns = len(sources)
    n_side = len(side.operands) if side else 0

    def body(*refs):
        src = refs[:ns]
        w_ref, x_ref, pw_ref, d_ref = refs[ns:ns + 4]
        side_in = refs[ns + 4:ns + 4 + n_side]
        dx_ref, dpw_ref = refs[ns + 4 + n_side:ns + 6 + n_side]
        side_out = refs[ns + 6 + n_side:ns + 6 + 2 * n_side]
        acc = refs[ns + 6 + 2 * n_side]
        sems = refs[ns + 7 + 2 * n_side:]
        i, j = pl.program_id(0), pl.program_id(1)

        if side is not None:
            @pl.when((i == 0) & (j == 0))
            def _():
                _comm_start(side.copies(side_in, side_out, sems))

        @pl.when((i == 0) & (j == 0))
        def _():
            dpw_ref[...] = jnp.zeros_like(dpw_ref)

        @pl.when(j == 0)
        def _():
            acc[...] = jnp.zeros_like(acc)

        for s_ref, (lo, hi) in zip(src, spans):
            @pl.when((j >= lo) & (j < hi))
            def _(s_ref=s_ref):
                four = jnp.concatenate([s_ref[k] for k in range(gs)], axis=-1)
                acc[...] += lax.dot_general(four, w_ref[...], (((1,), (1,)), ((), ())), preferred_element_type=F32)

        @pl.when(j == nj - 1)
        def _():
            dx, dw_rows = _rmsnorm_bwd(x_ref[...], pw_ref[...], acc[...])
            dx_ref[...] = d_ref[...] + dx
            dpw_ref[...] += jnp.sum(dw_rows, 0, keepdims=True)

        if side is not None:
            @pl.when((i == ni - 1) & (j == nj - 1))
            def _():
                _comm_wait(side.copies(side_in, side_out, sems))

    hbm = pl.BlockSpec(memory_space=pl.ANY)
    rows = pl.BlockSpec((tm, D_MODEL), lambda i, j: (i, 0))
    vec = pl.BlockSpec((1, D_MODEL), lambda i, j: (0, 0))
    dx, dpw, *side_res = _pcall(
        body, name=name, grid=(ni, nj),
        in_specs=src_specs + [pl.BlockSpec((None, D_MODEL, gs * DH), lambda i, j: (j, 0, 0)), rows, vec, rows]
        + [hbm] * n_side,
        out_specs=[rows, vec] + [hbm] * n_side,
        out_shape=[jax.ShapeDtypeStruct((n, D_MODEL), F32), jax.ShapeDtypeStruct((1, D_MODEL), F32)]
        + (side.out_shapes if side else []),
        scratch_shapes=[pltpu.VMEM((tm, D_MODEL), F32)] + (_comm_scratch(side) if side else []),
        compiler_params=_cparams(("arbitrary", "arbitrary")),
    )(*sources, w_g, x, pre_w, dxn, *(side.operands if side else []))
    return dx, dpw, side_res


def _dwin(hb, sources, name):
    n = hb.shape[0]
    tm = _row_tile(n)
    gs = GROUPS_PER_STEP
    ni, nj = n // tm, N_GROUPS // gs
    spans, src_specs = _source_specs(sources, True)
    ns = len(sources)

    def body(*refs):
        h_ref = refs[0]
        src = refs[1:1 + ns]
        out_ref, acc, sem = refs[1 + ns:]
        i, j = pl.program_id(0), pl.program_id(1)

        @pl.when((i == 0) & (j == 0))
        def _():
            acc[...] = jnp.zeros_like(acc)

        h = h_ref[...]
        for s_ref, (lo, hi) in zip(src, spans):
            @pl.when((j >= lo) & (j < hi))
            def _(s_ref=s_ref):
                four = jnp.concatenate([s_ref[k] for k in range(gs)], axis=-1)
                acc[j] += jnp.dot(h, four, preferred_element_type=F32)

        @pl.when((i == ni - 1) & (j == nj - 1))
        def _():
            done = pltpu.make_async_copy(acc, out_ref, sem)
            done.start()
            done.wait()

    return _pcall(
        body, name=name, grid=(ni, nj),
        in_specs=[pl.BlockSpec((D_MODEL, tm), lambda i, j: (0, i))] + src_specs,
        out_specs=pl.BlockSpec(memory_space=pl.ANY),
        out_shape=jax.ShapeDtypeStruct((nj, D_MODEL, gs * DH), F32),
        scratch_shapes=[pltpu.VMEM((nj, D_MODEL, gs * DH), F32), pltpu.SemaphoreType.DMA],
        compiler_params=_cparams(("arbitrary", "arbitrary")),
    )(jnp.transpose(hb), *sources)


def _adamw_math(w, g, m, v):
    c1 = 1.0 - ADAM_B1 ** ADAM_STEP
    c2 = 1.0 - ADAM_B2 ** ADAM_STEP
    nm = ADAM_B1 * m + (1.0 - ADAM_B1) * g
    nv = ADAM_B2 * v + (1.0 - ADAM_B2) * (g * g)
    return -ADAM_LR * ((nm / c1) / (jnp.sqrt(nv / c2) + ADAM_EPS) + ADAM_WD * w), nm, nv


def _adamw(w, g, m, v, name):
    r, c = w.shape
    tr = 256 if r % 256 == 0 else r

    def body(w_ref, g_ref, m_ref, v_ref, d_ref, nm_ref, nv_ref):
        d_ref[...], nm_ref[...], nv_ref[...] = _adamw_math(w_ref[...], g_ref[...], m_ref[...], v_ref[...])

    spec = pl.BlockSpec((tr, c), lambda i: (i, 0))
    return _pcall(
        body, name=name, grid=(r // tr,),
        in_specs=[spec] * 4, out_specs=[spec] * 3,
        out_shape=[jax.ShapeDtypeStruct((r, c), F32)] * 3,
        compiler_params=_cparams(("parallel",)),
    )(w, g, m, v)


def _sum_adamw(parts, w, m, v, name):
    r, c = w.shape
    tr = 128 if r % 128 == 0 else r

    def body(p_ref, w_ref, m_ref, v_ref, g_ref, d_ref, nm_ref, nv_ref):
        g = p_ref[0].astype(F32)
        for k in range(1, N_DEV):
            g = g + p_ref[k].astype(F32)
        g_ref[...] = g
        d_ref[...], nm_ref[...], nv_ref[...] = _adamw_math(w_ref[...], g, m_ref[...], v_ref[...])

    spec = pl.BlockSpec((tr, c), lambda i: (i, 0))
    return _pcall(
        body, name=name, grid=(r // tr,),
        in_specs=[pl.BlockSpec((N_DEV, tr, c), lambda i: (0, i, 0))] + [spec] * 3, out_specs=[spec] * 4,
        out_shape=[jax.ShapeDtypeStruct((r, c), F32)] * 4,
        compiler_params=_cparams(("parallel",)),
    )(parts, w, m, v)


def _me():
    return lax.axis_index("x"), lax.axis_index("y"), lax.axis_index("c")


def _flat(x, y, c):
    return 4 * x + 2 * y + c


def _peer(k):
    x, y, c = _me()
    return (x ^ ((k >> 2) & 1), y ^ ((k >> 1) & 1), c ^ (k & 1))


def _gather_plan(blocks):
    def copies(x_refs, out_refs, sems):
        send_sems, recv_sems, local_sems = sems
        me = _flat(*_me())
        local = [pltpu.make_async_copy(x, o.at[me], local_sems.at[a]) for a, (x, o) in enumerate(zip(x_refs, out_refs))]
        outgoing, incoming = [], []
        for k in range(1, N_DEV):
            src = _flat(*_peer(k))
            for a, (x, o) in enumerate(zip(x_refs, out_refs)):
                for slot, group in ((me, outgoing), (src, incoming)):
                    group.append(pltpu.make_async_remote_copy(
                        src_ref=x, dst_ref=o.at[slot], send_sem=send_sems.at[a, k - 1], recv_sem=recv_sems.at[a, k - 1],
                        device_id=_peer(k), device_id_type=MESH))
        return local, outgoing, incoming

    return _Comm(list(blocks), [jax.ShapeDtypeStruct((N_DEV,) + b.shape, b.dtype) for b in blocks], copies)


def _exchange_plan(sends):
    def copies(s_refs, out_refs, sems):
        send_sems, recv_sems, local_sems = sems
        me = _flat(*_me())
        local = [pltpu.make_async_copy(s.at[me], o.at[0], local_sems.at[i]) for i, (s, o) in enumerate(zip(s_refs, out_refs))]
        outgoing = []
        for k in range(1, N_DEV):
            to = _flat(*_peer(k))
            for i, (s, o) in enumerate(zip(s_refs, out_refs)):
                outgoing.append(pltpu.make_async_remote_copy(
                    src_ref=s.at[to], dst_ref=o.at[k], send_sem=send_sems.at[i, k - 1], recv_sem=recv_sems.at[i, k - 1],
                    device_id=_peer(k), device_id_type=MESH))
        return local, outgoing, outgoing

    return _Comm(list(sends), [jax.ShapeDtypeStruct(s.shape, s.dtype) for s in sends], copies)


def _comm_scratch(plan):
    n = len(plan.operands)
    return [pltpu.SemaphoreType.DMA((n, N_DEV - 1)), pltpu.SemaphoreType.DMA((n, N_DEV - 1)),
            pltpu.SemaphoreType.DMA((n,))]


def _comm_start(copies):
    local, outgoing, _ = copies
    for cp in local + outgoing:
        cp.start()


def _comm_wait(copies):
    local, outgoing, incoming = copies
    for cp in incoming:
        cp.wait_recv()
    for cp in outgoing:
        cp.wait_send()
    for cp in local:
        cp.wait()


def _run_comm(plan, name):
    n = len(plan.operands)

    def body(*refs):
        copies = plan.copies(refs[:n], refs[n:2 * n], refs[2 * n:])
        _comm_start(copies)
        _comm_wait(copies)

    return _pcall(
        body, name=name,
        in_specs=[pl.BlockSpec(memory_space=pl.ANY)] * n,
        out_specs=[pl.BlockSpec(memory_space=pl.ANY)] * n,
        out_shape=plan.out_shapes,
        scratch_shapes=_comm_scratch(plan),
    )(*plan.operands)


def _all_gather_two_level(blocks, name):
    na = len(blocks)

    def body(*refs):
        x_refs, out_refs = refs[:na], refs[na:2 * na]
        send_sems, recv_sems, local_sems = refs[2 * na:]
        x, y, c = _me()
        me, sibling = (x, y, c), (x, y, 1 - c)
        chips = [(1 - x, y), (x, 1 - y), (1 - x, 1 - y)]

        def copy(a, k, block, to, own=False):
            slot = out_refs[a].at[_flat(*block)]
            return pltpu.make_async_remote_copy(
                src_ref=x_refs[a] if own else slot, dst_ref=slot, send_sem=send_sems.at[a, k],
                recv_sem=recv_sems.at[a, k], device_id=to, device_id_type=MESH)

        mine = [pltpu.make_async_copy(x_refs[a], out_refs[a].at[_flat(*me)], local_sems.at[a]) for a in range(na)]
        first = [copy(a, 0, me, sibling, own=True) for a in range(na)]
        first += [copy(a, 1 + j, me, (*chip, c), own=True) for j, chip in enumerate(chips) for a in range(na)]
        for cp in mine + first:
            cp.start()
        passed = []
        for j, chip in enumerate(chips):
            for a in range(na):
                copy(a, 1 + j, (*chip, c), me).wait_recv()
                cp = copy(a, 4 + j, (*chip, c), sibling)
                cp.start()
                passed.append(cp)
        for a in range(na):
            copy(a, 0, sibling, me).wait_recv()
        for j, chip in enumerate(chips):
            for a in range(na):
                copy(a, 4 + j, (*chip, 1 - c), me).wait_recv()
        for cp in first + passed:
            cp.wait_send()
        for cp in mine:
            cp.wait()

    return _pcall(
        body, name=name,
        in_specs=[pl.BlockSpec(memory_space=pl.ANY)] * na,
        out_specs=[pl.BlockSpec(memory_space=pl.ANY)] * na,
        out_shape=[jax.ShapeDtypeStruct((N_DEV,) + b.shape, b.dtype) for b in blocks],
        scratch_shapes=[pltpu.SemaphoreType.DMA((na, N_DEV - 1)), pltpu.SemaphoreType.DMA((na, N_DEV - 1)),
                        pltpu.SemaphoreType.DMA((na,))],
    )(*blocks)


def _sum_slots(a, name):
    r = a.shape[1]

    def body(a_ref, o_ref):
        acc = a_ref[0]
        for d in range(1, N_DEV):
            acc = acc + a_ref[d]
        o_ref[...] = acc

    return _pcall(body, name=name, out_shape=jax.ShapeDtypeStruct((r, 128), F32), compiler_params=_cparams())(a)


def _all_reduce_small(blk, name):
    r = blk.shape[0]

    def body(x_ref, out_ref, gath, send_sems, recv_sems):
        me = _flat(*_me())
        gath[me] = x_ref[...]
        copies = []
        for k in range(1, N_DEV):
            cp = pltpu.make_async_remote_copy(
                src_ref=x_ref, dst_ref=gath.at[me],
                send_sem=send_sems.at[k - 1], recv_sem=recv_sems.at[k - 1],
                device_id=_peer(k), device_id_type=MESH)
            cp.start()
            copies.append(cp)
        for k in range(1, N_DEV):
            src = _flat(*_peer(k))
            pltpu.make_async_remote_copy(
                src_ref=x_ref, dst_ref=gath.at[src],
                send_sem=send_sems.at[k - 1], recv_sem=recv_sems.at[k - 1],
                device_id=_peer(k), device_id_type=MESH).wait_recv()
        for cp in copies:
            cp.wait_send()
        acc = gath[0]
        for d in range(1, N_DEV):
            acc = acc + gath[d]
        out_ref[...] = acc

    return _pcall(
        body, name=name,
        in_specs=[pl.BlockSpec(memory_space=pltpu.VMEM)],
        out_specs=pl.BlockSpec(memory_space=pltpu.VMEM),
        out_shape=jax.ShapeDtypeStruct((r, 128), F32),
        scratch_shapes=[pltpu.VMEM((N_DEV, r, 128), F32),
                        pltpu.SemaphoreType.DMA((N_DEV - 1,)), pltpu.SemaphoreType.DMA((N_DEV - 1,))],
    )(blk)


def _heads(vec):
    return vec.reshape(NH, 1, DH)


def _rep(vec4):
    return jnp.broadcast_to(vec4.reshape(NH, 1, 1), (NH, 1, DH))


def _onehot_lane(offset):
    m = np.zeros((NH, 1, DH), np.float32)
    for h in range(NH):
        m[h, 0, offset + h] = 1.0
    return jnp.asarray(m)


_TINY = (("gdn_conv_w", (DEPTH, 4, 96)), ("rwkv_w_up", (DEPTH, 64, 32)), ("rwkv_a_up", (DEPTH, 64, 32)),
         ("sc_conv_w", (DEPTH, 3, 32)))
_TINY_ROWS = -(-sum(int(np.prod(s)) for _, s in _TINY) // 1024) * 8


def _pack_rows(arrays, rows, fill=0.0):
    flat = jnp.concatenate([a.reshape(-1) for a in arrays])
    return jnp.pad(flat, (0, rows * 128 - flat.shape[0]), constant_values=fill).reshape(rows, 128)


def _unpack_rows(p, named_shapes):
    lead = p.shape[:-2]
    flat = p.reshape(lead + (-1,))
    out, o = {}, 0
    for n, s in named_shapes:
        size = int(np.prod(s))
        out[n] = flat[..., o:o + size].reshape(lead + tuple(s))
        o += size
    return out


def _gather_last(a):
    return jnp.transpose(a, (1, 0, 2)).reshape(a.shape[1], -1)


def _split_last(a):
    r, c8 = a.shape
    return jnp.transpose(a.reshape(r, N_DEV, c8 // N_DEV), (1, 0, 2))


_SMALL = (("pre_norm_w", (DEPTH, 1024)), ("gdn_a_log", (DEPTH, 4)), ("gdn_dt_bias", (DEPTH, 4)),
          ("gdn_norm_w", (DEPTH, 64)), ("rwkv_mu", (DEPTH, 1152)), ("rwkv_w0", (DEPTH, 256)),
          ("rwkv_a0", (DEPTH, 256)), ("rwkv_k_k", (DEPTH, 256)), ("rwkv_k_a", (DEPTH, 256)),
          ("rwkv_r_k", (DEPTH, 256)), ("rwkv_ln_w", (DEPTH, 256)), ("rwkv_ln_b", (DEPTH, 256)),
          ("gla_a_up", (DEPTH, 16, 128)), ("gla_a_bias", (DEPTH, 128)), ("gla_norm_w", (DEPTH, 64)),
          ("post_norm_w", (DEPTH, 1024)), ("loss", ()))
_SMALL_ROWS = -(-sum(int(np.prod(s)) for _, s in _SMALL) // 1024) * 8


def _wout_operand(w_out_all):
    return w_out_all.reshape(4 * NH, DH, D_MODEL).astype(BF16)


def _big_weights(w_in_all, w_out_all, l):
    big = dict(w_g=_regroup_in(w_in_all, f"regroup_in{l}"))
    if w_out_all is not None:
        big["wout_g"] = _wout_operand(w_out_all)
    return big


def _layer_params(wts, tiny, l):
    conv = _gather_last(tiny["gdn_conv_w"][:, l])
    q = {}
    q["gdn_conv"] = jnp.transpose(conv.reshape(GDN_TAPS, 12, DH), (1, 0, 2))
    q["gdn_prm"] = [_rep(wts["gdn_a_log"][l]), _rep(wts["gdn_dt_bias"][l]),
                    jnp.broadcast_to(wts["gdn_norm_w"][l].reshape(1, 1, DH), (NH, 1, DH))]
    q["gdn_cst"] = [_onehot_lane(0), _onehot_lane(NH)]
    q["rwkv_mu"] = wts["rwkv_mu"][l].reshape(18, 1, DH)
    gs, nj = GROUPS_PER_STEP, N_GROUPS // GROUPS_PER_STEP
    side_by_side = lambda a: jnp.transpose(a.reshape(-1, gs, a.shape[1], DH), (0, 2, 1, 3)).reshape(-1, a.shape[1], gs * DH)
    q["conv_blocks"] = jnp.pad(side_by_side(q["gdn_conv"]), ((0, nj - len(CONV_BLOCKS)), (0, 0), (0, 0)))
    singles = jnp.pad(q["rwkv_mu"][16:18].reshape(1, 1, 2 * DH), ((0, 0), (0, 0), (DH, DH)))
    q["mix_blocks"] = jnp.concatenate([jnp.zeros((4, 1, gs * DH), F32), side_by_side(q["rwkv_mu"][:16]),
                                       jnp.zeros((8, 1, gs * DH), F32), singles], axis=0)
    w_up = jnp.transpose(_gather_last(tiny["rwkv_w_up"][:, l]).reshape(64, NH, DH), (1, 0, 2))
    a_up = jnp.transpose(_gather_last(tiny["rwkv_a_up"][:, l]).reshape(64, NH, DH), (1, 0, 2))
    q["rwkv_prm"] = [_heads(wts["rwkv_w0"][l]), w_up, _heads(wts["rwkv_a0"][l]), a_up,
                     _heads(wts["rwkv_k_k"][l]), _heads(wts["rwkv_k_a"][l]), _heads(wts["rwkv_r_k"][l]),
                     _heads(wts["rwkv_ln_w"][l]), _heads(wts["rwkv_ln_b"][l])]
    sc = _gather_last(tiny["sc_conv_w"][:, l])
    q["sc_conv"] = jnp.transpose(sc.reshape(SC_TAPS, NH, DH), (1, 0, 2))
    gla_up = jnp.transpose(wts["gla_a_up"][l].reshape(16, NH, GLA_HEAD_K), (1, 0, 2))
    gla_up = jnp.pad(gla_up, ((0, 0), (0, DH - 16), (0, DH - GLA_HEAD_K)))
    gla_b = jnp.pad(wts["gla_a_bias"][l].reshape(NH, 1, GLA_HEAD_K), ((0, 0), (0, 0), (0, DH - GLA_HEAD_K)))
    q["gla_prm"] = [gla_up, gla_b, jnp.broadcast_to(wts["gla_norm_w"][l].reshape(1, 1, DH), (NH, 1, DH))]
    q["pre_w"] = wts["pre_norm_w"][l].reshape(1, D_MODEL)
    q["post_w"] = wts["post_norm_w"][l].reshape(1, D_MODEL)
    return q


def _mixer_inputs(p, ps):
    gdn = [(ps, 4, 0), (ps, 4, 1), (ps, 4, 2), (p, 4, G_GDN // 4 + 3), (p, 1, G_GDN_AB)]
    rwkv = [(ps, 4, G_PM // 4 + k) for k in range(4)] + [(ps, 1, G_PM_WD), (ps, 1, G_PM_AD)]
    gla = [(p, 4, G_GLA // 4 + k) for k in range(4)] + [(p, 1, G_GLA_AD)]
    return gdn, rwkv, gla


def _layer_fwd(x, q, nb, t, l, side=None, tgt=None, w_out_shard=None):
    hb, p, ps, y_sc = _norm_proj(x, q["pre_w"], q["w_g"], q["conv_blocks"], q["mix_blocks"], q["sc_conv"], t,
                                 f"norm_proj{l}")
    gdn_in, rwkv_in, gla_in = _mixer_inputs(p, ps)
    y_gdn, ck_gdn, got = _mixer_fwd(_gdn_chunk, f"gdn_fwd{l}", gdn_in, q["gdn_prm"], q["gdn_cst"], nb, t, n_kept=1,
                                    side=None if w_out_shard is None else _gather_plan([w_out_shard]))
    wout_g = q["wout_g"] if w_out_shard is None else _wout_operand(got[0])
    y_rwkv, ck_rwkv, side_res = _mixer_fwd(_rwkv_chunk, f"rwkv_fwd{l}", rwkv_in, q["rwkv_prm"], [], nb, t,
                                           first_fn=_rwkv_first_chunk, side=side, n_kept=1)
    y_gla, ck_gla, _ = _mixer_fwd(_gla_chunk, f"gla_fwd{l}", gla_in, q["gla_prm"], [], nb, t)
    ys = (y_gdn, y_rwkv, y_sc, y_gla)
    out, *res = _out_proj_norm(ys, wout_g, x, q["post_w"], f"out_proj{l}", tgt)
    saved = dict(x=x, hb=hb, p=p, ps=ps, ys=ys, out=out, ck=(ck_gdn, ck_rwkv, ck_gla), wout_g=wout_g)
    return (res[0] if tgt is None else res), saved, side_res


def _layer_bwd(dxn, q, sv, nb, t, l, side=None, exchange_own=False):
    p, ys = sv["p"], sv["ys"]
    dout, dy, d_post = _post_bwd(dxn, sv["out"], q["post_w"], sv["wout_g"], f"post_bwd{l}")
    d_wout = _dwout(ys, dout, f"dwout{l}").reshape(N_DEV, 128, D_MODEL).astype(BF16)
    gdn_in, rwkv_in, gla_in = _mixer_inputs(p, sv["ps"])
    ck_gdn, ck_rwkv, ck_gla = sv["ck"]
    g = {}

    (d_conv, dz, dab), (da_log, ddt, dnw), got_wout = _mixer_bwd(
        _gdn_chunk, f"gdn_bwd{l}", gdn_in, q["gdn_prm"], q["gdn_cst"], ck_gdn, dy, 0,
        [(12, F32), (4, BF16), (1, BF16)], [(0, 0), (0, 4), (0, 8), (1, 0), (2, 0)], nb, t,
        side=_exchange_plan([d_wout]) if exchange_own else None)
    dconv_in, d_gconv = _conv_bwd(p, G_GDN, 12, q["gdn_conv"], d_conv, nb, t, f"gdn_conv_bwd{l}")
    g["gdn_conv_w"] = jnp.transpose(d_gconv.sum(1), (1, 0, 2)).reshape(GDN_TAPS, 768)
    g["gdn_a_log"] = da_log.sum((0, 2, 3))
    g["gdn_dt_bias"] = ddt.sum((0, 2, 3))
    g["gdn_norm_w"] = dnw.sum((0, 1, 2))

    (d_pm,), d_rprm, side_res = _mixer_bwd(
        _rwkv_chunk, f"rwkv_bwd{l}", rwkv_in, q["rwkv_prm"], [], ck_rwkv, dy, 1,
        [(18, F32)], [(0, 0), (0, 4), (0, 8), (0, 12), (0, 16), (0, 17)], nb, t, first_fn=_rwkv_first_chunk,
        side=side)
    dp_rwkv, d_mu = _mix_bwd(p, q["rwkv_mu"], d_pm, nb, t, f"rwkv_mix_bwd{l}")
    g["rwkv_mu"] = d_mu.sum(1).reshape(1152)
    rp = [a.sum(0) for a in d_rprm]
    g["rwkv_w0"] = rp[0].reshape(256)
    g["rwkv_w_up"] = jnp.transpose(rp[1], (1, 0, 2)).reshape(64, 256)
    g["rwkv_a0"] = rp[2].reshape(256)
    g["rwkv_a_up"] = jnp.transpose(rp[3], (1, 0, 2)).reshape(64, 256)
    for i, nme in enumerate(("rwkv_k_k", "rwkv_k_a", "rwkv_r_k", "rwkv_ln_w", "rwkv_ln_b")):
        g[nme] = rp[4 + i].reshape(256)

    dp_sc, d_scw = _sc_bwd(p, q["sc_conv"], dy, nb, t, f"sc_bwd{l}")
    g["sc_conv_w"] = jnp.transpose(d_scw.sum(1), (1, 0, 2)).reshape(SC_TAPS, 256)

    (dp_gla, dad), (d_aup, d_ab, d_gnw), _ = _mixer_bwd(
        _gla_chunk, f"gla_bwd{l}", gla_in, q["gla_prm"], [], ck_gla, dy, 3,
        [(16, BF16), (1, BF16)], [(0, 0), (0, 4), (0, 8), (0, 12), (1, 0)], nb, t)
    g["gla_a_up"] = jnp.transpose(d_aup.sum(0)[:, :16, :GLA_HEAD_K], (1, 0, 2)).reshape(16, 128)
    g["gla_a_bias"] = d_ab.sum(0)[:, 0, :GLA_HEAD_K].reshape(128)
    g["gla_norm_w"] = d_gnw.sum((0, 1, 2))

    singles = jnp.concatenate([dab, dp_rwkv[16:18], dad], axis=0)
    sources = [dconv_in, dz, dp_rwkv, dp_sc, dp_gla, singles]
    d_win = _regroup_out(_dwin(sv["hb"], sources, f"dwin{l}"), f"regroup_out{l}")
    own = _exchange_plan([d_win]) if exchange_own else None
    dx, d_pre, got = _dh_prenorm_bwd(sources, q["w_g"], sv["x"], q["pre_w"], dxn, f"dh_bwd{l}", own)
    if exchange_own:
        (d_win,), (d_wout,) = got, got_wout
    g["pre_norm_w"] = d_pre.reshape(D_MODEL)
    g["post_norm_w"] = d_post.reshape(D_MODEL)
    return dx, g, d_win, d_wout, side_res


def _local_step(x, tgt, wts, tiny, w_in_all, w_out_all, later_shards=None, first_w_out=None):
    nb, t, d = x.shape
    xf = x.reshape(nb * t, d)
    overlap = later_shards is not None
    qs, saved = [], []
    big = _big_weights(w_in_all[0], w_out_all[0] if first_w_out is None else None, 0)
    for l in range(DEPTH):
        q = dict(_layer_params(wts, tiny, l), **big)
        nxt = l + 1 < DEPTH
        side = _gather_plan(later_shards[l]) if overlap and nxt else None
        xf, sv, got = _layer_fwd(xf, q, nb, t, l, side, None if nxt else tgt.reshape(nb * t, d),
                                 w_out_shard=first_w_out if l == 0 else None)
        if nxt:
            big = _big_weights(*(got if overlap else (w_in_all[l + 1], w_out_all[l + 1])), l + 1)
        qs.append(q)
        saved.append(sv)
    dxf, lpart = xf
    grads, d_win, d_wout = [None] * DEPTH, [None] * DEPTH, [None] * DEPTH
    for l in reversed(range(DEPTH)):
        side = _exchange_plan([d_win[l + 1], d_wout[l + 1]]) if overlap and l + 1 < DEPTH else None
        dxf, grads[l], d_win[l], d_wout[l], got = _layer_bwd(dxf, qs[l], saved[l], nb, t, l, side,
                                                             exchange_own=overlap and l == 0)
        if side is not None:
            d_win[l + 1], d_wout[l + 1] = got
    small = {k: jnp.stack([grads[l][k] for l in range(DEPTH)]) for k in grads[0]}
    return lpart[0, 0], dxf.reshape(nb, t, d), small, d_win, d_wout


_WEIGHTS = ("pre_norm_w", "w_in", "gdn_conv_w", "gdn_a_log", "gdn_dt_bias", "gdn_norm_w", "rwkv_mu", "rwkv_w0",
            "rwkv_w_up", "rwkv_a0", "rwkv_a_up", "rwkv_k_k", "rwkv_k_a", "rwkv_r_k", "rwkv_ln_w", "rwkv_ln_b",
            "sc_conv_w", "gla_a_up", "gla_a_bias", "gla_norm_w", "w_out", "post_norm_w")


def kernel(x, pre_norm_w, w_in, gdn_conv_w, gdn_a_log, gdn_dt_bias, gdn_norm_w, rwkv_mu, rwkv_w0, rwkv_w_up, rwkv_a0, rwkv_a_up, rwkv_k_k, rwkv_k_a, rwkv_r_k, rwkv_ln_w, rwkv_ln_b, sc_conv_w, gla_a_up, gla_a_bias, gla_norm_w, w_out, post_norm_w, loss_target, m_pre_norm_w, m_w_in, m_gdn_conv_w, m_gdn_a_log, m_gdn_dt_bias, m_gdn_norm_w, m_rwkv_mu, m_rwkv_w0, m_rwkv_w_up, m_rwkv_a0, m_rwkv_a_up, m_rwkv_k_k, m_rwkv_k_a, m_rwkv_r_k, m_rwkv_ln_w, m_rwkv_ln_b, m_sc_conv_w, m_gla_a_up, m_gla_a_bias, m_gla_norm_w, m_w_out, m_post_norm_w, v_pre_norm_w, v_w_in, v_gdn_conv_w, v_gdn_a_log, v_gdn_dt_bias, v_gdn_norm_w, v_rwkv_mu, v_rwkv_w0, v_rwkv_w_up, v_rwkv_a0, v_rwkv_a_up, v_rwkv_k_k, v_rwkv_k_a, v_rwkv_r_k, v_rwkv_ln_w, v_rwkv_ln_b, v_sc_conv_w, v_gla_a_up, v_gla_a_bias, v_gla_norm_w, v_w_out, v_post_norm_w):
    env = dict(locals())
    w = {n: env[n] for n in _WEIGHTS}
    m = {n: env["m_" + n] for n in _WEIGHTS}
    v = {n: env["v_" + n] for n in _WEIGHTS}
    tiny_names = [n for n, _ in _TINY]

    w_in_b, w_out_b = w_in.astype(BF16), w_out.astype(BF16)
    w_in_0, tiny_all = _all_gather_two_level(
        [w_in_b[0], _pack_rows([w[n] for n in tiny_names], _TINY_ROWS)], "gather_weights")
    tiny = _unpack_rows(tiny_all, _TINY)

    lpart, grad_x, small, r_win, r_wout = _local_step(
        x, loss_target, w, tiny, [w_in_0], [None], later_shards=[(w_in_b[l], w_out_b[l]) for l in range(1, DEPTH)],
        first_w_out=w_out_b[0])

    tiny_send = jnp.stack([_pack_rows([_split_last(small[n][l])[d] for n in tiny_names for l in range(DEPTH)],
                                      _TINY_ROWS) for d in range(N_DEV)])
    (r_tiny,) = _run_comm(_exchange_plan([tiny_send]), "scatter_grads")
    grads, delta, new_m, new_v = {}, {}, {}, {}
    for n, parts in (("w_in", r_win), ("w_out", r_wout)):
        res = [_sum_adamw(parts[l], w[n][l], m[n][l], v[n][l], f"adamw_{n}{l}") for l in range(DEPTH)]
        grads[n], delta[n], new_m[n], new_v[n] = [jnp.stack(o) for o in zip(*res)]
    tiny_sum = _sum_slots(r_tiny, "sum_tiny").reshape(-1)
    o = 0
    for n, s in _TINY:
        size = int(np.prod(s))
        grads[n] = tiny_sum[o:o + size].reshape(s)
        o += size

    small = dict(small)
    small["loss"] = lpart
    red = _unpack_rows(_all_reduce_small(_pack_rows([small[n] for n, _ in _SMALL], _SMALL_ROWS), "reduce_small"),
                       _SMALL)
    loss = red.pop("loss")
    grads.update(red)

    rest = [n for n in _WEIGHTS if n not in ("w_in", "w_out")]
    rest_shapes = [(n, w[n].shape) for n in rest]
    rows = -(-sum(int(np.prod(s)) for _, s in rest_shapes) // 1024) * 8
    outs = _adamw(_pack_rows([w[n] for n in rest], rows), _pack_rows([grads[n] for n in rest], rows),
                  _pack_rows([m[n] for n in rest], rows), _pack_rows([v[n] for n in rest], rows, 1.0), "adamw_rest")
    for dst, packed in zip((delta, new_m, new_v), outs):
        dst.update(_unpack_rows(packed, rest_shapes))

    return (loss, grad_x, *[grads[n] for n in _WEIGHTS], *[delta[n] for n in _WEIGHTS],
            *[new_m[n] for n in _WEIGHTS], *[new_v[n] for n in _WEIGHTS])
```

```python
import collections
import functools
import math

import numpy as np
import jax
import jax.numpy as jnp
from jax import lax
from jax.experimental import pallas as pl
from jax.experimental.pallas import tpu as pltpu

F32 = jnp.float32
BF16 = jnp.bfloat16

D_MODEL = 1024
DEPTH = 2
NH = 4
DH = 64
CH = 64
EPS = 1e-6
RWKV_GN_EPS = 64e-5
GLA_HEAD_K = 32
GLA_TAU = 16.0
GDN_TAPS = 4
SC_TAPS = 3
D_IN = 3992
N_DEV = 8
SHARD_COLS = D_IN // N_DEV

G_GDN = 0
G_RWKV = 16
G_SC = 32
G_GLA = 48
G_GDN_AB, G_RWKV_WD, G_RWKV_AD, G_GLA_AD = 64, 65, 66, 67
N_GROUPS = 68
GROUPS_PER_STEP = 4
TIME_BLOCK = 256
RWKV_EXACT_STEPS = 16

C_GDN, C_RWKV, C_SC, C_GLA = 0, 1032, 2184, 3208

ADAM_LR, ADAM_B1, ADAM_B2, ADAM_EPS, ADAM_WD, ADAM_STEP = 0.001, 0.9, 0.999, 1e-08, 0.01, 10

VMEM_LIMIT = 56 * 1024 * 1024
MESH = pl.DeviceIdType.MESH

_pcall = pl.pallas_call

_Comm = collections.namedtuple("_Comm", "operands out_shapes copies")


def _cparams(sem=None):
    if sem is None:
        return pltpu.CompilerParams(vmem_limit_bytes=VMEM_LIMIT)
    return pltpu.CompilerParams(dimension_semantics=sem, vmem_limit_bytes=VMEM_LIMIT)


def _group_segments():
    table = [(G_GDN + i, C_GDN + DH * i, DH) for i in range(16)]
    table.append((G_GDN_AB, C_GDN + 1024, 8))
    table += [(G_RWKV + i, C_RWKV + DH * i, DH) for i in range(16)]
    table += [(G_RWKV_WD, C_RWKV + 1024, DH), (G_RWKV_AD, C_RWKV + 1088, DH)]
    table += [(G_SC + 4 * j + k, C_SC + 256 * k + DH * j, DH) for j in range(NH) for k in range(4)]
    for h in range(NH):
        table += [(G_GLA + h, C_GLA + GLA_HEAD_K * h, GLA_HEAD_K),
                  (G_GLA + 4 + h, C_GLA + 128 + GLA_HEAD_K * h, GLA_HEAD_K),
                  (G_GLA + 8 + h, C_GLA + 256 + DH * h, DH),
                  (G_GLA + 12 + h, C_GLA + 512 + DH * h, DH)]
    table.append((G_GLA_AD, C_GLA + 768, 16))
    segs, padded = [], []
    for g, c, n in table:
        if n < DH:
            padded.append(g)
        a = 0
        while n > 0:
            d, off = divmod(c, SHARD_COLS)
            ln = min(n, SHARD_COLS - off)
            segs.append((g, a, d, off, ln))
            c, a, n = c + ln, a + ln, n - ln
    return segs, padded


_SEGMENTS, _PADDED_GROUPS = _group_segments()


def _dn(ta, tb):
    return (((1 if ta else 2,), (2 if tb else 1,)), ((0,), (0,)))


def _hdot(a, b, ta=False, tb=False):
    return lax.dot_general(a, b, _dn(ta, tb), precision=lax.Precision.HIGH, preferred_element_type=F32)


def _r(x):
    return x.astype(BF16)


def _rdot(a, b, ta=False, tb=False):
    return lax.dot_general(_r(a), _r(b), _dn(ta, tb), preferred_element_type=F32)


@jax.custom_vjp
def _bmm(a, b):
    return _rdot(a, b)


def _bmm_fwd(a, b):
    return _rdot(a, b), (a, b)


def _bmm_bwd(res, g):
    a, b = res
    return _rdot(g, b, tb=True), _rdot(a, g, ta=True)


_bmm.defvjp(_bmm_fwd, _bmm_bwd)


@jax.custom_vjp
def _bmm_nt(a, b):
    return _rdot(a, b, tb=True)


def _bmm_nt_fwd(a, b):
    return _rdot(a, b, tb=True), (a, b)


def _bmm_nt_bwd(res, g):
    a, b = res
    return _rdot(g, b), _rdot(g, a, ta=True)


_bmm_nt.defvjp(_bmm_nt_fwd, _bmm_nt_bwd)


@jax.custom_vjp
def _bmm_tn(a, b):
    return _rdot(a, b, ta=True)


def _bmm_tn_fwd(a, b):
    return _rdot(a, b, ta=True), (a, b)


def _bmm_tn_bwd(res, g):
    a, b = res
    return _rdot(b, g, tb=True), _rdot(a, g)


_bmm_tn.defvjp(_bmm_tn_fwd, _bmm_tn_bwd)


def _tri(n):
    i = lax.broadcasted_iota(jnp.int32, (n, n), 0)
    j = lax.broadcasted_iota(jnp.int32, (n, n), 1)
    return i >= j, i > j, i == j


def _heads_of(x, like):
    n = like.shape[0]
    if x.ndim == 2:
        return jnp.broadcast_to(x[None], (n,) + x.shape)
    seqs = x.shape[0]
    return jnp.broadcast_to(x[:, None], (seqs, n // seqs) + x.shape[1:]).reshape((n,) + x.shape[1:])


def _cumsum_rows(x):
    incl, _, _ = _tri(x.shape[-2])
    return _hdot(_heads_of(incl.astype(F32), x), x)


@jax.custom_vjp
def _inv_unit_lower(a):
    n = a.shape[-1]
    _, _, eye = _tri(n)
    pw = -a
    inv = eye.astype(F32) + pw
    for _ in range(math.ceil(math.log2(n)) - 1):
        pw = _hdot(pw, pw)
        inv = inv + _hdot(inv, pw)
    return inv


def _inv_unit_lower_fwd(a):
    inv = _inv_unit_lower(a)
    return inv, inv


def _inv_unit_lower_bwd(inv, g):
    return (-_hdot(_hdot(inv, g, ta=True), inv, tb=True),)


_inv_unit_lower.defvjp(_inv_unit_lower_fwd, _inv_unit_lower_bwd)


@jax.custom_vjp
def _inv_reuse(a, inv):
    return inv


def _inv_reuse_fwd(a, inv):
    return inv, inv


def _inv_reuse_bwd(inv, g):
    return _inv_unit_lower_bwd(inv, g)[0], jnp.zeros_like(inv)


_inv_reuse.defvjp(_inv_reuse_fwd, _inv_reuse_bwd)


def _silu(x):
    return x * jax.nn.sigmoid(x)


def _t(x):
    return jnp.swapaxes(x, -1, -2)


def _gdn_chunk(prm, cst, ins, s, tinv=None):
    a_log, dt_b, nw = prm
    m_a, m_b = cst
    cq, ck, cv, z, ab = ins
    ab = _heads_of(ab, m_a)
    incl, strict, _ = _tri(CH)
    q = _silu(cq)
    k = _silu(ck)
    v = _silu(cv)
    q = q * lax.rsqrt(jnp.sum(q * q, -1, keepdims=True) + EPS) * (DH ** -0.5)
    k = k * lax.rsqrt(jnp.sum(k * k, -1, keepdims=True) + EPS)
    a_raw = jnp.sum(ab * m_a, -1, keepdims=True)
    b_raw = jnp.sum(ab * m_b, -1, keepdims=True)
    gstep = -jnp.exp(a_log) * jax.nn.softplus(a_raw + dt_b)
    beta = jax.nn.sigmoid(b_raw)
    gc = _cumsum_rows(gstep)
    gl = jnp.sum(gstep, -2, keepdims=True)
    dec = jnp.where(incl, jnp.exp(jnp.where(incl, gc - _t(gc), 0.0)), 0.0)
    kb = k * beta
    a_mat = jnp.where(strict, _bmm_nt(kb, k) * dec, 0.0)
    tinv = _inv_unit_lower(a_mat) if tinv is None else _inv_reuse(a_mat, tinv)
    eg = jnp.exp(gc)
    u = _hdot(tinv, v * beta)
    w = _hdot(tinv, kb * eg)
    attn = _bmm_nt(q, k) * dec
    v_new = u - _bmm(w, s)
    o = _bmm(q * eg, s) + _bmm(attn, v_new)
    s_next = s * jnp.exp(gl) + _bmm_tn(k * jnp.exp(gl - gc), v_new)
    on = o * lax.rsqrt(jnp.mean(o * o, -1, keepdims=True) + EPS) * nw
    return on * _silu(z), s_next, tinv


def _gla_chunk(prm, cst, ins, st):
    a_up, a_bias, nw = prm
    q, k, v, z, ad = ins
    incl, _, _ = _tri(CH)
    la = jax.nn.log_sigmoid(_bmm(_heads_of(ad, a_up), a_up) + a_bias) * (1.0 / GLA_TAU)
    bc = _cumsum_rows(la)
    bl = jnp.sum(la, -2, keepdims=True)
    qe = q * (GLA_HEAD_K ** -0.5) * jnp.exp(bc)
    ke = k * jnp.exp(-bc)
    attn = jnp.where(incl, _bmm_nt(qe, ke), 0.0)
    o = _bmm_nt(qe, st) + _bmm(attn, v)
    st_next = st * jnp.exp(bl) + _bmm_tn(v, k * jnp.exp(bl - bc))
    on = o * lax.rsqrt(jnp.mean(o * o, -1, keepdims=True) + EPS) * nw
    return on * _silu(z), st_next


def _rwkv_chunk(prm, cst, ins, s, inv=None):
    r, v = ins[0], ins[2]
    incl, strict, _ = _tri(r.shape[-2])
    lw, kk, k2, m = _rwkv_pre(prm, ins)
    cum = _cumsum_rows(lw)
    ltot = jnp.sum(lw, -2, keepdims=True)
    n_t = -kk * jnp.exp(cum - lw)
    einv = jnp.exp(-cum)
    m_t = m * einv
    k_t = k2 * einv
    r_t = r * jnp.exp(cum)
    a_nm = jnp.where(strict, _hdot(n_t, m_t, tb=True), 0.0)
    a_nk = jnp.where(strict, _hdot(n_t, k_t, tb=True), 0.0)
    inv = _inv_unit_lower(-a_nm) if inv is None else _inv_reuse(-a_nm, inv)
    cm = _hdot(inv, _hdot(n_t, s, tb=True) + _bmm(a_nk, v))
    y = (_bmm_nt(r_t, s) + _bmm(jnp.where(incl, _hdot(r_t, m_t, tb=True), 0.0), cm)
         + _bmm(jnp.where(incl, _hdot(r_t, k_t, tb=True), 0.0), v))
    eend = jnp.exp(ltot - cum)
    s_next = s * jnp.exp(ltot) + _bmm_tn(cm, m * eend) + _bmm_tn(v, k2 * eend)
    return _rwkv_post(prm, ins, y, k2), s_next, inv


def _rwkv_pre(prm, ins):
    w0, w_up, a0, a_up, k_k, k_a = prm[:6]
    k, wd, ad = ins[1], ins[4], ins[5]
    lw = -math.exp(-0.5) * jax.nn.sigmoid(w0 + _bmm(_heads_of(jnp.tanh(wd), w_up), w_up))
    a = jax.nn.sigmoid(a0 + _bmm(_heads_of(ad, a_up), a_up))
    kk = k * k_k
    kk = kk * lax.rsqrt(jnp.sum(kk * kk, -1, keepdims=True) + EPS)
    k2 = k * (1.0 + (a - 1.0) * k_a)
    return lw, kk, k2, kk * a


def _rwkv_post(prm, ins, y, k2):
    r_k, ln_w, ln_b = prm[6:]
    r, v, z = ins[0], ins[2], ins[3]
    mean = jnp.mean(y, -1, keepdims=True)
    yc = y - mean
    var = jnp.mean(yc * yc, -1, keepdims=True)
    yn = yc * lax.rsqrt(var + RWKV_GN_EPS) * ln_w + ln_b
    bonus = jnp.sum(r * k2 * r_k, -1, keepdims=True) * v
    return (yn + bonus) * _silu(z)


@jax.custom_vjp
def _bmv(s, x):
    return jnp.sum(_r(s).astype(F32) * _r(x).astype(F32), -1, keepdims=True)


def _bmv_fwd(s, x):
    return _bmv(s, x), (s, x)


def _bmv_bwd(res, g):
    s, x = res
    return g * x, jnp.sum(_r(s).astype(F32) * _r(g).astype(F32), -2, keepdims=True)


_bmv.defvjp(_bmv_fwd, _bmv_bwd)


def _rwkv_steps(prm, cst, ins, s, steps):
    r, v = ins[0], ins[2]
    lw, kk, k2, m = _rwkv_pre(prm, ins)
    w = jnp.exp(lw)
    v_t = _t(v)
    lane = lax.broadcasted_iota(jnp.int32, (1, 1, CH), 2)
    y_t = jnp.zeros((s.shape[0], DH, CH), F32)
    for t in range(steps):
        e_t = (lane == t).astype(F32)
        row = (slice(None), slice(t, t + 1))
        sa = _bmv(s, -kk[row])
        s = s * w[row] + sa * m[row] + jnp.sum(v_t * e_t, -1, keepdims=True) * k2[row]
        y_t = y_t + _bmv(s, r[row]) * e_t
    return _rwkv_post(prm, ins, _t(y_t), k2)[:, :steps], s


def _rwkv_first_chunk(prm, cst, ins, s):
    k = RWKV_EXACT_STEPS
    y_head, s = _rwkv_steps(prm, cst, ins, s, k)
    y_tail, s, _ = _rwkv_chunk(prm, cst, [x[..., k:, :] for x in ins], s)
    return jnp.concatenate([y_head, y_tail], axis=-2), s


def _time_block(t):
    return TIME_BLOCK if t % TIME_BLOCK == 0 else t


def _load_chunk(ref, i):
    nb = ref.shape[1]
    if ref.shape[0] == NH:
        return jnp.concatenate([ref[:, b, pl.ds(i, CH), :] for b in range(nb)], axis=0)
    return ref[0, :, pl.ds(i, CH), :]


def _mixer_fwd(chunk_fn, name, ins, prm, cst, nb, t, first_fn=None, side=None, n_kept=0):
    tb = _time_block(t)
    nt, ncb, nch = t // tb, tb // CH, nb * NH
    n_in, n_prm, n_cst = len(ins), len(prm), len(cst)
    n_main, n_side = n_in + n_prm + n_cst, len(side.operands) if side else 0

    def body(*refs):
        in_refs = refs[:n_in]
        prm_refs = refs[n_in:n_in + n_prm]
        cst_refs = refs[n_in + n_prm:n_main]
        side_in = refs[n_main:n_main + n_side]
        y_ref, ck_ref = refs[n_main + n_side:n_main + n_side + 2]
        side_out = refs[n_main + n_side + 2:n_main + 2 * n_side + 2]
        s_scr = refs[n_main + 2 * n_side + 2]
        sems = refs[n_main + 2 * n_side + 3:]
        step_t = pl.program_id(0)

        if side is not None:
            @pl.when(step_t == 0)
            def _():
                _comm_start(side.copies(side_in, side_out, sems))

        @pl.when(step_t == 0)
        def _():
            s_scr[...] = jnp.zeros_like(s_scr)

        def chunk(c, i, fn=chunk_fn):
            s = s_scr[...]
            y, s_next, *kept = fn([jnp.tile(r[...], (nb, 1, 1)) for r in prm_refs],
                                  [jnp.tile(r[...], (nb, 1, 1)) for r in cst_refs],
                                  [_load_chunk(r, i) for r in in_refs], s)
            kept += [jnp.zeros_like(s)] * (n_kept - len(kept))
            for e, a in enumerate([s] + kept):
                ck_ref[c, e] = a
            for b in range(nb):
                y_ref[:, b, pl.ds(i, CH), :] = y[b * NH:(b + 1) * NH].astype(BF16)
            s_scr[...] = s_next

        def step(c, carry):
            chunk(c, pl.multiple_of(c * CH, CH))
            return carry

        if first_fn is None:
            lax.fori_loop(0, ncb, step, 0)
        else:
            @pl.when(step_t == 0)
            def _():
                chunk(0, 0, first_fn)

            @pl.when(step_t != 0)
            def _():
                chunk(0, 0)

            lax.fori_loop(1, ncb, step, 0)

        if side is not None:
            @pl.when(step_t == nt - 1)
            def _():
                _comm_wait(side.copies(side_in, side_out, sems))

    hbm = pl.BlockSpec(memory_space=pl.ANY)
    in_specs = [pl.BlockSpec((ng, nb, tb, DH), (lambda j, bi=bi: (bi, 0, j, 0))) for _, ng, bi in ins]
    in_specs += [pl.BlockSpec(p.shape, lambda j: (0, 0, 0)) for p in list(prm) + list(cst)]
    y, ck, *side_res = _pcall(
        body, name=name, grid=(nt,),
        in_specs=in_specs + [hbm] * n_side,
        out_specs=[pl.BlockSpec((NH, nb, tb, DH), lambda j: (0, 0, j, 0)),
                   pl.BlockSpec((ncb, 1 + n_kept, nch, DH, DH), lambda j: (j, 0, 0, 0, 0))] + [hbm] * n_side,
        out_shape=[jax.ShapeDtypeStruct((NH, nb, t, DH), BF16),
                   jax.ShapeDtypeStruct((t // CH, 1 + n_kept, nch, DH, DH), F32)]
        + (side.out_shapes if side else []),
        scratch_shapes=[pltpu.VMEM((nch, DH, DH), F32)] + (_comm_scratch(side) if side else []),
        compiler_params=_cparams(("arbitrary",)),
    )(*[a.reshape(a.shape[0], nb, t, DH) for a, _, _ in ins], *prm, *cst, *(side.operands if side else []))
    return y.reshape(NH, nb * t, DH), ck, side_res


def _mixer_bwd(chunk_fn, name, ins, prm, cst, ck, dy, dy_block, outs, routes, nb, t, first_fn=None, side=None):
    tb = _time_block(t)
    nt, ncb, nch = t // tb, tb // CH, nb * NH
    n_in, n_prm, n_cst, n_out = len(ins), len(prm), len(cst), len(outs)
    n_main, n_side = n_in + n_prm + n_cst + 2, len(side.operands) if side else 0
    n_kept = ck.shape[1] - 1

    def body(*refs):
        in_refs = refs[:n_in]
        prm_refs = refs[n_in:n_in + n_prm]
        cst_refs = refs[n_in + n_prm:n_in + n_prm + n_cst]
        ck_ref, dy_ref = refs[n_main - 2:n_main]
        side_in = refs[n_main:n_main + n_side]
        rest = refs[n_main + n_side:]
        out_refs = rest[:n_out]
        dprm_refs = rest[n_out:n_out + n_prm]
        side_out = rest[n_out + n_prm:n_out + n_prm + n_side]
        ds_scr = rest[n_out + n_prm + n_side]
        sems = rest[n_out + n_prm + n_side + 1:]
        step_t = pl.program_id(0)

        if side is not None:
            @pl.when(step_t == 0)
            def _():
                _comm_start(side.copies(side_in, side_out, sems))

        @pl.when(step_t == 0)
        def _():
            ds_scr[...] = jnp.zeros_like(ds_scr)
            for r in dprm_refs:
                r[...] = jnp.zeros_like(r)

        def chunk(c, i, fn=chunk_fn):
            cst_v = [jnp.tile(r[...], (nb, 1, 1)) for r in cst_refs]
            kept = [ck_ref[c, 1 + e] for e in range(n_kept)] if fn is chunk_fn else []
            _, vjp = jax.vjp(lambda p, x, s: fn(p, cst_v, x, s, *kept)[:2],
                             [jnp.tile(r[...], (nb, 1, 1)) for r in prm_refs],
                             [_load_chunk(r, i) for r in in_refs], ck_ref[c, 0])
            dy_c = jnp.concatenate([dy_ref[:, b, pl.ds(i, CH), :] for b in range(nb)], axis=0)
            d_prm, d_ins, d_s = vjp((dy_c, ds_scr[...]))
            for (oi, g0), r, g in zip(routes, in_refs, d_ins):
                o_ref = out_refs[oi]
                if r.shape[0] == NH:
                    for b in range(nb):
                        o_ref[g0:g0 + NH, b, pl.ds(i, CH), :] = g[b * NH:(b + 1) * NH].astype(o_ref.dtype)
                else:
                    o_ref[g0, :, pl.ds(i, CH), :] = g.astype(o_ref.dtype)
            for r, g in zip(dprm_refs, d_prm):
                r[...] += g
            ds_scr[...] = d_s

        def step(j, carry):
            c = ncb - 1 - j
            chunk(c, pl.multiple_of(c * CH, CH))
            return carry

        lax.fori_loop(0, ncb - 1, step, 0)
        if first_fn is None:
            chunk(0, 0)
        else:
            @pl.when(step_t == nt - 1)
            def _():
                chunk(0, 0, first_fn)

            @pl.when(step_t != nt - 1)
            def _():
                chunk(0, 0)

        if side is not None:
            @pl.when(step_t == nt - 1)
            def _():
                _comm_wait(side.copies(side_in, side_out, sems))

    def back(j):
        return nt - 1 - j

    hbm = pl.BlockSpec(memory_space=pl.ANY)
    in_specs = [pl.BlockSpec((ng, nb, tb, DH), (lambda j, bi=bi: (bi, 0, back(j), 0))) for _, ng, bi in ins]
    in_specs += [pl.BlockSpec(p.shape, lambda j: (0, 0, 0)) for p in list(prm) + list(cst)]
    in_specs += [pl.BlockSpec((ncb, 1 + n_kept, nch, DH, DH), lambda j: (back(j), 0, 0, 0, 0)),
                 pl.BlockSpec((NH, nb, tb, DH), lambda j: (dy_block, 0, back(j), 0))]
    out_specs = [pl.BlockSpec((ng, nb, tb, DH), lambda j: (0, 0, back(j), 0)) for ng, _ in outs]
    out_specs += [pl.BlockSpec((nch,) + p.shape[1:], lambda j: (0, 0, 0)) for p in prm]
    out_shape = [jax.ShapeDtypeStruct((ng, nb, t, DH), dt) for ng, dt in outs]
    out_shape += [jax.ShapeDtypeStruct((nch,) + p.shape[1:], F32) for p in prm]
    res = _pcall(
        body, name=name, grid=(nt,),
        in_specs=in_specs + [hbm] * n_side, out_specs=out_specs + [hbm] * n_side,
        out_shape=out_shape + (side.out_shapes if side else []),
        scratch_shapes=[pltpu.VMEM((nch, DH, DH), F32)] + (_comm_scratch(side) if side else []),
        compiler_params=_cparams(("arbitrary",)),
    )(*[a.reshape(a.shape[0], nb, t, DH) for a, _, _ in ins], *prm, *cst, ck, dy.reshape(dy.shape[0], nb, t, DH),
      *(side.operands if side else []))
    d_outs = [o.reshape(o.shape[0], nb * t, DH) for o in res[:n_out]]
    d_prm = [g.reshape((nb,) + p.shape) for g, p in zip(res[n_out:n_out + n_prm], prm)]
    return d_outs, d_prm, res[n_out + n_prm:]


def _shift_down(x, s):
    if s == 0:
        return x
    row = lax.broadcasted_iota(jnp.int32, x.shape, 0)
    return jnp.where(row < s, 0.0, pltpu.roll(x, s, 0))


def _shift_up(x, s):
    if s == 0:
        return x
    t = x.shape[0]
    row = lax.broadcasted_iota(jnp.int32, x.shape, 0)
    return jnp.where(row >= t - s, 0.0, pltpu.roll(x, t - s, 0))


def _conv_bwd(p, g0, ng, w, dy, nb, t, name):
    taps = w.shape[1]

    def body(x_ref, w_ref, dy_ref, dx_ref, dw_ref):
        x = x_ref[...]
        d = dy_ref[...]
        acc = w_ref[taps - 1:taps, :] * d
        dw_ref[taps - 1:taps, :] = jnp.sum(d * x, 0, keepdims=True)
        for i in range(taps - 1):
            s = taps - 1 - i
            acc = acc + w_ref[i:i + 1, :] * _shift_up(d, s)
            dw_ref[i:i + 1, :] = jnp.sum(d * _shift_down(x, s), 0, keepdims=True)
        dx_ref[...] = acc.astype(BF16)

    return _pcall(
        body, name=name, grid=(ng, nb),
        in_specs=[pl.BlockSpec((None, t, DH), lambda g, b: (g0 + g, b, 0)),
                  pl.BlockSpec((None, taps, DH), lambda g, b: (g, 0, 0)),
                  pl.BlockSpec((None, t, DH), lambda g, b: (g, b, 0))],
        out_specs=[pl.BlockSpec((None, t, DH), lambda g, b: (g, b, 0)),
                   pl.BlockSpec((None, None, taps, DH), lambda g, b: (g, b, 0, 0))],
        out_shape=[jax.ShapeDtypeStruct((ng, nb * t, DH), BF16),
                   jax.ShapeDtypeStruct((ng, nb, taps, DH), F32)],
        compiler_params=_cparams(("parallel", "parallel")),
    )(p, w, dy)


def _mix_group(g):
    return jnp.where(g < 16, G_RWKV + g, G_RWKV_WD + g - 16)


def _mix_bwd(p, mu, dy, nb, t, name):
    def body(x_ref, mu_ref, dy_ref, dx_ref, dmu_ref):
        x = x_ref[...]
        muv = mu_ref[...]
        d = dy_ref[...]
        dx_ref[...] = (d * (1.0 - muv) + _shift_up(d * muv, 1)).astype(BF16)
        dmu_ref[...] = jnp.sum(d * (_shift_down(x, 1) - x), 0, keepdims=True)

    return _pcall(
        body, name=name, grid=(18, nb),
        in_specs=[pl.BlockSpec((None, t, DH), lambda g, b: (_mix_group(g), b, 0)),
                  pl.BlockSpec((None, 1, DH), lambda g, b: (g, 0, 0)),
                  pl.BlockSpec((None, t, DH), lambda g, b: (g, b, 0))],
        out_specs=[pl.BlockSpec((None, t, DH), lambda g, b: (g, b, 0)),
                   pl.BlockSpec((None, None, 1, DH), lambda g, b: (g, b, 0, 0))],
        out_shape=[jax.ShapeDtypeStruct((18, nb * t, DH), BF16),
                   jax.ShapeDtypeStruct((18, nb, 1, DH), F32)],
        compiler_params=_cparams(("parallel", "parallel")),
    )(p, mu, dy)


def _sc_bwd(p, w, dy, nb, t, name):
    def body(p_ref, w_ref, dy_ref, dp_ref, dw_ref):
        bg, cg, xg, z = p_ref[0], p_ref[1], p_ref[2], p_ref[3]
        d = dy_ref[...]
        u = cg * xg
        u1 = _shift_down(u, 1)
        u2 = _shift_down(u, 2)
        conv = w_ref[2:3, :] * u + w_ref[1:2, :] * u1 + w_ref[0:1, :] * u2
        sg = jax.nn.sigmoid(z)
        sz = z * sg
        dp_ref[0] = (d * conv * sz).astype(BF16)
        dp_ref[3] = (d * bg * conv * (sg * (1.0 + z * (1.0 - sg)))).astype(BF16)
        dconv = d * bg * sz
        du = w_ref[2:3, :] * dconv + w_ref[1:2, :] * _shift_up(dconv, 1) + w_ref[0:1, :] * _shift_up(dconv, 2)
        dp_ref[1] = (du * xg).astype(BF16)
        dp_ref[2] = (du * cg).astype(BF16)
        dw_ref[2:3, :] = jnp.sum(dconv * u, 0, keepdims=True)
        dw_ref[1:2, :] = jnp.sum(dconv * u1, 0, keepdims=True)
        dw_ref[0:1, :] = jnp.sum(dconv * u2, 0, keepdims=True)

    return _pcall(
        body, name=name, grid=(NH, nb),
        in_specs=[pl.BlockSpec((4, t, DH), lambda j, b: (G_SC // 4 + j, b, 0)),
                  pl.BlockSpec((None, SC_TAPS, DH), lambda j, b: (j, 0, 0)),
                  pl.BlockSpec((None, t, DH), lambda j, b: (8 + j, b, 0))],
        out_specs=[pl.BlockSpec((4, t, DH), lambda j, b: (j, b, 0)),
                   pl.BlockSpec((None, None, SC_TAPS, DH), lambda j, b: (j, b, 0, 0))],
        out_shape=[jax.ShapeDtypeStruct((4 * NH, nb * t, DH), BF16),
                   jax.ShapeDtypeStruct((NH, nb, SC_TAPS, DH), F32)],
        compiler_params=_cparams(("parallel", "parallel")),
    )(p, w, dy)


def _row_tile(n):
    return 1024 if n % 1024 == 0 else n


def _regroup_in(w_all, name):
    tr = 256
    gs = GROUPS_PER_STEP

    def body(w_ref, o_ref):
        for g in _PADDED_GROUPS:
            o_ref[g // gs, :, DH * (g % gs):DH * (g % gs + 1)] = jnp.zeros((tr, DH), BF16)
        for g, a, d, off, ln in _SEGMENTS:
            lane = DH * (g % gs) + a
            o_ref[g // gs, :, lane:lane + ln] = w_ref[d, :, off:off + ln].astype(BF16)

    return _pcall(
        body, name=name, grid=(D_MODEL // tr,),
        in_specs=[pl.BlockSpec((N_DEV, tr, SHARD_COLS), lambda i: (0, i, 0))],
        out_specs=pl.BlockSpec((N_GROUPS // gs, tr, gs * DH), lambda i: (0, i, 0)),
        out_shape=jax.ShapeDtypeStruct((N_GROUPS // gs, D_MODEL, gs * DH), BF16),
        compiler_params=_cparams(("parallel",)),
    )(w_all)


def _regroup_out(dwg, name):
    tr = 256
    gs = GROUPS_PER_STEP

    def body(g_ref, o_ref):
        for g, a, d, off, ln in _SEGMENTS:
            lane = DH * (g % gs) + a
            o_ref[d, :, off:off + ln] = g_ref[g // gs, :, lane:lane + ln].astype(BF16)

    return _pcall(
        body, name=name, grid=(D_MODEL // tr,),
        in_specs=[pl.BlockSpec((N_GROUPS // gs, tr, gs * DH), lambda i: (0, i, 0))],
        out_specs=pl.BlockSpec((N_DEV, tr, SHARD_COLS), lambda i: (0, i, 0)),
        out_shape=jax.ShapeDtypeStruct((N_DEV, D_MODEL, SHARD_COLS), BF16),
        compiler_params=_cparams(("parallel",)),
    )(dwg)


CONV_BLOCKS = (0, 1, 2)
MIX_BLOCKS = (4, 5, 6, 7, 16)
SHIFTED_BLOCKS = {0: 0, 1: 1, 2: 2, 4: 3, 5: 4, 6: 5, 7: 6, 16: 7}
G_PM = 12
G_PM_WD, G_PM_AD = 29, 30
SC_BLOCK0 = G_SC // GROUPS_PER_STEP


def _norm_proj(x, pre_w, w_g, conv_w, mix_mu, sc_w, t, name):
    n = x.shape[0]
    tm = _row_tile(n)
    gs = GROUPS_PER_STEP
    nj = N_GROUPS // gs
    assert t % tm == 0, (t, tm)
    per_seq = t // tm

    def shifted_block(j):
        out = jnp.int32(len(SHIFTED_BLOCKS) - 1)
        for jj in sorted(SHIFTED_BLOCKS, reverse=True):
            out = jnp.where(j < jj, max(SHIFTED_BLOCKS[jj] - 1, 0), out)
        for jj, b in SHIFTED_BLOCKS.items():
            out = jnp.where(j == jj, b, out)
        return out

    def body(x_ref, pw_ref, w_ref, cw_ref, mu_ref, scw_ref, h_ref, p_ref, s_ref, ysc_ref, carry):
        i, j = pl.program_id(0), pl.program_id(1)

        @pl.when(j == 0)
        def _():
            xv = x_ref[...]
            h = xv * lax.rsqrt(jnp.mean(xv * xv, -1, keepdims=True) + EPS) * pw_ref[...]
            h_ref[...] = h.astype(BF16)

        r = jnp.dot(h_ref[...], w_ref[...], preferred_element_type=F32)
        for k in range(gs):
            p_ref[k] = r[:, DH * k:DH * (k + 1)]

        def shifts():
            first = (i % per_seq) == 0
            tail = jnp.where(first, 0.0, carry[j])
            above = jnp.concatenate([tail, jnp.zeros((tm - 8, gs * DH), F32)], axis=0)
            row = lax.broadcasted_iota(jnp.int32, r.shape, 0)
            out = [jnp.where(row < s, pltpu.roll(above, tm - 8 + s, 0), pltpu.roll(r, s, 0)) for s in (1, 2, 3)]
            carry[j] = r[tm - 8:, :]
            return out

        def store(v):
            for k in range(gs):
                s_ref[k] = v[:, DH * k:DH * (k + 1)]

        @pl.when(functools.reduce(jnp.logical_or, [j == b for b in CONV_BLOCKS]))
        def _():
            s1, s2, s3 = shifts()
            store(cw_ref[3:4, :] * r + cw_ref[0:1, :] * s3 + cw_ref[1:2, :] * s2 + cw_ref[2:3, :] * s1)

        @pl.when(functools.reduce(jnp.logical_or, [j == b for b in MIX_BLOCKS]))
        def _():
            s1 = shifts()[0]
            store(r + mu_ref[...] * (s1 - r))

        @pl.when((j >= SC_BLOCK0) & (j < SC_BLOCK0 + NH))
        def _():
            first = (i % per_seq) == 0
            tail = jnp.where(first, 0.0, carry[j])
            u = r[:, DH:2 * DH] * r[:, 2 * DH:3 * DH]
            above = jnp.concatenate([tail[:, DH:2 * DH] * tail[:, 2 * DH:3 * DH], jnp.zeros((tm - 8, DH), F32)], axis=0)
            row = lax.broadcasted_iota(jnp.int32, u.shape, 0)
            u1, u2 = [jnp.where(row < s, pltpu.roll(above, tm - 8 + s, 0), pltpu.roll(u, s, 0)) for s in (1, 2)]
            conv = scw_ref[2:3, :] * u + scw_ref[1:2, :] * u1 + scw_ref[0:1, :] * u2
            ysc_ref[...] = (r[:, :DH] * conv * _silu(r[:, 3 * DH:])).astype(BF16)
            carry[j] = r[tm - 8:, :]

    n_shifted = len(SHIFTED_BLOCKS) * gs
    return _pcall(
        body, name=name, grid=(n // tm, nj),
        in_specs=[pl.BlockSpec((tm, D_MODEL), lambda i, j: (i, 0)),
                  pl.BlockSpec((1, D_MODEL), lambda i, j: (0, 0)),
                  pl.BlockSpec((None, D_MODEL, gs * DH), lambda i, j: (j, 0, 0)),
                  pl.BlockSpec((None, GDN_TAPS, gs * DH), lambda i, j: (j, 0, 0)),
                  pl.BlockSpec((None, 1, gs * DH), lambda i, j: (j, 0, 0)),
                  pl.BlockSpec((None, SC_TAPS, DH), lambda i, j: (jnp.clip(j - SC_BLOCK0, 0, NH - 1), 0, 0))],
        out_specs=[pl.BlockSpec((tm, D_MODEL), lambda i, j: (i, 0)),
                   pl.BlockSpec((gs, tm, DH), lambda i, j: (j, i, 0)),
                   pl.BlockSpec((gs, tm, DH), lambda i, j: (shifted_block(j), i, 0)),
                   pl.BlockSpec((None, tm, DH), lambda i, j: (jnp.clip(j - SC_BLOCK0, 0, NH - 1), i, 0))],
        out_shape=[jax.ShapeDtypeStruct((n, D_MODEL), BF16),
                   jax.ShapeDtypeStruct((N_GROUPS, n, DH), F32),
                   jax.ShapeDtypeStruct((n_shifted, n, DH), F32),
                   jax.ShapeDtypeStruct((NH, n, DH), BF16)],
        scratch_shapes=[pltpu.VMEM((nj, 8, gs * DH), F32)],
        compiler_params=_cparams(("arbitrary", "arbitrary")),
    )(x, pre_w, w_g, conv_w, mix_mu, sc_w)


def _out_proj_norm(ys, wout_g, x, post_w, name, tgt=None):
    n = x.shape[0]
    tm = _row_tile(n)

    def body(y0, y1, y2, y3, w_ref, x_ref, pw_ref, *rest):
        y = jnp.concatenate([yr[h] for yr in (y0, y1, y2, y3) for h in range(NH)], axis=-1)
        acc = jnp.dot(y, w_ref[...], preferred_element_type=F32)
        xn = x_ref[...] + acc * lax.rsqrt(jnp.mean(acc * acc, -1, keepdims=True) + EPS) * pw_ref[...]
        if tgt is None:
            out_ref, xn_ref = rest
            xn_ref[...] = xn
        else:
            t_ref, out_ref, dx_ref, l_ref = rest

            @pl.when(pl.program_id(0) == 0)
            def _():
                l_ref[...] = jnp.zeros_like(l_ref)

            e = xn - t_ref[...]
            dx_ref[...] = e * (1.0 / D_MODEL)
            l_ref[...] += jnp.sum(jnp.sum(e * e, -1, keepdims=True), 0, keepdims=True) * (0.5 / D_MODEL)
        out_ref[...] = acc

    yspec = pl.BlockSpec((NH, tm, DH), lambda i: (0, i, 0))
    rows = pl.BlockSpec((tm, D_MODEL), lambda i: (i, 0))
    full = jax.ShapeDtypeStruct((n, D_MODEL), F32)
    head = tgt is not None
    return _pcall(
        body, name=name, grid=(n // tm,),
        in_specs=[yspec] * 4 + [pl.BlockSpec((D_MODEL, D_MODEL), lambda i: (0, 0)), rows,
                                pl.BlockSpec((1, D_MODEL), lambda i: (0, 0))] + [rows] * head,
        out_specs=[rows, rows] + [pl.BlockSpec((1, 128), lambda i: (0, 0))] * head,
        out_shape=[full, full] + [jax.ShapeDtypeStruct((1, 128), F32)] * head,
        compiler_params=_cparams(("arbitrary",) if head else ("parallel",)),
    )(*ys, wout_g.reshape(D_MODEL, D_MODEL), x, post_w, *([tgt] if head else []))


def _rmsnorm_bwd(xv, w, d):
    r = lax.rsqrt(jnp.mean(xv * xv, -1, keepdims=True) + EPS)
    xh = xv * r
    dxh = d * w
    dx = r * (dxh - xh * jnp.mean(dxh * xh, -1, keepdims=True))
    return dx, d * xh


def _post_bwd(dxn, out, post_w, wout_g, name):
    n = dxn.shape[0]
    tm = _row_tile(n)

    def body(d_ref, o_ref, pw_ref, w_ref, do_ref, dy_ref, dpw_ref):
        @pl.when(pl.program_id(0) == 0)
        def _():
            dpw_ref[...] = jnp.zeros_like(dpw_ref)

        dout, dw_rows = _rmsnorm_bwd(o_ref[...], pw_ref[...], d_ref[...])
        dpw_ref[...] += jnp.sum(dw_rows, 0, keepdims=True)
        db = dout.astype(BF16)
        do_ref[...] = db
        dy = lax.dot_general(db, w_ref[...], (((1,), (1,)), ((), ())), preferred_element_type=F32)
        for g in range(4 * NH):
            dy_ref[g] = dy[:, DH * g:DH * (g + 1)]

    rows = pl.BlockSpec((tm, D_MODEL), lambda i: (i, 0))
    vec = pl.BlockSpec((1, D_MODEL), lambda i: (0, 0))
    return _pcall(
        body, name=name, grid=(n // tm,),
        in_specs=[rows, rows, vec, pl.BlockSpec((D_MODEL, D_MODEL), lambda i: (0, 0))],
        out_specs=[rows, pl.BlockSpec((4 * NH, tm, DH), lambda i: (0, i, 0)), vec],
        out_shape=[jax.ShapeDtypeStruct((n, D_MODEL), BF16),
                   jax.ShapeDtypeStruct((4 * NH, n, DH), F32),
                   jax.ShapeDtypeStruct((1, D_MODEL), F32)],
        compiler_params=_cparams(("arbitrary",)),
    )(dxn, out, post_w, wout_g.reshape(D_MODEL, D_MODEL))


def _dwout(ys, dout, name):
    n = dout.shape[0]
    tm = _row_tile(n)

    def body(y0, y1, y2, y3, d_ref, dw_ref):
        @pl.when(pl.program_id(0) == 0)
        def _():
            dw_ref[...] = jnp.zeros_like(dw_ref)

        y = jnp.concatenate([yr[h] for yr in (y0, y1, y2, y3) for h in range(NH)], axis=-1)
        dw_ref[...] += lax.dot_general(y, d_ref[...], (((0,), (0,)), ((), ())), preferred_element_type=F32)

    yspec = pl.BlockSpec((NH, tm, DH), lambda i: (0, i, 0))
    return _pcall(
        body, name=name, grid=(n // tm,),
        in_specs=[yspec] * 4 + [pl.BlockSpec((tm, D_MODEL), lambda i: (i, 0))],
        out_specs=pl.BlockSpec((D_MODEL, D_MODEL), lambda i: (0, 0)),
        out_shape=jax.ShapeDtypeStruct((D_MODEL, D_MODEL), F32),
        compiler_params=_cparams(("arbitrary",)),
    )(*ys, dout)


def _source_specs(sources, rows_first):
    gs = GROUPS_PER_STEP
    spans, specs, j0 = [], [], 0
    for a in sources:
        nblk = a.shape[0] // gs
        spans.append((j0, j0 + nblk))
        shape = (gs, _row_tile(a.shape[1]), DH)

        def blk(j, j0=j0, nblk=nblk):
            return jnp.clip(j - j0, 0, nblk - 1)

        if rows_first:
            specs.append(pl.BlockSpec(shape, (lambda i, j, blk=blk: (blk(j), i, 0))))
        else:
            specs.append(pl.BlockSpec(shape, (lambda j, i, blk=blk: (blk(j), i, 0))))
        j0 += nblk
    return spans, specs


def _dh_prenorm_bwd(sources, w_g, x, pre_w, dxn, name, side=None):
    n = x.shape[0]
    tm = _row_tile(n)
    gs = GROUPS_PER_STEP
    nj = N_GROUPS // gs
    ni = n // tm
    spans, src_specs = _source_specs(sources, True)
    ns = len(sources)
    n_side = len(side.operands) if side else 0

    def body(*refs):
        src = refs[:ns]
        w_ref, x_ref, pw_ref, d_ref = refs[ns:ns + 4]
        side_in = refs[ns + 4:ns + 4 + n_side]
        dx_ref, dpw_ref = refs[ns + 4 + n_side:ns + 6 + n_side]
        side_out = refs[ns + 6 + n_side:ns + 6 + 2 * n_side]
        acc = refs[ns + 6 + 2 * n_side]
        sems = refs[ns + 7 + 2 * n_side:]
        i, j = pl.program_id(0), pl.program_id(1)

        if side is not None:
            @pl.when((i == 0) & (j == 0))
            def _():
                _comm_start(side.copies(side_in, side_out, sems))

        @pl.when((i == 0) & (j == 0))
        def _():
            dpw_ref[...] = jnp.zeros_like(dpw_ref)

        @pl.when(j == 0)
        def _():
            acc[...] = jnp.zeros_like(acc)

        for s_ref, (lo, hi) in zip(src, spans):
            @pl.when((j >= lo) & (j < hi))
            def _(s_ref=s_ref):
                four = jnp.concatenate([s_ref[k] for k in range(gs)], axis=-1)
                acc[...] += lax.dot_general(four, w_ref[...], (((1,), (1,)), ((), ())), preferred_element_type=F32)

        @pl.when(j == nj - 1)
        def _():
            dx, dw_rows = _rmsnorm_bwd(x_ref[...], pw_ref[...], acc[...])
            dx_ref[...] = d_ref[...] + dx
            dpw_ref[...] += jnp.sum(dw_rows, 0, keepdims=True)

        if side is not None:
            @pl.when((i == ni - 1) & (j == nj - 1))
            def _():
                _comm_wait(side.copies(side_in, side_out, sems))

    hbm = pl.BlockSpec(memory_space=pl.ANY)
    rows = pl.BlockSpec((tm, D_MODEL), lambda i, j: (i, 0))
    vec = pl.BlockSpec((1, D_MODEL), lambda i, j: (0, 0))
    dx, dpw, *side_res = _pcall(
        body, name=name, grid=(ni, nj),
        in_specs=src_specs + [pl.BlockSpec((None, D_MODEL, gs * DH), lambda i, j: (j, 0, 0)), rows, vec, rows]
        + [hbm] * n_side,
        out_specs=[rows, vec] + [hbm] * n_side,
        out_shape=[jax.ShapeDtypeStruct((n, D_MODEL), F32), jax.ShapeDtypeStruct((1, D_MODEL), F32)]
        + (side.out_shapes if side else []),
        scratch_shapes=[pltpu.VMEM((tm, D_MODEL), F32)] + (_comm_scratch(side) if side else []),
        compiler_params=_cparams(("arbitrary", "arbitrary")),
    )(*sources, w_g, x, pre_w, dxn, *(side.operands if side else []))
    return dx, dpw, side_res


def _dwin(hb, sources, name):
    n = hb.shape[0]
    tm = _row_tile(n)
    gs = GROUPS_PER_STEP
    ni, nj = n // tm, N_GROUPS // gs
    spans, src_specs = _source_specs(sources, True)
    ns = len(sources)

    def body(*refs):
        h_ref = refs[0]
        src = refs[1:1 + ns]
        out_ref, acc, sem = refs[1 + ns:]
        i, j = pl.program_id(0), pl.program_id(1)

        @pl.when((i == 0) & (j == 0))
        def _():
            acc[...] = jnp.zeros_like(acc)

        h = h_ref[...]
        for s_ref, (lo, hi) in zip(src, spans):
            @pl.when((j >= lo) & (j < hi))
            def _(s_ref=s_ref):
                four = jnp.concatenate([s_ref[k] for k in range(gs)], axis=-1)
                acc[j] += jnp.dot(h, four, preferred_element_type=F32)

        @pl.when((i == ni - 1) & (j == nj - 1))
        def _():
            done = pltpu.make_async_copy(acc, out_ref, sem)
            done.start()
            done.wait()

    return _pcall(
        body, name=name, grid=(ni, nj),
        in_specs=[pl.BlockSpec((D_MODEL, tm), lambda i, j: (0, i))] + src_specs,
        out_specs=pl.BlockSpec(memory_space=pl.ANY),
        out_shape=jax.ShapeDtypeStruct((nj, D_MODEL, gs * DH), F32),
        scratch_shapes=[pltpu.VMEM((nj, D_MODEL, gs * DH), F32), pltpu.SemaphoreType.DMA],
        compiler_params=_cparams(("arbitrary", "arbitrary")),
    )(jnp.transpose(hb), *sources)


def _adamw_math(w, g, m, v):
    c1 = 1.0 - ADAM_B1 ** ADAM_STEP
    c2 = 1.0 - ADAM_B2 ** ADAM_STEP
    nm = ADAM_B1 * m + (1.0 - ADAM_B1) * g
    nv = ADAM_B2 * v + (1.0 - ADAM_B2) * (g * g)
    return -ADAM_LR * ((nm / c1) / (jnp.sqrt(nv / c2) + ADAM_EPS) + ADAM_WD * w), nm, nv


def _adamw(w, g, m, v, name):
    r, c = w.shape
    tr = 256 if r % 256 == 0 else r

    def body(w_ref, g_ref, m_ref, v_ref, d_ref, nm_ref, nv_ref):
        d_ref[...], nm_ref[...], nv_ref[...] = _adamw_math(w_ref[...], g_ref[...], m_ref[...], v_ref[...])

    spec = pl.BlockSpec((tr, c), lambda i: (i, 0))
    return _pcall(
        body, name=name, grid=(r // tr,),
        in_specs=[spec] * 4, out_specs=[spec] * 3,
        out_shape=[jax.ShapeDtypeStruct((r, c), F32)] * 3,
        compiler_params=_cparams(("parallel",)),
    )(w, g, m, v)


def _sum_adamw(parts, w, m, v, name):
    r, c = w.shape
    tr = 128 if r % 128 == 0 else r

    def body(p_ref, w_ref, m_ref, v_ref, g_ref, d_ref, nm_ref, nv_ref):
        g = p_ref[0].astype(F32)
        for k in range(1, N_DEV):
            g = g + p_ref[k].astype(F32)
        g_ref[...] = g
        d_ref[...], nm_ref[...], nv_ref[...] = _adamw_math(w_ref[...], g, m_ref[...], v_ref[...])

    spec = pl.BlockSpec((tr, c), lambda i: (i, 0))
    return _pcall(
        body, name=name, grid=(r // tr,),
        in_specs=[pl.BlockSpec((N_DEV, tr, c), lambda i: (0, i, 0))] + [spec] * 3, out_specs=[spec] * 4,
        out_shape=[jax.ShapeDtypeStruct((r, c), F32)] * 4,
        compiler_params=_cparams(("parallel",)),
    )(parts, w, m, v)


def _me():
    return lax.axis_index("x"), lax.axis_index("y"), lax.axis_index("c")


def _flat(x, y, c):
    return 4 * x + 2 * y + c


def _peer(k):
    x, y, c = _me()
    return (x ^ ((k >> 2) & 1), y ^ ((k >> 1) & 1), c ^ (k & 1))


def _gather_plan(blocks):
    def copies(x_refs, out_refs, sems):
        send_sems, recv_sems, local_sems = sems
        me = _flat(*_me())
        local = [pltpu.make_async_copy(x, o.at[me], local_sems.at[a]) for a, (x, o) in enumerate(zip(x_refs, out_refs))]
        outgoing, incoming = [], []
        for k in range(1, N_DEV):
            src = _flat(*_peer(k))
            for a, (x, o) in enumerate(zip(x_refs, out_refs)):
                for slot, group in ((me, outgoing), (src, incoming)):
                    group.append(pltpu.make_async_remote_copy(
                        src_ref=x, dst_ref=o.at[slot], send_sem=send_sems.at[a, k - 1], recv_sem=recv_sems.at[a, k - 1],
                        device_id=_peer(k), device_id_type=MESH))
        return local, outgoing, incoming

    return _Comm(list(blocks), [jax.ShapeDtypeStruct((N_DEV,) + b.shape, b.dtype) for b in blocks], copies)


def _exchange_plan(sends):
    def copies(s_refs, out_refs, sems):
        send_sems, recv_sems, local_sems = sems
        me = _flat(*_me())
        local = [pltpu.make_async_copy(s.at[me], o.at[0], local_sems.at[i]) for i, (s, o) in enumerate(zip(s_refs, out_refs))]
        outgoing = []
        for k in range(1, N_DEV):
            to = _flat(*_peer(k))
            for i, (s, o) in enumerate(zip(s_refs, out_refs)):
                outgoing.append(pltpu.make_async_remote_copy(
                    src_ref=s.at[to], dst_ref=o.at[k], send_sem=send_sems.at[i, k - 1], recv_sem=recv_sems.at[i, k - 1],
                    device_id=_peer(k), device_id_type=MESH))
        return local, outgoing, outgoing

    return _Comm(list(sends), [jax.ShapeDtypeStruct(s.shape, s.dtype) for s in sends], copies)


def _comm_scratch(plan):
    n = len(plan.operands)
    return [pltpu.SemaphoreType.DMA((n, N_DEV - 1)), pltpu.SemaphoreType.DMA((n, N_DEV - 1)),
            pltpu.SemaphoreType.DMA((n,))]


def _comm_start(copies):
    local, outgoing, _ = copies
    for cp in local + outgoing:
        cp.start()


def _comm_wait(copies):
    local, outgoing, incoming = copies
    for cp in incoming:
        cp.wait_recv()
    for cp in outgoing:
        cp.wait_send()
    for cp in local:
        cp.wait()


def _all_gather_two_level(blocks, name):
    na = len(blocks)

    def body(*refs):
        x_refs, out_refs = refs[:na], refs[na:2 * na]
        send_sems, recv_sems, local_sems = refs[2 * na:]
        x, y, c = _me()
        me, sibling = (x, y, c), (x, y, 1 - c)
        chips = [(1 - x, y), (x, 1 - y), (1 - x, 1 - y)]

        def copy(a, k, block, to, own=False):
            slot = out_refs[a].at[_flat(*block)]
            return pltpu.make_async_remote_copy(
                src_ref=x_refs[a] if own else slot, dst_ref=slot, send_sem=send_sems.at[a, k],
                recv_sem=recv_sems.at[a, k], device_id=to, device_id_type=MESH)

        mine = [pltpu.make_async_copy(x_refs[a], out_refs[a].at[_flat(*me)], local_sems.at[a]) for a in range(na)]
        first = [copy(a, 0, me, sibling, own=True) for a in range(na)]
        first += [copy(a, 1 + j, me, (*chip, c), own=True) for j, chip in enumerate(chips) for a in range(na)]
        for cp in mine + first:
            cp.start()
        passed = []
        for j, chip in enumerate(chips):
            for a in range(na):
                copy(a, 1 + j, (*chip, c), me).wait_recv()
                cp = copy(a, 4 + j, (*chip, c), sibling)
                cp.start()
                passed.append(cp)
        for a in range(na):
            copy(a, 0, sibling, me).wait_recv()
        for j, chip in enumerate(chips):
            for a in range(na):
                copy(a, 4 + j, (*chip, 1 - c), me).wait_recv()
        for cp in first + passed:
            cp.wait_send()
        for cp in mine:
            cp.wait()

    return _pcall(
        body, name=name,
        in_specs=[pl.BlockSpec(memory_space=pl.ANY)] * na,
        out_specs=[pl.BlockSpec(memory_space=pl.ANY)] * na,
        out_shape=[jax.ShapeDtypeStruct((N_DEV,) + b.shape, b.dtype) for b in blocks],
        scratch_shapes=[pltpu.SemaphoreType.DMA((na, N_DEV - 1)), pltpu.SemaphoreType.DMA((na, N_DEV - 1)),
                        pltpu.SemaphoreType.DMA((na,))],
    )(*blocks)


def _reduce_small(blk, sends, name):
    r, s = blk.shape[0], sends.shape[1]

    def body(x_ref, s_ref, out_ref, mine_ref, gath, got, send_sems, recv_sems):
        me = _flat(*_me())
        gath[me] = x_ref[...]
        got[0] = s_ref[me]
        copies = []
        for k in range(1, N_DEV):
            to = _flat(*_peer(k))
            for a, (src, dst) in enumerate(((x_ref, gath.at[me]), (s_ref.at[to], got.at[k]))):
                cp = pltpu.make_async_remote_copy(
                    src_ref=src, dst_ref=dst, send_sem=send_sems.at[a, k - 1], recv_sem=recv_sems.at[a, k - 1],
                    device_id=_peer(k), device_id_type=MESH)
                cp.start()
                copies.append(cp)
        for k in range(1, N_DEV):
            src = _flat(*_peer(k))
            for a, (from_, dst) in enumerate(((x_ref, gath.at[src]), (s_ref.at[me], got.at[k]))):
                pltpu.make_async_remote_copy(
                    src_ref=from_, dst_ref=dst, send_sem=send_sems.at[a, k - 1], recv_sem=recv_sems.at[a, k - 1],
                    device_id=_peer(k), device_id_type=MESH).wait_recv()
        for cp in copies:
            cp.wait_send()
        acc, mine = gath[0], got[0]
        for d in range(1, N_DEV):
            acc = acc + gath[d]
            mine = mine + got[d]
        out_ref[...] = acc
        mine_ref[...] = mine

    return _pcall(
        body, name=name,
        in_specs=[pl.BlockSpec(memory_space=pltpu.VMEM)] * 2,
        out_specs=[pl.BlockSpec(memory_space=pltpu.VMEM)] * 2,
        out_shape=[jax.ShapeDtypeStruct((r, 128), F32), jax.ShapeDtypeStruct((s, 128), F32)],
        scratch_shapes=[pltpu.VMEM((N_DEV, r, 128), F32), pltpu.VMEM((N_DEV, s, 128), F32),
                        pltpu.SemaphoreType.DMA((2, N_DEV - 1)), pltpu.SemaphoreType.DMA((2, N_DEV - 1))],
    )(blk, sends)


def _heads(vec):
    return vec.reshape(NH, 1, DH)


def _rep(vec4):
    return jnp.broadcast_to(vec4.reshape(NH, 1, 1), (NH, 1, DH))


def _onehot_lane(offset):
    m = np.zeros((NH, 1, DH), np.float32)
    for h in range(NH):
        m[h, 0, offset + h] = 1.0
    return jnp.asarray(m)


_TINY = (("gdn_conv_w", (DEPTH, 4, 96)), ("rwkv_w_up", (DEPTH, 64, 32)), ("rwkv_a_up", (DEPTH, 64, 32)),
         ("sc_conv_w", (DEPTH, 3, 32)))
_TINY_ROWS = -(-sum(int(np.prod(s)) for _, s in _TINY) // 1024) * 8


def _pack_rows(arrays, rows, fill=0.0):
    flat = jnp.concatenate([a.reshape(-1) for a in arrays])
    return jnp.pad(flat, (0, rows * 128 - flat.shape[0]), constant_values=fill).reshape(rows, 128)


def _unpack_rows(p, named_shapes):
    lead = p.shape[:-2]
    flat = p.reshape(lead + (-1,))
    out, o = {}, 0
    for n, s in named_shapes:
        size = int(np.prod(s))
        out[n] = flat[..., o:o + size].reshape(lead + tuple(s))
        o += size
    return out


def _gather_last(a):
    return jnp.transpose(a, (1, 0, 2)).reshape(a.shape[1], -1)


def _split_last(a):
    r, c8 = a.shape
    return jnp.transpose(a.reshape(r, N_DEV, c8 // N_DEV), (1, 0, 2))


_SMALL = (("pre_norm_w", (DEPTH, 1024)), ("gdn_a_log", (DEPTH, 4)), ("gdn_dt_bias", (DEPTH, 4)),
          ("gdn_norm_w", (DEPTH, 64)), ("rwkv_mu", (DEPTH, 1152)), ("rwkv_w0", (DEPTH, 256)),
          ("rwkv_a0", (DEPTH, 256)), ("rwkv_k_k", (DEPTH, 256)), ("rwkv_k_a", (DEPTH, 256)),
          ("rwkv_r_k", (DEPTH, 256)), ("rwkv_ln_w", (DEPTH, 256)), ("rwkv_ln_b", (DEPTH, 256)),
          ("gla_a_up", (DEPTH, 16, 128)), ("gla_a_bias", (DEPTH, 128)), ("gla_norm_w", (DEPTH, 64)),
          ("post_norm_w", (DEPTH, 1024)), ("loss", ()))
_SMALL_ROWS = -(-sum(int(np.prod(s)) for _, s in _SMALL) // 1024) * 8


def _wout_operand(w_out_all):
    return w_out_all.reshape(4 * NH, DH, D_MODEL).astype(BF16)


def _big_weights(w_in_all, w_out_all, l):
    big = dict(w_g=_regroup_in(w_in_all, f"regroup_in{l}"))
    if w_out_all is not None:
        big["wout_g"] = _wout_operand(w_out_all)
    return big


def _layer_params(wts, tiny, l):
    conv = _gather_last(tiny["gdn_conv_w"][:, l])
    q = {}
    q["gdn_conv"] = jnp.transpose(conv.reshape(GDN_TAPS, 12, DH), (1, 0, 2))
    q["gdn_prm"] = [_rep(wts["gdn_a_log"][l]), _rep(wts["gdn_dt_bias"][l]),
                    jnp.broadcast_to(wts["gdn_norm_w"][l].reshape(1, 1, DH), (NH, 1, DH))]
    q["gdn_cst"] = [_onehot_lane(0), _onehot_lane(NH)]
    q["rwkv_mu"] = wts["rwkv_mu"][l].reshape(18, 1, DH)
    gs, nj = GROUPS_PER_STEP, N_GROUPS // GROUPS_PER_STEP
    side_by_side = lambda a: jnp.transpose(a.reshape(-1, gs, a.shape[1], DH), (0, 2, 1, 3)).reshape(-1, a.shape[1], gs * DH)
    q["conv_blocks"] = jnp.pad(side_by_side(q["gdn_conv"]), ((0, nj - len(CONV_BLOCKS)), (0, 0), (0, 0)))
    singles = jnp.pad(q["rwkv_mu"][16:18].reshape(1, 1, 2 * DH), ((0, 0), (0, 0), (DH, DH)))
    q["mix_blocks"] = jnp.concatenate([jnp.zeros((4, 1, gs * DH), F32), side_by_side(q["rwkv_mu"][:16]),
                                       jnp.zeros((8, 1, gs * DH), F32), singles], axis=0)
    w_up = jnp.transpose(_gather_last(tiny["rwkv_w_up"][:, l]).reshape(64, NH, DH), (1, 0, 2))
    a_up = jnp.transpose(_gather_last(tiny["rwkv_a_up"][:, l]).reshape(64, NH, DH), (1, 0, 2))
    q["rwkv_prm"] = [_heads(wts["rwkv_w0"][l]), w_up, _heads(wts["rwkv_a0"][l]), a_up,
                     _heads(wts["rwkv_k_k"][l]), _heads(wts["rwkv_k_a"][l]), _heads(wts["rwkv_r_k"][l]),
                     _heads(wts["rwkv_ln_w"][l]), _heads(wts["rwkv_ln_b"][l])]
    sc = _gather_last(tiny["sc_conv_w"][:, l])
    q["sc_conv"] = jnp.transpose(sc.reshape(SC_TAPS, NH, DH), (1, 0, 2))
    gla_up = jnp.transpose(wts["gla_a_up"][l].reshape(16, NH, GLA_HEAD_K), (1, 0, 2))
    gla_up = jnp.pad(gla_up, ((0, 0), (0, DH - 16), (0, DH - GLA_HEAD_K)))
    gla_b = jnp.pad(wts["gla_a_bias"][l].reshape(NH, 1, GLA_HEAD_K), ((0, 0), (0, 0), (0, DH - GLA_HEAD_K)))
    q["gla_prm"] = [gla_up, gla_b, jnp.broadcast_to(wts["gla_norm_w"][l].reshape(1, 1, DH), (NH, 1, DH))]
    q["pre_w"] = wts["pre_norm_w"][l].reshape(1, D_MODEL)
    q["post_w"] = wts["post_norm_w"][l].reshape(1, D_MODEL)
    return q


def _mixer_inputs(p, ps):
    gdn = [(ps, 4, 0), (ps, 4, 1), (ps, 4, 2), (p, 4, G_GDN // 4 + 3), (p, 1, G_GDN_AB)]
    rwkv = [(ps, 4, G_PM // 4 + k) for k in range(4)] + [(ps, 1, G_PM_WD), (ps, 1, G_PM_AD)]
    gla = [(p, 4, G_GLA // 4 + k) for k in range(4)] + [(p, 1, G_GLA_AD)]
    return gdn, rwkv, gla


def _layer_fwd(x, q, nb, t, l, side=None, tgt=None, w_out_shard=None):
    hb, p, ps, y_sc = _norm_proj(x, q["pre_w"], q["w_g"], q["conv_blocks"], q["mix_blocks"], q["sc_conv"], t,
                                 f"norm_proj{l}")
    gdn_in, rwkv_in, gla_in = _mixer_inputs(p, ps)
    y_gdn, ck_gdn, got = _mixer_fwd(_gdn_chunk, f"gdn_fwd{l}", gdn_in, q["gdn_prm"], q["gdn_cst"], nb, t, n_kept=1,
                                    side=None if w_out_shard is None else _gather_plan([w_out_shard]))
    wout_g = q["wout_g"] if w_out_shard is None else _wout_operand(got[0])
    y_rwkv, ck_rwkv, side_res = _mixer_fwd(_rwkv_chunk, f"rwkv_fwd{l}", rwkv_in, q["rwkv_prm"], [], nb, t,
                                           first_fn=_rwkv_first_chunk, side=side, n_kept=1)
    y_gla, ck_gla, _ = _mixer_fwd(_gla_chunk, f"gla_fwd{l}", gla_in, q["gla_prm"], [], nb, t)
    ys = (y_gdn, y_rwkv, y_sc, y_gla)
    out, *res = _out_proj_norm(ys, wout_g, x, q["post_w"], f"out_proj{l}", tgt)
    saved = dict(x=x, hb=hb, p=p, ps=ps, ys=ys, out=out, ck=(ck_gdn, ck_rwkv, ck_gla), wout_g=wout_g)
    return (res[0] if tgt is None else res), saved, side_res


def _layer_bwd(dxn, q, sv, nb, t, l, side=None, exchange_own=False):
    p, ys = sv["p"], sv["ys"]
    dout, dy, d_post = _post_bwd(dxn, sv["out"], q["post_w"], sv["wout_g"], f"post_bwd{l}")
    d_wout = _dwout(ys, dout, f"dwout{l}").reshape(N_DEV, 128, D_MODEL).astype(BF16)
    gdn_in, rwkv_in, gla_in = _mixer_inputs(p, sv["ps"])
    ck_gdn, ck_rwkv, ck_gla = sv["ck"]
    g = {}

    (d_conv, dz, dab), (da_log, ddt, dnw), _ = _mixer_bwd(
        _gdn_chunk, f"gdn_bwd{l}", gdn_in, q["gdn_prm"], q["gdn_cst"], ck_gdn, dy, 0,
        [(12, F32), (4, BF16), (1, BF16)], [(0, 0), (0, 4), (0, 8), (1, 0), (2, 0)], nb, t)
    dconv_in, d_gconv = _conv_bwd(p, G_GDN, 12, q["gdn_conv"], d_conv, nb, t, f"gdn_conv_bwd{l}")
    g["gdn_conv_w"] = jnp.transpose(d_gconv.sum(1), (1, 0, 2)).reshape(GDN_TAPS, 768)
    g["gdn_a_log"] = da_log.sum((0, 2, 3))
    g["gdn_dt_bias"] = ddt.sum((0, 2, 3))
    g["gdn_norm_w"] = dnw.sum((0, 1, 2))

    (d_pm,), d_rprm, side_res = _mixer_bwd(
        _rwkv_chunk, f"rwkv_bwd{l}", rwkv_in, q["rwkv_prm"], [], ck_rwkv, dy, 1,
        [(18, F32)], [(0, 0), (0, 4), (0, 8), (0, 12), (0, 16), (0, 17)], nb, t, first_fn=_rwkv_first_chunk,
        side=side)
    dp_rwkv, d_mu = _mix_bwd(p, q["rwkv_mu"], d_pm, nb, t, f"rwkv_mix_bwd{l}")
    g["rwkv_mu"] = d_mu.sum(1).reshape(1152)
    rp = [a.sum(0) for a in d_rprm]
    g["rwkv_w0"] = rp[0].reshape(256)
    g["rwkv_w_up"] = jnp.transpose(rp[1], (1, 0, 2)).reshape(64, 256)
    g["rwkv_a0"] = rp[2].reshape(256)
    g["rwkv_a_up"] = jnp.transpose(rp[3], (1, 0, 2)).reshape(64, 256)
    for i, nme in enumerate(("rwkv_k_k", "rwkv_k_a", "rwkv_r_k", "rwkv_ln_w", "rwkv_ln_b")):
        g[nme] = rp[4 + i].reshape(256)

    dp_sc, d_scw = _sc_bwd(p, q["sc_conv"], dy, nb, t, f"sc_bwd{l}")
    g["sc_conv_w"] = jnp.transpose(d_scw.sum(1), (1, 0, 2)).reshape(SC_TAPS, 256)

    (dp_gla, dad), (d_aup, d_ab, d_gnw), _ = _mixer_bwd(
        _gla_chunk, f"gla_bwd{l}", gla_in, q["gla_prm"], [], ck_gla, dy, 3,
        [(16, BF16), (1, BF16)], [(0, 0), (0, 4), (0, 8), (0, 12), (1, 0)], nb, t)
    g["gla_a_up"] = jnp.transpose(d_aup.sum(0)[:, :16, :GLA_HEAD_K], (1, 0, 2)).reshape(16, 128)
    g["gla_a_bias"] = d_ab.sum(0)[:, 0, :GLA_HEAD_K].reshape(128)
    g["gla_norm_w"] = d_gnw.sum((0, 1, 2))

    singles = jnp.concatenate([dab, dp_rwkv[16:18], dad], axis=0)
    sources = [dconv_in, dz, dp_rwkv, dp_sc, dp_gla, singles]
    d_win = _regroup_out(_dwin(sv["hb"], sources, f"dwin{l}"), f"regroup_out{l}")
    own = _exchange_plan([d_win, d_wout]) if exchange_own else None
    dx, d_pre, got = _dh_prenorm_bwd(sources, q["w_g"], sv["x"], q["pre_w"], dxn, f"dh_bwd{l}", own)
    if exchange_own:
        d_win, d_wout = got
    g["pre_norm_w"] = d_pre.reshape(D_MODEL)
    g["post_norm_w"] = d_post.reshape(D_MODEL)
    return dx, g, d_win, d_wout, side_res


def _local_step(x, tgt, wts, tiny, w_in_all, w_out_all, later_shards=None, first_w_out=None):
    nb, t, d = x.shape
    xf = x.reshape(nb * t, d)
    overlap = later_shards is not None
    qs, saved = [], []
    big = _big_weights(w_in_all[0], w_out_all[0] if first_w_out is None else None, 0)
    for l in range(DEPTH):
        q = dict(_layer_params(wts, tiny, l), **big)
        nxt = l + 1 < DEPTH
        side = _gather_plan(later_shards[l]) if overlap and nxt else None
        xf, sv, got = _layer_fwd(xf, q, nb, t, l, side, None if nxt else tgt.reshape(nb * t, d),
                                 w_out_shard=first_w_out if l == 0 else None)
        if nxt:
            big = _big_weights(*(got if overlap else (w_in_all[l + 1], w_out_all[l + 1])), l + 1)
        qs.append(q)
        saved.append(sv)
    dxf, lpart = xf
    grads, d_win, d_wout = [None] * DEPTH, [None] * DEPTH, [None] * DEPTH
    for l in reversed(range(DEPTH)):
        side = _exchange_plan([d_win[l + 1], d_wout[l + 1]]) if overlap and l + 1 < DEPTH else None
        dxf, grads[l], d_win[l], d_wout[l], got = _layer_bwd(dxf, qs[l], saved[l], nb, t, l, side,
                                                             exchange_own=overlap and l == 0)
        if side is not None:
            d_win[l + 1], d_wout[l + 1] = got
    small = {k: jnp.stack([grads[l][k] for l in range(DEPTH)]) for k in grads[0]}
    return lpart[0, 0], dxf.reshape(nb, t, d), small, d_win, d_wout


_WEIGHTS = ("pre_norm_w", "w_in", "gdn_conv_w", "gdn_a_log", "gdn_dt_bias", "gdn_norm_w", "rwkv_mu", "rwkv_w0",
            "rwkv_w_up", "rwkv_a0", "rwkv_a_up", "rwkv_k_k", "rwkv_k_a", "rwkv_r_k", "rwkv_ln_w", "rwkv_ln_b",
            "sc_conv_w", "gla_a_up", "gla_a_bias", "gla_norm_w", "w_out", "post_norm_w")


def kernel(x, pre_norm_w, w_in, gdn_conv_w, gdn_a_log, gdn_dt_bias, gdn_norm_w, rwkv_mu, rwkv_w0, rwkv_w_up, rwkv_a0, rwkv_a_up, rwkv_k_k, rwkv_k_a, rwkv_r_k, rwkv_ln_w, rwkv_ln_b, sc_conv_w, gla_a_up, gla_a_bias, gla_norm_w, w_out, post_norm_w, loss_target, m_pre_norm_w, m_w_in, m_gdn_conv_w, m_gdn_a_log, m_gdn_dt_bias, m_gdn_norm_w, m_rwkv_mu, m_rwkv_w0, m_rwkv_w_up, m_rwkv_a0, m_rwkv_a_up, m_rwkv_k_k, m_rwkv_k_a, m_rwkv_r_k, m_rwkv_ln_w, m_rwkv_ln_b, m_sc_conv_w, m_gla_a_up, m_gla_a_bias, m_gla_norm_w, m_w_out, m_post_norm_w, v_pre_norm_w, v_w_in, v_gdn_conv_w, v_gdn_a_log, v_gdn_dt_bias, v_gdn_norm_w, v_rwkv_mu, v_rwkv_w0, v_rwkv_w_up, v_rwkv_a0, v_rwkv_a_up, v_rwkv_k_k, v_rwkv_k_a, v_rwkv_r_k, v_rwkv_ln_w, v_rwkv_ln_b, v_sc_conv_w, v_gla_a_up, v_gla_a_bias, v_gla_norm_w, v_w_out, v_post_norm_w):
    env = dict(locals())
    w = {n: env[n] for n in _WEIGHTS}
    m = {n: env["m_" + n] for n in _WEIGHTS}
    v = {n: env["v_" + n] for n in _WEIGHTS}
    tiny_names = [n for n, _ in _TINY]

    w_in_b, w_out_b = w_in.astype(BF16), w_out.astype(BF16)
    w_in_0, tiny_all = _all_gather_two_level(
        [w_in_b[0], _pack_rows([w[n] for n in tiny_names], _TINY_ROWS)], "gather_weights")
    tiny = _unpack_rows(tiny_all, _TINY)

    lpart, grad_x, small, r_win, r_wout = _local_step(
        x, loss_target, w, tiny, [w_in_0], [None], later_shards=[(w_in_b[l], w_out_b[l]) for l in range(1, DEPTH)],
        first_w_out=w_out_b[0])

    grads, delta, new_m, new_v = {}, {}, {}, {}
    for n, parts in (("w_in", r_win), ("w_out", r_wout)):
        res = [_sum_adamw(parts[l], w[n][l], m[n][l], v[n][l], f"adamw_{n}{l}") for l in range(DEPTH)]
        grads[n], delta[n], new_m[n], new_v[n] = [jnp.stack(o) for o in zip(*res)]

    tiny_send = jnp.stack([_pack_rows([_split_last(small[n][l])[d] for n in tiny_names for l in range(DEPTH)],
                                      _TINY_ROWS) for d in range(N_DEV)])
    small = dict(small)
    small["loss"] = lpart
    red, tiny_sum = _reduce_small(_pack_rows([small[n] for n, _ in _SMALL], _SMALL_ROWS), tiny_send, "reduce_small")
    tiny_sum = tiny_sum.reshape(-1)
    o = 0
    for n, s in _TINY:
        size = int(np.prod(s))
        grads[n] = tiny_sum[o:o + size].reshape(s)
        o += size
    red = _unpack_rows(red, _SMALL)
    loss = red.pop("loss")
    grads.update(red)

    rest = [n for n in _WEIGHTS if n not in ("w_in", "w_out")]
    rest_shapes = [(n, w[n].shape) for n in rest]
    rows = -(-sum(int(np.prod(s)) for _, s in rest_shapes) // 1024) * 8
    outs = _adamw(_pack_rows([w[n] for n in rest], rows), _pack_rows([grads[n] for n in rest], rows),
                  _pack_rows([m[n] for n in rest], rows), _pack_rows([v[n] for n in rest], rows, 1.0), "adamw_rest")
    for dst, packed in zip((delta, new_m, new_v), outs):
        dst.update(_unpack_rows(packed, rest_shapes))

    return (loss, grad_x, *[grads[n] for n in _WEIGHTS], *[delta[n] for n in _WEIGHTS],
            *[new_m[n] for n in _WEIGHTS], *[new_v[n] for n in _WEIGHTS])
```
